```python
import math
import jax, jax.numpy as jnp
from jax import lax
import numpy as np

D_MODEL = 1024
BATCH = 4
SEQ = 4096
DEPTH = 4

HEAD_DIM = 64
H_A = 8
KV_RANK = 128
H_IDX = 8
D_IDX = 64
TOPK_MAX = 256
H_B = 8
Q_BLOCK = 128
N_BUCKETS = 32
T5_MAX_DIST = 128
G_C = 8
CHUNK = 128
C_WIDTH = G_C * HEAD_DIM
G_D = 8
D_WIDTH = G_D * HEAD_DIM
CONV_W = 3
D_FF = 2816
N_EVEN = (DEPTH + 1) // 2
N_ODD = DEPTH // 2
ALPHA = (2.0 * DEPTH) ** 0.25
BETA = (8.0 * DEPTH) ** -0.25
LN_EPS = 1e-5
EVEN_WIDTHS = (H_A * HEAD_DIM, KV_RANK, H_IDX * D_IDX, D_IDX, H_IDX,
               H_B * HEAD_DIM, H_B * HEAD_DIM, H_B * HEAD_DIM, H_B)
ODD_WIDTHS = (C_WIDTH, C_WIDTH, D_WIDTH, D_WIDTH, D_WIDTH)
EVEN_PROJ = sum(EVEN_WIDTHS)
ODD_PROJ = sum(ODD_WIDTHS)
EVEN_MIX = H_A * HEAD_DIM + H_B * HEAD_DIM
ODD_MIX = C_WIDTH + D_WIDTH

kernel_name = "hybrid_dsa_fox_gmlp_shortconv_trunk"


def _split(h, widths):
    out, off = [], 0
    for w in widths:
        out.append(h[..., off:off + w])
        off += w
    return out


def layer_norm(x, g, b=None):
    xf = x.astype(jnp.float32)
    mu = jnp.mean(xf, axis=-1, keepdims=True)
    xc = xf - mu
    var = jnp.mean(xc * xc, axis=-1, keepdims=True)
    y = xc * lax.rsqrt(var + LN_EPS) * g.astype(jnp.float32)
    if b is not None:
        y = y + b.astype(jnp.float32)
    return y.astype(x.dtype)


def causal_dwconv(z, w):
    width, ch = w.shape
    return lax.conv_general_dilated(
        z, w[:, None, :].astype(z.dtype), window_strides=(1,),
        padding=[(width - 1, 0)], dimension_numbers=('NWC', 'WIO', 'NWC'),
        feature_group_count=ch)


def t5_bucket(dist):
    max_exact = N_BUCKETS // 2
    n = jnp.maximum(dist, 0)
    nf = jnp.maximum(n, 1).astype(jnp.float32)
    large = max_exact + (jnp.log(nf / max_exact) / math.log(T5_MAX_DIST / max_exact)
                         * (N_BUCKETS - max_exact)).astype(jnp.int32)
    large = jnp.minimum(large, N_BUCKETS - 1)
    return jnp.where(n < max_exact, n, large)


def to_blocks(a):
    return jnp.moveaxis(a.reshape(a.shape[0], -1, Q_BLOCK, *a.shape[2:]), 1, 0)


def from_blocks(a):
    a = jnp.moveaxis(a, 0, 1)
    return a.reshape(a.shape[0], -1, *a.shape[3:])


def even_mixer(x, w_in, w_uk, w_uv, b_f, w_o, rel_bias):
    bsz, seq, _ = x.shape
    k_sel = min(TOPK_MAX, seq // 4)
    n_blk = seq // Q_BLOCK
    h = x @ w_in
    q_a, c_kv, q_idx, k_idx, w_idx, q_b, k_b, v_b, f_in = _split(h, EVEN_WIDTHS)
    q_a = q_a.reshape(bsz, seq, H_A, HEAD_DIM)
    q_lat = jnp.einsum('bshd,hrd->bshr', q_a, w_uk) * (HEAD_DIM ** -0.5)
    q_idx = q_idx.reshape(bsz, seq, H_IDX, D_IDX) * (D_IDX ** -0.5)
    w_idx = w_idx * (H_IDX ** -0.5)
    q_b = q_b.reshape(bsz, seq, H_B, HEAD_DIM) * (HEAD_DIM ** -0.5)
    k_b = k_b.reshape(bsz, seq, H_B, HEAD_DIM)
    v_b = v_b.reshape(bsz, seq, H_B, HEAD_DIM)
    log_f = jax.nn.log_sigmoid(f_in.astype(jnp.float32) + b_f.astype(jnp.float32))
    cum_f = jnp.cumsum(log_f, axis=1)
    cum_f_keys = jnp.swapaxes(cum_f, 1, 2)

    key_pos = jnp.arange(seq)
    bidx = jnp.arange(bsz)[:, None, None]

    def block(args):
        i, ql, qi, wi, qb, fq = args
        q_pos = i * Q_BLOCK + jnp.arange(Q_BLOCK)
        causal = key_pos[None, :] <= q_pos[:, None]
        idx = jnp.einsum('bthd,bsd->bths', qi, k_idx)
        score = jnp.einsum('bth,bths->bts', wi, jax.nn.relu(idx)).astype(jnp.float32)
        score = jnp.where(causal[None], score, -jnp.inf)
        _, sel = lax.top_k(score, k_sel)
        valid = sel <= q_pos[None, :, None]
        c_sel = c_kv[bidx, sel]
        la = jnp.einsum('bthr,btkr->bhtk', ql, c_sel).astype(jnp.float32)
        bias = rel_bias[t5_bucket(q_pos[None, :, None] - sel)]
        la = la + jnp.moveaxis(bias, -1, 1).astype(jnp.float32)
        la = jnp.where(valid[:, None], la, -jnp.inf)
        pa = jax.nn.softmax(la, axis=-1).astype(c_sel.dtype)
        oa = jnp.einsum('bhtk,btkr->bthr', pa, c_sel)
        lb = jnp.einsum('bthd,bshd->bhts', qb, k_b).astype(jnp.float32)
        lb = lb + jnp.swapaxes(fq, 1, 2)[..., None] - cum_f_keys[:, :, None, :]
        lb = jnp.where(causal[None, None], lb, -jnp.inf)
        pb = jax.nn.softmax(lb, axis=-1).astype(v_b.dtype)
        ob = jnp.einsum('bhts,bshd->bthd', pb, v_b)
        return oa, ob

    oa, ob = lax.map(block, (jnp.arange(n_blk), to_blocks(q_lat), to_blocks(q_idx),
                             to_blocks(w_idx), to_blocks(q_b), to_blocks(cum_f)))
    o_a = jnp.einsum('bshr,hrd->bshd', from_blocks(oa), w_uv).reshape(bsz, seq, H_A * HEAD_DIM)
    o_b = from_blocks(ob).reshape(bsz, seq, H_B * HEAD_DIM)
    return jnp.concatenate([o_a, o_b], axis=-1) @ w_o


def odd_mixer(x, w_in, sgu_g, sgu_w, sgu_b, conv_w, w_o):
    bsz, seq, _ = x.shape
    h = x @ w_in
    u, v, g_b, g_c, z = _split(h, ODD_WIDTHS)
    u = jax.nn.gelu(u)
    v = layer_norm(jax.nn.gelu(v).reshape(bsz, seq, G_C, HEAD_DIM), sgu_g)
    v = v.reshape(bsz, seq // CHUNK, CHUNK, G_C, HEAD_DIM)
    mix = jnp.einsum('gts,bnsgd->bntgd', jnp.tril(sgu_w), v) + jnp.swapaxes(sgu_b, 0, 1)[:, :, None]
    o_c = u * mix.reshape(bsz, seq, C_WIDTH)
    o_d = g_b * causal_dwconv(g_c * z, conv_w)
    return jnp.concatenate([o_c, o_d], axis=-1) @ w_o


def conv_ffn(x, w_up, conv_w, w_down):
    h = causal_dwconv(x @ w_up, conv_w)
    g, v = jnp.split(h, 2, axis=-1)
    return (jax.nn.silu(g) * v) @ w_down


def setup_inputs(seed: int = 0) -> dict:
    key = jax.random.key(seed)
    ks = jax.random.split(key, 18)

    def nrm(k, shape, scale):
        return jax.random.normal(k, shape, jnp.float32) * scale

    return {
        "x": nrm(ks[0], (BATCH, SEQ, D_MODEL), 1.0),
        "ln_g": 1.0 + nrm(ks[1], (DEPTH, 2, D_MODEL), 0.02),
        "ln_b": nrm(ks[2], (DEPTH, 2, D_MODEL), 0.02),
        "rel_bias": nrm(ks[3], (N_BUCKETS, H_A), 0.5),
        "ev_w_in": nrm(ks[4], (N_EVEN, D_MODEL, EVEN_PROJ), D_MODEL ** -0.5),
        "ev_w_uk": nrm(ks[5], (N_EVEN, H_A, KV_RANK, HEAD_DIM), KV_RANK ** -0.5),
        "ev_w_uv": nrm(ks[6], (N_EVEN, H_A, KV_RANK, HEAD_DIM), KV_RANK ** -0.5),
        "ev_b_f": jax.random.uniform(ks[7], (N_EVEN, H_B), jnp.float32, 1.0, 4.0),
        "ev_w_o": nrm(ks[8], (N_EVEN, EVEN_MIX, D_MODEL), BETA * EVEN_MIX ** -0.5),
        "od_w_in": nrm(ks[9], (N_ODD, D_MODEL, ODD_PROJ), D_MODEL ** -0.5),
        "od_sgu_g": 1.0 + nrm(ks[10], (N_ODD, G_C, HEAD_DIM), 0.02),
        "od_sgu_w": nrm(ks[11], (N_ODD, G_C, CHUNK, CHUNK), CHUNK ** -0.5),
        "od_sgu_b": 1.0 + nrm(ks[12], (N_ODD, G_C, CHUNK), 0.1),
        "od_conv_w": nrm(ks[13], (N_ODD, CONV_W, D_WIDTH), CONV_W ** -0.5),
        "od_w_o": nrm(ks[14], (N_ODD, ODD_MIX, D_MODEL), BETA * ODD_MIX ** -0.5),
        "ffn_w_up": nrm(ks[15], (DEPTH, D_MODEL, 2 * D_FF), D_MODEL ** -0.5),
        "ffn_conv_w": nrm(ks[16], (DEPTH, CONV_W, 2 * D_FF), CONV_W ** -0.5),
        "ffn_w_down": nrm(ks[17], (DEPTH, D_FF, D_MODEL), BETA * D_FF ** -0.5),
    }


def reference(x, ln_g, ln_b, rel_bias, ev_w_in, ev_w_uk, ev_w_uv, ev_b_f, ev_w_o,
              od_w_in, od_sgu_g, od_sgu_w, od_sgu_b, od_conv_w, od_w_o,
              ffn_w_up, ffn_conv_w, ffn_w_down):
    for layer in range(DEPTH):
        j = layer // 2
        if layer % 2 == 0:
            m = even_mixer(x, ev_w_in[j], ev_w_uk[j], ev_w_uv[j], ev_b_f[j], ev_w_o[j], rel_bias)
        else:
            m = odd_mixer(x, od_w_in[j], od_sgu_g[j], od_sgu_w[j], od_sgu_b[j], od_conv_w[j], od_w_o[j])
        x = layer_norm(ALPHA * x + m, ln_g[layer, 0], ln_b[layer, 0])
        f = conv_ffn(x, ffn_w_up[layer], ffn_conv_w[layer], ffn_w_down[layer])
        x = layer_norm(ALPHA * x + f, ln_g[layer, 1], ln_b[layer, 1])
    return x
```

```python
import functools
import math

import numpy as np
import jax
import jax.numpy as jnp
from jax import lax
from jax.experimental import pallas as pl
from jax.experimental.pallas import tpu as pltpu

F32 = jnp.float32
BF16 = jnp.bfloat16

HEAD_DIM = 64
N_HEADS = 8
KV_RANK = 128
TOPK_MAX = 256
N_BUCKETS = 32
T5_MAX_DIST = 128
CHUNK = 128
CONV_W = 3
LN_EPS = 1e-5
MIX_HALF = N_HEADS * HEAD_DIM

LANES = 128
SUBLANES = 8
VMEM_LIMIT = 56 * 1024 * 1024

BLK = 128
NEG_BIG = -1e30
F32_MAX = float(np.finfo(np.float32).max)


def _cparams(sem):
    return pltpu.CompilerParams(dimension_semantics=sem, vmem_limit_bytes=VMEM_LIMIT)


def _dot(a, b):
    return jnp.dot(a, b, preferred_element_type=F32)


def _dot_nt(a, b):
    return lax.dot_general(a, b, (((1,), (1,)), ((), ())), preferred_element_type=F32)


def _layer_norm_rows(y, g, b):
    mu = jnp.mean(y, axis=-1, keepdims=True)
    yc = y - mu
    var = jnp.mean(yc * yc, axis=-1, keepdims=True)
    return yc * lax.rsqrt(var + LN_EPS) * g + b


def _even_proj_kernel(x_ref, wb_ref, ws_ref, hb_ref, hs_ref):
    x = x_ref[...]
    hb_ref[...] = _dot(x, wb_ref[...]).astype(hb_ref.dtype)
    hs_ref[...] = _dot(x, ws_ref[...])


def _even_proj(xb, w_big, w_small, tm):
    n, k = xb.shape
    mb = w_big.shape[1]
    ms = w_small.shape[1]
    return pl.pallas_call(
        _even_proj_kernel,
        grid=(n // tm,),
        in_specs=[pl.BlockSpec((tm, k), lambda i: (i, 0)),
                  pl.BlockSpec((k, mb), lambda i: (0, 0)),
                  pl.BlockSpec((k, ms), lambda i: (0, 0))],
        out_specs=[pl.BlockSpec((tm, mb), lambda i: (i, 0)),
                   pl.BlockSpec((tm, ms), lambda i: (i, 0))],
        out_shape=[jax.ShapeDtypeStruct((n, mb), BF16),
                   jax.ShapeDtypeStruct((n, ms), F32)],
        compiler_params=_cparams(("parallel",)),
        name="even_proj",
    )(xb, w_big, w_small)


def _cumf_kernel(f_ref, b_ref, o_ref):
    seq = f_ref.shape[-1]
    row = lax.broadcasted_iota(jnp.int32, (LANES, LANES), 0)
    col = lax.broadcasted_iota(jnp.int32, (LANES, LANES), 1)
    upper = jnp.where(row <= col, 1.0, 0.0).astype(F32)
    carry = jnp.zeros((f_ref.shape[0], 1), F32)
    for c in range(seq // LANES):
        z = f_ref[:, c * LANES:(c + 1) * LANES] + b_ref[...]
        lf = jnp.minimum(z, 0.0) - jnp.log1p(jnp.exp(-jnp.abs(z)))
        cs = jnp.dot(lf, upper, preferred_element_type=F32,
                     precision=lax.Precision.HIGHEST) + carry
        o_ref[:, c * LANES:(c + 1) * LANES] = cs
        carry = cs[:, LANES - 1:LANES]


def _cumf(f_t, b_f):
    bsz, nh, seq = f_t.shape
    return pl.pallas_call(
        _cumf_kernel,
        grid=(bsz,),
        in_specs=[pl.BlockSpec((None, nh, seq), lambda b: (b, 0, 0)),
                  pl.BlockSpec((nh, 1), lambda b: (0, 0))],
        out_specs=pl.BlockSpec((None, nh, seq), lambda b: (b, 0, 0)),
        out_shape=jax.ShapeDtypeStruct((bsz, nh, seq), F32),
        compiler_params=_cparams(("parallel",)),
        name="fox_cumf",
    )(f_t, b_f)


def _attn_kernel(qa_ref, qi_ref, qb_ref, hs_ref, fq_ref, kb_ref, vb_ref, ckv_ref,
                 klo_ref, khi_ref, fk_ref, wuk_ref, wuv_ref, bnear_ref, bfar_ref,
                 o_ref,
                 score_ref, thr_ref, qlat_ref, qis_ref, wb_ref, fqb_ref,
                 m_ref, l_ref, acc_ref, *, k_sel):
    T = BLK
    H = N_HEADS
    i = pl.program_id(1)
    nb = i + 1
    lane = lax.broadcasted_iota(jnp.int32, (T, LANES), 1)
    row_i = lax.broadcasted_iota(jnp.int32, (T, T), 0)
    col_i = lax.broadcasted_iota(jnp.int32, (T, T), 1)
    causal = col_i <= row_i
    lo_half = lane < HEAD_DIM

    w_idx = hs_ref[:, 0:H] * (H ** -0.5)
    fq = fq_ref[...]
    for h in range(H):
        wb_ref[h] = jnp.broadcast_to(w_idx[:, h:h + 1], (T, LANES))
        fqb_ref[h] = jnp.broadcast_to(fq[:, h:h + 1], (T, LANES))
    for p in range(H // 2):
        qp = qa_ref[:, p * LANES:(p + 1) * LANES]
        qlat_ref[2 * p] = _dot(qp, wuk_ref[2 * p]).astype(BF16)
        qlat_ref[2 * p + 1] = _dot(qp, wuk_ref[2 * p + 1]).astype(BF16)
        qis_ref[p] = qi_ref[:, p * LANES:(p + 1) * LANES]

    def idx_scores(j):
        start = pl.multiple_of(j * T, T)
        qis = qis_ref[...].reshape(4 * T, LANES)
        s_even = _dot_nt(qis, klo_ref[pl.ds(start, T), :])
        s_odd = _dot_nt(qis, khi_ref[pl.ds(start, T), :])
        acc = jnp.zeros((T, T), F32)
        for p in range(H // 2):
            acc = acc + wb_ref[2 * p] * jnp.maximum(s_even[p * T:(p + 1) * T], 0.0)
            acc = acc + wb_ref[2 * p + 1] * jnp.maximum(s_odd[p * T:(p + 1) * T], 0.0)
        return acc

    def idx_body(j, c):
        score_ref[j] = idx_scores(j)
        return c

    lax.fori_loop(0, i, idx_body, 0)
    score_ref[i] = jnp.where(causal, idx_scores(i), -jnp.inf)

    thr_ref[...] = jnp.full((T, LANES), -F32_MAX, F32)

    @pl.when((i + 1) * T > k_sel)
    def _select():
        kf = float(k_sel)
        ones = jnp.ones((LANES, LANES), BF16)
        t_pos = (i * T + lax.broadcasted_iota(jnp.int32, (T, LANES), 0)).astype(F32)
        searching = (t_pos + 1.0) > kf

        def row_count(pred_fn):
            def body(j, c):
                return c + jnp.where(pred_fn(score_ref[j], j), 1.0, 0.0)
            c = lax.fori_loop(0, nb, body, jnp.zeros((T, LANES), F32))
            return _dot(c.astype(BF16), ones)

        def count_ge(thr):
            return row_count(lambda s, j: s >= thr)

        def mm_body(j, carry):
            mn, mx = carry
            s = score_ref[j]
            return (jnp.minimum(mn, jnp.where(s == -jnp.inf, jnp.inf, s)), jnp.maximum(mx, s))

        mn, mx = lax.fori_loop(0, nb, mm_body,
                               (jnp.full((T, LANES), jnp.inf, F32),
                                jnp.full((T, LANES), -jnp.inf, F32)))
        mn = jnp.broadcast_to(jnp.min(mn, axis=1, keepdims=True), (T, LANES))
        mx = jnp.broadcast_to(jnp.max(mx, axis=1, keepdims=True), (T, LANES))
        c_mx = count_ge(mx)
        top_tied = c_mx >= kf
        lo0 = jnp.where(top_tied, mx, mn)
        cnt_lo0 = jnp.where(top_tied, c_mx, t_pos + 1.0)
        cnt_hi0 = jnp.where(top_tied, 0.0, c_mx)
        active0 = jnp.where(searching & jnp.logical_not(top_tied) & (cnt_lo0 > kf), 1.0, 0.0)

        def cond(st):
            return jnp.logical_and(st[5] > 0.0, st[6] < 2200)

        def body(st):
            lo, hi, cnt_lo, cnt_hi, active, _, it = st
            mid = 0.5 * lo + 0.5 * hi
            adjacent = (mid <= lo) | (mid >= hi)
            c = count_ge(mid)
            act = active > 0.0
            go_lo = act & jnp.logical_not(adjacent) & (c >= kf)
            go_hi = act & jnp.logical_not(adjacent) & (c < kf)
            lo = jnp.where(go_lo, mid, lo)
            cnt_lo = jnp.where(go_lo, c, cnt_lo)
            hi = jnp.where(go_hi, mid, hi)
            cnt_hi = jnp.where(go_hi, c, cnt_hi)
            active = jnp.where(act & jnp.logical_not(adjacent) & (cnt_lo > kf), 1.0, 0.0)
            return lo, hi, cnt_lo, cnt_hi, active, jnp.max(active), it + 1

        lo, hi, cnt_lo, cnt_hi, _, _, _ = lax.while_loop(
            cond, body, (lo0, mx, cnt_lo0, cnt_hi0, active0, jnp.max(active0), 0))
        thr = jnp.where(searching, lo, -F32_MAX)
        thr_ref[...] = thr

        tied = searching & (cnt_lo > kf)
        need = kf - cnt_hi

        @pl.when(jnp.max(jnp.where(tied, 1.0, 0.0)) > 0.0)
        def _break_ties():
            def count_tied_upto(jcut):
                def pred(s, j):
                    kidx = (j * T + col_i).astype(F32)
                    return (s == thr) & (kidx <= jcut)
                return row_count(pred)

            def bs_body(_, st):
                jl, jh = st
                jm = jnp.floor(0.5 * (jl + jh))
                ok = count_tied_upto(jm) >= need
                return jnp.where(ok, jl, jm), jnp.where(ok, jm, jh)

            n_keys = (nb * T).astype(F32)
            jl0 = jnp.full((T, LANES), -1.0, F32)
            jh0 = jnp.zeros((T, LANES), F32) + (n_keys - 1.0)
            n_steps = int(math.ceil(math.log2(score_ref.shape[0] * T))) + 1
            _, jcut = lax.fori_loop(0, n_steps, bs_body, (jl0, jh0))

            def drop_body(j, c):
                s = score_ref[j]
                kidx = (j * T + col_i).astype(F32)
                score_ref[j] = jnp.where(tied & (s == thr) & (kidx > jcut), -jnp.inf, s)
                return c
            lax.fori_loop(0, nb, drop_body, 0)

    m_ref[...] = jnp.full(m_ref.shape, NEG_BIG, F32)
    l_ref[...] = jnp.zeros(l_ref.shape, F32)
    acc_ref[...] = jnp.zeros(acc_ref.shape, F32)

    def online_update(r0, s, v_blk):
        n = s.shape[0]
        m_old = m_ref[r0:r0 + n, :]
        m_new = jnp.maximum(m_old, jnp.max(s, axis=1, keepdims=True))
        alpha = jnp.exp(m_old - m_new)
        p = jnp.exp(s - m_new)
        l_ref[r0:r0 + n, :] = alpha * l_ref[r0:r0 + n, :] + jnp.sum(p, axis=1, keepdims=True)
        acc_ref[r0:r0 + n, :] = alpha * acc_ref[r0:r0 + n, :] + _dot(p.astype(BF16), v_blk)
        m_ref[r0:r0 + n, :] = m_new

    def dsa_block(j, bias):
        start = pl.multiple_of(j * T, T)
        c_blk = ckv_ref[pl.ds(start, T), :]
        sel = score_ref[j] >= thr_ref[...]
        for h in range(H):
            s = _dot_nt(qlat_ref[h], c_blk) + bias(h)
            online_update(h * T, jnp.where(sel, s, NEG_BIG), c_blk)

    def dsa_far(j, c):
        dsa_block(j, lambda h: bfar_ref[h:h + 1, :])
        return c

    lax.fori_loop(0, i - 1, dsa_far, 0)

    @pl.when(i >= 1)
    def _prev():
        dsa_block(i - 1, lambda h: bnear_ref[h, :, 0:T])

    dsa_block(i, lambda h: bnear_ref[h, :, T:2 * T])

    for p in range(H // 2):
        oa0 = acc_ref[(2 * p) * T:(2 * p + 1) * T, :] / l_ref[(2 * p) * T:(2 * p + 1) * T, :]
        oa1 = acc_ref[(2 * p + 1) * T:(2 * p + 2) * T, :] / l_ref[(2 * p + 1) * T:(2 * p + 2) * T, :]
        o_pair = _dot(oa0.astype(BF16), wuv_ref[2 * p]) + _dot(oa1.astype(BF16), wuv_ref[2 * p + 1])
        o_ref[:, p * LANES:(p + 1) * LANES] = o_pair.astype(o_ref.dtype)

    m_ref[...] = jnp.full(m_ref.shape, NEG_BIG, F32)
    l_ref[...] = jnp.zeros(l_ref.shape, F32)
    acc_ref[...] = jnp.zeros(acc_ref.shape, F32)

    def fox_block(j, diag):
        start = pl.multiple_of(j * T, T)
        fk = fk_ref[j]
        for p in range(H // 2):
            q_pair = qb_ref[:, p * LANES:(p + 1) * LANES]
            k_pair = kb_ref[pl.ds(start, T), p * LANES:(p + 1) * LANES]
            v_pair = vb_ref[pl.ds(start, T), p * LANES:(p + 1) * LANES]
            for e in range(2):
                h = 2 * p + e
                q_h = jnp.where(lo_half if e == 0 else jnp.logical_not(lo_half), q_pair, 0)
                s = _dot_nt(q_h, k_pair) + (fqb_ref[h] - fk[h:h + 1, :])
                if diag:
                    s = jnp.where(causal, s, NEG_BIG)
                online_update(h * T, s, v_pair)

    def fox_body(j, c):
        fox_block(j, False)
        return c

    lax.fori_loop(0, i, fox_body, 0)
    fox_block(i, True)

    for p in range(H // 2):
        ob0 = acc_ref[(2 * p) * T:(2 * p + 1) * T, :] / l_ref[(2 * p) * T:(2 * p + 1) * T, :]
        ob1 = acc_ref[(2 * p + 1) * T:(2 * p + 2) * T, :] / l_ref[(2 * p + 1) * T:(2 * p + 2) * T, :]
        o_ref[:, MIX_HALF + p * LANES:MIX_HALF + (p + 1) * LANES] = (
            jnp.where(lo_half, ob0, ob1).astype(o_ref.dtype))


_EV_QA, _EV_QI, _EV_QB, _EV_KB, _EV_VB = 0, 1, 2, 3, 4
_EV_CKV, _EV_KLO, _EV_KHI = 20, 21, 22
_EV_WIDTH = 23 * LANES


def _attention(hb, hs, fq, fk_blocks, wuk_pad, wuv_pad, bias_near, bias_far, k_sel):
    bsz, seq, _ = hb.shape
    T = BLK
    nblk = seq // T
    H = N_HEADS
    wide = lambda c: pl.BlockSpec((None, T, MIX_HALF), lambda b, i, c=c: (b, i, c))
    full_wide = lambda c: pl.BlockSpec((None, seq, MIX_HALF), lambda b, i, c=c: (b, 0, c))
    full_narrow = lambda c: pl.BlockSpec((None, seq, LANES), lambda b, i, c=c: (b, 0, c))
    const3 = lambda shape: pl.BlockSpec(shape, lambda b, i: (0, 0, 0))
    return pl.pallas_call(
        functools.partial(_attn_kernel, k_sel=k_sel),
        grid=(bsz, nblk),
        in_specs=[wide(_EV_QA), wide(_EV_QI), wide(_EV_QB),
                  pl.BlockSpec((None, T, LANES), lambda b, i: (b, i, 0)),
                  pl.BlockSpec((None, T, H), lambda b, i: (b, i, 0)),
                  full_wide(_EV_KB), full_wide(_EV_VB),
                  full_narrow(_EV_CKV), full_narrow(_EV_KLO), full_narrow(_EV_KHI),
                  pl.BlockSpec((None, nblk, H, T), lambda b, i: (b, 0, 0, 0)),
                  const3((H, LANES, KV_RANK)), const3((H, KV_RANK, LANES)),
                  const3((H, T, 2 * T)),
                  pl.BlockSpec((H, LANES), lambda b, i: (0, 0))],
        out_specs=pl.BlockSpec((None, T, 2 * MIX_HALF), lambda b, i: (b, i, 0)),
        out_shape=jax.ShapeDtypeStruct((bsz, seq, 2 * MIX_HALF), BF16),
        scratch_shapes=[pltpu.VMEM((nblk, T, T), F32),
                        pltpu.VMEM((T, LANES), F32),
                        pltpu.VMEM((H, T, KV_RANK), BF16),
                        pltpu.VMEM((H // 2, T, LANES), BF16),
                        pltpu.VMEM((H, T, LANES), F32),
                        pltpu.VMEM((H, T, LANES), F32),
                        pltpu.VMEM((H * T, 1), F32),
                        pltpu.VMEM((H * T, 1), F32),
                        pltpu.VMEM((H * T, LANES), F32)],
        compiler_params=_cparams(("parallel", "arbitrary")),
        name="dsa_fox_attention",
    )(hb, hb, hb, hs, fq, hb, hb, hb, hb, hb, fk_blocks, wuk_pad, wuv_pad, bias_near, bias_far)


def _proj_ln_kernel(a_ref, w_ref, x_ref, g_ref, b_ref, xo_ref, xb_ref, *, alpha):
    y = alpha * x_ref[...] + _dot(a_ref[...], w_ref[...])
    out = _layer_norm_rows(y, g_ref[...], b_ref[...])
    xo_ref[...] = out
    xb_ref[...] = out.astype(BF16)


def _proj_ln(a, w, x, g, b, alpha, tm):
    n, k = a.shape
    d = w.shape[1]
    return pl.pallas_call(
        functools.partial(_proj_ln_kernel, alpha=alpha),
        grid=(n // tm,),
        in_specs=[pl.BlockSpec((tm, k), lambda i: (i, 0)),
                  pl.BlockSpec((k, d), lambda i: (0, 0)),
                  pl.BlockSpec((tm, d), lambda i: (i, 0)),
                  pl.BlockSpec((1, d), lambda i: (0, 0)),
                  pl.BlockSpec((1, d), lambda i: (0, 0))],
        out_specs=[pl.BlockSpec((tm, d), lambda i: (i, 0)),
                   pl.BlockSpec((tm, d), lambda i: (i, 0))],
        out_shape=[jax.ShapeDtypeStruct((n, d), F32),
                   jax.ShapeDtypeStruct((n, d), BF16)],
        compiler_params=_cparams(("parallel",)),
        name="proj_residual_ln",
    )(a, w, x, g, b)


def _causal_conv3(h, tail_ref, cw, first_tile):
    tm = h.shape[0]
    w0, w1, w2 = cw[0:1, :], cw[1:2, :], cw[2:3, :]
    out = w0 * pltpu.roll(h, 2, axis=0) + w1 * pltpu.roll(h, 1, axis=0) + w2 * h
    tail = jnp.where(first_tile, 0.0, tail_ref[...])
    head = h[0:SUBLANES, :]
    r = lax.broadcasted_iota(jnp.int32, head.shape, 0)
    hm1 = jnp.where(r < 1, pltpu.roll(tail, 1, axis=0), pltpu.roll(head, 1, axis=0))
    hm2 = jnp.where(r < 2, pltpu.roll(tail, 2, axis=0), pltpu.roll(head, 2, axis=0))
    out_head = w0 * hm2 + w1 * hm1 + w2 * head
    tail_ref[...] = h[tm - SUBLANES:tm, :]
    return out, out_head


def _ffn_up_kernel(x_ref, wg_ref, wv_ref, cg_ref, cv_ref, o_ref, tg_ref, tv_ref, *, tiles_per_seq):
    first = (pl.program_id(1) % tiles_per_seq) == 0
    x = x_ref[...]
    g, g_head = _causal_conv3(_dot(x, wg_ref[...]), tg_ref, cg_ref[...], first)
    v, v_head = _causal_conv3(_dot(x, wv_ref[...]), tv_ref, cv_ref[...], first)

    def act(a, b):
        return (a / (1.0 + jnp.exp(-a)) * b).astype(o_ref.dtype)

    o_ref[...] = act(g, v)
    o_ref[0:SUBLANES, :] = act(g_head, v_head)


def _ffn_up(xb, w_up, conv_w, seq, tm, tn):
    n, k = xb.shape
    dff = w_up.shape[1] // 2
    ncol = dff // tn
    return pl.pallas_call(
        functools.partial(_ffn_up_kernel, tiles_per_seq=seq // tm),
        grid=(ncol, n // tm),
        in_specs=[pl.BlockSpec((tm, k), lambda j, i: (i, 0)),
                  pl.BlockSpec((k, tn), lambda j, i: (0, j)),
                  pl.BlockSpec((k, tn), lambda j, i, ncol=ncol: (0, j + ncol)),
                  pl.BlockSpec((CONV_W, tn), lambda j, i: (0, j)),
                  pl.BlockSpec((CONV_W, tn), lambda j, i, ncol=ncol: (0, j + ncol))],
        out_specs=pl.BlockSpec((tm, tn), lambda j, i: (i, j)),
        out_shape=jax.ShapeDtypeStruct((n, dff), BF16),
        scratch_shapes=[pltpu.VMEM((SUBLANES, tn), F32), pltpu.VMEM((SUBLANES, tn), F32)],
        compiler_params=_cparams(("arbitrary", "arbitrary")),
        name="ffn_up_conv_gate",
    )(xb, w_up, w_up, conv_w, conv_w)


def _gelu_tanh(x):
    return 0.5 * x * (1.0 + jnp.tanh(math.sqrt(2.0 / math.pi) * (x + 0.044715 * (x * x * x))))


def _group_mean(v, avg):
    hi = v.astype(BF16)
    lo = (v - hi.astype(F32)).astype(BF16)
    return _dot(hi, avg) + _dot(lo, avg)


def _odd_mixer_kernel(x_ref, w_ref, sg_ref, sw_ref, sb_ref, cw_ref, o_ref, tail_ref, *, tiles_per_seq):
    tm = x_ref.shape[0]
    W = MIX_HALF
    first = (pl.program_id(0) % tiles_per_seq) == 0
    x = x_ref[...]
    lane = lax.broadcasted_iota(jnp.int32, (CHUNK, LANES), 1)
    lo_half = lane < HEAD_DIM
    r = lax.broadcasted_iota(jnp.int32, (W, W), 0) // HEAD_DIM
    c = lax.broadcasted_iota(jnp.int32, (W, W), 1) // HEAD_DIM
    avg = jnp.where(r == c, 1.0 / HEAD_DIM, 0.0).astype(BF16)
    tri = (lax.broadcasted_iota(jnp.int32, (CHUNK, CHUNK), 1)
           <= lax.broadcasted_iota(jnp.int32, (CHUNK, CHUNK), 0))

    u = _gelu_tanh(_dot(x, w_ref[:, 0:W]))
    v = _gelu_tanh(_dot(x, w_ref[:, W:2 * W]))
    vc = v - _group_mean(v, avg)
    var = _group_mean(vc * vc, avg)
    vn = (vc * lax.rsqrt(var + LN_EPS) * sg_ref[...]).astype(BF16)
    for n in range(tm // CHUNK):
        rows = slice(n * CHUNK, (n + 1) * CHUNK)
        for p in range(N_HEADS // 2):
            cols = slice(p * LANES, (p + 1) * LANES)
            vp = vn[rows, cols]
            w_e = jnp.where(tri, sw_ref[2 * p], 0.0).astype(BF16)
            w_o = jnp.where(tri, sw_ref[2 * p + 1], 0.0).astype(BF16)
            mix = (_dot(w_e, jnp.where(lo_half, vp, 0)) + _dot(w_o, jnp.where(lo_half, 0, vp))
                   + sb_ref[:, cols])
            o_ref[rows, cols] = (u[rows, cols] * mix).astype(o_ref.dtype)

    g_b = _dot(x, w_ref[:, 2 * W:3 * W])
    y = _dot(x, w_ref[:, 3 * W:4 * W]) * _dot(x, w_ref[:, 4 * W:5 * W])
    conv, conv_head = _causal_conv3(y, tail_ref, cw_ref[...], first)
    o_ref[:, W:2 * W] = (g_b * conv).astype(o_ref.dtype)
    o_ref[0:SUBLANES, W:2 * W] = (g_b[0:SUBLANES, :] * conv_head).astype(o_ref.dtype)


def _odd_mixer(xb, w_in, sgu_g, sgu_w, sgu_b_tile, conv_w, seq, tm):
    n, k = xb.shape
    W = MIX_HALF
    return pl.pallas_call(
        functools.partial(_odd_mixer_kernel, tiles_per_seq=seq // tm),
        grid=(n // tm,),
        in_specs=[pl.BlockSpec((tm, k), lambda i: (i, 0)),
                  pl.BlockSpec((k, 5 * W), lambda i: (0, 0)),
                  pl.BlockSpec((1, W), lambda i: (0, 0)),
                  pl.BlockSpec((N_HEADS, CHUNK, CHUNK), lambda i: (0, 0, 0)),
                  pl.BlockSpec((CHUNK, W), lambda i: (0, 0)),
                  pl.BlockSpec((CONV_W, W), lambda i: (0, 0))],
        out_specs=pl.BlockSpec((tm, 2 * W), lambda i: (i, 0)),
        out_shape=jax.ShapeDtypeStruct((n, 2 * W), BF16),
        scratch_shapes=[pltpu.VMEM((SUBLANES, W), F32)],
        compiler_params=_cparams(("arbitrary",)),
        name="odd_mixer",
    )(xb, w_in, sgu_g, sgu_w, sgu_b_tile, conv_w)


def _t5_bucket_np(dist):
    max_exact = N_BUCKETS // 2
    n = np.maximum(dist, 0)
    nf = np.maximum(n, 1).astype(np.float32)
    large = max_exact + (np.log(nf / max_exact) / np.float32(math.log(T5_MAX_DIST / max_exact))
                         * (N_BUCKETS - max_exact)).astype(np.int32)
    large = np.minimum(large, N_BUCKETS - 1)
    return np.where(n < max_exact, n, large)


def _bias_tables(rel_bias):
    T = BLK
    dist = np.arange(T)[:, None] + T - np.arange(2 * T)[None, :]
    bucket = _t5_bucket_np(dist)
    far_bucket = _t5_bucket_np(np.array([T + 1]))[0]
    assert far_bucket == N_BUCKETS - 1 and bucket[T - 1, 0] < T5_MAX_DIST
    assert (_t5_bucket_np(np.arange(T + 1, 64 * T)) == far_bucket).all()
    near = jnp.moveaxis(rel_bias[bucket], -1, 0)
    far = jnp.broadcast_to(rel_bias[far_bucket][:, None], (N_HEADS, LANES))
    return near.astype(F32), far.astype(F32)


def _even_weights(w_in, w_uk, w_uv):
    hd, H = HEAD_DIM, N_HEADS
    widths = (H * hd, KV_RANK, H * hd, hd, H, H * hd, H * hd, H * hd, H)
    offs = np.cumsum((0,) + widths)
    seg = lambda s: w_in[:, offs[s]:offs[s + 1]]
    q_a, c_kv, q_idx, k_idx, w_idx, q_b, k_b, v_b, f_in = [seg(s) for s in range(9)]
    zk = jnp.zeros_like(k_idx)
    scale = hd ** -0.5
    w_big = jnp.concatenate([q_a, q_idx * scale, q_b * scale, k_b, v_b, c_kv,
                             k_idx, zk, zk, k_idx], axis=1).astype(BF16)
    pad = jnp.zeros((w_in.shape[0], LANES - 2 * H), w_in.dtype)
    w_small = jnp.concatenate([w_idx, f_in, pad], axis=1).astype(BF16)
    wuk_t = jnp.swapaxes(w_uk, 1, 2) * scale
    wuk_pad = jnp.zeros((H, LANES, KV_RANK), F32)
    wuv_pad = jnp.zeros((H, KV_RANK, LANES), F32)
    for h in range(H):
        o = hd * (h % 2)
        wuk_pad = wuk_pad.at[h, o:o + hd, :].set(wuk_t[h])
        wuv_pad = wuv_pad.at[h, :, o:o + hd].set(w_uv[h])
    return w_big, w_small, wuk_pad.astype(BF16), wuv_pad.astype(BF16)


def kernel(x, ln_g, ln_b, rel_bias, ev_w_in, ev_w_uk, ev_w_uv, ev_b_f, ev_w_o,
           od_w_in, od_sgu_g, od_sgu_w, od_sgu_b, od_conv_w, od_w_o,
           ffn_w_up, ffn_conv_w, ffn_w_down):
    bsz, seq, d = x.shape
    depth = ln_g.shape[0]
    n = bsz * seq
    alpha = (2.0 * depth) ** 0.25
    k_sel = min(TOPK_MAX, seq // 4)
    H = N_HEADS
    assert seq % BLK == 0 and seq % CHUNK == 0 and k_sel % 1 == 0
    tm = min(512, seq)
    tm_ffn = min(1024, seq)

    bias_near, bias_far = _bias_tables(rel_bias)
    xf = x.reshape(n, d)
    xb = xf.astype(BF16)
    for layer in range(depth):
        j = layer // 2
        g0, b0 = ln_g[layer, 0][None, :], ln_b[layer, 0][None, :]
        g1, b1 = ln_g[layer, 1][None, :], ln_b[layer, 1][None, :]
        if layer % 2 == 0:
            w_big, w_small, wuk_pad, wuv_pad = _even_weights(ev_w_in[j], ev_w_uk[j], ev_w_uv[j])
            hb, hs = _even_proj(xb, w_big, w_small, tm)
            hb = hb.reshape(bsz, seq, -1)
            hs = hs.reshape(bsz, seq, -1)
            f_t = jnp.swapaxes(hs[:, :, H:2 * H], 1, 2)
            cum_f = _cumf(f_t, ev_b_f[j][:, None])
            fq = jnp.swapaxes(cum_f, 1, 2)
            fk_blocks = jnp.swapaxes(cum_f.reshape(bsz, H, seq // BLK, BLK), 1, 2)
            mix = _attention(hb, hs, fq, fk_blocks, wuk_pad, wuv_pad, bias_near, bias_far, k_sel)
            w_o = ev_w_o[j]
        else:
            sgu_b_tile = jnp.repeat(jnp.swapaxes(od_sgu_b[j], 0, 1), HEAD_DIM, axis=1)
            mix = _odd_mixer(xb, od_w_in[j].astype(BF16), od_sgu_g[j].reshape(1, -1), od_sgu_w[j],
                             sgu_b_tile, od_conv_w[j], seq, min(256, seq))
            w_o = od_w_o[j]
        xf, xb = _proj_ln(mix.reshape(n, d), w_o.astype(BF16), xf, g0, b0, alpha, tm)
        act = _ffn_up(xb, ffn_w_up[layer].astype(BF16), ffn_conv_w[layer], seq, tm_ffn, 256)
        xf, xb = _proj_ln(act, ffn_w_down[layer].astype(BF16), xf, g1, b1, alpha, tm)
    return xf.reshape(bsz, seq, d)
```

```python
import functools
import math

import numpy as np
import jax
import jax.numpy as jnp
from jax import lax
from jax.experimental import pallas as pl
from jax.experimental.pallas import tpu as pltpu

F32 = jnp.float32
BF16 = jnp.bfloat16

HEAD_DIM = 64
N_HEADS = 8
KV_RANK = 128
TOPK_MAX = 256
N_BUCKETS = 32
T5_MAX_DIST = 128
CHUNK = 128
CONV_W = 3
LN_EPS = 1e-5
MIX_HALF = N_HEADS * HEAD_DIM

LANES = 128
SUBLANES = 8
VMEM_LIMIT = 56 * 1024 * 1024

BLK = 128
NEG_BIG = -1e30
F32_MAX = float(np.finfo(np.float32).max)


def _cparams(sem):
    return pltpu.CompilerParams(dimension_semantics=sem, vmem_limit_bytes=VMEM_LIMIT)


def _dot(a, b):
    return jnp.dot(a, b, preferred_element_type=F32)


def _dot_nt(a, b):
    return lax.dot_general(a, b, (((1,), (1,)), ((), ())), preferred_element_type=F32)


def _layer_norm_rows(y, g, b):
    mu = jnp.mean(y, axis=-1, keepdims=True)
    yc = y - mu
    var = jnp.mean(yc * yc, axis=-1, keepdims=True)
    return yc * lax.rsqrt(var + LN_EPS) * g + b


def _even_proj_kernel(x_ref, wb_ref, ws_ref, hb_ref, hs_ref):
    x = x_ref[...]
    hb_ref[...] = _dot(x, wb_ref[...]).astype(hb_ref.dtype)
    hs_ref[...] = _dot(x, ws_ref[...])


def _even_proj(xb, w_big, w_small, tm):
    n, k = xb.shape
    mb = w_big.shape[1]
    ms = w_small.shape[1]
    return pl.pallas_call(
        _even_proj_kernel,
        grid=(n // tm,),
        in_specs=[pl.BlockSpec((tm, k), lambda i: (i, 0)),
                  pl.BlockSpec((k, mb), lambda i: (0, 0)),
                  pl.BlockSpec((k, ms), lambda i: (0, 0))],
        out_specs=[pl.BlockSpec((tm, mb), lambda i: (i, 0)),
                   pl.BlockSpec((tm, ms), lambda i: (i, 0))],
        out_shape=[jax.ShapeDtypeStruct((n, mb), BF16),
                   jax.ShapeDtypeStruct((n, ms), F32)],
        compiler_params=_cparams(("parallel",)),
        name="even_proj",
    )(xb, w_big, w_small)


def _cumf_kernel(f_ref, b_ref, o_ref):
    seq = f_ref.shape[-1]
    row = lax.broadcasted_iota(jnp.int32, (LANES, LANES), 0)
    col = lax.broadcasted_iota(jnp.int32, (LANES, LANES), 1)
    upper = jnp.where(row <= col, 1.0, 0.0).astype(F32)
    carry = jnp.zeros((f_ref.shape[0], 1), F32)
    for c in range(seq // LANES):
        z = f_ref[:, c * LANES:(c + 1) * LANES] + b_ref[...]
        lf = jnp.minimum(z, 0.0) - jnp.log1p(jnp.exp(-jnp.abs(z)))
        cs = jnp.dot(lf, upper, preferred_element_type=F32,
                     precision=lax.Precision.HIGHEST) + carry
        o_ref[:, c * LANES:(c + 1) * LANES] = cs
        carry = cs[:, LANES - 1:LANES]


def _cumf(f_t, b_f):
    bsz, nh, seq = f_t.shape
    return pl.pallas_call(
        _cumf_kernel,
        grid=(bsz,),
        in_specs=[pl.BlockSpec((None, nh, seq), lambda b: (b, 0, 0)),
                  pl.BlockSpec((nh, 1), lambda b: (0, 0))],
        out_specs=pl.BlockSpec((None, nh, seq), lambda b: (b, 0, 0)),
        out_shape=jax.ShapeDtypeStruct((bsz, nh, seq), F32),
        compiler_params=_cparams(("parallel",)),
        name="fox_cumf",
    )(f_t, b_f)


def _attn_kernel(qa_ref, qi_ref, qb_ref, hs_ref, fq_ref, kb_ref, vb_ref, ckv_ref,
                 klo_ref, khi_ref, fk_ref, wuk_ref, wuv_ref, bnear_ref, bfar_ref,
                 o_ref,
                 score_ref, thr_ref, qlat_ref, qis_ref, qbm_ref, wb_ref, fqb_ref,
                 mlane_ref, m_ref, mfar_ref, lsum_ref, acc_ref, *, k_sel):
    T = BLK
    H = N_HEADS
    i = pl.program_id(1)
    nb = i + 1
    lane = lax.broadcasted_iota(jnp.int32, (T, LANES), 1)
    row_i = lax.broadcasted_iota(jnp.int32, (T, T), 0)
    col_i = lax.broadcasted_iota(jnp.int32, (T, T), 1)
    causal = col_i <= row_i
    lo_half = lane < HEAD_DIM

    w_idx = hs_ref[:, 0:H] * (H ** -0.5)
    fq = fq_ref[...]
    for h in range(H):
        wb_ref[h] = jnp.broadcast_to(w_idx[:, h:h + 1], (T, LANES))
        fqb_ref[h] = jnp.broadcast_to(fq[:, h:h + 1], (T, LANES))
    for p in range(H // 2):
        qp = qa_ref[:, p * LANES:(p + 1) * LANES]
        qlat_ref[2 * p] = _dot(qp, wuk_ref[2 * p]).astype(BF16)
        qlat_ref[2 * p + 1] = _dot(qp, wuk_ref[2 * p + 1]).astype(BF16)
        qis_ref[p] = qi_ref[:, p * LANES:(p + 1) * LANES]

    def idx_scores(j):
        start = pl.multiple_of(j * T, T)
        qis = qis_ref[...].reshape(4 * T, LANES)
        s_even = _dot_nt(qis, klo_ref[pl.ds(start, T), :])
        s_odd = _dot_nt(qis, khi_ref[pl.ds(start, T), :])
        acc = jnp.zeros((T, T), F32)
        for p in range(H // 2):
            acc = acc + wb_ref[2 * p] * jnp.maximum(s_even[p * T:(p + 1) * T], 0.0)
            acc = acc + wb_ref[2 * p + 1] * jnp.maximum(s_odd[p * T:(p + 1) * T], 0.0)
        return acc

    def idx_body(j, c):
        score_ref[j] = idx_scores(j)
        return c

    lax.fori_loop(0, i, idx_body, 0)
    score_ref[i] = jnp.where(causal, idx_scores(i), -jnp.inf)

    thr_ref[...] = jnp.full((T, LANES), -F32_MAX, F32)

    @pl.when((i + 1) * T > k_sel)
    def _select():
        kf = float(k_sel)
        ones = jnp.ones((LANES, LANES), BF16)
        t_pos = (i * T + lax.broadcasted_iota(jnp.int32, (T, LANES), 0)).astype(F32)
        searching = (t_pos + 1.0) > kf

        def row_count(pred_fn):
            def body(j, c):
                return c + jnp.where(pred_fn(score_ref[j], j), 1.0, 0.0)
            c = lax.fori_loop(0, nb, body, jnp.zeros((T, LANES), F32))
            return _dot(c.astype(BF16), ones)

        def count_ge(thr):
            return row_count(lambda s, j: s >= thr)

        def mm_body(j, carry):
            mn, mx = carry
            s = score_ref[j]
            return (jnp.minimum(mn, jnp.where(s == -jnp.inf, jnp.inf, s)), jnp.maximum(mx, s))

        mn, mx = lax.fori_loop(0, nb, mm_body,
                               (jnp.full((T, LANES), jnp.inf, F32),
                                jnp.full((T, LANES), -jnp.inf, F32)))
        mn = jnp.broadcast_to(jnp.min(mn, axis=1, keepdims=True), (T, LANES))
        mx = jnp.broadcast_to(jnp.max(mx, axis=1, keepdims=True), (T, LANES))
        c_mx = count_ge(mx)
        top_tied = c_mx >= kf
        lo0 = jnp.where(top_tied, mx, mn)
        cnt_lo0 = jnp.where(top_tied, c_mx, t_pos + 1.0)
        cnt_hi0 = jnp.where(top_tied, 0.0, c_mx)
        active0 = jnp.where(searching & jnp.logical_not(top_tied) & (cnt_lo0 > kf), 1.0, 0.0)

        def cond(st):
            return jnp.logical_and(st[5] > 0.0, st[6] < 2200)

        def body(st):
            lo, hi, cnt_lo, cnt_hi, active, _, it = st
            mid = 0.5 * lo + 0.5 * hi
            adjacent = (mid <= lo) | (mid >= hi)
            c = count_ge(mid)
            act = active > 0.0
            go_lo = act & jnp.logical_not(adjacent) & (c >= kf)
            go_hi = act & jnp.logical_not(adjacent) & (c < kf)
            lo = jnp.where(go_lo, mid, lo)
            cnt_lo = jnp.where(go_lo, c, cnt_lo)
            hi = jnp.where(go_hi, mid, hi)
            cnt_hi = jnp.where(go_hi, c, cnt_hi)
            active = jnp.where(act & jnp.logical_not(adjacent) & (cnt_lo > kf), 1.0, 0.0)
            return lo, hi, cnt_lo, cnt_hi, active, jnp.max(active), it + 1

        lo, hi, cnt_lo, cnt_hi, _, _, _ = lax.while_loop(
            cond, body, (lo0, mx, cnt_lo0, cnt_hi0, active0, jnp.max(active0), 0))
        thr = jnp.where(searching, lo, -F32_MAX)
        thr_ref[...] = thr

        tied = searching & (cnt_lo > kf)
        need = kf - cnt_hi

        @pl.when(jnp.max(jnp.where(tied, 1.0, 0.0)) > 0.0)
        def _break_ties():
            def count_tied_upto(jcut):
                def pred(s, j):
                    kidx = (j * T + col_i).astype(F32)
                    return (s == thr) & (kidx <= jcut)
                return row_count(pred)

            def bs_body(_, st):
                jl, jh = st
                jm = jnp.floor(0.5 * (jl + jh))
                ok = count_tied_upto(jm) >= need
                return jnp.where(ok, jl, jm), jnp.where(ok, jm, jh)

            n_keys = (nb * T).astype(F32)
            jl0 = jnp.full((T, LANES), -1.0, F32)
            jh0 = jnp.zeros((T, LANES), F32) + (n_keys - 1.0)
            n_steps = int(math.ceil(math.log2(score_ref.shape[0] * T))) + 1
            _, jcut = lax.fori_loop(0, n_steps, bs_body, (jl0, jh0))

            def drop_body(j, c):
                s = score_ref[j]
                kidx = (j * T + col_i).astype(F32)
                score_ref[j] = jnp.where(tied & (s == thr) & (kidx > jcut), -jnp.inf, s)
                return c
            lax.fori_loop(0, nb, drop_body, 0)

    HT = H * T

    def row_max_to_m():
        m_ref[...] = jnp.broadcast_to(jnp.max(mlane_ref[...], axis=1, keepdims=True), (HT, LANES))

    def start_pass_b():
        lsum_ref[...] = jnp.zeros(lsum_ref.shape, F32)
        acc_ref[...] = jnp.zeros(acc_ref.shape, F32)

    def normalised(h):
        rows = slice(h * T, (h + 1) * T)
        return acc_ref[rows, :] / jnp.sum(lsum_ref[rows, :], axis=1, keepdims=True)

    def dsa_masked(j):
        start = pl.multiple_of(j * T, T)
        c_blk = ckv_ref[pl.ds(start, T), :]
        s = _dot_nt(qlat_ref[...].reshape(HT, KV_RANK), c_blk).reshape(H, T, T)
        sel = score_ref[j] >= thr_ref[...]
        return jnp.where(sel[None], s, NEG_BIG), c_blk

    bfar = bfar_ref[...][:, None, :]

    def dsa_max_far(j, c):
        s, _ = dsa_masked(j)
        mlane_ref[...] = jnp.maximum(mlane_ref[...], s.reshape(HT, T))
        return c

    def dsa_max_near(j, bias):
        s, _ = dsa_masked(j)
        mlane_ref[...] = jnp.maximum(mlane_ref[...], (s + bias).reshape(HT, T))

    mlane_ref[...] = jnp.full(mlane_ref.shape, NEG_BIG, F32)
    lax.fori_loop(0, i - 1, dsa_max_far, 0)
    mlane_ref[...] = (mlane_ref[...].reshape(H, T, LANES) + bfar).reshape(HT, LANES)

    @pl.when(i >= 1)
    def _max_prev():
        dsa_max_near(i - 1, bnear_ref[:, :, 0:T])

    dsa_max_near(i, bnear_ref[:, :, T:2 * T])
    row_max_to_m()
    mfar_ref[...] = (m_ref[...].reshape(H, T, LANES) - bfar).reshape(HT, LANES)
    start_pass_b()

    def dsa_accumulate(p, c_blk):
        lsum_ref[...] += p
        acc_ref[...] += _dot(p.astype(BF16), c_blk)

    def dsa_acc_far(j, c):
        s, c_blk = dsa_masked(j)
        dsa_accumulate(jnp.exp(s.reshape(HT, T) - mfar_ref[...]), c_blk)
        return c

    def dsa_acc_near(j, bias):
        s, c_blk = dsa_masked(j)
        dsa_accumulate(jnp.exp((s + bias).reshape(HT, T) - m_ref[...]), c_blk)

    lax.fori_loop(0, i - 1, dsa_acc_far, 0)

    @pl.when(i >= 1)
    def _acc_prev():
        dsa_acc_near(i - 1, bnear_ref[:, :, 0:T])

    dsa_acc_near(i, bnear_ref[:, :, T:2 * T])

    for p in range(H // 2):
        o_pair = (_dot(normalised(2 * p).astype(BF16), wuv_ref[2 * p])
                  + _dot(normalised(2 * p + 1).astype(BF16), wuv_ref[2 * p + 1]))
        o_ref[:, p * LANES:(p + 1) * LANES] = o_pair.astype(o_ref.dtype)

    for p in range(H // 2):
        q_pair = qb_ref[:, p * LANES:(p + 1) * LANES]
        qbm_ref[2 * p] = jnp.where(lo_half, q_pair, 0)
        qbm_ref[2 * p + 1] = jnp.where(lo_half, 0, q_pair)

    def fox_logits(j, h, diag):
        start = pl.multiple_of(j * T, T)
        cols = slice((h // 2) * LANES, (h // 2 + 1) * LANES)
        s = _dot_nt(qbm_ref[h], kb_ref[pl.ds(start, T), cols])
        s = s + (fqb_ref[h] - fk_ref[j][h:h + 1, :])
        return jnp.where(causal, s, NEG_BIG) if diag else s

    def fox_max(j, diag):
        for h in range(H):
            rows = slice(h * T, (h + 1) * T)
            mlane_ref[rows, :] = jnp.maximum(mlane_ref[rows, :], fox_logits(j, h, diag))

    def fox_acc(j, diag):
        start = pl.multiple_of(j * T, T)
        for h in range(H):
            rows = slice(h * T, (h + 1) * T)
            cols = slice((h // 2) * LANES, (h // 2 + 1) * LANES)
            pr = jnp.exp(fox_logits(j, h, diag) - m_ref[rows, :])
            lsum_ref[rows, :] += pr
            acc_ref[rows, :] += _dot(pr.astype(BF16), vb_ref[pl.ds(start, T), cols])

    def fox_max_body(j, c):
        fox_max(j, False)
        return c

    def fox_acc_body(j, c):
        fox_acc(j, False)
        return c

    mlane_ref[...] = jnp.full(mlane_ref.shape, NEG_BIG, F32)
    lax.fori_loop(0, i, fox_max_body, 0)
    fox_max(i, True)
    row_max_to_m()
    start_pass_b()
    lax.fori_loop(0, i, fox_acc_body, 0)
    fox_acc(i, True)

    for p in range(H // 2):
        o_ref[:, MIX_HALF + p * LANES:MIX_HALF + (p + 1) * LANES] = (
            jnp.where(lo_half, normalised(2 * p), normalised(2 * p + 1)).astype(o_ref.dtype))


_EV_QA, _EV_QI, _EV_QB, _EV_KB, _EV_VB = 0, 1, 2, 3, 4
_EV_CKV, _EV_KLO, _EV_KHI = 20, 21, 22
_EV_WIDTH = 23 * LANES


def _attention(hb, hs, fq, fk_blocks, wuk_pad, wuv_pad, bias_near, bias_far, k_sel):
    bsz, seq, _ = hb.shape
    T = BLK
    nblk = seq // T
    H = N_HEADS
    wide = lambda c: pl.BlockSpec((None, T, MIX_HALF), lambda b, i, c=c: (b, i, c))
    full_wide = lambda c: pl.BlockSpec((None, seq, MIX_HALF), lambda b, i, c=c: (b, 0, c))
    full_narrow = lambda c: pl.BlockSpec((None, seq, LANES), lambda b, i, c=c: (b, 0, c))
    const3 = lambda shape: pl.BlockSpec(shape, lambda b, i: (0, 0, 0))
    return pl.pallas_call(
        functools.partial(_attn_kernel, k_sel=k_sel),
        grid=(bsz, nblk),
        in_specs=[wide(_EV_QA), wide(_EV_QI), wide(_EV_QB),
                  pl.BlockSpec((None, T, LANES), lambda b, i: (b, i, 0)),
                  pl.BlockSpec((None, T, H), lambda b, i: (b, i, 0)),
                  full_wide(_EV_KB), full_wide(_EV_VB),
                  full_narrow(_EV_CKV), full_narrow(_EV_KLO), full_narrow(_EV_KHI),
                  pl.BlockSpec((None, nblk, H, T), lambda b, i: (b, 0, 0, 0)),
                  const3((H, LANES, KV_RANK)), const3((H, KV_RANK, LANES)),
                  const3((H, T, 2 * T)),
                  pl.BlockSpec((H, LANES), lambda b, i: (0, 0))],
        out_specs=pl.BlockSpec((None, T, 2 * MIX_HALF), lambda b, i: (b, i, 0)),
        out_shape=jax.ShapeDtypeStruct((bsz, seq, 2 * MIX_HALF), BF16),
        scratch_shapes=[pltpu.VMEM((nblk, T, T), F32),
                        pltpu.VMEM((T, LANES), F32),
                        pltpu.VMEM((H, T, KV_RANK), BF16),
                        pltpu.VMEM((H // 2, T, LANES), BF16),
                        pltpu.VMEM((H, T, LANES), BF16),
                        pltpu.VMEM((H, T, LANES), F32),
                        pltpu.VMEM((H, T, LANES), F32),
                        pltpu.VMEM((H * T, LANES), F32),
                        pltpu.VMEM((H * T, LANES), F32),
                        pltpu.VMEM((H * T, LANES), F32),
                        pltpu.VMEM((H * T, LANES), F32),
                        pltpu.VMEM((H * T, LANES), F32)],
        compiler_params=_cparams(("parallel", "arbitrary")),
        name="dsa_fox_attention",
    )(hb, hb, hb, hs, fq, hb, hb, hb, hb, hb, fk_blocks, wuk_pad, wuv_pad, bias_near, bias_far)


def _proj_ln_kernel(a_ref, w_ref, x_ref, g_ref, b_ref, xo_ref, xb_ref, *, alpha):
    y = alpha * x_ref[...] + _dot(a_ref[...], w_ref[...])
    out = _layer_norm_rows(y, g_ref[...], b_ref[...])
    xo_ref[...] = out
    xb_ref[...] = out.astype(BF16)


def _proj_ln(a, w, x, g, b, alpha, tm):
    n, k = a.shape
    d = w.shape[1]
    return pl.pallas_call(
        functools.partial(_proj_ln_kernel, alpha=alpha),
        grid=(n // tm,),
        in_specs=[pl.BlockSpec((tm, k), lambda i: (i, 0)),
                  pl.BlockSpec((k, d), lambda i: (0, 0)),
                  pl.BlockSpec((tm, d), lambda i: (i, 0)),
                  pl.BlockSpec((1, d), lambda i: (0, 0)),
                  pl.BlockSpec((1, d), lambda i: (0, 0))],
        out_specs=[pl.BlockSpec((tm, d), lambda i: (i, 0)),
                   pl.BlockSpec((tm, d), lambda i: (i, 0))],
        out_shape=[jax.ShapeDtypeStruct((n, d), F32),
                   jax.ShapeDtypeStruct((n, d), BF16)],
        compiler_params=_cparams(("parallel",)),
        name="proj_residual_ln",
    )(a, w, x, g, b)


def _causal_conv3(h, tail_ref, cw, first_tile):
    tm = h.shape[0]
    w0, w1, w2 = cw[0:1, :], cw[1:2, :], cw[2:3, :]
    out = w0 * pltpu.roll(h, 2, axis=0) + w1 * pltpu.roll(h, 1, axis=0) + w2 * h
    tail = jnp.where(first_tile, 0.0, tail_ref[...])
    head = h[0:SUBLANES, :]
    r = lax.broadcasted_iota(jnp.int32, head.shape, 0)
    hm1 = jnp.where(r < 1, pltpu.roll(tail, 1, axis=0), pltpu.roll(head, 1, axis=0))
    hm2 = jnp.where(r < 2, pltpu.roll(tail, 2, axis=0), pltpu.roll(head, 2, axis=0))
    out_head = w0 * hm2 + w1 * hm1 + w2 * head
    tail_ref[...] = h[tm - SUBLANES:tm, :]
    return out, out_head


def _ffn_up_kernel(x_ref, wg_ref, wv_ref, cg_ref, cv_ref, o_ref, tg_ref, tv_ref, *, tiles_per_seq):
    first = (pl.program_id(1) % tiles_per_seq) == 0
    x = x_ref[...]
    g, g_head = _causal_conv3(_dot(x, wg_ref[...]), tg_ref, cg_ref[...], first)
    v, v_head = _causal_conv3(_dot(x, wv_ref[...]), tv_ref, cv_ref[...], first)

    def act(a, b):
        return (a / (1.0 + jnp.exp(-a)) * b).astype(o_ref.dtype)

    o_ref[...] = act(g, v)
    o_ref[0:SUBLANES, :] = act(g_head, v_head)


def _ffn_up(xb, w_up, conv_w, seq, tm, tn):
    n, k = xb.shape
    dff = w_up.shape[1] // 2
    ncol = dff // tn
    return pl.pallas_call(
        functools.partial(_ffn_up_kernel, tiles_per_seq=seq // tm),
        grid=(ncol, n // tm),
        in_specs=[pl.BlockSpec((tm, k), lambda j, i: (i, 0)),
                  pl.BlockSpec((k, tn), lambda j, i: (0, j)),
                  pl.BlockSpec((k, tn), lambda j, i, ncol=ncol: (0, j + ncol)),
                  pl.BlockSpec((CONV_W, tn), lambda j, i: (0, j)),
                  pl.BlockSpec((CONV_W, tn), lambda j, i, ncol=ncol: (0, j + ncol))],
        out_specs=pl.BlockSpec((tm, tn), lambda j, i: (i, j)),
        out_shape=jax.ShapeDtypeStruct((n, dff), BF16),
        scratch_shapes=[pltpu.VMEM((SUBLANES, tn), F32), pltpu.VMEM((SUBLANES, tn), F32)],
        compiler_params=_cparams(("arbitrary", "arbitrary")),
        name="ffn_up_conv_gate",
    )(xb, w_up, w_up, conv_w, conv_w)


def _gelu_tanh(x):
    return 0.5 * x * (1.0 + jnp.tanh(math.sqrt(2.0 / math.pi) * (x + 0.044715 * (x * x * x))))


def _group_mean(v, avg):
    hi = v.astype(BF16)
    lo = (v - hi.astype(F32)).astype(BF16)
    return _dot(hi, avg) + _dot(lo, avg)


def _odd_mixer_kernel(x_ref, w_ref, sg_ref, sw_ref, sb_ref, cw_ref, o_ref, tail_ref, *, tiles_per_seq):
    tm = x_ref.shape[0]
    W = MIX_HALF
    first = (pl.program_id(0) % tiles_per_seq) == 0
    x = x_ref[...]
    lane = lax.broadcasted_iota(jnp.int32, (CHUNK, LANES), 1)
    lo_half = lane < HEAD_DIM
    r = lax.broadcasted_iota(jnp.int32, (W, W), 0) // HEAD_DIM
    c = lax.broadcasted_iota(jnp.int32, (W, W), 1) // HEAD_DIM
    avg = jnp.where(r == c, 1.0 / HEAD_DIM, 0.0).astype(BF16)
    tri = (lax.broadcasted_iota(jnp.int32, (CHUNK, CHUNK), 1)
           <= lax.broadcasted_iota(jnp.int32, (CHUNK, CHUNK), 0))

    u = _gelu_tanh(_dot(x, w_ref[:, 0:W]))
    v = _gelu_tanh(_dot(x, w_ref[:, W:2 * W]))
    vc = v - _group_mean(v, avg)
    var = _group_mean(vc * vc, avg)
    vn = (vc * lax.rsqrt(var + LN_EPS) * sg_ref[...]).astype(BF16)
    for n in range(tm // CHUNK):
        rows = slice(n * CHUNK, (n + 1) * CHUNK)
        for p in range(N_HEADS // 2):
            cols = slice(p * LANES, (p + 1) * LANES)
            vp = vn[rows, cols]
            w_e = jnp.where(tri, sw_ref[2 * p], 0.0).astype(BF16)
            w_o = jnp.where(tri, sw_ref[2 * p + 1], 0.0).astype(BF16)
            mix = (_dot(w_e, jnp.where(lo_half, vp, 0)) + _dot(w_o, jnp.where(lo_half, 0, vp))
                   + sb_ref[:, cols])
            o_ref[rows, cols] = (u[rows, cols] * mix).astype(o_ref.dtype)

    g_b = _dot(x, w_ref[:, 2 * W:3 * W])
    y = _dot(x, w_ref[:, 3 * W:4 * W]) * _dot(x, w_ref[:, 4 * W:5 * W])
    conv, conv_head = _causal_conv3(y, tail_ref, cw_ref[...], first)
    o_ref[:, W:2 * W] = (g_b * conv).astype(o_ref.dtype)
    o_ref[0:SUBLANES, W:2 * W] = (g_b[0:SUBLANES, :] * conv_head).astype(o_ref.dtype)


def _odd_mixer(xb, w_in, sgu_g, sgu_w, sgu_b_tile, conv_w, seq, tm):
    n, k = xb.shape
    W = MIX_HALF
    return pl.pallas_call(
        functools.partial(_odd_mixer_kernel, tiles_per_seq=seq // tm),
        grid=(n // tm,),
        in_specs=[pl.BlockSpec((tm, k), lambda i: (i, 0)),
                  pl.BlockSpec((k, 5 * W), lambda i: (0, 0)),
                  pl.BlockSpec((1, W), lambda i: (0, 0)),
                  pl.BlockSpec((N_HEADS, CHUNK, CHUNK), lambda i: (0, 0, 0)),
                  pl.BlockSpec((CHUNK, W), lambda i: (0, 0)),
                  pl.BlockSpec((CONV_W, W), lambda i: (0, 0))],
        out_specs=pl.BlockSpec((tm, 2 * W), lambda i: (i, 0)),
        out_shape=jax.ShapeDtypeStruct((n, 2 * W), BF16),
        scratch_shapes=[pltpu.VMEM((SUBLANES, W), F32)],
        compiler_params=_cparams(("arbitrary",)),
        name="odd_mixer",
    )(xb, w_in, sgu_g, sgu_w, sgu_b_tile, conv_w)


def _t5_bucket_np(dist):
    max_exact = N_BUCKETS // 2
    n = np.maximum(dist, 0)
    nf = np.maximum(n, 1).astype(np.float32)
    large = max_exact + (np.log(nf / max_exact) / np.float32(math.log(T5_MAX_DIST / max_exact))
                         * (N_BUCKETS - max_exact)).astype(np.int32)
    large = np.minimum(large, N_BUCKETS - 1)
    return np.where(n < max_exact, n, large)


def _bias_tables(rel_bias):
    T = BLK
    dist = np.arange(T)[:, None] + T - np.arange(2 * T)[None, :]
    bucket = _t5_bucket_np(dist)
    far_bucket = _t5_bucket_np(np.array([T + 1]))[0]
    assert far_bucket == N_BUCKETS - 1
    assert (_t5_bucket_np(np.arange(T + 1, 64 * T)) == far_bucket).all()
    near = jnp.moveaxis(rel_bias[bucket], -1, 0)
    far = jnp.broadcast_to(rel_bias[far_bucket][:, None], (N_HEADS, LANES))
    return near.astype(F32), far.astype(F32)


def _even_weights(w_in, w_uk, w_uv):
    hd, H = HEAD_DIM, N_HEADS
    widths = (H * hd, KV_RANK, H * hd, hd, H, H * hd, H * hd, H * hd, H)
    offs = np.cumsum((0,) + widths)
    seg = lambda s: w_in[:, offs[s]:offs[s + 1]]
    q_a, c_kv, q_idx, k_idx, w_idx, q_b, k_b, v_b, f_in = [seg(s) for s in range(9)]
    zk = jnp.zeros_like(k_idx)
    scale = hd ** -0.5
    w_big = jnp.concatenate([q_a, q_idx * scale, q_b * scale, k_b, v_b, c_kv,
                             k_idx, zk, zk, k_idx], axis=1).astype(BF16)
    pad = jnp.zeros((w_in.shape[0], LANES - 2 * H), w_in.dtype)
    w_small = jnp.concatenate([w_idx, f_in, pad], axis=1).astype(BF16)
    wuk_t = jnp.swapaxes(w_uk, 1, 2) * scale
    wuk_pad = jnp.zeros((H, LANES, KV_RANK), F32)
    wuv_pad = jnp.zeros((H, KV_RANK, LANES), F32)
    for h in range(H):
        o = hd * (h % 2)
        wuk_pad = wuk_pad.at[h, o:o + hd, :].set(wuk_t[h])
        wuv_pad = wuv_pad.at[h, :, o:o + hd].set(w_uv[h])
    return w_big, w_small, wuk_pad.astype(BF16), wuv_pad.astype(BF16)


def kernel(x, ln_g, ln_b, rel_bias, ev_w_in, ev_w_uk, ev_w_uv, ev_b_f, ev_w_o,
           od_w_in, od_sgu_g, od_sgu_w, od_sgu_b, od_conv_w, od_w_o,
           ffn_w_up, ffn_conv_w, ffn_w_down):
    bsz, seq, d = x.shape
    depth = ln_g.shape[0]
    n = bsz * seq
    alpha = (2.0 * depth) ** 0.25
    k_sel = min(TOPK_MAX, seq // 4)
    H = N_HEADS
    assert seq % BLK == 0 and seq % CHUNK == 0 and k_sel % 1 == 0
    tm = min(512, seq)
    tm_ffn = min(1024, seq)

    bias_near, bias_far = _bias_tables(rel_bias)
    xf = x.reshape(n, d)
    xb = xf.astype(BF16)
    for layer in range(depth):
        j = layer // 2
        g0, b0 = ln_g[layer, 0][None, :], ln_b[layer, 0][None, :]
        g1, b1 = ln_g[layer, 1][None, :], ln_b[layer, 1][None, :]
        if layer % 2 == 0:
            w_big, w_small, wuk_pad, wuv_pad = _even_weights(ev_w_in[j], ev_w_uk[j], ev_w_uv[j])
            hb, hs = _even_proj(xb, w_big, w_small, tm)
            hb = hb.reshape(bsz, seq, -1)
            hs = hs.reshape(bsz, seq, -1)
            f_t = jnp.swapaxes(hs[:, :, H:2 * H], 1, 2)
            cum_f = _cumf(f_t, ev_b_f[j][:, None])
            fq = jnp.swapaxes(cum_f, 1, 2)
            fk_blocks = jnp.swapaxes(cum_f.reshape(bsz, H, seq // BLK, BLK), 1, 2)
            mix = _attention(hb, hs, fq, fk_blocks, wuk_pad, wuv_pad, bias_near, bias_far, k_sel)
            w_o = ev_w_o[j]
        else:
            sgu_b_tile = jnp.repeat(jnp.swapaxes(od_sgu_b[j], 0, 1), HEAD_DIM, axis=1)
            mix = _odd_mixer(xb, od_w_in[j].astype(BF16), od_sgu_g[j].reshape(1, -1), od_sgu_w[j],
                             sgu_b_tile, od_conv_w[j], seq, min(256, seq))
            w_o = od_w_o[j]
        xf, xb = _proj_ln(mix.reshape(n, d), w_o.astype(BF16), xf, g0, b0, alpha, tm)
        act = _ffn_up(xb, ffn_w_up[layer].astype(BF16), ffn_conv_w[layer], seq, tm_ffn, 256)
        xf, xb = _proj_ln(act, ffn_w_down[layer].astype(BF16), xf, g1, b1, alpha, tm)
    return xf.reshape(bsz, seq, d)
```

```python
import functools
import math

import numpy as np
import jax
import jax.numpy as jnp
from jax import lax
from jax.experimental import pallas as pl
from jax.experimental.pallas import tpu as pltpu

F32 = jnp.float32
BF16 = jnp.bfloat16

HEAD_DIM = 64
N_HEADS = 8
KV_RANK = 128
TOPK_MAX = 256
N_BUCKETS = 32
T5_MAX_DIST = 128
CHUNK = 128
CONV_W = 3
LN_EPS = 1e-5
MIX_HALF = N_HEADS * HEAD_DIM

LANES = 128
SUBLANES = 8
BF16_ROWS = 16
VMEM_LIMIT = 56 * 1024 * 1024

BLK = 128
PV_ROWS = KV_RANK + BF16_ROWS
NEG_BIG = -1e30
MASKED = -(2.0 ** 100)
F32_MAX = float(np.finfo(np.float32).max)
LOG2E = math.log2(math.e)


def _cparams(sem):
    return pltpu.CompilerParams(dimension_semantics=sem, vmem_limit_bytes=VMEM_LIMIT)


def _dot(a, b):
    return jnp.dot(a, b, preferred_element_type=F32)


def _dot_nt(a, b):
    return lax.dot_general(a, b, (((1,), (1,)), ((), ())), preferred_element_type=F32)


def _layer_norm_rows(y, g, b):
    mu = jnp.mean(y, axis=-1, keepdims=True)
    yc = y - mu
    var = jnp.mean(yc * yc, axis=-1, keepdims=True)
    return yc * lax.rsqrt(var + LN_EPS) * g + b


def _bf16_pieces(v):
    hi = v.astype(BF16)
    r1 = v - hi.astype(F32)
    mid = r1.astype(BF16)
    lo = (r1 - mid.astype(F32)).astype(BF16)
    return jnp.concatenate([hi, mid, lo], axis=1)


def _even_proj_kernel(x_ref, wb_ref, ws_ref, hb_ref, hs_ref):
    x = x_ref[...]
    hb_ref[...] = _dot(x, wb_ref[...]).astype(hb_ref.dtype)
    hs_ref[...] = _dot(x, ws_ref[...])


def _even_proj(xb, w_big, w_small, tm):
    n, k = xb.shape
    mb = w_big.shape[1]
    ms = w_small.shape[1]
    return pl.pallas_call(
        _even_proj_kernel,
        grid=(n // tm,),
        in_specs=[pl.BlockSpec((tm, k), lambda i: (i, 0)),
                  pl.BlockSpec((k, mb), lambda i: (0, 0)),
                  pl.BlockSpec((k, ms), lambda i: (0, 0))],
        out_specs=[pl.BlockSpec((tm, mb), lambda i: (i, 0)),
                   pl.BlockSpec((tm, ms), lambda i: (i, 0))],
        out_shape=[jax.ShapeDtypeStruct((n, mb), BF16),
                   jax.ShapeDtypeStruct((n, ms), F32)],
        compiler_params=_cparams(("parallel",)),
        name="even_proj",
    )(xb, w_big, w_small)


def _cumf_kernel(f_ref, b_ref, o_ref):
    seq = f_ref.shape[-1]
    row = lax.broadcasted_iota(jnp.int32, (LANES, LANES), 0)
    col = lax.broadcasted_iota(jnp.int32, (LANES, LANES), 1)
    upper = jnp.where(row <= col, 1.0, 0.0).astype(F32)
    carry = jnp.zeros((f_ref.shape[0], 1), F32)
    for c in range(seq // LANES):
        z = f_ref[:, c * LANES:(c + 1) * LANES] + b_ref[...]
        lf = jnp.minimum(z, 0.0) - jnp.log1p(jnp.exp(-jnp.abs(z)))
        cs = jnp.dot(lf, upper, preferred_element_type=F32,
                     precision=lax.Precision.HIGHEST) + carry
        o_ref[:, c * LANES:(c + 1) * LANES] = cs
        carry = cs[:, LANES - 1:LANES]


def _cumf(f_t, b_f):
    bsz, nh, seq = f_t.shape
    return pl.pallas_call(
        _cumf_kernel,
        grid=(bsz,),
        in_specs=[pl.BlockSpec((None, nh, seq), lambda b: (b, 0, 0)),
                  pl.BlockSpec((nh, 1), lambda b: (0, 0))],
        out_specs=pl.BlockSpec((None, nh, seq), lambda b: (b, 0, 0)),
        out_shape=jax.ShapeDtypeStruct((bsz, nh, seq), F32),
        compiler_params=_cparams(("parallel",)),
        name="fox_cumf",
    )(f_t, b_f)


def _attn_kernel(qa_ref, qi_ref, qb_ref, wt_ref, fq_ref, kb_ref, vb_ref, ckv_ref, kidx_ref, fk_ref,
                 wuk_ref, wuv_ref, bnear_ref, bfar_ref, pq_ref, cq_ref, pk_ref, ck_ref,
                 o_ref,
                 score_ref, lhsd_ref, ct_ref, kaug_ref, vt_ref, qsel_ref, qaug_ref, qim_ref,
                 m_ref, acc_ref, *, k_sel):
    T = BLK
    H = N_HEADS
    NP = H // 2
    G = T // SUBLANES
    i = pl.program_id(1)
    nb = i + 1
    nblk = score_ref.shape[0]
    lane = lax.broadcasted_iota(jnp.int32, (T, LANES), 1)
    lo_half = lane < HEAD_DIM
    key_i = lax.broadcasted_iota(jnp.int32, (T, T), 0)
    qry_i = lax.broadcasted_iota(jnp.int32, (T, T), 1)
    ones_rows = jnp.ones((BF16_ROWS, T), BF16)

    @pl.when(i == 0)
    def _per_batch():
        def prep(j, c):
            r = pl.ds(pl.multiple_of(j * T, T), T)
            cb = ckv_ref[r, :]
            lhsd_ref[r, 0:KV_RANK] = cb
            ct_ref[j, 0:KV_RANK, :] = cb.astype(F32).T.astype(BF16)
            ct_ref[j, KV_RANK:PV_ROWS, :] = ones_rows
            ext = (_dot(_bf16_pieces(fk_ref[r, :] * LOG2E), pk_ref[...]) + ck_ref[...]).astype(BF16)
            for p in range(NP):
                cols = slice(p * LANES, (p + 1) * LANES)
                kaug_ref[p, r, 0:LANES] = kb_ref[r, cols]
                kaug_ref[p, r, LANES:2 * LANES] = ext[:, cols]
                vt_ref[p, j, 0:LANES, :] = vb_ref[r, cols].astype(F32).T.astype(BF16)
                vt_ref[p, j, LANES:PV_ROWS, :] = ones_rows
            return c
        lax.fori_loop(0, nblk, prep, 0)

    w_rows = wt_ref[...] * (H ** -0.5)
    ext_q = (_dot(_bf16_pieces(fq_ref[...] * LOG2E), pq_ref[...]) + cq_ref[...]).astype(BF16)
    eye = jnp.where(key_i == qry_i, 1.0, 0.0).astype(BF16)
    for p in range(NP):
        cols = slice(p * LANES, (p + 1) * LANES)
        qa_pair, qi_pair, qb_pair = qa_ref[:, cols], qi_ref[:, cols], qb_ref[:, cols]
        for e in range(2):
            h = 2 * p + e
            rows = slice(e * T, (e + 1) * T)
            mine = lo_half if e == 0 else jnp.logical_not(lo_half)
            qsel_ref[p, rows, 0:KV_RANK] = _dot(qa_pair, wuk_ref[h]).astype(BF16)
            qsel_ref[p, rows, KV_RANK:2 * KV_RANK] = eye
            qaug_ref[p, rows, 0:LANES] = jnp.where(mine, qb_pair, 0)
            qaug_ref[p, rows, LANES:2 * LANES] = ext_q[:, h * LANES:(h + 1) * LANES]
            qim_ref[p, rows, :] = jnp.where(mine, qi_pair, 0)

    def idx_scores(j):
        k_blk = kidx_ref[pl.ds(pl.multiple_of(j * T, T), T), :]
        acc = jnp.zeros((T, T), F32)
        for p in range(NP):
            s = _dot_nt(k_blk, qim_ref[p])
            for e in range(2):
                h = 2 * p + e
                acc = acc + w_rows[h:h + 1, :] * jnp.maximum(s[:, e * T:(e + 1) * T], 0.0)
        return acc

    def idx_body(j, c):
        score_ref[j] = idx_scores(j)
        return c

    lax.fori_loop(0, i, idx_body, 0)
    score_ref[i] = jnp.where(key_i <= qry_i, idx_scores(i), -jnp.inf)

    kf = float(k_sel)
    t_pos = (i * T + lax.broadcasted_iota(jnp.int32, (SUBLANES, LANES), 1)).astype(F32)
    searching = (t_pos + 1.0) > kf
    key_in_blk = (lax.broadcasted_iota(jnp.int32, (G, SUBLANES, LANES), 0) * SUBLANES
                  + lax.broadcasted_iota(jnp.int32, (G, SUBLANES, LANES), 1))

    def rep(v):
        return jnp.broadcast_to(v, (SUBLANES, LANES))

    def count(pred_fn):
        def body(j, c):
            hit = jnp.where(pred_fn(score_ref[j].reshape(G, SUBLANES, LANES), j), 1.0, 0.0)
            return c + jnp.sum(hit, axis=0)
        c = lax.fori_loop(0, nb, body, jnp.zeros((SUBLANES, LANES), F32))
        return rep(jnp.sum(c, axis=0, keepdims=True))

    def count_ge(thr):
        return count(lambda s, j: s >= thr[None])

    def select_threshold():
        def mm_body(j, carry):
            mn, mx = carry
            s = score_ref[j].reshape(G, SUBLANES, LANES)
            return (jnp.minimum(mn, jnp.min(jnp.where(s == -jnp.inf, jnp.inf, s), axis=0)),
                    jnp.maximum(mx, jnp.max(s, axis=0)))

        mn, mx = lax.fori_loop(0, nb, mm_body,
                               (jnp.full((SUBLANES, LANES), jnp.inf, F32),
                                jnp.full((SUBLANES, LANES), -jnp.inf, F32)))
        mn = rep(jnp.min(mn, axis=0, keepdims=True))
        mx = rep(jnp.max(mx, axis=0, keepdims=True))
        c_mx = count_ge(mx)
        top_tied = c_mx >= kf
        lo0 = jnp.where(top_tied, mx, mn)
        cnt_lo0 = jnp.where(top_tied, c_mx, t_pos + 1.0)
        cnt_hi0 = jnp.where(top_tied, 0.0, c_mx)
        active0 = jnp.where(searching & jnp.logical_not(top_tied) & (cnt_lo0 > kf), 1.0, 0.0)

        def cond(st):
            return jnp.logical_and(st[5] > 0.0, st[6] < 2200)

        def body(st):
            lo, hi, cnt_lo, cnt_hi, active, _, it = st
            any_active = jnp.max(active)
            mid = 0.5 * lo + 0.5 * hi
            adjacent = (mid <= lo) | (mid >= hi)
            c = count_ge(mid)
            act = active > 0.0
            go_lo = act & jnp.logical_not(adjacent) & (c >= kf)
            go_hi = act & jnp.logical_not(adjacent) & (c < kf)
            lo = jnp.where(go_lo, mid, lo)
            cnt_lo = jnp.where(go_lo, c, cnt_lo)
            hi = jnp.where(go_hi, mid, hi)
            cnt_hi = jnp.where(go_hi, c, cnt_hi)
            active = jnp.where(act & jnp.logical_not(adjacent) & (cnt_lo > kf), 1.0, 0.0)
            return lo, hi, cnt_lo, cnt_hi, active, any_active, it + 1

        lo, hi, cnt_lo, cnt_hi, _, _, _ = lax.while_loop(
            cond, body, (lo0, mx, cnt_lo0, cnt_hi0, active0, jnp.float32(1.0), 0))
        thr = jnp.where(searching, lo, -F32_MAX)

        tied = searching & (cnt_lo > kf)
        need = kf - cnt_hi

        @pl.when(jnp.max(jnp.where(tied, 1.0, 0.0)) > 0.0)
        def _break_ties():
            def key_index(j):
                return (j * T + key_in_blk).astype(F32)

            def count_tied_upto(jcut):
                return count(lambda s, j: (s == thr[None]) & (key_index(j) <= jcut[None]))

            def bs_body(_, st):
                jl, jh = st
                jm = jnp.floor(0.5 * (jl + jh))
                ok = count_tied_upto(jm) >= need
                return jnp.where(ok, jl, jm), jnp.where(ok, jm, jh)

            n_keys = (nb * T).astype(F32)
            jl0 = jnp.full((SUBLANES, LANES), -1.0, F32)
            jh0 = jnp.zeros((SUBLANES, LANES), F32) + (n_keys - 1.0)
            n_steps = int(math.ceil(math.log2(nblk * T))) + 1
            _, jcut = lax.fori_loop(0, n_steps, bs_body, (jl0, jh0))

            def drop_body(j, c):
                s = score_ref[j].reshape(G, SUBLANES, LANES)
                drop = tied[None] & (s == thr[None]) & (key_index(j) > jcut[None])
                score_ref[j] = jnp.where(drop, -jnp.inf, s).reshape(T, T)
                return c
            lax.fori_loop(0, nb, drop_body, 0)

        return thr

    thr = lax.cond((i + 1) * T > k_sel, select_threshold,
                   lambda: jnp.full((SUBLANES, LANES), -F32_MAX, F32))

    def mask_body(j, c):
        r = pl.ds(pl.multiple_of(j * T, T), T)
        s = score_ref[j].reshape(G, SUBLANES, LANES)
        lhsd_ref[r, KV_RANK:2 * KV_RANK] = (
            jnp.where(s >= thr[None], 0.0, MASKED).reshape(T, T).astype(BF16))
        return c

    lax.fori_loop(0, nb, mask_body, 0)

    def reset_state():
        m_ref[...] = jnp.full(m_ref.shape, NEG_BIG, F32)
        acc_ref[...] = jnp.zeros(acc_ref.shape, F32)

    def online_update(p, s, pv_lhs):
        cols = slice(p * 2 * T, (p + 1) * 2 * T)
        m_old = m_ref[:, cols]
        m_new = jnp.maximum(m_old, jnp.max(s, axis=0, keepdims=True))
        alpha = jnp.exp2(m_old - m_new)
        pr = jnp.exp2(s - m_new[0:1, :])
        acc_ref[:, cols] = acc_ref[:, cols] * alpha[0:1, :] + _dot(pv_lhs, pr.astype(BF16))
        m_ref[:, cols] = m_new

    def key_rows(j, nblocks):
        return pl.ds(pl.multiple_of(j * T, T), nblocks * T)

    def blocks_t(load, j, nblocks):
        return load(j) if nblocks == 1 else jnp.concatenate([load(j), load(j + 1)], axis=1)

    def for_far_blocks(n_far, step):
        def pair_body(jj, c):
            step(2 * jj, 2)
            return c
        lax.fori_loop(0, n_far // 2, pair_body, 0)

        @pl.when(n_far % 2 == 1)
        def _odd():
            step(n_far - 1, 1)

    def dsa_step(j, nblocks, bias_kind=None):
        lhs = lhsd_ref[key_rows(j, nblocks), :]
        pv_lhs = blocks_t(lambda jb: ct_ref[jb], j, nblocks)
        for p in range(NP):
            s = _dot_nt(lhs, qsel_ref[p])
            if bias_kind is not None:
                s = s + bnear_ref[bias_kind, :, p * 2 * T:(p + 1) * 2 * T]
            online_update(p, s, pv_lhs)

    reset_state()
    for_far_blocks(jnp.maximum(i - 1, 0), dsa_step)
    m_ref[...] = m_ref[...] + bfar_ref[...]

    @pl.when(i >= 1)
    def _dsa_prev():
        dsa_step(i - 1, 1, 0)

    dsa_step(i, 1, 1)

    for p in range(NP):
        o_pair = jnp.zeros((T, LANES), F32)
        for e in range(2):
            cols = slice(p * 2 * T + e * T, p * 2 * T + (e + 1) * T)
            o_t = acc_ref[0:KV_RANK, cols] / acc_ref[KV_RANK:KV_RANK + 1, cols]
            o_pair = o_pair + _dot(o_t.T.astype(BF16), wuv_ref[2 * p + e])
        o_ref[:, p * LANES:(p + 1) * LANES] = o_pair.astype(o_ref.dtype)

    causal2 = jnp.concatenate([key_i <= qry_i, key_i <= qry_i], axis=1)

    def fox_step(j, nblocks, diag=False):
        for p in range(NP):
            s = _dot_nt(kaug_ref[p, key_rows(j, nblocks), :], qaug_ref[p])
            if diag:
                s = jnp.where(causal2, s, MASKED)
            online_update(p, s, blocks_t(lambda jb: vt_ref[p, jb], j, nblocks))

    reset_state()
    for_far_blocks(i, fox_step)
    fox_step(i, 1, True)

    row_lo = lax.broadcasted_iota(jnp.int32, (LANES, T), 0) < HEAD_DIM
    for p in range(NP):
        c0 = slice(p * 2 * T, p * 2 * T + T)
        c1 = slice(p * 2 * T + T, (p + 1) * 2 * T)
        o_t = jnp.where(row_lo, acc_ref[0:LANES, c0] / acc_ref[LANES:LANES + 1, c0],
                        acc_ref[0:LANES, c1] / acc_ref[LANES:LANES + 1, c1])
        o_ref[:, MIX_HALF + p * LANES:MIX_HALF + (p + 1) * LANES] = o_t.T.astype(o_ref.dtype)


_EV_QA, _EV_QI, _EV_QB, _EV_KB, _EV_VB = 0, 1, 2, 3, 4
_EV_CKV, _EV_KIDX = 20, 21


def _attention(hb, w_t, f_rows, wuk_pad, wuv_pad, bias_near, bias_far, place, k_sel):
    bsz, seq, _ = hb.shape
    T = BLK
    nblk = seq // T
    H = N_HEADS
    pq, cq, pk, ck = place
    wide = lambda c: pl.BlockSpec((None, T, MIX_HALF), lambda b, i, c=c: (b, i, c))
    full_wide = lambda c: pl.BlockSpec((None, seq, MIX_HALF), lambda b, i, c=c: (b, 0, c))
    full_narrow = lambda c: pl.BlockSpec((None, seq, LANES), lambda b, i, c=c: (b, 0, c))
    const = lambda a: pl.BlockSpec(a.shape, lambda b, i, nd=a.ndim: (0,) * nd)
    return pl.pallas_call(
        functools.partial(_attn_kernel, k_sel=k_sel),
        grid=(bsz, nblk),
        in_specs=[wide(_EV_QA), wide(_EV_QI), wide(_EV_QB),
                  pl.BlockSpec((None, H, T), lambda b, i: (b, 0, i)),
                  pl.BlockSpec((None, T, LANES), lambda b, i: (b, i, 0)),
                  full_wide(_EV_KB), full_wide(_EV_VB),
                  full_narrow(_EV_CKV), full_narrow(_EV_KIDX),
                  pl.BlockSpec((None, seq, LANES), lambda b, i: (b, 0, 0)),
                  const(wuk_pad), const(wuv_pad), const(bias_near), const(bias_far),
                  const(pq), const(cq), const(pk), const(ck)],
        out_specs=pl.BlockSpec((None, T, 2 * MIX_HALF), lambda b, i: (b, i, 0)),
        out_shape=jax.ShapeDtypeStruct((bsz, seq, 2 * MIX_HALF), BF16),
        scratch_shapes=[pltpu.VMEM((nblk, T, T), F32),
                        pltpu.VMEM((seq, 2 * KV_RANK), BF16),
                        pltpu.VMEM((nblk, PV_ROWS, T), BF16),
                        pltpu.VMEM((H // 2, seq, 2 * LANES), BF16),
                        pltpu.VMEM((H // 2, nblk, PV_ROWS, T), BF16),
                        pltpu.VMEM((H // 2, 2 * T, 2 * KV_RANK), BF16),
                        pltpu.VMEM((H // 2, 2 * T, 2 * LANES), BF16),
                        pltpu.VMEM((H // 2, 2 * T, LANES), BF16),
                        pltpu.VMEM((SUBLANES, H * T), F32),
                        pltpu.VMEM((PV_ROWS, H * T), F32)],
        compiler_params=_cparams(("parallel", "arbitrary")),
        name="dsa_fox_attention",
    )(hb, hb, hb, w_t, f_rows, hb, hb, hb, hb, f_rows, wuk_pad, wuv_pad, bias_near, bias_far,
      pq, cq, pk, ck)


def _proj_ln_kernel(a_ref, w_ref, x_ref, g_ref, b_ref, xo_ref, xb_ref, *, alpha):
    y = alpha * x_ref[...] + _dot(a_ref[...], w_ref[...])
    out = _layer_norm_rows(y, g_ref[...], b_ref[...])
    xo_ref[...] = out
    xb_ref[...] = out.astype(BF16)


def _proj_ln(a, w, x, g, b, alpha, tm):
    n, k = a.shape
    d = w.shape[1]
    return pl.pallas_call(
        functools.partial(_proj_ln_kernel, alpha=alpha),
        grid=(n // tm,),
        in_specs=[pl.BlockSpec((tm, k), lambda i: (i, 0)),
                  pl.BlockSpec((k, d), lambda i: (0, 0)),
                  pl.BlockSpec((tm, d), lambda i: (i, 0)),
                  pl.BlockSpec((1, d), lambda i: (0, 0)),
                  pl.BlockSpec((1, d), lambda i: (0, 0))],
        out_specs=[pl.BlockSpec((tm, d), lambda i: (i, 0)),
                   pl.BlockSpec((tm, d), lambda i: (i, 0))],
        out_shape=[jax.ShapeDtypeStruct((n, d), F32),
                   jax.ShapeDtypeStruct((n, d), BF16)],
        compiler_params=_cparams(("parallel",)),
        name="proj_residual_ln",
    )(a, w, x, g, b)


def _causal_conv3(h, tail_ref, cw, first_tile):
    tm = h.shape[0]
    w0, w1, w2 = cw[0:1, :], cw[1:2, :], cw[2:3, :]
    out = w0 * pltpu.roll(h, 2, axis=0) + w1 * pltpu.roll(h, 1, axis=0) + w2 * h
    tail = jnp.where(first_tile, 0.0, tail_ref[...])
    head = h[0:SUBLANES, :]
    r = lax.broadcasted_iota(jnp.int32, head.shape, 0)
    hm1 = jnp.where(r < 1, pltpu.roll(tail, 1, axis=0), pltpu.roll(head, 1, axis=0))
    hm2 = jnp.where(r < 2, pltpu.roll(tail, 2, axis=0), pltpu.roll(head, 2, axis=0))
    out_head = w0 * hm2 + w1 * hm1 + w2 * head
    tail_ref[...] = h[tm - SUBLANES:tm, :]
    return out, out_head


def _ffn_up_kernel(x_ref, wg_ref, wv_ref, cg_ref, cv_ref, o_ref, tg_ref, tv_ref, *, tiles_per_seq):
    first = (pl.program_id(1) % tiles_per_seq) == 0
    x = x_ref[...]
    g, g_head = _causal_conv3(_dot(x, wg_ref[...]), tg_ref, cg_ref[...], first)
    v, v_head = _causal_conv3(_dot(x, wv_ref[...]), tv_ref, cv_ref[...], first)

    def act(a, b):
        return (a / (1.0 + jnp.exp(-a)) * b).astype(o_ref.dtype)

    o_ref[...] = act(g, v)
    o_ref[0:SUBLANES, :] = act(g_head, v_head)


def _ffn_up(xb, w_up, conv_w, seq, tm, tn):
    n, k = xb.shape
    dff = w_up.shape[1] // 2
    ncol = dff // tn
    return pl.pallas_call(
        functools.partial(_ffn_up_kernel, tiles_per_seq=seq // tm),
        grid=(ncol, n // tm),
        in_specs=[pl.BlockSpec((tm, k), lambda j, i: (i, 0)),
                  pl.BlockSpec((k, tn), lambda j, i: (0, j)),
                  pl.BlockSpec((k, tn), lambda j, i, ncol=ncol: (0, j + ncol)),
                  pl.BlockSpec((CONV_W, tn), lambda j, i: (0, j)),
                  pl.BlockSpec((CONV_W, tn), lambda j, i, ncol=ncol: (0, j + ncol))],
        out_specs=pl.BlockSpec((tm, tn), lambda j, i: (i, j)),
        out_shape=jax.ShapeDtypeStruct((n, dff), BF16),
        scratch_shapes=[pltpu.VMEM((SUBLANES, tn), F32), pltpu.VMEM((SUBLANES, tn), F32)],
        compiler_params=_cparams(("arbitrary", "arbitrary")),
        name="ffn_up_conv_gate",
    )(xb, w_up, w_up, conv_w, conv_w)


def _gelu_tanh(x):
    return 0.5 * x * (1.0 + jnp.tanh(math.sqrt(2.0 / math.pi) * (x + 0.044715 * (x * x * x))))


def _group_mean(v, avg):
    hi = v.astype(BF16)
    lo = (v - hi.astype(F32)).astype(BF16)
    return _dot(hi, avg) + _dot(lo, avg)


def _odd_mixer_kernel(x_ref, w_ref, sg_ref, sw_ref, sb_ref, cw_ref, o_ref, tail_ref, *, tiles_per_seq):
    tm = x_ref.shape[0]
    W = MIX_HALF
    first = (pl.program_id(0) % tiles_per_seq) == 0
    x = x_ref[...]
    lane = lax.broadcasted_iota(jnp.int32, (CHUNK, LANES), 1)
    lo_half = lane < HEAD_DIM
    r = lax.broadcasted_iota(jnp.int32, (W, W), 0) // HEAD_DIM
    c = lax.broadcasted_iota(jnp.int32, (W, W), 1) // HEAD_DIM
    avg = jnp.where(r == c, 1.0 / HEAD_DIM, 0.0).astype(BF16)
    tri = (lax.broadcasted_iota(jnp.int32, (CHUNK, CHUNK), 1)
           <= lax.broadcasted_iota(jnp.int32, (CHUNK, CHUNK), 0))

    u = _gelu_tanh(_dot(x, w_ref[:, 0:W]))
    v = _gelu_tanh(_dot(x, w_ref[:, W:2 * W]))
    vc = v - _group_mean(v, avg)
    var = _group_mean(vc * vc, avg)
    vn = (vc * lax.rsqrt(var + LN_EPS) * sg_ref[...]).astype(BF16)
    for n in range(tm // CHUNK):
        rows = slice(n * CHUNK, (n + 1) * CHUNK)
        for p in range(N_HEADS // 2):
            cols = slice(p * LANES, (p + 1) * LANES)
            vp = vn[rows, cols]
            w_e = jnp.where(tri, sw_ref[2 * p], 0.0).astype(BF16)
            w_o = jnp.where(tri, sw_ref[2 * p + 1], 0.0).astype(BF16)
            mix = (_dot(w_e, jnp.where(lo_half, vp, 0)) + _dot(w_o, jnp.where(lo_half, 0, vp))
                   + sb_ref[:, cols])
            o_ref[rows, cols] = (u[rows, cols] * mix).astype(o_ref.dtype)

    g_b = _dot(x, w_ref[:, 2 * W:3 * W])
    y = _dot(x, w_ref[:, 3 * W:4 * W]) * _dot(x, w_ref[:, 4 * W:5 * W])
    conv, conv_head = _causal_conv3(y, tail_ref, cw_ref[...], first)
    o_ref[:, W:2 * W] = (g_b * conv).astype(o_ref.dtype)
    o_ref[0:SUBLANES, W:2 * W] = (g_b[0:SUBLANES, :] * conv_head).astype(o_ref.dtype)


def _odd_mixer(xb, w_in, sgu_g, sgu_w, sgu_b_tile, conv_w, seq, tm):
    n, k = xb.shape
    W = MIX_HALF
    return pl.pallas_call(
        functools.partial(_odd_mixer_kernel, tiles_per_seq=seq // tm),
        grid=(n // tm,),
        in_specs=[pl.BlockSpec((tm, k), lambda i: (i, 0)),
                  pl.BlockSpec((k, 5 * W), lambda i: (0, 0)),
                  pl.BlockSpec((1, W), lambda i: (0, 0)),
                  pl.BlockSpec((N_HEADS, CHUNK, CHUNK), lambda i: (0, 0, 0)),
                  pl.BlockSpec((CHUNK, W), lambda i: (0, 0)),
                  pl.BlockSpec((CONV_W, W), lambda i: (0, 0))],
        out_specs=pl.BlockSpec((tm, 2 * W), lambda i: (i, 0)),
        out_shape=jax.ShapeDtypeStruct((n, 2 * W), BF16),
        scratch_shapes=[pltpu.VMEM((SUBLANES, W), F32)],
        compiler_params=_cparams(("arbitrary",)),
        name="odd_mixer",
    )(xb, w_in, sgu_g, sgu_w, sgu_b_tile, conv_w)


def _t5_bucket_np(dist):
    max_exact = N_BUCKETS // 2
    n = np.maximum(dist, 0)
    nf = np.maximum(n, 1).astype(np.float32)
    large = max_exact + (np.log(nf / max_exact) / np.float32(math.log(T5_MAX_DIST / max_exact))
                         * (N_BUCKETS - max_exact)).astype(np.int32)
    large = np.minimum(large, N_BUCKETS - 1)
    return np.where(n < max_exact, n, large)


def _bias_tables(rel_bias):
    T = BLK
    s = np.arange(T)[:, None]
    t = np.arange(T)[None, :]
    bucket = _t5_bucket_np(np.stack([t + T - s, t - s]))
    far_bucket = _t5_bucket_np(np.array([T + 1]))[0]
    assert far_bucket == N_BUCKETS - 1
    assert (_t5_bucket_np(np.arange(T + 1, 64 * T)) == far_bucket).all()
    rb = rel_bias.astype(F32) * LOG2E
    near = jnp.swapaxes(rb[bucket], 2, 3).reshape(2, T, N_HEADS * T)
    far = jnp.repeat(rb[far_bucket], T)[None, :]
    return near, far


def _placement_constants():
    H, L = N_HEADS, LANES
    pq = np.zeros((3 * L, H * L), np.float32)
    cq = np.zeros((1, H * L), np.float32)
    pk = np.zeros((3 * L, (H // 2) * L), np.float32)
    ck = np.zeros((1, (H // 2) * L), np.float32)
    for h in range(H):
        p, e = divmod(h, 2)
        for piece in range(3):
            pq[piece * L + h, h * L + 6 + piece] = 1.0
            cq[0, h * L + 3 * e + piece] = 1.0
            pk[piece * L + h, p * L + 3 * e + piece] = -1.0
            ck[0, p * L + 6 + piece] = 1.0
    return jnp.asarray(pq, BF16), jnp.asarray(cq), jnp.asarray(pk, BF16), jnp.asarray(ck)


def _even_weights(w_in, w_uk, w_uv):
    hd, H = HEAD_DIM, N_HEADS
    widths = (H * hd, KV_RANK, H * hd, hd, H, H * hd, H * hd, H * hd, H)
    offs = np.cumsum((0,) + widths)
    seg = lambda s: w_in[:, offs[s]:offs[s + 1]]
    q_a, c_kv, q_idx, k_idx, w_idx, q_b, k_b, v_b, f_in = [seg(s) for s in range(9)]
    scale = hd ** -0.5
    w_big = jnp.concatenate([q_a, q_idx * scale, q_b * (scale * LOG2E), k_b, v_b, c_kv,
                             k_idx, k_idx], axis=1).astype(BF16)
    pad = jnp.zeros((w_in.shape[0], LANES - 2 * H), w_in.dtype)
    w_small = jnp.concatenate([w_idx, f_in, pad], axis=1).astype(BF16)
    wuk_t = jnp.swapaxes(w_uk, 1, 2) * (scale * LOG2E)
    wuk_pad = jnp.zeros((H, LANES, KV_RANK), F32)
    wuv_pad = jnp.zeros((H, KV_RANK, LANES), F32)
    for h in range(H):
        o = hd * (h % 2)
        wuk_pad = wuk_pad.at[h, o:o + hd, :].set(wuk_t[h])
        wuv_pad = wuv_pad.at[h, :, o:o + hd].set(w_uv[h])
    return w_big, w_small, wuk_pad.astype(BF16), wuv_pad.astype(BF16)


def kernel(x, ln_g, ln_b, rel_bias, ev_w_in, ev_w_uk, ev_w_uv, ev_b_f, ev_w_o,
           od_w_in, od_sgu_g, od_sgu_w, od_sgu_b, od_conv_w, od_w_o,
           ffn_w_up, ffn_conv_w, ffn_w_down):
    bsz, seq, d = x.shape
    depth = ln_g.shape[0]
    n = bsz * seq
    alpha = (2.0 * depth) ** 0.25
    k_sel = min(TOPK_MAX, seq // 4)
    H = N_HEADS
    assert seq % BLK == 0 and seq % CHUNK == 0
    tm = min(512, seq)
    tm_ffn = min(1024, seq)

    bias_near, bias_far = _bias_tables(rel_bias)
    place = _placement_constants()
    xf = x.reshape(n, d)
    xb = xf.astype(BF16)
    for layer in range(depth):
        j = layer // 2
        g0, b0 = ln_g[layer, 0][None, :], ln_b[layer, 0][None, :]
        g1, b1 = ln_g[layer, 1][None, :], ln_b[layer, 1][None, :]
        if layer % 2 == 0:
            w_big, w_small, wuk_pad, wuv_pad = _even_weights(ev_w_in[j], ev_w_uk[j], ev_w_uv[j])
            hb, hs = _even_proj(xb, w_big, w_small, tm)
            hb = hb.reshape(bsz, seq, -1)
            hs_t = jnp.swapaxes(hs.reshape(bsz, seq, -1)[:, :, 0:2 * H], 1, 2)
            cum_f = _cumf(hs_t[:, H:2 * H], ev_b_f[j][:, None])
            f_rows = jnp.pad(jnp.swapaxes(cum_f, 1, 2), ((0, 0), (0, 0), (0, LANES - H)))
            mix = _attention(hb, hs_t[:, 0:H], f_rows, wuk_pad, wuv_pad, bias_near, bias_far,
                             place, k_sel)
            w_o = ev_w_o[j]
        else:
            sgu_b_tile = jnp.repeat(jnp.swapaxes(od_sgu_b[j], 0, 1), HEAD_DIM, axis=1)
            mix = _odd_mixer(xb, od_w_in[j].astype(BF16), od_sgu_g[j].reshape(1, -1), od_sgu_w[j],
                             sgu_b_tile, od_conv_w[j], seq, min(256, seq))
            w_o = od_w_o[j]
        xf, xb = _proj_ln(mix.reshape(n, d), w_o.astype(BF16), xf, g0, b0, alpha, tm)
        act = _ffn_up(xb, ffn_w_up[layer].astype(BF16), ffn_conv_w[layer], seq, tm_ffn, 256)
        xf, xb = _proj_ln(act, ffn_w_down[layer].astype(BF16), xf, g1, b1, alpha, tm)
    return xf.reshape(bsz, seq, d)
```

```python
import functools
import math

import numpy as np
import jax
import jax.numpy as jnp
from jax import lax
from jax.experimental import pallas as pl
from jax.experimental.pallas import tpu as pltpu

F32 = jnp.float32
BF16 = jnp.bfloat16

HEAD_DIM = 64
N_HEADS = 8
KV_RANK = 128
TOPK_MAX = 256
N_BUCKETS = 32
T5_MAX_DIST = 128
CHUNK = 128
CONV_W = 3
LN_EPS = 1e-5
MIX_HALF = N_HEADS * HEAD_DIM

LANES = 128
SUBLANES = 8
BF16_ROWS = 16
VMEM_LIMIT = 56 * 1024 * 1024

BLK = 128
PV_ROWS = KV_RANK + BF16_ROWS
NEG_BIG = -1e30
MASKED = -(2.0 ** 100)
F32_MAX = float(np.finfo(np.float32).max)
LOG2E = math.log2(math.e)


def _cparams(sem):
    return pltpu.CompilerParams(dimension_semantics=sem, vmem_limit_bytes=VMEM_LIMIT)


def _dot(a, b):
    return jnp.dot(a, b, preferred_element_type=F32)


def _dot_nt(a, b):
    return lax.dot_general(a, b, (((1,), (1,)), ((), ())), preferred_element_type=F32)


def _layer_norm_rows(y, g, b):
    mu = jnp.mean(y, axis=-1, keepdims=True)
    yc = y - mu
    var = jnp.mean(yc * yc, axis=-1, keepdims=True)
    return yc * lax.rsqrt(var + LN_EPS) * g + b


def _tree(op, x):
    while x.shape[0] > 1:
        half = x.shape[0] // 2
        x = op(x[:half], x[half:])
    return x[0]


def _bf16_pieces(v):
    hi = v.astype(BF16)
    r1 = v - hi.astype(F32)
    mid = r1.astype(BF16)
    lo = (r1 - mid.astype(F32)).astype(BF16)
    return jnp.concatenate([hi, mid, lo], axis=1)


def _even_proj_kernel(x_ref, wb_ref, ws_ref, hb_ref, hs_ref):
    x = x_ref[...]
    hb_ref[...] = _dot(x, wb_ref[...]).astype(hb_ref.dtype)
    hs_ref[...] = _dot(x, ws_ref[...])


def _even_proj(xb, w_big, w_small, tm):
    n, k = xb.shape
    mb = w_big.shape[1]
    ms = w_small.shape[1]
    return pl.pallas_call(
        _even_proj_kernel,
        grid=(n // tm,),
        in_specs=[pl.BlockSpec((tm, k), lambda i: (i, 0)),
                  pl.BlockSpec((k, mb), lambda i: (0, 0)),
                  pl.BlockSpec((k, ms), lambda i: (0, 0))],
        out_specs=[pl.BlockSpec((tm, mb), lambda i: (i, 0)),
                   pl.BlockSpec((tm, ms), lambda i: (i, 0))],
        out_shape=[jax.ShapeDtypeStruct((n, mb), BF16),
                   jax.ShapeDtypeStruct((n, ms), F32)],
        compiler_params=_cparams(("parallel",)),
        name="even_proj",
    )(xb, w_big, w_small)


def _cumf_kernel(f_ref, b_ref, o_ref):
    seq = f_ref.shape[-1]
    row = lax.broadcasted_iota(jnp.int32, (LANES, LANES), 0)
    col = lax.broadcasted_iota(jnp.int32, (LANES, LANES), 1)
    upper = jnp.where(row <= col, 1.0, 0.0).astype(F32)
    carry = jnp.zeros((f_ref.shape[0], 1), F32)
    for c in range(seq // LANES):
        z = f_ref[:, c * LANES:(c + 1) * LANES] + b_ref[...]
        lf = jnp.minimum(z, 0.0) - jnp.log1p(jnp.exp(-jnp.abs(z)))
        cs = jnp.dot(lf, upper, preferred_element_type=F32,
                     precision=lax.Precision.HIGHEST) + carry
        o_ref[:, c * LANES:(c + 1) * LANES] = cs
        carry = cs[:, LANES - 1:LANES]


def _cumf(f_t, b_f):
    bsz, nh, seq = f_t.shape
    return pl.pallas_call(
        _cumf_kernel,
        grid=(bsz,),
        in_specs=[pl.BlockSpec((None, nh, seq), lambda b: (b, 0, 0)),
                  pl.BlockSpec((nh, 1), lambda b: (0, 0))],
        out_specs=pl.BlockSpec((None, nh, seq), lambda b: (b, 0, 0)),
        out_shape=jax.ShapeDtypeStruct((bsz, nh, seq), F32),
        compiler_params=_cparams(("parallel",)),
        name="fox_cumf",
    )(f_t, b_f)


def _attn_kernel(qa_ref, qi_ref, qb_ref, wt_ref, fq_ref, kb_ref, vb_ref, ckv_ref, kidx_ref, fk_ref,
                 wuk_ref, wuv_ref, bnear_ref, bfar_ref, pq_ref, cq_ref, pk_ref, ck_ref,
                 o_ref,
                 score_ref, lhsd_ref, ct_ref, kaug_ref, vt_ref, qsel_ref, qaug_ref, qim_ref,
                 m_ref, acc_ref, *, k_sel):
    T = BLK
    H = N_HEADS
    NP = H // 2
    G = T // SUBLANES
    i = pl.program_id(1)
    nb = i + 1
    nblk = ct_ref.shape[0]
    lane = lax.broadcasted_iota(jnp.int32, (T, LANES), 1)
    lo_half = lane < HEAD_DIM
    key_i = lax.broadcasted_iota(jnp.int32, (T, T), 0)
    qry_i = lax.broadcasted_iota(jnp.int32, (T, T), 1)
    ones_rows = jnp.ones((BF16_ROWS, T), BF16)

    @pl.when(i == 0)
    def _per_batch():
        def prep(j, c):
            r = pl.ds(pl.multiple_of(j * T, T), T)
            cb = ckv_ref[r, :]
            lhsd_ref[r, 0:KV_RANK] = cb
            ct_ref[j, 0:KV_RANK, :] = cb.astype(F32).T.astype(BF16)
            ct_ref[j, KV_RANK:PV_ROWS, :] = ones_rows
            ext = (_dot(_bf16_pieces(fk_ref[r, :] * LOG2E), pk_ref[...]) + ck_ref[...]).astype(BF16)
            for p in range(NP):
                cols = slice(p * LANES, (p + 1) * LANES)
                kaug_ref[p, r, 0:LANES] = kb_ref[r, cols]
                kaug_ref[p, r, LANES:2 * LANES] = ext[:, cols]
                vt_ref[p, j, 0:LANES, :] = vb_ref[r, cols].astype(F32).T.astype(BF16)
                vt_ref[p, j, LANES:PV_ROWS, :] = ones_rows
            return c
        lax.fori_loop(0, nblk, prep, 0)

    w_rows = wt_ref[...] * (H ** -0.5)
    ext_q = (_dot(_bf16_pieces(fq_ref[...] * LOG2E), pq_ref[...]) + cq_ref[...]).astype(BF16)
    eye = jnp.where(key_i == qry_i, 1.0, 0.0).astype(BF16)
    for p in range(NP):
        cols = slice(p * LANES, (p + 1) * LANES)
        qa_pair, qi_pair, qb_pair = qa_ref[:, cols], qi_ref[:, cols], qb_ref[:, cols]
        for e in range(2):
            h = 2 * p + e
            rows = slice(e * T, (e + 1) * T)
            mine = lo_half if e == 0 else jnp.logical_not(lo_half)
            qsel_ref[p, rows, 0:KV_RANK] = _dot(qa_pair, wuk_ref[h]).astype(BF16)
            qsel_ref[p, rows, KV_RANK:2 * KV_RANK] = eye
            qaug_ref[p, rows, 0:LANES] = jnp.where(mine, qb_pair, 0)
            qaug_ref[p, rows, LANES:2 * LANES] = ext_q[:, h * LANES:(h + 1) * LANES]
            qim_ref[p, rows, :] = jnp.where(mine, qi_pair, 0)

    def reset_state():
        m_ref[...] = jnp.full(m_ref.shape, NEG_BIG, F32)
        acc_ref[...] = jnp.zeros(acc_ref.shape, F32)

    def online_update(p, s, pv_lhs):
        cols = slice(p * 2 * T, (p + 1) * 2 * T)
        m_old = m_ref[:, cols]
        s_max = _tree(jnp.maximum, s.reshape(-1, SUBLANES, 2 * T))
        m_new = jnp.maximum(m_old, jnp.max(s_max, axis=0, keepdims=True))
        alpha = jnp.exp2(m_old - m_new)
        pr = jnp.exp2(s - m_new[0:1, :])
        acc_ref[:, cols] = acc_ref[:, cols] * alpha[0:1, :] + _dot(pv_lhs, pr.astype(BF16))
        m_ref[:, cols] = m_new

    def key_rows(j, nblocks):
        return pl.ds(pl.multiple_of(j * T, T), nblocks * T)

    def blocks_t(load, j, nblocks):
        return load(j) if nblocks == 1 else jnp.concatenate([load(j + b) for b in range(nblocks)], axis=1)

    def for_far_blocks(n_far, step):
        n_quad = lax.shift_right_logical(n_far, 2)

        def quad_body(jj, c):
            step(4 * jj, 4)
            return c
        lax.fori_loop(0, n_quad, quad_body, 0)

        @pl.when((n_far & 2) != 0)
        def _pair():
            step(4 * n_quad, 2)

        @pl.when((n_far & 1) != 0)
        def _single():
            step(n_far - 1, 1)

    def idx_scores(j, nblocks):
        k_blk = kidx_ref[key_rows(j, nblocks), :]
        acc = jnp.zeros((nblocks * T, T), F32)
        for p in range(NP):
            s = _dot_nt(k_blk, qim_ref[p])
            for e in range(2):
                h = 2 * p + e
                acc = acc + w_rows[h:h + 1, :] * jnp.maximum(s[:, e * T:(e + 1) * T], 0.0)
        return acc

    causal2 = jnp.concatenate([key_i <= qry_i, key_i <= qry_i], axis=1)

    def fox_step(j, nblocks, diag=False):
        for p in range(NP):
            s = _dot_nt(kaug_ref[p, key_rows(j, nblocks), :], qaug_ref[p])
            if diag:
                s = jnp.where(causal2, s, MASKED)
            online_update(p, s, blocks_t(lambda jb: vt_ref[p, jb], j, nblocks))

    def idx_fox_step(j, nblocks):
        score_ref[pl.ds(j, nblocks)] = idx_scores(j, nblocks).reshape(nblocks, T, T)
        fox_step(j, nblocks)

    reset_state()
    for_far_blocks(i, idx_fox_step)
    score_ref[i] = jnp.where(key_i <= qry_i, idx_scores(i, 1), -jnp.inf)
    score_ref[i + 1] = jnp.full((T, T), -jnp.inf, F32)
    fox_step(i, 1, True)

    row_lo = lax.broadcasted_iota(jnp.int32, (LANES, T), 0) < HEAD_DIM
    for p in range(NP):
        c0 = slice(p * 2 * T, p * 2 * T + T)
        c1 = slice(p * 2 * T + T, (p + 1) * 2 * T)
        o_t = jnp.where(row_lo, acc_ref[0:LANES, c0] / acc_ref[LANES:LANES + 1, c0],
                        acc_ref[0:LANES, c1] / acc_ref[LANES:LANES + 1, c1])
        o_ref[:, MIX_HALF + p * LANES:MIX_HALF + (p + 1) * LANES] = o_t.T.astype(o_ref.dtype)

    kf = float(k_sel)
    t_pos = (i * T + lax.broadcasted_iota(jnp.int32, (SUBLANES, LANES), 1)).astype(F32)
    searching = (t_pos + 1.0) > kf
    G2 = 2 * G
    n_pair = i // 2 + 1
    key_in_pair = (lax.broadcasted_iota(jnp.int32, (G2, SUBLANES, LANES), 0) * SUBLANES
                   + lax.broadcasted_iota(jnp.int32, (G2, SUBLANES, LANES), 1))

    def rep(v):
        return jnp.broadcast_to(v, (SUBLANES, LANES))

    def score_pair(jj):
        return score_ref[pl.ds(2 * jj, 2)].reshape(G2, SUBLANES, LANES)

    def count(*pred_fns):
        def body(jj, cs):
            s = score_pair(jj)
            return tuple(c + _tree(jnp.add, jnp.where(f(s, jj), 1.0, 0.0)) for c, f in zip(cs, pred_fns))
        zero = jnp.zeros((SUBLANES, LANES), F32)
        cs = lax.fori_loop(0, n_pair, body, (zero,) * len(pred_fns))
        return tuple(rep(jnp.sum(c, axis=0, keepdims=True)) for c in cs)

    def count_ge(thr):
        return count(lambda s, jj: s >= thr[None])[0]

    def select_threshold():
        def mm_body(jj, carry):
            mn, mx = carry
            s = score_pair(jj)
            return (jnp.minimum(mn, _tree(jnp.minimum, jnp.where(s == -jnp.inf, jnp.inf, s))),
                    jnp.maximum(mx, _tree(jnp.maximum, s)))

        mn, mx = lax.fori_loop(0, n_pair, mm_body,
                               (jnp.full((SUBLANES, LANES), jnp.inf, F32),
                                jnp.full((SUBLANES, LANES), -jnp.inf, F32)))
        mn = rep(jnp.min(mn, axis=0, keepdims=True))
        mx = rep(jnp.max(mx, axis=0, keepdims=True))
        above_mx = mx + jnp.maximum(jnp.abs(mx) * 2.0 ** -20, 1e-30)
        c_ge0, c_gt0 = count(lambda s, jj: s >= 0.0, lambda s, jj: s > 0.0)
        thr_ge0 = c_ge0 >= kf
        thr_gt0 = c_gt0 >= kf
        lo0 = jnp.where(thr_ge0, 0.0, mn)
        cnt_lo0 = jnp.where(thr_ge0, c_ge0, t_pos + 1.0)
        hi0 = jnp.where(thr_gt0, above_mx, 0.0)
        cnt_hi0 = jnp.where(thr_gt0, 0.0, jnp.where(thr_ge0, c_gt0, c_ge0))
        undecided = thr_gt0 | jnp.logical_not(thr_ge0)
        active0 = jnp.where(searching & undecided & (cnt_lo0 > kf), 1.0, 0.0)

        def cond(st):
            return jnp.logical_and(st[5] > 0.0, st[6] < 2200)

        def body(st):
            lo, hi, cnt_lo, cnt_hi, active, _, it = st
            any_active = jnp.max(active)
            mid = 0.5 * lo + 0.5 * hi
            adjacent = (mid <= lo) | (mid >= hi)
            c = count_ge(mid)
            act = active > 0.0
            go_lo = act & jnp.logical_not(adjacent) & (c >= kf)
            go_hi = act & jnp.logical_not(adjacent) & (c < kf)
            lo = jnp.where(go_lo, mid, lo)
            cnt_lo = jnp.where(go_lo, c, cnt_lo)
            hi = jnp.where(go_hi, mid, hi)
            cnt_hi = jnp.where(go_hi, c, cnt_hi)
            active = jnp.where(act & jnp.logical_not(adjacent) & (cnt_lo > kf), 1.0, 0.0)
            return lo, hi, cnt_lo, cnt_hi, active, any_active, it + 1

        lo, hi, cnt_lo, cnt_hi, _, _, _ = lax.while_loop(
            cond, body, (lo0, hi0, cnt_lo0, cnt_hi0, active0, jnp.float32(1.0), 0))
        thr = jnp.where(searching, lo, -F32_MAX)

        tied = searching & (cnt_lo > kf)
        need = kf - cnt_hi

        @pl.when(jnp.max(jnp.where(tied, 1.0, 0.0)) > 0.0)
        def _break_ties():
            def key_index(jj):
                return (2 * jj * T + key_in_pair).astype(F32)

            def count_tied_upto(jcut):
                return count(lambda s, jj: (s == thr[None]) & (key_index(jj) <= jcut[None]))[0]

            def bs_body(_, st):
                jl, jh = st
                jm = jnp.floor(0.5 * (jl + jh))
                ok = count_tied_upto(jm) >= need
                return jnp.where(ok, jl, jm), jnp.where(ok, jm, jh)

            n_keys = (nb * T).astype(F32)
            jl0 = jnp.full((SUBLANES, LANES), -1.0, F32)
            jh0 = jnp.zeros((SUBLANES, LANES), F32) + (n_keys - 1.0)
            n_steps = int(math.ceil(math.log2(nblk * T))) + 1
            _, jcut = lax.fori_loop(0, n_steps, bs_body, (jl0, jh0))

            def drop_body(jj, c):
                s = score_pair(jj)
                drop = tied[None] & (s == thr[None]) & (key_index(jj) > jcut[None])
                score_ref[pl.ds(2 * jj, 2)] = jnp.where(drop, -jnp.inf, s).reshape(2, T, T)
                return c
            lax.fori_loop(0, n_pair, drop_body, 0)

        return thr

    thr = lax.cond((i + 1) * T > k_sel, select_threshold,
                   lambda: jnp.full((SUBLANES, LANES), -F32_MAX, F32))

    def mask_body(jj, c):
        lhsd_ref[key_rows(2 * jj, 2), KV_RANK:2 * KV_RANK] = (
            jnp.where(score_pair(jj) >= thr[None], 0.0, MASKED).reshape(2 * T, T).astype(BF16))
        return c

    lax.fori_loop(0, n_pair, mask_body, 0)

    def dsa_step(j, nblocks, bias_kind=None):
        lhs = lhsd_ref[key_rows(j, nblocks), :]
        pv_lhs = blocks_t(lambda jb: ct_ref[jb], j, nblocks)
        for p in range(NP):
            s = _dot_nt(lhs, qsel_ref[p])
            if bias_kind is not None:
                s = s + bnear_ref[bias_kind, :, p * 2 * T:(p + 1) * 2 * T]
            online_update(p, s, pv_lhs)

    reset_state()
    for_far_blocks(jnp.maximum(i - 1, 0), dsa_step)
    m_ref[...] = m_ref[...] + bfar_ref[...]

    @pl.when(i >= 1)
    def _dsa_prev():
        dsa_step(i - 1, 1, 0)

    dsa_step(i, 1, 1)

    for p in range(NP):
        o_pair = jnp.zeros((T, LANES), F32)
        for e in range(2):
            cols = slice(p * 2 * T + e * T, p * 2 * T + (e + 1) * T)
            o_t = acc_ref[0:KV_RANK, cols] / acc_ref[KV_RANK:KV_RANK + 1, cols]
            o_pair = o_pair + _dot(o_t.T.astype(BF16), wuv_ref[2 * p + e])
        o_ref[:, p * LANES:(p + 1) * LANES] = o_pair.astype(o_ref.dtype)


_EV_QA, _EV_QI, _EV_QB, _EV_KB, _EV_VB = 0, 1, 2, 3, 4
_EV_CKV, _EV_KIDX = 20, 21


def _attention(hb, w_t, f_rows, wuk_pad, wuv_pad, bias_near, bias_far, place, k_sel):
    bsz, seq, _ = hb.shape
    T = BLK
    nblk = seq // T
    H = N_HEADS
    pq, cq, pk, ck = place
    wide = lambda c: pl.BlockSpec((None, T, MIX_HALF), lambda b, i, c=c: (b, i, c))
    full_wide = lambda c: pl.BlockSpec((None, seq, MIX_HALF), lambda b, i, c=c: (b, 0, c))
    full_narrow = lambda c: pl.BlockSpec((None, seq, LANES), lambda b, i, c=c: (b, 0, c))
    const = lambda a: pl.BlockSpec(a.shape, lambda b, i, nd=a.ndim: (0,) * nd)
    return pl.pallas_call(
        functools.partial(_attn_kernel, k_sel=k_sel),
        grid=(bsz, nblk),
        in_specs=[wide(_EV_QA), wide(_EV_QI), wide(_EV_QB),
                  pl.BlockSpec((None, H, T), lambda b, i: (b, 0, i)),
                  pl.BlockSpec((None, T, LANES), lambda b, i: (b, i, 0)),
                  full_wide(_EV_KB), full_wide(_EV_VB),
                  full_narrow(_EV_CKV), full_narrow(_EV_KIDX),
                  pl.BlockSpec((None, seq, LANES), lambda b, i: (b, 0, 0)),
                  const(wuk_pad), const(wuv_pad), const(bias_near), const(bias_far),
                  const(pq), const(cq), const(pk), const(ck)],
        out_specs=pl.BlockSpec((None, T, 2 * MIX_HALF), lambda b, i: (b, i, 0)),
        out_shape=jax.ShapeDtypeStruct((bsz, seq, 2 * MIX_HALF), BF16),
        scratch_shapes=[pltpu.VMEM((nblk + 2, T, T), F32),
                        pltpu.VMEM((seq + 2 * T, 2 * KV_RANK), BF16),
                        pltpu.VMEM((nblk, PV_ROWS, T), BF16),
                        pltpu.VMEM((H // 2, seq, 2 * LANES), BF16),
                        pltpu.VMEM((H // 2, nblk, PV_ROWS, T), BF16),
                        pltpu.VMEM((H // 2, 2 * T, 2 * KV_RANK), BF16),
                        pltpu.VMEM((H // 2, 2 * T, 2 * LANES), BF16),
                        pltpu.VMEM((H // 2, 2 * T, LANES), BF16),
                        pltpu.VMEM((SUBLANES, H * T), F32),
                        pltpu.VMEM((PV_ROWS, H * T), F32)],
        compiler_params=_cparams(("parallel", "arbitrary")),
        name="dsa_fox_attention",
    )(hb, hb, hb, w_t, f_rows, hb, hb, hb, hb, f_rows, wuk_pad, wuv_pad, bias_near, bias_far,
      pq, cq, pk, ck)


def _proj_ln_kernel(a_ref, w_ref, x_ref, g_ref, b_ref, xo_ref, xb_ref, *, alpha):
    y = alpha * x_ref[...] + _dot(a_ref[...], w_ref[...])
    out = _layer_norm_rows(y, g_ref[...], b_ref[...])
    xo_ref[...] = out
    xb_ref[...] = out.astype(BF16)


def _proj_ln(a, w, x, g, b, alpha, tm):
    n, k = a.shape
    d = w.shape[1]
    return pl.pallas_call(
        functools.partial(_proj_ln_kernel, alpha=alpha),
        grid=(n // tm,),
        in_specs=[pl.BlockSpec((tm, k), lambda i: (i, 0)),
                  pl.BlockSpec((k, d), lambda i: (0, 0)),
                  pl.BlockSpec((tm, d), lambda i: (i, 0)),
                  pl.BlockSpec((1, d), lambda i: (0, 0)),
                  pl.BlockSpec((1, d), lambda i: (0, 0))],
        out_specs=[pl.BlockSpec((tm, d), lambda i: (i, 0)),
                   pl.BlockSpec((tm, d), lambda i: (i, 0))],
        out_shape=[jax.ShapeDtypeStruct((n, d), F32),
                   jax.ShapeDtypeStruct((n, d), BF16)],
        compiler_params=_cparams(("parallel",)),
        name="proj_residual_ln",
    )(a, w, x, g, b)


def _causal_conv3(h, tail_ref, cw, first_tile):
    tm = h.shape[0]
    w0, w1, w2 = cw[0:1, :], cw[1:2, :], cw[2:3, :]
    out = w0 * pltpu.roll(h, 2, axis=0) + w1 * pltpu.roll(h, 1, axis=0) + w2 * h
    tail = jnp.where(first_tile, 0.0, tail_ref[...])
    head = h[0:SUBLANES, :]
    r = lax.broadcasted_iota(jnp.int32, head.shape, 0)
    hm1 = jnp.where(r < 1, pltpu.roll(tail, 1, axis=0), pltpu.roll(head, 1, axis=0))
    hm2 = jnp.where(r < 2, pltpu.roll(tail, 2, axis=0), pltpu.roll(head, 2, axis=0))
    out_head = w0 * hm2 + w1 * hm1 + w2 * head
    tail_ref[...] = h[tm - SUBLANES:tm, :]
    return out, out_head


def _ffn_up_kernel(x_ref, wg_ref, wv_ref, cg_ref, cv_ref, o_ref, tg_ref, tv_ref, *, tiles_per_seq):
    first = (pl.program_id(1) % tiles_per_seq) == 0
    x = x_ref[...]
    g, g_head = _causal_conv3(_dot(x, wg_ref[...]), tg_ref, cg_ref[...], first)
    v, v_head = _causal_conv3(_dot(x, wv_ref[...]), tv_ref, cv_ref[...], first)

    def act(a, b):
        return (a / (1.0 + jnp.exp(-a)) * b).astype(o_ref.dtype)

    o_ref[...] = act(g, v)
    o_ref[0:SUBLANES, :] = act(g_head, v_head)


def _ffn_up(xb, w_up, conv_w, seq, tm, tn):
    n, k = xb.shape
    dff = w_up.shape[1] // 2
    ncol = dff // tn
    return pl.pallas_call(
        functools.partial(_ffn_up_kernel, tiles_per_seq=seq // tm),
        grid=(ncol, n // tm),
        in_specs=[pl.BlockSpec((tm, k), lambda j, i: (i, 0)),
                  pl.BlockSpec((k, tn), lambda j, i: (0, j)),
                  pl.BlockSpec((k, tn), lambda j, i, ncol=ncol: (0, j + ncol)),
                  pl.BlockSpec((CONV_W, tn), lambda j, i: (0, j)),
                  pl.BlockSpec((CONV_W, tn), lambda j, i, ncol=ncol: (0, j + ncol))],
        out_specs=pl.BlockSpec((tm, tn), lambda j, i: (i, j)),
        out_shape=jax.ShapeDtypeStruct((n, dff), BF16),
        scratch_shapes=[pltpu.VMEM((SUBLANES, tn), F32), pltpu.VMEM((SUBLANES, tn), F32)],
        compiler_params=_cparams(("arbitrary", "arbitrary")),
        name="ffn_up_conv_gate",
    )(xb, w_up, w_up, conv_w, conv_w)


def _gelu_tanh(x):
    return 0.5 * x * (1.0 + jnp.tanh(math.sqrt(2.0 / math.pi) * (x + 0.044715 * (x * x * x))))


def _group_mean(v, avg):
    hi = v.astype(BF16)
    lo = (v - hi.astype(F32)).astype(BF16)
    return _dot(hi, avg) + _dot(lo, avg)


def _odd_mixer_kernel(x_ref, w_ref, sg_ref, sw_ref, sb_ref, cw_ref, o_ref, tail_ref, *, tiles_per_seq):
    tm = x_ref.shape[0]
    W = MIX_HALF
    first = (pl.program_id(0) % tiles_per_seq) == 0
    x = x_ref[...]
    lane = lax.broadcasted_iota(jnp.int32, (CHUNK, LANES), 1)
    lo_half = lane < HEAD_DIM
    r = lax.broadcasted_iota(jnp.int32, (W, W), 0) // HEAD_DIM
    c = lax.broadcasted_iota(jnp.int32, (W, W), 1) // HEAD_DIM
    avg = jnp.where(r == c, 1.0 / HEAD_DIM, 0.0).astype(BF16)
    tri = (lax.broadcasted_iota(jnp.int32, (CHUNK, CHUNK), 1)
           <= lax.broadcasted_iota(jnp.int32, (CHUNK, CHUNK), 0))

    u = _gelu_tanh(_dot(x, w_ref[:, 0:W]))
    v = _gelu_tanh(_dot(x, w_ref[:, W:2 * W]))
    vc = v - _group_mean(v, avg)
    var = _group_mean(vc * vc, avg)
    vn = (vc * lax.rsqrt(var + LN_EPS) * sg_ref[...]).astype(BF16)
    for n in range(tm // CHUNK):
        rows = slice(n * CHUNK, (n + 1) * CHUNK)
        for p in range(N_HEADS // 2):
            cols = slice(p * LANES, (p + 1) * LANES)
            vp = vn[rows, cols]
            w_e = jnp.where(tri, sw_ref[2 * p], 0.0).astype(BF16)
            w_o = jnp.where(tri, sw_ref[2 * p + 1], 0.0).astype(BF16)
            mix = (_dot(w_e, jnp.where(lo_half, vp, 0)) + _dot(w_o, jnp.where(lo_half, 0, vp))
                   + sb_ref[:, cols])
            o_ref[rows, cols] = (u[rows, cols] * mix).astype(o_ref.dtype)

    g_b = _dot(x, w_ref[:, 2 * W:3 * W])
    y = _dot(x, w_ref[:, 3 * W:4 * W]) * _dot(x, w_ref[:, 4 * W:5 * W])
    conv, conv_head = _causal_conv3(y, tail_ref, cw_ref[...], first)
    o_ref[:, W:2 * W] = (g_b * conv).astype(o_ref.dtype)
    o_ref[0:SUBLANES, W:2 * W] = (g_b[0:SUBLANES, :] * conv_head).astype(o_ref.dtype)


def _odd_mixer(xb, w_in, sgu_g, sgu_w, sgu_b_tile, conv_w, seq, tm):
    n, k = xb.shape
    W = MIX_HALF
    return pl.pallas_call(
        functools.partial(_odd_mixer_kernel, tiles_per_seq=seq // tm),
        grid=(n // tm,),
        in_specs=[pl.BlockSpec((tm, k), lambda i: (i, 0)),
                  pl.BlockSpec((k, 5 * W), lambda i: (0, 0)),
                  pl.BlockSpec((1, W), lambda i: (0, 0)),
                  pl.BlockSpec((N_HEADS, CHUNK, CHUNK), lambda i: (0, 0, 0)),
                  pl.BlockSpec((CHUNK, W), lambda i: (0, 0)),
                  pl.BlockSpec((CONV_W, W), lambda i: (0, 0))],
        out_specs=pl.BlockSpec((tm, 2 * W), lambda i: (i, 0)),
        out_shape=jax.ShapeDtypeStruct((n, 2 * W), BF16),
        scratch_shapes=[pltpu.VMEM((SUBLANES, W), F32)],
        compiler_params=_cparams(("arbitrary",)),
        name="odd_mixer",
    )(xb, w_in, sgu_g, sgu_w, sgu_b_tile, conv_w)


def _t5_bucket_np(dist):
    max_exact = N_BUCKETS // 2
    n = np.maximum(dist, 0)
    nf = np.maximum(n, 1).astype(np.float32)
    large = max_exact + (np.log(nf / max_exact) / np.float32(math.log(T5_MAX_DIST / max_exact))
                         * (N_BUCKETS - max_exact)).astype(np.int32)
    large = np.minimum(large, N_BUCKETS - 1)
    return np.where(n < max_exact, n, large)


def _bias_tables(rel_bias):
    T = BLK
    s = np.arange(T)[:, None]
    t = np.arange(T)[None, :]
    bucket = _t5_bucket_np(np.stack([t + T - s, t - s]))
    far_bucket = _t5_bucket_np(np.array([T + 1]))[0]
    assert far_bucket == N_BUCKETS - 1
    assert (_t5_bucket_np(np.arange(T + 1, 64 * T)) == far_bucket).all()
    rb = rel_bias.astype(F32) * LOG2E
    near = jnp.swapaxes(rb[bucket], 2, 3).reshape(2, T, N_HEADS * T)
    far = jnp.repeat(rb[far_bucket], T)[None, :]
    return near, far


def _placement_constants():
    H, L = N_HEADS, LANES
    pq = np.zeros((3 * L, H * L), np.float32)
    cq = np.zeros((1, H * L), np.float32)
    pk = np.zeros((3 * L, (H // 2) * L), np.float32)
    ck = np.zeros((1, (H // 2) * L), np.float32)
    for h in range(H):
        p, e = divmod(h, 2)
        for piece in range(3):
            pq[piece * L + h, h * L + 6 + piece] = 1.0
            cq[0, h * L + 3 * e + piece] = 1.0
            pk[piece * L + h, p * L + 3 * e + piece] = -1.0
            ck[0, p * L + 6 + piece] = 1.0
    return jnp.asarray(pq, BF16), jnp.asarray(cq), jnp.asarray(pk, BF16), jnp.asarray(ck)


def _even_weights(w_in, w_uk, w_uv):
    hd, H = HEAD_DIM, N_HEADS
    widths = (H * hd, KV_RANK, H * hd, hd, H, H * hd, H * hd, H * hd, H)
    offs = np.cumsum((0,) + widths)
    seg = lambda s: w_in[:, offs[s]:offs[s + 1]]
    q_a, c_kv, q_idx, k_idx, w_idx, q_b, k_b, v_b, f_in = [seg(s) for s in range(9)]
    scale = hd ** -0.5
    w_big = jnp.concatenate([q_a, q_idx * scale, q_b * (scale * LOG2E), k_b, v_b, c_kv,
                             k_idx, k_idx], axis=1).astype(BF16)
    pad = jnp.zeros((w_in.shape[0], LANES - 2 * H), w_in.dtype)
    w_small = jnp.concatenate([w_idx, f_in, pad], axis=1).astype(BF16)
    wuk_t = jnp.swapaxes(w_uk, 1, 2) * (scale * LOG2E)
    wuk_pad = jnp.zeros((H, LANES, KV_RANK), F32)
    wuv_pad = jnp.zeros((H, KV_RANK, LANES), F32)
    for h in range(H):
        o = hd * (h % 2)
        wuk_pad = wuk_pad.at[h, o:o + hd, :].set(wuk_t[h])
        wuv_pad = wuv_pad.at[h, :, o:o + hd].set(w_uv[h])
    return w_big, w_small, wuk_pad.astype(BF16), wuv_pad.astype(BF16)


def kernel(x, ln_g, ln_b, rel_bias, ev_w_in, ev_w_uk, ev_w_uv, ev_b_f, ev_w_o,
           od_w_in, od_sgu_g, od_sgu_w, od_sgu_b, od_conv_w, od_w_o,
           ffn_w_up, ffn_conv_w, ffn_w_down):
    bsz, seq, d = x.shape
    depth = ln_g.shape[0]
    n = bsz * seq
    alpha = (2.0 * depth) ** 0.25
    k_sel = min(TOPK_MAX, seq // 4)
    H = N_HEADS
    assert seq % BLK == 0 and seq % CHUNK == 0
    tm = min(512, seq)
    tm_ffn = min(1024, seq)

    bias_near, bias_far = _bias_tables(rel_bias)
    place = _placement_constants()
    xf = x.reshape(n, d)
    xb = xf.astype(BF16)
    for layer in range(depth):
        j = layer // 2
        g0, b0 = ln_g[layer, 0][None, :], ln_b[layer, 0][None, :]
        g1, b1 = ln_g[layer, 1][None, :], ln_b[layer, 1][None, :]
        if layer % 2 == 0:
            w_big, w_small, wuk_pad, wuv_pad = _even_weights(ev_w_in[j], ev_w_uk[j], ev_w_uv[j])
            hb, hs = _even_proj(xb, w_big, w_small, tm)
            hb = hb.reshape(bsz, seq, -1)
            hs_t = jnp.swapaxes(hs.reshape(bsz, seq, -1)[:, :, 0:2 * H], 1, 2)
            cum_f = _cumf(hs_t[:, H:2 * H], ev_b_f[j][:, None])
            f_rows = jnp.pad(jnp.swapaxes(cum_f, 1, 2), ((0, 0), (0, 0), (0, LANES - H)))
            mix = _attention(hb, hs_t[:, 0:H], f_rows, wuk_pad, wuv_pad, bias_near, bias_far,
                             place, k_sel)
            w_o = ev_w_o[j]
        else:
            sgu_b_tile = jnp.repeat(jnp.swapaxes(od_sgu_b[j], 0, 1), HEAD_DIM, axis=1)
            mix = _odd_mixer(xb, od_w_in[j].astype(BF16), od_sgu_g[j].reshape(1, -1), od_sgu_w[j],
                             sgu_b_tile, od_conv_w[j], seq, min(256, seq))
            w_o = od_w_o[j]
        xf, xb = _proj_ln(mix.reshape(n, d), w_o.astype(BF16), xf, g0, b0, alpha, tm)
        act = _ffn_up(xb, ffn_w_up[layer].astype(BF16), ffn_conv_w[layer], seq, tm_ffn, 256)
        xf, xb = _proj_ln(act, ffn_w_down[layer].astype(BF16), xf, g1, b1, alpha, tm)
    return xf.reshape(bsz, seq, d)
```

```python
import functools
import math

import numpy as np
import jax
import jax.numpy as jnp
from jax import lax
from jax.experimental import pallas as pl
from jax.experimental.pallas import tpu as pltpu

F32 = jnp.float32
BF16 = jnp.bfloat16

HEAD_DIM = 64
N_HEADS = 8
KV_RANK = 128
TOPK_MAX = 256
N_BUCKETS = 32
T5_MAX_DIST = 128
CHUNK = 128
CONV_W = 3
LN_EPS = 1e-5
MIX_HALF = N_HEADS * HEAD_DIM

LANES = 128
SUBLANES = 8
BF16_ROWS = 16
VMEM_LIMIT = 56 * 1024 * 1024

BLK = 128
PV_ROWS = KV_RANK + BF16_ROWS
NEG_BIG = -1e30
MASKED = -(2.0 ** 100)
F32_MAX = float(np.finfo(np.float32).max)
LOG2E = math.log2(math.e)


def _cparams(sem):
    return pltpu.CompilerParams(dimension_semantics=sem, vmem_limit_bytes=VMEM_LIMIT)


def _dot(a, b):
    return jnp.dot(a, b, preferred_element_type=F32)


def _dot_nt(a, b):
    return lax.dot_general(a, b, (((1,), (1,)), ((), ())), preferred_element_type=F32)


def _layer_norm_rows(y, g, b):
    mu = jnp.mean(y, axis=-1, keepdims=True)
    yc = y - mu
    var = jnp.mean(yc * yc, axis=-1, keepdims=True)
    return yc * lax.rsqrt(var + LN_EPS) * g + b


def _tree(op, x):
    while x.shape[0] > 1:
        half = x.shape[0] // 2
        x = op(x[:half], x[half:])
    return x[0]


def _bf16_pieces(v):
    hi = v.astype(BF16)
    r1 = v - hi.astype(F32)
    mid = r1.astype(BF16)
    lo = (r1 - mid.astype(F32)).astype(BF16)
    return jnp.concatenate([hi, mid, lo], axis=1)


def _even_proj_kernel(x_ref, wb_ref, ws_ref, hb_ref, hs_ref):
    x = x_ref[...]
    hb_ref[...] = _dot(x, wb_ref[...]).astype(hb_ref.dtype)
    hs_ref[...] = _dot(x, ws_ref[...])


def _even_proj(xb, w_big, w_small, tm):
    n, k = xb.shape
    mb = w_big.shape[1]
    ms = w_small.shape[1]
    return pl.pallas_call(
        _even_proj_kernel,
        grid=(n // tm,),
        in_specs=[pl.BlockSpec((tm, k), lambda i: (i, 0)),
                  pl.BlockSpec((k, mb), lambda i: (0, 0)),
                  pl.BlockSpec((k, ms), lambda i: (0, 0))],
        out_specs=[pl.BlockSpec((tm, mb), lambda i: (i, 0)),
                   pl.BlockSpec((tm, ms), lambda i: (i, 0))],
        out_shape=[jax.ShapeDtypeStruct((n, mb), BF16),
                   jax.ShapeDtypeStruct((n, ms), F32)],
        compiler_params=_cparams(("parallel",)),
        name="even_proj",
    )(xb, w_big, w_small)


def _cumf_kernel(f_ref, b_ref, o_ref):
    seq = f_ref.shape[-1]
    row = lax.broadcasted_iota(jnp.int32, (LANES, LANES), 0)
    col = lax.broadcasted_iota(jnp.int32, (LANES, LANES), 1)
    upper = jnp.where(row <= col, 1.0, 0.0).astype(F32)
    carry = jnp.zeros((f_ref.shape[0], 1), F32)
    for c in range(seq // LANES):
        z = f_ref[:, c * LANES:(c + 1) * LANES] + b_ref[...]
        lf = jnp.minimum(z, 0.0) - jnp.log1p(jnp.exp(-jnp.abs(z)))
        cs = jnp.dot(lf, upper, preferred_element_type=F32,
                     precision=lax.Precision.HIGHEST) + carry
        o_ref[:, c * LANES:(c + 1) * LANES] = cs
        carry = cs[:, LANES - 1:LANES]


def _cumf(f_t, b_f):
    bsz, nh, seq = f_t.shape
    return pl.pallas_call(
        _cumf_kernel,
        grid=(bsz,),
        in_specs=[pl.BlockSpec((None, nh, seq), lambda b: (b, 0, 0)),
                  pl.BlockSpec((nh, 1), lambda b: (0, 0))],
        out_specs=pl.BlockSpec((None, nh, seq), lambda b: (b, 0, 0)),
        out_shape=jax.ShapeDtypeStruct((bsz, nh, seq), F32),
        compiler_params=_cparams(("parallel",)),
        name="fox_cumf",
    )(f_t, b_f)


def _attn_kernel(qa_ref, qi_ref, qb_ref, wt_ref, fq_ref, kb_ref, vb_ref, ckv_ref, kidx_ref, fk_ref,
                 wuk_ref, wuv_ref, bnear_ref, bfar_ref, pq_ref, cq_ref, pk_ref, ck_ref,
                 o_ref,
                 score_ref, lhsd_ref, ct_ref, kaug_ref, vt_ref, qsel_ref, qaug_ref, qim_ref,
                 m_ref, acc_ref, *, k_sel):
    T = BLK
    H = N_HEADS
    NP = H // 2
    G = T // SUBLANES
    i = pl.program_id(1)
    nb = i + 1
    nblk = ct_ref.shape[0]
    lane = lax.broadcasted_iota(jnp.int32, (T, LANES), 1)
    lo_half = lane < HEAD_DIM
    key_i = lax.broadcasted_iota(jnp.int32, (T, T), 0)
    qry_i = lax.broadcasted_iota(jnp.int32, (T, T), 1)
    ones_rows = jnp.ones((BF16_ROWS, T), BF16)

    @pl.when(i == 0)
    def _per_batch():
        def prep(j, c):
            r = pl.ds(pl.multiple_of(j * T, T), T)
            cb = ckv_ref[r, :]
            lhsd_ref[r, 0:KV_RANK] = cb
            ct_ref[j, 0:KV_RANK, :] = cb.astype(F32).T.astype(BF16)
            ct_ref[j, KV_RANK:PV_ROWS, :] = ones_rows
            ext = (_dot(_bf16_pieces(fk_ref[r, :] * LOG2E), pk_ref[...]) + ck_ref[...]).astype(BF16)
            for p in range(NP):
                cols = slice(p * LANES, (p + 1) * LANES)
                kaug_ref[p, r, 0:LANES] = kb_ref[r, cols]
                kaug_ref[p, r, LANES:2 * LANES] = ext[:, cols]
                vt_ref[p, j, 0:LANES, :] = vb_ref[r, cols].astype(F32).T.astype(BF16)
                vt_ref[p, j, LANES:PV_ROWS, :] = ones_rows
            return c
        lax.fori_loop(0, nblk, prep, 0)

    w_rows = wt_ref[...] * (H ** -0.5)
    ext_q = (_dot(_bf16_pieces(fq_ref[...] * LOG2E), pq_ref[...]) + cq_ref[...]).astype(BF16)
    eye = jnp.where(key_i == qry_i, 1.0, 0.0).astype(BF16)
    for p in range(NP):
        cols = slice(p * LANES, (p + 1) * LANES)
        qa_pair, qi_pair, qb_pair = qa_ref[:, cols], qi_ref[:, cols], qb_ref[:, cols]
        for e in range(2):
            h = 2 * p + e
            rows = slice(e * T, (e + 1) * T)
            mine = lo_half if e == 0 else jnp.logical_not(lo_half)
            qsel_ref[p, rows, 0:KV_RANK] = _dot(qa_pair, wuk_ref[h]).astype(BF16)
            qsel_ref[p, rows, KV_RANK:2 * KV_RANK] = eye
            qaug_ref[p, rows, 0:LANES] = jnp.where(mine, qb_pair, 0)
            qaug_ref[p, rows, LANES:2 * LANES] = ext_q[:, h * LANES:(h + 1) * LANES]
            qim_ref[p, rows, :] = jnp.where(mine, qi_pair, 0)

    def reset_state():
        m_ref[...] = jnp.full(m_ref.shape, NEG_BIG, F32)
        acc_ref[...] = jnp.zeros(acc_ref.shape, F32)

    def online_update(p, s, pv_lhs):
        cols = slice(p * 2 * T, (p + 1) * 2 * T)
        m_old = m_ref[:, cols]
        s_max = _tree(jnp.maximum, s.reshape(-1, SUBLANES, 2 * T))
        m_new = jnp.maximum(m_old, jnp.max(s_max, axis=0, keepdims=True))
        alpha = jnp.exp2(m_old - m_new)
        pr = jnp.exp2((s - m_new[0:1, :]).astype(BF16))
        acc_ref[:, cols] = acc_ref[:, cols] * alpha[0:1, :] + _dot(pv_lhs, pr)
        m_ref[:, cols] = m_new

    def key_rows(j, nblocks):
        return pl.ds(pl.multiple_of(j * T, T), nblocks * T)

    def blocks_t(load, j, nblocks):
        return load(j) if nblocks == 1 else jnp.concatenate([load(j + b) for b in range(nblocks)], axis=1)

    def for_far_blocks(n_far, step):
        def pair_body(jj, c):
            step(2 * jj, 2)
            return c
        lax.fori_loop(0, lax.shift_right_logical(n_far, 1), pair_body, 0)

        @pl.when((n_far & 1) != 0)
        def _single():
            step(n_far - 1, 1)

    def idx_scores(j, nblocks):
        k_blk = kidx_ref[key_rows(j, nblocks), :]
        acc = jnp.zeros((nblocks * T, T), F32)
        for p in range(NP):
            s = _dot_nt(k_blk, qim_ref[p])
            for e in range(2):
                h = 2 * p + e
                acc = acc + w_rows[h:h + 1, :] * jnp.maximum(s[:, e * T:(e + 1) * T], 0.0)
        return acc

    causal2 = jnp.concatenate([key_i <= qry_i, key_i <= qry_i], axis=1)

    def fox_step(j, nblocks, diag=False):
        for p in range(NP):
            s = _dot_nt(kaug_ref[p, key_rows(j, nblocks), :], qaug_ref[p])
            if diag:
                s = jnp.where(causal2, s, MASKED)
            online_update(p, s, blocks_t(lambda jb: vt_ref[p, jb], j, nblocks))

    def idx_fox_step(j, nblocks):
        score_ref[pl.ds(j, nblocks)] = idx_scores(j, nblocks).reshape(nblocks, T, T)
        fox_step(j, nblocks)

    reset_state()
    for_far_blocks(i, idx_fox_step)
    score_ref[i] = jnp.where(key_i <= qry_i, idx_scores(i, 1), -jnp.inf)
    score_ref[i + 1] = jnp.full((T, T), -jnp.inf, F32)
    fox_step(i, 1, True)

    row_lo = lax.broadcasted_iota(jnp.int32, (LANES, T), 0) < HEAD_DIM
    for p in range(NP):
        c0 = slice(p * 2 * T, p * 2 * T + T)
        c1 = slice(p * 2 * T + T, (p + 1) * 2 * T)
        o_t = jnp.where(row_lo, acc_ref[0:LANES, c0] / acc_ref[LANES:LANES + 1, c0],
                        acc_ref[0:LANES, c1] / acc_ref[LANES:LANES + 1, c1])
        o_ref[:, MIX_HALF + p * LANES:MIX_HALF + (p + 1) * LANES] = o_t.T.astype(o_ref.dtype)

    kf = float(k_sel)
    t_pos = (i * T + lax.broadcasted_iota(jnp.int32, (SUBLANES, LANES), 1)).astype(F32)
    searching = (t_pos + 1.0) > kf
    G2 = 2 * G
    n_pair = i // 2 + 1
    key_in_pair = (lax.broadcasted_iota(jnp.int32, (G2, SUBLANES, LANES), 0) * SUBLANES
                   + lax.broadcasted_iota(jnp.int32, (G2, SUBLANES, LANES), 1))

    def rep(v):
        return jnp.broadcast_to(v, (SUBLANES, LANES))

    def score_pair(jj):
        return score_ref[pl.ds(2 * jj, 2)].reshape(G2, SUBLANES, LANES)

    def count(*pred_fns):
        def body(jj, cs):
            s = score_pair(jj)
            return tuple(c + _tree(jnp.add, jnp.where(f(s, jj), 1.0, 0.0)) for c, f in zip(cs, pred_fns))
        zero = jnp.zeros((SUBLANES, LANES), F32)
        cs = lax.fori_loop(0, n_pair, body, (zero,) * len(pred_fns))
        return tuple(rep(jnp.sum(c, axis=0, keepdims=True)) for c in cs)

    def count_ge(thr):
        return count(lambda s, jj: s >= thr[None])[0]

    def select_threshold():
        def mm_body(jj, carry):
            mn, mx = carry
            s = score_pair(jj)
            return (jnp.minimum(mn, _tree(jnp.minimum, jnp.where(s == -jnp.inf, jnp.inf, s))),
                    jnp.maximum(mx, _tree(jnp.maximum, s)))

        mn, mx = lax.fori_loop(0, n_pair, mm_body,
                               (jnp.full((SUBLANES, LANES), jnp.inf, F32),
                                jnp.full((SUBLANES, LANES), -jnp.inf, F32)))
        mn = rep(jnp.min(mn, axis=0, keepdims=True))
        mx = rep(jnp.max(mx, axis=0, keepdims=True))
        above_mx = mx + jnp.maximum(jnp.abs(mx) * 2.0 ** -20, 1e-30)
        c_ge0, c_gt0 = count(lambda s, jj: s >= 0.0, lambda s, jj: s > 0.0)
        thr_ge0 = c_ge0 >= kf
        thr_gt0 = c_gt0 >= kf
        lo0 = jnp.where(thr_ge0, 0.0, mn)
        cnt_lo0 = jnp.where(thr_ge0, c_ge0, t_pos + 1.0)
        hi0 = jnp.where(thr_gt0, above_mx, 0.0)
        cnt_hi0 = jnp.where(thr_gt0, 0.0, jnp.where(thr_ge0, c_gt0, c_ge0))
        undecided = thr_gt0 | jnp.logical_not(thr_ge0)
        active0 = jnp.where(searching & undecided & (cnt_lo0 > kf), 1.0, 0.0)

        def cond(st):
            return jnp.logical_and(st[5] > 0.0, st[6] < 2200)

        def body(st):
            lo, hi, cnt_lo, cnt_hi, active, _, it = st
            any_active = jnp.max(active)
            mid = 0.5 * lo + 0.5 * hi
            adjacent = (mid <= lo) | (mid >= hi)
            c = count_ge(mid)
            act = active > 0.0
            go_lo = act & jnp.logical_not(adjacent) & (c >= kf)
            go_hi = act & jnp.logical_not(adjacent) & (c < kf)
            lo = jnp.where(go_lo, mid, lo)
            cnt_lo = jnp.where(go_lo, c, cnt_lo)
            hi = jnp.where(go_hi, mid, hi)
            cnt_hi = jnp.where(go_hi, c, cnt_hi)
            active = jnp.where(act & jnp.logical_not(adjacent) & (cnt_lo > kf), 1.0, 0.0)
            return lo, hi, cnt_lo, cnt_hi, active, any_active, it + 1

        lo, hi, cnt_lo, cnt_hi, _, _, _ = lax.while_loop(
            cond, body, (lo0, hi0, cnt_lo0, cnt_hi0, active0, jnp.float32(1.0), 0))
        thr = jnp.where(searching, lo, -F32_MAX)

        tied = searching & (cnt_lo > kf)
        need = kf - cnt_hi

        @pl.when(jnp.max(jnp.where(tied, 1.0, 0.0)) > 0.0)
        def _break_ties():
            def key_index(jj):
                return (2 * jj * T + key_in_pair).astype(F32)

            def count_tied_upto(jcut):
                return count(lambda s, jj: (s == thr[None]) & (key_index(jj) <= jcut[None]))[0]

            def bs_body(_, st):
                jl, jh = st
                jm = jnp.floor(0.5 * (jl + jh))
                ok = count_tied_upto(jm) >= need
                return jnp.where(ok, jl, jm), jnp.where(ok, jm, jh)

            n_keys = (nb * T).astype(F32)
            jl0 = jnp.full((SUBLANES, LANES), -1.0, F32)
            jh0 = jnp.zeros((SUBLANES, LANES), F32) + (n_keys - 1.0)
            n_steps = int(math.ceil(math.log2(nblk * T))) + 1
            _, jcut = lax.fori_loop(0, n_steps, bs_body, (jl0, jh0))

            def drop_body(jj, c):
                s = score_pair(jj)
                drop = tied[None] & (s == thr[None]) & (key_index(jj) > jcut[None])
                score_ref[pl.ds(2 * jj, 2)] = jnp.where(drop, -jnp.inf, s).reshape(2, T, T)
                return c
            lax.fori_loop(0, n_pair, drop_body, 0)

        return thr

    thr = lax.cond((i + 1) * T > k_sel, select_threshold,
                   lambda: jnp.full((SUBLANES, LANES), -F32_MAX, F32))

    def mask_body(jj, c):
        lhsd_ref[key_rows(2 * jj, 2), KV_RANK:2 * KV_RANK] = (
            jnp.where(score_pair(jj) >= thr[None], 0.0, MASKED).reshape(2 * T, T).astype(BF16))
        return c

    lax.fori_loop(0, n_pair, mask_body, 0)

    def dsa_step(j, nblocks, bias_kind=None):
        lhs = lhsd_ref[key_rows(j, nblocks), :]
        pv_lhs = blocks_t(lambda jb: ct_ref[jb], j, nblocks)
        for p in range(NP):
            s = _dot_nt(lhs, qsel_ref[p])
            if bias_kind is not None:
                s = s + bnear_ref[bias_kind, :, p * 2 * T:(p + 1) * 2 * T]
            online_update(p, s, pv_lhs)

    reset_state()
    for_far_blocks(jnp.maximum(i - 1, 0), dsa_step)
    m_ref[...] = m_ref[...] + bfar_ref[...]

    @pl.when(i >= 1)
    def _dsa_prev():
        dsa_step(i - 1, 1, 0)

    dsa_step(i, 1, 1)

    for p in range(NP):
        o_pair = jnp.zeros((T, LANES), F32)
        for e in range(2):
            cols = slice(p * 2 * T + e * T, p * 2 * T + (e + 1) * T)
            o_t = acc_ref[0:KV_RANK, cols] / acc_ref[KV_RANK:KV_RANK + 1, cols]
            o_pair = o_pair + _dot(o_t.T.astype(BF16), wuv_ref[2 * p + e])
        o_ref[:, p * LANES:(p + 1) * LANES] = o_pair.astype(o_ref.dtype)


_EV_QA, _EV_QI, _EV_QB, _EV_KB, _EV_VB = 0, 1, 2, 3, 4
_EV_CKV, _EV_KIDX = 20, 21


def _attention(hb, w_t, f_rows, wuk_pad, wuv_pad, bias_near, bias_far, place, k_sel):
    bsz, seq, _ = hb.shape
    T = BLK
    nblk = seq // T
    H = N_HEADS
    pq, cq, pk, ck = place
    wide = lambda c: pl.BlockSpec((None, T, MIX_HALF), lambda b, i, c=c: (b, i, c))
    full_wide = lambda c: pl.BlockSpec((None, seq, MIX_HALF), lambda b, i, c=c: (b, 0, c))
    full_narrow = lambda c: pl.BlockSpec((None, seq, LANES), lambda b, i, c=c: (b, 0, c))
    const = lambda a: pl.BlockSpec(a.shape, lambda b, i, nd=a.ndim: (0,) * nd)
    return pl.pallas_call(
        functools.partial(_attn_kernel, k_sel=k_sel),
        grid=(bsz, nblk),
        in_specs=[wide(_EV_QA), wide(_EV_QI), wide(_EV_QB),
                  pl.BlockSpec((None, H, T), lambda b, i: (b, 0, i)),
                  pl.BlockSpec((None, T, LANES), lambda b, i: (b, i, 0)),
                  full_wide(_EV_KB), full_wide(_EV_VB),
                  full_narrow(_EV_CKV), full_narrow(_EV_KIDX),
                  pl.BlockSpec((None, seq, LANES), lambda b, i: (b, 0, 0)),
                  const(wuk_pad), const(wuv_pad), const(bias_near), const(bias_far),
                  const(pq), const(cq), const(pk), const(ck)],
        out_specs=pl.BlockSpec((None, T, 2 * MIX_HALF), lambda b, i: (b, i, 0)),
        out_shape=jax.ShapeDtypeStruct((bsz, seq, 2 * MIX_HALF), BF16),
        scratch_shapes=[pltpu.VMEM((nblk + 2, T, T), F32),
                        pltpu.VMEM((seq + 2 * T, 2 * KV_RANK), BF16),
                        pltpu.VMEM((nblk, PV_ROWS, T), BF16),
                        pltpu.VMEM((H // 2, seq, 2 * LANES), BF16),
                        pltpu.VMEM((H // 2, nblk, PV_ROWS, T), BF16),
                        pltpu.VMEM((H // 2, 2 * T, 2 * KV_RANK), BF16),
                        pltpu.VMEM((H // 2, 2 * T, 2 * LANES), BF16),
                        pltpu.VMEM((H // 2, 2 * T, LANES), BF16),
                        pltpu.VMEM((SUBLANES, H * T), F32),
                        pltpu.VMEM((PV_ROWS, H * T), F32)],
        compiler_params=_cparams(("parallel", "arbitrary")),
        name="dsa_fox_attention",
    )(hb, hb, hb, w_t, f_rows, hb, hb, hb, hb, f_rows, wuk_pad, wuv_pad, bias_near, bias_far,
      pq, cq, pk, ck)


def _proj_ln_kernel(a_ref, w_ref, x_ref, g_ref, b_ref, xo_ref, xb_ref, *, alpha):
    y = alpha * x_ref[...] + _dot(a_ref[...], w_ref[...])
    out = _layer_norm_rows(y, g_ref[...], b_ref[...])
    xo_ref[...] = out
    xb_ref[...] = out.astype(BF16)


def _proj_ln(a, w, x, g, b, alpha, tm):
    n, k = a.shape
    d = w.shape[1]
    return pl.pallas_call(
        functools.partial(_proj_ln_kernel, alpha=alpha),
        grid=(n // tm,),
        in_specs=[pl.BlockSpec((tm, k), lambda i: (i, 0)),
                  pl.BlockSpec((k, d), lambda i: (0, 0)),
                  pl.BlockSpec((tm, d), lambda i: (i, 0)),
                  pl.BlockSpec((1, d), lambda i: (0, 0)),
                  pl.BlockSpec((1, d), lambda i: (0, 0))],
        out_specs=[pl.BlockSpec((tm, d), lambda i: (i, 0)),
                   pl.BlockSpec((tm, d), lambda i: (i, 0))],
        out_shape=[jax.ShapeDtypeStruct((n, d), F32),
                   jax.ShapeDtypeStruct((n, d), BF16)],
        compiler_params=_cparams(("parallel",)),
        name="proj_residual_ln",
    )(a, w, x, g, b)


def _causal_conv3(h, tail_ref, cw, first_tile):
    tm = h.shape[0]
    w0, w1, w2 = cw[0:1, :], cw[1:2, :], cw[2:3, :]
    out = w0 * pltpu.roll(h, 2, axis=0) + w1 * pltpu.roll(h, 1, axis=0) + w2 * h
    tail = jnp.where(first_tile, 0.0, tail_ref[...])
    head = h[0:SUBLANES, :]
    r = lax.broadcasted_iota(jnp.int32, head.shape, 0)
    hm1 = jnp.where(r < 1, pltpu.roll(tail, 1, axis=0), pltpu.roll(head, 1, axis=0))
    hm2 = jnp.where(r < 2, pltpu.roll(tail, 2, axis=0), pltpu.roll(head, 2, axis=0))
    out_head = w0 * hm2 + w1 * hm1 + w2 * head
    tail_ref[...] = h[tm - SUBLANES:tm, :]
    return out, out_head


def _ffn_up_kernel(x_ref, wg_ref, wv_ref, cg_ref, cv_ref, o_ref, tg_ref, tv_ref, *, tiles_per_seq):
    first = (pl.program_id(1) % tiles_per_seq) == 0
    x = x_ref[...]
    g, g_head = _causal_conv3(_dot(x, wg_ref[...]), tg_ref, cg_ref[...], first)
    v, v_head = _causal_conv3(_dot(x, wv_ref[...]), tv_ref, cv_ref[...], first)

    def act(a, b):
        return (a / (1.0 + jnp.exp(-a)) * b).astype(o_ref.dtype)

    o_ref[...] = act(g, v)
    o_ref[0:SUBLANES, :] = act(g_head, v_head)


def _ffn_up(xb, w_up, conv_w, seq, tm, tn):
    n, k = xb.shape
    dff = w_up.shape[1] // 2
    ncol = dff // tn
    return pl.pallas_call(
        functools.partial(_ffn_up_kernel, tiles_per_seq=seq // tm),
        grid=(ncol, n // tm),
        in_specs=[pl.BlockSpec((tm, k), lambda j, i: (i, 0)),
                  pl.BlockSpec((k, tn), lambda j, i: (0, j)),
                  pl.BlockSpec((k, tn), lambda j, i, ncol=ncol: (0, j + ncol)),
                  pl.BlockSpec((CONV_W, tn), lambda j, i: (0, j)),
                  pl.BlockSpec((CONV_W, tn), lambda j, i, ncol=ncol: (0, j + ncol))],
        out_specs=pl.BlockSpec((tm, tn), lambda j, i: (i, j)),
        out_shape=jax.ShapeDtypeStruct((n, dff), BF16),
        scratch_shapes=[pltpu.VMEM((SUBLANES, tn), F32), pltpu.VMEM((SUBLANES, tn), F32)],
        compiler_params=_cparams(("arbitrary", "arbitrary")),
        name="ffn_up_conv_gate",
    )(xb, w_up, w_up, conv_w, conv_w)


def _gelu_tanh(x):
    return 0.5 * x * (1.0 + jnp.tanh(math.sqrt(2.0 / math.pi) * (x + 0.044715 * (x * x * x))))


def _group_mean(v, avg):
    hi = v.astype(BF16)
    lo = (v - hi.astype(F32)).astype(BF16)
    return _dot(hi, avg) + _dot(lo, avg)


def _odd_mixer_kernel(x_ref, w_ref, sg_ref, sw_ref, sb_ref, cw_ref, o_ref, tail_ref, *, tiles_per_seq):
    tm = x_ref.shape[0]
    W = MIX_HALF
    first = (pl.program_id(0) % tiles_per_seq) == 0
    x = x_ref[...]
    lane = lax.broadcasted_iota(jnp.int32, (CHUNK, LANES), 1)
    lo_half = lane < HEAD_DIM
    r = lax.broadcasted_iota(jnp.int32, (W, W), 0) // HEAD_DIM
    c = lax.broadcasted_iota(jnp.int32, (W, W), 1) // HEAD_DIM
    avg = jnp.where(r == c, 1.0 / HEAD_DIM, 0.0).astype(BF16)
    tri = (lax.broadcasted_iota(jnp.int32, (CHUNK, CHUNK), 1)
           <= lax.broadcasted_iota(jnp.int32, (CHUNK, CHUNK), 0))

    u = _gelu_tanh(_dot(x, w_ref[:, 0:W]))
    v = _gelu_tanh(_dot(x, w_ref[:, W:2 * W]))
    vc = v - _group_mean(v, avg)
    var = _group_mean(vc * vc, avg)
    vn = (vc * lax.rsqrt(var + LN_EPS) * sg_ref[...]).astype(BF16)
    for n in range(tm // CHUNK):
        rows = slice(n * CHUNK, (n + 1) * CHUNK)
        for p in range(N_HEADS // 2):
            cols = slice(p * LANES, (p + 1) * LANES)
            vp = vn[rows, cols]
            w_e = jnp.where(tri, sw_ref[2 * p], 0.0).astype(BF16)
            w_o = jnp.where(tri, sw_ref[2 * p + 1], 0.0).astype(BF16)
            mix = (_dot(w_e, jnp.where(lo_half, vp, 0)) + _dot(w_o, jnp.where(lo_half, 0, vp))
                   + sb_ref[:, cols])
            o_ref[rows, cols] = (u[rows, cols] * mix).astype(o_ref.dtype)

    g_b = _dot(x, w_ref[:, 2 * W:3 * W])
    y = _dot(x, w_ref[:, 3 * W:4 * W]) * _dot(x, w_ref[:, 4 * W:5 * W])
    conv, conv_head = _causal_conv3(y, tail_ref, cw_ref[...], first)
    o_ref[:, W:2 * W] = (g_b * conv).astype(o_ref.dtype)
    o_ref[0:SUBLANES, W:2 * W] = (g_b[0:SUBLANES, :] * conv_head).astype(o_ref.dtype)


def _odd_mixer(xb, w_in, sgu_g, sgu_w, sgu_b_tile, conv_w, seq, tm):
    n, k = xb.shape
    W = MIX_HALF
    return pl.pallas_call(
        functools.partial(_odd_mixer_kernel, tiles_per_seq=seq // tm),
        grid=(n // tm,),
        in_specs=[pl.BlockSpec((tm, k), lambda i: (i, 0)),
                  pl.BlockSpec((k, 5 * W), lambda i: (0, 0)),
                  pl.BlockSpec((1, W), lambda i: (0, 0)),
                  pl.BlockSpec((N_HEADS, CHUNK, CHUNK), lambda i: (0, 0, 0)),
                  pl.BlockSpec((CHUNK, W), lambda i: (0, 0)),
                  pl.BlockSpec((CONV_W, W), lambda i: (0, 0))],
        out_specs=pl.BlockSpec((tm, 2 * W), lambda i: (i, 0)),
        out_shape=jax.ShapeDtypeStruct((n, 2 * W), BF16),
        scratch_shapes=[pltpu.VMEM((SUBLANES, W), F32)],
        compiler_params=_cparams(("arbitrary",)),
        name="odd_mixer",
    )(xb, w_in, sgu_g, sgu_w, sgu_b_tile, conv_w)


def _t5_bucket_np(dist):
    max_exact = N_BUCKETS // 2
    n = np.maximum(dist, 0)
    nf = np.maximum(n, 1).astype(np.float32)
    large = max_exact + (np.log(nf / max_exact) / np.float32(math.log(T5_MAX_DIST / max_exact))
                         * (N_BUCKETS - max_exact)).astype(np.int32)
    large = np.minimum(large, N_BUCKETS - 1)
    return np.where(n < max_exact, n, large)


def _bias_tables(rel_bias):
    T = BLK
    s = np.arange(T)[:, None]
    t = np.arange(T)[None, :]
    bucket = _t5_bucket_np(np.stack([t + T - s, t - s]))
    far_bucket = _t5_bucket_np(np.array([T + 1]))[0]
    assert far_bucket == N_BUCKETS - 1
    assert (_t5_bucket_np(np.arange(T + 1, 64 * T)) == far_bucket).all()
    rb = rel_bias.astype(F32) * LOG2E
    near = jnp.swapaxes(rb[bucket], 2, 3).reshape(2, T, N_HEADS * T)
    far = jnp.repeat(rb[far_bucket], T)[None, :]
    return near, far


def _placement_constants():
    H, L = N_HEADS, LANES
    pq = np.zeros((3 * L, H * L), np.float32)
    cq = np.zeros((1, H * L), np.float32)
    pk = np.zeros((3 * L, (H // 2) * L), np.float32)
    ck = np.zeros((1, (H // 2) * L), np.float32)
    for h in range(H):
        p, e = divmod(h, 2)
        for piece in range(3):
            pq[piece * L + h, h * L + 6 + piece] = 1.0
            cq[0, h * L + 3 * e + piece] = 1.0
            pk[piece * L + h, p * L + 3 * e + piece] = -1.0
            ck[0, p * L + 6 + piece] = 1.0
    return jnp.asarray(pq, BF16), jnp.asarray(cq), jnp.asarray(pk, BF16), jnp.asarray(ck)


def _even_weights(w_in, w_uk, w_uv):
    hd, H = HEAD_DIM, N_HEADS
    widths = (H * hd, KV_RANK, H * hd, hd, H, H * hd, H * hd, H * hd, H)
    offs = np.cumsum((0,) + widths)
    seg = lambda s: w_in[:, offs[s]:offs[s + 1]]
    q_a, c_kv, q_idx, k_idx, w_idx, q_b, k_b, v_b, f_in = [seg(s) for s in range(9)]
    scale = hd ** -0.5
    w_big = jnp.concatenate([q_a, q_idx * scale, q_b * (scale * LOG2E), k_b, v_b, c_kv,
                             k_idx, k_idx], axis=1).astype(BF16)
    pad = jnp.zeros((w_in.shape[0], LANES - 2 * H), w_in.dtype)
    w_small = jnp.concatenate([w_idx, f_in, pad], axis=1).astype(BF16)
    wuk_t = jnp.swapaxes(w_uk, 1, 2) * (scale * LOG2E)
    wuk_pad = jnp.zeros((H, LANES, KV_RANK), F32)
    wuv_pad = jnp.zeros((H, KV_RANK, LANES), F32)
    for h in range(H):
        o = hd * (h % 2)
        wuk_pad = wuk_pad.at[h, o:o + hd, :].set(wuk_t[h])
        wuv_pad = wuv_pad.at[h, :, o:o + hd].set(w_uv[h])
    return w_big, w_small, wuk_pad.astype(BF16), wuv_pad.astype(BF16)


def kernel(x, ln_g, ln_b, rel_bias, ev_w_in, ev_w_uk, ev_w_uv, ev_b_f, ev_w_o,
           od_w_in, od_sgu_g, od_sgu_w, od_sgu_b, od_conv_w, od_w_o,
           ffn_w_up, ffn_conv_w, ffn_w_down):
    bsz, seq, d = x.shape
    depth = ln_g.shape[0]
    n = bsz * seq
    alpha = (2.0 * depth) ** 0.25
    k_sel = min(TOPK_MAX, seq // 4)
    H = N_HEADS
    assert seq % BLK == 0 and seq % CHUNK == 0
    tm = min(512, seq)
    tm_ffn = min(1024, seq)

    bias_near, bias_far = _bias_tables(rel_bias)
    place = _placement_constants()
    xf = x.reshape(n, d)
    xb = xf.astype(BF16)
    for layer in range(depth):
        j = layer // 2
        g0, b0 = ln_g[layer, 0][None, :], ln_b[layer, 0][None, :]
        g1, b1 = ln_g[layer, 1][None, :], ln_b[layer, 1][None, :]
        if layer % 2 == 0:
            w_big, w_small, wuk_pad, wuv_pad = _even_weights(ev_w_in[j], ev_w_uk[j], ev_w_uv[j])
            hb, hs = _even_proj(xb, w_big, w_small, tm)
            hb = hb.reshape(bsz, seq, -1)
            hs_t = jnp.swapaxes(hs.reshape(bsz, seq, -1)[:, :, 0:2 * H], 1, 2)
            cum_f = _cumf(hs_t[:, H:2 * H], ev_b_f[j][:, None])
            f_rows = jnp.pad(jnp.swapaxes(cum_f, 1, 2), ((0, 0), (0, 0), (0, LANES - H)))
            mix = _attention(hb, hs_t[:, 0:H], f_rows, wuk_pad, wuv_pad, bias_near, bias_far,
                             place, k_sel)
            w_o = ev_w_o[j]
        else:
            sgu_b_tile = jnp.repeat(jnp.swapaxes(od_sgu_b[j], 0, 1), HEAD_DIM, axis=1)
            mix = _odd_mixer(xb, od_w_in[j].astype(BF16), od_sgu_g[j].reshape(1, -1), od_sgu_w[j],
                             sgu_b_tile, od_conv_w[j], seq, min(256, seq))
            w_o = od_w_o[j]
        xf, xb = _proj_ln(mix.reshape(n, d), w_o.astype(BF16), xf, g0, b0, alpha, tm)
        act = _ffn_up(xb, ffn_w_up[layer].astype(BF16), ffn_conv_w[layer], seq, tm_ffn, 256)
        xf, xb = _proj_ln(act, ffn_w_down[layer].astype(BF16), xf, g1, b1, alpha, tm)
    return xf.reshape(bsz, seq, d)
```

```python
import functools
import math

import numpy as np
import jax
import jax.numpy as jnp
from jax import lax
from jax.experimental import pallas as pl
from jax.experimental.pallas import tpu as pltpu

F32 = jnp.float32
BF16 = jnp.bfloat16

HEAD_DIM = 64
N_HEADS = 8
KV_RANK = 128
TOPK_MAX = 256
N_BUCKETS = 32
T5_MAX_DIST = 128
CHUNK = 128
CONV_W = 3
LN_EPS = 1e-5
MIX_HALF = N_HEADS * HEAD_DIM

LANES = 128
SUBLANES = 8
BF16_ROWS = 16
VMEM_LIMIT = 56 * 1024 * 1024

BLK = 128
PV_ROWS = KV_RANK + BF16_ROWS
NEG_BIG = -1e30
MASKED = -(2.0 ** 100)
F32_MAX = float(np.finfo(np.float32).max)
LOG2E = math.log2(math.e)


def _cparams(sem):
    return pltpu.CompilerParams(dimension_semantics=sem, vmem_limit_bytes=VMEM_LIMIT)


def _dot(a, b):
    return jnp.dot(a, b, preferred_element_type=F32)


def _dot_nt(a, b):
    return lax.dot_general(a, b, (((1,), (1,)), ((), ())), preferred_element_type=F32)


def _layer_norm_rows(y, g, b):
    mu = jnp.mean(y, axis=-1, keepdims=True)
    yc = y - mu
    var = jnp.mean(yc * yc, axis=-1, keepdims=True)
    return yc * lax.rsqrt(var + LN_EPS) * g + b


def _tree(op, x):
    while x.shape[0] > 1:
        half = x.shape[0] // 2
        x = op(x[:half], x[half:])
    return x[0]


def _bf16_pieces(v):
    hi = v.astype(BF16)
    r1 = v - hi.astype(F32)
    mid = r1.astype(BF16)
    lo = (r1 - mid.astype(F32)).astype(BF16)
    return jnp.concatenate([hi, mid, lo], axis=1)


def _even_proj_kernel(x_ref, wb_ref, ws_ref, hb_ref, hs_ref):
    x = x_ref[...]
    hb_ref[...] = _dot(x, wb_ref[...]).astype(hb_ref.dtype)
    hs_ref[...] = _dot(x, ws_ref[...])


def _even_proj(xb, w_big, w_small, tm):
    n, k = xb.shape
    mb = w_big.shape[1]
    ms = w_small.shape[1]
    return pl.pallas_call(
        _even_proj_kernel,
        grid=(n // tm,),
        in_specs=[pl.BlockSpec((tm, k), lambda i: (i, 0)),
                  pl.BlockSpec((k, mb), lambda i: (0, 0)),
                  pl.BlockSpec((k, ms), lambda i: (0, 0))],
        out_specs=[pl.BlockSpec((tm, mb), lambda i: (i, 0)),
                   pl.BlockSpec((tm, ms), lambda i: (i, 0))],
        out_shape=[jax.ShapeDtypeStruct((n, mb), BF16),
                   jax.ShapeDtypeStruct((n, ms), F32)],
        compiler_params=_cparams(("parallel",)),
        name="even_proj",
    )(xb, w_big, w_small)


def _cumf_kernel(f_ref, b_ref, o_ref):
    seq = f_ref.shape[-1]
    row = lax.broadcasted_iota(jnp.int32, (LANES, LANES), 0)
    col = lax.broadcasted_iota(jnp.int32, (LANES, LANES), 1)
    upper = jnp.where(row <= col, 1.0, 0.0).astype(F32)
    carry = jnp.zeros((f_ref.shape[0], 1), F32)
    for c in range(seq // LANES):
        z = f_ref[:, c * LANES:(c + 1) * LANES] + b_ref[...]
        lf = jnp.minimum(z, 0.0) - jnp.log1p(jnp.exp(-jnp.abs(z)))
        cs = jnp.dot(lf, upper, preferred_element_type=F32,
                     precision=lax.Precision.HIGHEST) + carry
        o_ref[:, c * LANES:(c + 1) * LANES] = cs
        carry = cs[:, LANES - 1:LANES]


def _cumf(f_t, b_f):
    bsz, nh, seq = f_t.shape
    return pl.pallas_call(
        _cumf_kernel,
        grid=(bsz,),
        in_specs=[pl.BlockSpec((None, nh, seq), lambda b: (b, 0, 0)),
                  pl.BlockSpec((nh, 1), lambda b: (0, 0))],
        out_specs=pl.BlockSpec((None, nh, seq), lambda b: (b, 0, 0)),
        out_shape=jax.ShapeDtypeStruct((bsz, nh, seq), F32),
        compiler_params=_cparams(("parallel",)),
        name="fox_cumf",
    )(f_t, b_f)


def _attn_kernel(qa_ref, qi_ref, qb_ref, wt_ref, fq_ref, kb_ref, vb_ref, ckv_ref, kidx_ref, fk_ref,
                 wuk_ref, wuv_ref, bnear_ref, bfar_ref, pq_ref, cq_ref, pk_ref, ck_ref,
                 o_ref,
                 score_ref, lhsd_ref, ct_ref, kaug_ref, vt_ref, qsel_ref, qaug_ref, qim_ref,
                 m_ref, acc_ref, *, k_sel):
    T = BLK
    H = N_HEADS
    NP = H // 2
    G = T // SUBLANES
    i = pl.program_id(1)
    nb = i + 1
    nblk = ct_ref.shape[0]
    lane = lax.broadcasted_iota(jnp.int32, (T, LANES), 1)
    lo_half = lane < HEAD_DIM
    key_i = lax.broadcasted_iota(jnp.int32, (T, T), 0)
    qry_i = lax.broadcasted_iota(jnp.int32, (T, T), 1)
    ones_rows = jnp.ones((BF16_ROWS, T), BF16)

    @pl.when(i == 0)
    def _per_batch():
        def prep(j, c):
            r = pl.ds(pl.multiple_of(j * T, T), T)
            cb = ckv_ref[r, :]
            lhsd_ref[r, 0:KV_RANK] = cb
            ct_ref[j, 0:KV_RANK, :] = cb.astype(F32).T.astype(BF16)
            ct_ref[j, KV_RANK:PV_ROWS, :] = ones_rows
            ext = (_dot(_bf16_pieces(fk_ref[r, :] * LOG2E), pk_ref[...]) + ck_ref[...]).astype(BF16)
            for p in range(NP):
                cols = slice(p * LANES, (p + 1) * LANES)
                kaug_ref[p, r, 0:LANES] = kb_ref[r, cols]
                kaug_ref[p, r, LANES:2 * LANES] = ext[:, cols]
                vt_ref[p, j, 0:LANES, :] = vb_ref[r, cols].astype(F32).T.astype(BF16)
                vt_ref[p, j, LANES:PV_ROWS, :] = ones_rows
            return c
        lax.fori_loop(0, nblk, prep, 0)

    w_rows = wt_ref[...] * (H ** -0.5)
    ext_q = (_dot(_bf16_pieces(fq_ref[...] * LOG2E), pq_ref[...]) + cq_ref[...]).astype(BF16)
    eye = jnp.where(key_i == qry_i, 1.0, 0.0).astype(BF16)
    for p in range(NP):
        cols = slice(p * LANES, (p + 1) * LANES)
        qa_pair, qi_pair, qb_pair = qa_ref[:, cols], qi_ref[:, cols], qb_ref[:, cols]
        for e in range(2):
            h = 2 * p + e
            rows = slice(e * T, (e + 1) * T)
            mine = lo_half if e == 0 else jnp.logical_not(lo_half)
            qsel_ref[p, rows, 0:KV_RANK] = _dot(qa_pair, wuk_ref[h]).astype(BF16)
            qsel_ref[p, rows, KV_RANK:2 * KV_RANK] = eye
            qaug_ref[p, rows, 0:LANES] = jnp.where(mine, qb_pair, 0)
            qaug_ref[p, rows, LANES:2 * LANES] = ext_q[:, h * LANES:(h + 1) * LANES]
            qim_ref[p, rows, :] = jnp.where(mine, qi_pair, 0)

    def reset_state():
        m_ref[...] = jnp.full(m_ref.shape, NEG_BIG, F32)
        acc_ref[...] = jnp.zeros(acc_ref.shape, F32)

    def online_update(p, s, pv_lhs):
        cols = slice(p * 2 * T, (p + 1) * 2 * T)
        m_old = m_ref[:, cols]
        s_max = _tree(jnp.maximum, s.reshape(-1, SUBLANES, 2 * T))
        m_new = jnp.maximum(m_old, jnp.max(s_max, axis=0, keepdims=True))
        alpha = jnp.exp2(m_old - m_new)
        pr = jnp.exp2((s - m_new[0:1, :]).astype(BF16))
        acc_ref[:, cols] = acc_ref[:, cols] * alpha[0:1, :] + _dot(pv_lhs, pr)
        m_ref[:, cols] = m_new

    def key_rows(j, nblocks):
        return pl.ds(pl.multiple_of(j * T, T), nblocks * T)

    def blocks_t(load, j, nblocks):
        return load(j) if nblocks == 1 else jnp.concatenate([load(j + b) for b in range(nblocks)], axis=1)

    def for_far_blocks(n_far, step):
        def pair_body(jj, c):
            step(2 * jj, 2)
            return c
        lax.fori_loop(0, lax.shift_right_logical(n_far, 1), pair_body, 0)

        @pl.when((n_far & 1) != 0)
        def _single():
            step(n_far - 1, 1)

    def idx_scores(j, nblocks):
        k_blk = kidx_ref[key_rows(j, nblocks), :]
        acc = jnp.zeros((nblocks * T, T), F32)
        for p in range(NP):
            s = _dot_nt(k_blk, qim_ref[p])
            for e in range(2):
                h = 2 * p + e
                acc = acc + w_rows[h:h + 1, :] * jnp.maximum(s[:, e * T:(e + 1) * T], 0.0)
        return acc

    causal2 = jnp.concatenate([key_i <= qry_i, key_i <= qry_i], axis=1)

    def fox_step(j, nblocks, diag=False):
        for p in range(NP):
            s = _dot_nt(kaug_ref[p, key_rows(j, nblocks), :], qaug_ref[p])
            if diag:
                s = jnp.where(causal2, s, MASKED)
            online_update(p, s, blocks_t(lambda jb: vt_ref[p, jb], j, nblocks))

    def idx_fox_step(j, nblocks):
        score_ref[pl.ds(j, nblocks)] = idx_scores(j, nblocks).reshape(nblocks, T, T)
        fox_step(j, nblocks)

    reset_state()
    for_far_blocks(i, idx_fox_step)
    score_ref[i] = jnp.where(key_i <= qry_i, idx_scores(i, 1), -jnp.inf)
    score_ref[i + 1] = jnp.full((T, T), -jnp.inf, F32)
    fox_step(i, 1, True)

    row_lo = lax.broadcasted_iota(jnp.int32, (LANES, T), 0) < HEAD_DIM
    for p in range(NP):
        c0 = slice(p * 2 * T, p * 2 * T + T)
        c1 = slice(p * 2 * T + T, (p + 1) * 2 * T)
        o_t = jnp.where(row_lo, acc_ref[0:LANES, c0] / acc_ref[LANES:LANES + 1, c0],
                        acc_ref[0:LANES, c1] / acc_ref[LANES:LANES + 1, c1])
        o_ref[:, MIX_HALF + p * LANES:MIX_HALF + (p + 1) * LANES] = o_t.T.astype(o_ref.dtype)

    kf = float(k_sel)
    t_pos = (i * T + lax.broadcasted_iota(jnp.int32, (SUBLANES, LANES), 1)).astype(F32)
    searching = (t_pos + 1.0) > kf
    G2 = 2 * G
    n_pair = i // 2 + 1
    key_in_pair = (lax.broadcasted_iota(jnp.int32, (G2, SUBLANES, LANES), 0) * SUBLANES
                   + lax.broadcasted_iota(jnp.int32, (G2, SUBLANES, LANES), 1))

    def rep(v):
        return jnp.broadcast_to(v, (SUBLANES, LANES))

    def score_pair(jj):
        return score_ref[pl.ds(2 * jj, 2)].reshape(G2, SUBLANES, LANES)

    def count(*pred_fns):
        def body(jj, cs):
            s = score_pair(jj)
            return tuple(c + _tree(jnp.add, jnp.where(f(s, jj), 1.0, 0.0)) for c, f in zip(cs, pred_fns))
        zero = jnp.zeros((SUBLANES, LANES), F32)
        cs = lax.fori_loop(0, n_pair, body, (zero,) * len(pred_fns))
        return tuple(rep(jnp.sum(c, axis=0, keepdims=True)) for c in cs)

    def count_ge(thr):
        return count(lambda s, jj: s >= thr[None])[0]

    def select_threshold():
        def mm_body(jj, carry):
            mn, mx = carry
            s = score_pair(jj)
            return (jnp.minimum(mn, _tree(jnp.minimum, jnp.where(s == -jnp.inf, jnp.inf, s))),
                    jnp.maximum(mx, _tree(jnp.maximum, s)))

        mn, mx = lax.fori_loop(0, n_pair, mm_body,
                               (jnp.full((SUBLANES, LANES), jnp.inf, F32),
                                jnp.full((SUBLANES, LANES), -jnp.inf, F32)))
        mn = rep(jnp.min(mn, axis=0, keepdims=True))
        mx = rep(jnp.max(mx, axis=0, keepdims=True))
        above_mx = mx + jnp.maximum(jnp.abs(mx) * 2.0 ** -20, 1e-30)
        c_ge0, c_gt0 = count(lambda s, jj: s >= 0.0, lambda s, jj: s > 0.0)
        thr_ge0 = c_ge0 >= kf
        thr_gt0 = c_gt0 >= kf
        lo0 = jnp.where(thr_ge0, 0.0, mn)
        cnt_lo0 = jnp.where(thr_ge0, c_ge0, t_pos + 1.0)
        hi0 = jnp.where(thr_gt0, above_mx, 0.0)
        cnt_hi0 = jnp.where(thr_gt0, 0.0, jnp.where(thr_ge0, c_gt0, c_ge0))
        undecided = thr_gt0 | jnp.logical_not(thr_ge0)
        active0 = jnp.where(searching & undecided & (cnt_lo0 > kf), 1.0, 0.0)

        def cond(st):
            return jnp.logical_and(st[5] > 0.0, st[6] < 2200)

        def body(st):
            lo, hi, cnt_lo, cnt_hi, active, _, it = st
            any_active = jnp.max(active)
            mid = 0.5 * lo + 0.5 * hi
            adjacent = (mid <= lo) | (mid >= hi)
            c = count_ge(mid)
            act = active > 0.0
            go_lo = act & jnp.logical_not(adjacent) & (c >= kf)
            go_hi = act & jnp.logical_not(adjacent) & (c < kf)
            lo = jnp.where(go_lo, mid, lo)
            cnt_lo = jnp.where(go_lo, c, cnt_lo)
            hi = jnp.where(go_hi, mid, hi)
            cnt_hi = jnp.where(go_hi, c, cnt_hi)
            active = jnp.where(act & jnp.logical_not(adjacent) & (cnt_lo > kf), 1.0, 0.0)
            return lo, hi, cnt_lo, cnt_hi, active, any_active, it + 1

        lo, hi, cnt_lo, cnt_hi, _, _, _ = lax.while_loop(
            cond, body, (lo0, hi0, cnt_lo0, cnt_hi0, active0, jnp.float32(1.0), 0))
        thr = jnp.where(searching, lo, -F32_MAX)

        tied = searching & (cnt_lo > kf)
        need = kf - cnt_hi

        @pl.when(jnp.max(jnp.where(tied, 1.0, 0.0)) > 0.0)
        def _break_ties():
            def key_index(jj):
                return (2 * jj * T + key_in_pair).astype(F32)

            def count_tied_upto(jcut):
                return count(lambda s, jj: (s == thr[None]) & (key_index(jj) <= jcut[None]))[0]

            def bs_body(_, st):
                jl, jh = st
                jm = jnp.floor(0.5 * (jl + jh))
                ok = count_tied_upto(jm) >= need
                return jnp.where(ok, jl, jm), jnp.where(ok, jm, jh)

            n_keys = (nb * T).astype(F32)
            jl0 = jnp.full((SUBLANES, LANES), -1.0, F32)
            jh0 = jnp.zeros((SUBLANES, LANES), F32) + (n_keys - 1.0)
            n_steps = int(math.ceil(math.log2(nblk * T))) + 1
            _, jcut = lax.fori_loop(0, n_steps, bs_body, (jl0, jh0))

            def drop_body(jj, c):
                s = score_pair(jj)
                drop = tied[None] & (s == thr[None]) & (key_index(jj) > jcut[None])
                score_ref[pl.ds(2 * jj, 2)] = jnp.where(drop, -jnp.inf, s).reshape(2, T, T)
                return c
            lax.fori_loop(0, n_pair, drop_body, 0)

        return thr

    thr = lax.cond((i + 1) * T > k_sel, select_threshold,
                   lambda: jnp.full((SUBLANES, LANES), -F32_MAX, F32))

    def mask_body(jj, c):
        lhsd_ref[key_rows(2 * jj, 2), KV_RANK:2 * KV_RANK] = (
            jnp.where(score_pair(jj) >= thr[None], 0.0, MASKED).reshape(2 * T, T).astype(BF16))
        return c

    lax.fori_loop(0, n_pair, mask_body, 0)

    def dsa_step(j, nblocks, bias_kind=None):
        lhs = lhsd_ref[key_rows(j, nblocks), :]
        pv_lhs = blocks_t(lambda jb: ct_ref[jb], j, nblocks)
        for p in range(NP):
            s = _dot_nt(lhs, qsel_ref[p])
            if bias_kind is not None:
                s = s + bnear_ref[bias_kind, :, p * 2 * T:(p + 1) * 2 * T]
            online_update(p, s, pv_lhs)

    reset_state()
    for_far_blocks(jnp.maximum(i - 1, 0), dsa_step)
    m_ref[...] = m_ref[...] + bfar_ref[...]

    @pl.when(i >= 1)
    def _dsa_prev():
        dsa_step(i - 1, 1, 0)

    dsa_step(i, 1, 1)

    for p in range(NP):
        o_pair = jnp.zeros((T, LANES), F32)
        for e in range(2):
            cols = slice(p * 2 * T + e * T, p * 2 * T + (e + 1) * T)
            o_t = acc_ref[0:KV_RANK, cols] / acc_ref[KV_RANK:KV_RANK + 1, cols]
            o_pair = o_pair + _dot(o_t.T.astype(BF16), wuv_ref[2 * p + e])
        o_ref[:, p * LANES:(p + 1) * LANES] = o_pair.astype(o_ref.dtype)


_EV_QA, _EV_QI, _EV_QB, _EV_KB, _EV_VB = 0, 1, 2, 3, 4
_EV_CKV, _EV_KIDX = 20, 21


def _attention(hb, w_t, f_rows, wuk_pad, wuv_pad, bias_near, bias_far, place, k_sel):
    bsz, seq, _ = hb.shape
    T = BLK
    nblk = seq // T
    H = N_HEADS
    pq, cq, pk, ck = place
    wide = lambda c: pl.BlockSpec((None, T, MIX_HALF), lambda b, i, c=c: (b, i, c))
    full_wide = lambda c: pl.BlockSpec((None, seq, MIX_HALF), lambda b, i, c=c: (b, 0, c))
    full_narrow = lambda c: pl.BlockSpec((None, seq, LANES), lambda b, i, c=c: (b, 0, c))
    const = lambda a: pl.BlockSpec(a.shape, lambda b, i, nd=a.ndim: (0,) * nd)
    return pl.pallas_call(
        functools.partial(_attn_kernel, k_sel=k_sel),
        grid=(bsz, nblk),
        in_specs=[wide(_EV_QA), wide(_EV_QI), wide(_EV_QB),
                  pl.BlockSpec((None, H, T), lambda b, i: (b, 0, i)),
                  pl.BlockSpec((None, T, LANES), lambda b, i: (b, i, 0)),
                  full_wide(_EV_KB), full_wide(_EV_VB),
                  full_narrow(_EV_CKV), full_narrow(_EV_KIDX),
                  pl.BlockSpec((None, seq, LANES), lambda b, i: (b, 0, 0)),
                  const(wuk_pad), const(wuv_pad), const(bias_near), const(bias_far),
                  const(pq), const(cq), const(pk), const(ck)],
        out_specs=pl.BlockSpec((None, T, 2 * MIX_HALF), lambda b, i: (b, i, 0)),
        out_shape=jax.ShapeDtypeStruct((bsz, seq, 2 * MIX_HALF), BF16),
        scratch_shapes=[pltpu.VMEM((nblk + 2, T, T), F32),
                        pltpu.VMEM((seq + 2 * T, 2 * KV_RANK), BF16),
                        pltpu.VMEM((nblk, PV_ROWS, T), BF16),
                        pltpu.VMEM((H // 2, seq, 2 * LANES), BF16),
                        pltpu.VMEM((H // 2, nblk, PV_ROWS, T), BF16),
                        pltpu.VMEM((H // 2, 2 * T, 2 * KV_RANK), BF16),
                        pltpu.VMEM((H // 2, 2 * T, 2 * LANES), BF16),
                        pltpu.VMEM((H // 2, 2 * T, LANES), BF16),
                        pltpu.VMEM((SUBLANES, H * T), F32),
                        pltpu.VMEM((PV_ROWS, H * T), F32)],
        compiler_params=_cparams(("parallel", "arbitrary")),
        name="dsa_fox_attention",
    )(hb, hb, hb, w_t, f_rows, hb, hb, hb, hb, f_rows, wuk_pad, wuv_pad, bias_near, bias_far,
      pq, cq, pk, ck)


def _proj_ln_kernel(a_ref, w_ref, x_ref, g_ref, b_ref, xo_ref, xb_ref, *, alpha, sub):
    for c in range(a_ref.shape[0] // sub):
        rows = slice(c * sub, (c + 1) * sub)
        y = alpha * x_ref[rows, :] + _dot(a_ref[rows, :], w_ref[...])
        out = _layer_norm_rows(y, g_ref[...], b_ref[...])
        xo_ref[rows, :] = out
        xb_ref[rows, :] = out.astype(BF16)


def _proj_ln(a, w, x, g, b, alpha, tm):
    n, k = a.shape
    d = w.shape[1]
    return pl.pallas_call(
        functools.partial(_proj_ln_kernel, alpha=alpha, sub=min(128, tm)),
        grid=(n // tm,),
        in_specs=[pl.BlockSpec((tm, k), lambda i: (i, 0)),
                  pl.BlockSpec((k, d), lambda i: (0, 0)),
                  pl.BlockSpec((tm, d), lambda i: (i, 0)),
                  pl.BlockSpec((1, d), lambda i: (0, 0)),
                  pl.BlockSpec((1, d), lambda i: (0, 0))],
        out_specs=[pl.BlockSpec((tm, d), lambda i: (i, 0)),
                   pl.BlockSpec((tm, d), lambda i: (i, 0))],
        out_shape=[jax.ShapeDtypeStruct((n, d), F32),
                   jax.ShapeDtypeStruct((n, d), BF16)],
        compiler_params=_cparams(("parallel",)),
        name="proj_residual_ln",
    )(a, w, x, g, b)


def _causal_conv3(h, tail, cw):
    w0, w1, w2 = cw[0:1, :], cw[1:2, :], cw[2:3, :]
    body = w0 * pltpu.roll(h, 2, axis=0) + w1 * pltpu.roll(h, 1, axis=0) + w2 * h
    head = h[0:SUBLANES, :]
    r = lax.broadcasted_iota(jnp.int32, head.shape, 0)
    hm1 = jnp.where(r < 1, pltpu.roll(tail, 1, axis=0), pltpu.roll(head, 1, axis=0))
    hm2 = jnp.where(r < 2, pltpu.roll(tail, 2, axis=0), pltpu.roll(head, 2, axis=0))
    out_head = w0 * hm2 + w1 * hm1 + w2 * head
    return jnp.concatenate([out_head, body[SUBLANES:, :]], axis=0)


def _ffn_up_kernel(x_ref, wg_ref, wv_ref, cg_ref, cv_ref, o_ref, tg_ref, tv_ref, hg_ref, hv_ref,
                   *, tiles_per_seq, sub):
    j = pl.program_id(1)
    tm = x_ref.shape[0]
    first = (pl.program_id(0) % tiles_per_seq) == 0
    hg_ref[0:SUBLANES, :] = jnp.where(first, 0.0, tg_ref[j])
    hv_ref[0:SUBLANES, :] = jnp.where(first, 0.0, tv_ref[j])

    def conv(h_ref, r0, cw):
        return (cw[0:1, :] * h_ref[r0 + SUBLANES - 2:r0 + SUBLANES - 2 + sub, :]
                + cw[1:2, :] * h_ref[r0 + SUBLANES - 1:r0 + SUBLANES - 1 + sub, :]
                + cw[2:3, :] * h_ref[r0 + SUBLANES:r0 + SUBLANES + sub, :])

    for c in range(tm // sub):
        r0 = c * sub
        x = x_ref[r0:r0 + sub, :]
        hg_ref[r0 + SUBLANES:r0 + SUBLANES + sub, :] = _dot(x, wg_ref[...])
        hv_ref[r0 + SUBLANES:r0 + SUBLANES + sub, :] = _dot(x, wv_ref[...])
        g = conv(hg_ref, r0, cg_ref[...])
        v = conv(hv_ref, r0, cv_ref[...])
        o_ref[r0:r0 + sub, :] = (g / (1.0 + jnp.exp(-g)) * v).astype(o_ref.dtype)
    tg_ref[j] = hg_ref[tm:tm + SUBLANES, :]
    tv_ref[j] = hv_ref[tm:tm + SUBLANES, :]


def _ffn_up(xb, w_up, conv_w, seq, tm, tn, sub):
    n, k = xb.shape
    dff = w_up.shape[1] // 2
    ncol = dff // tn
    return pl.pallas_call(
        functools.partial(_ffn_up_kernel, tiles_per_seq=seq // tm, sub=sub),
        grid=(n // tm, ncol),
        in_specs=[pl.BlockSpec((tm, k), lambda i, j: (i, 0)),
                  pl.BlockSpec((k, tn), lambda i, j: (0, j)),
                  pl.BlockSpec((k, tn), lambda i, j, ncol=ncol: (0, j + ncol)),
                  pl.BlockSpec((CONV_W, tn), lambda i, j: (0, j)),
                  pl.BlockSpec((CONV_W, tn), lambda i, j, ncol=ncol: (0, j + ncol))],
        out_specs=pl.BlockSpec((tm, tn), lambda i, j: (i, j)),
        out_shape=jax.ShapeDtypeStruct((n, dff), BF16),
        scratch_shapes=[pltpu.VMEM((ncol, SUBLANES, tn), F32), pltpu.VMEM((ncol, SUBLANES, tn), F32),
                        pltpu.VMEM((tm + SUBLANES, tn), F32), pltpu.VMEM((tm + SUBLANES, tn), F32)],
        compiler_params=_cparams(("arbitrary", "arbitrary")),
        name="ffn_up_conv_gate",
    )(xb, w_up, w_up, conv_w, conv_w)


def _gelu_tanh(x):
    return 0.5 * x * (1.0 + jnp.tanh(math.sqrt(2.0 / math.pi) * (x + 0.044715 * (x * x * x))))


def _group_mean(v, avg):
    hi = v.astype(BF16)
    lo = (v - hi.astype(F32)).astype(BF16)
    return _dot(hi, avg) + _dot(lo, avg)


def _odd_mixer_kernel(x_ref, w_ref, sg_ref, sw_ref, sb_ref, cw_ref, o_ref, tail_ref, *, tiles_per_seq):
    tm = x_ref.shape[0]
    W = MIX_HALF
    first = (pl.program_id(0) % tiles_per_seq) == 0
    x = x_ref[...]
    lane = lax.broadcasted_iota(jnp.int32, (CHUNK, LANES), 1)
    lo_half = lane < HEAD_DIM
    r = lax.broadcasted_iota(jnp.int32, (W, W), 0) // HEAD_DIM
    c = lax.broadcasted_iota(jnp.int32, (W, W), 1) // HEAD_DIM
    avg = jnp.where(r == c, 1.0 / HEAD_DIM, 0.0).astype(BF16)
    tri = (lax.broadcasted_iota(jnp.int32, (CHUNK, CHUNK), 1)
           <= lax.broadcasted_iota(jnp.int32, (CHUNK, CHUNK), 0))

    u = _gelu_tanh(_dot(x, w_ref[:, 0:W]))
    v = _gelu_tanh(_dot(x, w_ref[:, W:2 * W]))
    vc = v - _group_mean(v, avg)
    var = _group_mean(vc * vc, avg)
    vn = (vc * lax.rsqrt(var + LN_EPS) * sg_ref[...]).astype(BF16)
    for n in range(tm // CHUNK):
        rows = slice(n * CHUNK, (n + 1) * CHUNK)
        for p in range(N_HEADS // 2):
            cols = slice(p * LANES, (p + 1) * LANES)
            vp = vn[rows, cols]
            w_e = jnp.where(tri, sw_ref[2 * p], 0.0).astype(BF16)
            w_o = jnp.where(tri, sw_ref[2 * p + 1], 0.0).astype(BF16)
            mix = (_dot(w_e, jnp.where(lo_half, vp, 0)) + _dot(w_o, jnp.where(lo_half, 0, vp))
                   + sb_ref[:, cols])
            o_ref[rows, cols] = (u[rows, cols] * mix).astype(o_ref.dtype)

    g_b = _dot(x, w_ref[:, 2 * W:3 * W])
    y = _dot(x, w_ref[:, 3 * W:4 * W]) * _dot(x, w_ref[:, 4 * W:5 * W])
    conv = _causal_conv3(y, jnp.where(first, 0.0, tail_ref[...]), cw_ref[...])
    tail_ref[...] = y[tm - SUBLANES:tm, :]
    o_ref[:, W:2 * W] = (g_b * conv).astype(o_ref.dtype)


def _odd_mixer(xb, w_in, sgu_g, sgu_w, sgu_b_tile, conv_w, seq, tm):
    n, k = xb.shape
    W = MIX_HALF
    return pl.pallas_call(
        functools.partial(_odd_mixer_kernel, tiles_per_seq=seq // tm),
        grid=(n // tm,),
        in_specs=[pl.BlockSpec((tm, k), lambda i: (i, 0)),
                  pl.BlockSpec((k, 5 * W), lambda i: (0, 0)),
                  pl.BlockSpec((1, W), lambda i: (0, 0)),
                  pl.BlockSpec((N_HEADS, CHUNK, CHUNK), lambda i: (0, 0, 0)),
                  pl.BlockSpec((CHUNK, W), lambda i: (0, 0)),
                  pl.BlockSpec((CONV_W, W), lambda i: (0, 0))],
        out_specs=pl.BlockSpec((tm, 2 * W), lambda i: (i, 0)),
        out_shape=jax.ShapeDtypeStruct((n, 2 * W), BF16),
        scratch_shapes=[pltpu.VMEM((SUBLANES, W), F32)],
        compiler_params=_cparams(("arbitrary",)),
        name="odd_mixer",
    )(xb, w_in, sgu_g, sgu_w, sgu_b_tile, conv_w)


def _t5_bucket_np(dist):
    max_exact = N_BUCKETS // 2
    n = np.maximum(dist, 0)
    nf = np.maximum(n, 1).astype(np.float32)
    large = max_exact + (np.log(nf / max_exact) / np.float32(math.log(T5_MAX_DIST / max_exact))
                         * (N_BUCKETS - max_exact)).astype(np.int32)
    large = np.minimum(large, N_BUCKETS - 1)
    return np.where(n < max_exact, n, large)


def _bias_tables(rel_bias):
    T = BLK
    s = np.arange(T)[:, None]
    t = np.arange(T)[None, :]
    bucket = _t5_bucket_np(np.stack([t + T - s, t - s]))
    far_bucket = _t5_bucket_np(np.array([T + 1]))[0]
    assert far_bucket == N_BUCKETS - 1
    assert (_t5_bucket_np(np.arange(T + 1, 64 * T)) == far_bucket).all()
    rb = rel_bias.astype(F32) * LOG2E
    near = jnp.swapaxes(rb[bucket], 2, 3).reshape(2, T, N_HEADS * T)
    far = jnp.repeat(rb[far_bucket], T)[None, :]
    return near, far


def _placement_constants():
    H, L = N_HEADS, LANES
    pq = np.zeros((3 * L, H * L), np.float32)
    cq = np.zeros((1, H * L), np.float32)
    pk = np.zeros((3 * L, (H // 2) * L), np.float32)
    ck = np.zeros((1, (H // 2) * L), np.float32)
    for h in range(H):
        p, e = divmod(h, 2)
        for piece in range(3):
            pq[piece * L + h, h * L + 6 + piece] = 1.0
            cq[0, h * L + 3 * e + piece] = 1.0
            pk[piece * L + h, p * L + 3 * e + piece] = -1.0
            ck[0, p * L + 6 + piece] = 1.0
    return jnp.asarray(pq, BF16), jnp.asarray(cq), jnp.asarray(pk, BF16), jnp.asarray(ck)


def _even_weights(w_in, w_uk, w_uv):
    hd, H = HEAD_DIM, N_HEADS
    widths = (H * hd, KV_RANK, H * hd, hd, H, H * hd, H * hd, H * hd, H)
    offs = np.cumsum((0,) + widths)
    seg = lambda s: w_in[:, offs[s]:offs[s + 1]]
    q_a, c_kv, q_idx, k_idx, w_idx, q_b, k_b, v_b, f_in = [seg(s) for s in range(9)]
    scale = hd ** -0.5
    w_big = jnp.concatenate([q_a, q_idx * scale, q_b * (scale * LOG2E), k_b, v_b, c_kv,
                             k_idx, k_idx], axis=1).astype(BF16)
    pad = jnp.zeros((w_in.shape[0], LANES - 2 * H), w_in.dtype)
    w_small = jnp.concatenate([w_idx, f_in, pad], axis=1).astype(BF16)
    wuk_t = jnp.swapaxes(w_uk, 1, 2) * (scale * LOG2E)
    wuk_pad = jnp.zeros((H, LANES, KV_RANK), F32)
    wuv_pad = jnp.zeros((H, KV_RANK, LANES), F32)
    for h in range(H):
        o = hd * (h % 2)
        wuk_pad = wuk_pad.at[h, o:o + hd, :].set(wuk_t[h])
        wuv_pad = wuv_pad.at[h, :, o:o + hd].set(w_uv[h])
    return w_big, w_small, wuk_pad.astype(BF16), wuv_pad.astype(BF16)


def kernel(x, ln_g, ln_b, rel_bias, ev_w_in, ev_w_uk, ev_w_uv, ev_b_f, ev_w_o,
           od_w_in, od_sgu_g, od_sgu_w, od_sgu_b, od_conv_w, od_w_o,
           ffn_w_up, ffn_conv_w, ffn_w_down):
    bsz, seq, d = x.shape
    depth = ln_g.shape[0]
    n = bsz * seq
    alpha = (2.0 * depth) ** 0.25
    k_sel = min(TOPK_MAX, seq // 4)
    H = N_HEADS
    assert seq % BLK == 0 and seq % CHUNK == 0
    tm = min(512, seq)
    tm_ffn = min(1024, seq)

    bias_near, bias_far = _bias_tables(rel_bias)
    place = _placement_constants()
    xf = x.reshape(n, d)
    xb = xf.astype(BF16)
    for layer in range(depth):
        j = layer // 2
        g0, b0 = ln_g[layer, 0][None, :], ln_b[layer, 0][None, :]
        g1, b1 = ln_g[layer, 1][None, :], ln_b[layer, 1][None, :]
        if layer % 2 == 0:
            w_big, w_small, wuk_pad, wuv_pad = _even_weights(ev_w_in[j], ev_w_uk[j], ev_w_uv[j])
            hb, hs = _even_proj(xb, w_big, w_small, tm)
            hb = hb.reshape(bsz, seq, -1)
            hs_t = jnp.swapaxes(hs.reshape(bsz, seq, -1)[:, :, 0:2 * H], 1, 2)
            cum_f = _cumf(hs_t[:, H:2 * H], ev_b_f[j][:, None])
            f_rows = jnp.pad(jnp.swapaxes(cum_f, 1, 2), ((0, 0), (0, 0), (0, LANES - H)))
            mix = _attention(hb, hs_t[:, 0:H], f_rows, wuk_pad, wuv_pad, bias_near, bias_far,
                             place, k_sel)
            w_o = ev_w_o[j]
        else:
            sgu_b_tile = jnp.repeat(jnp.swapaxes(od_sgu_b[j], 0, 1), HEAD_DIM, axis=1)
            mix = _odd_mixer(xb, od_w_in[j].astype(BF16), od_sgu_g[j].reshape(1, -1), od_sgu_w[j],
                             sgu_b_tile, od_conv_w[j], seq, min(256, seq))
            w_o = od_w_o[j]
        xf, xb = _proj_ln(mix.reshape(n, d), w_o.astype(BF16), xf, g0, b0, alpha, tm)
        act = _ffn_up(xb, ffn_w_up[layer].astype(BF16), ffn_conv_w[layer], seq, tm_ffn, 256,
                      min(128, seq))
        xf, xb = _proj_ln(act, ffn_w_down[layer].astype(BF16), xf, g1, b1, alpha, tm)
    return xf.reshape(bsz, seq, d)
```

```python
import functools
import math

import numpy as np
import jax
import jax.numpy as jnp
from jax import lax
from jax.experimental import pallas as pl
from jax.experimental.pallas import tpu as pltpu

F32 = jnp.float32
BF16 = jnp.bfloat16

HEAD_DIM = 64
N_HEADS = 8
KV_RANK = 128
TOPK_MAX = 256
N_BUCKETS = 32
T5_MAX_DIST = 128
CHUNK = 128
CONV_W = 3
LN_EPS = 1e-5
MIX_HALF = N_HEADS * HEAD_DIM

LANES = 128
SUBLANES = 8
BF16_ROWS = 16
VMEM_LIMIT = 56 * 1024 * 1024

BLK = 128
PV_ROWS = KV_RANK + BF16_ROWS
NEG_BIG = -1e30
MASKED = -(2.0 ** 100)
F32_MAX = float(np.finfo(np.float32).max)
LOG2E = math.log2(math.e)


def _cparams(sem):
    return pltpu.CompilerParams(dimension_semantics=sem, vmem_limit_bytes=VMEM_LIMIT)


def _dot(a, b):
    return jnp.dot(a, b, preferred_element_type=F32)


def _dot_nt(a, b):
    return lax.dot_general(a, b, (((1,), (1,)), ((), ())), preferred_element_type=F32)


def _layer_norm_rows(y, g, b):
    mu = jnp.mean(y, axis=-1, keepdims=True)
    yc = y - mu
    var = jnp.mean(yc * yc, axis=-1, keepdims=True)
    return yc * lax.rsqrt(var + LN_EPS) * g + b


def _tree(op, x):
    while x.shape[0] > 1:
        half = x.shape[0] // 2
        x = op(x[:half], x[half:])
    return x[0]


def _bf16_pieces(v):
    hi = v.astype(BF16)
    r1 = v - hi.astype(F32)
    mid = r1.astype(BF16)
    lo = (r1 - mid.astype(F32)).astype(BF16)
    return jnp.concatenate([hi, mid, lo], axis=1)


def _even_proj_kernel(x_ref, wb_ref, ws_ref, hb_ref, hs_ref):
    x = x_ref[...]
    hb_ref[...] = _dot(x, wb_ref[...]).astype(hb_ref.dtype)
    hs_ref[...] = _dot(x, ws_ref[...])


def _even_proj(xb, w_big, w_small, tm):
    n, k = xb.shape
    mb = w_big.shape[1]
    ms = w_small.shape[1]
    return pl.pallas_call(
        _even_proj_kernel,
        grid=(n // tm,),
        in_specs=[pl.BlockSpec((tm, k), lambda i: (i, 0)),
                  pl.BlockSpec((k, mb), lambda i: (0, 0)),
                  pl.BlockSpec((k, ms), lambda i: (0, 0))],
        out_specs=[pl.BlockSpec((tm, mb), lambda i: (i, 0)),
                   pl.BlockSpec((tm, ms), lambda i: (i, 0))],
        out_shape=[jax.ShapeDtypeStruct((n, mb), BF16),
                   jax.ShapeDtypeStruct((n, ms), F32)],
        compiler_params=_cparams(("parallel",)),
        name="even_proj",
    )(xb, w_big, w_small)


def _cumf_kernel(f_ref, b_ref, o_ref):
    seq = f_ref.shape[-1]
    row = lax.broadcasted_iota(jnp.int32, (LANES, LANES), 0)
    col = lax.broadcasted_iota(jnp.int32, (LANES, LANES), 1)
    upper = jnp.where(row <= col, 1.0, 0.0).astype(F32)
    carry = jnp.zeros((f_ref.shape[0], 1), F32)
    for c in range(seq // LANES):
        z = f_ref[:, c * LANES:(c + 1) * LANES] + b_ref[...]
        lf = jnp.minimum(z, 0.0) - jnp.log1p(jnp.exp(-jnp.abs(z)))
        cs = jnp.dot(lf, upper, preferred_element_type=F32,
                     precision=lax.Precision.HIGHEST) + carry
        o_ref[:, c * LANES:(c + 1) * LANES] = cs
        carry = cs[:, LANES - 1:LANES]


def _cumf(f_t, b_f):
    bsz, nh, seq = f_t.shape
    return pl.pallas_call(
        _cumf_kernel,
        grid=(bsz,),
        in_specs=[pl.BlockSpec((None, nh, seq), lambda b: (b, 0, 0)),
                  pl.BlockSpec((nh, 1), lambda b: (0, 0))],
        out_specs=pl.BlockSpec((None, nh, seq), lambda b: (b, 0, 0)),
        out_shape=jax.ShapeDtypeStruct((bsz, nh, seq), F32),
        compiler_params=_cparams(("parallel",)),
        name="fox_cumf",
    )(f_t, b_f)


def _attn_kernel(qa_ref, qi_ref, qb_ref, wt_ref, fq_ref, kb_ref, vb_ref, ckv_ref, kidx_ref, fk_ref,
                 wuk_ref, wuv_ref, bnear_ref, bfar_ref, pq_ref, cq_ref, pk_ref, ck_ref,
                 o_ref,
                 score_ref, lhsd_ref, ct_ref, kaug_ref, vt_ref, qsel_ref, qaug_ref, qim_ref,
                 m_ref, acc_ref, s_ref, *, k_sel):
    T = BLK
    H = N_HEADS
    NP = H // 2
    G = T // SUBLANES
    i = pl.program_id(1)
    nb = i + 1
    nblk = ct_ref.shape[0]
    lane = lax.broadcasted_iota(jnp.int32, (T, LANES), 1)
    lo_half = lane < HEAD_DIM
    key_i = lax.broadcasted_iota(jnp.int32, (T, T), 0)
    qry_i = lax.broadcasted_iota(jnp.int32, (T, T), 1)
    ones_rows = jnp.ones((BF16_ROWS, T), BF16)

    @pl.when(i == 0)
    def _per_batch():
        def prep(j, c):
            r = pl.ds(pl.multiple_of(j * T, T), T)
            cb = ckv_ref[r, :]
            lhsd_ref[r, 0:KV_RANK] = cb
            ct_ref[j, 0:KV_RANK, :] = cb.astype(F32).T.astype(BF16)
            ct_ref[j, KV_RANK:PV_ROWS, :] = ones_rows
            ext = (_dot(_bf16_pieces(fk_ref[r, :] * LOG2E), pk_ref[...]) + ck_ref[...]).astype(BF16)
            for p in range(NP):
                cols = slice(p * LANES, (p + 1) * LANES)
                kaug_ref[p, r, 0:LANES] = kb_ref[r, cols]
                kaug_ref[p, r, LANES:2 * LANES] = ext[:, cols]
                vt_ref[p, j, 0:LANES, :] = vb_ref[r, cols].astype(F32).T.astype(BF16)
                vt_ref[p, j, LANES:PV_ROWS, :] = ones_rows
            return c
        lax.fori_loop(0, nblk, prep, 0)

    w_rows = wt_ref[...] * (H ** -0.5)
    ext_q = (_dot(_bf16_pieces(fq_ref[...] * LOG2E), pq_ref[...]) + cq_ref[...]).astype(BF16)
    eye = jnp.where(key_i == qry_i, 1.0, 0.0).astype(BF16)
    for p in range(NP):
        cols = slice(p * LANES, (p + 1) * LANES)
        qa_pair, qi_pair, qb_pair = qa_ref[:, cols], qi_ref[:, cols], qb_ref[:, cols]
        for e in range(2):
            h = 2 * p + e
            rows = slice(e * T, (e + 1) * T)
            mine = lo_half if e == 0 else jnp.logical_not(lo_half)
            qsel_ref[p, rows, 0:KV_RANK] = _dot(qa_pair, wuk_ref[h]).astype(BF16)
            qsel_ref[p, rows, KV_RANK:2 * KV_RANK] = eye
            qaug_ref[p, rows, 0:LANES] = jnp.where(mine, qb_pair, 0)
            qaug_ref[p, rows, LANES:2 * LANES] = ext_q[:, h * LANES:(h + 1) * LANES]
            qim_ref[p, rows, :] = jnp.where(mine, qi_pair, 0)

    def reset_state():
        m_ref[...] = jnp.full(m_ref.shape, NEG_BIG, F32)
        acc_ref[...] = jnp.zeros(acc_ref.shape, F32)

    def online_update(p, s, pv_lhs):
        cols = slice(p * 2 * T, (p + 1) * 2 * T)
        m_old = m_ref[:, cols]
        s_max = _tree(jnp.maximum, s.reshape(-1, SUBLANES, 2 * T))
        m_new = jnp.maximum(m_old, jnp.max(s_max, axis=0, keepdims=True))
        alpha = jnp.exp2(m_old - m_new)
        pr = jnp.exp2((s - m_new[0:1, :]).astype(BF16))
        acc_ref[:, cols] = acc_ref[:, cols] * alpha[0:1, :] + _dot(pv_lhs, pr)
        m_ref[:, cols] = m_new

    def key_rows(j, nblocks):
        return pl.ds(pl.multiple_of(j * T, T), nblocks * T)

    def blocks_t(load, j, nblocks):
        return load(j) if nblocks == 1 else jnp.concatenate([load(j + b) for b in range(nblocks)], axis=1)

    def for_far_blocks(n_far, logits, absorb):
        n_quad = lax.shift_right_logical(n_far, 2)

        def quad_body(jj, c):
            logits(4 * jj, 2, 0)
            logits(4 * jj + 2, 2, 1)
            absorb(4 * jj, 2, 0)
            absorb(4 * jj + 2, 2, 1)
            return c
        lax.fori_loop(0, n_quad, quad_body, 0)

        @pl.when((n_far & 2) != 0)
        def _pair():
            logits(4 * n_quad, 2, 0)
            absorb(4 * n_quad, 2, 0)

        @pl.when((n_far & 1) != 0)
        def _single():
            logits(n_far - 1, 1, 0)
            absorb(n_far - 1, 1, 0)

    def slot_rows(nblocks):
        return slice(0, nblocks * T)

    n_far = jnp.maximum(i - 1, 0)
    causal_tail = jnp.concatenate([key_i >= 0, key_i <= qry_i], axis=0)

    def idx_scores(j, nblocks):
        k_blk = kidx_ref[key_rows(j, nblocks), :]
        acc = jnp.zeros((nblocks * T, T), F32)
        for p in range(NP):
            s = _dot_nt(k_blk, qim_ref[p])
            for e in range(2):
                h = 2 * p + e
                acc = acc + w_rows[h:h + 1, :] * jnp.maximum(s[:, e * T:(e + 1) * T], 0.0)
        return acc

    def fox_logits(j, nblocks, slot, mask=None):
        for p in range(NP):
            s = _dot_nt(kaug_ref[p, key_rows(j, nblocks), :], qaug_ref[p])
            if mask is not None:
                s = jnp.where(jnp.concatenate([mask, mask], axis=1), s, MASKED)
            s_ref[slot, slot_rows(nblocks), p * 2 * T:(p + 1) * 2 * T] = s

    def fox_absorb(j, nblocks, slot):
        for p in range(NP):
            online_update(p, s_ref[slot, slot_rows(nblocks), p * 2 * T:(p + 1) * 2 * T],
                          blocks_t(lambda jb: vt_ref[p, jb], j, nblocks))

    def idx_fox_logits(j, nblocks, slot):
        score_ref[pl.ds(j, nblocks)] = idx_scores(j, nblocks).reshape(nblocks, T, T)
        fox_logits(j, nblocks, slot)

    reset_state()
    for_far_blocks(n_far, idx_fox_logits, fox_absorb)

    @pl.when(i >= 1)
    def _idx_fox_tail():
        score_ref[pl.ds(i - 1, 2)] = (
            jnp.where(causal_tail, idx_scores(i - 1, 2), -jnp.inf).reshape(2, T, T))
        fox_logits(i - 1, 2, 0, causal_tail)
        fox_absorb(i - 1, 2, 0)

    @pl.when(i == 0)
    def _idx_fox_first():
        score_ref[0] = jnp.where(key_i <= qry_i, idx_scores(0, 1), -jnp.inf)
        fox_logits(0, 1, 0, key_i <= qry_i)
        fox_absorb(0, 1, 0)

    score_ref[i + 1] = jnp.full((T, T), -jnp.inf, F32)

    row_lo = lax.broadcasted_iota(jnp.int32, (LANES, T), 0) < HEAD_DIM
    for p in range(NP):
        c0 = slice(p * 2 * T, p * 2 * T + T)
        c1 = slice(p * 2 * T + T, (p + 1) * 2 * T)
        o_t = jnp.where(row_lo, acc_ref[0:LANES, c0] / acc_ref[LANES:LANES + 1, c0],
                        acc_ref[0:LANES, c1] / acc_ref[LANES:LANES + 1, c1])
        o_ref[:, MIX_HALF + p * LANES:MIX_HALF + (p + 1) * LANES] = o_t.T.astype(o_ref.dtype)

    kf = float(k_sel)
    t_pos = (i * T + lax.broadcasted_iota(jnp.int32, (SUBLANES, LANES), 1)).astype(F32)
    searching = (t_pos + 1.0) > kf
    G2 = 2 * G
    n_pair = i // 2 + 1
    key_in_pair = (lax.broadcasted_iota(jnp.int32, (G2, SUBLANES, LANES), 0) * SUBLANES
                   + lax.broadcasted_iota(jnp.int32, (G2, SUBLANES, LANES), 1))

    def rep(v):
        return jnp.broadcast_to(v, (SUBLANES, LANES))

    def score_pair(jj):
        return score_ref[pl.ds(2 * jj, 2)].reshape(G2, SUBLANES, LANES)

    def count(*pred_fns):
        def body(jj, cs):
            s = score_pair(jj)
            return tuple(c + _tree(jnp.add, jnp.where(f(s, jj), 1.0, 0.0)) for c, f in zip(cs, pred_fns))
        zero = jnp.zeros((SUBLANES, LANES), F32)
        cs = lax.fori_loop(0, n_pair, body, (zero,) * len(pred_fns))
        return tuple(rep(jnp.sum(c, axis=0, keepdims=True)) for c in cs)

    def count_ge(thr):
        return count(lambda s, jj: s >= thr[None])[0]

    def select_threshold():
        def mm_body(jj, carry):
            mn, mx = carry
            s = score_pair(jj)
            return (jnp.minimum(mn, _tree(jnp.minimum, jnp.where(s == -jnp.inf, jnp.inf, s))),
                    jnp.maximum(mx, _tree(jnp.maximum, s)))

        mn, mx = lax.fori_loop(0, n_pair, mm_body,
                               (jnp.full((SUBLANES, LANES), jnp.inf, F32),
                                jnp.full((SUBLANES, LANES), -jnp.inf, F32)))
        mn = rep(jnp.min(mn, axis=0, keepdims=True))
        mx = rep(jnp.max(mx, axis=0, keepdims=True))
        above_mx = mx + jnp.maximum(jnp.abs(mx) * 2.0 ** -20, 1e-30)
        c_ge0, c_gt0 = count(lambda s, jj: s >= 0.0, lambda s, jj: s > 0.0)
        thr_ge0 = c_ge0 >= kf
        thr_gt0 = c_gt0 >= kf
        lo0 = jnp.where(thr_ge0, 0.0, mn)
        cnt_lo0 = jnp.where(thr_ge0, c_ge0, t_pos + 1.0)
        hi0 = jnp.where(thr_gt0, above_mx, 0.0)
        cnt_hi0 = jnp.where(thr_gt0, 0.0, jnp.where(thr_ge0, c_gt0, c_ge0))
        undecided = thr_gt0 | jnp.logical_not(thr_ge0)
        active0 = jnp.where(searching & undecided & (cnt_lo0 > kf), 1.0, 0.0)

        def cond(st):
            return jnp.logical_and(st[5] > 0.0, st[6] < 2200)

        def body(st):
            lo, hi, cnt_lo, cnt_hi, active, _, it = st
            any_active = jnp.max(active)
            mid = 0.5 * lo + 0.5 * hi
            adjacent = (mid <= lo) | (mid >= hi)
            c = count_ge(mid)
            act = active > 0.0
            go_lo = act & jnp.logical_not(adjacent) & (c >= kf)
            go_hi = act & jnp.logical_not(adjacent) & (c < kf)
            lo = jnp.where(go_lo, mid, lo)
            cnt_lo = jnp.where(go_lo, c, cnt_lo)
            hi = jnp.where(go_hi, mid, hi)
            cnt_hi = jnp.where(go_hi, c, cnt_hi)
            active = jnp.where(act & jnp.logical_not(adjacent) & (cnt_lo > kf), 1.0, 0.0)
            return lo, hi, cnt_lo, cnt_hi, active, any_active, it + 1

        lo, hi, cnt_lo, cnt_hi, _, _, _ = lax.while_loop(
            cond, body, (lo0, hi0, cnt_lo0, cnt_hi0, active0, jnp.float32(1.0), 0))
        thr = jnp.where(searching, lo, -F32_MAX)

        tied = searching & (cnt_lo > kf)
        need = kf - cnt_hi

        @pl.when(jnp.max(jnp.where(tied, 1.0, 0.0)) > 0.0)
        def _break_ties():
            def key_index(jj):
                return (2 * jj * T + key_in_pair).astype(F32)

            def count_tied_upto(jcut):
                return count(lambda s, jj: (s == thr[None]) & (key_index(jj) <= jcut[None]))[0]

            def bs_body(_, st):
                jl, jh = st
                jm = jnp.floor(0.5 * (jl + jh))
                ok = count_tied_upto(jm) >= need
                return jnp.where(ok, jl, jm), jnp.where(ok, jm, jh)

            n_keys = (nb * T).astype(F32)
            jl0 = jnp.full((SUBLANES, LANES), -1.0, F32)
            jh0 = jnp.zeros((SUBLANES, LANES), F32) + (n_keys - 1.0)
            n_steps = int(math.ceil(math.log2(nblk * T))) + 1
            _, jcut = lax.fori_loop(0, n_steps, bs_body, (jl0, jh0))

            def drop_body(jj, c):
                s = score_pair(jj)
                drop = tied[None] & (s == thr[None]) & (key_index(jj) > jcut[None])
                score_ref[pl.ds(2 * jj, 2)] = jnp.where(drop, -jnp.inf, s).reshape(2, T, T)
                return c
            lax.fori_loop(0, n_pair, drop_body, 0)

        return thr

    thr = lax.cond((i + 1) * T > k_sel, select_threshold,
                   lambda: jnp.full((SUBLANES, LANES), -F32_MAX, F32))

    def mask_body(jj, c):
        lhsd_ref[key_rows(2 * jj, 2), KV_RANK:2 * KV_RANK] = (
            jnp.where(score_pair(jj) >= thr[None], 0.0, MASKED).reshape(2 * T, T).astype(BF16))
        return c

    lax.fori_loop(0, n_pair, mask_body, 0)

    def dsa_logits(j, nblocks, slot, bias=None):
        lhs = lhsd_ref[key_rows(j, nblocks), :]
        for p in range(NP):
            cols = slice(p * 2 * T, (p + 1) * 2 * T)
            s = _dot_nt(lhs, qsel_ref[p])
            s_ref[slot, slot_rows(nblocks), cols] = s if bias is None else s + bias(cols)

    def dsa_absorb(j, nblocks, slot):
        pv_lhs = blocks_t(lambda jb: ct_ref[jb], j, nblocks)
        for p in range(NP):
            online_update(p, s_ref[slot, slot_rows(nblocks), p * 2 * T:(p + 1) * 2 * T], pv_lhs)

    reset_state()
    for_far_blocks(n_far, dsa_logits, dsa_absorb)
    m_ref[...] = m_ref[...] + bfar_ref[...]

    @pl.when(i >= 1)
    def _dsa_tail():
        dsa_logits(i - 1, 2, 0, lambda cols: bnear_ref[:, :, cols].reshape(2 * T, 2 * T))
        dsa_absorb(i - 1, 2, 0)

    @pl.when(i == 0)
    def _dsa_first():
        dsa_logits(0, 1, 0, lambda cols: bnear_ref[1, :, cols])
        dsa_absorb(0, 1, 0)

    for p in range(NP):
        o_pair = jnp.zeros((T, LANES), F32)
        for e in range(2):
            cols = slice(p * 2 * T + e * T, p * 2 * T + (e + 1) * T)
            o_t = acc_ref[0:KV_RANK, cols] / acc_ref[KV_RANK:KV_RANK + 1, cols]
            o_pair = o_pair + _dot(o_t.T.astype(BF16), wuv_ref[2 * p + e])
        o_ref[:, p * LANES:(p + 1) * LANES] = o_pair.astype(o_ref.dtype)


_EV_QA, _EV_QI, _EV_QB, _EV_KB, _EV_VB = 0, 1, 2, 3, 4
_EV_CKV, _EV_KIDX = 20, 21


def _attention(hb, w_t, f_rows, wuk_pad, wuv_pad, bias_near, bias_far, place, k_sel):
    bsz, seq, _ = hb.shape
    T = BLK
    nblk = seq // T
    H = N_HEADS
    pq, cq, pk, ck = place
    wide = lambda c: pl.BlockSpec((None, T, MIX_HALF), lambda b, i, c=c: (b, i, c))
    once = dict(pipeline_mode=pl.Buffered(1))
    full_wide = lambda c: pl.BlockSpec((None, seq, MIX_HALF), lambda b, i, c=c: (b, 0, c), **once)
    full_narrow = lambda c: pl.BlockSpec((None, seq, LANES), lambda b, i, c=c: (b, 0, c), **once)
    const = lambda a: pl.BlockSpec(a.shape, lambda b, i, nd=a.ndim: (0,) * nd)
    return pl.pallas_call(
        functools.partial(_attn_kernel, k_sel=k_sel),
        grid=(bsz, nblk),
        in_specs=[wide(_EV_QA), wide(_EV_QI), wide(_EV_QB),
                  pl.BlockSpec((None, H, T), lambda b, i: (b, 0, i)),
                  pl.BlockSpec((None, T, LANES), lambda b, i: (b, i, 0)),
                  full_wide(_EV_KB), full_wide(_EV_VB),
                  full_narrow(_EV_CKV), full_narrow(_EV_KIDX),
                  pl.BlockSpec((None, seq, LANES), lambda b, i: (b, 0, 0), **once),
                  const(wuk_pad), const(wuv_pad), const(bias_near), const(bias_far),
                  const(pq), const(cq), const(pk), const(ck)],
        out_specs=pl.BlockSpec((None, T, 2 * MIX_HALF), lambda b, i: (b, i, 0)),
        out_shape=jax.ShapeDtypeStruct((bsz, seq, 2 * MIX_HALF), BF16),
        scratch_shapes=[pltpu.VMEM((nblk + 2, T, T), F32),
                        pltpu.VMEM((seq + 2 * T, 2 * KV_RANK), BF16),
                        pltpu.VMEM((nblk, PV_ROWS, T), BF16),
                        pltpu.VMEM((H // 2, seq, 2 * LANES), BF16),
                        pltpu.VMEM((H // 2, nblk, PV_ROWS, T), BF16),
                        pltpu.VMEM((H // 2, 2 * T, 2 * KV_RANK), BF16),
                        pltpu.VMEM((H // 2, 2 * T, 2 * LANES), BF16),
                        pltpu.VMEM((H // 2, 2 * T, LANES), BF16),
                        pltpu.VMEM((SUBLANES, H * T), F32),
                        pltpu.VMEM((PV_ROWS, H * T), F32),
                        pltpu.VMEM((2, 2 * T, H * T), F32)],
        compiler_params=_cparams(("parallel", "arbitrary")),
        name="dsa_fox_attention",
    )(hb, hb, hb, w_t, f_rows, hb, hb, hb, hb, f_rows, wuk_pad, wuv_pad, bias_near, bias_far,
      pq, cq, pk, ck)


def _proj_ln_kernel(a_ref, w_ref, x_ref, g_ref, b_ref, xo_ref, xb_ref, *, alpha, sub):
    for c in range(a_ref.shape[0] // sub):
        rows = slice(c * sub, (c + 1) * sub)
        y = alpha * x_ref[rows, :] + _dot(a_ref[rows, :], w_ref[...])
        out = _layer_norm_rows(y, g_ref[...], b_ref[...])
        xo_ref[rows, :] = out
        xb_ref[rows, :] = out.astype(BF16)


def _proj_ln(a, w, x, g, b, alpha, tm):
    n, k = a.shape
    d = w.shape[1]
    return pl.pallas_call(
        functools.partial(_proj_ln_kernel, alpha=alpha, sub=min(128, tm)),
        grid=(n // tm,),
        in_specs=[pl.BlockSpec((tm, k), lambda i: (i, 0)),
                  pl.BlockSpec((k, d), lambda i: (0, 0)),
                  pl.BlockSpec((tm, d), lambda i: (i, 0)),
                  pl.BlockSpec((1, d), lambda i: (0, 0)),
                  pl.BlockSpec((1, d), lambda i: (0, 0))],
        out_specs=[pl.BlockSpec((tm, d), lambda i: (i, 0)),
                   pl.BlockSpec((tm, d), lambda i: (i, 0))],
        out_shape=[jax.ShapeDtypeStruct((n, d), F32),
                   jax.ShapeDtypeStruct((n, d), BF16)],
        compiler_params=_cparams(("parallel",)),
        name="proj_residual_ln",
    )(a, w, x, g, b)


def _causal_conv3(h, tail, cw):
    w0, w1, w2 = cw[0:1, :], cw[1:2, :], cw[2:3, :]
    body = w0 * pltpu.roll(h, 2, axis=0) + w1 * pltpu.roll(h, 1, axis=0) + w2 * h
    head = h[0:SUBLANES, :]
    r = lax.broadcasted_iota(jnp.int32, head.shape, 0)
    hm1 = jnp.where(r < 1, pltpu.roll(tail, 1, axis=0), pltpu.roll(head, 1, axis=0))
    hm2 = jnp.where(r < 2, pltpu.roll(tail, 2, axis=0), pltpu.roll(head, 2, axis=0))
    out_head = w0 * hm2 + w1 * hm1 + w2 * head
    return jnp.concatenate([out_head, body[SUBLANES:, :]], axis=0)


def _ffn_up_kernel(x_ref, wg_ref, wv_ref, cg_ref, cv_ref, o_ref, tg_ref, tv_ref, *h_refs,
                   tiles_per_seq, sub):
    n_chunks = x_ref.shape[0] // sub
    hg_refs, hv_refs = h_refs[:n_chunks], h_refs[n_chunks:]
    first = (pl.program_id(1) % tiles_per_seq) == 0
    body_rows = slice(SUBLANES, SUBLANES + sub)

    def matmuls(c):
        x = x_ref[c * sub:(c + 1) * sub, :]
        hg_refs[c][body_rows, :] = _dot(x, wg_ref[...])
        hv_refs[c][body_rows, :] = _dot(x, wv_ref[...])

    def conv(h_ref, cw):
        return (cw[0:1, :] * h_ref[SUBLANES - 2:SUBLANES - 2 + sub, :]
                + cw[1:2, :] * h_ref[SUBLANES - 1:SUBLANES - 1 + sub, :]
                + cw[2:3, :] * h_ref[body_rows, :])

    def gate(c):
        if c == 0:
            hg_refs[0][0:SUBLANES, :] = jnp.where(first, 0.0, tg_ref[...])
            hv_refs[0][0:SUBLANES, :] = jnp.where(first, 0.0, tv_ref[...])
        else:
            hg_refs[c][0:SUBLANES, :] = hg_refs[c - 1][sub:sub + SUBLANES, :]
            hv_refs[c][0:SUBLANES, :] = hv_refs[c - 1][sub:sub + SUBLANES, :]
        g = conv(hg_refs[c], cg_ref[...])
        v = conv(hv_refs[c], cv_ref[...])
        o_ref[c * sub:(c + 1) * sub, :] = (g / (1.0 + jnp.exp(-g)) * v).astype(o_ref.dtype)

    matmuls(0)
    for c in range(n_chunks):
        if c + 1 < n_chunks:
            matmuls(c + 1)
        gate(c)
    tg_ref[...] = hg_refs[n_chunks - 1][sub:sub + SUBLANES, :]
    tv_ref[...] = hv_refs[n_chunks - 1][sub:sub + SUBLANES, :]


def _ffn_up(xb, w_up, conv_w, seq, tm, tn, sub):
    n, k = xb.shape
    dff = w_up.shape[1] // 2
    ncol = dff // tn
    return pl.pallas_call(
        functools.partial(_ffn_up_kernel, tiles_per_seq=seq // tm, sub=sub),
        grid=(ncol, n // tm),
        in_specs=[pl.BlockSpec((tm, k), lambda j, i: (i, 0)),
                  pl.BlockSpec((k, tn), lambda j, i: (0, j)),
                  pl.BlockSpec((k, tn), lambda j, i, ncol=ncol: (0, j + ncol)),
                  pl.BlockSpec((CONV_W, tn), lambda j, i: (0, j)),
                  pl.BlockSpec((CONV_W, tn), lambda j, i, ncol=ncol: (0, j + ncol))],
        out_specs=pl.BlockSpec((tm, tn), lambda j, i: (i, j)),
        out_shape=jax.ShapeDtypeStruct((n, dff), BF16),
        scratch_shapes=([pltpu.VMEM((SUBLANES, tn), F32)] * 2
                        + [pltpu.VMEM((sub + SUBLANES, tn), F32)] * (2 * (tm // sub))),
        compiler_params=_cparams(("arbitrary", "arbitrary")),
        name="ffn_up_conv_gate",
    )(xb, w_up, w_up, conv_w, conv_w)


def _gelu_tanh(x):
    return 0.5 * x * (1.0 + jnp.tanh(math.sqrt(2.0 / math.pi) * (x + 0.044715 * (x * x * x))))


def _group_mean(v, avg):
    hi = v.astype(BF16)
    lo = (v - hi.astype(F32)).astype(BF16)
    return _dot(hi, avg) + _dot(lo, avg)


def _odd_mixer_kernel(x_ref, w_ref, sg_ref, sw_ref, sb_ref, cw_ref, o_ref, tail_ref, *, tiles_per_seq):
    tm = x_ref.shape[0]
    W = MIX_HALF
    first = (pl.program_id(0) % tiles_per_seq) == 0
    x = x_ref[...]
    lane = lax.broadcasted_iota(jnp.int32, (CHUNK, LANES), 1)
    lo_half = lane < HEAD_DIM
    r = lax.broadcasted_iota(jnp.int32, (W, W), 0) // HEAD_DIM
    c = lax.broadcasted_iota(jnp.int32, (W, W), 1) // HEAD_DIM
    avg = jnp.where(r == c, 1.0 / HEAD_DIM, 0.0).astype(BF16)
    tri = (lax.broadcasted_iota(jnp.int32, (CHUNK, CHUNK), 1)
           <= lax.broadcasted_iota(jnp.int32, (CHUNK, CHUNK), 0))

    u = _gelu_tanh(_dot(x, w_ref[:, 0:W]))
    v = _gelu_tanh(_dot(x, w_ref[:, W:2 * W]))
    vc = v - _group_mean(v, avg)
    var = _group_mean(vc * vc, avg)
    vn = (vc * lax.rsqrt(var + LN_EPS) * sg_ref[...]).astype(BF16)
    for n in range(tm // CHUNK):
        rows = slice(n * CHUNK, (n + 1) * CHUNK)
        for p in range(N_HEADS // 2):
            cols = slice(p * LANES, (p + 1) * LANES)
            vp = vn[rows, cols]
            w_e = jnp.where(tri, sw_ref[2 * p], 0.0).astype(BF16)
            w_o = jnp.where(tri, sw_ref[2 * p + 1], 0.0).astype(BF16)
            mix = (_dot(w_e, jnp.where(lo_half, vp, 0)) + _dot(w_o, jnp.where(lo_half, 0, vp))
                   + sb_ref[:, cols])
            o_ref[rows, cols] = (u[rows, cols] * mix).astype(o_ref.dtype)

    g_b = _dot(x, w_ref[:, 2 * W:3 * W])
    y = _dot(x, w_ref[:, 3 * W:4 * W]) * _dot(x, w_ref[:, 4 * W:5 * W])
    conv = _causal_conv3(y, jnp.where(first, 0.0, tail_ref[...]), cw_ref[...])
    tail_ref[...] = y[tm - SUBLANES:tm, :]
    o_ref[:, W:2 * W] = (g_b * conv).astype(o_ref.dtype)


def _odd_mixer(xb, w_in, sgu_g, sgu_w, sgu_b_tile, conv_w, seq, tm):
    n, k = xb.shape
    W = MIX_HALF
    return pl.pallas_call(
        functools.partial(_odd_mixer_kernel, tiles_per_seq=seq // tm),
        grid=(n // tm,),
        in_specs=[pl.BlockSpec((tm, k), lambda i: (i, 0)),
                  pl.BlockSpec((k, 5 * W), lambda i: (0, 0)),
                  pl.BlockSpec((1, W), lambda i: (0, 0)),
                  pl.BlockSpec((N_HEADS, CHUNK, CHUNK), lambda i: (0, 0, 0)),
                  pl.BlockSpec((CHUNK, W), lambda i: (0, 0)),
                  pl.BlockSpec((CONV_W, W), lambda i: (0, 0))],
        out_specs=pl.BlockSpec((tm, 2 * W), lambda i: (i, 0)),
        out_shape=jax.ShapeDtypeStruct((n, 2 * W), BF16),
        scratch_shapes=[pltpu.VMEM((SUBLANES, W), F32)],
        compiler_params=_cparams(("arbitrary",)),
        name="odd_mixer",
    )(xb, w_in, sgu_g, sgu_w, sgu_b_tile, conv_w)


def _t5_bucket_np(dist):
    max_exact = N_BUCKETS // 2
    n = np.maximum(dist, 0)
    nf = np.maximum(n, 1).astype(np.float32)
    large = max_exact + (np.log(nf / max_exact) / np.float32(math.log(T5_MAX_DIST / max_exact))
                         * (N_BUCKETS - max_exact)).astype(np.int32)
    large = np.minimum(large, N_BUCKETS - 1)
    return np.where(n < max_exact, n, large)


def _bias_tables(rel_bias):
    T = BLK
    s = np.arange(T)[:, None]
    t = np.arange(T)[None, :]
    bucket = _t5_bucket_np(np.stack([t + T - s, t - s]))
    far_bucket = _t5_bucket_np(np.array([T + 1]))[0]
    assert far_bucket == N_BUCKETS - 1
    assert (_t5_bucket_np(np.arange(T + 1, 64 * T)) == far_bucket).all()
    rb = rel_bias.astype(F32) * LOG2E
    near = jnp.swapaxes(rb[bucket], 2, 3).reshape(2, T, N_HEADS * T)
    far = jnp.repeat(rb[far_bucket], T)[None, :]
    return near, far


def _placement_constants():
    H, L = N_HEADS, LANES
    pq = np.zeros((3 * L, H * L), np.float32)
    cq = np.zeros((1, H * L), np.float32)
    pk = np.zeros((3 * L, (H // 2) * L), np.float32)
    ck = np.zeros((1, (H // 2) * L), np.float32)
    for h in range(H):
        p, e = divmod(h, 2)
        for piece in range(3):
            pq[piece * L + h, h * L + 6 + piece] = 1.0
            cq[0, h * L + 3 * e + piece] = 1.0
            pk[piece * L + h, p * L + 3 * e + piece] = -1.0
            ck[0, p * L + 6 + piece] = 1.0
    return jnp.asarray(pq, BF16), jnp.asarray(cq), jnp.asarray(pk, BF16), jnp.asarray(ck)


def _even_weights(w_in, w_uk, w_uv):
    hd, H = HEAD_DIM, N_HEADS
    widths = (H * hd, KV_RANK, H * hd, hd, H, H * hd, H * hd, H * hd, H)
    offs = np.cumsum((0,) + widths)
    seg = lambda s: w_in[:, offs[s]:offs[s + 1]]
    q_a, c_kv, q_idx, k_idx, w_idx, q_b, k_b, v_b, f_in = [seg(s) for s in range(9)]
    scale = hd ** -0.5
    w_big = jnp.concatenate([q_a, q_idx * scale, q_b * (scale * LOG2E), k_b, v_b, c_kv,
                             k_idx, k_idx], axis=1).astype(BF16)
    pad = jnp.zeros((w_in.shape[0], LANES - 2 * H), w_in.dtype)
    w_small = jnp.concatenate([w_idx, f_in, pad], axis=1).astype(BF16)
    wuk_t = jnp.swapaxes(w_uk, 1, 2) * (scale * LOG2E)
    wuk_pad = jnp.zeros((H, LANES, KV_RANK), F32)
    wuv_pad = jnp.zeros((H, KV_RANK, LANES), F32)
    for h in range(H):
        o = hd * (h % 2)
        wuk_pad = wuk_pad.at[h, o:o + hd, :].set(wuk_t[h])
        wuv_pad = wuv_pad.at[h, :, o:o + hd].set(w_uv[h])
    return w_big, w_small, wuk_pad.astype(BF16), wuv_pad.astype(BF16)


def kernel(x, ln_g, ln_b, rel_bias, ev_w_in, ev_w_uk, ev_w_uv, ev_b_f, ev_w_o,
           od_w_in, od_sgu_g, od_sgu_w, od_sgu_b, od_conv_w, od_w_o,
           ffn_w_up, ffn_conv_w, ffn_w_down):
    bsz, seq, d = x.shape
    depth = ln_g.shape[0]
    n = bsz * seq
    alpha = (2.0 * depth) ** 0.25
    k_sel = min(TOPK_MAX, seq // 4)
    H = N_HEADS
    assert seq % BLK == 0 and seq % CHUNK == 0
    tm = min(512, seq)
    tm_ffn = min(1024, seq)

    bias_near, bias_far = _bias_tables(rel_bias)
    place = _placement_constants()
    xf = x.reshape(n, d)
    xb = xf.astype(BF16)
    for layer in range(depth):
        j = layer // 2
        g0, b0 = ln_g[layer, 0][None, :], ln_b[layer, 0][None, :]
        g1, b1 = ln_g[layer, 1][None, :], ln_b[layer, 1][None, :]
        if layer % 2 == 0:
            w_big, w_small, wuk_pad, wuv_pad = _even_weights(ev_w_in[j], ev_w_uk[j], ev_w_uv[j])
            hb, hs = _even_proj(xb, w_big, w_small, tm)
            hb = hb.reshape(bsz, seq, -1)
            hs_t = jnp.swapaxes(hs.reshape(bsz, seq, -1)[:, :, 0:2 * H], 1, 2)
            cum_f = _cumf(hs_t[:, H:2 * H], ev_b_f[j][:, None])
            f_rows = jnp.pad(jnp.swapaxes(cum_f, 1, 2), ((0, 0), (0, 0), (0, LANES - H)))
            mix = _attention(hb, hs_t[:, 0:H], f_rows, wuk_pad, wuv_pad, bias_near, bias_far,
                             place, k_sel)
            w_o = ev_w_o[j]
        else:
            sgu_b_tile = jnp.repeat(jnp.swapaxes(od_sgu_b[j], 0, 1), HEAD_DIM, axis=1)
            mix = _odd_mixer(xb, od_w_in[j].astype(BF16), od_sgu_g[j].reshape(1, -1), od_sgu_w[j],
                             sgu_b_tile, od_conv_w[j], seq, min(256, seq))
            w_o = od_w_o[j]
        xf, xb = _proj_ln(mix.reshape(n, d), w_o.astype(BF16), xf, g0, b0, alpha, tm)
        act = _ffn_up(xb, ffn_w_up[layer].astype(BF16), ffn_conv_w[layer], seq, tm_ffn, 256,
                      min(128, seq))
        xf, xb = _proj_ln(act, ffn_w_down[layer].astype(BF16), xf, g1, b1, alpha, tm)
    return xf.reshape(bsz, seq, d)
```

```python
import functools
import math

import numpy as np
import jax
import jax.numpy as jnp
from jax import lax
from jax.experimental import pallas as pl
from jax.experimental.pallas import tpu as pltpu

F32 = jnp.float32
BF16 = jnp.bfloat16

HEAD_DIM = 64
N_HEADS = 8
KV_RANK = 128
TOPK_MAX = 256
N_BUCKETS = 32
T5_MAX_DIST = 128
CHUNK = 128
CONV_W = 3
LN_EPS = 1e-5
MIX_HALF = N_HEADS * HEAD_DIM

LANES = 128
SUBLANES = 8
BF16_ROWS = 16
VMEM_LIMIT = 56 * 1024 * 1024

BLK = 128
PV_ROWS = KV_RANK + BF16_ROWS
NEG_BIG = -1e30
MASKED = -(2.0 ** 100)
F32_MAX = float(np.finfo(np.float32).max)
LOG2E = math.log2(math.e)
SEARCH_UNROLL = 4


def _cparams(sem):
    return pltpu.CompilerParams(dimension_semantics=sem, vmem_limit_bytes=VMEM_LIMIT)


def _dot(a, b):
    return jnp.dot(a, b, preferred_element_type=F32)


def _dot_nt(a, b):
    return lax.dot_general(a, b, (((1,), (1,)), ((), ())), preferred_element_type=F32)


def _layer_norm_rows(y, g, b):
    mu = jnp.mean(y, axis=-1, keepdims=True)
    yc = y - mu
    var = jnp.mean(yc * yc, axis=-1, keepdims=True)
    return yc * lax.rsqrt(var + LN_EPS) * g + b


def _tree(op, x):
    while x.shape[0] > 1:
        half = x.shape[0] // 2
        x = op(x[:half], x[half:])
    return x[0]


def _bf16_pieces(v):
    hi = v.astype(BF16)
    r1 = v - hi.astype(F32)
    mid = r1.astype(BF16)
    lo = (r1 - mid.astype(F32)).astype(BF16)
    return jnp.concatenate([hi, mid, lo], axis=1)


def _even_proj_kernel(x_ref, wb_ref, ws_ref, hb_ref, hs_ref):
    x = x_ref[...]
    hb_ref[...] = _dot(x, wb_ref[...]).astype(hb_ref.dtype)
    hs_ref[...] = _dot(x, ws_ref[...])


def _even_proj(xb, w_big, w_small, tm):
    n, k = xb.shape
    mb = w_big.shape[1]
    ms = w_small.shape[1]
    return pl.pallas_call(
        _even_proj_kernel,
        grid=(n // tm,),
        in_specs=[pl.BlockSpec((tm, k), lambda i: (i, 0)),
                  pl.BlockSpec((k, mb), lambda i: (0, 0)),
                  pl.BlockSpec((k, ms), lambda i: (0, 0))],
        out_specs=[pl.BlockSpec((tm, mb), lambda i: (i, 0)),
                   pl.BlockSpec((tm, ms), lambda i: (i, 0))],
        out_shape=[jax.ShapeDtypeStruct((n, mb), BF16),
                   jax.ShapeDtypeStruct((n, ms), F32)],
        compiler_params=_cparams(("parallel",)),
        name="even_proj",
    )(xb, w_big, w_small)


def _cumf_kernel(f_ref, b_ref, o_ref):
    seq = f_ref.shape[-1]
    row = lax.broadcasted_iota(jnp.int32, (LANES, LANES), 0)
    col = lax.broadcasted_iota(jnp.int32, (LANES, LANES), 1)
    upper = jnp.where(row <= col, 1.0, 0.0).astype(F32)
    carry = jnp.zeros((f_ref.shape[0], 1), F32)
    for c in range(seq // LANES):
        z = f_ref[:, c * LANES:(c + 1) * LANES] + b_ref[...]
        lf = jnp.minimum(z, 0.0) - jnp.log1p(jnp.exp(-jnp.abs(z)))
        cs = jnp.dot(lf, upper, preferred_element_type=F32,
                     precision=lax.Precision.HIGHEST) + carry
        o_ref[:, c * LANES:(c + 1) * LANES] = cs
        carry = cs[:, LANES - 1:LANES]


def _cumf(f_t, b_f):
    bsz, nh, seq = f_t.shape
    return pl.pallas_call(
        _cumf_kernel,
        grid=(bsz,),
        in_specs=[pl.BlockSpec((None, nh, seq), lambda b: (b, 0, 0)),
                  pl.BlockSpec((nh, 1), lambda b: (0, 0))],
        out_specs=pl.BlockSpec((None, nh, seq), lambda b: (b, 0, 0)),
        out_shape=jax.ShapeDtypeStruct((bsz, nh, seq), F32),
        compiler_params=_cparams(("parallel",)),
        name="fox_cumf",
    )(f_t, b_f)


def _attn_kernel(qa_ref, qi_ref, qb_ref, wt_ref, fq_ref, kb_ref, vb_ref, ckv_ref, kidx_ref, fk_ref,
                 wuk_ref, wuv_ref, bnear_ref, bfar_ref, pq_ref, cq_ref, pk_ref, ck_ref,
                 o_ref,
                 score_ref, lhsd_ref, ct_ref, kaug_ref, vt_ref, qsel_ref, qaug_ref, qim_ref,
                 m_ref, acc_ref, s_ref, *, k_sel):
    T = BLK
    H = N_HEADS
    NP = H // 2
    G = T // SUBLANES
    i = pl.program_id(1)
    nb = i + 1
    nblk = ct_ref.shape[0]
    lane = lax.broadcasted_iota(jnp.int32, (T, LANES), 1)
    lo_half = lane < HEAD_DIM
    key_i = lax.broadcasted_iota(jnp.int32, (T, T), 0)
    qry_i = lax.broadcasted_iota(jnp.int32, (T, T), 1)
    ones_rows = jnp.ones((BF16_ROWS, T), BF16)

    @pl.when(i == 0)
    def _per_batch():
        def prep(j, c):
            r = pl.ds(pl.multiple_of(j * T, T), T)
            cb = ckv_ref[r, :]
            lhsd_ref[r, 0:KV_RANK] = cb
            ct_ref[j, 0:KV_RANK, :] = cb.astype(F32).T.astype(BF16)
            ct_ref[j, KV_RANK:PV_ROWS, :] = ones_rows
            ext = (_dot(_bf16_pieces(fk_ref[r, :] * LOG2E), pk_ref[...]) + ck_ref[...]).astype(BF16)
            for p in range(NP):
                cols = slice(p * LANES, (p + 1) * LANES)
                kaug_ref[p, r, 0:LANES] = kb_ref[r, cols]
                kaug_ref[p, r, LANES:2 * LANES] = ext[:, cols]
                vt_ref[p, j, 0:LANES, :] = vb_ref[r, cols].astype(F32).T.astype(BF16)
                vt_ref[p, j, LANES:PV_ROWS, :] = ones_rows
            return c
        lax.fori_loop(0, nblk, prep, 0)

    w_rows = wt_ref[...] * (H ** -0.5)
    ext_q = (_dot(_bf16_pieces(fq_ref[...] * LOG2E), pq_ref[...]) + cq_ref[...]).astype(BF16)
    eye = jnp.where(key_i == qry_i, 1.0, 0.0).astype(BF16)
    for p in range(NP):
        cols = slice(p * LANES, (p + 1) * LANES)
        qa_pair, qi_pair, qb_pair = qa_ref[:, cols], qi_ref[:, cols], qb_ref[:, cols]
        for e in range(2):
            h = 2 * p + e
            rows = slice(e * T, (e + 1) * T)
            mine = lo_half if e == 0 else jnp.logical_not(lo_half)
            qsel_ref[p, rows, 0:KV_RANK] = _dot(qa_pair, wuk_ref[h]).astype(BF16)
            qsel_ref[p, rows, KV_RANK:2 * KV_RANK] = eye
            qaug_ref[p, rows, 0:LANES] = jnp.where(mine, qb_pair, 0)
            qaug_ref[p, rows, LANES:2 * LANES] = ext_q[:, h * LANES:(h + 1) * LANES]
            qim_ref[p, rows, :] = jnp.where(mine, qi_pair, 0)

    def reset_state():
        m_ref[...] = jnp.full(m_ref.shape, NEG_BIG, F32)
        acc_ref[...] = jnp.zeros(acc_ref.shape, F32)

    def online_update(p, s, pv_lhs):
        cols = slice(p * 2 * T, (p + 1) * 2 * T)
        m_old = m_ref[:, cols]
        s_max = _tree(jnp.maximum, s.reshape(-1, SUBLANES, 2 * T))
        m_new = jnp.maximum(m_old, jnp.max(s_max, axis=0, keepdims=True))
        alpha = jnp.exp2(m_old - m_new)
        pr = jnp.exp2((s - m_new[0:1, :]).astype(BF16))
        acc_ref[:, cols] = acc_ref[:, cols] * alpha[0:1, :] + _dot(pv_lhs, pr)
        m_ref[:, cols] = m_new

    def key_rows(j, nblocks):
        return pl.ds(pl.multiple_of(j * T, T), nblocks * T)

    def blocks_t(load, j, nblocks):
        return load(j) if nblocks == 1 else jnp.concatenate([load(j + b) for b in range(nblocks)], axis=1)

    def staged_steps(j0, n_steps, logits, absorb):
        logits(j0, 2, 0)
        for k in range(n_steps):
            if k + 1 < n_steps:
                logits(j0 + 2 * (k + 1), 2, (k + 1) % 2)
            absorb(j0 + 2 * k, 2, k % 2)

    def for_far_blocks(n_far, logits, absorb):
        n_oct = lax.shift_right_logical(n_far, 3)

        def oct_body(jj, c):
            staged_steps(8 * jj, 4, logits, absorb)
            return c
        lax.fori_loop(0, n_oct, oct_body, 0)

        @pl.when((n_far & 4) != 0)
        def _quad():
            staged_steps(8 * n_oct, 2, logits, absorb)

        @pl.when((n_far & 2) != 0)
        def _pair():
            staged_steps(8 * n_oct + (n_far & 4), 1, logits, absorb)

        @pl.when((n_far & 1) != 0)
        def _single():
            logits(n_far - 1, 1, 0)
            absorb(n_far - 1, 1, 0)

    def slot_rows(nblocks):
        return slice(0, nblocks * T)

    n_far = jnp.maximum(i - 1, 0)
    causal_tail = jnp.concatenate([key_i >= 0, key_i <= qry_i], axis=0)

    def idx_scores(j, nblocks):
        k_blk = kidx_ref[key_rows(j, nblocks), :]
        acc = jnp.zeros((nblocks * T, T), F32)
        for p in range(NP):
            s = _dot_nt(k_blk, qim_ref[p])
            for e in range(2):
                h = 2 * p + e
                acc = acc + w_rows[h:h + 1, :] * jnp.maximum(s[:, e * T:(e + 1) * T], 0.0)
        return acc

    def fox_logits(j, nblocks, slot, mask=None):
        for p in range(NP):
            s = _dot_nt(kaug_ref[p, key_rows(j, nblocks), :], qaug_ref[p])
            if mask is not None:
                s = jnp.where(jnp.concatenate([mask, mask], axis=1), s, MASKED)
            s_ref[slot, slot_rows(nblocks), p * 2 * T:(p + 1) * 2 * T] = s

    def fox_absorb(j, nblocks, slot):
        for p in range(NP):
            online_update(p, s_ref[slot, slot_rows(nblocks), p * 2 * T:(p + 1) * 2 * T],
                          blocks_t(lambda jb: vt_ref[p, jb], j, nblocks))

    def idx_fox_logits(j, nblocks, slot):
        score_ref[pl.ds(j, nblocks)] = idx_scores(j, nblocks).reshape(nblocks, T, T)
        fox_logits(j, nblocks, slot)

    reset_state()
    for_far_blocks(n_far, idx_fox_logits, fox_absorb)

    @pl.when(i >= 1)
    def _idx_fox_tail():
        score_ref[pl.ds(i - 1, 2)] = (
            jnp.where(causal_tail, idx_scores(i - 1, 2), -jnp.inf).reshape(2, T, T))
        fox_logits(i - 1, 2, 0, causal_tail)
        fox_absorb(i - 1, 2, 0)

    @pl.when(i == 0)
    def _idx_fox_first():
        score_ref[0] = jnp.where(key_i <= qry_i, idx_scores(0, 1), -jnp.inf)
        fox_logits(0, 1, 0, key_i <= qry_i)
        fox_absorb(0, 1, 0)

    score_ref[i + 1] = jnp.full((T, T), -jnp.inf, F32)

    row_lo = lax.broadcasted_iota(jnp.int32, (LANES, T), 0) < HEAD_DIM
    for p in range(NP):
        c0 = slice(p * 2 * T, p * 2 * T + T)
        c1 = slice(p * 2 * T + T, (p + 1) * 2 * T)
        o_t = jnp.where(row_lo, acc_ref[0:LANES, c0] / acc_ref[LANES:LANES + 1, c0],
                        acc_ref[0:LANES, c1] / acc_ref[LANES:LANES + 1, c1])
        o_ref[:, MIX_HALF + p * LANES:MIX_HALF + (p + 1) * LANES] = o_t.T.astype(o_ref.dtype)

    kf = float(k_sel)
    t_pos = (i * T + lax.broadcasted_iota(jnp.int32, (SUBLANES, LANES), 1)).astype(F32)
    searching = (t_pos + 1.0) > kf
    G2 = 2 * G
    n_pair = i // 2 + 1
    key_in_pair = (lax.broadcasted_iota(jnp.int32, (G2, SUBLANES, LANES), 0) * SUBLANES
                   + lax.broadcasted_iota(jnp.int32, (G2, SUBLANES, LANES), 1))

    def rep(v):
        return jnp.broadcast_to(v, (SUBLANES, LANES))

    def score_pair(jj):
        return score_ref[pl.ds(2 * jj, 2)].reshape(G2, SUBLANES, LANES)

    def count(*pred_fns):
        def body(jj, cs):
            s = score_pair(jj)
            return tuple(c + _tree(jnp.add, jnp.where(f(s, jj), 1.0, 0.0)) for c, f in zip(cs, pred_fns))
        zero = jnp.zeros((SUBLANES, LANES), F32)
        cs = lax.fori_loop(0, n_pair, body, (zero,) * len(pred_fns))
        return tuple(rep(jnp.sum(c, axis=0, keepdims=True)) for c in cs)

    def count_ge(thr):
        return count(lambda s, jj: s >= thr[None])[0]

    def select_threshold():
        def mm_body(jj, carry):
            mn, mx = carry
            s = score_pair(jj)
            return (jnp.minimum(mn, _tree(jnp.minimum, jnp.where(s == -jnp.inf, jnp.inf, s))),
                    jnp.maximum(mx, _tree(jnp.maximum, s)))

        mn, mx = lax.fori_loop(0, n_pair, mm_body,
                               (jnp.full((SUBLANES, LANES), jnp.inf, F32),
                                jnp.full((SUBLANES, LANES), -jnp.inf, F32)))
        mn = rep(jnp.min(mn, axis=0, keepdims=True))
        mx = rep(jnp.max(mx, axis=0, keepdims=True))
        above_mx = mx + jnp.maximum(jnp.abs(mx) * 2.0 ** -20, 1e-30)
        c_ge0, c_gt0 = count(lambda s, jj: s >= 0.0, lambda s, jj: s > 0.0)
        thr_ge0 = c_ge0 >= kf
        thr_gt0 = c_gt0 >= kf
        lo0 = jnp.where(thr_ge0, 0.0, mn)
        cnt_lo0 = jnp.where(thr_ge0, c_ge0, t_pos + 1.0)
        hi0 = jnp.where(thr_gt0, above_mx, 0.0)
        cnt_hi0 = jnp.where(thr_gt0, 0.0, jnp.where(thr_ge0, c_gt0, c_ge0))
        undecided = thr_gt0 | jnp.logical_not(thr_ge0)
        active0 = jnp.where(searching & undecided & (cnt_lo0 > kf), 1.0, 0.0)

        def cond(st):
            return jnp.logical_and(st[5] > 0.0, st[6] < 2200)

        def bisect(st):
            lo, hi, cnt_lo, cnt_hi, active = st
            mid = 0.5 * lo + 0.5 * hi
            adjacent = (mid <= lo) | (mid >= hi)
            c = count_ge(mid)
            act = active > 0.0
            go_lo = act & jnp.logical_not(adjacent) & (c >= kf)
            go_hi = act & jnp.logical_not(adjacent) & (c < kf)
            lo = jnp.where(go_lo, mid, lo)
            cnt_lo = jnp.where(go_lo, c, cnt_lo)
            hi = jnp.where(go_hi, mid, hi)
            cnt_hi = jnp.where(go_hi, c, cnt_hi)
            active = jnp.where(act & jnp.logical_not(adjacent) & (cnt_lo > kf), 1.0, 0.0)
            return lo, hi, cnt_lo, cnt_hi, active

        def body(st):
            inner = st[:5]
            for _ in range(SEARCH_UNROLL):
                inner = bisect(inner)
            return inner + (jnp.max(inner[4]), st[6] + SEARCH_UNROLL)

        lo, hi, cnt_lo, cnt_hi, _, _, _ = lax.while_loop(
            cond, body, (lo0, hi0, cnt_lo0, cnt_hi0, active0, jnp.max(active0), 0))
        thr = jnp.where(searching, lo, -F32_MAX)

        tied = searching & (cnt_lo > kf)
        need = kf - cnt_hi

        @pl.when(jnp.max(jnp.where(tied, 1.0, 0.0)) > 0.0)
        def _break_ties():
            def key_index(jj):
                return (2 * jj * T + key_in_pair).astype(F32)

            def count_tied_upto(jcut):
                return count(lambda s, jj: (s == thr[None]) & (key_index(jj) <= jcut[None]))[0]

            def bs_body(_, st):
                jl, jh = st
                jm = jnp.floor(0.5 * (jl + jh))
                ok = count_tied_upto(jm) >= need
                return jnp.where(ok, jl, jm), jnp.where(ok, jm, jh)

            n_keys = (nb * T).astype(F32)
            jl0 = jnp.full((SUBLANES, LANES), -1.0, F32)
            jh0 = jnp.zeros((SUBLANES, LANES), F32) + (n_keys - 1.0)
            n_steps = int(math.ceil(math.log2(nblk * T))) + 1
            _, jcut = lax.fori_loop(0, n_steps, bs_body, (jl0, jh0))

            def drop_body(jj, c):
                s = score_pair(jj)
                drop = tied[None] & (s == thr[None]) & (key_index(jj) > jcut[None])
                score_ref[pl.ds(2 * jj, 2)] = jnp.where(drop, -jnp.inf, s).reshape(2, T, T)
                return c
            lax.fori_loop(0, n_pair, drop_body, 0)

        return thr

    thr = lax.cond((i + 1) * T > k_sel, select_threshold,
                   lambda: jnp.full((SUBLANES, LANES), -F32_MAX, F32))

    def mask_body(jj, c):
        lhsd_ref[key_rows(2 * jj, 2), KV_RANK:2 * KV_RANK] = (
            jnp.where(score_pair(jj) >= thr[None], 0.0, MASKED).reshape(2 * T, T).astype(BF16))
        return c

    lax.fori_loop(0, n_pair, mask_body, 0)

    def dsa_logits(j, nblocks, slot, bias=None):
        lhs = lhsd_ref[key_rows(j, nblocks), :]
        for p in range(NP):
            cols = slice(p * 2 * T, (p + 1) * 2 * T)
            s = _dot_nt(lhs, qsel_ref[p])
            s_ref[slot, slot_rows(nblocks), cols] = s if bias is None else s + bias(cols)

    def dsa_absorb(j, nblocks, slot):
        pv_lhs = blocks_t(lambda jb: ct_ref[jb], j, nblocks)
        for p in range(NP):
            online_update(p, s_ref[slot, slot_rows(nblocks), p * 2 * T:(p + 1) * 2 * T], pv_lhs)

    reset_state()
    for_far_blocks(n_far, dsa_logits, dsa_absorb)
    m_ref[...] = m_ref[...] + bfar_ref[...]

    @pl.when(i >= 1)
    def _dsa_tail():
        dsa_logits(i - 1, 2, 0, lambda cols: bnear_ref[:, :, cols].reshape(2 * T, 2 * T))
        dsa_absorb(i - 1, 2, 0)

    @pl.when(i == 0)
    def _dsa_first():
        dsa_logits(0, 1, 0, lambda cols: bnear_ref[1, :, cols])
        dsa_absorb(0, 1, 0)

    for p in range(NP):
        o_pair = jnp.zeros((T, LANES), F32)
        for e in range(2):
            cols = slice(p * 2 * T + e * T, p * 2 * T + (e + 1) * T)
            o_t = acc_ref[0:KV_RANK, cols] / acc_ref[KV_RANK:KV_RANK + 1, cols]
            o_pair = o_pair + _dot(o_t.T.astype(BF16), wuv_ref[2 * p + e])
        o_ref[:, p * LANES:(p + 1) * LANES] = o_pair.astype(o_ref.dtype)


_EV_QA, _EV_QI, _EV_QB, _EV_KB, _EV_VB = 0, 1, 2, 3, 4
_EV_CKV, _EV_KIDX = 20, 21


def _attention(hb, w_t, f_rows, wuk_pad, wuv_pad, bias_near, bias_far, place, k_sel):
    bsz, seq, _ = hb.shape
    T = BLK
    nblk = seq // T
    H = N_HEADS
    pq, cq, pk, ck = place
    wide = lambda c: pl.BlockSpec((None, T, MIX_HALF), lambda b, i, c=c: (b, i, c))
    once = dict(pipeline_mode=pl.Buffered(1))
    full_wide = lambda c: pl.BlockSpec((None, seq, MIX_HALF), lambda b, i, c=c: (b, 0, c), **once)
    full_narrow = lambda c: pl.BlockSpec((None, seq, LANES), lambda b, i, c=c: (b, 0, c), **once)
    const = lambda a: pl.BlockSpec(a.shape, lambda b, i, nd=a.ndim: (0,) * nd)
    return pl.pallas_call(
        functools.partial(_attn_kernel, k_sel=k_sel),
        grid=(bsz, nblk),
        in_specs=[wide(_EV_QA), wide(_EV_QI), wide(_EV_QB),
                  pl.BlockSpec((None, H, T), lambda b, i: (b, 0, i)),
                  pl.BlockSpec((None, T, LANES), lambda b, i: (b, i, 0)),
                  full_wide(_EV_KB), full_wide(_EV_VB),
                  full_narrow(_EV_CKV), full_narrow(_EV_KIDX),
                  pl.BlockSpec((None, seq, LANES), lambda b, i: (b, 0, 0), **once),
                  const(wuk_pad), const(wuv_pad), const(bias_near), const(bias_far),
                  const(pq), const(cq), const(pk), const(ck)],
        out_specs=pl.BlockSpec((None, T, 2 * MIX_HALF), lambda b, i: (b, i, 0)),
        out_shape=jax.ShapeDtypeStruct((bsz, seq, 2 * MIX_HALF), BF16),
        scratch_shapes=[pltpu.VMEM((nblk + 2, T, T), F32),
                        pltpu.VMEM((seq + 2 * T, 2 * KV_RANK), BF16),
                        pltpu.VMEM((nblk, PV_ROWS, T), BF16),
                        pltpu.VMEM((H // 2, seq, 2 * LANES), BF16),
                        pltpu.VMEM((H // 2, nblk, PV_ROWS, T), BF16),
                        pltpu.VMEM((H // 2, 2 * T, 2 * KV_RANK), BF16),
                        pltpu.VMEM((H // 2, 2 * T, 2 * LANES), BF16),
                        pltpu.VMEM((H // 2, 2 * T, LANES), BF16),
                        pltpu.VMEM((SUBLANES, H * T), F32),
                        pltpu.VMEM((PV_ROWS, H * T), F32),
                        pltpu.VMEM((2, 2 * T, H * T), F32)],
        compiler_params=_cparams(("parallel", "arbitrary")),
        name="dsa_fox_attention",
    )(hb, hb, hb, w_t, f_rows, hb, hb, hb, hb, f_rows, wuk_pad, wuv_pad, bias_near, bias_far,
      pq, cq, pk, ck)


def _proj_ln_kernel(a_ref, w_ref, x_ref, g_ref, b_ref, xo_ref, xb_ref, *, alpha, sub):
    for c in range(a_ref.shape[0] // sub):
        rows = slice(c * sub, (c + 1) * sub)
        y = alpha * x_ref[rows, :] + _dot(a_ref[rows, :], w_ref[...])
        out = _layer_norm_rows(y, g_ref[...], b_ref[...])
        xo_ref[rows, :] = out
        xb_ref[rows, :] = out.astype(BF16)


def _proj_ln(a, w, x, g, b, alpha, tm):
    n, k = a.shape
    d = w.shape[1]
    return pl.pallas_call(
        functools.partial(_proj_ln_kernel, alpha=alpha, sub=min(128, tm)),
        grid=(n // tm,),
        in_specs=[pl.BlockSpec((tm, k), lambda i: (i, 0)),
                  pl.BlockSpec((k, d), lambda i: (0, 0)),
                  pl.BlockSpec((tm, d), lambda i: (i, 0)),
                  pl.BlockSpec((1, d), lambda i: (0, 0)),
                  pl.BlockSpec((1, d), lambda i: (0, 0))],
        out_specs=[pl.BlockSpec((tm, d), lambda i: (i, 0)),
                   pl.BlockSpec((tm, d), lambda i: (i, 0))],
        out_shape=[jax.ShapeDtypeStruct((n, d), F32),
                   jax.ShapeDtypeStruct((n, d), BF16)],
        compiler_params=_cparams(("parallel",)),
        name="proj_residual_ln",
    )(a, w, x, g, b)


def _causal_conv3(h, tail, cw):
    w0, w1, w2 = cw[0:1, :], cw[1:2, :], cw[2:3, :]
    body = w0 * pltpu.roll(h, 2, axis=0) + w1 * pltpu.roll(h, 1, axis=0) + w2 * h
    head = h[0:SUBLANES, :]
    r = lax.broadcasted_iota(jnp.int32, head.shape, 0)
    hm1 = jnp.where(r < 1, pltpu.roll(tail, 1, axis=0), pltpu.roll(head, 1, axis=0))
    hm2 = jnp.where(r < 2, pltpu.roll(tail, 2, axis=0), pltpu.roll(head, 2, axis=0))
    out_head = w0 * hm2 + w1 * hm1 + w2 * head
    return jnp.concatenate([out_head, body[SUBLANES:, :]], axis=0)


def _ffn_up_kernel(x_ref, wg_ref, wv_ref, cg_ref, cv_ref, o_ref, tg_ref, tv_ref, *h_refs,
                   tiles_per_seq, sub):
    n_chunks = x_ref.shape[0] // sub
    hg_refs, hv_refs = h_refs[:n_chunks], h_refs[n_chunks:]
    first = (pl.program_id(1) % tiles_per_seq) == 0
    body_rows = slice(SUBLANES, SUBLANES + sub)

    def matmuls(c):
        x = x_ref[c * sub:(c + 1) * sub, :]
        hg_refs[c][body_rows, :] = _dot(x, wg_ref[...])
        hv_refs[c][body_rows, :] = _dot(x, wv_ref[...])

    def conv(h_ref, cw):
        return (cw[0:1, :] * h_ref[SUBLANES - 2:SUBLANES - 2 + sub, :]
                + cw[1:2, :] * h_ref[SUBLANES - 1:SUBLANES - 1 + sub, :]
                + cw[2:3, :] * h_ref[body_rows, :])

    def gate(c):
        if c == 0:
            hg_refs[0][0:SUBLANES, :] = jnp.where(first, 0.0, tg_ref[...])
            hv_refs[0][0:SUBLANES, :] = jnp.where(first, 0.0, tv_ref[...])
        else:
            hg_refs[c][0:SUBLANES, :] = hg_refs[c - 1][sub:sub + SUBLANES, :]
            hv_refs[c][0:SUBLANES, :] = hv_refs[c - 1][sub:sub + SUBLANES, :]
        g = conv(hg_refs[c], cg_ref[...])
        v = conv(hv_refs[c], cv_ref[...])
        o_ref[c * sub:(c + 1) * sub, :] = (g / (1.0 + jnp.exp(-g)) * v).astype(o_ref.dtype)

    matmuls(0)
    for c in range(n_chunks):
        if c + 1 < n_chunks:
            matmuls(c + 1)
        gate(c)
    tg_ref[...] = hg_refs[n_chunks - 1][sub:sub + SUBLANES, :]
    tv_ref[...] = hv_refs[n_chunks - 1][sub:sub + SUBLANES, :]


def _ffn_up(xb, w_up, conv_w, seq, tm, tn, sub):
    n, k = xb.shape
    dff = w_up.shape[1] // 2
    ncol = dff // tn
    return pl.pallas_call(
        functools.partial(_ffn_up_kernel, tiles_per_seq=seq // tm, sub=sub),
        grid=(ncol, n // tm),
        in_specs=[pl.BlockSpec((tm, k), lambda j, i: (i, 0)),
                  pl.BlockSpec((k, tn), lambda j, i: (0, j)),
                  pl.BlockSpec((k, tn), lambda j, i, ncol=ncol: (0, j + ncol)),
                  pl.BlockSpec((CONV_W, tn), lambda j, i: (0, j)),
                  pl.BlockSpec((CONV_W, tn), lambda j, i, ncol=ncol: (0, j + ncol))],
        out_specs=pl.BlockSpec((tm, tn), lambda j, i: (i, j)),
        out_shape=jax.ShapeDtypeStruct((n, dff), BF16),
        scratch_shapes=([pltpu.VMEM((SUBLANES, tn), F32)] * 2
                        + [pltpu.VMEM((sub + SUBLANES, tn), F32)] * (2 * (tm // sub))),
        compiler_params=_cparams(("arbitrary", "arbitrary")),
        name="ffn_up_conv_gate",
    )(xb, w_up, w_up, conv_w, conv_w)


def _gelu_tanh(x):
    return 0.5 * x * (1.0 + jnp.tanh(math.sqrt(2.0 / math.pi) * (x + 0.044715 * (x * x * x))))


def _group_mean(v, avg):
    hi = v.astype(BF16)
    lo = (v - hi.astype(F32)).astype(BF16)
    return _dot(hi, avg) + _dot(lo, avg)


def _odd_mixer_kernel(x_ref, w_ref, sg_ref, sw_ref, sb_ref, cw_ref, o_ref, tail_ref, *, tiles_per_seq):
    tm = x_ref.shape[0]
    W = MIX_HALF
    first = (pl.program_id(0) % tiles_per_seq) == 0
    x = x_ref[...]
    lane = lax.broadcasted_iota(jnp.int32, (CHUNK, LANES), 1)
    lo_half = lane < HEAD_DIM
    r = lax.broadcasted_iota(jnp.int32, (W, W), 0) // HEAD_DIM
    c = lax.broadcasted_iota(jnp.int32, (W, W), 1) // HEAD_DIM
    avg = jnp.where(r == c, 1.0 / HEAD_DIM, 0.0).astype(BF16)
    tri = (lax.broadcasted_iota(jnp.int32, (CHUNK, CHUNK), 1)
           <= lax.broadcasted_iota(jnp.int32, (CHUNK, CHUNK), 0))

    u = _gelu_tanh(_dot(x, w_ref[:, 0:W]))
    v = _gelu_tanh(_dot(x, w_ref[:, W:2 * W]))
    vc = v - _group_mean(v, avg)
    var = _group_mean(vc * vc, avg)
    vn = (vc * lax.rsqrt(var + LN_EPS) * sg_ref[...]).astype(BF16)
    for n in range(tm // CHUNK):
        rows = slice(n * CHUNK, (n + 1) * CHUNK)
        for p in range(N_HEADS // 2):
            cols = slice(p * LANES, (p + 1) * LANES)
            vp = vn[rows, cols]
            w_e = jnp.where(tri, sw_ref[2 * p], 0.0).astype(BF16)
            w_o = jnp.where(tri, sw_ref[2 * p + 1], 0.0).astype(BF16)
            mix = (_dot(w_e, jnp.where(lo_half, vp, 0)) + _dot(w_o, jnp.where(lo_half, 0, vp))
                   + sb_ref[:, cols])
            o_ref[rows, cols] = (u[rows, cols] * mix).astype(o_ref.dtype)

    g_b = _dot(x, w_ref[:, 2 * W:3 * W])
    y = _dot(x, w_ref[:, 3 * W:4 * W]) * _dot(x, w_ref[:, 4 * W:5 * W])
    conv = _causal_conv3(y, jnp.where(first, 0.0, tail_ref[...]), cw_ref[...])
    tail_ref[...] = y[tm - SUBLANES:tm, :]
    o_ref[:, W:2 * W] = (g_b * conv).astype(o_ref.dtype)


def _odd_mixer(xb, w_in, sgu_g, sgu_w, sgu_b_tile, conv_w, seq, tm):
    n, k = xb.shape
    W = MIX_HALF
    return pl.pallas_call(
        functools.partial(_odd_mixer_kernel, tiles_per_seq=seq // tm),
        grid=(n // tm,),
        in_specs=[pl.BlockSpec((tm, k), lambda i: (i, 0)),
                  pl.BlockSpec((k, 5 * W), lambda i: (0, 0)),
                  pl.BlockSpec((1, W), lambda i: (0, 0)),
                  pl.BlockSpec((N_HEADS, CHUNK, CHUNK), lambda i: (0, 0, 0)),
                  pl.BlockSpec((CHUNK, W), lambda i: (0, 0)),
                  pl.BlockSpec((CONV_W, W), lambda i: (0, 0))],
        out_specs=pl.BlockSpec((tm, 2 * W), lambda i: (i, 0)),
        out_shape=jax.ShapeDtypeStruct((n, 2 * W), BF16),
        scratch_shapes=[pltpu.VMEM((SUBLANES, W), F32)],
        compiler_params=_cparams(("arbitrary",)),
        name="odd_mixer",
    )(xb, w_in, sgu_g, sgu_w, sgu_b_tile, conv_w)


def _t5_bucket_np(dist):
    max_exact = N_BUCKETS // 2
    n = np.maximum(dist, 0)
    nf = np.maximum(n, 1).astype(np.float32)
    large = max_exact + (np.log(nf / max_exact) / np.float32(math.log(T5_MAX_DIST / max_exact))
                         * (N_BUCKETS - max_exact)).astype(np.int32)
    large = np.minimum(large, N_BUCKETS - 1)
    return np.where(n < max_exact, n, large)


def _bias_tables(rel_bias):
    T = BLK
    s = np.arange(T)[:, None]
    t = np.arange(T)[None, :]
    bucket = _t5_bucket_np(np.stack([t + T - s, t - s]))
    far_bucket = _t5_bucket_np(np.array([T + 1]))[0]
    assert far_bucket == N_BUCKETS - 1
    assert (_t5_bucket_np(np.arange(T + 1, 64 * T)) == far_bucket).all()
    rb = rel_bias.astype(F32) * LOG2E
    near = jnp.swapaxes(rb[bucket], 2, 3).reshape(2, T, N_HEADS * T)
    far = jnp.repeat(rb[far_bucket], T)[None, :]
    return near, far


def _placement_constants():
    H, L = N_HEADS, LANES
    pq = np.zeros((3 * L, H * L), np.float32)
    cq = np.zeros((1, H * L), np.float32)
    pk = np.zeros((3 * L, (H // 2) * L), np.float32)
    ck = np.zeros((1, (H // 2) * L), np.float32)
    for h in range(H):
        p, e = divmod(h, 2)
        for piece in range(3):
            pq[piece * L + h, h * L + 6 + piece] = 1.0
            cq[0, h * L + 3 * e + piece] = 1.0
            pk[piece * L + h, p * L + 3 * e + piece] = -1.0
            ck[0, p * L + 6 + piece] = 1.0
    return jnp.asarray(pq, BF16), jnp.asarray(cq), jnp.asarray(pk, BF16), jnp.asarray(ck)


def _even_weights(w_in, w_uk, w_uv):
    hd, H = HEAD_DIM, N_HEADS
    widths = (H * hd, KV_RANK, H * hd, hd, H, H * hd, H * hd, H * hd, H)
    offs = np.cumsum((0,) + widths)
    seg = lambda s: w_in[:, offs[s]:offs[s + 1]]
    q_a, c_kv, q_idx, k_idx, w_idx, q_b, k_b, v_b, f_in = [seg(s) for s in range(9)]
    scale = hd ** -0.5
    w_big = jnp.concatenate([q_a, q_idx * scale, q_b * (scale * LOG2E), k_b, v_b, c_kv,
                             k_idx, k_idx], axis=1).astype(BF16)
    pad = jnp.zeros((w_in.shape[0], LANES - 2 * H), w_in.dtype)
    w_small = jnp.concatenate([w_idx, f_in, pad], axis=1).astype(BF16)
    wuk_t = jnp.swapaxes(w_uk, 1, 2) * (scale * LOG2E)
    wuk_pad = jnp.zeros((H, LANES, KV_RANK), F32)
    wuv_pad = jnp.zeros((H, KV_RANK, LANES), F32)
    for h in range(H):
        o = hd * (h % 2)
        wuk_pad = wuk_pad.at[h, o:o + hd, :].set(wuk_t[h])
        wuv_pad = wuv_pad.at[h, :, o:o + hd].set(w_uv[h])
    return w_big, w_small, wuk_pad.astype(BF16), wuv_pad.astype(BF16)


def kernel(x, ln_g, ln_b, rel_bias, ev_w_in, ev_w_uk, ev_w_uv, ev_b_f, ev_w_o,
           od_w_in, od_sgu_g, od_sgu_w, od_sgu_b, od_conv_w, od_w_o,
           ffn_w_up, ffn_conv_w, ffn_w_down):
    bsz, seq, d = x.shape
    depth = ln_g.shape[0]
    n = bsz * seq
    alpha = (2.0 * depth) ** 0.25
    k_sel = min(TOPK_MAX, seq // 4)
    H = N_HEADS
    assert seq % BLK == 0 and seq % CHUNK == 0
    tm = min(512, seq)
    tm_ffn = min(1024, seq)

    bias_near, bias_far = _bias_tables(rel_bias)
    place = _placement_constants()
    xf = x.reshape(n, d)
    xb = xf.astype(BF16)
    for layer in range(depth):
        j = layer // 2
        g0, b0 = ln_g[layer, 0][None, :], ln_b[layer, 0][None, :]
        g1, b1 = ln_g[layer, 1][None, :], ln_b[layer, 1][None, :]
        if layer % 2 == 0:
            w_big, w_small, wuk_pad, wuv_pad = _even_weights(ev_w_in[j], ev_w_uk[j], ev_w_uv[j])
            hb, hs = _even_proj(xb, w_big, w_small, tm)
            hb = hb.reshape(bsz, seq, -1)
            hs_t = jnp.swapaxes(hs.reshape(bsz, seq, -1)[:, :, 0:2 * H], 1, 2)
            cum_f = _cumf(hs_t[:, H:2 * H], ev_b_f[j][:, None])
            f_rows = jnp.pad(jnp.swapaxes(cum_f, 1, 2), ((0, 0), (0, 0), (0, LANES - H)))
            mix = _attention(hb, hs_t[:, 0:H], f_rows, wuk_pad, wuv_pad, bias_near, bias_far,
                             place, k_sel)
            w_o = ev_w_o[j]
        else:
            sgu_b_tile = jnp.repeat(jnp.swapaxes(od_sgu_b[j], 0, 1), HEAD_DIM, axis=1)
            mix = _odd_mixer(xb, od_w_in[j].astype(BF16), od_sgu_g[j].reshape(1, -1), od_sgu_w[j],
                             sgu_b_tile, od_conv_w[j], seq, min(256, seq))
            w_o = od_w_o[j]
        xf, xb = _proj_ln(mix.reshape(n, d), w_o.astype(BF16), xf, g0, b0, alpha, tm)
        act = _ffn_up(xb, ffn_w_up[layer].astype(BF16), ffn_conv_w[layer], seq, tm_ffn, 256,
                      min(512, seq))
        xf, xb = _proj_ln(act, ffn_w_down[layer].astype(BF16), xf, g1, b1, alpha, tm)
    return xf.reshape(bsz, seq, d)
```

```python
import functools
import math

import numpy as np
import jax
import jax.numpy as jnp
from jax import lax
from jax.experimental import pallas as pl
from jax.experimental.pallas import tpu as pltpu

F32 = jnp.float32
BF16 = jnp.bfloat16

HEAD_DIM = 64
N_HEADS = 8
KV_RANK = 128
TOPK_MAX = 256
N_BUCKETS = 32
T5_MAX_DIST = 128
CHUNK = 128
CONV_W = 3
LN_EPS = 1e-5
MIX_HALF = N_HEADS * HEAD_DIM

LANES = 128
SUBLANES = 8
BF16_ROWS = 16
VMEM_LIMIT = 56 * 1024 * 1024

BLK = 128
PV_ROWS = KV_RANK + BF16_ROWS
NEG_BIG = -1e30
MASKED = -(2.0 ** 100)
F32_MAX = float(np.finfo(np.float32).max)
LOG2E = math.log2(math.e)
SEARCH_UNROLL = 4


def _cparams(sem):
    return pltpu.CompilerParams(dimension_semantics=sem, vmem_limit_bytes=VMEM_LIMIT)


def _dot(a, b):
    return jnp.dot(a, b, preferred_element_type=F32)


def _dot_nt(a, b):
    return lax.dot_general(a, b, (((1,), (1,)), ((), ())), preferred_element_type=F32)


def _layer_norm_rows(y, g, b):
    mu = jnp.mean(y, axis=-1, keepdims=True)
    yc = y - mu
    var = jnp.mean(yc * yc, axis=-1, keepdims=True)
    return yc * lax.rsqrt(var + LN_EPS) * g + b


def _tree(op, x):
    while x.shape[0] > 1:
        half = x.shape[0] // 2
        x = op(x[:half], x[half:])
    return x[0]


def _bf16_pieces(v):
    hi = v.astype(BF16)
    r1 = v - hi.astype(F32)
    mid = r1.astype(BF16)
    lo = (r1 - mid.astype(F32)).astype(BF16)
    return jnp.concatenate([hi, mid, lo], axis=1)


def _even_proj_kernel(x_ref, wb_ref, ws_ref, hb_ref, hs_ref):
    x = x_ref[...]
    hb_ref[...] = _dot(x, wb_ref[...]).astype(hb_ref.dtype)
    hs_ref[...] = _dot(x, ws_ref[...])


def _even_proj(xb, w_big, w_small, tm):
    n, k = xb.shape
    mb = w_big.shape[1]
    ms = w_small.shape[1]
    return pl.pallas_call(
        _even_proj_kernel,
        grid=(n // tm,),
        in_specs=[pl.BlockSpec((tm, k), lambda i: (i, 0)),
                  pl.BlockSpec((k, mb), lambda i: (0, 0)),
                  pl.BlockSpec((k, ms), lambda i: (0, 0))],
        out_specs=[pl.BlockSpec((tm, mb), lambda i: (i, 0)),
                   pl.BlockSpec((tm, ms), lambda i: (i, 0))],
        out_shape=[jax.ShapeDtypeStruct((n, mb), BF16),
                   jax.ShapeDtypeStruct((n, ms), F32)],
        compiler_params=_cparams(("parallel",)),
        name="even_proj",
    )(xb, w_big, w_small)


def _cumf_kernel(f_ref, b_ref, o_ref):
    seq = f_ref.shape[-1]
    row = lax.broadcasted_iota(jnp.int32, (LANES, LANES), 0)
    col = lax.broadcasted_iota(jnp.int32, (LANES, LANES), 1)
    upper = jnp.where(row <= col, 1.0, 0.0).astype(F32)
    carry = jnp.zeros((f_ref.shape[0], 1), F32)
    for c in range(seq // LANES):
        z = f_ref[:, c * LANES:(c + 1) * LANES] + b_ref[...]
        lf = jnp.minimum(z, 0.0) - jnp.log1p(jnp.exp(-jnp.abs(z)))
        cs = jnp.dot(lf, upper, preferred_element_type=F32,
                     precision=lax.Precision.HIGHEST) + carry
        o_ref[:, c * LANES:(c + 1) * LANES] = cs
        carry = cs[:, LANES - 1:LANES]


def _cumf(f_t, b_f):
    bsz, nh, seq = f_t.shape
    return pl.pallas_call(
        _cumf_kernel,
        grid=(bsz,),
        in_specs=[pl.BlockSpec((None, nh, seq), lambda b: (b, 0, 0)),
                  pl.BlockSpec((nh, 1), lambda b: (0, 0))],
        out_specs=pl.BlockSpec((None, nh, seq), lambda b: (b, 0, 0)),
        out_shape=jax.ShapeDtypeStruct((bsz, nh, seq), F32),
        compiler_params=_cparams(("parallel",)),
        name="fox_cumf",
    )(f_t, b_f)


def _attn_kernel(qa_ref, qi_ref, qb_ref, wt_ref, fq_ref, kb_ref, vb_ref, ckv_ref, kidx_ref, fk_ref,
                 wuk_ref, wuv_ref, bnear_ref, bfar_ref, pq_ref, cq_ref, pk_ref, ck_ref,
                 o_ref,
                 score_ref, lhsd_ref, ct_ref, kaug_ref, vt_ref, qsel_ref, qaug_ref, qim_ref,
                 m_ref, acc_ref, s_ref, tie_ref, *, k_sel):
    T = BLK
    H = N_HEADS
    NP = H // 2
    G = T // SUBLANES
    i = pl.program_id(1)
    nb = i + 1
    nblk = ct_ref.shape[0]
    lane = lax.broadcasted_iota(jnp.int32, (T, LANES), 1)
    lo_half = lane < HEAD_DIM
    key_i = lax.broadcasted_iota(jnp.int32, (T, T), 0)
    qry_i = lax.broadcasted_iota(jnp.int32, (T, T), 1)
    ones_rows = jnp.ones((BF16_ROWS, T), BF16)

    @pl.when(i == 0)
    def _per_batch():
        def prep(j, c):
            r = pl.ds(pl.multiple_of(j * T, T), T)
            cb = ckv_ref[r, :]
            lhsd_ref[r, 0:KV_RANK] = cb
            ct_ref[j, 0:KV_RANK, :] = cb.astype(F32).T.astype(BF16)
            ct_ref[j, KV_RANK:PV_ROWS, :] = ones_rows
            ext = (_dot(_bf16_pieces(fk_ref[r, :] * LOG2E), pk_ref[...]) + ck_ref[...]).astype(BF16)
            for p in range(NP):
                cols = slice(p * LANES, (p + 1) * LANES)
                kaug_ref[p, r, 0:LANES] = kb_ref[r, cols]
                kaug_ref[p, r, LANES:2 * LANES] = ext[:, cols]
                vt_ref[p, j, 0:LANES, :] = vb_ref[r, cols].astype(F32).T.astype(BF16)
                vt_ref[p, j, LANES:PV_ROWS, :] = ones_rows
            return c
        lax.fori_loop(0, nblk, prep, 0)

    w_rows = wt_ref[...] * (H ** -0.5)
    ext_q = (_dot(_bf16_pieces(fq_ref[...] * LOG2E), pq_ref[...]) + cq_ref[...]).astype(BF16)
    eye = jnp.where(key_i == qry_i, 1.0, 0.0).astype(BF16)
    for p in range(NP):
        cols = slice(p * LANES, (p + 1) * LANES)
        qa_pair, qi_pair, qb_pair = qa_ref[:, cols], qi_ref[:, cols], qb_ref[:, cols]
        for e in range(2):
            h = 2 * p + e
            rows = slice(e * T, (e + 1) * T)
            mine = lo_half if e == 0 else jnp.logical_not(lo_half)
            qsel_ref[p, rows, 0:KV_RANK] = _dot(qa_pair, wuk_ref[h]).astype(BF16)
            qsel_ref[p, rows, KV_RANK:2 * KV_RANK] = eye
            qaug_ref[p, rows, 0:LANES] = jnp.where(mine, qb_pair, 0)
            qaug_ref[p, rows, LANES:2 * LANES] = ext_q[:, h * LANES:(h + 1) * LANES]
            qim_ref[p, rows, :] = jnp.where(mine, qi_pair, 0)

    def reset_state():
        m_ref[...] = jnp.full(m_ref.shape, NEG_BIG, F32)
        acc_ref[...] = jnp.zeros(acc_ref.shape, F32)

    def online_update(p, s, pv_lhs):
        cols = slice(p * 2 * T, (p + 1) * 2 * T)
        m_old = m_ref[:, cols]
        s_max = _tree(jnp.maximum, s.reshape(-1, SUBLANES, 2 * T))
        m_new = jnp.maximum(m_old, jnp.max(s_max, axis=0, keepdims=True))
        alpha = jnp.exp2(m_old - m_new)
        pr = jnp.exp2((s - m_new[0:1, :]).astype(BF16))
        acc_ref[:, cols] = acc_ref[:, cols] * alpha[0:1, :] + _dot(pv_lhs, pr)
        m_ref[:, cols] = m_new

    def key_rows(j, nblocks):
        return pl.ds(pl.multiple_of(j * T, T), nblocks * T)

    def blocks_t(load, j, nblocks):
        return load(j) if nblocks == 1 else jnp.concatenate([load(j + b) for b in range(nblocks)], axis=1)

    def staged_steps(j0, n_steps, logits, absorb):
        logits(j0, 2, 0)
        for k in range(n_steps):
            if k + 1 < n_steps:
                logits(j0 + 2 * (k + 1), 2, (k + 1) % 2)
            absorb(j0 + 2 * k, 2, k % 2)

    def for_far_blocks(n_far, logits, absorb):
        n_oct = lax.shift_right_logical(n_far, 3)

        def oct_body(jj, c):
            staged_steps(8 * jj, 4, logits, absorb)
            return c
        lax.fori_loop(0, n_oct, oct_body, 0)

        @pl.when((n_far & 4) != 0)
        def _quad():
            staged_steps(8 * n_oct, 2, logits, absorb)

        @pl.when((n_far & 2) != 0)
        def _pair():
            staged_steps(8 * n_oct + (n_far & 4), 1, logits, absorb)

        @pl.when((n_far & 1) != 0)
        def _single():
            logits(n_far - 1, 1, 0)
            absorb(n_far - 1, 1, 0)

    def slot_rows(nblocks):
        return slice(0, nblocks * T)

    n_far = jnp.maximum(i - 1, 0)
    causal_tail = jnp.concatenate([key_i >= 0, key_i <= qry_i], axis=0)

    def idx_scores(j, nblocks):
        k_blk = kidx_ref[key_rows(j, nblocks), :]
        acc = jnp.zeros((nblocks * T, T), F32)
        for p in range(NP):
            s = _dot_nt(k_blk, qim_ref[p])
            for e in range(2):
                h = 2 * p + e
                acc = acc + w_rows[h:h + 1, :] * jnp.maximum(s[:, e * T:(e + 1) * T], 0.0)
        return acc

    def fox_logits(j, nblocks, slot, mask=None):
        for p in range(NP):
            s = _dot_nt(kaug_ref[p, key_rows(j, nblocks), :], qaug_ref[p])
            if mask is not None:
                s = jnp.where(jnp.concatenate([mask, mask], axis=1), s, MASKED)
            s_ref[slot, slot_rows(nblocks), p * 2 * T:(p + 1) * 2 * T] = s

    def fox_absorb(j, nblocks, slot):
        for p in range(NP):
            online_update(p, s_ref[slot, slot_rows(nblocks), p * 2 * T:(p + 1) * 2 * T],
                          blocks_t(lambda jb: vt_ref[p, jb], j, nblocks))

    def idx_fox_logits(j, nblocks, slot):
        score_ref[pl.ds(j, nblocks)] = idx_scores(j, nblocks).reshape(nblocks, T, T)
        fox_logits(j, nblocks, slot)

    reset_state()
    for_far_blocks(n_far, idx_fox_logits, fox_absorb)

    @pl.when(i >= 1)
    def _idx_fox_tail():
        score_ref[pl.ds(i - 1, 2)] = (
            jnp.where(causal_tail, idx_scores(i - 1, 2), -jnp.inf).reshape(2, T, T))
        fox_logits(i - 1, 2, 0, causal_tail)
        fox_absorb(i - 1, 2, 0)

    @pl.when(i == 0)
    def _idx_fox_first():
        score_ref[0] = jnp.where(key_i <= qry_i, idx_scores(0, 1), -jnp.inf)
        fox_logits(0, 1, 0, key_i <= qry_i)
        fox_absorb(0, 1, 0)

    score_ref[i + 1] = jnp.full((T, T), -jnp.inf, F32)

    row_lo = lax.broadcasted_iota(jnp.int32, (LANES, T), 0) < HEAD_DIM
    for p in range(NP):
        c0 = slice(p * 2 * T, p * 2 * T + T)
        c1 = slice(p * 2 * T + T, (p + 1) * 2 * T)
        o_t = jnp.where(row_lo, acc_ref[0:LANES, c0] / acc_ref[LANES:LANES + 1, c0],
                        acc_ref[0:LANES, c1] / acc_ref[LANES:LANES + 1, c1])
        o_ref[:, MIX_HALF + p * LANES:MIX_HALF + (p + 1) * LANES] = o_t.T.astype(o_ref.dtype)

    kf = float(k_sel)
    t_pos = (i * T + lax.broadcasted_iota(jnp.int32, (SUBLANES, LANES), 1)).astype(F32)
    searching = (t_pos + 1.0) > kf
    G2 = 2 * G
    n_pair = i // 2 + 1
    key_in_pair = (lax.broadcasted_iota(jnp.int32, (G2, SUBLANES, LANES), 0) * SUBLANES
                   + lax.broadcasted_iota(jnp.int32, (G2, SUBLANES, LANES), 1))

    def rep(v):
        return jnp.broadcast_to(v, (SUBLANES, LANES))

    def score_pair(jj):
        return score_ref[pl.ds(2 * jj, 2)].reshape(G2, SUBLANES, LANES)

    def count(*pred_fns, load=score_pair):
        def add_pair(jj, cs):
            s = load(jj)
            return tuple(c + _tree(jnp.add, jnp.where(f(s, jj), 1.0, 0.0)) for c, f in zip(cs, pred_fns))

        def two_pairs(kk, cs):
            return add_pair(2 * kk + 1, add_pair(2 * kk, cs))

        zero = jnp.zeros((SUBLANES, LANES), F32)
        cs = lax.fori_loop(0, lax.shift_right_logical(n_pair, 1), two_pairs, (zero,) * len(pred_fns))
        cs = lax.cond((n_pair & 1) != 0, lambda: add_pair(n_pair - 1, cs), lambda: cs)
        return tuple(rep(jnp.sum(c, axis=0, keepdims=True)) for c in cs)

    def count_ge(thr):
        return count(lambda s, jj: s >= thr[None])[0]

    def select_threshold():
        def mm_body(jj, carry):
            mn, mx = carry
            s = score_pair(jj)
            return (jnp.minimum(mn, _tree(jnp.minimum, jnp.where(s == -jnp.inf, jnp.inf, s))),
                    jnp.maximum(mx, _tree(jnp.maximum, s)))

        mn, mx = lax.fori_loop(0, n_pair, mm_body,
                               (jnp.full((SUBLANES, LANES), jnp.inf, F32),
                                jnp.full((SUBLANES, LANES), -jnp.inf, F32)))
        mn = rep(jnp.min(mn, axis=0, keepdims=True))
        mx = rep(jnp.max(mx, axis=0, keepdims=True))
        above_mx = mx + jnp.maximum(jnp.abs(mx) * 2.0 ** -20, 1e-30)
        c_ge0, c_gt0 = count(lambda s, jj: s >= 0.0, lambda s, jj: s > 0.0)
        thr_ge0 = c_ge0 >= kf
        thr_gt0 = c_gt0 >= kf
        lo0 = jnp.where(thr_ge0, 0.0, mn)
        cnt_lo0 = jnp.where(thr_ge0, c_ge0, t_pos + 1.0)
        hi0 = jnp.where(thr_gt0, above_mx, 0.0)
        cnt_hi0 = jnp.where(thr_gt0, 0.0, jnp.where(thr_ge0, c_gt0, c_ge0))
        undecided = thr_gt0 | jnp.logical_not(thr_ge0)
        active0 = jnp.where(searching & undecided & (cnt_lo0 > kf), 1.0, 0.0)

        def cond(st):
            return jnp.logical_and(st[5] > 0.0, st[6] < 2200)

        def bisect(st):
            lo, hi, cnt_lo, cnt_hi, active = st
            mid = 0.5 * lo + 0.5 * hi
            adjacent = (mid <= lo) | (mid >= hi)
            c = count_ge(mid)
            act = active > 0.0
            go_lo = act & jnp.logical_not(adjacent) & (c >= kf)
            go_hi = act & jnp.logical_not(adjacent) & (c < kf)
            lo = jnp.where(go_lo, mid, lo)
            cnt_lo = jnp.where(go_lo, c, cnt_lo)
            hi = jnp.where(go_hi, mid, hi)
            cnt_hi = jnp.where(go_hi, c, cnt_hi)
            active = jnp.where(act & jnp.logical_not(adjacent) & (cnt_lo > kf), 1.0, 0.0)
            return lo, hi, cnt_lo, cnt_hi, active

        def body(st):
            inner = st[:5]
            for _ in range(SEARCH_UNROLL):
                inner = bisect(inner)
            return inner + (jnp.max(inner[4]), st[6] + SEARCH_UNROLL)

        lo, hi, cnt_lo, cnt_hi, _, _, _ = lax.while_loop(
            cond, body, (lo0, hi0, cnt_lo0, cnt_hi0, active0, jnp.max(active0), 0))
        thr = jnp.where(searching, lo, -F32_MAX)

        tied = searching & (cnt_lo > kf)
        need = kf - cnt_hi

        @pl.when(jnp.max(jnp.where(tied, 1.0, 0.0)) > 0.0)
        def _break_ties():
            def tie_pair(jj):
                return tie_ref[pl.ds(2 * jj, 2)].reshape(G2, SUBLANES, LANES)

            def mark_body(jj, c):
                key_index = (2 * jj * T + key_in_pair).astype(F32)
                tie_ref[pl.ds(2 * jj, 2)] = (
                    jnp.where(score_pair(jj) == thr[None], key_index, jnp.inf).reshape(2, T, T))
                return c
            lax.fori_loop(0, n_pair, mark_body, 0)

            def count_tied_upto(jcut):
                return count(lambda e, jj: e <= jcut[None], load=tie_pair)[0]

            def bs_body(_, st):
                jl, jh = st
                jm = jnp.floor(0.5 * (jl + jh))
                ok = count_tied_upto(jm) >= need
                return jnp.where(ok, jl, jm), jnp.where(ok, jm, jh)

            n_keys = (nb * T).astype(F32)
            jl0 = jnp.full((SUBLANES, LANES), -1.0, F32)
            jh0 = jnp.zeros((SUBLANES, LANES), F32) + (n_keys - 1.0)
            n_steps = int(math.ceil(math.log2(nblk * T))) + 1
            _, jcut = lax.fori_loop(0, n_steps, bs_body, (jl0, jh0))

            def drop_body(jj, c):
                e = tie_pair(jj)
                drop = tied[None] & (e > jcut[None]) & (e < jnp.inf)
                score_ref[pl.ds(2 * jj, 2)] = jnp.where(drop, -jnp.inf, score_pair(jj)).reshape(2, T, T)
                return c
            lax.fori_loop(0, n_pair, drop_body, 0)

        return thr

    thr = lax.cond((i + 1) * T > k_sel, select_threshold,
                   lambda: jnp.full((SUBLANES, LANES), -F32_MAX, F32))

    def mask_body(jj, c):
        lhsd_ref[key_rows(2 * jj, 2), KV_RANK:2 * KV_RANK] = (
            jnp.where(score_pair(jj) >= thr[None], 0.0, MASKED).reshape(2 * T, T).astype(BF16))
        return c

    lax.fori_loop(0, n_pair, mask_body, 0)

    def dsa_logits(j, nblocks, slot, bias=None):
        lhs = lhsd_ref[key_rows(j, nblocks), :]
        for p in range(NP):
            cols = slice(p * 2 * T, (p + 1) * 2 * T)
            s = _dot_nt(lhs, qsel_ref[p])
            s_ref[slot, slot_rows(nblocks), cols] = s if bias is None else s + bias(cols)

    def dsa_absorb(j, nblocks, slot):
        pv_lhs = blocks_t(lambda jb: ct_ref[jb], j, nblocks)
        for p in range(NP):
            online_update(p, s_ref[slot, slot_rows(nblocks), p * 2 * T:(p + 1) * 2 * T], pv_lhs)

    reset_state()
    for_far_blocks(n_far, dsa_logits, dsa_absorb)
    m_ref[...] = m_ref[...] + bfar_ref[...]

    @pl.when(i >= 1)
    def _dsa_tail():
        dsa_logits(i - 1, 2, 0, lambda cols: bnear_ref[:, :, cols].reshape(2 * T, 2 * T))
        dsa_absorb(i - 1, 2, 0)

    @pl.when(i == 0)
    def _dsa_first():
        dsa_logits(0, 1, 0, lambda cols: bnear_ref[1, :, cols])
        dsa_absorb(0, 1, 0)

    for p in range(NP):
        o_pair = jnp.zeros((T, LANES), F32)
        for e in range(2):
            cols = slice(p * 2 * T + e * T, p * 2 * T + (e + 1) * T)
            o_t = acc_ref[0:KV_RANK, cols] / acc_ref[KV_RANK:KV_RANK + 1, cols]
            o_pair = o_pair + _dot(o_t.T.astype(BF16), wuv_ref[2 * p + e])
        o_ref[:, p * LANES:(p + 1) * LANES] = o_pair.astype(o_ref.dtype)


_EV_QA, _EV_QI, _EV_QB, _EV_KB, _EV_VB = 0, 1, 2, 3, 4
_EV_CKV, _EV_KIDX = 20, 21


def _attention(hb, w_t, f_rows, wuk_pad, wuv_pad, bias_near, bias_far, place, k_sel):
    bsz, seq, _ = hb.shape
    T = BLK
    nblk = seq // T
    H = N_HEADS
    pq, cq, pk, ck = place
    wide = lambda c: pl.BlockSpec((None, T, MIX_HALF), lambda b, i, c=c: (b, i, c))
    once = dict(pipeline_mode=pl.Buffered(1))
    full_wide = lambda c: pl.BlockSpec((None, seq, MIX_HALF), lambda b, i, c=c: (b, 0, c), **once)
    full_narrow = lambda c: pl.BlockSpec((None, seq, LANES), lambda b, i, c=c: (b, 0, c), **once)
    const = lambda a: pl.BlockSpec(a.shape, lambda b, i, nd=a.ndim: (0,) * nd)
    return pl.pallas_call(
        functools.partial(_attn_kernel, k_sel=k_sel),
        grid=(bsz, nblk),
        in_specs=[wide(_EV_QA), wide(_EV_QI), wide(_EV_QB),
                  pl.BlockSpec((None, H, T), lambda b, i: (b, 0, i)),
                  pl.BlockSpec((None, T, LANES), lambda b, i: (b, i, 0)),
                  full_wide(_EV_KB), full_wide(_EV_VB),
                  full_narrow(_EV_CKV), full_narrow(_EV_KIDX),
                  pl.BlockSpec((None, seq, LANES), lambda b, i: (b, 0, 0), **once),
                  const(wuk_pad), const(wuv_pad), const(bias_near), const(bias_far),
                  const(pq), const(cq), const(pk), const(ck)],
        out_specs=pl.BlockSpec((None, T, 2 * MIX_HALF), lambda b, i: (b, i, 0)),
        out_shape=jax.ShapeDtypeStruct((bsz, seq, 2 * MIX_HALF), BF16),
        scratch_shapes=[pltpu.VMEM((nblk + 2, T, T), F32),
                        pltpu.VMEM((seq + 2 * T, 2 * KV_RANK), BF16),
                        pltpu.VMEM((nblk, PV_ROWS, T), BF16),
                        pltpu.VMEM((H // 2, seq, 2 * LANES), BF16),
                        pltpu.VMEM((H // 2, nblk, PV_ROWS, T), BF16),
                        pltpu.VMEM((H // 2, 2 * T, 2 * KV_RANK), BF16),
                        pltpu.VMEM((H // 2, 2 * T, 2 * LANES), BF16),
                        pltpu.VMEM((H // 2, 2 * T, LANES), BF16),
                        pltpu.VMEM((SUBLANES, H * T), F32),
                        pltpu.VMEM((PV_ROWS, H * T), F32),
                        pltpu.VMEM((2, 2 * T, H * T), F32),
                        pltpu.VMEM((nblk + 2, T, T), F32)],
        compiler_params=_cparams(("parallel", "arbitrary")),
        name="dsa_fox_attention",
    )(hb, hb, hb, w_t, f_rows, hb, hb, hb, hb, f_rows, wuk_pad, wuv_pad, bias_near, bias_far,
      pq, cq, pk, ck)


def _proj_ln_kernel(a_ref, w_ref, x_ref, g_ref, b_ref, xo_ref, xb_ref, *, alpha, sub):
    for c in range(a_ref.shape[0] // sub):
        rows = slice(c * sub, (c + 1) * sub)
        y = alpha * x_ref[rows, :] + _dot(a_ref[rows, :], w_ref[...])
        out = _layer_norm_rows(y, g_ref[...], b_ref[...])
        xo_ref[rows, :] = out
        xb_ref[rows, :] = out.astype(BF16)


def _proj_ln(a, w, x, g, b, alpha, tm):
    n, k = a.shape
    d = w.shape[1]
    return pl.pallas_call(
        functools.partial(_proj_ln_kernel, alpha=alpha, sub=min(128, tm)),
        grid=(n // tm,),
        in_specs=[pl.BlockSpec((tm, k), lambda i: (i, 0)),
                  pl.BlockSpec((k, d), lambda i: (0, 0)),
                  pl.BlockSpec((tm, d), lambda i: (i, 0)),
                  pl.BlockSpec((1, d), lambda i: (0, 0)),
                  pl.BlockSpec((1, d), lambda i: (0, 0))],
        out_specs=[pl.BlockSpec((tm, d), lambda i: (i, 0)),
                   pl.BlockSpec((tm, d), lambda i: (i, 0))],
        out_shape=[jax.ShapeDtypeStruct((n, d), F32),
                   jax.ShapeDtypeStruct((n, d), BF16)],
        compiler_params=_cparams(("parallel",)),
        name="proj_residual_ln",
    )(a, w, x, g, b)


def _causal_conv3(h, tail, cw):
    w0, w1, w2 = cw[0:1, :], cw[1:2, :], cw[2:3, :]
    body = w0 * pltpu.roll(h, 2, axis=0) + w1 * pltpu.roll(h, 1, axis=0) + w2 * h
    head = h[0:SUBLANES, :]
    r = lax.broadcasted_iota(jnp.int32, head.shape, 0)
    hm1 = jnp.where(r < 1, pltpu.roll(tail, 1, axis=0), pltpu.roll(head, 1, axis=0))
    hm2 = jnp.where(r < 2, pltpu.roll(tail, 2, axis=0), pltpu.roll(head, 2, axis=0))
    out_head = w0 * hm2 + w1 * hm1 + w2 * head
    return jnp.concatenate([out_head, body[SUBLANES:, :]], axis=0)


def _ffn_up_kernel(x_ref, wg_ref, wv_ref, cg_ref, cv_ref, o_ref, tg_ref, tv_ref, *h_refs,
                   tiles_per_seq, sub):
    n_chunks = x_ref.shape[0] // sub
    hg_refs, hv_refs = h_refs[:n_chunks], h_refs[n_chunks:]
    first = (pl.program_id(1) % tiles_per_seq) == 0
    body_rows = slice(SUBLANES, SUBLANES + sub)

    def matmuls(c):
        x = x_ref[c * sub:(c + 1) * sub, :]
        hg_refs[c][body_rows, :] = _dot(x, wg_ref[...])
        hv_refs[c][body_rows, :] = _dot(x, wv_ref[...])

    def conv(h_ref, cw):
        return (cw[0:1, :] * h_ref[SUBLANES - 2:SUBLANES - 2 + sub, :]
                + cw[1:2, :] * h_ref[SUBLANES - 1:SUBLANES - 1 + sub, :]
                + cw[2:3, :] * h_ref[body_rows, :])

    def gate(c):
        if c == 0:
            hg_refs[0][0:SUBLANES, :] = jnp.where(first, 0.0, tg_ref[...])
            hv_refs[0][0:SUBLANES, :] = jnp.where(first, 0.0, tv_ref[...])
        else:
            hg_refs[c][0:SUBLANES, :] = hg_refs[c - 1][sub:sub + SUBLANES, :]
            hv_refs[c][0:SUBLANES, :] = hv_refs[c - 1][sub:sub + SUBLANES, :]
        g = conv(hg_refs[c], cg_ref[...])
        v = conv(hv_refs[c], cv_ref[...])
        o_ref[c * sub:(c + 1) * sub, :] = (g / (1.0 + jnp.exp(-g)) * v).astype(o_ref.dtype)

    matmuls(0)
    for c in range(n_chunks):
        if c + 1 < n_chunks:
            matmuls(c + 1)
        gate(c)
    tg_ref[...] = hg_refs[n_chunks - 1][sub:sub + SUBLANES, :]
    tv_ref[...] = hv_refs[n_chunks - 1][sub:sub + SUBLANES, :]


def _ffn_up(xb, w_up, conv_w, seq, tm, tn, sub):
    n, k = xb.shape
    dff = w_up.shape[1] // 2
    ncol = dff // tn
    return pl.pallas_call(
        functools.partial(_ffn_up_kernel, tiles_per_seq=seq // tm, sub=sub),
        grid=(ncol, n // tm),
        in_specs=[pl.BlockSpec((tm, k), lambda j, i: (i, 0)),
                  pl.BlockSpec((k, tn), lambda j, i: (0, j)),
                  pl.BlockSpec((k, tn), lambda j, i, ncol=ncol: (0, j + ncol)),
                  pl.BlockSpec((CONV_W, tn), lambda j, i: (0, j)),
                  pl.BlockSpec((CONV_W, tn), lambda j, i, ncol=ncol: (0, j + ncol))],
        out_specs=pl.BlockSpec((tm, tn), lambda j, i: (i, j)),
        out_shape=jax.ShapeDtypeStruct((n, dff), BF16),
        scratch_shapes=([pltpu.VMEM((SUBLANES, tn), F32)] * 2
                        + [pltpu.VMEM((sub + SUBLANES, tn), F32)] * (2 * (tm // sub))),
        compiler_params=_cparams(("arbitrary", "arbitrary")),
        name="ffn_up_conv_gate",
    )(xb, w_up, w_up, conv_w, conv_w)


def _gelu_tanh(x):
    return 0.5 * x * (1.0 + jnp.tanh(math.sqrt(2.0 / math.pi) * (x + 0.044715 * (x * x * x))))


def _group_mean(v, avg):
    hi = v.astype(BF16)
    lo = (v - hi.astype(F32)).astype(BF16)
    return _dot(hi, avg) + _dot(lo, avg)


def _odd_mixer_kernel(x_ref, w_ref, sg_ref, sw_ref, sb_ref, cw_ref, o_ref, tail_ref, *, tiles_per_seq):
    tm = x_ref.shape[0]
    W = MIX_HALF
    first = (pl.program_id(0) % tiles_per_seq) == 0
    x = x_ref[...]
    lane = lax.broadcasted_iota(jnp.int32, (CHUNK, LANES), 1)
    lo_half = lane < HEAD_DIM
    r = lax.broadcasted_iota(jnp.int32, (W, W), 0) // HEAD_DIM
    c = lax.broadcasted_iota(jnp.int32, (W, W), 1) // HEAD_DIM
    avg = jnp.where(r == c, 1.0 / HEAD_DIM, 0.0).astype(BF16)
    tri = (lax.broadcasted_iota(jnp.int32, (CHUNK, CHUNK), 1)
           <= lax.broadcasted_iota(jnp.int32, (CHUNK, CHUNK), 0))

    u = _gelu_tanh(_dot(x, w_ref[:, 0:W]))
    v = _gelu_tanh(_dot(x, w_ref[:, W:2 * W]))
    vc = v - _group_mean(v, avg)
    var = _group_mean(vc * vc, avg)
    vn = (vc * lax.rsqrt(var + LN_EPS) * sg_ref[...]).astype(BF16)
    for n in range(tm // CHUNK):
        rows = slice(n * CHUNK, (n + 1) * CHUNK)
        for p in range(N_HEADS // 2):
            cols = slice(p * LANES, (p + 1) * LANES)
            vp = vn[rows, cols]
            w_e = jnp.where(tri, sw_ref[2 * p], 0.0).astype(BF16)
            w_o = jnp.where(tri, sw_ref[2 * p + 1], 0.0).astype(BF16)
            mix = (_dot(w_e, jnp.where(lo_half, vp, 0)) + _dot(w_o, jnp.where(lo_half, 0, vp))
                   + sb_ref[:, cols])
            o_ref[rows, cols] = (u[rows, cols] * mix).astype(o_ref.dtype)

    g_b = _dot(x, w_ref[:, 2 * W:3 * W])
    y = _dot(x, w_ref[:, 3 * W:4 * W]) * _dot(x, w_ref[:, 4 * W:5 * W])
    conv = _causal_conv3(y, jnp.where(first, 0.0, tail_ref[...]), cw_ref[...])
    tail_ref[...] = y[tm - SUBLANES:tm, :]
    o_ref[:, W:2 * W] = (g_b * conv).astype(o_ref.dtype)


def _odd_mixer(xb, w_in, sgu_g, sgu_w, sgu_b_tile, conv_w, seq, tm):
    n, k = xb.shape
    W = MIX_HALF
    return pl.pallas_call(
        functools.partial(_odd_mixer_kernel, tiles_per_seq=seq // tm),
        grid=(n // tm,),
        in_specs=[pl.BlockSpec((tm, k), lambda i: (i, 0)),
                  pl.BlockSpec((k, 5 * W), lambda i: (0, 0)),
                  pl.BlockSpec((1, W), lambda i: (0, 0)),
                  pl.BlockSpec((N_HEADS, CHUNK, CHUNK), lambda i: (0, 0, 0)),
                  pl.BlockSpec((CHUNK, W), lambda i: (0, 0)),
                  pl.BlockSpec((CONV_W, W), lambda i: (0, 0))],
        out_specs=pl.BlockSpec((tm, 2 * W), lambda i: (i, 0)),
        out_shape=jax.ShapeDtypeStruct((n, 2 * W), BF16),
        scratch_shapes=[pltpu.VMEM((SUBLANES, W), F32)],
        compiler_params=_cparams(("arbitrary",)),
        name="odd_mixer",
    )(xb, w_in, sgu_g, sgu_w, sgu_b_tile, conv_w)


def _t5_bucket_np(dist):
    max_exact = N_BUCKETS // 2
    n = np.maximum(dist, 0)
    nf = np.maximum(n, 1).astype(np.float32)
    large = max_exact + (np.log(nf / max_exact) / np.float32(math.log(T5_MAX_DIST / max_exact))
                         * (N_BUCKETS - max_exact)).astype(np.int32)
    large = np.minimum(large, N_BUCKETS - 1)
    return np.where(n < max_exact, n, large)


def _bias_tables(rel_bias):
    T = BLK
    s = np.arange(T)[:, None]
    t = np.arange(T)[None, :]
    bucket = _t5_bucket_np(np.stack([t + T - s, t - s]))
    far_bucket = _t5_bucket_np(np.array([T + 1]))[0]
    assert far_bucket == N_BUCKETS - 1
    assert (_t5_bucket_np(np.arange(T + 1, 64 * T)) == far_bucket).all()
    rb = rel_bias.astype(F32) * LOG2E
    tiles = sum(jnp.where(jnp.asarray(bucket == b)[..., None], rb[b], 0.0) for b in range(N_BUCKETS))
    near = jnp.swapaxes(tiles, 2, 3).reshape(2, T, N_HEADS * T)
    far = jnp.repeat(rb[far_bucket], T)[None, :]
    return near, far


def _placement_constants():
    H, L = N_HEADS, LANES
    pq = np.zeros((3 * L, H * L), np.float32)
    cq = np.zeros((1, H * L), np.float32)
    pk = np.zeros((3 * L, (H // 2) * L), np.float32)
    ck = np.zeros((1, (H // 2) * L), np.float32)
    for h in range(H):
        p, e = divmod(h, 2)
        for piece in range(3):
            pq[piece * L + h, h * L + 6 + piece] = 1.0
            cq[0, h * L + 3 * e + piece] = 1.0
            pk[piece * L + h, p * L + 3 * e + piece] = -1.0
            ck[0, p * L + 6 + piece] = 1.0
    return jnp.asarray(pq, BF16), jnp.asarray(cq), jnp.asarray(pk, BF16), jnp.asarray(ck)


def _even_weights(w_in, w_uk, w_uv):
    hd, H = HEAD_DIM, N_HEADS
    widths = (H * hd, KV_RANK, H * hd, hd, H, H * hd, H * hd, H * hd, H)
    offs = np.cumsum((0,) + widths)
    seg = lambda s: w_in[:, offs[s]:offs[s + 1]]
    q_a, c_kv, q_idx, k_idx, w_idx, q_b, k_b, v_b, f_in = [seg(s) for s in range(9)]
    scale = hd ** -0.5
    w_big = jnp.concatenate([q_a, q_idx * scale, q_b * (scale * LOG2E), k_b, v_b, c_kv,
                             k_idx, k_idx], axis=1).astype(BF16)
    pad = jnp.zeros((w_in.shape[0], LANES - 2 * H), w_in.dtype)
    w_small = jnp.concatenate([w_idx, f_in, pad], axis=1).astype(BF16)
    wuk_t = jnp.swapaxes(w_uk, 1, 2) * (scale * LOG2E)
    wuk_pad = jnp.zeros((H, LANES, KV_RANK), F32)
    wuv_pad = jnp.zeros((H, KV_RANK, LANES), F32)
    for h in range(H):
        o = hd * (h % 2)
        wuk_pad = wuk_pad.at[h, o:o + hd, :].set(wuk_t[h])
        wuv_pad = wuv_pad.at[h, :, o:o + hd].set(w_uv[h])
    return w_big, w_small, wuk_pad.astype(BF16), wuv_pad.astype(BF16)


def kernel(x, ln_g, ln_b, rel_bias, ev_w_in, ev_w_uk, ev_w_uv, ev_b_f, ev_w_o,
           od_w_in, od_sgu_g, od_sgu_w, od_sgu_b, od_conv_w, od_w_o,
           ffn_w_up, ffn_conv_w, ffn_w_down):
    bsz, seq, d = x.shape
    depth = ln_g.shape[0]
    n = bsz * seq
    alpha = (2.0 * depth) ** 0.25
    k_sel = min(TOPK_MAX, seq // 4)
    H = N_HEADS
    assert seq % BLK == 0 and seq % CHUNK == 0
    tm = min(512, seq)
    tm_ffn = min(1024, seq)

    bias_near, bias_far = _bias_tables(rel_bias)
    place = _placement_constants()
    xf = x.reshape(n, d)
    xb = xf.astype(BF16)
    for layer in range(depth):
        j = layer // 2
        g0, b0 = ln_g[layer, 0][None, :], ln_b[layer, 0][None, :]
        g1, b1 = ln_g[layer, 1][None, :], ln_b[layer, 1][None, :]
        if layer % 2 == 0:
            w_big, w_small, wuk_pad, wuv_pad = _even_weights(ev_w_in[j], ev_w_uk[j], ev_w_uv[j])
            hb, hs = _even_proj(xb, w_big, w_small, tm)
            hb = hb.reshape(bsz, seq, -1)
            hs_t = jnp.swapaxes(hs.reshape(bsz, seq, -1)[:, :, 0:2 * H], 1, 2)
            cum_f = _cumf(hs_t[:, H:2 * H], ev_b_f[j][:, None])
            f_rows = jnp.pad(jnp.swapaxes(cum_f, 1, 2), ((0, 0), (0, 0), (0, LANES - H)))
            mix = _attention(hb, hs_t[:, 0:H], f_rows, wuk_pad, wuv_pad, bias_near, bias_far,
                             place, k_sel)
            w_o = ev_w_o[j]
        else:
            sgu_b_tile = jnp.repeat(jnp.swapaxes(od_sgu_b[j], 0, 1), HEAD_DIM, axis=1)
            mix = _odd_mixer(xb, od_w_in[j].astype(BF16), od_sgu_g[j].reshape(1, -1), od_sgu_w[j],
                             sgu_b_tile, od_conv_w[j], seq, min(256, seq))
            w_o = od_w_o[j]
        xf, xb = _proj_ln(mix.reshape(n, d), w_o.astype(BF16), xf, g0, b0, alpha, tm)
        act = _ffn_up(xb, ffn_w_up[layer].astype(BF16), ffn_conv_w[layer], seq, tm_ffn, 256,
                      min(512, seq))
        xf, xb = _proj_ln(act, ffn_w_down[layer].astype(BF16), xf, g1, b1, alpha, tm)
    return xf.reshape(bsz, seq, d)
```

```python
import functools
import math

import numpy as np
import jax
import jax.numpy as jnp
from jax import lax
from jax.experimental import pallas as pl
from jax.experimental.pallas import tpu as pltpu

F32 = jnp.float32
BF16 = jnp.bfloat16

HEAD_DIM = 64
N_HEADS = 8
KV_RANK = 128
TOPK_MAX = 256
N_BUCKETS = 32
T5_MAX_DIST = 128
CHUNK = 128
CONV_W = 3
LN_EPS = 1e-5
MIX_HALF = N_HEADS * HEAD_DIM

LANES = 128
SUBLANES = 8
BF16_ROWS = 16
VMEM_LIMIT = 56 * 1024 * 1024

BLK = 128
QUERIES = 2 * BLK
PV_ROWS = KV_RANK + BF16_ROWS
NEG_BIG = -1e30
MASKED = -(2.0 ** 100)
F32_MAX = float(np.finfo(np.float32).max)
LOG2E = math.log2(math.e)
SEARCH_UNROLL = 4


def _cparams(sem):
    return pltpu.CompilerParams(dimension_semantics=sem, vmem_limit_bytes=VMEM_LIMIT)


def _dot(a, b):
    return jnp.dot(a, b, preferred_element_type=F32)


def _dot_nt(a, b):
    return lax.dot_general(a, b, (((1,), (1,)), ((), ())), preferred_element_type=F32)


def _layer_norm_rows(y, g, b):
    mu = jnp.mean(y, axis=-1, keepdims=True)
    yc = y - mu
    var = jnp.mean(yc * yc, axis=-1, keepdims=True)
    return yc * lax.rsqrt(var + LN_EPS) * g + b


def _tree(op, x):
    while x.shape[0] > 1:
        half = x.shape[0] // 2
        x = op(x[:half], x[half:])
    return x[0]


def _bf16_pieces(v):
    hi = v.astype(BF16)
    r1 = v - hi.astype(F32)
    mid = r1.astype(BF16)
    lo = (r1 - mid.astype(F32)).astype(BF16)
    return jnp.concatenate([hi, mid, lo], axis=1)


def _even_proj_kernel(x_ref, wb_ref, ws_ref, hb_ref, hs_ref):
    x = x_ref[...]
    hb_ref[...] = _dot(x, wb_ref[...]).astype(hb_ref.dtype)
    hs_ref[...] = _dot(x, ws_ref[...])


def _even_proj(xb, w_big, w_small, tm):
    n, k = xb.shape
    mb = w_big.shape[1]
    ms = w_small.shape[1]
    return pl.pallas_call(
        _even_proj_kernel,
        grid=(n // tm,),
        in_specs=[pl.BlockSpec((tm, k), lambda i: (i, 0)),
                  pl.BlockSpec((k, mb), lambda i: (0, 0)),
                  pl.BlockSpec((k, ms), lambda i: (0, 0))],
        out_specs=[pl.BlockSpec((tm, mb), lambda i: (i, 0)),
                   pl.BlockSpec((tm, ms), lambda i: (i, 0))],
        out_shape=[jax.ShapeDtypeStruct((n, mb), BF16),
                   jax.ShapeDtypeStruct((n, ms), F32)],
        compiler_params=_cparams(("parallel",)),
        name="even_proj",
    )(xb, w_big, w_small)


def _cumf_kernel(f_ref, b_ref, o_ref):
    seq = f_ref.shape[-1]
    row = lax.broadcasted_iota(jnp.int32, (LANES, LANES), 0)
    col = lax.broadcasted_iota(jnp.int32, (LANES, LANES), 1)
    upper = jnp.where(row <= col, 1.0, 0.0).astype(F32)
    carry = jnp.zeros((f_ref.shape[0], 1), F32)
    for c in range(seq // LANES):
        z = f_ref[:, c * LANES:(c + 1) * LANES] + b_ref[...]
        lf = jnp.minimum(z, 0.0) - jnp.log1p(jnp.exp(-jnp.abs(z)))
        cs = jnp.dot(lf, upper, preferred_element_type=F32,
                     precision=lax.Precision.HIGHEST) + carry
        o_ref[:, c * LANES:(c + 1) * LANES] = cs
        carry = cs[:, LANES - 1:LANES]


def _cumf(f_t, b_f):
    bsz, nh, seq = f_t.shape
    return pl.pallas_call(
        _cumf_kernel,
        grid=(bsz,),
        in_specs=[pl.BlockSpec((None, nh, seq), lambda b: (b, 0, 0)),
                  pl.BlockSpec((nh, 1), lambda b: (0, 0))],
        out_specs=pl.BlockSpec((None, nh, seq), lambda b: (b, 0, 0)),
        out_shape=jax.ShapeDtypeStruct((bsz, nh, seq), F32),
        compiler_params=_cparams(("parallel",)),
        name="fox_cumf",
    )(f_t, b_f)


def _attn_kernel(qa_ref, qi_ref, qb_ref, wt_ref, fq_ref, kb_ref, vb_ref, ckv_ref, kidx_ref, fk_ref,
                 wuk_ref, wuv_ref, bias_ref, bfar_ref, pq_ref, cq_ref, pk_ref, ck_ref,
                 o_ref,
                 score_ref, lhsd_ref, ct_ref, kaug_ref, vt_ref, qsel_ref, qaug_ref, qim_ref,
                 m_ref, acc_ref, s_ref, tie_ref, *, k_sel):
    T = BLK
    Q = QUERIES
    H = N_HEADS
    NP = H // 2
    NT = 2 * NP
    TW = 2 * T
    g = pl.program_id(1)
    i0 = 2 * g
    nblk = ct_ref.shape[0]
    lane = lax.broadcasted_iota(jnp.int32, (T, LANES), 1)
    lo_half = lane < HEAD_DIM
    ones_rows = jnp.ones((BF16_ROWS, T), BF16)
    causal_qq = (lax.broadcasted_iota(jnp.int32, (Q, Q), 0) <= lax.broadcasted_iota(jnp.int32, (Q, Q), 1))

    @pl.when(g == 0)
    def _per_batch():
        def prep(j, c):
            r = pl.ds(pl.multiple_of(j * T, T), T)
            cb = ckv_ref[r, :]
            lhsd_ref[0, r, 0:KV_RANK] = cb
            lhsd_ref[1, r, 0:KV_RANK] = cb
            ct_ref[j, 0:KV_RANK, :] = cb.astype(F32).T.astype(BF16)
            ct_ref[j, KV_RANK:PV_ROWS, :] = ones_rows
            ext = (_dot(_bf16_pieces(fk_ref[r, :] * LOG2E), pk_ref[...]) + ck_ref[...]).astype(BF16)
            for p in range(NP):
                cols = slice(p * LANES, (p + 1) * LANES)
                kaug_ref[p, r, 0:LANES] = kb_ref[r, cols]
                kaug_ref[p, r, LANES:2 * LANES] = ext[:, cols]
                vt_ref[p, j, 0:LANES, :] = vb_ref[r, cols].astype(F32).T.astype(BF16)
                vt_ref[p, j, LANES:PV_ROWS, :] = ones_rows
            return c
        lax.fori_loop(0, nblk, prep, 0)

    w_rows = wt_ref[...] * (H ** -0.5)
    ext_q = (_dot(_bf16_pieces(fq_ref[...] * LOG2E), pq_ref[...]) + cq_ref[...]).astype(BF16)
    eye = jnp.where(lax.broadcasted_iota(jnp.int32, (T, T), 0) == lax.broadcasted_iota(jnp.int32, (T, T), 1),
                    1.0, 0.0).astype(BF16)
    for tt in range(NT):
        hf, p = divmod(tt, NP)
        qrows = slice(hf * T, (hf + 1) * T)
        cols = slice(p * LANES, (p + 1) * LANES)
        qa_pair, qi_pair, qb_pair = qa_ref[qrows, cols], qi_ref[qrows, cols], qb_ref[qrows, cols]
        for e in range(2):
            h = 2 * p + e
            rows = slice(e * T, (e + 1) * T)
            mine = lo_half if e == 0 else jnp.logical_not(lo_half)
            qsel_ref[tt, rows, 0:KV_RANK] = _dot(qa_pair, wuk_ref[h]).astype(BF16)
            qsel_ref[tt, rows, KV_RANK:2 * KV_RANK] = eye
            qaug_ref[tt, rows, 0:LANES] = jnp.where(mine, qb_pair, 0)
            qaug_ref[tt, rows, LANES:2 * LANES] = ext_q[qrows, h * LANES:(h + 1) * LANES]
            qim_ref[tt, rows, :] = jnp.where(mine, qi_pair, 0)

    def reset_state():
        m_ref[...] = jnp.full(m_ref.shape, NEG_BIG, F32)
        acc_ref[...] = jnp.zeros(acc_ref.shape, F32)

    def tile_cols(tt):
        return slice(tt * TW, (tt + 1) * TW)

    def online_update(tt, s, pv_lhs):
        cols = tile_cols(tt)
        m_old = m_ref[:, cols]
        s_max = _tree(jnp.maximum, s.reshape(-1, SUBLANES, TW))
        m_new = jnp.maximum(m_old, jnp.max(s_max, axis=0, keepdims=True))
        alpha = jnp.exp2(m_old - m_new)
        pr = jnp.exp2((s - m_new[0:1, :]).astype(BF16))
        acc_ref[:, cols] = acc_ref[:, cols] * alpha[0:1, :] + _dot(pv_lhs, pr)
        m_ref[:, cols] = m_new

    def key_rows(j, nblocks):
        return pl.ds(pl.multiple_of(j * T, T), nblocks * T)

    def slot_rows(nblocks):
        return slice(0, nblocks * T)

    def blocks_t(load, j, nblocks):
        return load(j) if nblocks == 1 else jnp.concatenate([load(j + b) for b in range(nblocks)], axis=1)

    def staged_steps(j0, n_steps, logits, absorb):
        logits(j0, 2, 0)
        for k in range(n_steps):
            if k + 1 < n_steps:
                logits(j0 + 2 * (k + 1), 2, (k + 1) % 2)
            absorb(j0 + 2 * k, 2, k % 2)

    def for_far_blocks(n_far, logits, absorb):
        n_oct = lax.shift_right_logical(n_far, 3)

        def oct_body(jj, c):
            staged_steps(8 * jj, 4, logits, absorb)
            return c
        lax.fori_loop(0, n_oct, oct_body, 0)

        @pl.when((n_far & 4) != 0)
        def _quad():
            staged_steps(8 * n_oct, 2, logits, absorb)

        @pl.when((n_far & 2) != 0)
        def _pair():
            staged_steps(8 * n_oct + (n_far & 4), 1, logits, absorb)

        @pl.when((n_far & 1) != 0)
        def _single():
            logits(n_far - 1, 1, 0)
            absorb(n_far - 1, 1, 0)

    def idx_scores(j, nblocks):
        k_blk = kidx_ref[key_rows(j, nblocks), :]
        acc = [jnp.zeros((nblocks * T, T), F32), jnp.zeros((nblocks * T, T), F32)]
        for tt in range(NT):
            hf, p = divmod(tt, NP)
            s = _dot_nt(k_blk, qim_ref[tt])
            for e in range(2):
                h = 2 * p + e
                w_h = w_rows[h:h + 1, hf * T:(hf + 1) * T]
                acc[hf] = acc[hf] + w_h * jnp.maximum(s[:, e * T:(e + 1) * T], 0.0)
        return jnp.concatenate(acc, axis=1)

    def fox_logits(j, nblocks, slot, causal=False):
        for tt in range(NT):
            hf, p = divmod(tt, NP)
            s = _dot_nt(kaug_ref[p, key_rows(j, nblocks), :], qaug_ref[tt])
            if causal:
                ok = causal_qq[:, hf * T:(hf + 1) * T]
                s = jnp.where(jnp.concatenate([ok, ok], axis=1), s, MASKED)
            s_ref[slot, slot_rows(nblocks), tile_cols(tt)] = s

    def fox_absorb(j, nblocks, slot):
        for tt in range(NT):
            p = tt % NP
            online_update(tt, s_ref[slot, slot_rows(nblocks), tile_cols(tt)],
                          blocks_t(lambda jb: vt_ref[p, jb], j, nblocks))

    def idx_fox_logits(j, nblocks, slot):
        score_ref[pl.ds(j, nblocks)] = idx_scores(j, nblocks).reshape(nblocks, T, Q)
        fox_logits(j, nblocks, slot)

    reset_state()
    for_far_blocks(i0, idx_fox_logits, fox_absorb)
    score_ref[pl.ds(i0, 2)] = jnp.where(causal_qq, idx_scores(i0, 2), -jnp.inf).reshape(2, T, Q)
    fox_logits(i0, 2, 0, causal=True)
    fox_absorb(i0, 2, 0)

    row_lo = lax.broadcasted_iota(jnp.int32, (LANES, T), 0) < HEAD_DIM
    for tt in range(NT):
        hf, p = divmod(tt, NP)
        c0 = slice(tt * TW, tt * TW + T)
        c1 = slice(tt * TW + T, (tt + 1) * TW)
        o_t = jnp.where(row_lo, acc_ref[0:LANES, c0] / acc_ref[LANES:LANES + 1, c0],
                        acc_ref[0:LANES, c1] / acc_ref[LANES:LANES + 1, c1])
        o_ref[hf * T:(hf + 1) * T, MIX_HALF + p * LANES:MIX_HALF + (p + 1) * LANES] = (
            o_t.T.astype(o_ref.dtype))

    kf = float(k_sel)
    t_pos = (g * Q + lax.broadcasted_iota(jnp.int32, (SUBLANES, Q), 1)).astype(F32)
    searching = (t_pos + 1.0) > kf
    G2 = 2 * T // SUBLANES
    n_pair = g + 1
    key_in_pair = (lax.broadcasted_iota(jnp.int32, (G2, SUBLANES, Q), 0) * SUBLANES
                   + lax.broadcasted_iota(jnp.int32, (G2, SUBLANES, Q), 1))

    def rep(v):
        return jnp.broadcast_to(v, (SUBLANES, Q))

    def score_pair(jj):
        return score_ref[pl.ds(2 * jj, 2)].reshape(G2, SUBLANES, Q)

    def count(*pred_fns, load=score_pair):
        def body(jj, cs):
            s = load(jj)
            return tuple(c + _tree(jnp.add, jnp.where(f(s, jj), 1.0, 0.0)) for c, f in zip(cs, pred_fns))
        zero = jnp.zeros((SUBLANES, Q), F32)
        cs = lax.fori_loop(0, n_pair, body, (zero,) * len(pred_fns))
        return tuple(rep(jnp.sum(c, axis=0, keepdims=True)) for c in cs)

    def count_ge(thr):
        return count(lambda s, jj: s >= thr[None])[0]

    def select_threshold():
        def mm_body(jj, carry):
            mn, mx = carry
            s = score_pair(jj)
            return (jnp.minimum(mn, _tree(jnp.minimum, jnp.where(s == -jnp.inf, jnp.inf, s))),
                    jnp.maximum(mx, _tree(jnp.maximum, s)))

        mn, mx = lax.fori_loop(0, n_pair, mm_body,
                               (jnp.full((SUBLANES, Q), jnp.inf, F32),
                                jnp.full((SUBLANES, Q), -jnp.inf, F32)))
        mn = rep(jnp.min(mn, axis=0, keepdims=True))
        mx = rep(jnp.max(mx, axis=0, keepdims=True))
        above_mx = mx + jnp.maximum(jnp.abs(mx) * 2.0 ** -20, 1e-30)
        c_ge0, c_gt0 = count(lambda s, jj: s >= 0.0, lambda s, jj: s > 0.0)
        thr_ge0 = c_ge0 >= kf
        thr_gt0 = c_gt0 >= kf
        lo0 = jnp.where(thr_ge0, 0.0, mn)
        cnt_lo0 = jnp.where(thr_ge0, c_ge0, t_pos + 1.0)
        hi0 = jnp.where(thr_gt0, above_mx, 0.0)
        cnt_hi0 = jnp.where(thr_gt0, 0.0, jnp.where(thr_ge0, c_gt0, c_ge0))
        undecided = thr_gt0 | jnp.logical_not(thr_ge0)
        active0 = jnp.where(searching & undecided & (cnt_lo0 > kf), 1.0, 0.0)

        def cond(st):
            return jnp.logical_and(st[5] > 0.0, st[6] < 2200)

        def bisect(st):
            lo, hi, cnt_lo, cnt_hi, active = st
            mid = 0.5 * lo + 0.5 * hi
            adjacent = (mid <= lo) | (mid >= hi)
            c = count_ge(mid)
            act = active > 0.0
            go_lo = act & jnp.logical_not(adjacent) & (c >= kf)
            go_hi = act & jnp.logical_not(adjacent) & (c < kf)
            lo = jnp.where(go_lo, mid, lo)
            cnt_lo = jnp.where(go_lo, c, cnt_lo)
            hi = jnp.where(go_hi, mid, hi)
            cnt_hi = jnp.where(go_hi, c, cnt_hi)
            active = jnp.where(act & jnp.logical_not(adjacent) & (cnt_lo > kf), 1.0, 0.0)
            return lo, hi, cnt_lo, cnt_hi, active

        def body(st):
            inner = st[:5]
            for _ in range(SEARCH_UNROLL):
                inner = bisect(inner)
            return inner + (jnp.max(inner[4]), st[6] + SEARCH_UNROLL)

        lo, hi, cnt_lo, cnt_hi, _, _, _ = lax.while_loop(
            cond, body, (lo0, hi0, cnt_lo0, cnt_hi0, active0, jnp.max(active0), 0))
        thr = jnp.where(searching, lo, -F32_MAX)

        tied = searching & (cnt_lo > kf)
        need = kf - cnt_hi

        @pl.when(jnp.max(jnp.where(tied, 1.0, 0.0)) > 0.0)
        def _break_ties():
            def tie_pair(jj):
                return tie_ref[pl.ds(2 * jj, 2)].reshape(G2, SUBLANES, Q)

            def mark_body(jj, c):
                key_index = (2 * jj * T + key_in_pair).astype(F32)
                tie_ref[pl.ds(2 * jj, 2)] = (
                    jnp.where(score_pair(jj) == thr[None], key_index, jnp.inf).reshape(2, T, Q))
                return c
            lax.fori_loop(0, n_pair, mark_body, 0)

            def count_tied_upto(jcut):
                return count(lambda e, jj: e <= jcut[None], load=tie_pair)[0]

            def bs_body(_, st):
                jl, jh = st
                jm = jnp.floor(0.5 * (jl + jh))
                ok = count_tied_upto(jm) >= need
                return jnp.where(ok, jl, jm), jnp.where(ok, jm, jh)

            n_keys = (2 * n_pair * T).astype(F32)
            jl0 = jnp.full((SUBLANES, Q), -1.0, F32)
            jh0 = jnp.zeros((SUBLANES, Q), F32) + (n_keys - 1.0)
            n_steps = int(math.ceil(math.log2(nblk * T))) + 1
            _, jcut = lax.fori_loop(0, n_steps, bs_body, (jl0, jh0))

            def drop_body(jj, c):
                e = tie_pair(jj)
                drop = tied[None] & (e > jcut[None]) & (e < jnp.inf)
                score_ref[pl.ds(2 * jj, 2)] = jnp.where(drop, -jnp.inf, score_pair(jj)).reshape(2, T, Q)
                return c
            lax.fori_loop(0, n_pair, drop_body, 0)

        return thr

    thr = lax.cond((i0 + 2) * T > k_sel, select_threshold,
                   lambda: jnp.full((SUBLANES, Q), -F32_MAX, F32))

    def mask_body(jj, c):
        add = jnp.where(score_pair(jj) >= thr[None], 0.0, MASKED).reshape(2 * T, Q).astype(BF16)
        for hf in range(2):
            lhsd_ref[hf, key_rows(2 * jj, 2), KV_RANK:2 * KV_RANK] = add[:, hf * T:(hf + 1) * T]
        return c

    lax.fori_loop(0, n_pair, mask_body, 0)

    def dsa_logits(j, nblocks, slot, bias=None):
        for tt in range(NT):
            hf, p = divmod(tt, NP)
            s = _dot_nt(lhsd_ref[hf, key_rows(j, nblocks), :], qsel_ref[tt])
            if bias is not None:
                s = s + bias(hf, slice(p * TW, (p + 1) * TW))
            s_ref[slot, slot_rows(nblocks), tile_cols(tt)] = s

    def dsa_absorb(j, nblocks, slot):
        pv_lhs = blocks_t(lambda jb: ct_ref[jb], j, nblocks)
        for tt in range(NT):
            online_update(tt, s_ref[slot, slot_rows(nblocks), tile_cols(tt)], pv_lhs)

    def bias_before(hf, cols):
        return bias_ref[1 + hf, :, cols]

    def bias_diag(hf, cols):
        first = bias_ref[hf, :, cols]
        return jnp.concatenate([first, bias_ref[0, :, cols]], axis=0)

    reset_state()
    for_far_blocks(jnp.maximum(i0 - 1, 0), dsa_logits, dsa_absorb)
    m_ref[...] = m_ref[...] + bfar_ref[...]

    @pl.when(g >= 1)
    def _dsa_tail():
        dsa_logits(i0 - 1, 1, 0, bias_before)
        dsa_logits(i0, 2, 1, bias_diag)
        dsa_absorb(i0 - 1, 1, 0)
        dsa_absorb(i0, 2, 1)

    @pl.when(g == 0)
    def _dsa_first():
        dsa_logits(0, 2, 0, bias_diag)
        dsa_absorb(0, 2, 0)

    for tt in range(NT):
        hf, p = divmod(tt, NP)
        o_pair = jnp.zeros((T, LANES), F32)
        for e in range(2):
            cols = slice(tt * TW + e * T, tt * TW + (e + 1) * T)
            o_t = acc_ref[0:KV_RANK, cols] / acc_ref[KV_RANK:KV_RANK + 1, cols]
            o_pair = o_pair + _dot(o_t.T.astype(BF16), wuv_ref[2 * p + e])
        o_ref[hf * T:(hf + 1) * T, p * LANES:(p + 1) * LANES] = o_pair.astype(o_ref.dtype)


_EV_QA, _EV_QI, _EV_QB, _EV_KB, _EV_VB = 0, 1, 2, 3, 4
_EV_CKV, _EV_KIDX = 20, 21


def _attention(hb, w_t, f_rows, wuk_pad, wuv_pad, bias_tiles, bias_far, place, k_sel):
    bsz, seq, _ = hb.shape
    T, Q, H = BLK, QUERIES, N_HEADS
    nblk = seq // T
    lanes = 2 * H * T
    pq, cq, pk, ck = place
    wide = lambda c: pl.BlockSpec((None, Q, MIX_HALF), lambda b, i, c=c: (b, i, c))
    once = dict(pipeline_mode=pl.Buffered(1))
    full_wide = lambda c: pl.BlockSpec((None, seq, MIX_HALF), lambda b, i, c=c: (b, 0, c), **once)
    full_narrow = lambda c: pl.BlockSpec((None, seq, LANES), lambda b, i, c=c: (b, 0, c), **once)
    const = lambda a: pl.BlockSpec(a.shape, lambda b, i, nd=a.ndim: (0,) * nd, **once)
    return pl.pallas_call(
        functools.partial(_attn_kernel, k_sel=k_sel),
        grid=(bsz, seq // Q),
        in_specs=[wide(_EV_QA), wide(_EV_QI), wide(_EV_QB),
                  pl.BlockSpec((None, H, Q), lambda b, i: (b, 0, i)),
                  pl.BlockSpec((None, Q, LANES), lambda b, i: (b, i, 0)),
                  full_wide(_EV_KB), full_wide(_EV_VB),
                  full_narrow(_EV_CKV), full_narrow(_EV_KIDX),
                  pl.BlockSpec((None, seq, LANES), lambda b, i: (b, 0, 0), **once),
                  const(wuk_pad), const(wuv_pad), const(bias_tiles), const(bias_far),
                  const(pq), const(cq), const(pk), const(ck)],
        out_specs=pl.BlockSpec((None, Q, 2 * MIX_HALF), lambda b, i: (b, i, 0)),
        out_shape=jax.ShapeDtypeStruct((bsz, seq, 2 * MIX_HALF), BF16),
        scratch_shapes=[pltpu.VMEM((nblk, T, Q), F32),
                        pltpu.VMEM((2, seq, 2 * KV_RANK), BF16),
                        pltpu.VMEM((nblk, PV_ROWS, T), BF16),
                        pltpu.VMEM((H // 2, seq, 2 * LANES), BF16),
                        pltpu.VMEM((H // 2, nblk, PV_ROWS, T), BF16),
                        pltpu.VMEM((H, 2 * T, 2 * KV_RANK), BF16),
                        pltpu.VMEM((H, 2 * T, 2 * LANES), BF16),
                        pltpu.VMEM((H, 2 * T, LANES), BF16),
                        pltpu.VMEM((SUBLANES, lanes), F32),
                        pltpu.VMEM((PV_ROWS, lanes), F32),
                        pltpu.VMEM((2, 2 * T, lanes), F32),
                        pltpu.VMEM((nblk, T, Q), F32)],
        compiler_params=_cparams(("parallel", "arbitrary")),
        name="dsa_fox_attention",
    )(hb, hb, hb, w_t, f_rows, hb, hb, hb, hb, f_rows, wuk_pad, wuv_pad, bias_tiles, bias_far,
      pq, cq, pk, ck)


def _proj_ln_kernel(a_ref, w_ref, x_ref, g_ref, b_ref, xo_ref, xb_ref, *, alpha, sub):
    for c in range(a_ref.shape[0] // sub):
        rows = slice(c * sub, (c + 1) * sub)
        y = alpha * x_ref[rows, :] + _dot(a_ref[rows, :], w_ref[...])
        out = _layer_norm_rows(y, g_ref[...], b_ref[...])
        xo_ref[rows, :] = out
        xb_ref[rows, :] = out.astype(BF16)


def _proj_ln(a, w, x, g, b, alpha, tm):
    n, k = a.shape
    d = w.shape[1]
    return pl.pallas_call(
        functools.partial(_proj_ln_kernel, alpha=alpha, sub=min(128, tm)),
        grid=(n // tm,),
        in_specs=[pl.BlockSpec((tm, k), lambda i: (i, 0)),
                  pl.BlockSpec((k, d), lambda i: (0, 0)),
                  pl.BlockSpec((tm, d), lambda i: (i, 0)),
                  pl.BlockSpec((1, d), lambda i: (0, 0)),
                  pl.BlockSpec((1, d), lambda i: (0, 0))],
        out_specs=[pl.BlockSpec((tm, d), lambda i: (i, 0)),
                   pl.BlockSpec((tm, d), lambda i: (i, 0))],
        out_shape=[jax.ShapeDtypeStruct((n, d), F32),
                   jax.ShapeDtypeStruct((n, d), BF16)],
        compiler_params=_cparams(("parallel",)),
        name="proj_residual_ln",
    )(a, w, x, g, b)


def _causal_conv3(h, tail, cw):
    w0, w1, w2 = cw[0:1, :], cw[1:2, :], cw[2:3, :]
    body = w0 * pltpu.roll(h, 2, axis=0) + w1 * pltpu.roll(h, 1, axis=0) + w2 * h
    head = h[0:SUBLANES, :]
    r = lax.broadcasted_iota(jnp.int32, head.shape, 0)
    hm1 = jnp.where(r < 1, pltpu.roll(tail, 1, axis=0), pltpu.roll(head, 1, axis=0))
    hm2 = jnp.where(r < 2, pltpu.roll(tail, 2, axis=0), pltpu.roll(head, 2, axis=0))
    out_head = w0 * hm2 + w1 * hm1 + w2 * head
    return jnp.concatenate([out_head, body[SUBLANES:, :]], axis=0)


def _ffn_up_kernel(x_ref, wg_ref, wv_ref, cg_ref, cv_ref, o_ref, tg_ref, tv_ref, *h_refs,
                   tiles_per_seq, sub):
    n_chunks = x_ref.shape[0] // sub
    hg_refs, hv_refs = h_refs[:n_chunks], h_refs[n_chunks:]
    first = (pl.program_id(1) % tiles_per_seq) == 0
    body_rows = slice(SUBLANES, SUBLANES + sub)

    def matmuls(c):
        x = x_ref[c * sub:(c + 1) * sub, :]
        hg_refs[c][body_rows, :] = _dot(x, wg_ref[...])
        hv_refs[c][body_rows, :] = _dot(x, wv_ref[...])

    def conv(h_ref, cw):
        return (cw[0:1, :] * h_ref[SUBLANES - 2:SUBLANES - 2 + sub, :]
                + cw[1:2, :] * h_ref[SUBLANES - 1:SUBLANES - 1 + sub, :]
                + cw[2:3, :] * h_ref[body_rows, :])

    def gate(c):
        if c == 0:
            hg_refs[0][0:SUBLANES, :] = jnp.where(first, 0.0, tg_ref[...])
            hv_refs[0][0:SUBLANES, :] = jnp.where(first, 0.0, tv_ref[...])
        else:
            hg_refs[c][0:SUBLANES, :] = hg_refs[c - 1][sub:sub + SUBLANES, :]
            hv_refs[c][0:SUBLANES, :] = hv_refs[c - 1][sub:sub + SUBLANES, :]
        g = conv(hg_refs[c], cg_ref[...])
        v = conv(hv_refs[c], cv_ref[...])
        o_ref[c * sub:(c + 1) * sub, :] = (g / (1.0 + jnp.exp(-g)) * v).astype(o_ref.dtype)

    matmuls(0)
    for c in range(n_chunks):
        if c + 1 < n_chunks:
            matmuls(c + 1)
        gate(c)
    tg_ref[...] = hg_refs[n_chunks - 1][sub:sub + SUBLANES, :]
    tv_ref[...] = hv_refs[n_chunks - 1][sub:sub + SUBLANES, :]


def _ffn_up(xb, w_up, conv_w, seq, tm, tn, sub):
    n, k = xb.shape
    dff = w_up.shape[1] // 2
    ncol = dff // tn
    return pl.pallas_call(
        functools.partial(_ffn_up_kernel, tiles_per_seq=seq // tm, sub=sub),
        grid=(ncol, n // tm),
        in_specs=[pl.BlockSpec((tm, k), lambda j, i: (i, 0)),
                  pl.BlockSpec((k, tn), lambda j, i: (0, j)),
                  pl.BlockSpec((k, tn), lambda j, i, ncol=ncol: (0, j + ncol)),
                  pl.BlockSpec((CONV_W, tn), lambda j, i: (0, j)),
                  pl.BlockSpec((CONV_W, tn), lambda j, i, ncol=ncol: (0, j + ncol))],
        out_specs=pl.BlockSpec((tm, tn), lambda j, i: (i, j)),
        out_shape=jax.ShapeDtypeStruct((n, dff), BF16),
        scratch_shapes=([pltpu.VMEM((SUBLANES, tn), F32)] * 2
                        + [pltpu.VMEM((sub + SUBLANES, tn), F32)] * (2 * (tm // sub))),
        compiler_params=_cparams(("arbitrary", "arbitrary")),
        name="ffn_up_conv_gate",
    )(xb, w_up, w_up, conv_w, conv_w)


def _gelu_tanh(x):
    return 0.5 * x * (1.0 + jnp.tanh(math.sqrt(2.0 / math.pi) * (x + 0.044715 * (x * x * x))))


def _group_mean(v, avg):
    hi = v.astype(BF16)
    lo = (v - hi.astype(F32)).astype(BF16)
    return _dot(hi, avg) + _dot(lo, avg)


def _odd_mixer_kernel(x_ref, w_ref, sg_ref, sw_ref, sb_ref, cw_ref, o_ref, tail_ref, *, tiles_per_seq):
    tm = x_ref.shape[0]
    W = MIX_HALF
    first = (pl.program_id(0) % tiles_per_seq) == 0
    x = x_ref[...]
    lane = lax.broadcasted_iota(jnp.int32, (CHUNK, LANES), 1)
    lo_half = lane < HEAD_DIM
    r = lax.broadcasted_iota(jnp.int32, (W, W), 0) // HEAD_DIM
    c = lax.broadcasted_iota(jnp.int32, (W, W), 1) // HEAD_DIM
    avg = jnp.where(r == c, 1.0 / HEAD_DIM, 0.0).astype(BF16)
    tri = (lax.broadcasted_iota(jnp.int32, (CHUNK, CHUNK), 1)
           <= lax.broadcasted_iota(jnp.int32, (CHUNK, CHUNK), 0))

    u = _gelu_tanh(_dot(x, w_ref[:, 0:W]))
    v = _gelu_tanh(_dot(x, w_ref[:, W:2 * W]))
    vc = v - _group_mean(v, avg)
    var = _group_mean(vc * vc, avg)
    vn = (vc * lax.rsqrt(var + LN_EPS) * sg_ref[...]).astype(BF16)
    for n in range(tm // CHUNK):
        rows = slice(n * CHUNK, (n + 1) * CHUNK)
        for p in range(N_HEADS // 2):
            cols = slice(p * LANES, (p + 1) * LANES)
            vp = vn[rows, cols]
            w_e = jnp.where(tri, sw_ref[2 * p], 0.0).astype(BF16)
            w_o = jnp.where(tri, sw_ref[2 * p + 1], 0.0).astype(BF16)
            mix = (_dot(w_e, jnp.where(lo_half, vp, 0)) + _dot(w_o, jnp.where(lo_half, 0, vp))
                   + sb_ref[:, cols])
            o_ref[rows, cols] = (u[rows, cols] * mix).astype(o_ref.dtype)

    g_b = _dot(x, w_ref[:, 2 * W:3 * W])
    y = _dot(x, w_ref[:, 3 * W:4 * W]) * _dot(x, w_ref[:, 4 * W:5 * W])
    conv = _causal_conv3(y, jnp.where(first, 0.0, tail_ref[...]), cw_ref[...])
    tail_ref[...] = y[tm - SUBLANES:tm, :]
    o_ref[:, W:2 * W] = (g_b * conv).astype(o_ref.dtype)


def _odd_mixer(xb, w_in, sgu_g, sgu_w, sgu_b_tile, conv_w, seq, tm):
    n, k = xb.shape
    W = MIX_HALF
    return pl.pallas_call(
        functools.partial(_odd_mixer_kernel, tiles_per_seq=seq // tm),
        grid=(n // tm,),
        in_specs=[pl.BlockSpec((tm, k), lambda i: (i, 0)),
                  pl.BlockSpec((k, 5 * W), lambda i: (0, 0)),
                  pl.BlockSpec((1, W), lambda i: (0, 0)),
                  pl.BlockSpec((N_HEADS, CHUNK, CHUNK), lambda i: (0, 0, 0)),
                  pl.BlockSpec((CHUNK, W), lambda i: (0, 0)),
                  pl.BlockSpec((CONV_W, W), lambda i: (0, 0))],
        out_specs=pl.BlockSpec((tm, 2 * W), lambda i: (i, 0)),
        out_shape=jax.ShapeDtypeStruct((n, 2 * W), BF16),
        scratch_shapes=[pltpu.VMEM((SUBLANES, W), F32)],
        compiler_params=_cparams(("arbitrary",)),
        name="odd_mixer",
    )(xb, w_in, sgu_g, sgu_w, sgu_b_tile, conv_w)


def _t5_bucket_np(dist):
    max_exact = N_BUCKETS // 2
    n = np.maximum(dist, 0)
    nf = np.maximum(n, 1).astype(np.float32)
    large = max_exact + (np.log(nf / max_exact) / np.float32(math.log(T5_MAX_DIST / max_exact))
                         * (N_BUCKETS - max_exact)).astype(np.int32)
    large = np.minimum(large, N_BUCKETS - 1)
    return np.where(n < max_exact, n, large)


def _bias_tables(rel_bias):
    T = BLK
    s = np.arange(T)[:, None]
    t = np.arange(T)[None, :]
    bucket = _t5_bucket_np(np.stack([t + d * T - s for d in range(3)]))
    far_bucket = N_BUCKETS - 1
    assert (bucket[2] == far_bucket).all()
    assert (_t5_bucket_np(np.arange(T + 1, 64 * T)) == far_bucket).all()
    rb = rel_bias.astype(F32) * LOG2E
    tiles = sum(jnp.where(jnp.asarray(bucket == b)[..., None], rb[b], 0.0) for b in range(N_BUCKETS))
    tiles = jnp.swapaxes(tiles, 2, 3).reshape(3, T, N_HEADS * T)
    far = jnp.tile(jnp.repeat(rb[far_bucket], T)[None, :], (1, 2))
    return tiles, far


def _placement_constants():
    H, L = N_HEADS, LANES
    pq = np.zeros((3 * L, H * L), np.float32)
    cq = np.zeros((1, H * L), np.float32)
    pk = np.zeros((3 * L, (H // 2) * L), np.float32)
    ck = np.zeros((1, (H // 2) * L), np.float32)
    for h in range(H):
        p, e = divmod(h, 2)
        for piece in range(3):
            pq[piece * L + h, h * L + 6 + piece] = 1.0
            cq[0, h * L + 3 * e + piece] = 1.0
            pk[piece * L + h, p * L + 3 * e + piece] = -1.0
            ck[0, p * L + 6 + piece] = 1.0
    return jnp.asarray(pq, BF16), jnp.asarray(cq), jnp.asarray(pk, BF16), jnp.asarray(ck)


def _even_weights(w_in, w_uk, w_uv):
    hd, H = HEAD_DIM, N_HEADS
    widths = (H * hd, KV_RANK, H * hd, hd, H, H * hd, H * hd, H * hd, H)
    offs = np.cumsum((0,) + widths)
    seg = lambda s: w_in[:, offs[s]:offs[s + 1]]
    q_a, c_kv, q_idx, k_idx, w_idx, q_b, k_b, v_b, f_in = [seg(s) for s in range(9)]
    scale = hd ** -0.5
    w_big = jnp.concatenate([q_a, q_idx * scale, q_b * (scale * LOG2E), k_b, v_b, c_kv,
                             k_idx, k_idx], axis=1).astype(BF16)
    pad = jnp.zeros((w_in.shape[0], LANES - 2 * H), w_in.dtype)
    w_small = jnp.concatenate([w_idx, f_in, pad], axis=1).astype(BF16)
    wuk_t = jnp.swapaxes(w_uk, 1, 2) * (scale * LOG2E)
    wuk_pad = jnp.zeros((H, LANES, KV_RANK), F32)
    wuv_pad = jnp.zeros((H, KV_RANK, LANES), F32)
    for h in range(H):
        o = hd * (h % 2)
        wuk_pad = wuk_pad.at[h, o:o + hd, :].set(wuk_t[h])
        wuv_pad = wuv_pad.at[h, :, o:o + hd].set(w_uv[h])
    return w_big, w_small, wuk_pad.astype(BF16), wuv_pad.astype(BF16)


def kernel(x, ln_g, ln_b, rel_bias, ev_w_in, ev_w_uk, ev_w_uv, ev_b_f, ev_w_o,
           od_w_in, od_sgu_g, od_sgu_w, od_sgu_b, od_conv_w, od_w_o,
           ffn_w_up, ffn_conv_w, ffn_w_down):
    bsz, seq, d = x.shape
    depth = ln_g.shape[0]
    n = bsz * seq
    alpha = (2.0 * depth) ** 0.25
    k_sel = min(TOPK_MAX, seq // 4)
    H = N_HEADS
    assert seq % QUERIES == 0 and seq % CHUNK == 0
    tm = min(512, seq)
    tm_ffn = min(1024, seq)

    bias_tiles, bias_far = _bias_tables(rel_bias)
    place = _placement_constants()
    xf = x.reshape(n, d)
    xb = xf.astype(BF16)
    for layer in range(depth):
        j = layer // 2
        g0, b0 = ln_g[layer, 0][None, :], ln_b[layer, 0][None, :]
        g1, b1 = ln_g[layer, 1][None, :], ln_b[layer, 1][None, :]
        if layer % 2 == 0:
            w_big, w_small, wuk_pad, wuv_pad = _even_weights(ev_w_in[j], ev_w_uk[j], ev_w_uv[j])
            hb, hs = _even_proj(xb, w_big, w_small, tm)
            hb = hb.reshape(bsz, seq, -1)
            hs_t = jnp.swapaxes(hs.reshape(bsz, seq, -1)[:, :, 0:2 * H], 1, 2)
            cum_f = _cumf(hs_t[:, H:2 * H], ev_b_f[j][:, None])
            f_rows = jnp.pad(jnp.swapaxes(cum_f, 1, 2), ((0, 0), (0, 0), (0, LANES - H)))
            mix = _attention(hb, hs_t[:, 0:H], f_rows, wuk_pad, wuv_pad, bias_tiles, bias_far,
                             place, k_sel)
            w_o = ev_w_o[j]
        else:
            sgu_b_tile = jnp.repeat(jnp.swapaxes(od_sgu_b[j], 0, 1), HEAD_DIM, axis=1)
            mix = _odd_mixer(xb, od_w_in[j].astype(BF16), od_sgu_g[j].reshape(1, -1), od_sgu_w[j],
                             sgu_b_tile, od_conv_w[j], seq, min(256, seq))
            w_o = od_w_o[j]
        xf, xb = _proj_ln(mix.reshape(n, d), w_o.astype(BF16), xf, g0, b0, alpha, tm)
        act = _ffn_up(xb, ffn_w_up[layer].astype(BF16), ffn_conv_w[layer], seq, tm_ffn, 256,
                      min(512, seq))
        xf, xb = _proj_ln(act, ffn_w_down[layer].astype(BF16), xf, g1, b1, alpha, tm)
    return xf.reshape(bsz, seq, d)
```

```python
import functools
import math

import numpy as np
import jax
import jax.numpy as jnp
from jax import lax
from jax.experimental import pallas as pl
from jax.experimental.pallas import tpu as pltpu

F32 = jnp.float32
BF16 = jnp.bfloat16

HEAD_DIM = 64
N_HEADS = 8
KV_RANK = 128
TOPK_MAX = 256
N_BUCKETS = 32
T5_MAX_DIST = 128
CHUNK = 128
CONV_W = 3
LN_EPS = 1e-5
MIX_HALF = N_HEADS * HEAD_DIM

LANES = 128
SUBLANES = 8
BF16_ROWS = 16
VMEM_LIMIT = 56 * 1024 * 1024

BLK = 128
QUERIES = 2 * BLK
PV_ROWS = KV_RANK + BF16_ROWS
NEG_BIG = -1e30
MASKED = -(2.0 ** 100)
F32_MAX = float(np.finfo(np.float32).max)
LOG2E = math.log2(math.e)
SEARCH_UNROLL = 4


def _cparams(sem):
    return pltpu.CompilerParams(dimension_semantics=sem, vmem_limit_bytes=VMEM_LIMIT)


def _dot(a, b):
    return jnp.dot(a, b, preferred_element_type=F32)


def _dot_nt(a, b):
    return lax.dot_general(a, b, (((1,), (1,)), ((), ())), preferred_element_type=F32)


def _layer_norm_rows(y, g, b):
    mu = jnp.mean(y, axis=-1, keepdims=True)
    yc = y - mu
    var = jnp.mean(yc * yc, axis=-1, keepdims=True)
    return yc * lax.rsqrt(var + LN_EPS) * g + b


def _tree(op, x):
    while x.shape[0] > 1:
        half = x.shape[0] // 2
        x = op(x[:half], x[half:])
    return x[0]


def _bf16_pieces(v):
    hi = v.astype(BF16)
    r1 = v - hi.astype(F32)
    mid = r1.astype(BF16)
    lo = (r1 - mid.astype(F32)).astype(BF16)
    return jnp.concatenate([hi, mid, lo], axis=1)


def _even_proj_kernel(x_ref, wb_ref, ws_ref, hb_ref, hs_ref):
    x = x_ref[...].astype(BF16)
    hb_ref[...] = _dot(x, wb_ref[...]).astype(hb_ref.dtype)
    hs_ref[...] = _dot(x, ws_ref[...])


def _even_proj(xb, w_big, w_small, tm):
    n, k = xb.shape
    mb = w_big.shape[1]
    ms = w_small.shape[1]
    return pl.pallas_call(
        _even_proj_kernel,
        grid=(n // tm,),
        in_specs=[pl.BlockSpec((tm, k), lambda i: (i, 0)),
                  pl.BlockSpec((k, mb), lambda i: (0, 0)),
                  pl.BlockSpec((k, ms), lambda i: (0, 0))],
        out_specs=[pl.BlockSpec((tm, mb), lambda i: (i, 0)),
                   pl.BlockSpec((tm, ms), lambda i: (i, 0))],
        out_shape=[jax.ShapeDtypeStruct((n, mb), BF16),
                   jax.ShapeDtypeStruct((n, ms), F32)],
        compiler_params=_cparams(("parallel",)),
        name="even_proj",
    )(xb, w_big, w_small)


def _cumf_kernel(f_ref, b_ref, o_ref):
    seq = f_ref.shape[-1]
    row = lax.broadcasted_iota(jnp.int32, (LANES, LANES), 0)
    col = lax.broadcasted_iota(jnp.int32, (LANES, LANES), 1)
    upper = jnp.where(row <= col, 1.0, 0.0).astype(F32)
    carry = jnp.zeros((f_ref.shape[0], 1), F32)
    for c in range(seq // LANES):
        z = f_ref[:, c * LANES:(c + 1) * LANES] + b_ref[...]
        lf = jnp.minimum(z, 0.0) - jnp.log1p(jnp.exp(-jnp.abs(z)))
        cs = jnp.dot(lf, upper, preferred_element_type=F32,
                     precision=lax.Precision.HIGHEST) + carry
        o_ref[:, c * LANES:(c + 1) * LANES] = cs
        carry = cs[:, LANES - 1:LANES]


def _cumf(f_t, b_f):
    bsz, nh, seq = f_t.shape
    return pl.pallas_call(
        _cumf_kernel,
        grid=(bsz,),
        in_specs=[pl.BlockSpec((None, nh, seq), lambda b: (b, 0, 0)),
                  pl.BlockSpec((nh, 1), lambda b: (0, 0))],
        out_specs=pl.BlockSpec((None, nh, seq), lambda b: (b, 0, 0)),
        out_shape=jax.ShapeDtypeStruct((bsz, nh, seq), F32),
        compiler_params=_cparams(("parallel",)),
        name="fox_cumf",
    )(f_t, b_f)


def _attn_kernel(qa_ref, qi_ref, qb_ref, wt_ref, fq_ref, kb_ref, vb_ref, ckv_ref, kidx_ref, fk_ref,
                 wuk_ref, wuv_ref, bias_ref, bfar_ref, pq_ref, cq_ref, pk_ref, ck_ref,
                 o_ref,
                 score_ref, lhsd_ref, ct_ref, kaug_ref, vt_ref, qsel_ref, qaug_ref, qim_ref,
                 m_ref, acc_ref, s_ref, tie_ref, *, k_sel):
    T = BLK
    Q = QUERIES
    H = N_HEADS
    NP = H // 2
    NT = 2 * NP
    TW = 2 * T
    g = pl.program_id(1)
    i0 = 2 * g
    nblk = ct_ref.shape[0]
    lane = lax.broadcasted_iota(jnp.int32, (T, LANES), 1)
    lo_half = lane < HEAD_DIM
    ones_rows = jnp.ones((BF16_ROWS, T), BF16)
    causal_qq = (lax.broadcasted_iota(jnp.int32, (Q, Q), 0) <= lax.broadcasted_iota(jnp.int32, (Q, Q), 1))

    @pl.when(g == 0)
    def _per_batch():
        def prep(j, c):
            r = pl.ds(pl.multiple_of(j * T, T), T)
            cb = ckv_ref[r, :]
            lhsd_ref[0, r, 0:KV_RANK] = cb
            lhsd_ref[1, r, 0:KV_RANK] = cb
            ct_ref[j, 0:KV_RANK, :] = cb.astype(F32).T.astype(BF16)
            ct_ref[j, KV_RANK:PV_ROWS, :] = ones_rows
            ext = (_dot(_bf16_pieces(fk_ref[r, :] * LOG2E), pk_ref[...]) + ck_ref[...]).astype(BF16)
            for p in range(NP):
                cols = slice(p * LANES, (p + 1) * LANES)
                kaug_ref[p, r, 0:LANES] = kb_ref[r, cols]
                kaug_ref[p, r, LANES:2 * LANES] = ext[:, cols]
                vt_ref[p, j, 0:LANES, :] = vb_ref[r, cols].astype(F32).T.astype(BF16)
                vt_ref[p, j, LANES:PV_ROWS, :] = ones_rows
            return c
        lax.fori_loop(0, nblk, prep, 0)

    w_rows = wt_ref[...] * (H ** -0.5)
    ext_q = (_dot(_bf16_pieces(fq_ref[...] * LOG2E), pq_ref[...]) + cq_ref[...]).astype(BF16)
    eye = jnp.where(lax.broadcasted_iota(jnp.int32, (T, T), 0) == lax.broadcasted_iota(jnp.int32, (T, T), 1),
                    1.0, 0.0).astype(BF16)
    for tt in range(NT):
        hf, p = divmod(tt, NP)
        qrows = slice(hf * T, (hf + 1) * T)
        cols = slice(p * LANES, (p + 1) * LANES)
        qa_pair, qi_pair, qb_pair = qa_ref[qrows, cols], qi_ref[qrows, cols], qb_ref[qrows, cols]
        for e in range(2):
            h = 2 * p + e
            rows = slice(e * T, (e + 1) * T)
            mine = lo_half if e == 0 else jnp.logical_not(lo_half)
            qsel_ref[tt, rows, 0:KV_RANK] = _dot(qa_pair, wuk_ref[h]).astype(BF16)
            qsel_ref[tt, rows, KV_RANK:2 * KV_RANK] = eye
            qaug_ref[tt, rows, 0:LANES] = jnp.where(mine, qb_pair, 0)
            qaug_ref[tt, rows, LANES:2 * LANES] = ext_q[qrows, h * LANES:(h + 1) * LANES]
            qim_ref[tt, rows, :] = jnp.where(mine, qi_pair, 0)

    def reset_state():
        m_ref[...] = jnp.full(m_ref.shape, NEG_BIG, F32)
        acc_ref[...] = jnp.zeros(acc_ref.shape, F32)

    def tile_cols(tt):
        return slice(tt * TW, (tt + 1) * TW)

    def online_update(tt, s, pv_lhs):
        cols = tile_cols(tt)
        m_old = m_ref[:, cols]
        s_max = _tree(jnp.maximum, s.reshape(-1, SUBLANES, TW))
        m_new = jnp.maximum(m_old, jnp.max(s_max, axis=0, keepdims=True))
        alpha = jnp.exp2(m_old - m_new)
        pr = jnp.exp2((s - m_new[0:1, :]).astype(BF16))
        acc_ref[:, cols] = acc_ref[:, cols] * alpha[0:1, :] + _dot(pv_lhs, pr)
        m_ref[:, cols] = m_new

    def key_rows(j, nblocks):
        return pl.ds(pl.multiple_of(j * T, T), nblocks * T)

    def slot_rows(nblocks):
        return slice(0, nblocks * T)

    def blocks_t(load, j, nblocks):
        return load(j) if nblocks == 1 else jnp.concatenate([load(j + b) for b in range(nblocks)], axis=1)

    def staged_steps(j0, n_steps, logits, absorb):
        logits(j0, 2, 0)
        for k in range(n_steps):
            if k + 1 < n_steps:
                logits(j0 + 2 * (k + 1), 2, (k + 1) % 2)
            absorb(j0 + 2 * k, 2, k % 2)

    def for_far_blocks(n_far, logits, absorb):
        n_oct = lax.shift_right_logical(n_far, 3)

        def oct_body(jj, c):
            staged_steps(8 * jj, 4, logits, absorb)
            return c
        lax.fori_loop(0, n_oct, oct_body, 0)

        @pl.when((n_far & 4) != 0)
        def _quad():
            staged_steps(8 * n_oct, 2, logits, absorb)

        @pl.when((n_far & 2) != 0)
        def _pair():
            staged_steps(8 * n_oct + (n_far & 4), 1, logits, absorb)

        @pl.when((n_far & 1) != 0)
        def _single():
            logits(n_far - 1, 1, 0)
            absorb(n_far - 1, 1, 0)

    def idx_scores(j, nblocks):
        k_blk = kidx_ref[key_rows(j, nblocks), :]
        acc = [jnp.zeros((nblocks * T, T), F32), jnp.zeros((nblocks * T, T), F32)]
        for tt in range(NT):
            hf, p = divmod(tt, NP)
            s = _dot_nt(k_blk, qim_ref[tt])
            for e in range(2):
                h = 2 * p + e
                w_h = w_rows[h:h + 1, hf * T:(hf + 1) * T]
                acc[hf] = acc[hf] + w_h * jnp.maximum(s[:, e * T:(e + 1) * T], 0.0)
        return jnp.concatenate(acc, axis=1)

    def fox_logits(j, nblocks, slot, causal=False):
        for tt in range(NT):
            hf, p = divmod(tt, NP)
            s = _dot_nt(kaug_ref[p, key_rows(j, nblocks), :], qaug_ref[tt])
            if causal:
                ok = causal_qq[:, hf * T:(hf + 1) * T]
                s = jnp.where(jnp.concatenate([ok, ok], axis=1), s, MASKED)
            s_ref[slot, slot_rows(nblocks), tile_cols(tt)] = s

    def fox_absorb(j, nblocks, slot):
        for tt in range(NT):
            p = tt % NP
            online_update(tt, s_ref[slot, slot_rows(nblocks), tile_cols(tt)],
                          blocks_t(lambda jb: vt_ref[p, jb], j, nblocks))

    def idx_fox_logits(j, nblocks, slot):
        score_ref[pl.ds(j, nblocks)] = idx_scores(j, nblocks).reshape(nblocks, T, Q)
        fox_logits(j, nblocks, slot)

    reset_state()
    for_far_blocks(i0, idx_fox_logits, fox_absorb)
    score_ref[pl.ds(i0, 2)] = jnp.where(causal_qq, idx_scores(i0, 2), -jnp.inf).reshape(2, T, Q)
    fox_logits(i0, 2, 0, causal=True)
    fox_absorb(i0, 2, 0)

    row_lo = lax.broadcasted_iota(jnp.int32, (LANES, T), 0) < HEAD_DIM
    for tt in range(NT):
        hf, p = divmod(tt, NP)
        c0 = slice(tt * TW, tt * TW + T)
        c1 = slice(tt * TW + T, (tt + 1) * TW)
        o_t = jnp.where(row_lo, acc_ref[0:LANES, c0] / acc_ref[LANES:LANES + 1, c0],
                        acc_ref[0:LANES, c1] / acc_ref[LANES:LANES + 1, c1])
        o_ref[hf * T:(hf + 1) * T, MIX_HALF + p * LANES:MIX_HALF + (p + 1) * LANES] = (
            o_t.T.astype(o_ref.dtype))

    kf = float(k_sel)
    t_pos = (g * Q + lax.broadcasted_iota(jnp.int32, (SUBLANES, Q), 1)).astype(F32)
    searching = (t_pos + 1.0) > kf
    G2 = 2 * T // SUBLANES
    n_pair = g + 1
    key_in_pair = (lax.broadcasted_iota(jnp.int32, (G2, SUBLANES, Q), 0) * SUBLANES
                   + lax.broadcasted_iota(jnp.int32, (G2, SUBLANES, Q), 1))

    def rep(v):
        return jnp.broadcast_to(v, (SUBLANES, Q))

    def score_pair(jj):
        return score_ref[pl.ds(2 * jj, 2)].reshape(G2, SUBLANES, Q)

    def count(*pred_fns, load=score_pair):
        def body(jj, cs):
            s = load(jj)
            return tuple(c + _tree(jnp.add, jnp.where(f(s, jj), 1.0, 0.0)) for c, f in zip(cs, pred_fns))
        zero = jnp.zeros((SUBLANES, Q), F32)
        cs = lax.fori_loop(0, n_pair, body, (zero,) * len(pred_fns))
        return tuple(rep(jnp.sum(c, axis=0, keepdims=True)) for c in cs)

    def count_ge(thr):
        return count(lambda s, jj: s >= thr[None])[0]

    def select_threshold():
        def mm_body(jj, carry):
            mn, mx = carry
            s = score_pair(jj)
            return (jnp.minimum(mn, _tree(jnp.minimum, jnp.where(s == -jnp.inf, jnp.inf, s))),
                    jnp.maximum(mx, _tree(jnp.maximum, s)))

        mn, mx = lax.fori_loop(0, n_pair, mm_body,
                               (jnp.full((SUBLANES, Q), jnp.inf, F32),
                                jnp.full((SUBLANES, Q), -jnp.inf, F32)))
        mn = rep(jnp.min(mn, axis=0, keepdims=True))
        mx = rep(jnp.max(mx, axis=0, keepdims=True))
        above_mx = mx + jnp.maximum(jnp.abs(mx) * 2.0 ** -20, 1e-30)
        c_ge0, c_gt0 = count(lambda s, jj: s >= 0.0, lambda s, jj: s > 0.0)
        thr_ge0 = c_ge0 >= kf
        thr_gt0 = c_gt0 >= kf
        lo0 = jnp.where(thr_ge0, 0.0, mn)
        cnt_lo0 = jnp.where(thr_ge0, c_ge0, t_pos + 1.0)
        hi0 = jnp.where(thr_gt0, above_mx, 0.0)
        cnt_hi0 = jnp.where(thr_gt0, 0.0, jnp.where(thr_ge0, c_gt0, c_ge0))
        undecided = thr_gt0 | jnp.logical_not(thr_ge0)
        active0 = jnp.where(searching & undecided & (cnt_lo0 > kf), 1.0, 0.0)

        def cond(st):
            return jnp.logical_and(st[5] > 0.0, st[6] < 2200)

        def bisect(st):
            lo, hi, cnt_lo, cnt_hi, active = st
            mid = 0.5 * lo + 0.5 * hi
            adjacent = (mid <= lo) | (mid >= hi)
            c = count_ge(mid)
            act = active > 0.0
            go_lo = act & jnp.logical_not(adjacent) & (c >= kf)
            go_hi = act & jnp.logical_not(adjacent) & (c < kf)
            lo = jnp.where(go_lo, mid, lo)
            cnt_lo = jnp.where(go_lo, c, cnt_lo)
            hi = jnp.where(go_hi, mid, hi)
            cnt_hi = jnp.where(go_hi, c, cnt_hi)
            active = jnp.where(act & jnp.logical_not(adjacent) & (cnt_lo > kf), 1.0, 0.0)
            return lo, hi, cnt_lo, cnt_hi, active

        def body(st):
            inner = st[:5]
            for _ in range(SEARCH_UNROLL):
                inner = bisect(inner)
            return inner + (jnp.max(inner[4]), st[6] + SEARCH_UNROLL)

        lo, hi, cnt_lo, cnt_hi, _, _, _ = lax.while_loop(
            cond, body, (lo0, hi0, cnt_lo0, cnt_hi0, active0, jnp.max(active0), 0))
        thr = jnp.where(searching, lo, -F32_MAX)

        tied = searching & (cnt_lo > kf)
        need = kf - cnt_hi

        @pl.when(jnp.max(jnp.where(tied, 1.0, 0.0)) > 0.0)
        def _break_ties():
            def tie_pair(jj):
                return tie_ref[pl.ds(2 * jj, 2)].reshape(G2, SUBLANES, Q)

            def mark_body(jj, c):
                key_index = (2 * jj * T + key_in_pair).astype(F32)
                tie_ref[pl.ds(2 * jj, 2)] = (
                    jnp.where(score_pair(jj) == thr[None], key_index, jnp.inf).reshape(2, T, Q))
                return c
            lax.fori_loop(0, n_pair, mark_body, 0)

            def count_tied_upto(jcut):
                return count(lambda e, jj: e <= jcut[None], load=tie_pair)[0]

            def bs_body(_, st):
                jl, jh = st
                jm = jnp.floor(0.5 * (jl + jh))
                ok = count_tied_upto(jm) >= need
                return jnp.where(ok, jl, jm), jnp.where(ok, jm, jh)

            n_keys = (2 * n_pair * T).astype(F32)
            jl0 = jnp.full((SUBLANES, Q), -1.0, F32)
            jh0 = jnp.zeros((SUBLANES, Q), F32) + (n_keys - 1.0)
            n_steps = int(math.ceil(math.log2(nblk * T))) + 1
            _, jcut = lax.fori_loop(0, n_steps, bs_body, (jl0, jh0))

            def drop_body(jj, c):
                e = tie_pair(jj)
                drop = tied[None] & (e > jcut[None]) & (e < jnp.inf)
                score_ref[pl.ds(2 * jj, 2)] = jnp.where(drop, -jnp.inf, score_pair(jj)).reshape(2, T, Q)
                return c
            lax.fori_loop(0, n_pair, drop_body, 0)

        return thr

    thr = lax.cond((i0 + 2) * T > k_sel, select_threshold,
                   lambda: jnp.full((SUBLANES, Q), -F32_MAX, F32))

    def mask_body(jj, c):
        add = jnp.where(score_pair(jj) >= thr[None], 0.0, MASKED).reshape(2 * T, Q).astype(BF16)
        for hf in range(2):
            lhsd_ref[hf, key_rows(2 * jj, 2), KV_RANK:2 * KV_RANK] = add[:, hf * T:(hf + 1) * T]
        return c

    lax.fori_loop(0, n_pair, mask_body, 0)

    def dsa_logits(j, nblocks, slot, bias=None):
        for tt in range(NT):
            hf, p = divmod(tt, NP)
            s = _dot_nt(lhsd_ref[hf, key_rows(j, nblocks), :], qsel_ref[tt])
            if bias is not None:
                s = s + bias(hf, slice(p * TW, (p + 1) * TW))
            s_ref[slot, slot_rows(nblocks), tile_cols(tt)] = s

    def dsa_absorb(j, nblocks, slot):
        pv_lhs = blocks_t(lambda jb: ct_ref[jb], j, nblocks)
        for tt in range(NT):
            online_update(tt, s_ref[slot, slot_rows(nblocks), tile_cols(tt)], pv_lhs)

    def bias_before(hf, cols):
        return bias_ref[1 + hf, :, cols]

    def bias_diag(hf, cols):
        first = bias_ref[hf, :, cols]
        return jnp.concatenate([first, bias_ref[0, :, cols]], axis=0)

    reset_state()
    for_far_blocks(jnp.maximum(i0 - 1, 0), dsa_logits, dsa_absorb)
    m_ref[...] = m_ref[...] + bfar_ref[...]

    @pl.when(g >= 1)
    def _dsa_tail():
        dsa_logits(i0 - 1, 1, 0, bias_before)
        dsa_logits(i0, 2, 1, bias_diag)
        dsa_absorb(i0 - 1, 1, 0)
        dsa_absorb(i0, 2, 1)

    @pl.when(g == 0)
    def _dsa_first():
        dsa_logits(0, 2, 0, bias_diag)
        dsa_absorb(0, 2, 0)

    for tt in range(NT):
        hf, p = divmod(tt, NP)
        o_pair = jnp.zeros((T, LANES), F32)
        for e in range(2):
            cols = slice(tt * TW + e * T, tt * TW + (e + 1) * T)
            o_t = acc_ref[0:KV_RANK, cols] / acc_ref[KV_RANK:KV_RANK + 1, cols]
            o_pair = o_pair + _dot(o_t.T.astype(BF16), wuv_ref[2 * p + e])
        o_ref[hf * T:(hf + 1) * T, p * LANES:(p + 1) * LANES] = o_pair.astype(o_ref.dtype)


_EV_QA, _EV_QI, _EV_QB, _EV_KB, _EV_VB = 0, 1, 2, 3, 4
_EV_CKV, _EV_KIDX = 20, 21


def _attention(hb, w_t, f_rows, wuk_pad, wuv_pad, bias_tiles, bias_far, place, k_sel):
    bsz, seq, _ = hb.shape
    T, Q, H = BLK, QUERIES, N_HEADS
    nblk = seq // T
    lanes = 2 * H * T
    pq, cq, pk, ck = place
    wide = lambda c: pl.BlockSpec((None, Q, MIX_HALF), lambda b, i, c=c: (b, i, c))
    once = dict(pipeline_mode=pl.Buffered(1))
    full_wide = lambda c: pl.BlockSpec((None, seq, MIX_HALF), lambda b, i, c=c: (b, 0, c), **once)
    full_narrow = lambda c: pl.BlockSpec((None, seq, LANES), lambda b, i, c=c: (b, 0, c), **once)
    const = lambda a: pl.BlockSpec(a.shape, lambda b, i, nd=a.ndim: (0,) * nd, **once)
    return pl.pallas_call(
        functools.partial(_attn_kernel, k_sel=k_sel),
        grid=(bsz, seq // Q),
        in_specs=[wide(_EV_QA), wide(_EV_QI), wide(_EV_QB),
                  pl.BlockSpec((None, H, Q), lambda b, i: (b, 0, i)),
                  pl.BlockSpec((None, Q, LANES), lambda b, i: (b, i, 0)),
                  full_wide(_EV_KB), full_wide(_EV_VB),
                  full_narrow(_EV_CKV), full_narrow(_EV_KIDX),
                  pl.BlockSpec((None, seq, LANES), lambda b, i: (b, 0, 0), **once),
                  const(wuk_pad), const(wuv_pad), const(bias_tiles), const(bias_far),
                  const(pq), const(cq), const(pk), const(ck)],
        out_specs=pl.BlockSpec((None, Q, 2 * MIX_HALF), lambda b, i: (b, i, 0)),
        out_shape=jax.ShapeDtypeStruct((bsz, seq, 2 * MIX_HALF), BF16),
        scratch_shapes=[pltpu.VMEM((nblk, T, Q), F32),
                        pltpu.VMEM((2, seq, 2 * KV_RANK), BF16),
                        pltpu.VMEM((nblk, PV_ROWS, T), BF16),
                        pltpu.VMEM((H // 2, seq, 2 * LANES), BF16),
                        pltpu.VMEM((H // 2, nblk, PV_ROWS, T), BF16),
                        pltpu.VMEM((H, 2 * T, 2 * KV_RANK), BF16),
                        pltpu.VMEM((H, 2 * T, 2 * LANES), BF16),
                        pltpu.VMEM((H, 2 * T, LANES), BF16),
                        pltpu.VMEM((SUBLANES, lanes), F32),
                        pltpu.VMEM((PV_ROWS, lanes), F32),
                        pltpu.VMEM((2, 2 * T, lanes), F32),
                        pltpu.VMEM((nblk, T, Q), F32)],
        compiler_params=_cparams(("parallel", "arbitrary")),
        name="dsa_fox_attention",
    )(hb, hb, hb, w_t, f_rows, hb, hb, hb, hb, f_rows, wuk_pad, wuv_pad, bias_tiles, bias_far,
      pq, cq, pk, ck)


def _proj_ln_kernel(a_ref, w_ref, x_ref, g_ref, b_ref, xo_ref, xb_ref, *, alpha, sub):
    for c in range(a_ref.shape[0] // sub):
        rows = slice(c * sub, (c + 1) * sub)
        y = alpha * x_ref[rows, :] + _dot(a_ref[rows, :], w_ref[...])
        out = _layer_norm_rows(y, g_ref[...], b_ref[...])
        xo_ref[rows, :] = out
        xb_ref[rows, :] = out.astype(BF16)


def _proj_ln(a, w, x, g, b, alpha, tm):
    n, k = a.shape
    d = w.shape[1]
    return pl.pallas_call(
        functools.partial(_proj_ln_kernel, alpha=alpha, sub=min(128, tm)),
        grid=(n // tm,),
        in_specs=[pl.BlockSpec((tm, k), lambda i: (i, 0)),
                  pl.BlockSpec((k, d), lambda i: (0, 0)),
                  pl.BlockSpec((tm, d), lambda i: (i, 0)),
                  pl.BlockSpec((1, d), lambda i: (0, 0)),
                  pl.BlockSpec((1, d), lambda i: (0, 0))],
        out_specs=[pl.BlockSpec((tm, d), lambda i: (i, 0)),
                   pl.BlockSpec((tm, d), lambda i: (i, 0))],
        out_shape=[jax.ShapeDtypeStruct((n, d), F32),
                   jax.ShapeDtypeStruct((n, d), BF16)],
        compiler_params=_cparams(("parallel",)),
        name="proj_residual_ln",
    )(a, w, x, g, b)


def _causal_conv3(h, tail, cw):
    w0, w1, w2 = cw[0:1, :], cw[1:2, :], cw[2:3, :]
    body = w0 * pltpu.roll(h, 2, axis=0) + w1 * pltpu.roll(h, 1, axis=0) + w2 * h
    head = h[0:SUBLANES, :]
    r = lax.broadcasted_iota(jnp.int32, head.shape, 0)
    hm1 = jnp.where(r < 1, pltpu.roll(tail, 1, axis=0), pltpu.roll(head, 1, axis=0))
    hm2 = jnp.where(r < 2, pltpu.roll(tail, 2, axis=0), pltpu.roll(head, 2, axis=0))
    out_head = w0 * hm2 + w1 * hm1 + w2 * head
    return jnp.concatenate([out_head, body[SUBLANES:, :]], axis=0)


def _ffn_up_kernel(x_ref, wg_ref, wv_ref, cg_ref, cv_ref, o_ref, tg_ref, tv_ref, *h_refs,
                   tiles_per_seq, sub):
    n_chunks = x_ref.shape[0] // sub
    hg_refs, hv_refs = h_refs[:n_chunks], h_refs[n_chunks:]
    first = (pl.program_id(1) % tiles_per_seq) == 0
    body_rows = slice(SUBLANES, SUBLANES + sub)

    def matmuls(c):
        x = x_ref[c * sub:(c + 1) * sub, :]
        hg_refs[c][body_rows, :] = _dot(x, wg_ref[...])
        hv_refs[c][body_rows, :] = _dot(x, wv_ref[...])

    def conv(h_ref, cw):
        return (cw[0:1, :] * h_ref[SUBLANES - 2:SUBLANES - 2 + sub, :]
                + cw[1:2, :] * h_ref[SUBLANES - 1:SUBLANES - 1 + sub, :]
                + cw[2:3, :] * h_ref[body_rows, :])

    def gate(c):
        if c == 0:
            hg_refs[0][0:SUBLANES, :] = jnp.where(first, 0.0, tg_ref[...])
            hv_refs[0][0:SUBLANES, :] = jnp.where(first, 0.0, tv_ref[...])
        else:
            hg_refs[c][0:SUBLANES, :] = hg_refs[c - 1][sub:sub + SUBLANES, :]
            hv_refs[c][0:SUBLANES, :] = hv_refs[c - 1][sub:sub + SUBLANES, :]
        g = conv(hg_refs[c], cg_ref[...])
        v = conv(hv_refs[c], cv_ref[...])
        half_g = 0.5 * g
        o_ref[c * sub:(c + 1) * sub, :] = ((half_g + half_g * jnp.tanh(half_g)) * v).astype(o_ref.dtype)

    matmuls(0)
    for c in range(n_chunks):
        if c + 1 < n_chunks:
            matmuls(c + 1)
        gate(c)
    tg_ref[...] = hg_refs[n_chunks - 1][sub:sub + SUBLANES, :]
    tv_ref[...] = hv_refs[n_chunks - 1][sub:sub + SUBLANES, :]


def _ffn_up(xb, w_up, conv_w, seq, tm, tn, sub):
    n, k = xb.shape
    dff = w_up.shape[1] // 2
    ncol = dff // tn
    return pl.pallas_call(
        functools.partial(_ffn_up_kernel, tiles_per_seq=seq // tm, sub=sub),
        grid=(ncol, n // tm),
        in_specs=[pl.BlockSpec((tm, k), lambda j, i: (i, 0)),
                  pl.BlockSpec((k, tn), lambda j, i: (0, j)),
                  pl.BlockSpec((k, tn), lambda j, i, ncol=ncol: (0, j + ncol)),
                  pl.BlockSpec((CONV_W, tn), lambda j, i: (0, j)),
                  pl.BlockSpec((CONV_W, tn), lambda j, i, ncol=ncol: (0, j + ncol))],
        out_specs=pl.BlockSpec((tm, tn), lambda j, i: (i, j)),
        out_shape=jax.ShapeDtypeStruct((n, dff), BF16),
        scratch_shapes=([pltpu.VMEM((SUBLANES, tn), F32)] * 2
                        + [pltpu.VMEM((sub + SUBLANES, tn), F32)] * (2 * (tm // sub))),
        compiler_params=_cparams(("arbitrary", "arbitrary")),
        name="ffn_up_conv_gate",
    )(xb, w_up, w_up, conv_w, conv_w)


def _gelu_tanh(x):
    return 0.5 * x * (1.0 + jnp.tanh(math.sqrt(2.0 / math.pi) * (x + 0.044715 * (x * x * x))))


def _group_mean(v, avg):
    hi = v.astype(BF16)
    lo = (v - hi.astype(F32)).astype(BF16)
    return _dot(hi, avg) + _dot(lo, avg)


def _odd_mixer_kernel(x_ref, w_ref, sg_ref, sw_ref, sb_ref, cw_ref, o_ref, tail_ref, *, tiles_per_seq):
    tm = x_ref.shape[0]
    W = MIX_HALF
    first = (pl.program_id(0) % tiles_per_seq) == 0
    x = x_ref[...]
    lane = lax.broadcasted_iota(jnp.int32, (CHUNK, LANES), 1)
    lo_half = lane < HEAD_DIM
    r = lax.broadcasted_iota(jnp.int32, (W, W), 0) // HEAD_DIM
    c = lax.broadcasted_iota(jnp.int32, (W, W), 1) // HEAD_DIM
    avg = jnp.where(r == c, 1.0 / HEAD_DIM, 0.0).astype(BF16)
    tri = (lax.broadcasted_iota(jnp.int32, (CHUNK, CHUNK), 1)
           <= lax.broadcasted_iota(jnp.int32, (CHUNK, CHUNK), 0))

    u = _gelu_tanh(_dot(x, w_ref[:, 0:W]))
    v = _gelu_tanh(_dot(x, w_ref[:, W:2 * W]))
    vc = v - _group_mean(v, avg)
    var = _group_mean(vc * vc, avg)
    vn = (vc * lax.rsqrt(var + LN_EPS) * sg_ref[...]).astype(BF16)
    for n in range(tm // CHUNK):
        rows = slice(n * CHUNK, (n + 1) * CHUNK)
        for p in range(N_HEADS // 2):
            cols = slice(p * LANES, (p + 1) * LANES)
            vp = vn[rows, cols]
            w_e = jnp.where(tri, sw_ref[2 * p], 0.0).astype(BF16)
            w_o = jnp.where(tri, sw_ref[2 * p + 1], 0.0).astype(BF16)
            mix = (_dot(w_e, jnp.where(lo_half, vp, 0)) + _dot(w_o, jnp.where(lo_half, 0, vp))
                   + sb_ref[:, cols])
            o_ref[rows, cols] = (u[rows, cols] * mix).astype(o_ref.dtype)

    g_b = _dot(x, w_ref[:, 2 * W:3 * W])
    y = _dot(x, w_ref[:, 3 * W:4 * W]) * _dot(x, w_ref[:, 4 * W:5 * W])
    conv = _causal_conv3(y, jnp.where(first, 0.0, tail_ref[...]), cw_ref[...])
    tail_ref[...] = y[tm - SUBLANES:tm, :]
    o_ref[:, W:2 * W] = (g_b * conv).astype(o_ref.dtype)


def _odd_mixer(xb, w_in, sgu_g, sgu_w, sgu_b_tile, conv_w, seq, tm):
    n, k = xb.shape
    W = MIX_HALF
    return pl.pallas_call(
        functools.partial(_odd_mixer_kernel, tiles_per_seq=seq // tm),
        grid=(n // tm,),
        in_specs=[pl.BlockSpec((tm, k), lambda i: (i, 0)),
                  pl.BlockSpec((k, 5 * W), lambda i: (0, 0)),
                  pl.BlockSpec((1, W), lambda i: (0, 0)),
                  pl.BlockSpec((N_HEADS, CHUNK, CHUNK), lambda i: (0, 0, 0)),
                  pl.BlockSpec((CHUNK, W), lambda i: (0, 0)),
                  pl.BlockSpec((CONV_W, W), lambda i: (0, 0))],
        out_specs=pl.BlockSpec((tm, 2 * W), lambda i: (i, 0)),
        out_shape=jax.ShapeDtypeStruct((n, 2 * W), BF16),
        scratch_shapes=[pltpu.VMEM((SUBLANES, W), F32)],
        compiler_params=_cparams(("arbitrary",)),
        name="odd_mixer",
    )(xb, w_in, sgu_g, sgu_w, sgu_b_tile, conv_w)


def _t5_bucket_np(dist):
    max_exact = N_BUCKETS // 2
    n = np.maximum(dist, 0)
    nf = np.maximum(n, 1).astype(np.float32)
    large = max_exact + (np.log(nf / max_exact) / np.float32(math.log(T5_MAX_DIST / max_exact))
                         * (N_BUCKETS - max_exact)).astype(np.int32)
    large = np.minimum(large, N_BUCKETS - 1)
    return np.where(n < max_exact, n, large)


def _bias_tables(rel_bias):
    T = BLK
    s = np.arange(T)[:, None]
    t = np.arange(T)[None, :]
    bucket = _t5_bucket_np(np.stack([t + d * T - s for d in range(3)]))
    far_bucket = N_BUCKETS - 1
    assert (bucket[2] == far_bucket).all()
    assert (_t5_bucket_np(np.arange(T + 1, 64 * T)) == far_bucket).all()
    rb = rel_bias.astype(F32) * LOG2E
    tiles = sum(jnp.where(jnp.asarray(bucket == b)[..., None], rb[b], 0.0) for b in range(N_BUCKETS))
    tiles = jnp.swapaxes(tiles, 2, 3).reshape(3, T, N_HEADS * T)
    far = jnp.tile(jnp.repeat(rb[far_bucket], T)[None, :], (1, 2))
    return tiles, far


def _placement_constants():
    H, L = N_HEADS, LANES
    pq = np.zeros((3 * L, H * L), np.float32)
    cq = np.zeros((1, H * L), np.float32)
    pk = np.zeros((3 * L, (H // 2) * L), np.float32)
    ck = np.zeros((1, (H // 2) * L), np.float32)
    for h in range(H):
        p, e = divmod(h, 2)
        for piece in range(3):
            pq[piece * L + h, h * L + 6 + piece] = 1.0
            cq[0, h * L + 3 * e + piece] = 1.0
            pk[piece * L + h, p * L + 3 * e + piece] = -1.0
            ck[0, p * L + 6 + piece] = 1.0
    return jnp.asarray(pq, BF16), jnp.asarray(cq), jnp.asarray(pk, BF16), jnp.asarray(ck)


def _even_weights(w_in, w_uk, w_uv):
    hd, H = HEAD_DIM, N_HEADS
    widths = (H * hd, KV_RANK, H * hd, hd, H, H * hd, H * hd, H * hd, H)
    offs = np.cumsum((0,) + widths)
    seg = lambda s: w_in[:, offs[s]:offs[s + 1]]
    q_a, c_kv, q_idx, k_idx, w_idx, q_b, k_b, v_b, f_in = [seg(s) for s in range(9)]
    scale = hd ** -0.5
    w_big = jnp.concatenate([q_a, q_idx * scale, q_b * (scale * LOG2E), k_b, v_b, c_kv,
                             k_idx, k_idx], axis=1).astype(BF16)
    pad = jnp.zeros((w_in.shape[0], LANES - 2 * H), w_in.dtype)
    w_small = jnp.concatenate([w_idx, f_in, pad], axis=1).astype(BF16)
    wuk_t = jnp.swapaxes(w_uk, 1, 2) * (scale * LOG2E)
    odd = (np.arange(H) % 2 == 1)[:, None, None]
    zk, zv = jnp.zeros_like(wuk_t), jnp.zeros_like(w_uv)
    wuk_pad = jnp.where(odd, jnp.concatenate([zk, wuk_t], axis=1), jnp.concatenate([wuk_t, zk], axis=1))
    wuv_pad = jnp.where(odd, jnp.concatenate([zv, w_uv], axis=2), jnp.concatenate([w_uv, zv], axis=2))
    return w_big, w_small, wuk_pad.astype(BF16), wuv_pad.astype(BF16)


def kernel(x, ln_g, ln_b, rel_bias, ev_w_in, ev_w_uk, ev_w_uv, ev_b_f, ev_w_o,
           od_w_in, od_sgu_g, od_sgu_w, od_sgu_b, od_conv_w, od_w_o,
           ffn_w_up, ffn_conv_w, ffn_w_down):
    bsz, seq, d = x.shape
    depth = ln_g.shape[0]
    n = bsz * seq
    alpha = (2.0 * depth) ** 0.25
    k_sel = min(TOPK_MAX, seq // 4)
    H = N_HEADS
    assert seq % QUERIES == 0 and seq % CHUNK == 0
    tm = min(512, seq)
    tm_ffn = min(1024, seq)

    bias_tiles, bias_far = _bias_tables(rel_bias)
    place = _placement_constants()
    xf = x.reshape(n, d)
    xb = xf
    for layer in range(depth):
        j = layer // 2
        g0, b0 = ln_g[layer, 0][None, :], ln_b[layer, 0][None, :]
        g1, b1 = ln_g[layer, 1][None, :], ln_b[layer, 1][None, :]
        if layer % 2 == 0:
            w_big, w_small, wuk_pad, wuv_pad = _even_weights(ev_w_in[j], ev_w_uk[j], ev_w_uv[j])
            hb, hs = _even_proj(xb, w_big, w_small, tm)
            hb = hb.reshape(bsz, seq, -1)
            hs_t = jnp.swapaxes(hs.reshape(bsz, seq, -1)[:, :, 0:2 * H], 1, 2)
            cum_f = _cumf(hs_t[:, H:2 * H], ev_b_f[j][:, None])
            f_rows = jnp.pad(jnp.swapaxes(cum_f, 1, 2), ((0, 0), (0, 0), (0, LANES - H)))
            mix = _attention(hb, hs_t[:, 0:H], f_rows, wuk_pad, wuv_pad, bias_tiles, bias_far,
                             place, k_sel)
            w_o = ev_w_o[j]
        else:
            sgu_b_tile = jnp.repeat(jnp.swapaxes(od_sgu_b[j], 0, 1), HEAD_DIM, axis=1)
            mix = _odd_mixer(xb, od_w_in[j].astype(BF16), od_sgu_g[j].reshape(1, -1), od_sgu_w[j],
                             sgu_b_tile, od_conv_w[j], seq, min(256, seq))
            w_o = od_w_o[j]
        xf, xb = _proj_ln(mix.reshape(n, d), w_o.astype(BF16), xf, g0, b0, alpha, tm_ffn)
        act = _ffn_up(xb, ffn_w_up[layer].astype(BF16), ffn_conv_w[layer], seq, tm_ffn, 256,
                      min(512, seq))
        xf, xb = _proj_ln(act, ffn_w_down[layer].astype(BF16), xf, g1, b1, alpha, tm_ffn)
    return xf.reshape(bsz, seq, d)
```

```python
import functools
import math

import numpy as np
import jax
import jax.numpy as jnp
from jax import lax
from jax.experimental import pallas as pl
from jax.experimental.pallas import tpu as pltpu

F32 = jnp.float32
BF16 = jnp.bfloat16

HEAD_DIM = 64
N_HEADS = 8
KV_RANK = 128
TOPK_MAX = 256
N_BUCKETS = 32
T5_MAX_DIST = 128
CHUNK = 128
CONV_W = 3
LN_EPS = 1e-5
MIX_HALF = N_HEADS * HEAD_DIM

LANES = 128
SUBLANES = 8
BF16_ROWS = 16
VMEM_LIMIT = 56 * 1024 * 1024

BLK = 128
QUERIES = 2 * BLK
PV_ROWS = KV_RANK + BF16_ROWS
NEG_BIG = -1e30
MASKED = -(2.0 ** 100)
F32_MAX = float(np.finfo(np.float32).max)
LOG2E = math.log2(math.e)
SEARCH_UNROLL = 4


def _cparams(sem):
    return pltpu.CompilerParams(dimension_semantics=sem, vmem_limit_bytes=VMEM_LIMIT)


def _dot(a, b):
    return jnp.dot(a, b, preferred_element_type=F32)


def _dot_nt(a, b):
    return lax.dot_general(a, b, (((1,), (1,)), ((), ())), preferred_element_type=F32)


def _layer_norm_rows(y, g, b):
    mu = jnp.mean(y, axis=-1, keepdims=True)
    yc = y - mu
    var = jnp.mean(yc * yc, axis=-1, keepdims=True)
    return yc * lax.rsqrt(var + LN_EPS) * g + b


def _tree(op, x):
    while x.shape[0] > 1:
        half = x.shape[0] // 2
        x = op(x[:half], x[half:])
    return x[0]


def _bf16_pieces(v):
    hi = v.astype(BF16)
    r1 = v - hi.astype(F32)
    mid = r1.astype(BF16)
    lo = (r1 - mid.astype(F32)).astype(BF16)
    return jnp.concatenate([hi, mid, lo], axis=1)


def _even_proj_kernel(x_ref, wb_ref, ws_ref, hb_ref, hs_ref):
    x = x_ref[...].astype(BF16)
    hb_ref[...] = _dot(x, wb_ref[...]).astype(hb_ref.dtype)
    hs_ref[...] = _dot(x, ws_ref[...])


def _even_proj(xb, w_big, w_small, tm):
    n, k = xb.shape
    mb = w_big.shape[1]
    ms = w_small.shape[1]
    return pl.pallas_call(
        _even_proj_kernel,
        grid=(n // tm,),
        in_specs=[pl.BlockSpec((tm, k), lambda i: (i, 0)),
                  pl.BlockSpec((k, mb), lambda i: (0, 0)),
                  pl.BlockSpec((k, ms), lambda i: (0, 0))],
        out_specs=[pl.BlockSpec((tm, mb), lambda i: (i, 0)),
                   pl.BlockSpec((tm, ms), lambda i: (i, 0))],
        out_shape=[jax.ShapeDtypeStruct((n, mb), BF16),
                   jax.ShapeDtypeStruct((n, ms), F32)],
        compiler_params=_cparams(("parallel",)),
        name="even_proj",
    )(xb, w_big, w_small)


def _cumf_kernel(f_ref, b_ref, o_ref):
    seq = f_ref.shape[-1]
    row = lax.broadcasted_iota(jnp.int32, (LANES, LANES), 0)
    col = lax.broadcasted_iota(jnp.int32, (LANES, LANES), 1)
    upper = jnp.where(row <= col, 1.0, 0.0).astype(F32)
    carry = jnp.zeros((f_ref.shape[0], 1), F32)
    for c in range(seq // LANES):
        z = f_ref[:, c * LANES:(c + 1) * LANES] + b_ref[...]
        lf = jnp.minimum(z, 0.0) - jnp.log1p(jnp.exp(-jnp.abs(z)))
        cs = jnp.dot(lf, upper, preferred_element_type=F32,
                     precision=lax.Precision.HIGHEST) + carry
        o_ref[:, c * LANES:(c + 1) * LANES] = cs
        carry = cs[:, LANES - 1:LANES]


def _cumf(f_t, b_f):
    bsz, nh, seq = f_t.shape
    return pl.pallas_call(
        _cumf_kernel,
        grid=(bsz,),
        in_specs=[pl.BlockSpec((None, nh, seq), lambda b: (b, 0, 0)),
                  pl.BlockSpec((nh, 1), lambda b: (0, 0))],
        out_specs=pl.BlockSpec((None, nh, seq), lambda b: (b, 0, 0)),
        out_shape=jax.ShapeDtypeStruct((bsz, nh, seq), F32),
        compiler_params=_cparams(("parallel",)),
        name="fox_cumf",
    )(f_t, b_f)


def _attn_kernel(qa_ref, qi_ref, qb_ref, wt_ref, fq_ref, kb_ref, vb_ref, ckv_ref, kidx_ref, fk_ref,
                 wuk_ref, wuv_ref, bias_ref, bfar_ref, pq_ref, cq_ref, pk_ref, ck_ref,
                 o_ref,
                 score_ref, lhsd_ref, ct_ref, kaug_ref, vt_ref, qsel_ref, qaug_ref, qim_ref,
                 m_ref, acc_ref, s_ref, tie_ref, *, k_sel):
    T = BLK
    Q = QUERIES
    H = N_HEADS
    NP = H // 2
    NT = 2 * NP
    TW = 2 * T
    g = pl.program_id(1)
    i0 = 2 * g
    nblk = ct_ref.shape[0]
    lane = lax.broadcasted_iota(jnp.int32, (T, LANES), 1)
    lo_half = lane < HEAD_DIM
    ones_rows = jnp.ones((BF16_ROWS, T), BF16)
    causal_qq = (lax.broadcasted_iota(jnp.int32, (Q, Q), 0) <= lax.broadcasted_iota(jnp.int32, (Q, Q), 1))

    @pl.when(g == 0)
    def _per_batch():
        def prep(j, c):
            r = pl.ds(pl.multiple_of(j * T, T), T)
            cb = ckv_ref[r, :]
            lhsd_ref[0, r, 0:KV_RANK] = cb
            lhsd_ref[1, r, 0:KV_RANK] = cb
            ct_ref[j, 0:KV_RANK, :] = cb.astype(F32).T.astype(BF16)
            ct_ref[j, KV_RANK:PV_ROWS, :] = ones_rows
            ext = (_dot(_bf16_pieces(fk_ref[r, :] * LOG2E), pk_ref[...]) + ck_ref[...]).astype(BF16)
            for p in range(NP):
                cols = slice(p * LANES, (p + 1) * LANES)
                kaug_ref[p, r, 0:LANES] = kb_ref[r, cols]
                kaug_ref[p, r, LANES:2 * LANES] = ext[:, cols]
                vt_ref[p, j, 0:LANES, :] = vb_ref[r, cols].astype(F32).T.astype(BF16)
                vt_ref[p, j, LANES:PV_ROWS, :] = ones_rows
            return c
        lax.fori_loop(0, nblk, prep, 0)

    w_rows = wt_ref[...] * (H ** -0.5)
    ext_q = (_dot(_bf16_pieces(fq_ref[...] * LOG2E), pq_ref[...]) + cq_ref[...]).astype(BF16)
    eye = jnp.where(lax.broadcasted_iota(jnp.int32, (T, T), 0) == lax.broadcasted_iota(jnp.int32, (T, T), 1),
                    1.0, 0.0).astype(BF16)
    for tt in range(NT):
        hf, p = divmod(tt, NP)
        qrows = slice(hf * T, (hf + 1) * T)
        cols = slice(p * LANES, (p + 1) * LANES)
        qa_pair, qi_pair, qb_pair = qa_ref[qrows, cols], qi_ref[qrows, cols], qb_ref[qrows, cols]
        for e in range(2):
            h = 2 * p + e
            rows = slice(e * T, (e + 1) * T)
            mine = lo_half if e == 0 else jnp.logical_not(lo_half)
            qsel_ref[tt, rows, 0:KV_RANK] = _dot(qa_pair, wuk_ref[h]).astype(BF16)
            qsel_ref[tt, rows, KV_RANK:2 * KV_RANK] = eye
            qaug_ref[tt, rows, 0:LANES] = jnp.where(mine, qb_pair, 0)
            qaug_ref[tt, rows, LANES:2 * LANES] = ext_q[qrows, h * LANES:(h + 1) * LANES]
            qim_ref[tt, rows, :] = jnp.where(mine, qi_pair, 0)

    def reset_state():
        m_ref[...] = jnp.full(m_ref.shape, NEG_BIG, F32)
        acc_ref[...] = jnp.zeros(acc_ref.shape, F32)

    def tile_cols(tt):
        return slice(tt * TW, (tt + 1) * TW)

    def online_update(tt, s, pv_lhs):
        cols = tile_cols(tt)
        m_old = m_ref[:, cols]
        s_max = _tree(jnp.maximum, s.reshape(-1, SUBLANES, TW))
        m_new = jnp.maximum(m_old, jnp.max(s_max, axis=0, keepdims=True))
        alpha = jnp.exp2(m_old - m_new)
        pr = jnp.exp2((s - m_new[0:1, :]).astype(BF16))
        acc_ref[:, cols] = acc_ref[:, cols] * alpha[0:1, :] + _dot(pv_lhs, pr)
        m_ref[:, cols] = m_new

    def key_rows(j, nblocks):
        return pl.ds(pl.multiple_of(j * T, T), nblocks * T)

    def slot_rows(nblocks):
        return slice(0, nblocks * T)

    def blocks_t(load, j, nblocks):
        return load(j) if nblocks == 1 else jnp.concatenate([load(j + b) for b in range(nblocks)], axis=1)

    def staged_steps(j0, n_steps, logits, absorb):
        logits(j0, 2, 0)
        for k in range(n_steps):
            if k + 1 < n_steps:
                logits(j0 + 2 * (k + 1), 2, (k + 1) % 2)
            absorb(j0 + 2 * k, 2, k % 2)

    def for_far_blocks(n_far, logits, absorb):
        n_oct = lax.shift_right_logical(n_far, 3)

        def oct_body(jj, c):
            staged_steps(8 * jj, 4, logits, absorb)
            return c
        lax.fori_loop(0, n_oct, oct_body, 0)

        @pl.when((n_far & 4) != 0)
        def _quad():
            staged_steps(8 * n_oct, 2, logits, absorb)

        @pl.when((n_far & 2) != 0)
        def _pair():
            staged_steps(8 * n_oct + (n_far & 4), 1, logits, absorb)

        @pl.when((n_far & 1) != 0)
        def _single():
            logits(n_far - 1, 1, 0)
            absorb(n_far - 1, 1, 0)

    def idx_scores(j, nblocks):
        k_blk = kidx_ref[key_rows(j, nblocks), :]
        acc = [jnp.zeros((nblocks * T, T), F32), jnp.zeros((nblocks * T, T), F32)]
        for tt in range(NT):
            hf, p = divmod(tt, NP)
            s = _dot_nt(k_blk, qim_ref[tt])
            for e in range(2):
                h = 2 * p + e
                w_h = w_rows[h:h + 1, hf * T:(hf + 1) * T]
                acc[hf] = acc[hf] + w_h * jnp.maximum(s[:, e * T:(e + 1) * T], 0.0)
        return jnp.concatenate(acc, axis=1)

    def fox_logits(j, nblocks, slot, causal=False):
        for tt in range(NT):
            hf, p = divmod(tt, NP)
            s = _dot_nt(kaug_ref[p, key_rows(j, nblocks), :], qaug_ref[tt])
            if causal:
                ok = causal_qq[:, hf * T:(hf + 1) * T]
                s = jnp.where(jnp.concatenate([ok, ok], axis=1), s, MASKED)
            s_ref[slot, slot_rows(nblocks), tile_cols(tt)] = s

    def fox_absorb(j, nblocks, slot):
        for tt in range(NT):
            p = tt % NP
            online_update(tt, s_ref[slot, slot_rows(nblocks), tile_cols(tt)],
                          blocks_t(lambda jb: vt_ref[p, jb], j, nblocks))

    def idx_fox_logits(j, nblocks, slot):
        score_ref[pl.ds(j, nblocks)] = idx_scores(j, nblocks).reshape(nblocks, T, Q)
        fox_logits(j, nblocks, slot)

    reset_state()
    for_far_blocks(i0, idx_fox_logits, fox_absorb)
    score_ref[pl.ds(i0, 2)] = jnp.where(causal_qq, idx_scores(i0, 2), -jnp.inf).reshape(2, T, Q)
    fox_logits(i0, 2, 0, causal=True)
    fox_absorb(i0, 2, 0)

    row_lo = lax.broadcasted_iota(jnp.int32, (LANES, T), 0) < HEAD_DIM
    for tt in range(NT):
        hf, p = divmod(tt, NP)
        c0 = slice(tt * TW, tt * TW + T)
        c1 = slice(tt * TW + T, (tt + 1) * TW)
        o_t = jnp.where(row_lo, acc_ref[0:LANES, c0] / acc_ref[LANES:LANES + 1, c0],
                        acc_ref[0:LANES, c1] / acc_ref[LANES:LANES + 1, c1])
        o_ref[hf * T:(hf + 1) * T, MIX_HALF + p * LANES:MIX_HALF + (p + 1) * LANES] = (
            o_t.T.astype(o_ref.dtype))

    kf = float(k_sel)
    t_pos = (g * Q + lax.broadcasted_iota(jnp.int32, (SUBLANES, Q), 1)).astype(F32)
    searching = (t_pos + 1.0) > kf
    G2 = 2 * T // SUBLANES
    n_pair = g + 1
    key_in_pair = (lax.broadcasted_iota(jnp.int32, (G2, SUBLANES, Q), 0) * SUBLANES
                   + lax.broadcasted_iota(jnp.int32, (G2, SUBLANES, Q), 1))

    def rep(v):
        return jnp.broadcast_to(v, (SUBLANES, Q))

    def score_pair(jj):
        return score_ref[pl.ds(2 * jj, 2)].reshape(G2, SUBLANES, Q)

    def count(*pred_fns, load=score_pair):
        def body(jj, cs):
            s = load(jj)
            return tuple(c + _tree(jnp.add, jnp.where(f(s, jj), 1.0, 0.0)) for c, f in zip(cs, pred_fns))
        zero = jnp.zeros((SUBLANES, Q), F32)
        cs = lax.fori_loop(0, n_pair, body, (zero,) * len(pred_fns))
        return tuple(rep(jnp.sum(c, axis=0, keepdims=True)) for c in cs)

    def count_ge(thr):
        return count(lambda s, jj: s >= thr[None])[0]

    def select_threshold():
        def mm_body(jj, carry):
            mn, mx = carry
            s = score_pair(jj)
            return (jnp.minimum(mn, _tree(jnp.minimum, jnp.where(s == -jnp.inf, jnp.inf, s))),
                    jnp.maximum(mx, _tree(jnp.maximum, s)))

        mn, mx = lax.fori_loop(0, n_pair, mm_body,
                               (jnp.full((SUBLANES, Q), jnp.inf, F32),
                                jnp.full((SUBLANES, Q), -jnp.inf, F32)))
        mn = rep(jnp.min(mn, axis=0, keepdims=True))
        mx = rep(jnp.max(mx, axis=0, keepdims=True))
        above_mx = mx + jnp.maximum(jnp.abs(mx) * 2.0 ** -20, 1e-30)
        c_ge0, c_gt0 = count(lambda s, jj: s >= 0.0, lambda s, jj: s > 0.0)
        thr_ge0 = c_ge0 >= kf
        thr_gt0 = c_gt0 >= kf
        lo0 = jnp.where(thr_ge0, 0.0, mn)
        cnt_lo0 = jnp.where(thr_ge0, c_ge0, t_pos + 1.0)
        hi0 = jnp.where(thr_gt0, above_mx, 0.0)
        cnt_hi0 = jnp.where(thr_gt0, 0.0, jnp.where(thr_ge0, c_gt0, c_ge0))
        undecided = thr_gt0 | jnp.logical_not(thr_ge0)
        active0 = jnp.where(searching & undecided & (cnt_lo0 > kf), 1.0, 0.0)

        def cond(st):
            return jnp.logical_and(st[5] > 0.0, st[6] < 2200)

        def bisect(st):
            lo, hi, cnt_lo, cnt_hi, active = st
            mid = 0.5 * lo + 0.5 * hi
            adjacent = (mid <= lo) | (mid >= hi)
            c = count_ge(mid)
            act = active > 0.0
            go_lo = act & jnp.logical_not(adjacent) & (c >= kf)
            go_hi = act & jnp.logical_not(adjacent) & (c < kf)
            lo = jnp.where(go_lo, mid, lo)
            cnt_lo = jnp.where(go_lo, c, cnt_lo)
            hi = jnp.where(go_hi, mid, hi)
            cnt_hi = jnp.where(go_hi, c, cnt_hi)
            active = jnp.where(act & jnp.logical_not(adjacent) & (cnt_lo > kf), 1.0, 0.0)
            return lo, hi, cnt_lo, cnt_hi, active

        def body(st):
            inner = st[:5]
            for _ in range(SEARCH_UNROLL):
                inner = bisect(inner)
            return inner + (jnp.max(inner[4]), st[6] + SEARCH_UNROLL)

        lo, hi, cnt_lo, cnt_hi, _, _, _ = lax.while_loop(
            cond, body, (lo0, hi0, cnt_lo0, cnt_hi0, active0, jnp.max(active0), 0))
        thr = jnp.where(searching, lo, -F32_MAX)

        tied = searching & (cnt_lo > kf)
        need = kf - cnt_hi

        @pl.when(jnp.max(jnp.where(tied, 1.0, 0.0)) > 0.0)
        def _break_ties():
            def tie_pair(jj):
                return tie_ref[pl.ds(2 * jj, 2)].reshape(G2, SUBLANES, Q)

            def mark_body(jj, c):
                key_index = (2 * jj * T + key_in_pair).astype(F32)
                tie_ref[pl.ds(2 * jj, 2)] = (
                    jnp.where(score_pair(jj) == thr[None], key_index, jnp.inf).reshape(2, T, Q))
                return c
            lax.fori_loop(0, n_pair, mark_body, 0)

            def count_tied_upto(jcut):
                return count(lambda e, jj: e <= jcut[None], load=tie_pair)[0]

            def bs_body(_, st):
                jl, jh = st
                jm = jnp.floor(0.5 * (jl + jh))
                ok = count_tied_upto(jm) >= need
                return jnp.where(ok, jl, jm), jnp.where(ok, jm, jh)

            n_keys = (2 * n_pair * T).astype(F32)
            jl0 = jnp.full((SUBLANES, Q), -1.0, F32)
            jh0 = jnp.zeros((SUBLANES, Q), F32) + (n_keys - 1.0)
            n_steps = int(math.ceil(math.log2(nblk * T))) + 1
            _, jcut = lax.fori_loop(0, n_steps, bs_body, (jl0, jh0))

            def drop_body(jj, c):
                e = tie_pair(jj)
                drop = tied[None] & (e > jcut[None]) & (e < jnp.inf)
                score_ref[pl.ds(2 * jj, 2)] = jnp.where(drop, -jnp.inf, score_pair(jj)).reshape(2, T, Q)
                return c
            lax.fori_loop(0, n_pair, drop_body, 0)

        return thr

    thr = lax.cond((i0 + 2) * T > k_sel, select_threshold,
                   lambda: jnp.full((SUBLANES, Q), -F32_MAX, F32))

    def mask_body(jj, c):
        add = jnp.where(score_pair(jj) >= thr[None], 0.0, MASKED).reshape(2 * T, Q).astype(BF16)
        for hf in range(2):
            lhsd_ref[hf, key_rows(2 * jj, 2), KV_RANK:2 * KV_RANK] = add[:, hf * T:(hf + 1) * T]
        return c

    lax.fori_loop(0, n_pair, mask_body, 0)

    def dsa_logits(j, nblocks, slot, bias=None):
        for tt in range(NT):
            hf, p = divmod(tt, NP)
            s = _dot_nt(lhsd_ref[hf, key_rows(j, nblocks), :], qsel_ref[tt])
            if bias is not None:
                s = s + bias(hf, slice(p * TW, (p + 1) * TW))
            s_ref[slot, slot_rows(nblocks), tile_cols(tt)] = s

    def dsa_absorb(j, nblocks, slot):
        pv_lhs = blocks_t(lambda jb: ct_ref[jb], j, nblocks)
        for tt in range(NT):
            online_update(tt, s_ref[slot, slot_rows(nblocks), tile_cols(tt)], pv_lhs)

    def bias_before(hf, cols):
        return bias_ref[1 + hf, :, cols]

    def bias_diag(hf, cols):
        first = bias_ref[hf, :, cols]
        return jnp.concatenate([first, bias_ref[0, :, cols]], axis=0)

    reset_state()
    for_far_blocks(jnp.maximum(i0 - 1, 0), dsa_logits, dsa_absorb)
    m_ref[...] = m_ref[...] + bfar_ref[...]

    @pl.when(g >= 1)
    def _dsa_tail():
        dsa_logits(i0 - 1, 1, 0, bias_before)
        dsa_logits(i0, 2, 1, bias_diag)
        dsa_absorb(i0 - 1, 1, 0)
        dsa_absorb(i0, 2, 1)

    @pl.when(g == 0)
    def _dsa_first():
        dsa_logits(0, 2, 0, bias_diag)
        dsa_absorb(0, 2, 0)

    for tt in range(NT):
        hf, p = divmod(tt, NP)
        o_pair = jnp.zeros((T, LANES), F32)
        for e in range(2):
            cols = slice(tt * TW + e * T, tt * TW + (e + 1) * T)
            o_t = acc_ref[0:KV_RANK, cols] / acc_ref[KV_RANK:KV_RANK + 1, cols]
            o_pair = o_pair + _dot(o_t.T.astype(BF16), wuv_ref[2 * p + e])
        o_ref[hf * T:(hf + 1) * T, p * LANES:(p + 1) * LANES] = o_pair.astype(o_ref.dtype)


_EV_QA, _EV_QI, _EV_QB, _EV_KB, _EV_VB = 0, 1, 2, 3, 4
_EV_CKV, _EV_KIDX = 20, 21


def _attention(hb, w_t, f_rows, wuk_pad, wuv_pad, bias_tiles, bias_far, place, k_sel):
    bsz, seq, _ = hb.shape
    T, Q, H = BLK, QUERIES, N_HEADS
    nblk = seq // T
    lanes = 2 * H * T
    pq, cq, pk, ck = place
    wide = lambda c: pl.BlockSpec((None, Q, MIX_HALF), lambda b, i, c=c: (b, i, c))
    once = dict(pipeline_mode=pl.Buffered(1))
    full_wide = lambda c: pl.BlockSpec((None, seq, MIX_HALF), lambda b, i, c=c: (b, 0, c), **once)
    full_narrow = lambda c: pl.BlockSpec((None, seq, LANES), lambda b, i, c=c: (b, 0, c), **once)
    const = lambda a: pl.BlockSpec(a.shape, lambda b, i, nd=a.ndim: (0,) * nd, **once)
    return pl.pallas_call(
        functools.partial(_attn_kernel, k_sel=k_sel),
        grid=(bsz, seq // Q),
        in_specs=[wide(_EV_QA), wide(_EV_QI), wide(_EV_QB),
                  pl.BlockSpec((None, H, Q), lambda b, i: (b, 0, i)),
                  pl.BlockSpec((None, Q, LANES), lambda b, i: (b, i, 0)),
                  full_wide(_EV_KB), full_wide(_EV_VB),
                  full_narrow(_EV_CKV), full_narrow(_EV_KIDX),
                  pl.BlockSpec((None, seq, LANES), lambda b, i: (b, 0, 0), **once),
                  const(wuk_pad), const(wuv_pad), const(bias_tiles), const(bias_far),
                  const(pq), const(cq), const(pk), const(ck)],
        out_specs=pl.BlockSpec((None, Q, 2 * MIX_HALF), lambda b, i: (b, i, 0)),
        out_shape=jax.ShapeDtypeStruct((bsz, seq, 2 * MIX_HALF), BF16),
        scratch_shapes=[pltpu.VMEM((nblk, T, Q), F32),
                        pltpu.VMEM((2, seq, 2 * KV_RANK), BF16),
                        pltpu.VMEM((nblk, PV_ROWS, T), BF16),
                        pltpu.VMEM((H // 2, seq, 2 * LANES), BF16),
                        pltpu.VMEM((H // 2, nblk, PV_ROWS, T), BF16),
                        pltpu.VMEM((H, 2 * T, 2 * KV_RANK), BF16),
                        pltpu.VMEM((H, 2 * T, 2 * LANES), BF16),
                        pltpu.VMEM((H, 2 * T, LANES), BF16),
                        pltpu.VMEM((SUBLANES, lanes), F32),
                        pltpu.VMEM((PV_ROWS, lanes), F32),
                        pltpu.VMEM((2, 2 * T, lanes), F32),
                        pltpu.VMEM((nblk, T, Q), F32)],
        compiler_params=_cparams(("parallel", "arbitrary")),
        name="dsa_fox_attention",
    )(hb, hb, hb, w_t, f_rows, hb, hb, hb, hb, f_rows, wuk_pad, wuv_pad, bias_tiles, bias_far,
      pq, cq, pk, ck)


def _proj_ln_kernel(a_ref, w_ref, x_ref, g_ref, b_ref, xo_ref, xb_ref, *, alpha, sub):
    for c in range(a_ref.shape[0] // sub):
        rows = slice(c * sub, (c + 1) * sub)
        y = alpha * x_ref[rows, :] + _dot(a_ref[rows, :], w_ref[...])
        out = _layer_norm_rows(y, g_ref[...], b_ref[...])
        xo_ref[rows, :] = out
        xb_ref[rows, :] = out.astype(BF16)


def _proj_ln(a, w, x, g, b, alpha, tm):
    n, k = a.shape
    d = w.shape[1]
    return pl.pallas_call(
        functools.partial(_proj_ln_kernel, alpha=alpha, sub=min(128, tm)),
        grid=(n // tm,),
        in_specs=[pl.BlockSpec((tm, k), lambda i: (i, 0)),
                  pl.BlockSpec((k, d), lambda i: (0, 0)),
                  pl.BlockSpec((tm, d), lambda i: (i, 0)),
                  pl.BlockSpec((1, d), lambda i: (0, 0)),
                  pl.BlockSpec((1, d), lambda i: (0, 0))],
        out_specs=[pl.BlockSpec((tm, d), lambda i: (i, 0)),
                   pl.BlockSpec((tm, d), lambda i: (i, 0))],
        out_shape=[jax.ShapeDtypeStruct((n, d), F32),
                   jax.ShapeDtypeStruct((n, d), BF16)],
        compiler_params=_cparams(("parallel",)),
        name="proj_residual_ln",
    )(a, w, x, g, b)


def _causal_conv3(h, tail, cw):
    w0, w1, w2 = cw[0:1, :], cw[1:2, :], cw[2:3, :]
    body = w0 * pltpu.roll(h, 2, axis=0) + w1 * pltpu.roll(h, 1, axis=0) + w2 * h
    head = h[0:SUBLANES, :]
    r = lax.broadcasted_iota(jnp.int32, head.shape, 0)
    hm1 = jnp.where(r < 1, pltpu.roll(tail, 1, axis=0), pltpu.roll(head, 1, axis=0))
    hm2 = jnp.where(r < 2, pltpu.roll(tail, 2, axis=0), pltpu.roll(head, 2, axis=0))
    out_head = w0 * hm2 + w1 * hm1 + w2 * head
    return jnp.concatenate([out_head, body[SUBLANES:, :]], axis=0)


def _ffn_up_kernel(x_ref, wg_ref, wv_ref, cg_ref, cv_ref, o_ref, tg_ref, tv_ref, *h_refs,
                   tiles_per_seq, sub):
    n_chunks = x_ref.shape[0] // sub
    hg_refs, hv_refs = h_refs[:n_chunks], h_refs[n_chunks:]
    first = (pl.program_id(1) % tiles_per_seq) == 0
    body_rows = slice(SUBLANES, SUBLANES + sub)

    def matmuls(c):
        x = x_ref[c * sub:(c + 1) * sub, :]
        hg_refs[c][body_rows, :] = _dot(x, wg_ref[...])
        hv_refs[c][body_rows, :] = _dot(x, wv_ref[...])

    def conv(h_ref, cw):
        return (cw[0:1, :] * h_ref[SUBLANES - 2:SUBLANES - 2 + sub, :]
                + cw[1:2, :] * h_ref[SUBLANES - 1:SUBLANES - 1 + sub, :]
                + cw[2:3, :] * h_ref[body_rows, :])

    def gate(c):
        if c == 0:
            hg_refs[0][0:SUBLANES, :] = jnp.where(first, 0.0, tg_ref[...])
            hv_refs[0][0:SUBLANES, :] = jnp.where(first, 0.0, tv_ref[...])
        else:
            hg_refs[c][0:SUBLANES, :] = hg_refs[c - 1][sub:sub + SUBLANES, :]
            hv_refs[c][0:SUBLANES, :] = hv_refs[c - 1][sub:sub + SUBLANES, :]
        g = conv(hg_refs[c], cg_ref[...])
        v = conv(hv_refs[c], cv_ref[...])
        half_g = 0.5 * g
        o_ref[c * sub:(c + 1) * sub, :] = ((half_g + half_g * jnp.tanh(half_g)) * v).astype(o_ref.dtype)

    matmuls(0)
    for c in range(n_chunks):
        if c + 1 < n_chunks:
            matmuls(c + 1)
        gate(c)
    tg_ref[...] = hg_refs[n_chunks - 1][sub:sub + SUBLANES, :]
    tv_ref[...] = hv_refs[n_chunks - 1][sub:sub + SUBLANES, :]


def _ffn_up(xb, w_up, conv_w, seq, tm, tn, sub):
    n, k = xb.shape
    dff = w_up.shape[1] // 2
    ncol = dff // tn
    return pl.pallas_call(
        functools.partial(_ffn_up_kernel, tiles_per_seq=seq // tm, sub=sub),
        grid=(ncol, n // tm),
        in_specs=[pl.BlockSpec((tm, k), lambda j, i: (i, 0)),
                  pl.BlockSpec((k, tn), lambda j, i: (0, j)),
                  pl.BlockSpec((k, tn), lambda j, i, ncol=ncol: (0, j + ncol)),
                  pl.BlockSpec((CONV_W, tn), lambda j, i: (0, j)),
                  pl.BlockSpec((CONV_W, tn), lambda j, i, ncol=ncol: (0, j + ncol))],
        out_specs=pl.BlockSpec((tm, tn), lambda j, i: (i, j)),
        out_shape=jax.ShapeDtypeStruct((n, dff), BF16),
        scratch_shapes=([pltpu.VMEM((SUBLANES, tn), F32)] * 2
                        + [pltpu.VMEM((sub + SUBLANES, tn), F32)] * (2 * (tm // sub))),
        compiler_params=_cparams(("arbitrary", "arbitrary")),
        name="ffn_up_conv_gate",
    )(xb, w_up, w_up, conv_w, conv_w)


def _gelu_tanh(x):
    return 0.5 * x * (1.0 + jnp.tanh(math.sqrt(2.0 / math.pi) * (x + 0.044715 * (x * x * x))))


def _group_mean(v, avg):
    hi = v.astype(BF16)
    lo = (v - hi.astype(F32)).astype(BF16)
    return _dot(hi, avg) + _dot(lo, avg)


def _odd_mixer_kernel(x_ref, w_ref, sg_ref, sw_ref, sb_ref, cw_ref, wo_ref, xf_ref, g_ref, b_ref,
                      xo_ref, xb_ref, o_ref, tail_ref, *, tiles_per_seq, alpha):
    tm = x_ref.shape[0]
    W = MIX_HALF
    first = (pl.program_id(0) % tiles_per_seq) == 0
    x = x_ref[...]
    lane = lax.broadcasted_iota(jnp.int32, (CHUNK, LANES), 1)
    lo_half = lane < HEAD_DIM
    r = lax.broadcasted_iota(jnp.int32, (W, W), 0) // HEAD_DIM
    c = lax.broadcasted_iota(jnp.int32, (W, W), 1) // HEAD_DIM
    avg = jnp.where(r == c, 1.0 / HEAD_DIM, 0.0).astype(BF16)
    tri = (lax.broadcasted_iota(jnp.int32, (CHUNK, CHUNK), 1)
           <= lax.broadcasted_iota(jnp.int32, (CHUNK, CHUNK), 0))

    u = _gelu_tanh(_dot(x, w_ref[:, 0:W]))
    v = _gelu_tanh(_dot(x, w_ref[:, W:2 * W]))
    vc = v - _group_mean(v, avg)
    var = _group_mean(vc * vc, avg)
    vn = (vc * lax.rsqrt(var + LN_EPS) * sg_ref[...]).astype(BF16)
    for n in range(tm // CHUNK):
        rows = slice(n * CHUNK, (n + 1) * CHUNK)
        for p in range(N_HEADS // 2):
            cols = slice(p * LANES, (p + 1) * LANES)
            vp = vn[rows, cols]
            w_e = jnp.where(tri, sw_ref[2 * p], 0.0).astype(BF16)
            w_o = jnp.where(tri, sw_ref[2 * p + 1], 0.0).astype(BF16)
            mix = (_dot(w_e, jnp.where(lo_half, vp, 0)) + _dot(w_o, jnp.where(lo_half, 0, vp))
                   + sb_ref[:, cols])
            o_ref[rows, cols] = (u[rows, cols] * mix).astype(o_ref.dtype)

    g_b = _dot(x, w_ref[:, 2 * W:3 * W])
    y = _dot(x, w_ref[:, 3 * W:4 * W]) * _dot(x, w_ref[:, 4 * W:5 * W])
    conv = _causal_conv3(y, jnp.where(first, 0.0, tail_ref[...]), cw_ref[...])
    tail_ref[...] = y[tm - SUBLANES:tm, :]
    o_ref[:, W:2 * W] = (g_b * conv).astype(o_ref.dtype)

    out = _layer_norm_rows(alpha * xf_ref[...] + _dot(o_ref[...], wo_ref[...]), g_ref[...], b_ref[...])
    xo_ref[...] = out
    xb_ref[...] = out.astype(BF16)


def _odd_mixer(xb, w_in, sgu_g, sgu_w, sgu_b_tile, conv_w, w_o, xf, g, b, alpha, seq, tm):
    n, k = xb.shape
    W = MIX_HALF
    d = w_o.shape[1]
    const2 = lambda shape: pl.BlockSpec(shape, lambda i: (0, 0))
    row_tile = lambda width: pl.BlockSpec((tm, width), lambda i: (i, 0))
    return pl.pallas_call(
        functools.partial(_odd_mixer_kernel, tiles_per_seq=seq // tm, alpha=alpha),
        grid=(n // tm,),
        in_specs=[row_tile(k), const2((k, 5 * W)), const2((1, W)),
                  pl.BlockSpec((N_HEADS, CHUNK, CHUNK), lambda i: (0, 0, 0)),
                  const2((CHUNK, W)), const2((CONV_W, W)),
                  const2((2 * W, d)), row_tile(d), const2((1, d)), const2((1, d))],
        out_specs=[row_tile(d), row_tile(d)],
        out_shape=[jax.ShapeDtypeStruct((n, d), F32), jax.ShapeDtypeStruct((n, d), BF16)],
        scratch_shapes=[pltpu.VMEM((tm, 2 * W), BF16),
                        pltpu.VMEM((SUBLANES, W), F32)],
        compiler_params=_cparams(("arbitrary",)),
        name="odd_mixer",
    )(xb, w_in, sgu_g, sgu_w, sgu_b_tile, conv_w, w_o, xf, g, b)


def _t5_bucket_np(dist):
    max_exact = N_BUCKETS // 2
    n = np.maximum(dist, 0)
    nf = np.maximum(n, 1).astype(np.float32)
    large = max_exact + (np.log(nf / max_exact) / np.float32(math.log(T5_MAX_DIST / max_exact))
                         * (N_BUCKETS - max_exact)).astype(np.int32)
    large = np.minimum(large, N_BUCKETS - 1)
    return np.where(n < max_exact, n, large)


def _bias_tables(rel_bias):
    T = BLK
    s = np.arange(T)[:, None]
    t = np.arange(T)[None, :]
    bucket = _t5_bucket_np(np.stack([t + d * T - s for d in range(3)]))
    far_bucket = N_BUCKETS - 1
    assert (bucket[2] == far_bucket).all()
    assert (_t5_bucket_np(np.arange(T + 1, 64 * T)) == far_bucket).all()
    rb = rel_bias.astype(F32) * LOG2E
    tiles = sum(jnp.where(jnp.asarray(bucket == b)[..., None], rb[b], 0.0) for b in range(N_BUCKETS))
    tiles = jnp.swapaxes(tiles, 2, 3).reshape(3, T, N_HEADS * T)
    far = jnp.tile(jnp.repeat(rb[far_bucket], T)[None, :], (1, 2))
    return tiles, far


def _placement_constants():
    H, L = N_HEADS, LANES
    pq = np.zeros((3 * L, H * L), np.float32)
    cq = np.zeros((1, H * L), np.float32)
    pk = np.zeros((3 * L, (H // 2) * L), np.float32)
    ck = np.zeros((1, (H // 2) * L), np.float32)
    for h in range(H):
        p, e = divmod(h, 2)
        for piece in range(3):
            pq[piece * L + h, h * L + 6 + piece] = 1.0
            cq[0, h * L + 3 * e + piece] = 1.0
            pk[piece * L + h, p * L + 3 * e + piece] = -1.0
            ck[0, p * L + 6 + piece] = 1.0
    return jnp.asarray(pq, BF16), jnp.asarray(cq), jnp.asarray(pk, BF16), jnp.asarray(ck)


def _even_weights(w_in, w_uk, w_uv):
    hd, H = HEAD_DIM, N_HEADS
    widths = (H * hd, KV_RANK, H * hd, hd, H, H * hd, H * hd, H * hd, H)
    offs = np.cumsum((0,) + widths)
    seg = lambda s: w_in[:, offs[s]:offs[s + 1]]
    q_a, c_kv, q_idx, k_idx, w_idx, q_b, k_b, v_b, f_in = [seg(s) for s in range(9)]
    scale = hd ** -0.5
    w_big = jnp.concatenate([q_a, q_idx * scale, q_b * (scale * LOG2E), k_b, v_b, c_kv,
                             k_idx, k_idx], axis=1).astype(BF16)
    pad = jnp.zeros((w_in.shape[0], LANES - 2 * H), w_in.dtype)
    w_small = jnp.concatenate([w_idx, f_in, pad], axis=1).astype(BF16)
    wuk_t = jnp.swapaxes(w_uk, 1, 2) * (scale * LOG2E)
    odd = (np.arange(H) % 2 == 1)[:, None, None]
    zk, zv = jnp.zeros_like(wuk_t), jnp.zeros_like(w_uv)
    wuk_pad = jnp.where(odd, jnp.concatenate([zk, wuk_t], axis=1), jnp.concatenate([wuk_t, zk], axis=1))
    wuv_pad = jnp.where(odd, jnp.concatenate([zv, w_uv], axis=2), jnp.concatenate([w_uv, zv], axis=2))
    return w_big, w_small, wuk_pad.astype(BF16), wuv_pad.astype(BF16)


def kernel(x, ln_g, ln_b, rel_bias, ev_w_in, ev_w_uk, ev_w_uv, ev_b_f, ev_w_o,
           od_w_in, od_sgu_g, od_sgu_w, od_sgu_b, od_conv_w, od_w_o,
           ffn_w_up, ffn_conv_w, ffn_w_down):
    bsz, seq, d = x.shape
    depth = ln_g.shape[0]
    n = bsz * seq
    alpha = (2.0 * depth) ** 0.25
    k_sel = min(TOPK_MAX, seq // 4)
    H = N_HEADS
    assert seq % QUERIES == 0 and seq % CHUNK == 0
    tm = min(512, seq)
    tm_ffn = min(1024, seq)

    bias_tiles, bias_far = _bias_tables(rel_bias)
    place = _placement_constants()
    xf = x.reshape(n, d)
    xb = xf
    for layer in range(depth):
        j = layer // 2
        g0, b0 = ln_g[layer, 0][None, :], ln_b[layer, 0][None, :]
        g1, b1 = ln_g[layer, 1][None, :], ln_b[layer, 1][None, :]
        if layer % 2 == 0:
            w_big, w_small, wuk_pad, wuv_pad = _even_weights(ev_w_in[j], ev_w_uk[j], ev_w_uv[j])
            hb, hs = _even_proj(xb, w_big, w_small, tm)
            hb = hb.reshape(bsz, seq, -1)
            hs_t = jnp.swapaxes(hs.reshape(bsz, seq, -1)[:, :, 0:2 * H], 1, 2)
            cum_f = _cumf(hs_t[:, H:2 * H], ev_b_f[j][:, None])
            f_rows = jnp.pad(jnp.swapaxes(cum_f, 1, 2), ((0, 0), (0, 0), (0, LANES - H)))
            mix = _attention(hb, hs_t[:, 0:H], f_rows, wuk_pad, wuv_pad, bias_tiles, bias_far,
                             place, k_sel)
            xf, xb = _proj_ln(mix.reshape(n, d), ev_w_o[j].astype(BF16), xf, g0, b0, alpha, tm_ffn)
        else:
            sgu_b_tile = jnp.repeat(jnp.swapaxes(od_sgu_b[j], 0, 1), HEAD_DIM, axis=1)
            xf, xb = _odd_mixer(xb, od_w_in[j].astype(BF16), od_sgu_g[j].reshape(1, -1), od_sgu_w[j],
                                sgu_b_tile, od_conv_w[j], od_w_o[j].astype(BF16), xf, g0, b0, alpha,
                                seq, min(256, seq))
        act = _ffn_up(xb, ffn_w_up[layer].astype(BF16), ffn_conv_w[layer], seq, tm_ffn, 256,
                      min(512, seq))
        xf, xb = _proj_ln(act, ffn_w_down[layer].astype(BF16), xf, g1, b1, alpha, tm_ffn)
    return xf.reshape(bsz, seq, d)
```

```python
import functools
import math

import numpy as np
import jax
import jax.numpy as jnp
from jax import lax
from jax.experimental import pallas as pl
from jax.experimental.pallas import tpu as pltpu

F32 = jnp.float32
BF16 = jnp.bfloat16

HEAD_DIM = 64
N_HEADS = 8
KV_RANK = 128
TOPK_MAX = 256
N_BUCKETS = 32
T5_MAX_DIST = 128
CHUNK = 128
CONV_W = 3
LN_EPS = 1e-5
MIX_HALF = N_HEADS * HEAD_DIM

LANES = 128
SUBLANES = 8
BF16_ROWS = 16
VMEM_LIMIT = 56 * 1024 * 1024

BLK = 128
QUERIES = 2 * BLK
PV_ROWS = KV_RANK + BF16_ROWS
NEG_BIG = -1e30
MASKED = -(2.0 ** 100)
F32_MAX = float(np.finfo(np.float32).max)
LOG2E = math.log2(math.e)
SEARCH_UNROLL = 3


def _cparams(sem):
    return pltpu.CompilerParams(dimension_semantics=sem, vmem_limit_bytes=VMEM_LIMIT)


def _dot(a, b):
    return jnp.dot(a, b, preferred_element_type=F32)


def _dot_nt(a, b):
    return lax.dot_general(a, b, (((1,), (1,)), ((), ())), preferred_element_type=F32)


def _layer_norm_rows(y, g, b):
    mu = jnp.mean(y, axis=-1, keepdims=True)
    yc = y - mu
    var = jnp.mean(yc * yc, axis=-1, keepdims=True)
    return yc * lax.rsqrt(var + LN_EPS) * g + b


def _tree(op, x):
    while x.shape[0] > 1:
        half = x.shape[0] // 2
        x = op(x[:half], x[half:])
    return x[0]


def _bf16_pieces(v):
    hi = v.astype(BF16)
    r1 = v - hi.astype(F32)
    mid = r1.astype(BF16)
    lo = (r1 - mid.astype(F32)).astype(BF16)
    return jnp.concatenate([hi, mid, lo], axis=1)


def _even_proj_kernel(x_ref, wb_ref, ws_ref, hb_ref, hs_ref):
    x = x_ref[...].astype(BF16)
    hb_ref[...] = _dot(x, wb_ref[...]).astype(hb_ref.dtype)
    hs_ref[...] = _dot(x, ws_ref[...])


def _even_proj(xb, w_big, w_small, tm):
    n, k = xb.shape
    mb = w_big.shape[1]
    ms = w_small.shape[1]
    return pl.pallas_call(
        _even_proj_kernel,
        grid=(n // tm,),
        in_specs=[pl.BlockSpec((tm, k), lambda i: (i, 0)),
                  pl.BlockSpec((k, mb), lambda i: (0, 0)),
                  pl.BlockSpec((k, ms), lambda i: (0, 0))],
        out_specs=[pl.BlockSpec((tm, mb), lambda i: (i, 0)),
                   pl.BlockSpec((tm, ms), lambda i: (i, 0))],
        out_shape=[jax.ShapeDtypeStruct((n, mb), BF16),
                   jax.ShapeDtypeStruct((n, ms), F32)],
        compiler_params=_cparams(("parallel",)),
        name="even_proj",
    )(xb, w_big, w_small)


def _cumf_kernel(f_ref, b_ref, o_ref):
    seq = f_ref.shape[-1]
    row = lax.broadcasted_iota(jnp.int32, (LANES, LANES), 0)
    col = lax.broadcasted_iota(jnp.int32, (LANES, LANES), 1)
    upper = jnp.where(row <= col, 1.0, 0.0).astype(F32)
    carry = jnp.zeros((f_ref.shape[0], 1), F32)
    for c in range(seq // LANES):
        z = f_ref[:, c * LANES:(c + 1) * LANES] + b_ref[...]
        lf = jnp.minimum(z, 0.0) - jnp.log1p(jnp.exp(-jnp.abs(z)))
        cs = jnp.dot(lf, upper, preferred_element_type=F32,
                     precision=lax.Precision.HIGHEST) + carry
        o_ref[:, c * LANES:(c + 1) * LANES] = cs
        carry = cs[:, LANES - 1:LANES]


def _cumf(f_t, b_f):
    bsz, nh, seq = f_t.shape
    return pl.pallas_call(
        _cumf_kernel,
        grid=(bsz,),
        in_specs=[pl.BlockSpec((None, nh, seq), lambda b: (b, 0, 0)),
                  pl.BlockSpec((nh, 1), lambda b: (0, 0))],
        out_specs=pl.BlockSpec((None, nh, seq), lambda b: (b, 0, 0)),
        out_shape=jax.ShapeDtypeStruct((bsz, nh, seq), F32),
        compiler_params=_cparams(("parallel",)),
        name="fox_cumf",
    )(f_t, b_f)


def _attn_kernel(qa_ref, qi_ref, qb_ref, wt_ref, fq_ref, kb_ref, vb_ref, ckv_ref, kidx_ref, fk_ref,
                 wuk_ref, wuv_ref, bias_ref, bfar_ref, pq_ref, cq_ref, pk_ref, ck_ref,
                 o_ref,
                 score_ref, lhsd_ref, ct_ref, kaug_ref, vt_ref, qsel_ref, qaug_ref, qim_ref,
                 m_ref, acc_ref, s_ref, tie_ref, *, k_sel):
    T = BLK
    Q = QUERIES
    H = N_HEADS
    NP = H // 2
    NT = 2 * NP
    TW = 2 * T
    g = pl.program_id(1)
    i0 = 2 * g
    nblk = ct_ref.shape[0]
    lane = lax.broadcasted_iota(jnp.int32, (T, LANES), 1)
    lo_half = lane < HEAD_DIM
    ones_rows = jnp.ones((BF16_ROWS, T), BF16)
    causal_qq = (lax.broadcasted_iota(jnp.int32, (Q, Q), 0) <= lax.broadcasted_iota(jnp.int32, (Q, Q), 1))

    @pl.when(g == 0)
    def _per_batch():
        def prep(j, c):
            r = pl.ds(pl.multiple_of(j * T, T), T)
            cb = ckv_ref[r, :]
            lhsd_ref[0, r, 0:KV_RANK] = cb
            lhsd_ref[1, r, 0:KV_RANK] = cb
            ct_ref[j, 0:KV_RANK, :] = cb.astype(F32).T.astype(BF16)
            ct_ref[j, KV_RANK:PV_ROWS, :] = ones_rows
            ext = (_dot(_bf16_pieces(fk_ref[r, :] * LOG2E), pk_ref[...]) + ck_ref[...]).astype(BF16)
            for p in range(NP):
                cols = slice(p * LANES, (p + 1) * LANES)
                kaug_ref[p, r, 0:LANES] = kb_ref[r, cols]
                kaug_ref[p, r, LANES:2 * LANES] = ext[:, cols]
                vt_ref[p, j, 0:LANES, :] = vb_ref[r, cols].astype(F32).T.astype(BF16)
                vt_ref[p, j, LANES:PV_ROWS, :] = ones_rows
            return c
        lax.fori_loop(0, nblk, prep, 0)

    w_rows = wt_ref[...] * (H ** -0.5)
    ext_q = (_dot(_bf16_pieces(fq_ref[...] * LOG2E), pq_ref[...]) + cq_ref[...]).astype(BF16)
    eye = jnp.where(lax.broadcasted_iota(jnp.int32, (T, T), 0) == lax.broadcasted_iota(jnp.int32, (T, T), 1),
                    1.0, 0.0).astype(BF16)
    for tt in range(NT):
        hf, p = divmod(tt, NP)
        qrows = slice(hf * T, (hf + 1) * T)
        cols = slice(p * LANES, (p + 1) * LANES)
        qa_pair, qi_pair, qb_pair = qa_ref[qrows, cols], qi_ref[qrows, cols], qb_ref[qrows, cols]
        for e in range(2):
            h = 2 * p + e
            rows = slice(e * T, (e + 1) * T)
            mine = lo_half if e == 0 else jnp.logical_not(lo_half)
            qsel_ref[tt, rows, 0:KV_RANK] = _dot(qa_pair, wuk_ref[h]).astype(BF16)
            qsel_ref[tt, rows, KV_RANK:2 * KV_RANK] = eye
            qaug_ref[tt, rows, 0:LANES] = jnp.where(mine, qb_pair, 0)
            qaug_ref[tt, rows, LANES:2 * LANES] = ext_q[qrows, h * LANES:(h + 1) * LANES]
            qim_ref[tt, rows, :] = jnp.where(mine, qi_pair, 0)

    def reset_state():
        m_ref[...] = jnp.full(m_ref.shape, NEG_BIG, F32)
        acc_ref[...] = jnp.zeros(acc_ref.shape, F32)

    def tile_cols(tt):
        return slice(tt * TW, (tt + 1) * TW)

    def online_update(tt, s, pv_lhs):
        cols = tile_cols(tt)
        m_old = m_ref[:, cols]
        s_max = _tree(jnp.maximum, s.reshape(-1, SUBLANES, TW))
        m_new = jnp.maximum(m_old, jnp.max(s_max, axis=0, keepdims=True))
        alpha = jnp.exp2(m_old - m_new)
        pr = jnp.exp2((s - m_new[0:1, :]).astype(BF16))
        acc_ref[:, cols] = acc_ref[:, cols] * alpha[0:1, :] + _dot(pv_lhs, pr)
        m_ref[:, cols] = m_new

    def key_rows(j, nblocks):
        return pl.ds(pl.multiple_of(j * T, T), nblocks * T)

    def slot_rows(nblocks):
        return slice(0, nblocks * T)

    def blocks_t(load, j, nblocks):
        return load(j) if nblocks == 1 else jnp.concatenate([load(j + b) for b in range(nblocks)], axis=1)

    def staged_steps(j0, n_steps, logits, absorb):
        logits(j0, 2, 0)
        for k in range(n_steps):
            if k + 1 < n_steps:
                logits(j0 + 2 * (k + 1), 2, (k + 1) % 2)
            absorb(j0 + 2 * k, 2, k % 2)

    def for_far_blocks(n_far, logits, absorb):
        n_oct = lax.shift_right_logical(n_far, 3)

        def oct_body(jj, c):
            staged_steps(8 * jj, 4, logits, absorb)
            return c
        lax.fori_loop(0, n_oct, oct_body, 0)

        @pl.when((n_far & 4) != 0)
        def _quad():
            staged_steps(8 * n_oct, 2, logits, absorb)

        @pl.when((n_far & 2) != 0)
        def _pair():
            staged_steps(8 * n_oct + (n_far & 4), 1, logits, absorb)

        @pl.when((n_far & 1) != 0)
        def _single():
            logits(n_far - 1, 1, 0)
            absorb(n_far - 1, 1, 0)

    def idx_scores(j, nblocks):
        k_blk = kidx_ref[key_rows(j, nblocks), :]
        acc = [jnp.zeros((nblocks * T, T), F32), jnp.zeros((nblocks * T, T), F32)]
        for tt in range(NT):
            hf, p = divmod(tt, NP)
            s = _dot_nt(k_blk, qim_ref[tt])
            for e in range(2):
                h = 2 * p + e
                w_h = w_rows[h:h + 1, hf * T:(hf + 1) * T]
                acc[hf] = acc[hf] + w_h * jnp.maximum(s[:, e * T:(e + 1) * T], 0.0)
        return jnp.concatenate(acc, axis=1)

    def fox_logits(j, nblocks, slot, causal=False):
        for tt in range(NT):
            hf, p = divmod(tt, NP)
            s = _dot_nt(kaug_ref[p, key_rows(j, nblocks), :], qaug_ref[tt])
            if causal:
                ok = causal_qq[:, hf * T:(hf + 1) * T]
                s = jnp.where(jnp.concatenate([ok, ok], axis=1), s, MASKED)
            s_ref[slot, slot_rows(nblocks), tile_cols(tt)] = s

    def fox_absorb(j, nblocks, slot):
        for tt in range(NT):
            p = tt % NP
            online_update(tt, s_ref[slot, slot_rows(nblocks), tile_cols(tt)],
                          blocks_t(lambda jb: vt_ref[p, jb], j, nblocks))

    def idx_fox_logits(j, nblocks, slot):
        score_ref[pl.ds(j, nblocks)] = idx_scores(j, nblocks).reshape(nblocks, T, Q)
        fox_logits(j, nblocks, slot)

    reset_state()
    for_far_blocks(jnp.maximum(i0 - 2, 0), idx_fox_logits, fox_absorb)

    def diag_logits(slot):
        score_ref[pl.ds(i0, 2)] = jnp.where(causal_qq, idx_scores(i0, 2), -jnp.inf).reshape(2, T, Q)
        fox_logits(i0, 2, slot, causal=True)

    @pl.when(g >= 1)
    def _idx_fox_tail():
        idx_fox_logits(i0 - 2, 2, 0)
        diag_logits(1)
        fox_absorb(i0 - 2, 2, 0)
        fox_absorb(i0, 2, 1)

    @pl.when(g == 0)
    def _idx_fox_first():
        diag_logits(0)
        fox_absorb(0, 2, 0)

    row_lo = lax.broadcasted_iota(jnp.int32, (LANES, T), 0) < HEAD_DIM
    for tt in range(NT):
        hf, p = divmod(tt, NP)
        c0 = slice(tt * TW, tt * TW + T)
        c1 = slice(tt * TW + T, (tt + 1) * TW)
        o_t = jnp.where(row_lo, acc_ref[0:LANES, c0] / acc_ref[LANES:LANES + 1, c0],
                        acc_ref[0:LANES, c1] / acc_ref[LANES:LANES + 1, c1])
        o_ref[hf * T:(hf + 1) * T, MIX_HALF + p * LANES:MIX_HALF + (p + 1) * LANES] = (
            o_t.T.astype(o_ref.dtype))

    kf = float(k_sel)
    t_pos = (g * Q + lax.broadcasted_iota(jnp.int32, (SUBLANES, Q), 1)).astype(F32)
    searching = (t_pos + 1.0) > kf
    G2 = 2 * T // SUBLANES
    n_pair = g + 1
    key_in_pair = (lax.broadcasted_iota(jnp.int32, (G2, SUBLANES, Q), 0) * SUBLANES
                   + lax.broadcasted_iota(jnp.int32, (G2, SUBLANES, Q), 1))

    def rep(v):
        return jnp.broadcast_to(v, (SUBLANES, Q))

    def score_pair(jj):
        return score_ref[pl.ds(2 * jj, 2)].reshape(G2, SUBLANES, Q)

    def count(*pred_fns, load=score_pair):
        def body(jj, cs):
            s = load(jj)
            return tuple(c + _tree(jnp.add, jnp.where(f(s, jj), 1.0, 0.0)) for c, f in zip(cs, pred_fns))
        zero = jnp.zeros((SUBLANES, Q), F32)
        cs = lax.fori_loop(0, n_pair, body, (zero,) * len(pred_fns))
        return tuple(rep(jnp.sum(c, axis=0, keepdims=True)) for c in cs)

    def count_ge(thr):
        return count(lambda s, jj: s >= thr[None])[0]

    def select_threshold():
        def first_body(jj, carry):
            mn, mx, ge0, gt0 = carry
            s = score_pair(jj)
            return (jnp.minimum(mn, _tree(jnp.minimum, jnp.where(s == -jnp.inf, jnp.inf, s))),
                    jnp.maximum(mx, _tree(jnp.maximum, s)),
                    ge0 + _tree(jnp.add, jnp.where(s >= 0.0, 1.0, 0.0)),
                    gt0 + _tree(jnp.add, jnp.where(s > 0.0, 1.0, 0.0)))

        zero = jnp.zeros((SUBLANES, Q), F32)
        mn, mx, c_ge0, c_gt0 = lax.fori_loop(0, n_pair, first_body, (zero + jnp.inf, zero - jnp.inf, zero, zero))
        mn = rep(jnp.min(mn, axis=0, keepdims=True))
        mx = rep(jnp.max(mx, axis=0, keepdims=True))
        c_ge0 = rep(jnp.sum(c_ge0, axis=0, keepdims=True))
        c_gt0 = rep(jnp.sum(c_gt0, axis=0, keepdims=True))
        above_mx = mx + jnp.maximum(jnp.abs(mx) * 2.0 ** -20, 1e-30)
        thr_ge0 = c_ge0 >= kf
        thr_gt0 = c_gt0 >= kf
        lo0 = jnp.where(thr_ge0, 0.0, mn)
        cnt_lo0 = jnp.where(thr_ge0, c_ge0, t_pos + 1.0)
        hi0 = jnp.where(thr_gt0, above_mx, 0.0)
        cnt_hi0 = jnp.where(thr_gt0, 0.0, jnp.where(thr_ge0, c_gt0, c_ge0))
        undecided = thr_gt0 | jnp.logical_not(thr_ge0)
        active0 = jnp.where(searching & undecided & (cnt_lo0 > kf), 1.0, 0.0)

        def cond(st):
            return jnp.logical_and(st[5] > 0.0, st[6] < 2200)

        def bisect(st):
            lo, hi, cnt_lo, cnt_hi, active = st
            mid = 0.5 * lo + 0.5 * hi
            adjacent = (mid <= lo) | (mid >= hi)
            c = count_ge(mid)
            act = active > 0.0
            go_lo = act & jnp.logical_not(adjacent) & (c >= kf)
            go_hi = act & jnp.logical_not(adjacent) & (c < kf)
            lo = jnp.where(go_lo, mid, lo)
            cnt_lo = jnp.where(go_lo, c, cnt_lo)
            hi = jnp.where(go_hi, mid, hi)
            cnt_hi = jnp.where(go_hi, c, cnt_hi)
            active = jnp.where(act & jnp.logical_not(adjacent) & (cnt_lo > kf), 1.0, 0.0)
            return lo, hi, cnt_lo, cnt_hi, active

        def body(st):
            inner = st[:5]
            for _ in range(SEARCH_UNROLL):
                inner = bisect(inner)
            return inner + (jnp.max(inner[4]), st[6] + SEARCH_UNROLL)

        lo, hi, cnt_lo, cnt_hi, _, _, _ = lax.while_loop(
            cond, body, (lo0, hi0, cnt_lo0, cnt_hi0, active0, jnp.max(active0), 0))
        thr = jnp.where(searching, lo, -F32_MAX)

        tied = searching & (cnt_lo > kf)
        need = kf - cnt_hi

        @pl.when(jnp.max(jnp.where(tied, 1.0, 0.0)) > 0.0)
        def _break_ties():
            def tie_pair(jj):
                return tie_ref[pl.ds(2 * jj, 2)].reshape(G2, SUBLANES, Q)

            def mark_body(jj, c):
                key_index = (2 * jj * T + key_in_pair).astype(F32)
                tie_ref[pl.ds(2 * jj, 2)] = (
                    jnp.where(score_pair(jj) == thr[None], key_index, jnp.inf).reshape(2, T, Q))
                return c
            lax.fori_loop(0, n_pair, mark_body, 0)

            def count_tied_upto(jcut):
                return count(lambda e, jj: e <= jcut[None], load=tie_pair)[0]

            def bs_body(_, st):
                jl, jh = st
                jm = jnp.floor(0.5 * (jl + jh))
                ok = count_tied_upto(jm) >= need
                return jnp.where(ok, jl, jm), jnp.where(ok, jm, jh)

            n_keys = (2 * n_pair * T).astype(F32)
            jl0 = jnp.full((SUBLANES, Q), -1.0, F32)
            jh0 = jnp.zeros((SUBLANES, Q), F32) + (n_keys - 1.0)
            n_steps = int(math.ceil(math.log2(nblk * T))) + 1
            _, jcut = lax.fori_loop(0, n_steps, bs_body, (jl0, jh0))

            def drop_body(jj, c):
                e = tie_pair(jj)
                drop = tied[None] & (e > jcut[None]) & (e < jnp.inf)
                score_ref[pl.ds(2 * jj, 2)] = jnp.where(drop, -jnp.inf, score_pair(jj)).reshape(2, T, Q)
                return c
            lax.fori_loop(0, n_pair, drop_body, 0)

        return thr

    thr = lax.cond((i0 + 2) * T > k_sel, select_threshold,
                   lambda: jnp.full((SUBLANES, Q), -F32_MAX, F32))

    def mask_body(jj, c):
        add = jnp.where(score_pair(jj) >= thr[None], 0.0, MASKED).reshape(2 * T, Q).astype(BF16)
        for hf in range(2):
            lhsd_ref[hf, key_rows(2 * jj, 2), KV_RANK:2 * KV_RANK] = add[:, hf * T:(hf + 1) * T]
        return c

    lax.fori_loop(0, n_pair, mask_body, 0)

    def dsa_logits(j, nblocks, slot, bias=None):
        for tt in range(NT):
            hf, p = divmod(tt, NP)
            s = _dot_nt(lhsd_ref[hf, key_rows(j, nblocks), :], qsel_ref[tt])
            if bias is not None:
                s = s + bias(hf, slice(p * TW, (p + 1) * TW))
            s_ref[slot, slot_rows(nblocks), tile_cols(tt)] = s

    def dsa_absorb(j, nblocks, slot):
        pv_lhs = blocks_t(lambda jb: ct_ref[jb], j, nblocks)
        for tt in range(NT):
            online_update(tt, s_ref[slot, slot_rows(nblocks), tile_cols(tt)], pv_lhs)

    def bias_before(hf, cols):
        return jnp.concatenate([bias_ref[2, :, cols], bias_ref[1 + hf, :, cols]], axis=0)

    def bias_diag(hf, cols):
        first = bias_ref[hf, :, cols]
        return jnp.concatenate([first, bias_ref[0, :, cols]], axis=0)

    reset_state()
    for_far_blocks(jnp.maximum(i0 - 2, 0), dsa_logits, dsa_absorb)
    m_ref[...] = m_ref[...] + bfar_ref[...]

    @pl.when(g >= 1)
    def _dsa_tail():
        dsa_logits(i0 - 2, 2, 0, bias_before)
        dsa_logits(i0, 2, 1, bias_diag)
        dsa_absorb(i0 - 2, 2, 0)
        dsa_absorb(i0, 2, 1)

    @pl.when(g == 0)
    def _dsa_first():
        dsa_logits(0, 2, 0, bias_diag)
        dsa_absorb(0, 2, 0)

    for tt in range(NT):
        hf, p = divmod(tt, NP)
        o_pair = jnp.zeros((T, LANES), F32)
        for e in range(2):
            cols = slice(tt * TW + e * T, tt * TW + (e + 1) * T)
            o_t = acc_ref[0:KV_RANK, cols] / acc_ref[KV_RANK:KV_RANK + 1, cols]
            o_pair = o_pair + _dot(o_t.T.astype(BF16), wuv_ref[2 * p + e])
        o_ref[hf * T:(hf + 1) * T, p * LANES:(p + 1) * LANES] = o_pair.astype(o_ref.dtype)


_EV_QA, _EV_QI, _EV_QB, _EV_KB, _EV_VB = 0, 1, 2, 3, 4
_EV_CKV, _EV_KIDX = 20, 21


def _attention(hb, w_t, f_rows, wuk_pad, wuv_pad, bias_tiles, bias_far, place, k_sel):
    bsz, seq, _ = hb.shape
    T, Q, H = BLK, QUERIES, N_HEADS
    nblk = seq // T
    lanes = 2 * H * T
    pq, cq, pk, ck = place
    wide = lambda c: pl.BlockSpec((None, Q, MIX_HALF), lambda b, i, c=c: (b, i, c))
    once = dict(pipeline_mode=pl.Buffered(1))
    full_wide = lambda c: pl.BlockSpec((None, seq, MIX_HALF), lambda b, i, c=c: (b, 0, c), **once)
    full_narrow = lambda c: pl.BlockSpec((None, seq, LANES), lambda b, i, c=c: (b, 0, c), **once)
    const = lambda a: pl.BlockSpec(a.shape, lambda b, i, nd=a.ndim: (0,) * nd, **once)
    return pl.pallas_call(
        functools.partial(_attn_kernel, k_sel=k_sel),
        grid=(bsz, seq // Q),
        in_specs=[wide(_EV_QA), wide(_EV_QI), wide(_EV_QB),
                  pl.BlockSpec((None, H, Q), lambda b, i: (b, 0, i)),
                  pl.BlockSpec((None, Q, LANES), lambda b, i: (b, i, 0)),
                  full_wide(_EV_KB), full_wide(_EV_VB),
                  full_narrow(_EV_CKV), full_narrow(_EV_KIDX),
                  pl.BlockSpec((None, seq, LANES), lambda b, i: (b, 0, 0), **once),
                  const(wuk_pad), const(wuv_pad), const(bias_tiles), const(bias_far),
                  const(pq), const(cq), const(pk), const(ck)],
        out_specs=pl.BlockSpec((None, Q, 2 * MIX_HALF), lambda b, i: (b, i, 0)),
        out_shape=jax.ShapeDtypeStruct((bsz, seq, 2 * MIX_HALF), BF16),
        scratch_shapes=[pltpu.VMEM((nblk, T, Q), F32),
                        pltpu.VMEM((2, seq, 2 * KV_RANK), BF16),
                        pltpu.VMEM((nblk, PV_ROWS, T), BF16),
                        pltpu.VMEM((H // 2, seq, 2 * LANES), BF16),
                        pltpu.VMEM((H // 2, nblk, PV_ROWS, T), BF16),
                        pltpu.VMEM((H, 2 * T, 2 * KV_RANK), BF16),
                        pltpu.VMEM((H, 2 * T, 2 * LANES), BF16),
                        pltpu.VMEM((H, 2 * T, LANES), BF16),
                        pltpu.VMEM((SUBLANES, lanes), F32),
                        pltpu.VMEM((PV_ROWS, lanes), F32),
                        pltpu.VMEM((2, 2 * T, lanes), F32),
                        pltpu.VMEM((nblk, T, Q), F32)],
        compiler_params=_cparams(("parallel", "arbitrary")),
        name="dsa_fox_attention",
    )(hb, hb, hb, w_t, f_rows, hb, hb, hb, hb, f_rows, wuk_pad, wuv_pad, bias_tiles, bias_far,
      pq, cq, pk, ck)


def _proj_ln_kernel(a_ref, w_ref, x_ref, g_ref, b_ref, xo_ref, xb_ref, *, alpha, sub):
    for c in range(a_ref.shape[0] // sub):
        rows = slice(c * sub, (c + 1) * sub)
        y = alpha * x_ref[rows, :] + _dot(a_ref[rows, :], w_ref[...])
        out = _layer_norm_rows(y, g_ref[...], b_ref[...])
        xo_ref[rows, :] = out
        xb_ref[rows, :] = out.astype(BF16)


def _proj_ln(a, w, x, g, b, alpha, tm):
    n, k = a.shape
    d = w.shape[1]
    return pl.pallas_call(
        functools.partial(_proj_ln_kernel, alpha=alpha, sub=min(128, tm)),
        grid=(n // tm,),
        in_specs=[pl.BlockSpec((tm, k), lambda i: (i, 0)),
                  pl.BlockSpec((k, d), lambda i: (0, 0)),
                  pl.BlockSpec((tm, d), lambda i: (i, 0)),
                  pl.BlockSpec((1, d), lambda i: (0, 0)),
                  pl.BlockSpec((1, d), lambda i: (0, 0))],
        out_specs=[pl.BlockSpec((tm, d), lambda i: (i, 0)),
                   pl.BlockSpec((tm, d), lambda i: (i, 0))],
        out_shape=[jax.ShapeDtypeStruct((n, d), F32),
                   jax.ShapeDtypeStruct((n, d), BF16)],
        compiler_params=_cparams(("parallel",)),
        name="proj_residual_ln",
    )(a, w, x, g, b)


def _causal_conv3(h, tail, cw):
    w0, w1, w2 = cw[0:1, :], cw[1:2, :], cw[2:3, :]
    body = w0 * pltpu.roll(h, 2, axis=0) + w1 * pltpu.roll(h, 1, axis=0) + w2 * h
    head = h[0:SUBLANES, :]
    r = lax.broadcasted_iota(jnp.int32, head.shape, 0)
    hm1 = jnp.where(r < 1, pltpu.roll(tail, 1, axis=0), pltpu.roll(head, 1, axis=0))
    hm2 = jnp.where(r < 2, pltpu.roll(tail, 2, axis=0), pltpu.roll(head, 2, axis=0))
    out_head = w0 * hm2 + w1 * hm1 + w2 * head
    return jnp.concatenate([out_head, body[SUBLANES:, :]], axis=0)


def _ffn_up_kernel(x_ref, wg_ref, wv_ref, cg_ref, cv_ref, o_ref, tg_ref, tv_ref, *h_refs,
                   tiles_per_seq, sub):
    n_chunks = x_ref.shape[0] // sub
    hg_refs, hv_refs = h_refs[:n_chunks], h_refs[n_chunks:]
    first = (pl.program_id(1) % tiles_per_seq) == 0
    body_rows = slice(SUBLANES, SUBLANES + sub)

    def matmuls(c):
        x = x_ref[c * sub:(c + 1) * sub, :]
        hg_refs[c][body_rows, :] = _dot(x, wg_ref[...])
        hv_refs[c][body_rows, :] = _dot(x, wv_ref[...])

    def conv(h_ref, cw):
        return (cw[0:1, :] * h_ref[SUBLANES - 2:SUBLANES - 2 + sub, :]
                + cw[1:2, :] * h_ref[SUBLANES - 1:SUBLANES - 1 + sub, :]
                + cw[2:3, :] * h_ref[body_rows, :])

    def gate(c):
        if c == 0:
            hg_refs[0][0:SUBLANES, :] = jnp.where(first, 0.0, tg_ref[...])
            hv_refs[0][0:SUBLANES, :] = jnp.where(first, 0.0, tv_ref[...])
        else:
            hg_refs[c][0:SUBLANES, :] = hg_refs[c - 1][sub:sub + SUBLANES, :]
            hv_refs[c][0:SUBLANES, :] = hv_refs[c - 1][sub:sub + SUBLANES, :]
        g = conv(hg_refs[c], cg_ref[...])
        v = conv(hv_refs[c], cv_ref[...])
        half_g = 0.5 * g
        o_ref[c * sub:(c + 1) * sub, :] = ((half_g + half_g * jnp.tanh(half_g)) * v).astype(o_ref.dtype)

    matmuls(0)
    for c in range(n_chunks):
        if c + 1 < n_chunks:
            matmuls(c + 1)
        gate(c)
    tg_ref[...] = hg_refs[n_chunks - 1][sub:sub + SUBLANES, :]
    tv_ref[...] = hv_refs[n_chunks - 1][sub:sub + SUBLANES, :]


def _ffn_up(xb, w_up, conv_w, seq, tm, tn, sub):
    n, k = xb.shape
    dff = w_up.shape[1] // 2
    ncol = dff // tn
    return pl.pallas_call(
        functools.partial(_ffn_up_kernel, tiles_per_seq=seq // tm, sub=sub),
        grid=(ncol, n // tm),
        in_specs=[pl.BlockSpec((tm, k), lambda j, i: (i, 0)),
                  pl.BlockSpec((k, tn), lambda j, i: (0, j)),
                  pl.BlockSpec((k, tn), lambda j, i, ncol=ncol: (0, j + ncol)),
                  pl.BlockSpec((CONV_W, tn), lambda j, i: (0, j)),
                  pl.BlockSpec((CONV_W, tn), lambda j, i, ncol=ncol: (0, j + ncol))],
        out_specs=pl.BlockSpec((tm, tn), lambda j, i: (i, j)),
        out_shape=jax.ShapeDtypeStruct((n, dff), BF16),
        scratch_shapes=([pltpu.VMEM((SUBLANES, tn), F32)] * 2
                        + [pltpu.VMEM((sub + SUBLANES, tn), F32)] * (2 * (tm // sub))),
        compiler_params=_cparams(("arbitrary", "arbitrary")),
        name="ffn_up_conv_gate",
    )(xb, w_up, w_up, conv_w, conv_w)


def _gelu_tanh(x):
    return 0.5 * x * (1.0 + jnp.tanh(math.sqrt(2.0 / math.pi) * (x + 0.044715 * (x * x * x))))


def _group_mean(v, avg):
    hi = v.astype(BF16)
    lo = (v - hi.astype(F32)).astype(BF16)
    return _dot(hi, avg) + _dot(lo, avg)


def _odd_mixer_kernel(x_ref, w_ref, sg_ref, sw_ref, sb_ref, cw_ref, wo_ref, xf_ref, g_ref, b_ref,
                      xo_ref, xb_ref, o_ref, tail_ref, *, tiles_per_seq, alpha):
    tm = x_ref.shape[0]
    W = MIX_HALF
    first = (pl.program_id(0) % tiles_per_seq) == 0
    x = x_ref[...]
    lane = lax.broadcasted_iota(jnp.int32, (CHUNK, LANES), 1)
    lo_half = lane < HEAD_DIM
    r = lax.broadcasted_iota(jnp.int32, (W, W), 0) // HEAD_DIM
    c = lax.broadcasted_iota(jnp.int32, (W, W), 1) // HEAD_DIM
    avg = jnp.where(r == c, 1.0 / HEAD_DIM, 0.0).astype(BF16)
    tri = (lax.broadcasted_iota(jnp.int32, (CHUNK, CHUNK), 1)
           <= lax.broadcasted_iota(jnp.int32, (CHUNK, CHUNK), 0))

    u = _gelu_tanh(_dot(x, w_ref[:, 0:W]))
    v = _gelu_tanh(_dot(x, w_ref[:, W:2 * W]))
    vc = v - _group_mean(v, avg)
    var = _group_mean(vc * vc, avg)
    vn = (vc * lax.rsqrt(var + LN_EPS) * sg_ref[...]).astype(BF16)
    for n in range(tm // CHUNK):
        rows = slice(n * CHUNK, (n + 1) * CHUNK)
        for p in range(N_HEADS // 2):
            cols = slice(p * LANES, (p + 1) * LANES)
            vp = vn[rows, cols]
            w_e = jnp.where(tri, sw_ref[2 * p], 0.0).astype(BF16)
            w_o = jnp.where(tri, sw_ref[2 * p + 1], 0.0).astype(BF16)
            mix = (_dot(w_e, jnp.where(lo_half, vp, 0)) + _dot(w_o, jnp.where(lo_half, 0, vp))
                   + sb_ref[:, cols])
            o_ref[rows, cols] = (u[rows, cols] * mix).astype(o_ref.dtype)

    g_b = _dot(x, w_ref[:, 2 * W:3 * W])
    y = _dot(x, w_ref[:, 3 * W:4 * W]) * _dot(x, w_ref[:, 4 * W:5 * W])
    conv = _causal_conv3(y, jnp.where(first, 0.0, tail_ref[...]), cw_ref[...])
    tail_ref[...] = y[tm - SUBLANES:tm, :]
    o_ref[:, W:2 * W] = (g_b * conv).astype(o_ref.dtype)

    out = _layer_norm_rows(alpha * xf_ref[...] + _dot(o_ref[...], wo_ref[...]), g_ref[...], b_ref[...])
    xo_ref[...] = out
    xb_ref[...] = out.astype(BF16)


def _odd_mixer(xb, w_in, sgu_g, sgu_w, sgu_b_tile, conv_w, w_o, xf, g, b, alpha, seq, tm):
    n, k = xb.shape
    W = MIX_HALF
    d = w_o.shape[1]
    const2 = lambda shape: pl.BlockSpec(shape, lambda i: (0, 0))
    row_tile = lambda width: pl.BlockSpec((tm, width), lambda i: (i, 0))
    return pl.pallas_call(
        functools.partial(_odd_mixer_kernel, tiles_per_seq=seq // tm, alpha=alpha),
        grid=(n // tm,),
        in_specs=[row_tile(k), const2((k, 5 * W)), const2((1, W)),
                  pl.BlockSpec((N_HEADS, CHUNK, CHUNK), lambda i: (0, 0, 0)),
                  const2((CHUNK, W)), const2((CONV_W, W)),
                  const2((2 * W, d)), row_tile(d), const2((1, d)), const2((1, d))],
        out_specs=[row_tile(d), row_tile(d)],
        out_shape=[jax.ShapeDtypeStruct((n, d), F32), jax.ShapeDtypeStruct((n, d), BF16)],
        scratch_shapes=[pltpu.VMEM((tm, 2 * W), BF16),
                        pltpu.VMEM((SUBLANES, W), F32)],
        compiler_params=_cparams(("arbitrary",)),
        name="odd_mixer",
    )(xb, w_in, sgu_g, sgu_w, sgu_b_tile, conv_w, w_o, xf, g, b)


def _t5_bucket_np(dist):
    max_exact = N_BUCKETS // 2
    n = np.maximum(dist, 0)
    nf = np.maximum(n, 1).astype(np.float32)
    large = max_exact + (np.log(nf / max_exact) / np.float32(math.log(T5_MAX_DIST / max_exact))
                         * (N_BUCKETS - max_exact)).astype(np.int32)
    large = np.minimum(large, N_BUCKETS - 1)
    return np.where(n < max_exact, n, large)


def _bias_tables(rel_bias):
    T = BLK
    s = np.arange(T)[:, None]
    t = np.arange(T)[None, :]
    bucket = _t5_bucket_np(np.stack([t + d * T - s for d in range(3)]))
    far_bucket = N_BUCKETS - 1
    assert (bucket[2] == far_bucket).all()
    assert (_t5_bucket_np(np.arange(T + 1, 64 * T)) == far_bucket).all()
    rb = rel_bias.astype(F32) * LOG2E
    tiles = sum(jnp.where(jnp.asarray(bucket == b)[..., None], rb[b], 0.0) for b in range(N_BUCKETS))
    tiles = jnp.swapaxes(tiles, 2, 3).reshape(3, T, N_HEADS * T)
    far = jnp.tile(jnp.repeat(rb[far_bucket], T)[None, :], (1, 2))
    return tiles, far


def _placement_constants():
    H, L = N_HEADS, LANES
    pq = np.zeros((3 * L, H * L), np.float32)
    cq = np.zeros((1, H * L), np.float32)
    pk = np.zeros((3 * L, (H // 2) * L), np.float32)
    ck = np.zeros((1, (H // 2) * L), np.float32)
    for h in range(H):
        p, e = divmod(h, 2)
        for piece in range(3):
            pq[piece * L + h, h * L + 6 + piece] = 1.0
            cq[0, h * L + 3 * e + piece] = 1.0
            pk[piece * L + h, p * L + 3 * e + piece] = -1.0
            ck[0, p * L + 6 + piece] = 1.0
    return jnp.asarray(pq, BF16), jnp.asarray(cq), jnp.asarray(pk, BF16), jnp.asarray(ck)


def _even_weights(w_in, w_uk, w_uv):
    hd, H = HEAD_DIM, N_HEADS
    widths = (H * hd, KV_RANK, H * hd, hd, H, H * hd, H * hd, H * hd, H)
    offs = np.cumsum((0,) + widths)
    seg = lambda s: w_in[:, offs[s]:offs[s + 1]]
    q_a, c_kv, q_idx, k_idx, w_idx, q_b, k_b, v_b, f_in = [seg(s) for s in range(9)]
    scale = hd ** -0.5
    w_big = jnp.concatenate([q_a, q_idx * scale, q_b * (scale * LOG2E), k_b, v_b, c_kv,
                             k_idx, k_idx], axis=1).astype(BF16)
    pad = jnp.zeros((w_in.shape[0], LANES - 2 * H), w_in.dtype)
    w_small = jnp.concatenate([w_idx, f_in, pad], axis=1).astype(BF16)
    wuk_t = jnp.swapaxes(w_uk, 1, 2) * (scale * LOG2E)
    odd = (np.arange(H) % 2 == 1)[:, None, None]
    zk, zv = jnp.zeros_like(wuk_t), jnp.zeros_like(w_uv)
    wuk_pad = jnp.where(odd, jnp.concatenate([zk, wuk_t], axis=1), jnp.concatenate([wuk_t, zk], axis=1))
    wuv_pad = jnp.where(odd, jnp.concatenate([zv, w_uv], axis=2), jnp.concatenate([w_uv, zv], axis=2))
    return w_big, w_small, wuk_pad.astype(BF16), wuv_pad.astype(BF16)


def kernel(x, ln_g, ln_b, rel_bias, ev_w_in, ev_w_uk, ev_w_uv, ev_b_f, ev_w_o,
           od_w_in, od_sgu_g, od_sgu_w, od_sgu_b, od_conv_w, od_w_o,
           ffn_w_up, ffn_conv_w, ffn_w_down):
    bsz, seq, d = x.shape
    depth = ln_g.shape[0]
    n = bsz * seq
    alpha = (2.0 * depth) ** 0.25
    k_sel = min(TOPK_MAX, seq // 4)
    H = N_HEADS
    assert seq % QUERIES == 0 and seq % CHUNK == 0
    tm = min(512, seq)
    tm_ffn = min(1024, seq)

    bias_tiles, bias_far = _bias_tables(rel_bias)
    place = _placement_constants()
    xf = x.reshape(n, d)
    xb = xf
    for layer in range(depth):
        j = layer // 2
        g0, b0 = ln_g[layer, 0][None, :], ln_b[layer, 0][None, :]
        g1, b1 = ln_g[layer, 1][None, :], ln_b[layer, 1][None, :]
        if layer % 2 == 0:
            w_big, w_small, wuk_pad, wuv_pad = _even_weights(ev_w_in[j], ev_w_uk[j], ev_w_uv[j])
            hb, hs = _even_proj(xb, w_big, w_small, tm)
            hb = hb.reshape(bsz, seq, -1)
            hs_t = jnp.swapaxes(hs.reshape(bsz, seq, -1)[:, :, 0:2 * H], 1, 2)
            cum_f = _cumf(hs_t[:, H:2 * H], ev_b_f[j][:, None])
            f_rows = jnp.pad(jnp.swapaxes(cum_f, 1, 2), ((0, 0), (0, 0), (0, LANES - H)))
            mix = _attention(hb, hs_t[:, 0:H], f_rows, wuk_pad, wuv_pad, bias_tiles, bias_far,
                             place, k_sel)
            xf, xb = _proj_ln(mix.reshape(n, d), ev_w_o[j].astype(BF16), xf, g0, b0, alpha, tm_ffn)
        else:
            sgu_b_tile = jnp.repeat(jnp.swapaxes(od_sgu_b[j], 0, 1), HEAD_DIM, axis=1)
            xf, xb = _odd_mixer(xb, od_w_in[j].astype(BF16), od_sgu_g[j].reshape(1, -1), od_sgu_w[j],
                                sgu_b_tile, od_conv_w[j], od_w_o[j].astype(BF16), xf, g0, b0, alpha,
                                seq, min(256, seq))
        act = _ffn_up(xb, ffn_w_up[layer].astype(BF16), ffn_conv_w[layer], seq, tm_ffn, 256,
                      min(512, seq))
        xf, xb = _proj_ln(act, ffn_w_down[layer].astype(BF16), xf, g1, b1, alpha, tm_ffn)
    return xf.reshape(bsz, seq, d)
```

```python
import functools
import math

import numpy as np
import jax
import jax.numpy as jnp
from jax import lax
from jax.experimental import pallas as pl
from jax.experimental.pallas import tpu as pltpu

F32 = jnp.float32
BF16 = jnp.bfloat16

HEAD_DIM = 64
N_HEADS = 8
KV_RANK = 128
TOPK_MAX = 256
N_BUCKETS = 32
T5_MAX_DIST = 128
CHUNK = 128
CONV_W = 3
LN_EPS = 1e-5
MIX_HALF = N_HEADS * HEAD_DIM

LANES = 128
SUBLANES = 8
BF16_ROWS = 16
VMEM_LIMIT = 56 * 1024 * 1024

BLK = 128
QUERIES = 2 * BLK
PV_ROWS = KV_RANK + BF16_ROWS
NEG_BIG = -1e30
MASKED = -(2.0 ** 100)
F32_MAX = float(np.finfo(np.float32).max)
LOG2E = math.log2(math.e)
SEARCH_UNROLL = 3
FFN_TN = 256


def _cparams(sem):
    return pltpu.CompilerParams(dimension_semantics=sem, vmem_limit_bytes=VMEM_LIMIT)


def _dot(a, b):
    return jnp.dot(a, b, preferred_element_type=F32)


def _dot_nt(a, b):
    return lax.dot_general(a, b, (((1,), (1,)), ((), ())), preferred_element_type=F32)


def _layer_norm_rows(y, g, b):
    mu = jnp.mean(y, axis=-1, keepdims=True)
    yc = y - mu
    var = jnp.mean(yc * yc, axis=-1, keepdims=True)
    return yc * lax.rsqrt(var + LN_EPS) * g + b


def _tree(op, x):
    while x.shape[0] > 1:
        half = x.shape[0] // 2
        x = op(x[:half], x[half:])
    return x[0]


def _bf16_pieces(v):
    hi = v.astype(BF16)
    r1 = v - hi.astype(F32)
    mid = r1.astype(BF16)
    lo = (r1 - mid.astype(F32)).astype(BF16)
    return jnp.concatenate([hi, mid, lo], axis=1)


def _even_proj_kernel(x_ref, wb_ref, ws_ref, hb_ref, hs_ref):
    x = x_ref[...].astype(BF16)
    hb_ref[...] = _dot(x, wb_ref[...]).astype(hb_ref.dtype)
    hs_ref[...] = _dot(x, ws_ref[...])


def _even_proj(xb, w_big, w_small, tm):
    n, k = xb.shape
    mb = w_big.shape[1]
    ms = w_small.shape[1]
    return pl.pallas_call(
        _even_proj_kernel,
        grid=(n // tm,),
        in_specs=[pl.BlockSpec((tm, k), lambda i: (i, 0)),
                  pl.BlockSpec((k, mb), lambda i: (0, 0)),
                  pl.BlockSpec((k, ms), lambda i: (0, 0))],
        out_specs=[pl.BlockSpec((tm, mb), lambda i: (i, 0)),
                   pl.BlockSpec((tm, ms), lambda i: (i, 0))],
        out_shape=[jax.ShapeDtypeStruct((n, mb), BF16),
                   jax.ShapeDtypeStruct((n, ms), F32)],
        compiler_params=_cparams(("parallel",)),
        name="even_proj",
    )(xb, w_big, w_small)


def _cumf_kernel(f_ref, b_ref, o_ref):
    seq = f_ref.shape[-1]
    row = lax.broadcasted_iota(jnp.int32, (LANES, LANES), 0)
    col = lax.broadcasted_iota(jnp.int32, (LANES, LANES), 1)
    upper = jnp.where(row <= col, 1.0, 0.0).astype(F32)
    carry = jnp.zeros((f_ref.shape[0], 1), F32)
    for c in range(seq // LANES):
        z = f_ref[:, c * LANES:(c + 1) * LANES] + b_ref[...]
        lf = jnp.minimum(z, 0.0) - jnp.log1p(jnp.exp(-jnp.abs(z)))
        cs = jnp.dot(lf, upper, preferred_element_type=F32,
                     precision=lax.Precision.HIGHEST) + carry
        o_ref[:, c * LANES:(c + 1) * LANES] = cs
        carry = cs[:, LANES - 1:LANES]


def _cumf(f_t, b_f):
    bsz, nh, seq = f_t.shape
    return pl.pallas_call(
        _cumf_kernel,
        grid=(bsz,),
        in_specs=[pl.BlockSpec((None, nh, seq), lambda b: (b, 0, 0)),
                  pl.BlockSpec((nh, 1), lambda b: (0, 0))],
        out_specs=pl.BlockSpec((None, nh, seq), lambda b: (b, 0, 0)),
        out_shape=jax.ShapeDtypeStruct((bsz, nh, seq), F32),
        compiler_params=_cparams(("parallel",)),
        name="fox_cumf",
    )(f_t, b_f)


def _attn_kernel(qa_ref, qi_ref, qb_ref, wt_ref, fq_ref, kb_ref, vb_ref, ckv_ref, kidx_ref, fk_ref,
                 wuk_ref, wuv_ref, bias_ref, bfar_ref, pq_ref, cq_ref, pk_ref, ck_ref,
                 o_ref,
                 score_ref, lhsd_ref, ct_ref, kaug_ref, vt_ref, qsel_ref, qaug_ref, qim_ref,
                 m_ref, acc_ref, s_ref, tie_ref, *, k_sel):
    T = BLK
    Q = QUERIES
    H = N_HEADS
    NP = H // 2
    NT = 2 * NP
    TW = 2 * T
    g = pl.program_id(1)
    i0 = 2 * g
    nblk = ct_ref.shape[0]
    lane = lax.broadcasted_iota(jnp.int32, (T, LANES), 1)
    lo_half = lane < HEAD_DIM
    ones_rows = jnp.ones((BF16_ROWS, T), BF16)
    causal_qq = (lax.broadcasted_iota(jnp.int32, (Q, Q), 0) <= lax.broadcasted_iota(jnp.int32, (Q, Q), 1))

    @pl.when(g == 0)
    def _per_batch():
        def prep(j, c):
            r = pl.ds(pl.multiple_of(j * T, T), T)
            cb = ckv_ref[r, :]
            lhsd_ref[0, r, 0:KV_RANK] = cb
            lhsd_ref[1, r, 0:KV_RANK] = cb
            ct_ref[j, 0:KV_RANK, :] = cb.astype(F32).T.astype(BF16)
            ct_ref[j, KV_RANK:PV_ROWS, :] = ones_rows
            ext = (_dot(_bf16_pieces(fk_ref[r, :] * LOG2E), pk_ref[...]) + ck_ref[...]).astype(BF16)
            for p in range(NP):
                cols = slice(p * LANES, (p + 1) * LANES)
                kaug_ref[p, r, 0:LANES] = kb_ref[r, cols]
                kaug_ref[p, r, LANES:2 * LANES] = ext[:, cols]
                vt_ref[p, j, 0:LANES, :] = vb_ref[r, cols].astype(F32).T.astype(BF16)
                vt_ref[p, j, LANES:PV_ROWS, :] = ones_rows
            return c
        lax.fori_loop(0, nblk, prep, 0)

    w_rows = wt_ref[...] * (H ** -0.5)
    ext_q = (_dot(_bf16_pieces(fq_ref[...] * LOG2E), pq_ref[...]) + cq_ref[...]).astype(BF16)
    eye = jnp.where(lax.broadcasted_iota(jnp.int32, (T, T), 0) == lax.broadcasted_iota(jnp.int32, (T, T), 1),
                    1.0, 0.0).astype(BF16)
    for tt in range(NT):
        hf, p = divmod(tt, NP)
        qrows = slice(hf * T, (hf + 1) * T)
        cols = slice(p * LANES, (p + 1) * LANES)
        qa_pair, qi_pair, qb_pair = qa_ref[qrows, cols], qi_ref[qrows, cols], qb_ref[qrows, cols]
        for e in range(2):
            h = 2 * p + e
            rows = slice(e * T, (e + 1) * T)
            mine = lo_half if e == 0 else jnp.logical_not(lo_half)
            qsel_ref[tt, rows, 0:KV_RANK] = _dot(qa_pair, wuk_ref[h]).astype(BF16)
            qsel_ref[tt, rows, KV_RANK:2 * KV_RANK] = eye
            qaug_ref[tt, rows, 0:LANES] = jnp.where(mine, qb_pair, 0)
            qaug_ref[tt, rows, LANES:2 * LANES] = ext_q[qrows, h * LANES:(h + 1) * LANES]
            qim_ref[tt, rows, :] = jnp.where(mine, qi_pair, 0)

    def reset_state():
        m_ref[...] = jnp.full(m_ref.shape, NEG_BIG, F32)
        acc_ref[...] = jnp.zeros(acc_ref.shape, F32)

    def tile_cols(tt):
        return slice(tt * TW, (tt + 1) * TW)

    def online_update(tt, s, pv_lhs):
        cols = tile_cols(tt)
        m_old = m_ref[:, cols]
        s_max = _tree(jnp.maximum, s.reshape(-1, SUBLANES, TW))
        m_new = jnp.maximum(m_old, jnp.max(s_max, axis=0, keepdims=True))
        alpha = jnp.exp2(m_old - m_new)
        pr = jnp.exp2((s - m_new[0:1, :]).astype(BF16))
        acc_ref[:, cols] = acc_ref[:, cols] * alpha[0:1, :] + _dot(pv_lhs, pr)
        m_ref[:, cols] = m_new

    def key_rows(j, nblocks):
        return pl.ds(pl.multiple_of(j * T, T), nblocks * T)

    def slot_rows(nblocks):
        return slice(0, nblocks * T)

    def blocks_t(load, j, nblocks):
        return load(j) if nblocks == 1 else jnp.concatenate([load(j + b) for b in range(nblocks)], axis=1)

    def staged_steps(j0, n_steps, logits, absorb):
        logits(j0, 2, 0)
        for k in range(n_steps):
            if k + 1 < n_steps:
                logits(j0 + 2 * (k + 1), 2, (k + 1) % 2)
            absorb(j0 + 2 * k, 2, k % 2)

    def for_far_blocks(n_far, logits, absorb):
        n_oct = lax.shift_right_logical(n_far, 3)

        def oct_body(jj, c):
            staged_steps(8 * jj, 4, logits, absorb)
            return c
        lax.fori_loop(0, n_oct, oct_body, 0)

        @pl.when((n_far & 4) != 0)
        def _quad():
            staged_steps(8 * n_oct, 2, logits, absorb)

        @pl.when((n_far & 2) != 0)
        def _pair():
            staged_steps(8 * n_oct + (n_far & 4), 1, logits, absorb)

        @pl.when((n_far & 1) != 0)
        def _single():
            logits(n_far - 1, 1, 0)
            absorb(n_far - 1, 1, 0)

    def idx_scores(j, nblocks):
        k_blk = kidx_ref[key_rows(j, nblocks), :]
        acc = [jnp.zeros((nblocks * T, T), F32), jnp.zeros((nblocks * T, T), F32)]
        for tt in range(NT):
            hf, p = divmod(tt, NP)
            s = _dot_nt(k_blk, qim_ref[tt])
            for e in range(2):
                h = 2 * p + e
                w_h = w_rows[h:h + 1, hf * T:(hf + 1) * T]
                acc[hf] = acc[hf] + w_h * jnp.maximum(s[:, e * T:(e + 1) * T], 0.0)
        return jnp.concatenate(acc, axis=1)

    def fox_logits(j, nblocks, slot, causal=False):
        for tt in range(NT):
            hf, p = divmod(tt, NP)
            s = _dot_nt(kaug_ref[p, key_rows(j, nblocks), :], qaug_ref[tt])
            if causal:
                ok = causal_qq[:, hf * T:(hf + 1) * T]
                s = jnp.where(jnp.concatenate([ok, ok], axis=1), s, MASKED)
            s_ref[slot, slot_rows(nblocks), tile_cols(tt)] = s

    def fox_absorb(j, nblocks, slot):
        for tt in range(NT):
            p = tt % NP
            online_update(tt, s_ref[slot, slot_rows(nblocks), tile_cols(tt)],
                          blocks_t(lambda jb: vt_ref[p, jb], j, nblocks))

    def idx_fox_logits(j, nblocks, slot):
        score_ref[pl.ds(j, nblocks)] = idx_scores(j, nblocks).reshape(nblocks, T, Q)
        fox_logits(j, nblocks, slot)

    reset_state()
    for_far_blocks(jnp.maximum(i0 - 2, 0), idx_fox_logits, fox_absorb)

    def diag_logits(slot):
        score_ref[pl.ds(i0, 2)] = jnp.where(causal_qq, idx_scores(i0, 2), -jnp.inf).reshape(2, T, Q)
        fox_logits(i0, 2, slot, causal=True)

    @pl.when(g >= 1)
    def _idx_fox_tail():
        idx_fox_logits(i0 - 2, 2, 0)
        diag_logits(1)
        fox_absorb(i0 - 2, 2, 0)
        fox_absorb(i0, 2, 1)

    @pl.when(g == 0)
    def _idx_fox_first():
        diag_logits(0)
        fox_absorb(0, 2, 0)

    row_lo = lax.broadcasted_iota(jnp.int32, (LANES, T), 0) < HEAD_DIM
    for tt in range(NT):
        hf, p = divmod(tt, NP)
        c0 = slice(tt * TW, tt * TW + T)
        c1 = slice(tt * TW + T, (tt + 1) * TW)
        o_t = jnp.where(row_lo, acc_ref[0:LANES, c0] / acc_ref[LANES:LANES + 1, c0],
                        acc_ref[0:LANES, c1] / acc_ref[LANES:LANES + 1, c1])
        o_ref[hf * T:(hf + 1) * T, MIX_HALF + p * LANES:MIX_HALF + (p + 1) * LANES] = (
            o_t.T.astype(o_ref.dtype))

    kf = float(k_sel)
    t_pos = (g * Q + lax.broadcasted_iota(jnp.int32, (SUBLANES, Q), 1)).astype(F32)
    searching = (t_pos + 1.0) > kf
    G2 = 2 * T // SUBLANES
    n_pair = g + 1
    key_in_pair = (lax.broadcasted_iota(jnp.int32, (G2, SUBLANES, Q), 0) * SUBLANES
                   + lax.broadcasted_iota(jnp.int32, (G2, SUBLANES, Q), 1))

    def rep(v):
        return jnp.broadcast_to(v, (SUBLANES, Q))

    def score_pair(jj):
        return score_ref[pl.ds(2 * jj, 2)].reshape(G2, SUBLANES, Q)

    def count(*pred_fns, load=score_pair):
        def body(jj, cs):
            s = load(jj)
            return tuple(c + _tree(jnp.add, jnp.where(f(s, jj), 1.0, 0.0)) for c, f in zip(cs, pred_fns))
        zero = jnp.zeros((SUBLANES, Q), F32)
        cs = lax.fori_loop(0, n_pair, body, (zero,) * len(pred_fns))
        return tuple(rep(jnp.sum(c, axis=0, keepdims=True)) for c in cs)

    def count_ge(thr):
        return count(lambda s, jj: s >= thr[None])[0]

    def select_threshold():
        def first_body(jj, carry):
            mn, mx, ge0, gt0 = carry
            s = score_pair(jj)
            return (jnp.minimum(mn, _tree(jnp.minimum, jnp.where(s == -jnp.inf, jnp.inf, s))),
                    jnp.maximum(mx, _tree(jnp.maximum, s)),
                    ge0 + _tree(jnp.add, jnp.where(s >= 0.0, 1.0, 0.0)),
                    gt0 + _tree(jnp.add, jnp.where(s > 0.0, 1.0, 0.0)))

        zero = jnp.zeros((SUBLANES, Q), F32)
        mn, mx, c_ge0, c_gt0 = lax.fori_loop(0, n_pair, first_body, (zero + jnp.inf, zero - jnp.inf, zero, zero))
        mn = rep(jnp.min(mn, axis=0, keepdims=True))
        mx = rep(jnp.max(mx, axis=0, keepdims=True))
        c_ge0 = rep(jnp.sum(c_ge0, axis=0, keepdims=True))
        c_gt0 = rep(jnp.sum(c_gt0, axis=0, keepdims=True))
        above_mx = mx + jnp.maximum(jnp.abs(mx) * 2.0 ** -20, 1e-30)
        thr_ge0 = c_ge0 >= kf
        thr_gt0 = c_gt0 >= kf
        lo0 = jnp.where(thr_ge0, 0.0, mn)
        cnt_lo0 = jnp.where(thr_ge0, c_ge0, t_pos + 1.0)
        hi0 = jnp.where(thr_gt0, above_mx, 0.0)
        cnt_hi0 = jnp.where(thr_gt0, 0.0, jnp.where(thr_ge0, c_gt0, c_ge0))
        undecided = thr_gt0 | jnp.logical_not(thr_ge0)
        active0 = jnp.where(searching & undecided & (cnt_lo0 > kf), 1.0, 0.0)

        def cond(st):
            return jnp.logical_and(st[5] > 0.0, st[6] < 2200)

        def bisect(st):
            lo, hi, cnt_lo, cnt_hi, active = st
            mid = 0.5 * lo + 0.5 * hi
            adjacent = (mid <= lo) | (mid >= hi)
            c = count_ge(mid)
            act = active > 0.0
            go_lo = act & jnp.logical_not(adjacent) & (c >= kf)
            go_hi = act & jnp.logical_not(adjacent) & (c < kf)
            lo = jnp.where(go_lo, mid, lo)
            cnt_lo = jnp.where(go_lo, c, cnt_lo)
            hi = jnp.where(go_hi, mid, hi)
            cnt_hi = jnp.where(go_hi, c, cnt_hi)
            active = jnp.where(act & jnp.logical_not(adjacent) & (cnt_lo > kf), 1.0, 0.0)
            return lo, hi, cnt_lo, cnt_hi, active

        def body(st):
            inner = st[:5]
            for _ in range(SEARCH_UNROLL):
                inner = bisect(inner)
            return inner + (jnp.max(inner[4]), st[6] + SEARCH_UNROLL)

        lo, hi, cnt_lo, cnt_hi, _, _, _ = lax.while_loop(
            cond, body, (lo0, hi0, cnt_lo0, cnt_hi0, active0, jnp.max(active0), 0))
        thr = jnp.where(searching, lo, -F32_MAX)

        tied = searching & (cnt_lo > kf)
        need = kf - cnt_hi

        @pl.when(jnp.max(jnp.where(tied, 1.0, 0.0)) > 0.0)
        def _break_ties():
            def tie_pair(jj):
                return tie_ref[pl.ds(2 * jj, 2)].reshape(G2, SUBLANES, Q)

            def mark_body(jj, c):
                key_index = (2 * jj * T + key_in_pair).astype(F32)
                tie_ref[pl.ds(2 * jj, 2)] = (
                    jnp.where(score_pair(jj) == thr[None], key_index, jnp.inf).reshape(2, T, Q))
                return c
            lax.fori_loop(0, n_pair, mark_body, 0)

            def count_tied_upto(jcut):
                return count(lambda e, jj: e <= jcut[None], load=tie_pair)[0]

            def bs_body(_, st):
                jl, jh = st
                jm = jnp.floor(0.5 * (jl + jh))
                ok = count_tied_upto(jm) >= need
                return jnp.where(ok, jl, jm), jnp.where(ok, jm, jh)

            n_keys = (2 * n_pair * T).astype(F32)
            jl0 = jnp.full((SUBLANES, Q), -1.0, F32)
            jh0 = jnp.zeros((SUBLANES, Q), F32) + (n_keys - 1.0)
            n_steps = int(math.ceil(math.log2(nblk * T))) + 1
            _, jcut = lax.fori_loop(0, n_steps, bs_body, (jl0, jh0))

            def drop_body(jj, c):
                e = tie_pair(jj)
                drop = tied[None] & (e > jcut[None]) & (e < jnp.inf)
                score_ref[pl.ds(2 * jj, 2)] = jnp.where(drop, -jnp.inf, score_pair(jj)).reshape(2, T, Q)
                return c
            lax.fori_loop(0, n_pair, drop_body, 0)

        return thr

    thr = lax.cond((i0 + 2) * T > k_sel, select_threshold,
                   lambda: jnp.full((SUBLANES, Q), -F32_MAX, F32))

    def mask_body(jj, c):
        add = jnp.where(score_pair(jj) >= thr[None], 0.0, MASKED).reshape(2 * T, Q).astype(BF16)
        for hf in range(2):
            lhsd_ref[hf, key_rows(2 * jj, 2), KV_RANK:2 * KV_RANK] = add[:, hf * T:(hf + 1) * T]
        return c

    lax.fori_loop(0, n_pair, mask_body, 0)

    def dsa_logits(j, nblocks, slot, bias=None):
        for tt in range(NT):
            hf, p = divmod(tt, NP)
            s = _dot_nt(lhsd_ref[hf, key_rows(j, nblocks), :], qsel_ref[tt])
            if bias is not None:
                s = s + bias(hf, slice(p * TW, (p + 1) * TW))
            s_ref[slot, slot_rows(nblocks), tile_cols(tt)] = s

    def dsa_absorb(j, nblocks, slot):
        pv_lhs = blocks_t(lambda jb: ct_ref[jb], j, nblocks)
        for tt in range(NT):
            online_update(tt, s_ref[slot, slot_rows(nblocks), tile_cols(tt)], pv_lhs)

    def bias_before(hf, cols):
        return jnp.concatenate([bias_ref[2, :, cols], bias_ref[1 + hf, :, cols]], axis=0)

    def bias_diag(hf, cols):
        first = bias_ref[hf, :, cols]
        return jnp.concatenate([first, bias_ref[0, :, cols]], axis=0)

    reset_state()
    for_far_blocks(jnp.maximum(i0 - 2, 0), dsa_logits, dsa_absorb)
    m_ref[...] = m_ref[...] + bfar_ref[...]

    @pl.when(g >= 1)
    def _dsa_tail():
        dsa_logits(i0 - 2, 2, 0, bias_before)
        dsa_logits(i0, 2, 1, bias_diag)
        dsa_absorb(i0 - 2, 2, 0)
        dsa_absorb(i0, 2, 1)

    @pl.when(g == 0)
    def _dsa_first():
        dsa_logits(0, 2, 0, bias_diag)
        dsa_absorb(0, 2, 0)

    for tt in range(NT):
        hf, p = divmod(tt, NP)
        o_pair = jnp.zeros((T, LANES), F32)
        for e in range(2):
            cols = slice(tt * TW + e * T, tt * TW + (e + 1) * T)
            o_t = acc_ref[0:KV_RANK, cols] / acc_ref[KV_RANK:KV_RANK + 1, cols]
            o_pair = o_pair + _dot(o_t.T.astype(BF16), wuv_ref[2 * p + e])
        o_ref[hf * T:(hf + 1) * T, p * LANES:(p + 1) * LANES] = o_pair.astype(o_ref.dtype)


_EV_QA, _EV_QI, _EV_QB, _EV_KB, _EV_VB = 0, 1, 2, 3, 4
_EV_CKV, _EV_KIDX = 20, 21


def _attention(hb, w_t, f_rows, wuk_pad, wuv_pad, bias_tiles, bias_far, place, k_sel):
    bsz, seq, _ = hb.shape
    T, Q, H = BLK, QUERIES, N_HEADS
    nblk = seq // T
    lanes = 2 * H * T
    pq, cq, pk, ck = place
    wide = lambda c: pl.BlockSpec((None, Q, MIX_HALF), lambda b, i, c=c: (b, i, c))
    once = dict(pipeline_mode=pl.Buffered(1))
    full_wide = lambda c: pl.BlockSpec((None, seq, MIX_HALF), lambda b, i, c=c: (b, 0, c), **once)
    full_narrow = lambda c: pl.BlockSpec((None, seq, LANES), lambda b, i, c=c: (b, 0, c), **once)
    const = lambda a: pl.BlockSpec(a.shape, lambda b, i, nd=a.ndim: (0,) * nd, **once)
    return pl.pallas_call(
        functools.partial(_attn_kernel, k_sel=k_sel),
        grid=(bsz, seq // Q),
        in_specs=[wide(_EV_QA), wide(_EV_QI), wide(_EV_QB),
                  pl.BlockSpec((None, H, Q), lambda b, i: (b, 0, i)),
                  pl.BlockSpec((None, Q, LANES), lambda b, i: (b, i, 0)),
                  full_wide(_EV_KB), full_wide(_EV_VB),
                  full_narrow(_EV_CKV), full_narrow(_EV_KIDX),
                  pl.BlockSpec((None, seq, LANES), lambda b, i: (b, 0, 0), **once),
                  const(wuk_pad), const(wuv_pad), const(bias_tiles), const(bias_far),
                  const(pq), const(cq), const(pk), const(ck)],
        out_specs=pl.BlockSpec((None, Q, 2 * MIX_HALF), lambda b, i: (b, i, 0)),
        out_shape=jax.ShapeDtypeStruct((bsz, seq, 2 * MIX_HALF), BF16),
        scratch_shapes=[pltpu.VMEM((nblk, T, Q), F32),
                        pltpu.VMEM((2, seq, 2 * KV_RANK), BF16),
                        pltpu.VMEM((nblk, PV_ROWS, T), BF16),
                        pltpu.VMEM((H // 2, seq, 2 * LANES), BF16),
                        pltpu.VMEM((H // 2, nblk, PV_ROWS, T), BF16),
                        pltpu.VMEM((H, 2 * T, 2 * KV_RANK), BF16),
                        pltpu.VMEM((H, 2 * T, 2 * LANES), BF16),
                        pltpu.VMEM((H, 2 * T, LANES), BF16),
                        pltpu.VMEM((SUBLANES, lanes), F32),
                        pltpu.VMEM((PV_ROWS, lanes), F32),
                        pltpu.VMEM((2, 2 * T, lanes), F32),
                        pltpu.VMEM((nblk, T, Q), F32)],
        compiler_params=_cparams(("parallel", "arbitrary")),
        name="dsa_fox_attention",
    )(hb, hb, hb, w_t, f_rows, hb, hb, hb, hb, f_rows, wuk_pad, wuv_pad, bias_tiles, bias_far,
      pq, cq, pk, ck)


def _proj_ln_kernel(a_ref, w_ref, x_ref, g_ref, b_ref, xo_ref, xb_ref, *, alpha, sub):
    for c in range(a_ref.shape[0] // sub):
        rows = slice(c * sub, (c + 1) * sub)
        y = alpha * x_ref[rows, :] + _dot(a_ref[rows, :], w_ref[...])
        out = _layer_norm_rows(y, g_ref[...], b_ref[...])
        xo_ref[rows, :] = out
        xb_ref[rows, :] = out.astype(BF16)


def _proj_ln(a, w, x, g, b, alpha, tm):
    n, k = a.shape
    d = w.shape[1]
    return pl.pallas_call(
        functools.partial(_proj_ln_kernel, alpha=alpha, sub=min(128, tm)),
        grid=(n // tm,),
        in_specs=[pl.BlockSpec((tm, k), lambda i: (i, 0)),
                  pl.BlockSpec((k, d), lambda i: (0, 0)),
                  pl.BlockSpec((tm, d), lambda i: (i, 0)),
                  pl.BlockSpec((1, d), lambda i: (0, 0)),
                  pl.BlockSpec((1, d), lambda i: (0, 0))],
        out_specs=[pl.BlockSpec((tm, d), lambda i: (i, 0)),
                   pl.BlockSpec((tm, d), lambda i: (i, 0))],
        out_shape=[jax.ShapeDtypeStruct((n, d), F32),
                   jax.ShapeDtypeStruct((n, d), BF16)],
        compiler_params=_cparams(("parallel",)),
        name="proj_residual_ln",
    )(a, w, x, g, b)


def _causal_conv3(h, tail, cw):
    w0, w1, w2 = cw[0:1, :], cw[1:2, :], cw[2:3, :]
    body = w0 * pltpu.roll(h, 2, axis=0) + w1 * pltpu.roll(h, 1, axis=0) + w2 * h
    head = h[0:SUBLANES, :]
    r = lax.broadcasted_iota(jnp.int32, head.shape, 0)
    hm1 = jnp.where(r < 1, pltpu.roll(tail, 1, axis=0), pltpu.roll(head, 1, axis=0))
    hm2 = jnp.where(r < 2, pltpu.roll(tail, 2, axis=0), pltpu.roll(head, 2, axis=0))
    out_head = w0 * hm2 + w1 * hm1 + w2 * head
    return jnp.concatenate([out_head, body[SUBLANES:, :]], axis=0)


def _ffn_up_kernel(x_ref, wg_ref, wv_ref, cg_ref, cv_ref, o_ref, tg_ref, tv_ref, *h_refs,
                   tiles_per_seq, sub):
    n_chunks = x_ref.shape[0] // sub
    hg_refs, hv_refs = h_refs[:n_chunks], h_refs[n_chunks:]
    j = pl.program_id(1)
    first = (pl.program_id(0) % tiles_per_seq) == 0
    body_rows = slice(SUBLANES, SUBLANES + sub)

    def matmuls(c):
        x = x_ref[c * sub:(c + 1) * sub, :]
        hg_refs[c][body_rows, :] = _dot(x, wg_ref[...])
        hv_refs[c][body_rows, :] = _dot(x, wv_ref[...])

    def conv(h_ref, cw):
        return (cw[0:1, :] * h_ref[SUBLANES - 2:SUBLANES - 2 + sub, :]
                + cw[1:2, :] * h_ref[SUBLANES - 1:SUBLANES - 1 + sub, :]
                + cw[2:3, :] * h_ref[body_rows, :])

    def gate(c):
        if c == 0:
            hg_refs[0][0:SUBLANES, :] = jnp.where(first, 0.0, tg_ref[j])
            hv_refs[0][0:SUBLANES, :] = jnp.where(first, 0.0, tv_ref[j])
        else:
            hg_refs[c][0:SUBLANES, :] = hg_refs[c - 1][sub:sub + SUBLANES, :]
            hv_refs[c][0:SUBLANES, :] = hv_refs[c - 1][sub:sub + SUBLANES, :]
        g = conv(hg_refs[c], cg_ref[...])
        v = conv(hv_refs[c], cv_ref[...])
        half_g = 0.5 * g
        o_ref[c * sub:(c + 1) * sub, :] = ((half_g + half_g * jnp.tanh(half_g)) * v).astype(o_ref.dtype)

    matmuls(0)
    for c in range(n_chunks):
        if c + 1 < n_chunks:
            matmuls(c + 1)
        gate(c)
    tg_ref[j] = hg_refs[n_chunks - 1][sub:sub + SUBLANES, :]
    tv_ref[j] = hv_refs[n_chunks - 1][sub:sub + SUBLANES, :]


def _ffn_up(xb, w_tiles, conv_w, seq, tm, sub):
    n, k = xb.shape
    ncol, tn = w_tiles.shape[0] // 2, w_tiles.shape[2]
    return pl.pallas_call(
        functools.partial(_ffn_up_kernel, tiles_per_seq=seq // tm, sub=sub),
        grid=(n // tm, ncol),
        in_specs=[pl.BlockSpec((tm, k), lambda i, j: (i, 0)),
                  pl.BlockSpec((None, k, tn), lambda i, j: (j, 0, 0)),
                  pl.BlockSpec((None, k, tn), lambda i, j, ncol=ncol: (j + ncol, 0, 0)),
                  pl.BlockSpec((CONV_W, tn), lambda i, j: (0, j)),
                  pl.BlockSpec((CONV_W, tn), lambda i, j, ncol=ncol: (0, j + ncol))],
        out_specs=pl.BlockSpec((tm, tn), lambda i, j: (i, j)),
        out_shape=jax.ShapeDtypeStruct((n, ncol * tn), BF16),
        scratch_shapes=([pltpu.VMEM((ncol, SUBLANES, tn), F32)] * 2
                        + [pltpu.VMEM((sub + SUBLANES, tn), F32)] * (2 * (tm // sub))),
        compiler_params=_cparams(("arbitrary", "arbitrary")),
        name="ffn_up_conv_gate",
    )(xb, w_tiles, w_tiles, conv_w, conv_w)


def _gelu_tanh(x):
    return 0.5 * x * (1.0 + jnp.tanh(math.sqrt(2.0 / math.pi) * (x + 0.044715 * (x * x * x))))


def _group_mean(v, avg):
    hi = v.astype(BF16)
    lo = (v - hi.astype(F32)).astype(BF16)
    return _dot(hi, avg) + _dot(lo, avg)


def _odd_mixer_kernel(x_ref, w_ref, sg_ref, sw_ref, sb_ref, cw_ref, wo_ref, xf_ref, g_ref, b_ref,
                      xo_ref, xb_ref, o_ref, tail_ref, *, tiles_per_seq, alpha):
    tm = x_ref.shape[0]
    W = MIX_HALF
    first = (pl.program_id(0) % tiles_per_seq) == 0
    x = x_ref[...]
    lane = lax.broadcasted_iota(jnp.int32, (CHUNK, LANES), 1)
    lo_half = lane < HEAD_DIM
    r = lax.broadcasted_iota(jnp.int32, (W, W), 0) // HEAD_DIM
    c = lax.broadcasted_iota(jnp.int32, (W, W), 1) // HEAD_DIM
    avg = jnp.where(r == c, 1.0 / HEAD_DIM, 0.0).astype(BF16)
    tri = (lax.broadcasted_iota(jnp.int32, (CHUNK, CHUNK), 1)
           <= lax.broadcasted_iota(jnp.int32, (CHUNK, CHUNK), 0))

    u = _gelu_tanh(_dot(x, w_ref[:, 0:W]))
    v = _gelu_tanh(_dot(x, w_ref[:, W:2 * W]))
    vc = v - _group_mean(v, avg)
    var = _group_mean(vc * vc, avg)
    vn = (vc * lax.rsqrt(var + LN_EPS) * sg_ref[...]).astype(BF16)
    for n in range(tm // CHUNK):
        rows = slice(n * CHUNK, (n + 1) * CHUNK)
        for p in range(N_HEADS // 2):
            cols = slice(p * LANES, (p + 1) * LANES)
            vp = vn[rows, cols]
            w_e = jnp.where(tri, sw_ref[2 * p], 0.0).astype(BF16)
            w_o = jnp.where(tri, sw_ref[2 * p + 1], 0.0).astype(BF16)
            mix = (_dot(w_e, jnp.where(lo_half, vp, 0)) + _dot(w_o, jnp.where(lo_half, 0, vp))
                   + sb_ref[:, cols])
            o_ref[rows, cols] = (u[rows, cols] * mix).astype(o_ref.dtype)

    g_b = _dot(x, w_ref[:, 2 * W:3 * W])
    y = _dot(x, w_ref[:, 3 * W:4 * W]) * _dot(x, w_ref[:, 4 * W:5 * W])
    conv = _causal_conv3(y, jnp.where(first, 0.0, tail_ref[...]), cw_ref[...])
    tail_ref[...] = y[tm - SUBLANES:tm, :]
    o_ref[:, W:2 * W] = (g_b * conv).astype(o_ref.dtype)

    out = _layer_norm_rows(alpha * xf_ref[...] + _dot(o_ref[...], wo_ref[...]), g_ref[...], b_ref[...])
    xo_ref[...] = out
    xb_ref[...] = out.astype(BF16)


def _odd_mixer(xb, w_in, sgu_g, sgu_w, sgu_b_tile, conv_w, w_o, xf, g, b, alpha, seq, tm):
    n, k = xb.shape
    W = MIX_HALF
    d = w_o.shape[1]
    const2 = lambda shape: pl.BlockSpec(shape, lambda i: (0, 0))
    row_tile = lambda width: pl.BlockSpec((tm, width), lambda i: (i, 0))
    return pl.pallas_call(
        functools.partial(_odd_mixer_kernel, tiles_per_seq=seq // tm, alpha=alpha),
        grid=(n // tm,),
        in_specs=[row_tile(k), const2((k, 5 * W)), const2((1, W)),
                  pl.BlockSpec((N_HEADS, CHUNK, CHUNK), lambda i: (0, 0, 0)),
                  const2((CHUNK, W)), const2((CONV_W, W)),
                  const2((2 * W, d)), row_tile(d), const2((1, d)), const2((1, d))],
        out_specs=[row_tile(d), row_tile(d)],
        out_shape=[jax.ShapeDtypeStruct((n, d), F32), jax.ShapeDtypeStruct((n, d), BF16)],
        scratch_shapes=[pltpu.VMEM((tm, 2 * W), BF16),
                        pltpu.VMEM((SUBLANES, W), F32)],
        compiler_params=_cparams(("arbitrary",)),
        name="odd_mixer",
    )(xb, w_in, sgu_g, sgu_w, sgu_b_tile, conv_w, w_o, xf, g, b)


def _t5_bucket_np(dist):
    max_exact = N_BUCKETS // 2
    n = np.maximum(dist, 0)
    nf = np.maximum(n, 1).astype(np.float32)
    large = max_exact + (np.log(nf / max_exact) / np.float32(math.log(T5_MAX_DIST / max_exact))
                         * (N_BUCKETS - max_exact)).astype(np.int32)
    large = np.minimum(large, N_BUCKETS - 1)
    return np.where(n < max_exact, n, large)


def _bias_tables(rel_bias):
    T = BLK
    s = np.arange(T)[:, None]
    t = np.arange(T)[None, :]
    bucket = _t5_bucket_np(np.stack([t + d * T - s for d in range(3)]))
    far_bucket = N_BUCKETS - 1
    assert (bucket[2] == far_bucket).all()
    assert (_t5_bucket_np(np.arange(T + 1, 64 * T)) == far_bucket).all()
    rb = rel_bias.astype(F32) * LOG2E
    tiles = sum(jnp.where(jnp.asarray(bucket == b)[..., None], rb[b], 0.0) for b in range(N_BUCKETS))
    tiles = jnp.swapaxes(tiles, 2, 3).reshape(3, T, N_HEADS * T)
    far = jnp.tile(jnp.repeat(rb[far_bucket], T)[None, :], (1, 2))
    return tiles, far


def _placement_constants():
    H, L = N_HEADS, LANES
    pq = np.zeros((3 * L, H * L), np.float32)
    cq = np.zeros((1, H * L), np.float32)
    pk = np.zeros((3 * L, (H // 2) * L), np.float32)
    ck = np.zeros((1, (H // 2) * L), np.float32)
    for h in range(H):
        p, e = divmod(h, 2)
        for piece in range(3):
            pq[piece * L + h, h * L + 6 + piece] = 1.0
            cq[0, h * L + 3 * e + piece] = 1.0
            pk[piece * L + h, p * L + 3 * e + piece] = -1.0
            ck[0, p * L + 6 + piece] = 1.0
    return jnp.asarray(pq, BF16), jnp.asarray(cq), jnp.asarray(pk, BF16), jnp.asarray(ck)


def _even_weights(w_in, w_uk, w_uv):
    hd, H = HEAD_DIM, N_HEADS
    widths = (H * hd, KV_RANK, H * hd, hd, H, H * hd, H * hd, H * hd, H)
    offs = np.cumsum((0,) + widths)
    seg = lambda s: w_in[:, offs[s]:offs[s + 1]]
    q_a, c_kv, q_idx, k_idx, w_idx, q_b, k_b, v_b, f_in = [seg(s) for s in range(9)]
    scale = hd ** -0.5
    w_big = jnp.concatenate([q_a, q_idx * scale, q_b * (scale * LOG2E), k_b, v_b, c_kv,
                             k_idx, k_idx], axis=1).astype(BF16)
    pad = jnp.zeros((w_in.shape[0], LANES - 2 * H), w_in.dtype)
    w_small = jnp.concatenate([w_idx, f_in, pad], axis=1).astype(BF16)
    wuk_t = jnp.swapaxes(w_uk, 1, 2) * (scale * LOG2E)
    odd = (np.arange(H) % 2 == 1)[:, None, None]
    zk, zv = jnp.zeros_like(wuk_t), jnp.zeros_like(w_uv)
    wuk_pad = jnp.where(odd, jnp.concatenate([zk, wuk_t], axis=1), jnp.concatenate([wuk_t, zk], axis=1))
    wuv_pad = jnp.where(odd, jnp.concatenate([zv, w_uv], axis=2), jnp.concatenate([w_uv, zv], axis=2))
    return w_big, w_small, wuk_pad.astype(BF16), wuv_pad.astype(BF16)


def kernel(x, ln_g, ln_b, rel_bias, ev_w_in, ev_w_uk, ev_w_uv, ev_b_f, ev_w_o,
           od_w_in, od_sgu_g, od_sgu_w, od_sgu_b, od_conv_w, od_w_o,
           ffn_w_up, ffn_conv_w, ffn_w_down):
    bsz, seq, d = x.shape
    depth = ln_g.shape[0]
    n = bsz * seq
    alpha = (2.0 * depth) ** 0.25
    k_sel = min(TOPK_MAX, seq // 4)
    H = N_HEADS
    assert seq % QUERIES == 0 and seq % CHUNK == 0
    tm = min(512, seq)
    tm_ffn = min(1024, seq)

    bias_tiles, bias_far = _bias_tables(rel_bias)
    place = _placement_constants()
    xf = x.reshape(n, d)
    xb = xf
    for layer in range(depth):
        j = layer // 2
        g0, b0 = ln_g[layer, 0][None, :], ln_b[layer, 0][None, :]
        g1, b1 = ln_g[layer, 1][None, :], ln_b[layer, 1][None, :]
        if layer % 2 == 0:
            w_big, w_small, wuk_pad, wuv_pad = _even_weights(ev_w_in[j], ev_w_uk[j], ev_w_uv[j])
            hb, hs = _even_proj(xb, w_big, w_small, tm)
            hb = hb.reshape(bsz, seq, -1)
            hs_t = jnp.swapaxes(hs.reshape(bsz, seq, -1)[:, :, 0:2 * H], 1, 2)
            cum_f = _cumf(hs_t[:, H:2 * H], ev_b_f[j][:, None])
            f_rows = jnp.pad(jnp.swapaxes(cum_f, 1, 2), ((0, 0), (0, 0), (0, LANES - H)))
            mix = _attention(hb, hs_t[:, 0:H], f_rows, wuk_pad, wuv_pad, bias_tiles, bias_far,
                             place, k_sel)
            xf, xb = _proj_ln(mix.reshape(n, d), ev_w_o[j].astype(BF16), xf, g0, b0, alpha, tm_ffn)
        else:
            sgu_b_tile = jnp.repeat(jnp.swapaxes(od_sgu_b[j], 0, 1), HEAD_DIM, axis=1)
            xf, xb = _odd_mixer(xb, od_w_in[j].astype(BF16), od_sgu_g[j].reshape(1, -1), od_sgu_w[j],
                                sgu_b_tile, od_conv_w[j], od_w_o[j].astype(BF16), xf, g0, b0, alpha,
                                seq, min(256, seq))
        w_tiles = jnp.swapaxes(ffn_w_up[layer].astype(BF16).reshape(d, -1, FFN_TN), 0, 1)
        act = _ffn_up(xb, w_tiles, ffn_conv_w[layer], seq, tm_ffn, min(512, seq))
        xf, xb = _proj_ln(act, ffn_w_down[layer].astype(BF16), xf, g1, b1, alpha, tm_ffn)
    return xf.reshape(bsz, seq, d)
```

```python
import functools
import math

import numpy as np
import jax
import jax.numpy as jnp
from jax import lax
from jax.experimental import pallas as pl
from jax.experimental.pallas import tpu as pltpu

F32 = jnp.float32
BF16 = jnp.bfloat16

HEAD_DIM = 64
N_HEADS = 8
KV_RANK = 128
TOPK_MAX = 256
N_BUCKETS = 32
T5_MAX_DIST = 128
CHUNK = 128
CONV_W = 3
LN_EPS = 1e-5
MIX_HALF = N_HEADS * HEAD_DIM

LANES = 128
SUBLANES = 8
BF16_ROWS = 16
VMEM_LIMIT = 56 * 1024 * 1024

BLK = 128
QUERIES = 2 * BLK
PV_ROWS = KV_RANK + BF16_ROWS
NEG_BIG = -1e30
MASKED = -(2.0 ** 100)
F32_MAX = float(np.finfo(np.float32).max)
LOG2E = math.log2(math.e)
SEARCH_UNROLL = 3


def _cparams(sem):
    return pltpu.CompilerParams(dimension_semantics=sem, vmem_limit_bytes=VMEM_LIMIT)


def _dot(a, b):
    return jnp.dot(a, b, preferred_element_type=F32)


def _dot_nt(a, b):
    return lax.dot_general(a, b, (((1,), (1,)), ((), ())), preferred_element_type=F32)


def _layer_norm_rows(y, g, b):
    mu = jnp.mean(y, axis=-1, keepdims=True)
    yc = y - mu
    var = jnp.mean(yc * yc, axis=-1, keepdims=True)
    return yc * lax.rsqrt(var + LN_EPS) * g + b


def _tree(op, x):
    while x.shape[0] > 1:
        half = x.shape[0] // 2
        x = op(x[:half], x[half:])
    return x[0]


def _bf16_pieces(v):
    hi = v.astype(BF16)
    r1 = v - hi.astype(F32)
    mid = r1.astype(BF16)
    lo = (r1 - mid.astype(F32)).astype(BF16)
    return jnp.concatenate([hi, mid, lo], axis=1)


def _even_proj_kernel(x_ref, wb_ref, ws_ref, hb_ref, hs_ref):
    x = x_ref[...].astype(BF16)
    hb_ref[...] = _dot(x, wb_ref[...]).astype(hb_ref.dtype)
    hs_ref[...] = _dot(x, ws_ref[...])


def _even_proj(xb, w_big, w_small, tm):
    n, k = xb.shape
    mb = w_big.shape[1]
    ms = w_small.shape[1]
    return pl.pallas_call(
        _even_proj_kernel,
        grid=(n // tm,),
        in_specs=[pl.BlockSpec((tm, k), lambda i: (i, 0)),
                  pl.BlockSpec((k, mb), lambda i: (0, 0)),
                  pl.BlockSpec((k, ms), lambda i: (0, 0))],
        out_specs=[pl.BlockSpec((tm, mb), lambda i: (i, 0)),
                   pl.BlockSpec((tm, ms), lambda i: (i, 0))],
        out_shape=[jax.ShapeDtypeStruct((n, mb), BF16),
                   jax.ShapeDtypeStruct((n, ms), F32)],
        compiler_params=_cparams(("parallel",)),
        name="even_proj",
    )(xb, w_big, w_small)


def _cumf_kernel(f_ref, b_ref, o_ref):
    seq = f_ref.shape[-1]
    row = lax.broadcasted_iota(jnp.int32, (LANES, LANES), 0)
    col = lax.broadcasted_iota(jnp.int32, (LANES, LANES), 1)
    upper = jnp.where(row <= col, 1.0, 0.0).astype(F32)
    carry = jnp.zeros((f_ref.shape[0], 1), F32)
    for c in range(seq // LANES):
        z = f_ref[:, c * LANES:(c + 1) * LANES] + b_ref[...]
        lf = jnp.minimum(z, 0.0) - jnp.log1p(jnp.exp(-jnp.abs(z)))
        cs = jnp.dot(lf, upper, preferred_element_type=F32,
                     precision=lax.Precision.HIGHEST) + carry
        o_ref[:, c * LANES:(c + 1) * LANES] = cs
        carry = cs[:, LANES - 1:LANES]


def _cumf(f_t, b_f):
    bsz, nh, seq = f_t.shape
    return pl.pallas_call(
        _cumf_kernel,
        grid=(bsz,),
        in_specs=[pl.BlockSpec((None, nh, seq), lambda b: (b, 0, 0)),
                  pl.BlockSpec((nh, 1), lambda b: (0, 0))],
        out_specs=pl.BlockSpec((None, nh, seq), lambda b: (b, 0, 0)),
        out_shape=jax.ShapeDtypeStruct((bsz, nh, seq), F32),
        compiler_params=_cparams(("parallel",)),
        name="fox_cumf",
    )(f_t, b_f)


def _attn_kernel(qa_ref, qi_ref, qb_ref, wt_ref, fq_ref, kb_ref, vb_ref, ckv_ref, kidx_ref, fk_ref,
                 wuk_ref, wuv_ref, bias_ref, bfar_ref, pq_ref, cq_ref, pk_ref, ck_ref,
                 o_ref,
                 score_ref, lhsd_ref, ct_ref, kaug_ref, vt_ref, qsel_ref, qaug_ref, qim_ref,
                 m_ref, acc_ref, s_ref, tie_ref, *, k_sel):
    T = BLK
    Q = QUERIES
    H = N_HEADS
    NP = H // 2
    NT = 2 * NP
    TW = 2 * T
    g = pl.program_id(1)
    i0 = 2 * g
    nblk = ct_ref.shape[0]
    lane = lax.broadcasted_iota(jnp.int32, (T, LANES), 1)
    lo_half = lane < HEAD_DIM
    ones_rows = jnp.ones((BF16_ROWS, T), BF16)
    causal_qq = (lax.broadcasted_iota(jnp.int32, (Q, Q), 0) <= lax.broadcasted_iota(jnp.int32, (Q, Q), 1))

    @pl.when(g == 0)
    def _per_batch():
        def prep(j, c):
            r = pl.ds(pl.multiple_of(j * T, T), T)
            cb = ckv_ref[r, :]
            lhsd_ref[0, r, 0:KV_RANK] = cb
            lhsd_ref[1, r, 0:KV_RANK] = cb
            ct_ref[j, 0:KV_RANK, :] = cb.astype(F32).T.astype(BF16)
            ct_ref[j, KV_RANK:PV_ROWS, :] = ones_rows
            ext = (_dot(_bf16_pieces(fk_ref[r, :] * LOG2E), pk_ref[...]) + ck_ref[...]).astype(BF16)
            for p in range(NP):
                cols = slice(p * LANES, (p + 1) * LANES)
                kaug_ref[p, r, 0:LANES] = kb_ref[r, cols]
                kaug_ref[p, r, LANES:2 * LANES] = ext[:, cols]
                vt_ref[p, j, 0:LANES, :] = vb_ref[r, cols].astype(F32).T.astype(BF16)
                vt_ref[p, j, LANES:PV_ROWS, :] = ones_rows
            return c
        lax.fori_loop(0, nblk, prep, 0)

    w_rows = wt_ref[...] * (H ** -0.5)
    ext_q = (_dot(_bf16_pieces(fq_ref[...] * LOG2E), pq_ref[...]) + cq_ref[...]).astype(BF16)
    eye = jnp.where(lax.broadcasted_iota(jnp.int32, (T, T), 0) == lax.broadcasted_iota(jnp.int32, (T, T), 1),
                    1.0, 0.0).astype(BF16)
    for tt in range(NT):
        hf, p = divmod(tt, NP)
        qrows = slice(hf * T, (hf + 1) * T)
        cols = slice(p * LANES, (p + 1) * LANES)
        qa_pair, qi_pair, qb_pair = qa_ref[qrows, cols], qi_ref[qrows, cols], qb_ref[qrows, cols]
        for e in range(2):
            h = 2 * p + e
            rows = slice(e * T, (e + 1) * T)
            mine = lo_half if e == 0 else jnp.logical_not(lo_half)
            qsel_ref[tt, rows, 0:KV_RANK] = _dot(qa_pair, wuk_ref[h]).astype(BF16)
            qsel_ref[tt, rows, KV_RANK:2 * KV_RANK] = eye
            qaug_ref[tt, rows, 0:LANES] = jnp.where(mine, qb_pair, 0)
            qaug_ref[tt, rows, LANES:2 * LANES] = ext_q[qrows, h * LANES:(h + 1) * LANES]
            qim_ref[tt, rows, :] = jnp.where(mine, qi_pair, 0)

    def reset_state():
        m_ref[...] = jnp.full(m_ref.shape, NEG_BIG, F32)
        acc_ref[...] = jnp.zeros(acc_ref.shape, F32)

    def tile_cols(tt):
        return slice(tt * TW, (tt + 1) * TW)

    def online_update(tt, s, pv_lhs):
        cols = tile_cols(tt)
        m_old = m_ref[:, cols]
        s_max = _tree(jnp.maximum, s.reshape(-1, SUBLANES, TW))
        m_new = jnp.maximum(m_old, jnp.max(s_max, axis=0, keepdims=True))
        alpha = jnp.exp2(m_old - m_new)
        pr = jnp.exp2((s - m_new[0:1, :]).astype(BF16))
        acc_ref[:, cols] = acc_ref[:, cols] * alpha[0:1, :] + _dot(pv_lhs, pr)
        m_ref[:, cols] = m_new

    def key_rows(j, nblocks):
        return pl.ds(pl.multiple_of(j * T, T), nblocks * T)

    def slot_rows(nblocks):
        return slice(0, nblocks * T)

    def blocks_t(load, j, nblocks):
        return load(j) if nblocks == 1 else jnp.concatenate([load(j + b) for b in range(nblocks)], axis=1)

    def staged_steps(j0, n_steps, logits, absorb):
        logits(j0, 2, 0)
        for k in range(n_steps):
            if k + 1 < n_steps:
                logits(j0 + 2 * (k + 1), 2, (k + 1) % 2)
            absorb(j0 + 2 * k, 2, k % 2)

    def for_far_blocks(n_far, logits, absorb):
        n_oct = lax.shift_right_logical(n_far, 3)

        def oct_body(jj, c):
            staged_steps(8 * jj, 4, logits, absorb)
            return c
        lax.fori_loop(0, n_oct, oct_body, 0)

        @pl.when((n_far & 4) != 0)
        def _quad():
            staged_steps(8 * n_oct, 2, logits, absorb)

        @pl.when((n_far & 2) != 0)
        def _pair():
            staged_steps(8 * n_oct + (n_far & 4), 1, logits, absorb)

        @pl.when((n_far & 1) != 0)
        def _single():
            logits(n_far - 1, 1, 0)
            absorb(n_far - 1, 1, 0)

    def idx_scores(j, nblocks):
        k_blk = kidx_ref[key_rows(j, nblocks), :]
        acc = [jnp.zeros((nblocks * T, T), F32), jnp.zeros((nblocks * T, T), F32)]
        for tt in range(NT):
            hf, p = divmod(tt, NP)
            s = _dot_nt(k_blk, qim_ref[tt])
            for e in range(2):
                h = 2 * p + e
                w_h = w_rows[h:h + 1, hf * T:(hf + 1) * T]
                acc[hf] = acc[hf] + w_h * jnp.maximum(s[:, e * T:(e + 1) * T], 0.0)
        return jnp.concatenate(acc, axis=1)

    def fox_logits(j, nblocks, slot, causal=False):
        for tt in range(NT):
            hf, p = divmod(tt, NP)
            s = _dot_nt(kaug_ref[p, key_rows(j, nblocks), :], qaug_ref[tt])
            if causal:
                ok = causal_qq[:, hf * T:(hf + 1) * T]
                s = jnp.where(jnp.concatenate([ok, ok], axis=1), s, MASKED)
            s_ref[slot, slot_rows(nblocks), tile_cols(tt)] = s

    def fox_absorb(j, nblocks, slot):
        for tt in range(NT):
            p = tt % NP
            online_update(tt, s_ref[slot, slot_rows(nblocks), tile_cols(tt)],
                          blocks_t(lambda jb: vt_ref[p, jb], j, nblocks))

    def idx_fox_logits(j, nblocks, slot):
        score_ref[pl.ds(j, nblocks)] = idx_scores(j, nblocks).reshape(nblocks, T, Q)
        fox_logits(j, nblocks, slot)

    reset_state()
    for_far_blocks(jnp.maximum(i0 - 2, 0), idx_fox_logits, fox_absorb)

    def diag_logits(slot):
        score_ref[pl.ds(i0, 2)] = jnp.where(causal_qq, idx_scores(i0, 2), -jnp.inf).reshape(2, T, Q)
        fox_logits(i0, 2, slot, causal=True)

    @pl.when(g >= 1)
    def _idx_fox_tail():
        idx_fox_logits(i0 - 2, 2, 0)
        diag_logits(1)
        fox_absorb(i0 - 2, 2, 0)
        fox_absorb(i0, 2, 1)

    @pl.when(g == 0)
    def _idx_fox_first():
        diag_logits(0)
        fox_absorb(0, 2, 0)

    row_lo = lax.broadcasted_iota(jnp.int32, (LANES, T), 0) < HEAD_DIM
    for tt in range(NT):
        hf, p = divmod(tt, NP)
        c0 = slice(tt * TW, tt * TW + T)
        c1 = slice(tt * TW + T, (tt + 1) * TW)
        o_t = jnp.where(row_lo, acc_ref[0:LANES, c0] / acc_ref[LANES:LANES + 1, c0],
                        acc_ref[0:LANES, c1] / acc_ref[LANES:LANES + 1, c1])
        o_ref[hf * T:(hf + 1) * T, MIX_HALF + p * LANES:MIX_HALF + (p + 1) * LANES] = (
            o_t.T.astype(o_ref.dtype))

    kf = float(k_sel)
    t_pos = (g * Q + lax.broadcasted_iota(jnp.int32, (SUBLANES, Q), 1)).astype(F32)
    searching = (t_pos + 1.0) > kf
    G2 = 2 * T // SUBLANES
    n_pair = g + 1
    key_in_pair = (lax.broadcasted_iota(jnp.int32, (G2, SUBLANES, Q), 0) * SUBLANES
                   + lax.broadcasted_iota(jnp.int32, (G2, SUBLANES, Q), 1))

    def rep(v):
        return jnp.broadcast_to(v, (SUBLANES, Q))

    def score_pair(jj):
        return score_ref[pl.ds(2 * jj, 2)].reshape(G2, SUBLANES, Q)

    def count(*pred_fns, load=score_pair):
        def body(jj, cs):
            s = load(jj)
            return tuple(c + _tree(jnp.add, jnp.where(f(s, jj), 1.0, 0.0)) for c, f in zip(cs, pred_fns))
        zero = jnp.zeros((SUBLANES, Q), F32)
        cs = lax.fori_loop(0, n_pair, body, (zero,) * len(pred_fns))
        return tuple(rep(jnp.sum(c, axis=0, keepdims=True)) for c in cs)

    def count_ge(thr):
        return count(lambda s, jj: s >= thr[None])[0]

    def select_threshold():
        def first_body(jj, carry):
            mn, mx, ge0, gt0 = carry
            s = score_pair(jj)
            return (jnp.minimum(mn, _tree(jnp.minimum, jnp.where(s == -jnp.inf, jnp.inf, s))),
                    jnp.maximum(mx, _tree(jnp.maximum, s)),
                    ge0 + _tree(jnp.add, jnp.where(s >= 0.0, 1.0, 0.0)),
                    gt0 + _tree(jnp.add, jnp.where(s > 0.0, 1.0, 0.0)))

        zero = jnp.zeros((SUBLANES, Q), F32)
        mn, mx, c_ge0, c_gt0 = lax.fori_loop(0, n_pair, first_body, (zero + jnp.inf, zero - jnp.inf, zero, zero))
        mn = rep(jnp.min(mn, axis=0, keepdims=True))
        mx = rep(jnp.max(mx, axis=0, keepdims=True))
        c_ge0 = rep(jnp.sum(c_ge0, axis=0, keepdims=True))
        c_gt0 = rep(jnp.sum(c_gt0, axis=0, keepdims=True))
        above_mx = mx + jnp.maximum(jnp.abs(mx) * 2.0 ** -20, 1e-30)
        thr_ge0 = c_ge0 >= kf
        thr_gt0 = c_gt0 >= kf
        lo0 = jnp.where(thr_ge0, 0.0, mn)
        cnt_lo0 = jnp.where(thr_ge0, c_ge0, t_pos + 1.0)
        hi0 = jnp.where(thr_gt0, above_mx, 0.0)
        cnt_hi0 = jnp.where(thr_gt0, 0.0, jnp.where(thr_ge0, c_gt0, c_ge0))
        undecided = thr_gt0 | jnp.logical_not(thr_ge0)
        active0 = jnp.where(searching & undecided & (cnt_lo0 > kf), 1.0, 0.0)

        def cond(st):
            return jnp.logical_and(st[5] > 0.0, st[6] < 2200)

        def bisect(st):
            lo, hi, cnt_lo, cnt_hi, active = st
            mid = 0.5 * lo + 0.5 * hi
            adjacent = (mid <= lo) | (mid >= hi)
            c = count_ge(mid)
            act = active > 0.0
            go_lo = act & jnp.logical_not(adjacent) & (c >= kf)
            go_hi = act & jnp.logical_not(adjacent) & (c < kf)
            lo = jnp.where(go_lo, mid, lo)
            cnt_lo = jnp.where(go_lo, c, cnt_lo)
            hi = jnp.where(go_hi, mid, hi)
            cnt_hi = jnp.where(go_hi, c, cnt_hi)
            active = jnp.where(act & jnp.logical_not(adjacent) & (cnt_lo > kf), 1.0, 0.0)
            return lo, hi, cnt_lo, cnt_hi, active

        def body(st):
            inner = st[:5]
            for _ in range(SEARCH_UNROLL):
                inner = bisect(inner)
            return inner + (jnp.max(inner[4]), st[6] + SEARCH_UNROLL)

        lo, hi, cnt_lo, cnt_hi, _, _, _ = lax.while_loop(
            cond, body, (lo0, hi0, cnt_lo0, cnt_hi0, active0, jnp.max(active0), 0))
        thr = jnp.where(searching, lo, -F32_MAX)

        tied = searching & (cnt_lo > kf)
        need = kf - cnt_hi

        @pl.when(jnp.max(jnp.where(tied, 1.0, 0.0)) > 0.0)
        def _break_ties():
            def tie_pair(jj):
                return tie_ref[pl.ds(2 * jj, 2)].reshape(G2, SUBLANES, Q)

            def mark_body(jj, c):
                key_index = (2 * jj * T + key_in_pair).astype(F32)
                tie_ref[pl.ds(2 * jj, 2)] = (
                    jnp.where(score_pair(jj) == thr[None], key_index, jnp.inf).reshape(2, T, Q))
                return c
            lax.fori_loop(0, n_pair, mark_body, 0)

            def count_tied_upto(jcut):
                return count(lambda e, jj: e <= jcut[None], load=tie_pair)[0]

            def bs_body(_, st):
                jl, jh = st
                jm = jnp.floor(0.5 * (jl + jh))
                ok = count_tied_upto(jm) >= need
                return jnp.where(ok, jl, jm), jnp.where(ok, jm, jh)

            n_keys = (2 * n_pair * T).astype(F32)
            jl0 = jnp.full((SUBLANES, Q), -1.0, F32)
            jh0 = jnp.zeros((SUBLANES, Q), F32) + (n_keys - 1.0)
            n_steps = int(math.ceil(math.log2(nblk * T))) + 1
            _, jcut = lax.fori_loop(0, n_steps, bs_body, (jl0, jh0))

            def drop_body(jj, c):
                e = tie_pair(jj)
                drop = tied[None] & (e > jcut[None]) & (e < jnp.inf)
                score_ref[pl.ds(2 * jj, 2)] = jnp.where(drop, -jnp.inf, score_pair(jj)).reshape(2, T, Q)
                return c
            lax.fori_loop(0, n_pair, drop_body, 0)

        return thr

    thr = lax.cond((i0 + 2) * T > k_sel, select_threshold,
                   lambda: jnp.full((SUBLANES, Q), -F32_MAX, F32))

    def mask_body(jj, c):
        add = jnp.where(score_pair(jj) >= thr[None], 0.0, MASKED).reshape(2 * T, Q).astype(BF16)
        for hf in range(2):
            lhsd_ref[hf, key_rows(2 * jj, 2), KV_RANK:2 * KV_RANK] = add[:, hf * T:(hf + 1) * T]
        return c

    lax.fori_loop(0, n_pair, mask_body, 0)

    def dsa_logits(j, nblocks, slot, bias=None):
        for tt in range(NT):
            hf, p = divmod(tt, NP)
            s = _dot_nt(lhsd_ref[hf, key_rows(j, nblocks), :], qsel_ref[tt])
            if bias is not None:
                s = s + bias(hf, slice(p * TW, (p + 1) * TW))
            s_ref[slot, slot_rows(nblocks), tile_cols(tt)] = s

    def dsa_absorb(j, nblocks, slot):
        pv_lhs = blocks_t(lambda jb: ct_ref[jb], j, nblocks)
        for tt in range(NT):
            online_update(tt, s_ref[slot, slot_rows(nblocks), tile_cols(tt)], pv_lhs)

    def bias_before(hf, cols):
        return jnp.concatenate([bias_ref[2, :, cols], bias_ref[1 + hf, :, cols]], axis=0)

    def bias_diag(hf, cols):
        first = bias_ref[hf, :, cols]
        return jnp.concatenate([first, bias_ref[0, :, cols]], axis=0)

    reset_state()
    for_far_blocks(jnp.maximum(i0 - 2, 0), dsa_logits, dsa_absorb)
    m_ref[...] = m_ref[...] + bfar_ref[...]

    @pl.when(g >= 1)
    def _dsa_tail():
        dsa_logits(i0 - 2, 2, 0, bias_before)
        dsa_logits(i0, 2, 1, bias_diag)
        dsa_absorb(i0 - 2, 2, 0)
        dsa_absorb(i0, 2, 1)

    @pl.when(g == 0)
    def _dsa_first():
        dsa_logits(0, 2, 0, bias_diag)
        dsa_absorb(0, 2, 0)

    for tt in range(NT):
        hf, p = divmod(tt, NP)
        o_pair = jnp.zeros((T, LANES), F32)
        for e in range(2):
            cols = slice(tt * TW + e * T, tt * TW + (e + 1) * T)
            o_t = acc_ref[0:KV_RANK, cols] / acc_ref[KV_RANK:KV_RANK + 1, cols]
            o_pair = o_pair + _dot(o_t.T.astype(BF16), wuv_ref[2 * p + e])
        o_ref[hf * T:(hf + 1) * T, p * LANES:(p + 1) * LANES] = o_pair.astype(o_ref.dtype)


_EV_QA, _EV_QI, _EV_QB, _EV_KB, _EV_VB = 0, 1, 2, 3, 4
_EV_CKV, _EV_KIDX = 20, 21


def _attention(hb, w_t, f_rows, wuk_pad, wuv_pad, bias_tiles, bias_far, place, k_sel):
    bsz, seq, _ = hb.shape
    T, Q, H = BLK, QUERIES, N_HEADS
    nblk = seq // T
    lanes = 2 * H * T
    pq, cq, pk, ck = place
    wide = lambda c: pl.BlockSpec((None, Q, MIX_HALF), lambda b, i, c=c: (b, i, c))
    once = dict(pipeline_mode=pl.Buffered(1))
    full_wide = lambda c: pl.BlockSpec((None, seq, MIX_HALF), lambda b, i, c=c: (b, 0, c), **once)
    full_narrow = lambda c: pl.BlockSpec((None, seq, LANES), lambda b, i, c=c: (b, 0, c), **once)
    const = lambda a: pl.BlockSpec(a.shape, lambda b, i, nd=a.ndim: (0,) * nd, **once)
    return pl.pallas_call(
        functools.partial(_attn_kernel, k_sel=k_sel),
        grid=(bsz, seq // Q),
        in_specs=[wide(_EV_QA), wide(_EV_QI), wide(_EV_QB),
                  pl.BlockSpec((None, H, Q), lambda b, i: (b, 0, i)),
                  pl.BlockSpec((None, Q, LANES), lambda b, i: (b, i, 0)),
                  full_wide(_EV_KB), full_wide(_EV_VB),
                  full_narrow(_EV_CKV), full_narrow(_EV_KIDX),
                  pl.BlockSpec((None, seq, LANES), lambda b, i: (b, 0, 0), **once),
                  const(wuk_pad), const(wuv_pad), const(bias_tiles), const(bias_far),
                  const(pq), const(cq), const(pk), const(ck)],
        out_specs=pl.BlockSpec((None, Q, 2 * MIX_HALF), lambda b, i: (b, i, 0)),
        out_shape=jax.ShapeDtypeStruct((bsz, seq, 2 * MIX_HALF), BF16),
        scratch_shapes=[pltpu.VMEM((nblk, T, Q), F32),
                        pltpu.VMEM((2, seq, 2 * KV_RANK), BF16),
                        pltpu.VMEM((nblk, PV_ROWS, T), BF16),
                        pltpu.VMEM((H // 2, seq, 2 * LANES), BF16),
                        pltpu.VMEM((H // 2, nblk, PV_ROWS, T), BF16),
                        pltpu.VMEM((H, 2 * T, 2 * KV_RANK), BF16),
                        pltpu.VMEM((H, 2 * T, 2 * LANES), BF16),
                        pltpu.VMEM((H, 2 * T, LANES), BF16),
                        pltpu.VMEM((SUBLANES, lanes), F32),
                        pltpu.VMEM((PV_ROWS, lanes), F32),
                        pltpu.VMEM((2, 2 * T, lanes), F32),
                        pltpu.VMEM((nblk, T, Q), F32)],
        compiler_params=_cparams(("parallel", "arbitrary")),
        name="dsa_fox_attention",
    )(hb, hb, hb, w_t, f_rows, hb, hb, hb, hb, f_rows, wuk_pad, wuv_pad, bias_tiles, bias_far,
      pq, cq, pk, ck)


def _proj_ln_kernel(a_ref, w_ref, x_ref, g_ref, b_ref, xo_ref, xb_ref, *, alpha, sub):
    for c in range(a_ref.shape[0] // sub):
        rows = slice(c * sub, (c + 1) * sub)
        y = alpha * x_ref[rows, :] + _dot(a_ref[rows, :], w_ref[...])
        out = _layer_norm_rows(y, g_ref[...], b_ref[...])
        xo_ref[rows, :] = out
        xb_ref[rows, :] = out.astype(BF16)


def _proj_ln(a, w, x, g, b, alpha, tm):
    n, k = a.shape
    d = w.shape[1]
    return pl.pallas_call(
        functools.partial(_proj_ln_kernel, alpha=alpha, sub=min(128, tm)),
        grid=(n // tm,),
        in_specs=[pl.BlockSpec((tm, k), lambda i: (i, 0)),
                  pl.BlockSpec((k, d), lambda i: (0, 0)),
                  pl.BlockSpec((tm, d), lambda i: (i, 0)),
                  pl.BlockSpec((1, d), lambda i: (0, 0)),
                  pl.BlockSpec((1, d), lambda i: (0, 0))],
        out_specs=[pl.BlockSpec((tm, d), lambda i: (i, 0)),
                   pl.BlockSpec((tm, d), lambda i: (i, 0))],
        out_shape=[jax.ShapeDtypeStruct((n, d), F32),
                   jax.ShapeDtypeStruct((n, d), BF16)],
        compiler_params=_cparams(("parallel",)),
        name="proj_residual_ln",
    )(a, w, x, g, b)


def _causal_conv3(h, tail, cw):
    w0, w1, w2 = cw[0:1, :], cw[1:2, :], cw[2:3, :]
    body = w0 * pltpu.roll(h, 2, axis=0) + w1 * pltpu.roll(h, 1, axis=0) + w2 * h
    head = h[0:SUBLANES, :]
    r = lax.broadcasted_iota(jnp.int32, head.shape, 0)
    hm1 = jnp.where(r < 1, pltpu.roll(tail, 1, axis=0), pltpu.roll(head, 1, axis=0))
    hm2 = jnp.where(r < 2, pltpu.roll(tail, 2, axis=0), pltpu.roll(head, 2, axis=0))
    out_head = w0 * hm2 + w1 * hm1 + w2 * head
    return jnp.concatenate([out_head, body[SUBLANES:, :]], axis=0)


def _ffn_up_kernel(x_ref, wg_ref, wv_ref, cg_ref, cv_ref, o_ref, tg_ref, tv_ref, *h_refs,
                   tiles_per_seq, sub):
    n_chunks = x_ref.shape[0] // sub
    hg_refs, hv_refs = h_refs[:n_chunks], h_refs[n_chunks:]
    first = (pl.program_id(1) % tiles_per_seq) == 0
    body_rows = slice(SUBLANES, SUBLANES + sub)

    def matmuls(c):
        x = x_ref[c * sub:(c + 1) * sub, :]
        hg_refs[c][body_rows, :] = _dot(x, wg_ref[...])
        hv_refs[c][body_rows, :] = _dot(x, wv_ref[...])

    def conv(h_ref, cw):
        return (cw[0:1, :] * h_ref[SUBLANES - 2:SUBLANES - 2 + sub, :]
                + cw[1:2, :] * h_ref[SUBLANES - 1:SUBLANES - 1 + sub, :]
                + cw[2:3, :] * h_ref[body_rows, :])

    def gate(c):
        if c == 0:
            hg_refs[0][0:SUBLANES, :] = jnp.where(first, 0.0, tg_ref[...])
            hv_refs[0][0:SUBLANES, :] = jnp.where(first, 0.0, tv_ref[...])
        else:
            hg_refs[c][0:SUBLANES, :] = hg_refs[c - 1][sub:sub + SUBLANES, :]
            hv_refs[c][0:SUBLANES, :] = hv_refs[c - 1][sub:sub + SUBLANES, :]
        g = conv(hg_refs[c], cg_ref[...])
        v = conv(hv_refs[c], cv_ref[...])
        half_g = 0.5 * g
        o_ref[c * sub:(c + 1) * sub, :] = ((half_g + half_g * jnp.tanh(half_g)) * v).astype(o_ref.dtype)

    matmuls(0)
    for c in range(n_chunks):
        if c + 1 < n_chunks:
            matmuls(c + 1)
        gate(c)
    tg_ref[...] = hg_refs[n_chunks - 1][sub:sub + SUBLANES, :]
    tv_ref[...] = hv_refs[n_chunks - 1][sub:sub + SUBLANES, :]


def _ffn_up(xb, w_up, conv_w, seq, tm, tn, sub):
    n, k = xb.shape
    dff = w_up.shape[1] // 2
    ncol = dff // tn
    return pl.pallas_call(
        functools.partial(_ffn_up_kernel, tiles_per_seq=seq // tm, sub=sub),
        grid=(ncol, n // tm),
        in_specs=[pl.BlockSpec((tm, k), lambda j, i: (i, 0)),
                  pl.BlockSpec((k, tn), lambda j, i: (0, j)),
                  pl.BlockSpec((k, tn), lambda j, i, ncol=ncol: (0, j + ncol)),
                  pl.BlockSpec((CONV_W, tn), lambda j, i: (0, j)),
                  pl.BlockSpec((CONV_W, tn), lambda j, i, ncol=ncol: (0, j + ncol))],
        out_specs=pl.BlockSpec((tm, tn), lambda j, i: (i, j)),
        out_shape=jax.ShapeDtypeStruct((n, dff), BF16),
        scratch_shapes=([pltpu.VMEM((SUBLANES, tn), F32)] * 2
                        + [pltpu.VMEM((sub + SUBLANES, tn), F32)] * (2 * (tm // sub))),
        compiler_params=_cparams(("arbitrary", "arbitrary")),
        name="ffn_up_conv_gate",
    )(xb, w_up, w_up, conv_w, conv_w)


def _gelu_tanh(x):
    return 0.5 * x * (1.0 + jnp.tanh(math.sqrt(2.0 / math.pi) * (x + 0.044715 * (x * x * x))))


def _group_mean(v, avg):
    hi = v.astype(BF16)
    lo = (v - hi.astype(F32)).astype(BF16)
    return _dot(hi, avg) + _dot(lo, avg)


def _odd_mixer_kernel(x_ref, w_ref, sg_ref, sw_ref, sb_ref, cw_ref, wo_ref, xf_ref, g_ref, b_ref,
                      xo_ref, xb_ref, o_ref, tail_ref, *, tiles_per_seq, alpha):
    tm = x_ref.shape[0]
    W = MIX_HALF
    first = (pl.program_id(0) % tiles_per_seq) == 0
    x = x_ref[...]
    lane = lax.broadcasted_iota(jnp.int32, (CHUNK, LANES), 1)
    lo_half = lane < HEAD_DIM
    r = lax.broadcasted_iota(jnp.int32, (W, W), 0) // HEAD_DIM
    c = lax.broadcasted_iota(jnp.int32, (W, W), 1) // HEAD_DIM
    avg = jnp.where(r == c, 1.0 / HEAD_DIM, 0.0).astype(BF16)
    tri = (lax.broadcasted_iota(jnp.int32, (CHUNK, CHUNK), 1)
           <= lax.broadcasted_iota(jnp.int32, (CHUNK, CHUNK), 0))

    u = _gelu_tanh(_dot(x, w_ref[:, 0:W]))
    v = _gelu_tanh(_dot(x, w_ref[:, W:2 * W]))
    vc = v - _group_mean(v, avg)
    var = _group_mean(vc * vc, avg)
    vn = (vc * lax.rsqrt(var + LN_EPS) * sg_ref[...]).astype(BF16)
    for n in range(tm // CHUNK):
        rows = slice(n * CHUNK, (n + 1) * CHUNK)
        for p in range(N_HEADS // 2):
            cols = slice(p * LANES, (p + 1) * LANES)
            vp = vn[rows, cols]
            w_e = jnp.where(tri, sw_ref[2 * p], 0.0).astype(BF16)
            w_o = jnp.where(tri, sw_ref[2 * p + 1], 0.0).astype(BF16)
            mix = (_dot(w_e, jnp.where(lo_half, vp, 0)) + _dot(w_o, jnp.where(lo_half, 0, vp))
                   + sb_ref[:, cols])
            o_ref[rows, cols] = (u[rows, cols] * mix).astype(o_ref.dtype)

    g_b = _dot(x, w_ref[:, 2 * W:3 * W])
    y = _dot(x, w_ref[:, 3 * W:4 * W]) * _dot(x, w_ref[:, 4 * W:5 * W])
    conv = _causal_conv3(y, jnp.where(first, 0.0, tail_ref[...]), cw_ref[...])
    tail_ref[...] = y[tm - SUBLANES:tm, :]
    o_ref[:, W:2 * W] = (g_b * conv).astype(o_ref.dtype)

    out = _layer_norm_rows(alpha * xf_ref[...] + _dot(o_ref[...], wo_ref[...]), g_ref[...], b_ref[...])
    xo_ref[...] = out
    xb_ref[...] = out.astype(BF16)


def _odd_mixer(xb, w_in, sgu_g, sgu_w, sgu_b_tile, conv_w, w_o, xf, g, b, alpha, seq, tm):
    n, k = xb.shape
    W = MIX_HALF
    d = w_o.shape[1]
    const2 = lambda shape: pl.BlockSpec(shape, lambda i: (0, 0))
    row_tile = lambda width: pl.BlockSpec((tm, width), lambda i: (i, 0))
    return pl.pallas_call(
        functools.partial(_odd_mixer_kernel, tiles_per_seq=seq // tm, alpha=alpha),
        grid=(n // tm,),
        in_specs=[row_tile(k), const2((k, 5 * W)), const2((1, W)),
                  pl.BlockSpec((N_HEADS, CHUNK, CHUNK), lambda i: (0, 0, 0)),
                  const2((CHUNK, W)), const2((CONV_W, W)),
                  const2((2 * W, d)), row_tile(d), const2((1, d)), const2((1, d))],
        out_specs=[row_tile(d), row_tile(d)],
        out_shape=[jax.ShapeDtypeStruct((n, d), F32), jax.ShapeDtypeStruct((n, d), BF16)],
        scratch_shapes=[pltpu.VMEM((tm, 2 * W), BF16),
                        pltpu.VMEM((SUBLANES, W), F32)],
        compiler_params=_cparams(("arbitrary",)),
        name="odd_mixer",
    )(xb, w_in, sgu_g, sgu_w, sgu_b_tile, conv_w, w_o, xf, g, b)


def _t5_bucket_np(dist):
    max_exact = N_BUCKETS // 2
    n = np.maximum(dist, 0)
    nf = np.maximum(n, 1).astype(np.float32)
    large = max_exact + (np.log(nf / max_exact) / np.float32(math.log(T5_MAX_DIST / max_exact))
                         * (N_BUCKETS - max_exact)).astype(np.int32)
    large = np.minimum(large, N_BUCKETS - 1)
    return np.where(n < max_exact, n, large)


def _bias_tables(rel_bias):
    T = BLK
    s = np.arange(T)[:, None]
    t = np.arange(T)[None, :]
    bucket = _t5_bucket_np(np.stack([t + d * T - s for d in range(3)]))
    far_bucket = N_BUCKETS - 1
    assert (bucket[2] == far_bucket).all()
    assert (_t5_bucket_np(np.arange(T + 1, 64 * T)) == far_bucket).all()
    rb = rel_bias.astype(F32) * LOG2E
    tiles = sum(jnp.where(jnp.asarray(bucket == b)[..., None], rb[b], 0.0) for b in range(N_BUCKETS))
    tiles = jnp.swapaxes(tiles, 2, 3).reshape(3, T, N_HEADS * T)
    far = jnp.tile(jnp.repeat(rb[far_bucket], T)[None, :], (1, 2))
    return tiles, far


def _placement_constants():
    H, L = N_HEADS, LANES
    pq = np.zeros((3 * L, H * L), np.float32)
    cq = np.zeros((1, H * L), np.float32)
    pk = np.zeros((3 * L, (H // 2) * L), np.float32)
    ck = np.zeros((1, (H // 2) * L), np.float32)
    for h in range(H):
        p, e = divmod(h, 2)
        for piece in range(3):
            pq[piece * L + h, h * L + 6 + piece] = 1.0
            cq[0, h * L + 3 * e + piece] = 1.0
            pk[piece * L + h, p * L + 3 * e + piece] = -1.0
            ck[0, p * L + 6 + piece] = 1.0
    return jnp.asarray(pq, BF16), jnp.asarray(cq), jnp.asarray(pk, BF16), jnp.asarray(ck)


def _even_weights(w_in, w_uk, w_uv):
    hd, H = HEAD_DIM, N_HEADS
    widths = (H * hd, KV_RANK, H * hd, hd, H, H * hd, H * hd, H * hd, H)
    offs = np.cumsum((0,) + widths)
    seg = lambda s: w_in[:, offs[s]:offs[s + 1]]
    q_a, c_kv, q_idx, k_idx, w_idx, q_b, k_b, v_b, f_in = [seg(s) for s in range(9)]
    scale = hd ** -0.5
    w_big = jnp.concatenate([q_a, q_idx * scale, q_b * (scale * LOG2E), k_b, v_b, c_kv,
                             k_idx, k_idx], axis=1).astype(BF16)
    pad = jnp.zeros((w_in.shape[0], LANES - 2 * H), w_in.dtype)
    w_small = jnp.concatenate([w_idx, f_in, pad], axis=1).astype(BF16)
    wuk_t = jnp.swapaxes(w_uk, 1, 2) * (scale * LOG2E)
    odd = (np.arange(H) % 2 == 1)[:, None, None]
    zk, zv = jnp.zeros_like(wuk_t), jnp.zeros_like(w_uv)
    wuk_pad = jnp.where(odd, jnp.concatenate([zk, wuk_t], axis=1), jnp.concatenate([wuk_t, zk], axis=1))
    wuv_pad = jnp.where(odd, jnp.concatenate([zv, w_uv], axis=2), jnp.concatenate([w_uv, zv], axis=2))
    return w_big, w_small, wuk_pad.astype(BF16), wuv_pad.astype(BF16)


def kernel(x, ln_g, ln_b, rel_bias, ev_w_in, ev_w_uk, ev_w_uv, ev_b_f, ev_w_o,
           od_w_in, od_sgu_g, od_sgu_w, od_sgu_b, od_conv_w, od_w_o,
           ffn_w_up, ffn_conv_w, ffn_w_down):
    bsz, seq, d = x.shape
    depth = ln_g.shape[0]
    n = bsz * seq
    alpha = (2.0 * depth) ** 0.25
    k_sel = min(TOPK_MAX, seq // 4)
    H = N_HEADS
    assert seq % QUERIES == 0 and seq % CHUNK == 0
    tm_ffn = min(1024, seq)

    bias_tiles, bias_far = _bias_tables(rel_bias)
    place = _placement_constants()
    xf = x.reshape(n, d)
    xb = xf
    for layer in range(depth):
        j = layer // 2
        g0, b0 = ln_g[layer, 0][None, :], ln_b[layer, 0][None, :]
        g1, b1 = ln_g[layer, 1][None, :], ln_b[layer, 1][None, :]
        if layer % 2 == 0:
            w_big, w_small, wuk_pad, wuv_pad = _even_weights(ev_w_in[j], ev_w_uk[j], ev_w_uv[j])
            hb, hs = _even_proj(xb, w_big, w_small, tm_ffn)
            hb = hb.reshape(bsz, seq, -1)
            hs_t = jnp.swapaxes(hs.reshape(bsz, seq, -1)[:, :, 0:2 * H], 1, 2)
            cum_f = _cumf(hs_t[:, H:2 * H], ev_b_f[j][:, None])
            f_rows = jnp.pad(jnp.swapaxes(cum_f, 1, 2), ((0, 0), (0, 0), (0, LANES - H)))
            mix = _attention(hb, hs_t[:, 0:H], f_rows, wuk_pad, wuv_pad, bias_tiles, bias_far,
                             place, k_sel)
            xf, xb = _proj_ln(mix.reshape(n, d), ev_w_o[j].astype(BF16), xf, g0, b0, alpha, tm_ffn)
        else:
            sgu_b_tile = jnp.repeat(jnp.swapaxes(od_sgu_b[j], 0, 1), HEAD_DIM, axis=1)
            xf, xb = _odd_mixer(xb, od_w_in[j].astype(BF16), od_sgu_g[j].reshape(1, -1), od_sgu_w[j],
                                sgu_b_tile, od_conv_w[j], od_w_o[j].astype(BF16), xf, g0, b0, alpha,
                                seq, min(512, seq))
        act = _ffn_up(xb, ffn_w_up[layer].astype(BF16), ffn_conv_w[layer], seq, min(2048, seq), 256,
                      min(512, seq))
        xf, xb = _proj_ln(act, ffn_w_down[layer].astype(BF16), xf, g1, b1, alpha, tm_ffn)
    return xf.reshape(bsz, seq, d)
```

```python
import functools
import math

import numpy as np
import jax
import jax.numpy as jnp
from jax import lax
from jax.experimental import pallas as pl
from jax.experimental.pallas import tpu as pltpu

F32 = jnp.float32
BF16 = jnp.bfloat16

HEAD_DIM = 64
N_HEADS = 8
KV_RANK = 128
TOPK_MAX = 256
N_BUCKETS = 32
T5_MAX_DIST = 128
CHUNK = 128
CONV_W = 3
LN_EPS = 1e-5
MIX_HALF = N_HEADS * HEAD_DIM

LANES = 128
SUBLANES = 8
BF16_ROWS = 16
VMEM_LIMIT = 56 * 1024 * 1024

BLK = 128
QUERIES = 2 * BLK
PV_ROWS = KV_RANK + BF16_ROWS
NEG_BIG = -1e30
MASKED = -(2.0 ** 100)
F32_MAX = float(np.finfo(np.float32).max)
LOG2E = math.log2(math.e)
SEARCH_UNROLL = 3


def _cparams(sem):
    return pltpu.CompilerParams(dimension_semantics=sem, vmem_limit_bytes=VMEM_LIMIT)


def _dot(a, b):
    return jnp.dot(a, b, preferred_element_type=F32)


def _dot_nt(a, b):
    return lax.dot_general(a, b, (((1,), (1,)), ((), ())), preferred_element_type=F32)


def _layer_norm_rows(y, g, b):
    mu = jnp.mean(y, axis=-1, keepdims=True)
    yc = y - mu
    var = jnp.mean(yc * yc, axis=-1, keepdims=True)
    return yc * lax.rsqrt(var + LN_EPS) * g + b


def _tree(op, x):
    while x.shape[0] > 1:
        half = x.shape[0] // 2
        x = op(x[:half], x[half:])
    return x[0]


def _bf16_pieces(v):
    hi = v.astype(BF16)
    r1 = v - hi.astype(F32)
    mid = r1.astype(BF16)
    lo = (r1 - mid.astype(F32)).astype(BF16)
    return jnp.concatenate([hi, mid, lo], axis=1)


def _even_proj_kernel(x_ref, wb_ref, ws_ref, hb_ref, hs_ref):
    x = x_ref[...].astype(BF16)
    hb_ref[...] = _dot(x, wb_ref[...]).astype(hb_ref.dtype)
    hs_ref[...] = _dot(x, ws_ref[...])


def _even_proj(xb, w_big, w_small, tm):
    n, k = xb.shape
    mb = w_big.shape[1]
    ms = w_small.shape[1]
    return pl.pallas_call(
        _even_proj_kernel,
        grid=(n // tm,),
        in_specs=[pl.BlockSpec((tm, k), lambda i: (i, 0)),
                  pl.BlockSpec((k, mb), lambda i: (0, 0)),
                  pl.BlockSpec((k, ms), lambda i: (0, 0))],
        out_specs=[pl.BlockSpec((tm, mb), lambda i: (i, 0)),
                   pl.BlockSpec((tm, ms), lambda i: (i, 0))],
        out_shape=[jax.ShapeDtypeStruct((n, mb), BF16),
                   jax.ShapeDtypeStruct((n, ms), F32)],
        compiler_params=_cparams(("parallel",)),
        name="even_proj",
    )(xb, w_big, w_small)


def _cumf_kernel(f_ref, b_ref, o_ref):
    seq = f_ref.shape[-1]
    row = lax.broadcasted_iota(jnp.int32, (LANES, LANES), 0)
    col = lax.broadcasted_iota(jnp.int32, (LANES, LANES), 1)
    upper = jnp.where(row <= col, 1.0, 0.0).astype(F32)
    carry = jnp.zeros((f_ref.shape[0], 1), F32)
    for c in range(seq // LANES):
        z = f_ref[:, c * LANES:(c + 1) * LANES] + b_ref[...]
        lf = jnp.minimum(z, 0.0) - jnp.log1p(jnp.exp(-jnp.abs(z)))
        cs = jnp.dot(lf, upper, preferred_element_type=F32,
                     precision=lax.Precision.HIGHEST) + carry
        o_ref[:, c * LANES:(c + 1) * LANES] = cs
        carry = cs[:, LANES - 1:LANES]


def _cumf(f_t, b_f):
    bsz, nh, seq = f_t.shape
    return pl.pallas_call(
        _cumf_kernel,
        grid=(bsz,),
        in_specs=[pl.BlockSpec((None, nh, seq), lambda b: (b, 0, 0)),
                  pl.BlockSpec((nh, 1), lambda b: (0, 0))],
        out_specs=pl.BlockSpec((None, nh, seq), lambda b: (b, 0, 0)),
        out_shape=jax.ShapeDtypeStruct((bsz, nh, seq), F32),
        compiler_params=_cparams(("parallel",)),
        name="fox_cumf",
    )(f_t, b_f)


def _attn_kernel(qa_ref, qi_ref, qb_ref, wt_ref, fq_ref, kb_ref, vb_ref, ckv_ref, kidx_ref, fk_ref,
                 wuk_ref, wuv_ref, bias_ref, bfar_ref, pq_ref, cq_ref, pk_ref, ck_ref,
                 o_ref,
                 score_ref, lhsd_ref, ct_ref, kaug_ref, vt_ref, qsel_ref, qaug_ref, qim_ref,
                 m_ref, acc_ref, s_ref, tie_ref, *, k_sel):
    T = BLK
    Q = QUERIES
    H = N_HEADS
    NP = H // 2
    NT = 2 * NP
    TW = 2 * T
    g = pl.program_id(1)
    i0 = 2 * g
    nblk = ct_ref.shape[0]
    lane = lax.broadcasted_iota(jnp.int32, (T, LANES), 1)
    lo_half = lane < HEAD_DIM
    ones_rows = jnp.ones((BF16_ROWS, T), BF16)
    causal_qq = (lax.broadcasted_iota(jnp.int32, (Q, Q), 0) <= lax.broadcasted_iota(jnp.int32, (Q, Q), 1))

    @pl.when(g == 0)
    def _per_batch():
        def prep(j, c):
            r = pl.ds(pl.multiple_of(j * T, T), T)
            cb = ckv_ref[r, :]
            lhsd_ref[0, r, 0:KV_RANK] = cb
            lhsd_ref[1, r, 0:KV_RANK] = cb
            ct_ref[j, 0:KV_RANK, :] = cb.astype(F32).T.astype(BF16)
            ct_ref[j, KV_RANK:PV_ROWS, :] = ones_rows
            ext = (_dot(_bf16_pieces(fk_ref[r, :] * LOG2E), pk_ref[...]) + ck_ref[...]).astype(BF16)
            for p in range(NP):
                cols = slice(p * LANES, (p + 1) * LANES)
                kaug_ref[p, r, 0:LANES] = kb_ref[r, cols]
                kaug_ref[p, r, LANES:2 * LANES] = ext[:, cols]
                vt_ref[p, j, 0:LANES, :] = vb_ref[r, cols].astype(F32).T.astype(BF16)
                vt_ref[p, j, LANES:PV_ROWS, :] = ones_rows
            return c
        lax.fori_loop(0, nblk, prep, 0)

    w_rows = wt_ref[...] * (H ** -0.5)
    ext_q = (_dot(_bf16_pieces(fq_ref[...] * LOG2E), pq_ref[...]) + cq_ref[...]).astype(BF16)
    eye = jnp.where(lax.broadcasted_iota(jnp.int32, (T, T), 0) == lax.broadcasted_iota(jnp.int32, (T, T), 1),
                    1.0, 0.0).astype(BF16)
    for tt in range(NT):
        hf, p = divmod(tt, NP)
        qrows = slice(hf * T, (hf + 1) * T)
        cols = slice(p * LANES, (p + 1) * LANES)
        qa_pair, qi_pair, qb_pair = qa_ref[qrows, cols], qi_ref[qrows, cols], qb_ref[qrows, cols]
        for e in range(2):
            h = 2 * p + e
            rows = slice(e * T, (e + 1) * T)
            mine = lo_half if e == 0 else jnp.logical_not(lo_half)
            qsel_ref[tt, rows, 0:KV_RANK] = _dot(qa_pair, wuk_ref[h]).astype(BF16)
            qsel_ref[tt, rows, KV_RANK:2 * KV_RANK] = eye
            qaug_ref[tt, rows, 0:LANES] = jnp.where(mine, qb_pair, 0)
            qaug_ref[tt, rows, LANES:2 * LANES] = ext_q[qrows, h * LANES:(h + 1) * LANES]
            qim_ref[tt, rows, :] = jnp.where(mine, qi_pair, 0)

    def reset_state():
        m_ref[...] = jnp.full(m_ref.shape, NEG_BIG, F32)
        acc_ref[...] = jnp.zeros(acc_ref.shape, F32)

    def tile_cols(tt):
        return slice(tt * TW, (tt + 1) * TW)

    def online_update(tt, s, pv_lhs):
        cols = tile_cols(tt)
        m_old = m_ref[:, cols]
        s_max = _tree(jnp.maximum, s.reshape(-1, SUBLANES, TW))
        m_new = jnp.maximum(m_old, jnp.max(s_max, axis=0, keepdims=True))
        alpha = jnp.exp2(m_old - m_new)
        pr = jnp.exp2((s - m_new[0:1, :]).astype(BF16))
        acc_ref[:, cols] = acc_ref[:, cols] * alpha[0:1, :] + _dot(pv_lhs, pr)
        m_ref[:, cols] = m_new

    def key_rows(j, nblocks):
        return pl.ds(pl.multiple_of(j * T, T), nblocks * T)

    def slot_rows(nblocks):
        return slice(0, nblocks * T)

    def blocks_t(load, j, nblocks):
        return load(j) if nblocks == 1 else jnp.concatenate([load(j + b) for b in range(nblocks)], axis=1)

    def staged_steps(j0, n_steps, logits, absorb):
        logits(j0, 2, 0)
        for k in range(n_steps):
            if k + 1 < n_steps:
                logits(j0 + 2 * (k + 1), 2, (k + 1) % 2)
            absorb(j0 + 2 * k, 2, k % 2)

    def for_far_blocks(n_far, logits, absorb):
        n_oct = lax.shift_right_logical(n_far, 3)

        def oct_body(jj, c):
            staged_steps(8 * jj, 4, logits, absorb)
            return c
        lax.fori_loop(0, n_oct, oct_body, 0)

        @pl.when((n_far & 4) != 0)
        def _quad():
            staged_steps(8 * n_oct, 2, logits, absorb)

        @pl.when((n_far & 2) != 0)
        def _pair():
            staged_steps(8 * n_oct + (n_far & 4), 1, logits, absorb)

        @pl.when((n_far & 1) != 0)
        def _single():
            logits(n_far - 1, 1, 0)
            absorb(n_far - 1, 1, 0)

    def idx_scores(j, nblocks):
        k_blk = kidx_ref[key_rows(j, nblocks), :]
        acc = [jnp.zeros((nblocks * T, T), F32), jnp.zeros((nblocks * T, T), F32)]
        for tt in range(NT):
            hf, p = divmod(tt, NP)
            s = _dot_nt(k_blk, qim_ref[tt])
            for e in range(2):
                h = 2 * p + e
                w_h = w_rows[h:h + 1, hf * T:(hf + 1) * T]
                acc[hf] = acc[hf] + w_h * jnp.maximum(s[:, e * T:(e + 1) * T], 0.0)
        return jnp.concatenate(acc, axis=1)

    def fox_logits(j, nblocks, slot, causal=False):
        for tt in range(NT):
            hf, p = divmod(tt, NP)
            s = _dot_nt(kaug_ref[p, key_rows(j, nblocks), :], qaug_ref[tt])
            if causal:
                ok = causal_qq[:, hf * T:(hf + 1) * T]
                s = jnp.where(jnp.concatenate([ok, ok], axis=1), s, MASKED)
            s_ref[slot, slot_rows(nblocks), tile_cols(tt)] = s

    def fox_absorb(j, nblocks, slot):
        for tt in range(NT):
            p = tt % NP
            online_update(tt, s_ref[slot, slot_rows(nblocks), tile_cols(tt)],
                          blocks_t(lambda jb: vt_ref[p, jb], j, nblocks))

    def idx_fox_logits(j, nblocks, slot):
        score_ref[pl.ds(j, nblocks)] = idx_scores(j, nblocks).reshape(nblocks, T, Q)
        fox_logits(j, nblocks, slot)

    reset_state()
    for_far_blocks(jnp.maximum(i0 - 2, 0), idx_fox_logits, fox_absorb)

    def diag_logits(slot):
        score_ref[pl.ds(i0, 2)] = jnp.where(causal_qq, idx_scores(i0, 2), -jnp.inf).reshape(2, T, Q)
        fox_logits(i0, 2, slot, causal=True)

    @pl.when(g >= 1)
    def _idx_fox_tail():
        idx_fox_logits(i0 - 2, 2, 0)
        diag_logits(1)
        fox_absorb(i0 - 2, 2, 0)
        fox_absorb(i0, 2, 1)

    @pl.when(g == 0)
    def _idx_fox_first():
        diag_logits(0)
        fox_absorb(0, 2, 0)

    row_lo = lax.broadcasted_iota(jnp.int32, (LANES, T), 0) < HEAD_DIM
    for tt in range(NT):
        hf, p = divmod(tt, NP)
        c0 = slice(tt * TW, tt * TW + T)
        c1 = slice(tt * TW + T, (tt + 1) * TW)
        o_t = jnp.where(row_lo, acc_ref[0:LANES, c0] / acc_ref[LANES:LANES + 1, c0],
                        acc_ref[0:LANES, c1] / acc_ref[LANES:LANES + 1, c1])
        o_ref[hf * T:(hf + 1) * T, MIX_HALF + p * LANES:MIX_HALF + (p + 1) * LANES] = (
            o_t.T.astype(o_ref.dtype))

    kf = float(k_sel)
    t_pos = (g * Q + lax.broadcasted_iota(jnp.int32, (SUBLANES, Q), 1)).astype(F32)
    searching = (t_pos + 1.0) > kf
    G2 = 2 * T // SUBLANES
    n_pair = g + 1
    key_in_pair = (lax.broadcasted_iota(jnp.int32, (G2, SUBLANES, Q), 0) * SUBLANES
                   + lax.broadcasted_iota(jnp.int32, (G2, SUBLANES, Q), 1))

    def rep(v):
        return jnp.broadcast_to(v, (SUBLANES, Q))

    def score_pair(jj):
        return score_ref[pl.ds(2 * jj, 2)].reshape(G2, SUBLANES, Q)

    def count(*pred_fns, load=score_pair):
        def body(jj, cs):
            s = load(jj)
            return tuple(c + _tree(jnp.add, jnp.where(f(s, jj), 1.0, 0.0)) for c, f in zip(cs, pred_fns))
        zero = jnp.zeros((SUBLANES, Q), F32)
        cs = lax.fori_loop(0, n_pair, body, (zero,) * len(pred_fns))
        return tuple(rep(jnp.sum(c, axis=0, keepdims=True)) for c in cs)

    def count_ge(thr):
        return count(lambda s, jj: s >= thr[None])[0]

    def select_threshold():
        def first_body(jj, carry):
            mn, mx, ge0, gt0 = carry
            s = score_pair(jj)
            return (jnp.minimum(mn, _tree(jnp.minimum, jnp.where(s == -jnp.inf, jnp.inf, s))),
                    jnp.maximum(mx, _tree(jnp.maximum, s)),
                    ge0 + _tree(jnp.add, jnp.where(s >= 0.0, 1.0, 0.0)),
                    gt0 + _tree(jnp.add, jnp.where(s > 0.0, 1.0, 0.0)))

        zero = jnp.zeros((SUBLANES, Q), F32)
        mn, mx, c_ge0, c_gt0 = lax.fori_loop(0, n_pair, first_body, (zero + jnp.inf, zero - jnp.inf, zero, zero))
        mn = rep(jnp.min(mn, axis=0, keepdims=True))
        mx = rep(jnp.max(mx, axis=0, keepdims=True))
        c_ge0 = rep(jnp.sum(c_ge0, axis=0, keepdims=True))
        c_gt0 = rep(jnp.sum(c_gt0, axis=0, keepdims=True))
        above_mx = mx + jnp.maximum(jnp.abs(mx) * 2.0 ** -20, 1e-30)
        thr_ge0 = c_ge0 >= kf
        thr_gt0 = c_gt0 >= kf
        lo0 = jnp.where(thr_ge0, 0.0, mn)
        cnt_lo0 = jnp.where(thr_ge0, c_ge0, t_pos + 1.0)
        hi0 = jnp.where(thr_gt0, above_mx, 0.0)
        cnt_hi0 = jnp.where(thr_gt0, 0.0, jnp.where(thr_ge0, c_gt0, c_ge0))
        undecided = thr_gt0 | jnp.logical_not(thr_ge0)
        active0 = jnp.where(searching & undecided & (cnt_lo0 > kf), 1.0, 0.0)

        def cond(st):
            return jnp.logical_and(st[5] > 0.0, st[6] < 2200)

        def bisect(st):
            lo, hi, cnt_lo, cnt_hi, active = st
            mid = 0.5 * lo + 0.5 * hi
            adjacent = (mid <= lo) | (mid >= hi)
            c = count_ge(mid)
            act = active > 0.0
            go_lo = act & jnp.logical_not(adjacent) & (c >= kf)
            go_hi = act & jnp.logical_not(adjacent) & (c < kf)
            lo = jnp.where(go_lo, mid, lo)
            cnt_lo = jnp.where(go_lo, c, cnt_lo)
            hi = jnp.where(go_hi, mid, hi)
            cnt_hi = jnp.where(go_hi, c, cnt_hi)
            active = jnp.where(act & jnp.logical_not(adjacent) & (cnt_lo > kf), 1.0, 0.0)
            return lo, hi, cnt_lo, cnt_hi, active

        def body(st):
            inner = st[:5]
            for _ in range(SEARCH_UNROLL):
                inner = bisect(inner)
            return inner + (jnp.max(inner[4]), st[6] + SEARCH_UNROLL)

        lo, hi, cnt_lo, cnt_hi, _, _, _ = lax.while_loop(
            cond, body, (lo0, hi0, cnt_lo0, cnt_hi0, active0, jnp.max(active0), 0))
        thr = jnp.where(searching, lo, -F32_MAX)

        tied = searching & (cnt_lo > kf)
        need = kf - cnt_hi

        @pl.when(jnp.max(jnp.where(tied, 1.0, 0.0)) > 0.0)
        def _break_ties():
            def tie_pair(jj):
                return tie_ref[pl.ds(2 * jj, 2)].reshape(G2, SUBLANES, Q)

            def mark_body(jj, c):
                key_index = (2 * jj * T + key_in_pair).astype(F32)
                tie_ref[pl.ds(2 * jj, 2)] = (
                    jnp.where(score_pair(jj) == thr[None], key_index, jnp.inf).reshape(2, T, Q))
                return c
            lax.fori_loop(0, n_pair, mark_body, 0)

            def count_tied_upto(jcut):
                return count(lambda e, jj: e <= jcut[None], load=tie_pair)[0]

            def bs_body(_, st):
                jl, jh = st
                jm = jnp.floor(0.5 * (jl + jh))
                ok = count_tied_upto(jm) >= need
                return jnp.where(ok, jl, jm), jnp.where(ok, jm, jh)

            n_keys = (2 * n_pair * T).astype(F32)
            jl0 = jnp.full((SUBLANES, Q), -1.0, F32)
            jh0 = jnp.zeros((SUBLANES, Q), F32) + (n_keys - 1.0)
            n_steps = int(math.ceil(math.log2(nblk * T))) + 1
            _, jcut = lax.fori_loop(0, n_steps, bs_body, (jl0, jh0))

            def drop_body(jj, c):
                e = tie_pair(jj)
                drop = tied[None] & (e > jcut[None]) & (e < jnp.inf)
                score_ref[pl.ds(2 * jj, 2)] = jnp.where(drop, -jnp.inf, score_pair(jj)).reshape(2, T, Q)
                return c
            lax.fori_loop(0, n_pair, drop_body, 0)

        return thr

    thr = lax.cond((i0 + 2) * T > k_sel, select_threshold,
                   lambda: jnp.full((SUBLANES, Q), -F32_MAX, F32))

    def mask_body(jj, c):
        add = jnp.where(score_pair(jj) >= thr[None], 0.0, MASKED).reshape(2 * T, Q).astype(BF16)
        for hf in range(2):
            lhsd_ref[hf, key_rows(2 * jj, 2), KV_RANK:2 * KV_RANK] = add[:, hf * T:(hf + 1) * T]
        return c

    lax.fori_loop(0, n_pair, mask_body, 0)

    def dsa_logits(j, nblocks, slot, bias=None):
        for tt in range(NT):
            hf, p = divmod(tt, NP)
            s = _dot_nt(lhsd_ref[hf, key_rows(j, nblocks), :], qsel_ref[tt])
            if bias is not None:
                s = s + bias(hf, slice(p * TW, (p + 1) * TW))
            s_ref[slot, slot_rows(nblocks), tile_cols(tt)] = s

    def dsa_absorb(j, nblocks, slot):
        pv_lhs = blocks_t(lambda jb: ct_ref[jb], j, nblocks)
        for tt in range(NT):
            online_update(tt, s_ref[slot, slot_rows(nblocks), tile_cols(tt)], pv_lhs)

    def bias_before(hf, cols):
        return jnp.concatenate([bias_ref[2, :, cols], bias_ref[1 + hf, :, cols]], axis=0)

    def bias_diag(hf, cols):
        first = bias_ref[hf, :, cols]
        return jnp.concatenate([first, bias_ref[0, :, cols]], axis=0)

    reset_state()
    for_far_blocks(jnp.maximum(i0 - 2, 0), dsa_logits, dsa_absorb)
    m_ref[...] = m_ref[...] + bfar_ref[...]

    @pl.when(g >= 1)
    def _dsa_tail():
        dsa_logits(i0 - 2, 2, 0, bias_before)
        dsa_logits(i0, 2, 1, bias_diag)
        dsa_absorb(i0 - 2, 2, 0)
        dsa_absorb(i0, 2, 1)

    @pl.when(g == 0)
    def _dsa_first():
        dsa_logits(0, 2, 0, bias_diag)
        dsa_absorb(0, 2, 0)

    for tt in range(NT):
        hf, p = divmod(tt, NP)
        o_pair = jnp.zeros((T, LANES), F32)
        for e in range(2):
            cols = slice(tt * TW + e * T, tt * TW + (e + 1) * T)
            o_t = acc_ref[0:KV_RANK, cols] / acc_ref[KV_RANK:KV_RANK + 1, cols]
            o_pair = o_pair + _dot(o_t.T.astype(BF16), wuv_ref[2 * p + e])
        o_ref[hf * T:(hf + 1) * T, p * LANES:(p + 1) * LANES] = o_pair.astype(o_ref.dtype)


_EV_QA, _EV_QI, _EV_QB, _EV_KB, _EV_VB = 0, 1, 2, 3, 4
_EV_CKV, _EV_KIDX = 20, 21


def _attention(hb, w_t, f_rows, wuk_pad, wuv_pad, bias_tiles, bias_far, place, k_sel):
    bsz, seq, _ = hb.shape
    T, Q, H = BLK, QUERIES, N_HEADS
    nblk = seq // T
    lanes = 2 * H * T
    pq, cq, pk, ck = place
    wide = lambda c: pl.BlockSpec((None, Q, MIX_HALF), lambda b, i, c=c: (b, i, c))
    once = dict(pipeline_mode=pl.Buffered(1))
    full_wide = lambda c: pl.BlockSpec((None, seq, MIX_HALF), lambda b, i, c=c: (b, 0, c), **once)
    full_narrow = lambda c: pl.BlockSpec((None, seq, LANES), lambda b, i, c=c: (b, 0, c), **once)
    const = lambda a: pl.BlockSpec(a.shape, lambda b, i, nd=a.ndim: (0,) * nd, **once)
    return pl.pallas_call(
        functools.partial(_attn_kernel, k_sel=k_sel),
        grid=(bsz, seq // Q),
        in_specs=[wide(_EV_QA), wide(_EV_QI), wide(_EV_QB),
                  pl.BlockSpec((None, H, Q), lambda b, i: (b, 0, i)),
                  pl.BlockSpec((None, Q, LANES), lambda b, i: (b, i, 0)),
                  full_wide(_EV_KB), full_wide(_EV_VB),
                  full_narrow(_EV_CKV), full_narrow(_EV_KIDX),
                  pl.BlockSpec((None, seq, LANES), lambda b, i: (b, 0, 0), **once),
                  const(wuk_pad), const(wuv_pad), const(bias_tiles), const(bias_far),
                  const(pq), const(cq), const(pk), const(ck)],
        out_specs=pl.BlockSpec((None, Q, 2 * MIX_HALF), lambda b, i: (b, i, 0)),
        out_shape=jax.ShapeDtypeStruct((bsz, seq, 2 * MIX_HALF), BF16),
        scratch_shapes=[pltpu.VMEM((nblk, T, Q), F32),
                        pltpu.VMEM((2, seq, 2 * KV_RANK), BF16),
                        pltpu.VMEM((nblk, PV_ROWS, T), BF16),
                        pltpu.VMEM((H // 2, seq, 2 * LANES), BF16),
                        pltpu.VMEM((H // 2, nblk, PV_ROWS, T), BF16),
                        pltpu.VMEM((H, 2 * T, 2 * KV_RANK), BF16),
                        pltpu.VMEM((H, 2 * T, 2 * LANES), BF16),
                        pltpu.VMEM((H, 2 * T, LANES), BF16),
                        pltpu.VMEM((SUBLANES, lanes), F32),
                        pltpu.VMEM((PV_ROWS, lanes), F32),
                        pltpu.VMEM((2, 2 * T, lanes), F32),
                        pltpu.VMEM((nblk, T, Q), F32)],
        compiler_params=_cparams(("parallel", "arbitrary")),
        name="dsa_fox_attention",
    )(hb, hb, hb, w_t, f_rows, hb, hb, hb, hb, f_rows, wuk_pad, wuv_pad, bias_tiles, bias_far,
      pq, cq, pk, ck)


def _proj_ln_kernel(a_ref, w_ref, x_ref, g_ref, b_ref, xo_ref, xb_ref, *, alpha, sub):
    for c in range(a_ref.shape[0] // sub):
        rows = slice(c * sub, (c + 1) * sub)
        y = alpha * x_ref[rows, :] + _dot(a_ref[rows, :], w_ref[...])
        out = _layer_norm_rows(y, g_ref[...], b_ref[...])
        xo_ref[rows, :] = out
        xb_ref[rows, :] = out.astype(BF16)


def _proj_ln(a, w, x, g, b, alpha, tm):
    n, k = a.shape
    d = w.shape[1]
    return pl.pallas_call(
        functools.partial(_proj_ln_kernel, alpha=alpha, sub=min(128, tm)),
        grid=(n // tm,),
        in_specs=[pl.BlockSpec((tm, k), lambda i: (i, 0)),
                  pl.BlockSpec((k, d), lambda i: (0, 0)),
                  pl.BlockSpec((tm, d), lambda i: (i, 0)),
                  pl.BlockSpec((1, d), lambda i: (0, 0)),
                  pl.BlockSpec((1, d), lambda i: (0, 0))],
        out_specs=[pl.BlockSpec((tm, d), lambda i: (i, 0)),
                   pl.BlockSpec((tm, d), lambda i: (i, 0))],
        out_shape=[jax.ShapeDtypeStruct((n, d), F32),
                   jax.ShapeDtypeStruct((n, d), BF16)],
        compiler_params=_cparams(("parallel",)),
        name="proj_residual_ln",
    )(a, w, x, g, b)


def _causal_conv3(h, tail, cw):
    w0, w1, w2 = cw[0:1, :], cw[1:2, :], cw[2:3, :]
    body = w0 * pltpu.roll(h, 2, axis=0) + w1 * pltpu.roll(h, 1, axis=0) + w2 * h
    head = h[0:SUBLANES, :]
    r = lax.broadcasted_iota(jnp.int32, head.shape, 0)
    hm1 = jnp.where(r < 1, pltpu.roll(tail, 1, axis=0), pltpu.roll(head, 1, axis=0))
    hm2 = jnp.where(r < 2, pltpu.roll(tail, 2, axis=0), pltpu.roll(head, 2, axis=0))
    out_head = w0 * hm2 + w1 * hm1 + w2 * head
    return jnp.concatenate([out_head, body[SUBLANES:, :]], axis=0)


def _ffn_up_kernel(x_ref, wg_ref, wv_ref, cg_ref, cv_ref, o_ref, tg_ref, tv_ref, *h_refs,
                   tiles_per_seq, sub):
    n_chunks = x_ref.shape[0] // sub
    hg_refs, hv_refs = h_refs[:n_chunks], h_refs[n_chunks:]
    first = (pl.program_id(1) % tiles_per_seq) == 0
    body_rows = slice(SUBLANES, SUBLANES + sub)

    def matmuls(c):
        x = x_ref[c * sub:(c + 1) * sub, :]
        hg_refs[c][body_rows, :] = _dot(x, wg_ref[...])
        hv_refs[c][body_rows, :] = _dot(x, wv_ref[...])

    def conv(h_ref, cw):
        return (cw[0:1, :] * h_ref[SUBLANES - 2:SUBLANES - 2 + sub, :]
                + cw[1:2, :] * h_ref[SUBLANES - 1:SUBLANES - 1 + sub, :]
                + cw[2:3, :] * h_ref[body_rows, :])

    def gate(c):
        if c == 0:
            hg_refs[0][0:SUBLANES, :] = jnp.where(first, 0.0, tg_ref[...])
            hv_refs[0][0:SUBLANES, :] = jnp.where(first, 0.0, tv_ref[...])
        else:
            hg_refs[c][0:SUBLANES, :] = hg_refs[c - 1][sub:sub + SUBLANES, :]
            hv_refs[c][0:SUBLANES, :] = hv_refs[c - 1][sub:sub + SUBLANES, :]
        g = conv(hg_refs[c], cg_ref[...])
        v = conv(hv_refs[c], cv_ref[...])
        half_g = 0.5 * g
        o_ref[c * sub:(c + 1) * sub, :] = ((half_g + half_g * jnp.tanh(half_g)) * v).astype(o_ref.dtype)

    matmuls(0)
    for c in range(n_chunks):
        if c + 1 < n_chunks:
            matmuls(c + 1)
        gate(c)
    tg_ref[...] = hg_refs[n_chunks - 1][sub:sub + SUBLANES, :]
    tv_ref[...] = hv_refs[n_chunks - 1][sub:sub + SUBLANES, :]


def _ffn_up(xb, w_up, conv_w, seq, tm, tn, sub):
    n, k = xb.shape
    dff = w_up.shape[1] // 2
    ncol = dff // tn
    return pl.pallas_call(
        functools.partial(_ffn_up_kernel, tiles_per_seq=seq // tm, sub=sub),
        grid=(ncol, n // tm),
        in_specs=[pl.BlockSpec((tm, k), lambda j, i: (i, 0)),
                  pl.BlockSpec((k, tn), lambda j, i: (0, j)),
                  pl.BlockSpec((k, tn), lambda j, i, ncol=ncol: (0, j + ncol)),
                  pl.BlockSpec((CONV_W, tn), lambda j, i: (0, j)),
                  pl.BlockSpec((CONV_W, tn), lambda j, i, ncol=ncol: (0, j + ncol))],
        out_specs=pl.BlockSpec((tm, tn), lambda j, i: (i, j)),
        out_shape=jax.ShapeDtypeStruct((n, dff), BF16),
        scratch_shapes=([pltpu.VMEM((SUBLANES, tn), F32)] * 2
                        + [pltpu.VMEM((sub + SUBLANES, tn), F32)] * (2 * (tm // sub))),
        compiler_params=_cparams(("arbitrary", "arbitrary")),
        name="ffn_up_conv_gate",
    )(xb, w_up, w_up, conv_w, conv_w)


def _gelu_tanh(x):
    return 0.5 * x * (1.0 + jnp.tanh(math.sqrt(2.0 / math.pi) * (x + 0.044715 * (x * x * x))))


def _group_mean(v, avg):
    hi = v.astype(BF16)
    lo = (v - hi.astype(F32)).astype(BF16)
    return _dot(hi, avg) + _dot(lo, avg)


def _odd_mixer_kernel(x_ref, w_ref, sg_ref, sw_ref, sb_ref, cw_ref, wo_ref, xf_ref, g_ref, b_ref,
                      xo_ref, xb_ref, o_ref, tail_ref, *, tiles_per_seq, alpha):
    tm = x_ref.shape[0]
    W = MIX_HALF
    first = (pl.program_id(0) % tiles_per_seq) == 0
    x = x_ref[...]
    lane = lax.broadcasted_iota(jnp.int32, (CHUNK, LANES), 1)
    lo_half = lane < HEAD_DIM
    r = lax.broadcasted_iota(jnp.int32, (W, W), 0) // HEAD_DIM
    c = lax.broadcasted_iota(jnp.int32, (W, W), 1) // HEAD_DIM
    avg = jnp.where(r == c, 1.0 / HEAD_DIM, 0.0).astype(BF16)
    tri = (lax.broadcasted_iota(jnp.int32, (CHUNK, CHUNK), 1)
           <= lax.broadcasted_iota(jnp.int32, (CHUNK, CHUNK), 0))

    u = _gelu_tanh(_dot(x, w_ref[:, 0:W]))
    v = _gelu_tanh(_dot(x, w_ref[:, W:2 * W]))
    vc = v - _group_mean(v, avg)
    var = _group_mean(vc * vc, avg)
    vn = (vc * lax.rsqrt(var + LN_EPS) * sg_ref[...]).astype(BF16)
    for n in range(tm // CHUNK):
        rows = slice(n * CHUNK, (n + 1) * CHUNK)
        for p in range(N_HEADS // 2):
            cols = slice(p * LANES, (p + 1) * LANES)
            vp = vn[rows, cols]
            w_e = jnp.where(tri, sw_ref[2 * p], 0.0).astype(BF16)
            w_o = jnp.where(tri, sw_ref[2 * p + 1], 0.0).astype(BF16)
            mix = (_dot(w_e, jnp.where(lo_half, vp, 0)) + _dot(w_o, jnp.where(lo_half, 0, vp))
                   + sb_ref[:, cols])
            o_ref[rows, cols] = (u[rows, cols] * mix).astype(o_ref.dtype)

    g_b = _dot(x, w_ref[:, 2 * W:3 * W])
    y = _dot(x, w_ref[:, 3 * W:4 * W]) * _dot(x, w_ref[:, 4 * W:5 * W])
    conv = _causal_conv3(y, jnp.where(first, 0.0, tail_ref[...]), cw_ref[...])
    tail_ref[...] = y[tm - SUBLANES:tm, :]
    o_ref[:, W:2 * W] = (g_b * conv).astype(o_ref.dtype)

    out = _layer_norm_rows(alpha * xf_ref[...] + _dot(o_ref[...], wo_ref[...]), g_ref[...], b_ref[...])
    xo_ref[...] = out
    xb_ref[...] = out.astype(BF16)


def _odd_mixer(xb, w_in, sgu_g, sgu_w, sgu_b_tile, conv_w, w_o, xf, g, b, alpha, seq, tm):
    n, k = xb.shape
    W = MIX_HALF
    d = w_o.shape[1]
    const2 = lambda shape: pl.BlockSpec(shape, lambda i: (0, 0))
    row_tile = lambda width: pl.BlockSpec((tm, width), lambda i: (i, 0))
    return pl.pallas_call(
        functools.partial(_odd_mixer_kernel, tiles_per_seq=seq // tm, alpha=alpha),
        grid=(n // tm,),
        in_specs=[row_tile(k), const2((k, 5 * W)), const2((1, W)),
                  pl.BlockSpec((N_HEADS, CHUNK, CHUNK), lambda i: (0, 0, 0)),
                  const2((CHUNK, W)), const2((CONV_W, W)),
                  const2((2 * W, d)), row_tile(d), const2((1, d)), const2((1, d))],
        out_specs=[row_tile(d), row_tile(d)],
        out_shape=[jax.ShapeDtypeStruct((n, d), F32), jax.ShapeDtypeStruct((n, d), BF16)],
        scratch_shapes=[pltpu.VMEM((tm, 2 * W), BF16),
                        pltpu.VMEM((SUBLANES, W), F32)],
        compiler_params=_cparams(("arbitrary",)),
        name="odd_mixer",
    )(xb, w_in, sgu_g, sgu_w, sgu_b_tile, conv_w, w_o, xf, g, b)


def _t5_bucket_np(dist):
    max_exact = N_BUCKETS // 2
    n = np.maximum(dist, 0)
    nf = np.maximum(n, 1).astype(np.float32)
    large = max_exact + (np.log(nf / max_exact) / np.float32(math.log(T5_MAX_DIST / max_exact))
                         * (N_BUCKETS - max_exact)).astype(np.int32)
    large = np.minimum(large, N_BUCKETS - 1)
    return np.where(n < max_exact, n, large)


def _bias_tables(rel_bias):
    T = BLK
    s = np.arange(T)[:, None]
    t = np.arange(T)[None, :]
    bucket = _t5_bucket_np(np.stack([t + d * T - s for d in range(3)]))
    far_bucket = N_BUCKETS - 1
    assert (bucket[2] == far_bucket).all()
    assert (_t5_bucket_np(np.arange(T + 1, 64 * T)) == far_bucket).all()
    rb = rel_bias.astype(F32) * LOG2E
    tiles = sum(jnp.where(jnp.asarray(bucket == b)[..., None], rb[b], 0.0) for b in range(N_BUCKETS))
    tiles = jnp.swapaxes(tiles, 2, 3).reshape(3, T, N_HEADS * T)
    far = jnp.tile(jnp.repeat(rb[far_bucket], T)[None, :], (1, 2))
    return tiles, far


def _placement_constants():
    H, L = N_HEADS, LANES
    pq = np.zeros((3 * L, H * L), np.float32)
    cq = np.zeros((1, H * L), np.float32)
    pk = np.zeros((3 * L, (H // 2) * L), np.float32)
    ck = np.zeros((1, (H // 2) * L), np.float32)
    for h in range(H):
        p, e = divmod(h, 2)
        for piece in range(3):
            pq[piece * L + h, h * L + 6 + piece] = 1.0
            cq[0, h * L + 3 * e + piece] = 1.0
            pk[piece * L + h, p * L + 3 * e + piece] = -1.0
            ck[0, p * L + 6 + piece] = 1.0
    return jnp.asarray(pq, BF16), jnp.asarray(cq), jnp.asarray(pk, BF16), jnp.asarray(ck)


def _even_weights(w_in, w_uk, w_uv):
    hd, H = HEAD_DIM, N_HEADS
    widths = (H * hd, KV_RANK, H * hd, hd, H, H * hd, H * hd, H * hd, H)
    offs = np.cumsum((0,) + widths)
    seg = lambda s: w_in[:, offs[s]:offs[s + 1]]
    q_a, c_kv, q_idx, k_idx, w_idx, q_b, k_b, v_b, f_in = [seg(s) for s in range(9)]
    scale = hd ** -0.5
    w_big = jnp.concatenate([q_a, q_idx * scale, q_b * (scale * LOG2E), k_b, v_b, c_kv,
                             k_idx, k_idx], axis=1).astype(BF16)
    pad = jnp.zeros((w_in.shape[0], LANES - 2 * H), w_in.dtype)
    w_small = jnp.concatenate([w_idx, f_in, pad], axis=1).astype(BF16)
    wuk_t = jnp.swapaxes(w_uk, 1, 2) * (scale * LOG2E)
    odd = (np.arange(H) % 2 == 1)[:, None, None]
    zk, zv = jnp.zeros_like(wuk_t), jnp.zeros_like(w_uv)
    wuk_pad = jnp.where(odd, jnp.concatenate([zk, wuk_t], axis=1), jnp.concatenate([wuk_t, zk], axis=1))
    wuv_pad = jnp.where(odd, jnp.concatenate([zv, w_uv], axis=2), jnp.concatenate([w_uv, zv], axis=2))
    return w_big, w_small, wuk_pad.astype(BF16), wuv_pad.astype(BF16)


def kernel(x, ln_g, ln_b, rel_bias, ev_w_in, ev_w_uk, ev_w_uv, ev_b_f, ev_w_o,
           od_w_in, od_sgu_g, od_sgu_w, od_sgu_b, od_conv_w, od_w_o,
           ffn_w_up, ffn_conv_w, ffn_w_down):
    bsz, seq, d = x.shape
    depth = ln_g.shape[0]
    n = bsz * seq
    alpha = (2.0 * depth) ** 0.25
    k_sel = min(TOPK_MAX, seq // 4)
    H = N_HEADS
    assert seq % QUERIES == 0 and seq % CHUNK == 0
    tm_ffn = min(1024, seq)

    bias_tiles, bias_far = _bias_tables(rel_bias)
    place = _placement_constants()
    xf = x.reshape(n, d)
    xb = xf
    for layer in range(depth):
        j = layer // 2
        g0, b0 = ln_g[layer, 0][None, :], ln_b[layer, 0][None, :]
        g1, b1 = ln_g[layer, 1][None, :], ln_b[layer, 1][None, :]
        if layer % 2 == 0:
            w_big, w_small, wuk_pad, wuv_pad = _even_weights(ev_w_in[j], ev_w_uk[j], ev_w_uv[j])
            hb, hs = _even_proj(xb, w_big, w_small, tm_ffn)
            hb = hb.reshape(bsz, seq, -1)
            hs_t = jnp.swapaxes(hs.reshape(bsz, seq, -1)[:, :, 0:2 * H], 1, 2)
            cum_f = _cumf(hs_t[:, H:2 * H], ev_b_f[j][:, None])
            f_rows = jnp.pad(jnp.swapaxes(cum_f, 1, 2), ((0, 0), (0, 0), (0, LANES - H)))
            mix = _attention(hb, hs_t[:, 0:H], f_rows, wuk_pad, wuv_pad, bias_tiles, bias_far,
                             place, k_sel)
            xf, xb = _proj_ln(mix.reshape(n, d), ev_w_o[j].astype(BF16), xf, g0, b0, alpha, tm_ffn)
        else:
            sgu_b_tile = jnp.repeat(jnp.swapaxes(od_sgu_b[j], 0, 1), HEAD_DIM, axis=1)
            xf, xb = _odd_mixer(xb, od_w_in[j].astype(BF16), od_sgu_g[j].reshape(1, -1), od_sgu_w[j],
                                sgu_b_tile, od_conv_w[j], od_w_o[j].astype(BF16), xf, g0, b0, alpha,
                                seq, min(1024, seq))
        act = _ffn_up(xb, ffn_w_up[layer].astype(BF16), ffn_conv_w[layer], seq, min(4096, seq), 256,
                      min(512, seq))
        xf, xb = _proj_ln(act, ffn_w_down[layer].astype(BF16), xf, g1, b1, alpha, tm_ffn)
    return xf.reshape(bsz, seq, d)
```

```python
import functools
import math

import numpy as np
import jax
import jax.numpy as jnp
from jax import lax
from jax.experimental import pallas as pl
from jax.experimental.pallas import tpu as pltpu

F32 = jnp.float32
BF16 = jnp.bfloat16

HEAD_DIM = 64
N_HEADS = 8
KV_RANK = 128
TOPK_MAX = 256
N_BUCKETS = 32
T5_MAX_DIST = 128
CHUNK = 128
CONV_W = 3
LN_EPS = 1e-5
MIX_HALF = N_HEADS * HEAD_DIM

LANES = 128
SUBLANES = 8
BF16_ROWS = 16
VMEM_LIMIT = 56 * 1024 * 1024

BLK = 128
QUERIES = 2 * BLK
PV_ROWS = KV_RANK + BF16_ROWS
NEG_BIG = -1e30
MASKED = -(2.0 ** 100)
F32_MAX = float(np.finfo(np.float32).max)
LOG2E = math.log2(math.e)
SEARCH_UNROLL = 3


def _cparams(sem):
    return pltpu.CompilerParams(dimension_semantics=sem, vmem_limit_bytes=VMEM_LIMIT)


def _dot(a, b):
    return jnp.dot(a, b, preferred_element_type=F32)


def _dot_nt(a, b):
    return lax.dot_general(a, b, (((1,), (1,)), ((), ())), preferred_element_type=F32)


def _layer_norm_rows(y, g, b):
    mu = jnp.mean(y, axis=-1, keepdims=True)
    yc = y - mu
    var = jnp.mean(yc * yc, axis=-1, keepdims=True)
    return yc * lax.rsqrt(var + LN_EPS) * g + b


def _tree(op, x):
    while x.shape[0] > 1:
        half = x.shape[0] // 2
        x = op(x[:half], x[half:])
    return x[0]


def _bf16_pieces(v):
    hi = v.astype(BF16)
    r1 = v - hi.astype(F32)
    mid = r1.astype(BF16)
    lo = (r1 - mid.astype(F32)).astype(BF16)
    return jnp.concatenate([hi, mid, lo], axis=1)


def _even_proj_kernel(x_ref, wb_ref, ws_ref, hb_ref, hs_ref):
    x = x_ref[...].astype(BF16)
    hb_ref[...] = _dot(x, wb_ref[...]).astype(hb_ref.dtype)
    hs_ref[...] = _dot(x, ws_ref[...])


def _even_proj(xb, w_big, w_small, tm):
    n, k = xb.shape
    mb = w_big.shape[1]
    ms = w_small.shape[1]
    return pl.pallas_call(
        _even_proj_kernel,
        grid=(n // tm,),
        in_specs=[pl.BlockSpec((tm, k), lambda i: (i, 0)),
                  pl.BlockSpec((k, mb), lambda i: (0, 0)),
                  pl.BlockSpec((k, ms), lambda i: (0, 0))],
        out_specs=[pl.BlockSpec((tm, mb), lambda i: (i, 0)),
                   pl.BlockSpec((tm, ms), lambda i: (i, 0))],
        out_shape=[jax.ShapeDtypeStruct((n, mb), BF16),
                   jax.ShapeDtypeStruct((n, ms), F32)],
        compiler_params=_cparams(("parallel",)),
        name="even_proj",
    )(xb, w_big, w_small)


def _cumf_kernel(f_ref, b_ref, o_ref):
    seq = f_ref.shape[-1]
    row = lax.broadcasted_iota(jnp.int32, (LANES, LANES), 0)
    col = lax.broadcasted_iota(jnp.int32, (LANES, LANES), 1)
    upper = jnp.where(row <= col, 1.0, 0.0).astype(F32)
    carry = jnp.zeros((f_ref.shape[0], 1), F32)
    for c in range(seq // LANES):
        z = f_ref[:, c * LANES:(c + 1) * LANES] + b_ref[...]
        lf = jnp.minimum(z, 0.0) - jnp.log1p(jnp.exp(-jnp.abs(z)))
        cs = jnp.dot(lf, upper, preferred_element_type=F32,
                     precision=lax.Precision.HIGHEST) + carry
        o_ref[:, c * LANES:(c + 1) * LANES] = cs
        carry = cs[:, LANES - 1:LANES]


def _cumf(f_t, b_f):
    bsz, nh, seq = f_t.shape
    return pl.pallas_call(
        _cumf_kernel,
        grid=(bsz,),
        in_specs=[pl.BlockSpec((None, nh, seq), lambda b: (b, 0, 0)),
                  pl.BlockSpec((nh, 1), lambda b: (0, 0))],
        out_specs=pl.BlockSpec((None, nh, seq), lambda b: (b, 0, 0)),
        out_shape=jax.ShapeDtypeStruct((bsz, nh, seq), F32),
        compiler_params=_cparams(("parallel",)),
        name="fox_cumf",
    )(f_t, b_f)


def _attn_kernel(qa_ref, qi_ref, qb_ref, wt_ref, fq_ref, kb_ref, vb_ref, ckv_ref, kidx_ref, fk_ref,
                 wuk_ref, wuv_ref, bias_ref, bfar_ref, pq_ref, cq_ref, pk_ref, ck_ref,
                 o_ref,
                 score_ref, lhsd_ref, ct_ref, kaug_ref, vt_ref, qsel_ref, qaug_ref, qim_ref,
                 m_ref, acc_ref, s_ref, tie_ref, *, k_sel):
    T = BLK
    Q = QUERIES
    H = N_HEADS
    NP = H // 2
    NT = 2 * NP
    TW = 2 * T
    g = pl.program_id(1)
    i0 = 2 * g
    nblk = ct_ref.shape[0]
    lane = lax.broadcasted_iota(jnp.int32, (T, LANES), 1)
    lo_half = lane < HEAD_DIM
    ones_rows = jnp.ones((BF16_ROWS, T), BF16)
    causal_qq = (lax.broadcasted_iota(jnp.int32, (Q, Q), 0) <= lax.broadcasted_iota(jnp.int32, (Q, Q), 1))

    @pl.when(g == 0)
    def _per_batch():
        def prep(j, c):
            r = pl.ds(pl.multiple_of(j * T, T), T)
            cb = ckv_ref[r, :]
            lhsd_ref[0, r, 0:KV_RANK] = cb
            lhsd_ref[1, r, 0:KV_RANK] = cb
            ct_ref[j, 0:KV_RANK, :] = cb.astype(F32).T.astype(BF16)
            ct_ref[j, KV_RANK:PV_ROWS, :] = ones_rows
            ext = (_dot(_bf16_pieces(fk_ref[r, :] * LOG2E), pk_ref[...]) + ck_ref[...]).astype(BF16)
            for p in range(NP):
                cols = slice(p * LANES, (p + 1) * LANES)
                kaug_ref[p, r, 0:LANES] = kb_ref[r, cols]
                kaug_ref[p, r, LANES:2 * LANES] = ext[:, cols]
                vt_ref[p, j, 0:LANES, :] = vb_ref[r, cols].astype(F32).T.astype(BF16)
                vt_ref[p, j, LANES:PV_ROWS, :] = ones_rows
            return c
        lax.fori_loop(0, nblk, prep, 0)

    w_rows = wt_ref[...] * (H ** -0.5)
    ext_q = (_dot(_bf16_pieces(fq_ref[...] * LOG2E), pq_ref[...]) + cq_ref[...]).astype(BF16)
    eye = jnp.where(lax.broadcasted_iota(jnp.int32, (T, T), 0) == lax.broadcasted_iota(jnp.int32, (T, T), 1),
                    1.0, 0.0).astype(BF16)
    for tt in range(NT):
        hf, p = divmod(tt, NP)
        qrows = slice(hf * T, (hf + 1) * T)
        cols = slice(p * LANES, (p + 1) * LANES)
        qa_pair, qi_pair, qb_pair = qa_ref[qrows, cols], qi_ref[qrows, cols], qb_ref[qrows, cols]
        for e in range(2):
            h = 2 * p + e
            rows = slice(e * T, (e + 1) * T)
            mine = lo_half if e == 0 else jnp.logical_not(lo_half)
            qsel_ref[tt, rows, 0:KV_RANK] = _dot(qa_pair, wuk_ref[h]).astype(BF16)
            qsel_ref[tt, rows, KV_RANK:2 * KV_RANK] = eye
            qaug_ref[tt, rows, 0:LANES] = jnp.where(mine, qb_pair, 0)
            qaug_ref[tt, rows, LANES:2 * LANES] = ext_q[qrows, h * LANES:(h + 1) * LANES]
            qim_ref[tt, rows, :] = jnp.where(mine, qi_pair, 0)

    def reset_state():
        m_ref[...] = jnp.full(m_ref.shape, NEG_BIG, F32)
        acc_ref[...] = jnp.zeros(acc_ref.shape, F32)

    def tile_cols(tt):
        return slice(tt * TW, (tt + 1) * TW)

    def online_update(tt, s, pv_lhs):
        cols = tile_cols(tt)
        m_old = m_ref[:, cols]
        s_max = _tree(jnp.maximum, s.reshape(-1, SUBLANES, TW))
        m_new = jnp.maximum(m_old, jnp.max(s_max, axis=0, keepdims=True))
        alpha = jnp.exp2(m_old - m_new)
        pr = jnp.exp2((s - m_new[0:1, :]).astype(BF16))
        acc_ref[:, cols] = acc_ref[:, cols] * alpha[0:1, :] + _dot(pv_lhs, pr)
        m_ref[:, cols] = m_new

    def key_rows(j, nblocks):
        return pl.ds(pl.multiple_of(j * T, T), nblocks * T)

    def slot_rows(nblocks):
        return slice(0, nblocks * T)

    def blocks_t(load, j, nblocks):
        return load(j) if nblocks == 1 else jnp.concatenate([load(j + b) for b in range(nblocks)], axis=1)

    def staged_steps(j0, n_steps, logits, absorb):
        logits(j0, 2, 0)
        for k in range(n_steps):
            if k + 1 < n_steps:
                logits(j0 + 2 * (k + 1), 2, (k + 1) % 2)
            absorb(j0 + 2 * k, 2, k % 2)

    def run_staged(steps):
        steps[0][0](0)
        for k, (_, absorb_k) in enumerate(steps):
            if k + 1 < len(steps):
                steps[k + 1][0]((k + 1) % 2)
            absorb_k(k % 2)

    def for_far_blocks(n_far, logits, absorb):
        n_oct = lax.shift_right_logical(n_far, 3)

        def oct_body(jj, c):
            staged_steps(8 * jj, 4, logits, absorb)
            return c
        lax.fori_loop(0, n_oct, oct_body, 0)

        @pl.when((n_far & 4) != 0)
        def _quad():
            staged_steps(8 * n_oct, 2, logits, absorb)

        return (n_far & 2) != 0, 8 * n_oct + (n_far & 4)

    def finish_blocks(leftover, far_step, before_step, diag_step):
        has_pair, pair_j = leftover

        @pl.when(jnp.logical_and(g >= 1, has_pair))
        def _three():
            run_staged([far_step(pair_j), before_step, diag_step])

        @pl.when(jnp.logical_and(g >= 1, jnp.logical_not(has_pair)))
        def _two():
            run_staged([before_step, diag_step])

        @pl.when(g == 0)
        def _first():
            run_staged([diag_step])

    def idx_scores(j, nblocks):
        k_blk = kidx_ref[key_rows(j, nblocks), :]
        acc = [jnp.zeros((nblocks * T, T), F32), jnp.zeros((nblocks * T, T), F32)]
        for tt in range(NT):
            hf, p = divmod(tt, NP)
            s = _dot_nt(k_blk, qim_ref[tt])
            for e in range(2):
                h = 2 * p + e
                w_h = w_rows[h:h + 1, hf * T:(hf + 1) * T]
                acc[hf] = acc[hf] + w_h * jnp.maximum(s[:, e * T:(e + 1) * T], 0.0)
        return jnp.concatenate(acc, axis=1)

    def fox_logits(j, nblocks, slot, causal=False):
        for tt in range(NT):
            hf, p = divmod(tt, NP)
            s = _dot_nt(kaug_ref[p, key_rows(j, nblocks), :], qaug_ref[tt])
            if causal:
                ok = causal_qq[:, hf * T:(hf + 1) * T]
                s = jnp.where(jnp.concatenate([ok, ok], axis=1), s, MASKED)
            s_ref[slot, slot_rows(nblocks), tile_cols(tt)] = s

    def fox_absorb(j, nblocks, slot):
        for tt in range(NT):
            p = tt % NP
            online_update(tt, s_ref[slot, slot_rows(nblocks), tile_cols(tt)],
                          blocks_t(lambda jb: vt_ref[p, jb], j, nblocks))

    def idx_fox_logits(j, nblocks, slot):
        score_ref[pl.ds(j, nblocks)] = idx_scores(j, nblocks).reshape(nblocks, T, Q)
        fox_logits(j, nblocks, slot)

    reset_state()
    leftover = for_far_blocks(jnp.maximum(i0 - 2, 0), idx_fox_logits, fox_absorb)

    def diag_logits(slot):
        score_ref[pl.ds(i0, 2)] = jnp.where(causal_qq, idx_scores(i0, 2), -jnp.inf).reshape(2, T, Q)
        fox_logits(i0, 2, slot, causal=True)

    def fox_far_step(j):
        return (lambda slot: idx_fox_logits(j, 2, slot), lambda slot: fox_absorb(j, 2, slot))

    finish_blocks(leftover, fox_far_step, fox_far_step(i0 - 2),
                  (diag_logits, lambda slot: fox_absorb(i0, 2, slot)))

    row_lo = lax.broadcasted_iota(jnp.int32, (LANES, T), 0) < HEAD_DIM
    for tt in range(NT):
        hf, p = divmod(tt, NP)
        c0 = slice(tt * TW, tt * TW + T)
        c1 = slice(tt * TW + T, (tt + 1) * TW)
        o_t = jnp.where(row_lo, acc_ref[0:LANES, c0] / acc_ref[LANES:LANES + 1, c0],
                        acc_ref[0:LANES, c1] / acc_ref[LANES:LANES + 1, c1])
        o_ref[hf * T:(hf + 1) * T, MIX_HALF + p * LANES:MIX_HALF + (p + 1) * LANES] = (
            o_t.T.astype(o_ref.dtype))

    kf = float(k_sel)
    t_pos = (g * Q + lax.broadcasted_iota(jnp.int32, (SUBLANES, Q), 1)).astype(F32)
    searching = (t_pos + 1.0) > kf
    G2 = 2 * T // SUBLANES
    n_pair = g + 1
    key_in_pair = (lax.broadcasted_iota(jnp.int32, (G2, SUBLANES, Q), 0) * SUBLANES
                   + lax.broadcasted_iota(jnp.int32, (G2, SUBLANES, Q), 1))

    def rep(v):
        return jnp.broadcast_to(v, (SUBLANES, Q))

    def score_pair(jj):
        return score_ref[pl.ds(2 * jj, 2)].reshape(G2, SUBLANES, Q)

    def count(*pred_fns, load=score_pair):
        def body(jj, cs):
            s = load(jj)
            return tuple(c + _tree(jnp.add, jnp.where(f(s, jj), 1.0, 0.0)) for c, f in zip(cs, pred_fns))
        zero = jnp.zeros((SUBLANES, Q), F32)
        cs = lax.fori_loop(0, n_pair, body, (zero,) * len(pred_fns))
        return tuple(rep(jnp.sum(c, axis=0, keepdims=True)) for c in cs)

    def count_ge(thr):
        return count(lambda s, jj: s >= thr[None])[0]

    def select_threshold():
        def first_body(jj, carry):
            mn, mx, ge0, gt0 = carry
            s = score_pair(jj)
            return (jnp.minimum(mn, _tree(jnp.minimum, jnp.where(s == -jnp.inf, jnp.inf, s))),
                    jnp.maximum(mx, _tree(jnp.maximum, s)),
                    ge0 + _tree(jnp.add, jnp.where(s >= 0.0, 1.0, 0.0)),
                    gt0 + _tree(jnp.add, jnp.where(s > 0.0, 1.0, 0.0)))

        zero = jnp.zeros((SUBLANES, Q), F32)
        mn, mx, c_ge0, c_gt0 = lax.fori_loop(0, n_pair, first_body, (zero + jnp.inf, zero - jnp.inf, zero, zero))
        mn = rep(jnp.min(mn, axis=0, keepdims=True))
        mx = rep(jnp.max(mx, axis=0, keepdims=True))
        c_ge0 = rep(jnp.sum(c_ge0, axis=0, keepdims=True))
        c_gt0 = rep(jnp.sum(c_gt0, axis=0, keepdims=True))
        above_mx = mx + jnp.maximum(jnp.abs(mx) * 2.0 ** -20, 1e-30)
        thr_ge0 = c_ge0 >= kf
        thr_gt0 = c_gt0 >= kf
        lo0 = jnp.where(thr_ge0, 0.0, mn)
        cnt_lo0 = jnp.where(thr_ge0, c_ge0, t_pos + 1.0)
        hi0 = jnp.where(thr_gt0, above_mx, 0.0)
        cnt_hi0 = jnp.where(thr_gt0, 0.0, jnp.where(thr_ge0, c_gt0, c_ge0))
        undecided = thr_gt0 | jnp.logical_not(thr_ge0)
        active0 = jnp.where(searching & undecided & (cnt_lo0 > kf), 1.0, 0.0)

        def cond(st):
            return jnp.logical_and(st[5] > 0.0, st[6] < 2200)

        def bisect(st):
            lo, hi, cnt_lo, cnt_hi, active = st
            mid = 0.5 * lo + 0.5 * hi
            adjacent = (mid <= lo) | (mid >= hi)
            c = count_ge(mid)
            act = active > 0.0
            go_lo = act & jnp.logical_not(adjacent) & (c >= kf)
            go_hi = act & jnp.logical_not(adjacent) & (c < kf)
            lo = jnp.where(go_lo, mid, lo)
            cnt_lo = jnp.where(go_lo, c, cnt_lo)
            hi = jnp.where(go_hi, mid, hi)
            cnt_hi = jnp.where(go_hi, c, cnt_hi)
            active = jnp.where(act & jnp.logical_not(adjacent) & (cnt_lo > kf), 1.0, 0.0)
            return lo, hi, cnt_lo, cnt_hi, active

        def body(st):
            inner = st[:5]
            for _ in range(SEARCH_UNROLL):
                inner = bisect(inner)
            return inner + (jnp.max(inner[4]), st[6] + SEARCH_UNROLL)

        lo, hi, cnt_lo, cnt_hi, _, _, _ = lax.while_loop(
            cond, body, (lo0, hi0, cnt_lo0, cnt_hi0, active0, jnp.max(active0), 0))
        thr = jnp.where(searching, lo, -F32_MAX)

        tied = searching & (cnt_lo > kf)
        need = kf - cnt_hi

        @pl.when(jnp.max(jnp.where(tied, 1.0, 0.0)) > 0.0)
        def _break_ties():
            def tie_pair(jj):
                return tie_ref[pl.ds(2 * jj, 2)].reshape(G2, SUBLANES, Q)

            def mark_body(jj, c):
                key_index = (2 * jj * T + key_in_pair).astype(F32)
                tie_ref[pl.ds(2 * jj, 2)] = (
                    jnp.where(score_pair(jj) == thr[None], key_index, jnp.inf).reshape(2, T, Q))
                return c
            lax.fori_loop(0, n_pair, mark_body, 0)

            def count_tied_upto(jcut):
                return count(lambda e, jj: e <= jcut[None], load=tie_pair)[0]

            def bs_body(_, st):
                jl, jh = st
                jm = jnp.floor(0.5 * (jl + jh))
                ok = count_tied_upto(jm) >= need
                return jnp.where(ok, jl, jm), jnp.where(ok, jm, jh)

            n_keys = (2 * n_pair * T).astype(F32)
            jl0 = jnp.full((SUBLANES, Q), -1.0, F32)
            jh0 = jnp.zeros((SUBLANES, Q), F32) + (n_keys - 1.0)
            n_steps = int(math.ceil(math.log2(nblk * T))) + 1
            _, jcut = lax.fori_loop(0, n_steps, bs_body, (jl0, jh0))

            def drop_body(jj, c):
                e = tie_pair(jj)
                drop = tied[None] & (e > jcut[None]) & (e < jnp.inf)
                score_ref[pl.ds(2 * jj, 2)] = jnp.where(drop, -jnp.inf, score_pair(jj)).reshape(2, T, Q)
                return c
            lax.fori_loop(0, n_pair, drop_body, 0)

        return thr

    thr = lax.cond((i0 + 2) * T > k_sel, select_threshold,
                   lambda: jnp.full((SUBLANES, Q), -F32_MAX, F32))

    def mask_body(jj, c):
        add = jnp.where(score_pair(jj) >= thr[None], 0.0, MASKED).reshape(2 * T, Q).astype(BF16)
        for hf in range(2):
            lhsd_ref[hf, key_rows(2 * jj, 2), KV_RANK:2 * KV_RANK] = add[:, hf * T:(hf + 1) * T]
        return c

    lax.fori_loop(0, n_pair, mask_body, 0)

    def dsa_logits(j, nblocks, slot, bias=None):
        for tt in range(NT):
            hf, p = divmod(tt, NP)
            s = _dot_nt(lhsd_ref[hf, key_rows(j, nblocks), :], qsel_ref[tt])
            if bias is not None:
                s = s + bias(hf, slice(p * TW, (p + 1) * TW))
            s_ref[slot, slot_rows(nblocks), tile_cols(tt)] = s

    def dsa_absorb(j, nblocks, slot):
        pv_lhs = blocks_t(lambda jb: ct_ref[jb], j, nblocks)
        for tt in range(NT):
            online_update(tt, s_ref[slot, slot_rows(nblocks), tile_cols(tt)], pv_lhs)

    def bias_before(hf, cols):
        return jnp.concatenate([bias_ref[2, :, cols], bias_ref[1 + hf, :, cols]], axis=0)

    def bias_diag(hf, cols):
        first = bias_ref[hf, :, cols]
        return jnp.concatenate([first, bias_ref[0, :, cols]], axis=0)

    reset_state()
    leftover = for_far_blocks(jnp.maximum(i0 - 2, 0), dsa_logits, dsa_absorb)

    def dsa_far_step(j):
        return (lambda slot: dsa_logits(j, 2, slot), lambda slot: dsa_absorb(j, 2, slot))

    def absorb_first_biased(j):
        def absorb(slot):
            m_ref[...] = m_ref[...] + bfar_ref[...]
            dsa_absorb(j, 2, slot)
        return absorb

    diag_step = lambda absorb: (lambda slot: dsa_logits(i0, 2, slot, bias_diag), absorb)
    before_step = (lambda slot: dsa_logits(i0 - 2, 2, slot, bias_before), absorb_first_biased(i0 - 2))
    has_pair, pair_j = leftover

    @pl.when(jnp.logical_and(g >= 1, has_pair))
    def _dsa_three():
        run_staged([dsa_far_step(pair_j), before_step, diag_step(lambda slot: dsa_absorb(i0, 2, slot))])

    @pl.when(jnp.logical_and(g >= 1, jnp.logical_not(has_pair)))
    def _dsa_two():
        run_staged([before_step, diag_step(lambda slot: dsa_absorb(i0, 2, slot))])

    @pl.when(g == 0)
    def _dsa_first():
        run_staged([diag_step(absorb_first_biased(0))])

    for tt in range(NT):
        hf, p = divmod(tt, NP)
        o_pair = jnp.zeros((T, LANES), F32)
        for e in range(2):
            cols = slice(tt * TW + e * T, tt * TW + (e + 1) * T)
            o_t = acc_ref[0:KV_RANK, cols] / acc_ref[KV_RANK:KV_RANK + 1, cols]
            o_pair = o_pair + _dot(o_t.T.astype(BF16), wuv_ref[2 * p + e])
        o_ref[hf * T:(hf + 1) * T, p * LANES:(p + 1) * LANES] = o_pair.astype(o_ref.dtype)


_EV_QA, _EV_QI, _EV_QB, _EV_KB, _EV_VB = 0, 1, 2, 3, 4
_EV_CKV, _EV_KIDX = 20, 21


def _attention(hb, w_t, f_rows, wuk_pad, wuv_pad, bias_tiles, bias_far, place, k_sel):
    bsz, seq, _ = hb.shape
    T, Q, H = BLK, QUERIES, N_HEADS
    nblk = seq // T
    lanes = 2 * H * T
    pq, cq, pk, ck = place
    wide = lambda c: pl.BlockSpec((None, Q, MIX_HALF), lambda b, i, c=c: (b, i, c))
    once = dict(pipeline_mode=pl.Buffered(1))
    full_wide = lambda c: pl.BlockSpec((None, seq, MIX_HALF), lambda b, i, c=c: (b, 0, c), **once)
    full_narrow = lambda c: pl.BlockSpec((None, seq, LANES), lambda b, i, c=c: (b, 0, c), **once)
    const = lambda a: pl.BlockSpec(a.shape, lambda b, i, nd=a.ndim: (0,) * nd, **once)
    return pl.pallas_call(
        functools.partial(_attn_kernel, k_sel=k_sel),
        grid=(bsz, seq // Q),
        in_specs=[wide(_EV_QA), wide(_EV_QI), wide(_EV_QB),
                  pl.BlockSpec((None, H, Q), lambda b, i: (b, 0, i)),
                  pl.BlockSpec((None, Q, LANES), lambda b, i: (b, i, 0)),
                  full_wide(_EV_KB), full_wide(_EV_VB),
                  full_narrow(_EV_CKV), full_narrow(_EV_KIDX),
                  pl.BlockSpec((None, seq, LANES), lambda b, i: (b, 0, 0), **once),
                  const(wuk_pad), const(wuv_pad), const(bias_tiles), const(bias_far),
                  const(pq), const(cq), const(pk), const(ck)],
        out_specs=pl.BlockSpec((None, Q, 2 * MIX_HALF), lambda b, i: (b, i, 0)),
        out_shape=jax.ShapeDtypeStruct((bsz, seq, 2 * MIX_HALF), BF16),
        scratch_shapes=[pltpu.VMEM((nblk, T, Q), F32),
                        pltpu.VMEM((2, seq, 2 * KV_RANK), BF16),
                        pltpu.VMEM((nblk, PV_ROWS, T), BF16),
                        pltpu.VMEM((H // 2, seq, 2 * LANES), BF16),
                        pltpu.VMEM((H // 2, nblk, PV_ROWS, T), BF16),
                        pltpu.VMEM((H, 2 * T, 2 * KV_RANK), BF16),
                        pltpu.VMEM((H, 2 * T, 2 * LANES), BF16),
                        pltpu.VMEM((H, 2 * T, LANES), BF16),
                        pltpu.VMEM((SUBLANES, lanes), F32),
                        pltpu.VMEM((PV_ROWS, lanes), F32),
                        pltpu.VMEM((2, 2 * T, lanes), F32),
                        pltpu.VMEM((nblk, T, Q), F32)],
        compiler_params=_cparams(("parallel", "arbitrary")),
        name="dsa_fox_attention",
    )(hb, hb, hb, w_t, f_rows, hb, hb, hb, hb, f_rows, wuk_pad, wuv_pad, bias_tiles, bias_far,
      pq, cq, pk, ck)


def _proj_ln_kernel(a_ref, w_ref, x_ref, g_ref, b_ref, xo_ref, xb_ref, *, alpha, sub):
    for c in range(a_ref.shape[0] // sub):
        rows = slice(c * sub, (c + 1) * sub)
        y = alpha * x_ref[rows, :] + _dot(a_ref[rows, :], w_ref[...])
        out = _layer_norm_rows(y, g_ref[...], b_ref[...])
        xo_ref[rows, :] = out
        xb_ref[rows, :] = out.astype(BF16)


def _proj_ln(a, w, x, g, b, alpha, tm):
    n, k = a.shape
    d = w.shape[1]
    return pl.pallas_call(
        functools.partial(_proj_ln_kernel, alpha=alpha, sub=min(128, tm)),
        grid=(n // tm,),
        in_specs=[pl.BlockSpec((tm, k), lambda i: (i, 0)),
                  pl.BlockSpec((k, d), lambda i: (0, 0)),
                  pl.BlockSpec((tm, d), lambda i: (i, 0)),
                  pl.BlockSpec((1, d), lambda i: (0, 0)),
                  pl.BlockSpec((1, d), lambda i: (0, 0))],
        out_specs=[pl.BlockSpec((tm, d), lambda i: (i, 0)),
                   pl.BlockSpec((tm, d), lambda i: (i, 0))],
        out_shape=[jax.ShapeDtypeStruct((n, d), F32),
                   jax.ShapeDtypeStruct((n, d), BF16)],
        compiler_params=_cparams(("parallel",)),
        name="proj_residual_ln",
    )(a, w, x, g, b)


def _causal_conv3(h, tail, cw):
    w0, w1, w2 = cw[0:1, :], cw[1:2, :], cw[2:3, :]
    body = w0 * pltpu.roll(h, 2, axis=0) + w1 * pltpu.roll(h, 1, axis=0) + w2 * h
    head = h[0:SUBLANES, :]
    r = lax.broadcasted_iota(jnp.int32, head.shape, 0)
    hm1 = jnp.where(r < 1, pltpu.roll(tail, 1, axis=0), pltpu.roll(head, 1, axis=0))
    hm2 = jnp.where(r < 2, pltpu.roll(tail, 2, axis=0), pltpu.roll(head, 2, axis=0))
    out_head = w0 * hm2 + w1 * hm1 + w2 * head
    return jnp.concatenate([out_head, body[SUBLANES:, :]], axis=0)


def _ffn_up_kernel(x_ref, wg_ref, wv_ref, cg_ref, cv_ref, o_ref, tg_ref, tv_ref, *h_refs,
                   tiles_per_seq, sub):
    n_chunks = x_ref.shape[0] // sub
    hg_refs, hv_refs = h_refs[:n_chunks], h_refs[n_chunks:]
    first = (pl.program_id(1) % tiles_per_seq) == 0
    body_rows = slice(SUBLANES, SUBLANES + sub)

    def matmuls(c):
        x = x_ref[c * sub:(c + 1) * sub, :]
        hg_refs[c][body_rows, :] = _dot(x, wg_ref[...])
        hv_refs[c][body_rows, :] = _dot(x, wv_ref[...])

    def conv(h_ref, cw):
        return (cw[0:1, :] * h_ref[SUBLANES - 2:SUBLANES - 2 + sub, :]
                + cw[1:2, :] * h_ref[SUBLANES - 1:SUBLANES - 1 + sub, :]
                + cw[2:3, :] * h_ref[body_rows, :])

    def gate(c):
        if c == 0:
            hg_refs[0][0:SUBLANES, :] = jnp.where(first, 0.0, tg_ref[...])
            hv_refs[0][0:SUBLANES, :] = jnp.where(first, 0.0, tv_ref[...])
        else:
            hg_refs[c][0:SUBLANES, :] = hg_refs[c - 1][sub:sub + SUBLANES, :]
            hv_refs[c][0:SUBLANES, :] = hv_refs[c - 1][sub:sub + SUBLANES, :]
        g = conv(hg_refs[c], cg_ref[...])
        v = conv(hv_refs[c], cv_ref[...])
        half_g = 0.5 * g
        o_ref[c * sub:(c + 1) * sub, :] = ((half_g + half_g * jnp.tanh(half_g)) * v).astype(o_ref.dtype)

    matmuls(0)
    for c in range(n_chunks):
        if c + 1 < n_chunks:
            matmuls(c + 1)
        gate(c)
    tg_ref[...] = hg_refs[n_chunks - 1][sub:sub + SUBLANES, :]
    tv_ref[...] = hv_refs[n_chunks - 1][sub:sub + SUBLANES, :]


def _ffn_up(xb, w_up, conv_w, seq, tm, tn, sub):
    n, k = xb.shape
    dff = w_up.shape[1] // 2
    ncol = dff // tn
    return pl.pallas_call(
        functools.partial(_ffn_up_kernel, tiles_per_seq=seq // tm, sub=sub),
        grid=(ncol, n // tm),
        in_specs=[pl.BlockSpec((tm, k), lambda j, i: (i, 0)),
                  pl.BlockSpec((k, tn), lambda j, i: (0, j)),
                  pl.BlockSpec((k, tn), lambda j, i, ncol=ncol: (0, j + ncol)),
                  pl.BlockSpec((CONV_W, tn), lambda j, i: (0, j)),
                  pl.BlockSpec((CONV_W, tn), lambda j, i, ncol=ncol: (0, j + ncol))],
        out_specs=pl.BlockSpec((tm, tn), lambda j, i: (i, j)),
        out_shape=jax.ShapeDtypeStruct((n, dff), BF16),
        scratch_shapes=([pltpu.VMEM((SUBLANES, tn), F32)] * 2
                        + [pltpu.VMEM((sub + SUBLANES, tn), F32)] * (2 * (tm // sub))),
        compiler_params=_cparams(("arbitrary", "arbitrary")),
        name="ffn_up_conv_gate",
    )(xb, w_up, w_up, conv_w, conv_w)


def _gelu_tanh(x):
    return 0.5 * x * (1.0 + jnp.tanh(math.sqrt(2.0 / math.pi) * (x + 0.044715 * (x * x * x))))


def _group_mean(v, avg):
    hi = v.astype(BF16)
    lo = (v - hi.astype(F32)).astype(BF16)
    return _dot(hi, avg) + _dot(lo, avg)


def _odd_mixer_kernel(x_ref, w_ref, sg_ref, sw_ref, sb_ref, cw_ref, wo_ref, xf_ref, g_ref, b_ref,
                      xo_ref, xb_ref, o_ref, tail_ref, *, tiles_per_seq, alpha):
    tm = x_ref.shape[0]
    W = MIX_HALF
    first = (pl.program_id(0) % tiles_per_seq) == 0
    x = x_ref[...]
    lane = lax.broadcasted_iota(jnp.int32, (CHUNK, LANES), 1)
    lo_half = lane < HEAD_DIM
    r = lax.broadcasted_iota(jnp.int32, (W, W), 0) // HEAD_DIM
    c = lax.broadcasted_iota(jnp.int32, (W, W), 1) // HEAD_DIM
    avg = jnp.where(r == c, 1.0 / HEAD_DIM, 0.0).astype(BF16)
    tri = (lax.broadcasted_iota(jnp.int32, (CHUNK, CHUNK), 1)
           <= lax.broadcasted_iota(jnp.int32, (CHUNK, CHUNK), 0))

    u = _gelu_tanh(_dot(x, w_ref[:, 0:W]))
    v = _gelu_tanh(_dot(x, w_ref[:, W:2 * W]))
    vc = v - _group_mean(v, avg)
    var = _group_mean(vc * vc, avg)
    vn = (vc * lax.rsqrt(var + LN_EPS) * sg_ref[...]).astype(BF16)
    for n in range(tm // CHUNK):
        rows = slice(n * CHUNK, (n + 1) * CHUNK)
        for p in range(N_HEADS // 2):
            cols = slice(p * LANES, (p + 1) * LANES)
            vp = vn[rows, cols]
            w_e = jnp.where(tri, sw_ref[2 * p], 0.0).astype(BF16)
            w_o = jnp.where(tri, sw_ref[2 * p + 1], 0.0).astype(BF16)
            mix = (_dot(w_e, jnp.where(lo_half, vp, 0)) + _dot(w_o, jnp.where(lo_half, 0, vp))
                   + sb_ref[:, cols])
            o_ref[rows, cols] = (u[rows, cols] * mix).astype(o_ref.dtype)

    g_b = _dot(x, w_ref[:, 2 * W:3 * W])
    y = _dot(x, w_ref[:, 3 * W:4 * W]) * _dot(x, w_ref[:, 4 * W:5 * W])
    conv = _causal_conv3(y, jnp.where(first, 0.0, tail_ref[...]), cw_ref[...])
    tail_ref[...] = y[tm - SUBLANES:tm, :]
    o_ref[:, W:2 * W] = (g_b * conv).astype(o_ref.dtype)

    out = _layer_norm_rows(alpha * xf_ref[...] + _dot(o_ref[...], wo_ref[...]), g_ref[...], b_ref[...])
    xo_ref[...] = out
    xb_ref[...] = out.astype(BF16)


def _odd_mixer(xb, w_in, sgu_g, sgu_w, sgu_b_tile, conv_w, w_o, xf, g, b, alpha, seq, tm):
    n, k = xb.shape
    W = MIX_HALF
    d = w_o.shape[1]
    const2 = lambda shape: pl.BlockSpec(shape, lambda i: (0, 0))
    row_tile = lambda width: pl.BlockSpec((tm, width), lambda i: (i, 0))
    return pl.pallas_call(
        functools.partial(_odd_mixer_kernel, tiles_per_seq=seq // tm, alpha=alpha),
        grid=(n // tm,),
        in_specs=[row_tile(k), const2((k, 5 * W)), const2((1, W)),
                  pl.BlockSpec((N_HEADS, CHUNK, CHUNK), lambda i: (0, 0, 0)),
                  const2((CHUNK, W)), const2((CONV_W, W)),
                  const2((2 * W, d)), row_tile(d), const2((1, d)), const2((1, d))],
        out_specs=[row_tile(d), row_tile(d)],
        out_shape=[jax.ShapeDtypeStruct((n, d), F32), jax.ShapeDtypeStruct((n, d), BF16)],
        scratch_shapes=[pltpu.VMEM((tm, 2 * W), BF16),
                        pltpu.VMEM((SUBLANES, W), F32)],
        compiler_params=_cparams(("arbitrary",)),
        name="odd_mixer",
    )(xb, w_in, sgu_g, sgu_w, sgu_b_tile, conv_w, w_o, xf, g, b)


def _t5_bucket_np(dist):
    max_exact = N_BUCKETS // 2
    n = np.maximum(dist, 0)
    nf = np.maximum(n, 1).astype(np.float32)
    large = max_exact + (np.log(nf / max_exact) / np.float32(math.log(T5_MAX_DIST / max_exact))
                         * (N_BUCKETS - max_exact)).astype(np.int32)
    large = np.minimum(large, N_BUCKETS - 1)
    return np.where(n < max_exact, n, large)


def _bias_tables(rel_bias):
    T = BLK
    s = np.arange(T)[:, None]
    t = np.arange(T)[None, :]
    bucket = _t5_bucket_np(np.stack([t + d * T - s for d in range(3)]))
    far_bucket = N_BUCKETS - 1
    assert (bucket[2] == far_bucket).all()
    assert (_t5_bucket_np(np.arange(T + 1, 64 * T)) == far_bucket).all()
    rb = rel_bias.astype(F32) * LOG2E
    tiles = sum(jnp.where(jnp.asarray(bucket == b)[..., None], rb[b], 0.0) for b in range(N_BUCKETS))
    tiles = jnp.swapaxes(tiles, 2, 3).reshape(3, T, N_HEADS * T)
    far = jnp.tile(jnp.repeat(rb[far_bucket], T)[None, :], (1, 2))
    return tiles, far


def _placement_constants():
    H, L = N_HEADS, LANES
    pq = np.zeros((3 * L, H * L), np.float32)
    cq = np.zeros((1, H * L), np.float32)
    pk = np.zeros((3 * L, (H // 2) * L), np.float32)
    ck = np.zeros((1, (H // 2) * L), np.float32)
    for h in range(H):
        p, e = divmod(h, 2)
        for piece in range(3):
            pq[piece * L + h, h * L + 6 + piece] = 1.0
            cq[0, h * L + 3 * e + piece] = 1.0
            pk[piece * L + h, p * L + 3 * e + piece] = -1.0
            ck[0, p * L + 6 + piece] = 1.0
    return jnp.asarray(pq, BF16), jnp.asarray(cq), jnp.asarray(pk, BF16), jnp.asarray(ck)


def _even_weights(w_in, w_uk, w_uv):
    hd, H = HEAD_DIM, N_HEADS
    widths = (H * hd, KV_RANK, H * hd, hd, H, H * hd, H * hd, H * hd, H)
    offs = np.cumsum((0,) + widths)
    seg = lambda s: w_in[:, offs[s]:offs[s + 1]]
    q_a, c_kv, q_idx, k_idx, w_idx, q_b, k_b, v_b, f_in = [seg(s) for s in range(9)]
    scale = hd ** -0.5
    w_big = jnp.concatenate([q_a, q_idx * scale, q_b * (scale * LOG2E), k_b, v_b, c_kv,
                             k_idx, k_idx], axis=1).astype(BF16)
    pad = jnp.zeros((w_in.shape[0], LANES - 2 * H), w_in.dtype)
    w_small = jnp.concatenate([w_idx, f_in, pad], axis=1).astype(BF16)
    wuk_t = jnp.swapaxes(w_uk, 1, 2) * (scale * LOG2E)
    odd = (np.arange(H) % 2 == 1)[:, None, None]
    zk, zv = jnp.zeros_like(wuk_t), jnp.zeros_like(w_uv)
    wuk_pad = jnp.where(odd, jnp.concatenate([zk, wuk_t], axis=1), jnp.concatenate([wuk_t, zk], axis=1))
    wuv_pad = jnp.where(odd, jnp.concatenate([zv, w_uv], axis=2), jnp.concatenate([w_uv, zv], axis=2))
    return w_big, w_small, wuk_pad.astype(BF16), wuv_pad.astype(BF16)


def kernel(x, ln_g, ln_b, rel_bias, ev_w_in, ev_w_uk, ev_w_uv, ev_b_f, ev_w_o,
           od_w_in, od_sgu_g, od_sgu_w, od_sgu_b, od_conv_w, od_w_o,
           ffn_w_up, ffn_conv_w, ffn_w_down):
    bsz, seq, d = x.shape
    depth = ln_g.shape[0]
    n = bsz * seq
    alpha = (2.0 * depth) ** 0.25
    k_sel = min(TOPK_MAX, seq // 4)
    H = N_HEADS
    assert seq % QUERIES == 0 and seq % CHUNK == 0
    tm_ffn = min(1024, seq)

    bias_tiles, bias_far = _bias_tables(rel_bias)
    place = _placement_constants()
    xf = x.reshape(n, d)
    xb = xf
    for layer in range(depth):
        j = layer // 2
        g0, b0 = ln_g[layer, 0][None, :], ln_b[layer, 0][None, :]
        g1, b1 = ln_g[layer, 1][None, :], ln_b[layer, 1][None, :]
        if layer % 2 == 0:
            w_big, w_small, wuk_pad, wuv_pad = _even_weights(ev_w_in[j], ev_w_uk[j], ev_w_uv[j])
            hb, hs = _even_proj(xb, w_big, w_small, tm_ffn)
            hb = hb.reshape(bsz, seq, -1)
            hs_t = jnp.swapaxes(hs.reshape(bsz, seq, -1)[:, :, 0:2 * H], 1, 2)
            cum_f = _cumf(hs_t[:, H:2 * H], ev_b_f[j][:, None])
            f_rows = jnp.pad(jnp.swapaxes(cum_f, 1, 2), ((0, 0), (0, 0), (0, LANES - H)))
            mix = _attention(hb, hs_t[:, 0:H], f_rows, wuk_pad, wuv_pad, bias_tiles, bias_far,
                             place, k_sel)
            xf, xb = _proj_ln(mix.reshape(n, d), ev_w_o[j].astype(BF16), xf, g0, b0, alpha, tm_ffn)
        else:
            sgu_b_tile = jnp.repeat(jnp.swapaxes(od_sgu_b[j], 0, 1), HEAD_DIM, axis=1)
            xf, xb = _odd_mixer(xb, od_w_in[j].astype(BF16), od_sgu_g[j].reshape(1, -1), od_sgu_w[j],
                                sgu_b_tile, od_conv_w[j], od_w_o[j].astype(BF16), xf, g0, b0, alpha,
                                seq, min(1024, seq))
        act = _ffn_up(xb, ffn_w_up[layer].astype(BF16), ffn_conv_w[layer], seq, min(4096, seq), 256,
                      min(512, seq))
        xf, xb = _proj_ln(act, ffn_w_down[layer].astype(BF16), xf, g1, b1, alpha, tm_ffn)
    return xf.reshape(bsz, seq, d)
```

```python
import functools
import math

import numpy as np
import jax
import jax.numpy as jnp
from jax import lax
from jax.experimental import pallas as pl
from jax.experimental.pallas import tpu as pltpu

F32 = jnp.float32
BF16 = jnp.bfloat16

HEAD_DIM = 64
N_HEADS = 8
KV_RANK = 128
TOPK_MAX = 256
N_BUCKETS = 32
T5_MAX_DIST = 128
CHUNK = 128
CONV_W = 3
LN_EPS = 1e-5
MIX_HALF = N_HEADS * HEAD_DIM

LANES = 128
SUBLANES = 8
BF16_ROWS = 16
VMEM_LIMIT = 56 * 1024 * 1024

BLK = 128
QUERIES = 2 * BLK
PV_ROWS = KV_RANK + BF16_ROWS
NEG_BIG = -1e30
MASKED = -(2.0 ** 100)
F32_MAX = float(np.finfo(np.float32).max)
LOG2E = math.log2(math.e)
SEARCH_UNROLL = 3
LOOKAHEAD = 2
N_SLOTS = LOOKAHEAD + 1


def _cparams(sem):
    return pltpu.CompilerParams(dimension_semantics=sem, vmem_limit_bytes=VMEM_LIMIT)


def _dot(a, b):
    return jnp.dot(a, b, preferred_element_type=F32)


def _dot_nt(a, b):
    return lax.dot_general(a, b, (((1,), (1,)), ((), ())), preferred_element_type=F32)


def _layer_norm_rows(y, g, b):
    mu = jnp.mean(y, axis=-1, keepdims=True)
    yc = y - mu
    var = jnp.mean(yc * yc, axis=-1, keepdims=True)
    return yc * lax.rsqrt(var + LN_EPS) * g + b


def _tree(op, x):
    while x.shape[0] > 1:
        half = x.shape[0] // 2
        x = op(x[:half], x[half:])
    return x[0]


def _bf16_pieces(v):
    hi = v.astype(BF16)
    r1 = v - hi.astype(F32)
    mid = r1.astype(BF16)
    lo = (r1 - mid.astype(F32)).astype(BF16)
    return jnp.concatenate([hi, mid, lo], axis=1)


def _even_proj_kernel(x_ref, wb_ref, ws_ref, hb_ref, hs_ref):
    x = x_ref[...].astype(BF16)
    hb_ref[...] = _dot(x, wb_ref[...]).astype(hb_ref.dtype)
    hs_ref[...] = _dot(x, ws_ref[...])


def _even_proj(xb, w_big, w_small, tm):
    n, k = xb.shape
    mb = w_big.shape[1]
    ms = w_small.shape[1]
    return pl.pallas_call(
        _even_proj_kernel,
        grid=(n // tm,),
        in_specs=[pl.BlockSpec((tm, k), lambda i: (i, 0)),
                  pl.BlockSpec((k, mb), lambda i: (0, 0)),
                  pl.BlockSpec((k, ms), lambda i: (0, 0))],
        out_specs=[pl.BlockSpec((tm, mb), lambda i: (i, 0)),
                   pl.BlockSpec((tm, ms), lambda i: (i, 0))],
        out_shape=[jax.ShapeDtypeStruct((n, mb), BF16),
                   jax.ShapeDtypeStruct((n, ms), F32)],
        compiler_params=_cparams(("parallel",)),
        name="even_proj",
    )(xb, w_big, w_small)


def _cumf_kernel(f_ref, b_ref, o_ref):
    seq = f_ref.shape[-1]
    row = lax.broadcasted_iota(jnp.int32, (LANES, LANES), 0)
    col = lax.broadcasted_iota(jnp.int32, (LANES, LANES), 1)
    upper = jnp.where(row <= col, 1.0, 0.0).astype(F32)
    carry = jnp.zeros((f_ref.shape[0], 1), F32)
    for c in range(seq // LANES):
        z = f_ref[:, c * LANES:(c + 1) * LANES] + b_ref[...]
        lf = jnp.minimum(z, 0.0) - jnp.log1p(jnp.exp(-jnp.abs(z)))
        cs = jnp.dot(lf, upper, preferred_element_type=F32,
                     precision=lax.Precision.HIGHEST) + carry
        o_ref[:, c * LANES:(c + 1) * LANES] = cs
        carry = cs[:, LANES - 1:LANES]


def _cumf(f_t, b_f):
    bsz, nh, seq = f_t.shape
    return pl.pallas_call(
        _cumf_kernel,
        grid=(bsz,),
        in_specs=[pl.BlockSpec((None, nh, seq), lambda b: (b, 0, 0)),
                  pl.BlockSpec((nh, 1), lambda b: (0, 0))],
        out_specs=pl.BlockSpec((None, nh, seq), lambda b: (b, 0, 0)),
        out_shape=jax.ShapeDtypeStruct((bsz, nh, seq), F32),
        compiler_params=_cparams(("parallel",)),
        name="fox_cumf",
    )(f_t, b_f)


def _attn_kernel(qa_ref, qi_ref, qb_ref, wt_ref, fq_ref, kb_ref, vb_ref, ckv_ref, kidx_ref, fk_ref,
                 wuk_ref, wuv_ref, bias_ref, bfar_ref, pq_ref, cq_ref, pk_ref, ck_ref,
                 o_ref,
                 score_ref, lhsd_ref, ct_ref, kaug_ref, vt_ref, qsel_ref, qaug_ref, qim_ref,
                 m_ref, acc_ref, work_ref, *, k_sel):
    T = BLK
    Q = QUERIES
    H = N_HEADS
    NP = H // 2
    NT = 2 * NP
    TW = 2 * T
    g = pl.program_id(1)
    i0 = 2 * g
    nblk = ct_ref.shape[0]
    lane = lax.broadcasted_iota(jnp.int32, (T, LANES), 1)
    lo_half = lane < HEAD_DIM
    ones_rows = jnp.ones((BF16_ROWS, T), BF16)
    causal_qq = (lax.broadcasted_iota(jnp.int32, (Q, Q), 0) <= lax.broadcasted_iota(jnp.int32, (Q, Q), 1))

    @pl.when(g == 0)
    def _per_batch():
        def prep(j, c):
            r = pl.ds(pl.multiple_of(j * T, T), T)
            cb = ckv_ref[r, :]
            lhsd_ref[0, r, 0:KV_RANK] = cb
            lhsd_ref[1, r, 0:KV_RANK] = cb
            ct_ref[j, 0:KV_RANK, :] = cb.astype(F32).T.astype(BF16)
            ct_ref[j, KV_RANK:PV_ROWS, :] = ones_rows
            ext = (_dot(_bf16_pieces(fk_ref[r, :] * LOG2E), pk_ref[...]) + ck_ref[...]).astype(BF16)
            for p in range(NP):
                cols = slice(p * LANES, (p + 1) * LANES)
                kaug_ref[p, r, 0:LANES] = kb_ref[r, cols]
                kaug_ref[p, r, LANES:2 * LANES] = ext[:, cols]
                vt_ref[p, j, 0:LANES, :] = vb_ref[r, cols].astype(F32).T.astype(BF16)
                vt_ref[p, j, LANES:PV_ROWS, :] = ones_rows
            return c
        lax.fori_loop(0, nblk, prep, 0)

    w_rows = wt_ref[...] * (H ** -0.5)
    ext_q = (_dot(_bf16_pieces(fq_ref[...] * LOG2E), pq_ref[...]) + cq_ref[...]).astype(BF16)
    eye = jnp.where(lax.broadcasted_iota(jnp.int32, (T, T), 0) == lax.broadcasted_iota(jnp.int32, (T, T), 1),
                    1.0, 0.0).astype(BF16)
    for tt in range(NT):
        hf, p = divmod(tt, NP)
        qrows = slice(hf * T, (hf + 1) * T)
        cols = slice(p * LANES, (p + 1) * LANES)
        qa_pair, qi_pair, qb_pair = qa_ref[qrows, cols], qi_ref[qrows, cols], qb_ref[qrows, cols]
        for e in range(2):
            h = 2 * p + e
            rows = slice(e * T, (e + 1) * T)
            mine = lo_half if e == 0 else jnp.logical_not(lo_half)
            qsel_ref[tt, rows, 0:KV_RANK] = _dot(qa_pair, wuk_ref[h]).astype(BF16)
            qsel_ref[tt, rows, KV_RANK:2 * KV_RANK] = eye
            qaug_ref[tt, rows, 0:LANES] = jnp.where(mine, qb_pair, 0)
            qaug_ref[tt, rows, LANES:2 * LANES] = ext_q[qrows, h * LANES:(h + 1) * LANES]
            qim_ref[tt, rows, :] = jnp.where(mine, qi_pair, 0)

    def reset_state():
        m_ref[...] = jnp.full(m_ref.shape, NEG_BIG, F32)
        acc_ref[...] = jnp.zeros(acc_ref.shape, F32)

    def tile_cols(tt):
        return slice(tt * TW, (tt + 1) * TW)

    def online_update(tt, s, pv_lhs):
        cols = tile_cols(tt)
        m_old = m_ref[:, cols]
        s_max = _tree(jnp.maximum, s.reshape(-1, SUBLANES, TW))
        m_new = jnp.maximum(m_old, jnp.max(s_max, axis=0, keepdims=True))
        alpha = jnp.exp2(m_old - m_new)
        pr = jnp.exp2((s - m_new[0:1, :]).astype(BF16))
        acc_ref[:, cols] = acc_ref[:, cols] * alpha[0:1, :] + _dot(pv_lhs, pr)
        m_ref[:, cols] = m_new

    def key_rows(j, nblocks):
        return pl.ds(pl.multiple_of(j * T, T), nblocks * T)

    def blocks_t(load, j, nblocks):
        return load(j) if nblocks == 1 else jnp.concatenate([load(j + b) for b in range(nblocks)], axis=1)

    def slot_tile(slot, tt, nblocks):
        return (slot * NT + tt, slice(0, nblocks * T), slice(None))

    def run_staged(steps):
        for k in range(min(LOOKAHEAD, len(steps))):
            steps[k][0](k % N_SLOTS)
        for k, (_, absorb_k) in enumerate(steps):
            if k + LOOKAHEAD < len(steps):
                steps[k + LOOKAHEAD][0]((k + LOOKAHEAD) % N_SLOTS)
            absorb_k(k % N_SLOTS)

    def staged_steps(j0, n_steps, logits, absorb):
        run_staged([(functools.partial(logits, j0 + 2 * k, 2), functools.partial(absorb, j0 + 2 * k, 2))
                    for k in range(n_steps)])

    def for_far_blocks(n_far, logits, absorb):
        n_oct = lax.shift_right_logical(n_far, 3)

        def oct_body(jj, c):
            staged_steps(8 * jj, 4, logits, absorb)
            return c
        lax.fori_loop(0, n_oct, oct_body, 0)

        @pl.when((n_far & 4) != 0)
        def _quad():
            staged_steps(8 * n_oct, 2, logits, absorb)

        return (n_far & 2) != 0, 8 * n_oct + (n_far & 4)

    def finish_blocks(leftover, far_step, before_step, diag_step):
        has_pair, pair_j = leftover

        @pl.when(jnp.logical_and(g >= 1, has_pair))
        def _three():
            run_staged([far_step(pair_j), before_step, diag_step])

        @pl.when(jnp.logical_and(g >= 1, jnp.logical_not(has_pair)))
        def _two():
            run_staged([before_step, diag_step])

        @pl.when(g == 0)
        def _first():
            run_staged([diag_step])

    def idx_scores(j, nblocks):
        k_blk = kidx_ref[key_rows(j, nblocks), :]
        acc = [jnp.zeros((nblocks * T, T), F32), jnp.zeros((nblocks * T, T), F32)]
        for tt in range(NT):
            hf, p = divmod(tt, NP)
            s = _dot_nt(k_blk, qim_ref[tt])
            for e in range(2):
                h = 2 * p + e
                w_h = w_rows[h:h + 1, hf * T:(hf + 1) * T]
                acc[hf] = acc[hf] + w_h * jnp.maximum(s[:, e * T:(e + 1) * T], 0.0)
        return jnp.concatenate(acc, axis=1)

    def fox_logits(j, nblocks, slot, causal=False):
        for tt in range(NT):
            hf, p = divmod(tt, NP)
            s = _dot_nt(kaug_ref[p, key_rows(j, nblocks), :], qaug_ref[tt])
            if causal:
                ok = causal_qq[:, hf * T:(hf + 1) * T]
                s = jnp.where(jnp.concatenate([ok, ok], axis=1), s, MASKED)
            work_ref[slot_tile(slot, tt, nblocks)] = s

    def fox_absorb(j, nblocks, slot):
        for tt in range(NT):
            p = tt % NP
            online_update(tt, work_ref[slot_tile(slot, tt, nblocks)],
                          blocks_t(lambda jb: vt_ref[p, jb], j, nblocks))

    def idx_fox_logits(j, nblocks, slot):
        score_ref[pl.ds(j, nblocks)] = idx_scores(j, nblocks).reshape(nblocks, T, Q)
        fox_logits(j, nblocks, slot)

    reset_state()
    leftover = for_far_blocks(jnp.maximum(i0 - 2, 0), idx_fox_logits, fox_absorb)

    def diag_logits(slot):
        score_ref[pl.ds(i0, 2)] = jnp.where(causal_qq, idx_scores(i0, 2), -jnp.inf).reshape(2, T, Q)
        fox_logits(i0, 2, slot, causal=True)

    def fox_far_step(j):
        return (lambda slot: idx_fox_logits(j, 2, slot), lambda slot: fox_absorb(j, 2, slot))

    finish_blocks(leftover, fox_far_step, fox_far_step(i0 - 2),
                  (diag_logits, lambda slot: fox_absorb(i0, 2, slot)))

    row_lo = lax.broadcasted_iota(jnp.int32, (LANES, T), 0) < HEAD_DIM
    for tt in range(NT):
        hf, p = divmod(tt, NP)
        c0 = slice(tt * TW, tt * TW + T)
        c1 = slice(tt * TW + T, (tt + 1) * TW)
        o_t = jnp.where(row_lo, acc_ref[0:LANES, c0] / acc_ref[LANES:LANES + 1, c0],
                        acc_ref[0:LANES, c1] / acc_ref[LANES:LANES + 1, c1])
        o_ref[hf * T:(hf + 1) * T, MIX_HALF + p * LANES:MIX_HALF + (p + 1) * LANES] = (
            o_t.T.astype(o_ref.dtype))

    kf = float(k_sel)
    t_pos = (g * Q + lax.broadcasted_iota(jnp.int32, (SUBLANES, Q), 1)).astype(F32)
    searching = (t_pos + 1.0) > kf
    G2 = 2 * T // SUBLANES
    n_pair = g + 1
    key_in_pair = (lax.broadcasted_iota(jnp.int32, (G2, SUBLANES, Q), 0) * SUBLANES
                   + lax.broadcasted_iota(jnp.int32, (G2, SUBLANES, Q), 1))

    def rep(v):
        return jnp.broadcast_to(v, (SUBLANES, Q))

    def score_pair(jj):
        return score_ref[pl.ds(2 * jj, 2)].reshape(G2, SUBLANES, Q)

    def count(*pred_fns, load=score_pair):
        def body(jj, cs):
            s = load(jj)
            return tuple(c + _tree(jnp.add, jnp.where(f(s, jj), 1.0, 0.0)) for c, f in zip(cs, pred_fns))
        zero = jnp.zeros((SUBLANES, Q), F32)
        cs = lax.fori_loop(0, n_pair, body, (zero,) * len(pred_fns))
        return tuple(rep(jnp.sum(c, axis=0, keepdims=True)) for c in cs)

    def count_ge(thr):
        return count(lambda s, jj: s >= thr[None])[0]

    def select_threshold():
        def first_body(jj, carry):
            mn, mx, ge0, gt0 = carry
            s = score_pair(jj)
            return (jnp.minimum(mn, _tree(jnp.minimum, jnp.where(s == -jnp.inf, jnp.inf, s))),
                    jnp.maximum(mx, _tree(jnp.maximum, s)),
                    ge0 + _tree(jnp.add, jnp.where(s >= 0.0, 1.0, 0.0)),
                    gt0 + _tree(jnp.add, jnp.where(s > 0.0, 1.0, 0.0)))

        zero = jnp.zeros((SUBLANES, Q), F32)
        mn, mx, c_ge0, c_gt0 = lax.fori_loop(0, n_pair, first_body, (zero + jnp.inf, zero - jnp.inf, zero, zero))
        mn = rep(jnp.min(mn, axis=0, keepdims=True))
        mx = rep(jnp.max(mx, axis=0, keepdims=True))
        c_ge0 = rep(jnp.sum(c_ge0, axis=0, keepdims=True))
        c_gt0 = rep(jnp.sum(c_gt0, axis=0, keepdims=True))
        above_mx = mx + jnp.maximum(jnp.abs(mx) * 2.0 ** -20, 1e-30)
        thr_ge0 = c_ge0 >= kf
        thr_gt0 = c_gt0 >= kf
        lo0 = jnp.where(thr_ge0, 0.0, mn)
        cnt_lo0 = jnp.where(thr_ge0, c_ge0, t_pos + 1.0)
        hi0 = jnp.where(thr_gt0, above_mx, 0.0)
        cnt_hi0 = jnp.where(thr_gt0, 0.0, jnp.where(thr_ge0, c_gt0, c_ge0))
        undecided = thr_gt0 | jnp.logical_not(thr_ge0)
        active0 = jnp.where(searching & undecided & (cnt_lo0 > kf), 1.0, 0.0)

        def cond(st):
            return jnp.logical_and(st[5] > 0.0, st[6] < 2200)

        def bisect(st):
            lo, hi, cnt_lo, cnt_hi, active = st
            mid = 0.5 * lo + 0.5 * hi
            adjacent = (mid <= lo) | (mid >= hi)
            c = count_ge(mid)
            act = active > 0.0
            go_lo = act & jnp.logical_not(adjacent) & (c >= kf)
            go_hi = act & jnp.logical_not(adjacent) & (c < kf)
            lo = jnp.where(go_lo, mid, lo)
            cnt_lo = jnp.where(go_lo, c, cnt_lo)
            hi = jnp.where(go_hi, mid, hi)
            cnt_hi = jnp.where(go_hi, c, cnt_hi)
            active = jnp.where(act & jnp.logical_not(adjacent) & (cnt_lo > kf), 1.0, 0.0)
            return lo, hi, cnt_lo, cnt_hi, active

        def body(st):
            inner = st[:5]
            for _ in range(SEARCH_UNROLL):
                inner = bisect(inner)
            return inner + (jnp.max(inner[4]), st[6] + SEARCH_UNROLL)

        lo, hi, cnt_lo, cnt_hi, _, _, _ = lax.while_loop(
            cond, body, (lo0, hi0, cnt_lo0, cnt_hi0, active0, jnp.max(active0), 0))
        thr = jnp.where(searching, lo, -F32_MAX)

        tied = searching & (cnt_lo > kf)
        need = kf - cnt_hi

        @pl.when(jnp.max(jnp.where(tied, 1.0, 0.0)) > 0.0)
        def _break_ties():
            def tie_pair(jj):
                return work_ref[jj].reshape(G2, SUBLANES, Q)

            def mark_body(jj, c):
                key_index = (2 * jj * T + key_in_pair).astype(F32)
                work_ref[jj] = (
                    jnp.where(score_pair(jj) == thr[None], key_index, jnp.inf).reshape(2 * T, Q))
                return c
            lax.fori_loop(0, n_pair, mark_body, 0)

            def count_tied_upto(jcut):
                return count(lambda e, jj: e <= jcut[None], load=tie_pair)[0]

            def bs_body(_, st):
                jl, jh = st
                jm = jnp.floor(0.5 * (jl + jh))
                ok = count_tied_upto(jm) >= need
                return jnp.where(ok, jl, jm), jnp.where(ok, jm, jh)

            n_keys = (2 * n_pair * T).astype(F32)
            jl0 = jnp.full((SUBLANES, Q), -1.0, F32)
            jh0 = jnp.zeros((SUBLANES, Q), F32) + (n_keys - 1.0)
            n_steps = int(math.ceil(math.log2(nblk * T))) + 1
            _, jcut = lax.fori_loop(0, n_steps, bs_body, (jl0, jh0))

            def drop_body(jj, c):
                e = tie_pair(jj)
                drop = tied[None] & (e > jcut[None]) & (e < jnp.inf)
                score_ref[pl.ds(2 * jj, 2)] = jnp.where(drop, -jnp.inf, score_pair(jj)).reshape(2, T, Q)
                return c
            lax.fori_loop(0, n_pair, drop_body, 0)

        return thr

    thr = lax.cond((i0 + 2) * T > k_sel, select_threshold,
                   lambda: jnp.full((SUBLANES, Q), -F32_MAX, F32))

    def mask_body(jj, c):
        add = jnp.where(score_pair(jj) >= thr[None], 0.0, MASKED).reshape(2 * T, Q).astype(BF16)
        for hf in range(2):
            lhsd_ref[hf, key_rows(2 * jj, 2), KV_RANK:2 * KV_RANK] = add[:, hf * T:(hf + 1) * T]
        return c

    lax.fori_loop(0, n_pair, mask_body, 0)

    def dsa_logits(j, nblocks, slot, bias=None):
        for tt in range(NT):
            hf, p = divmod(tt, NP)
            s = _dot_nt(lhsd_ref[hf, key_rows(j, nblocks), :], qsel_ref[tt])
            if bias is not None:
                s = s + bias(hf, slice(p * TW, (p + 1) * TW))
            work_ref[slot_tile(slot, tt, nblocks)] = s

    def dsa_absorb(j, nblocks, slot):
        pv_lhs = blocks_t(lambda jb: ct_ref[jb], j, nblocks)
        for tt in range(NT):
            online_update(tt, work_ref[slot_tile(slot, tt, nblocks)], pv_lhs)

    def bias_before(hf, cols):
        return jnp.concatenate([bias_ref[2, :, cols], bias_ref[1 + hf, :, cols]], axis=0)

    def bias_diag(hf, cols):
        first = bias_ref[hf, :, cols]
        return jnp.concatenate([first, bias_ref[0, :, cols]], axis=0)

    reset_state()
    leftover = for_far_blocks(jnp.maximum(i0 - 2, 0), dsa_logits, dsa_absorb)

    def dsa_far_step(j):
        return (lambda slot: dsa_logits(j, 2, slot), lambda slot: dsa_absorb(j, 2, slot))

    def absorb_first_biased(j):
        def absorb(slot):
            m_ref[...] = m_ref[...] + bfar_ref[...]
            dsa_absorb(j, 2, slot)
        return absorb

    diag_step = lambda absorb: (lambda slot: dsa_logits(i0, 2, slot, bias_diag), absorb)
    before_step = (lambda slot: dsa_logits(i0 - 2, 2, slot, bias_before), absorb_first_biased(i0 - 2))
    has_pair, pair_j = leftover

    @pl.when(jnp.logical_and(g >= 1, has_pair))
    def _dsa_three():
        run_staged([dsa_far_step(pair_j), before_step, diag_step(lambda slot: dsa_absorb(i0, 2, slot))])

    @pl.when(jnp.logical_and(g >= 1, jnp.logical_not(has_pair)))
    def _dsa_two():
        run_staged([before_step, diag_step(lambda slot: dsa_absorb(i0, 2, slot))])

    @pl.when(g == 0)
    def _dsa_first():
        run_staged([diag_step(absorb_first_biased(0))])

    for tt in range(NT):
        hf, p = divmod(tt, NP)
        o_pair = jnp.zeros((T, LANES), F32)
        for e in range(2):
            cols = slice(tt * TW + e * T, tt * TW + (e + 1) * T)
            o_t = acc_ref[0:KV_RANK, cols] / acc_ref[KV_RANK:KV_RANK + 1, cols]
            o_pair = o_pair + _dot(o_t.T.astype(BF16), wuv_ref[2 * p + e])
        o_ref[hf * T:(hf + 1) * T, p * LANES:(p + 1) * LANES] = o_pair.astype(o_ref.dtype)


_EV_QA, _EV_QI, _EV_QB, _EV_KB, _EV_VB = 0, 1, 2, 3, 4
_EV_CKV, _EV_KIDX = 20, 21


def _attention(hb, w_t, f_rows, wuk_pad, wuv_pad, bias_tiles, bias_far, place, k_sel):
    bsz, seq, _ = hb.shape
    T, Q, H = BLK, QUERIES, N_HEADS
    nblk = seq // T
    lanes = 2 * H * T
    pq, cq, pk, ck = place
    wide = lambda c: pl.BlockSpec((None, Q, MIX_HALF), lambda b, i, c=c: (b, i, c))
    once = dict(pipeline_mode=pl.Buffered(1))
    full_wide = lambda c: pl.BlockSpec((None, seq, MIX_HALF), lambda b, i, c=c: (b, 0, c), **once)
    full_narrow = lambda c: pl.BlockSpec((None, seq, LANES), lambda b, i, c=c: (b, 0, c), **once)
    const = lambda a: pl.BlockSpec(a.shape, lambda b, i, nd=a.ndim: (0,) * nd, **once)
    return pl.pallas_call(
        functools.partial(_attn_kernel, k_sel=k_sel),
        grid=(bsz, seq // Q),
        in_specs=[wide(_EV_QA), wide(_EV_QI), wide(_EV_QB),
                  pl.BlockSpec((None, H, Q), lambda b, i: (b, 0, i)),
                  pl.BlockSpec((None, Q, LANES), lambda b, i: (b, i, 0)),
                  full_wide(_EV_KB), full_wide(_EV_VB),
                  full_narrow(_EV_CKV), full_narrow(_EV_KIDX),
                  pl.BlockSpec((None, seq, LANES), lambda b, i: (b, 0, 0), **once),
                  const(wuk_pad), const(wuv_pad), const(bias_tiles), const(bias_far),
                  const(pq), const(cq), const(pk), const(ck)],
        out_specs=pl.BlockSpec((None, Q, 2 * MIX_HALF), lambda b, i: (b, i, 0)),
        out_shape=jax.ShapeDtypeStruct((bsz, seq, 2 * MIX_HALF), BF16),
        scratch_shapes=[pltpu.VMEM((nblk, T, Q), F32),
                        pltpu.VMEM((2, seq, 2 * KV_RANK), BF16),
                        pltpu.VMEM((nblk, PV_ROWS, T), BF16),
                        pltpu.VMEM((H // 2, seq, 2 * LANES), BF16),
                        pltpu.VMEM((H // 2, nblk, PV_ROWS, T), BF16),
                        pltpu.VMEM((H, 2 * T, 2 * KV_RANK), BF16),
                        pltpu.VMEM((H, 2 * T, 2 * LANES), BF16),
                        pltpu.VMEM((H, 2 * T, LANES), BF16),
                        pltpu.VMEM((SUBLANES, lanes), F32),
                        pltpu.VMEM((PV_ROWS, lanes), F32),
                        pltpu.VMEM((max(N_SLOTS * H, nblk // 2), 2 * T, 2 * T), F32)],
        compiler_params=_cparams(("parallel", "arbitrary")),
        name="dsa_fox_attention",
    )(hb, hb, hb, w_t, f_rows, hb, hb, hb, hb, f_rows, wuk_pad, wuv_pad, bias_tiles, bias_far,
      pq, cq, pk, ck)


def _proj_ln_kernel(a_ref, w_ref, x_ref, g_ref, b_ref, xo_ref, xb_ref, *, alpha, sub):
    for c in range(a_ref.shape[0] // sub):
        rows = slice(c * sub, (c + 1) * sub)
        y = alpha * x_ref[rows, :] + _dot(a_ref[rows, :], w_ref[...])
        out = _layer_norm_rows(y, g_ref[...], b_ref[...])
        xo_ref[rows, :] = out
        xb_ref[rows, :] = out.astype(BF16)


def _proj_ln(a, w, x, g, b, alpha, tm):
    n, k = a.shape
    d = w.shape[1]
    return pl.pallas_call(
        functools.partial(_proj_ln_kernel, alpha=alpha, sub=min(128, tm)),
        grid=(n // tm,),
        in_specs=[pl.BlockSpec((tm, k), lambda i: (i, 0)),
                  pl.BlockSpec((k, d), lambda i: (0, 0)),
                  pl.BlockSpec((tm, d), lambda i: (i, 0)),
                  pl.BlockSpec((1, d), lambda i: (0, 0)),
                  pl.BlockSpec((1, d), lambda i: (0, 0))],
        out_specs=[pl.BlockSpec((tm, d), lambda i: (i, 0)),
                   pl.BlockSpec((tm, d), lambda i: (i, 0))],
        out_shape=[jax.ShapeDtypeStruct((n, d), F32),
                   jax.ShapeDtypeStruct((n, d), BF16)],
        compiler_params=_cparams(("parallel",)),
        name="proj_residual_ln",
    )(a, w, x, g, b)


def _causal_conv3(h, tail, cw):
    w0, w1, w2 = cw[0:1, :], cw[1:2, :], cw[2:3, :]
    body = w0 * pltpu.roll(h, 2, axis=0) + w1 * pltpu.roll(h, 1, axis=0) + w2 * h
    head = h[0:SUBLANES, :]
    r = lax.broadcasted_iota(jnp.int32, head.shape, 0)
    hm1 = jnp.where(r < 1, pltpu.roll(tail, 1, axis=0), pltpu.roll(head, 1, axis=0))
    hm2 = jnp.where(r < 2, pltpu.roll(tail, 2, axis=0), pltpu.roll(head, 2, axis=0))
    out_head = w0 * hm2 + w1 * hm1 + w2 * head
    return jnp.concatenate([out_head, body[SUBLANES:, :]], axis=0)


def _ffn_up_kernel(x_ref, wg_ref, wv_ref, cg_ref, cv_ref, o_ref, tg_ref, tv_ref, *h_refs,
                   tiles_per_seq, sub):
    n_chunks = x_ref.shape[0] // sub
    hg_refs, hv_refs = h_refs[:n_chunks], h_refs[n_chunks:]
    first = (pl.program_id(1) % tiles_per_seq) == 0
    body_rows = slice(SUBLANES, SUBLANES + sub)

    def matmuls(c):
        x = x_ref[c * sub:(c + 1) * sub, :]
        hg_refs[c][body_rows, :] = _dot(x, wg_ref[...])
        hv_refs[c][body_rows, :] = _dot(x, wv_ref[...])

    def conv(h_ref, cw):
        return (cw[0:1, :] * h_ref[SUBLANES - 2:SUBLANES - 2 + sub, :]
                + cw[1:2, :] * h_ref[SUBLANES - 1:SUBLANES - 1 + sub, :]
                + cw[2:3, :] * h_ref[body_rows, :])

    def gate(c):
        if c == 0:
            hg_refs[0][0:SUBLANES, :] = jnp.where(first, 0.0, tg_ref[...])
            hv_refs[0][0:SUBLANES, :] = jnp.where(first, 0.0, tv_ref[...])
        else:
            hg_refs[c][0:SUBLANES, :] = hg_refs[c - 1][sub:sub + SUBLANES, :]
            hv_refs[c][0:SUBLANES, :] = hv_refs[c - 1][sub:sub + SUBLANES, :]
        g = conv(hg_refs[c], cg_ref[...])
        v = conv(hv_refs[c], cv_ref[...])
        half_g = 0.5 * g
        o_ref[c * sub:(c + 1) * sub, :] = ((half_g + half_g * jnp.tanh(half_g)) * v).astype(o_ref.dtype)

    matmuls(0)
    for c in range(n_chunks):
        if c + 1 < n_chunks:
            matmuls(c + 1)
        gate(c)
    tg_ref[...] = hg_refs[n_chunks - 1][sub:sub + SUBLANES, :]
    tv_ref[...] = hv_refs[n_chunks - 1][sub:sub + SUBLANES, :]


def _ffn_up(xb, w_up, conv_w, seq, tm, tn, sub):
    n, k = xb.shape
    dff = w_up.shape[1] // 2
    ncol = dff // tn
    return pl.pallas_call(
        functools.partial(_ffn_up_kernel, tiles_per_seq=seq // tm, sub=sub),
        grid=(ncol, n // tm),
        in_specs=[pl.BlockSpec((tm, k), lambda j, i: (i, 0)),
                  pl.BlockSpec((k, tn), lambda j, i: (0, j)),
                  pl.BlockSpec((k, tn), lambda j, i, ncol=ncol: (0, j + ncol)),
                  pl.BlockSpec((CONV_W, tn), lambda j, i: (0, j)),
                  pl.BlockSpec((CONV_W, tn), lambda j, i, ncol=ncol: (0, j + ncol))],
        out_specs=pl.BlockSpec((tm, tn), lambda j, i: (i, j)),
        out_shape=jax.ShapeDtypeStruct((n, dff), BF16),
        scratch_shapes=([pltpu.VMEM((SUBLANES, tn), F32)] * 2
                        + [pltpu.VMEM((sub + SUBLANES, tn), F32)] * (2 * (tm // sub))),
        compiler_params=_cparams(("arbitrary", "arbitrary")),
        name="ffn_up_conv_gate",
    )(xb, w_up, w_up, conv_w, conv_w)


def _gelu_tanh(x):
    return 0.5 * x * (1.0 + jnp.tanh(math.sqrt(2.0 / math.pi) * (x + 0.044715 * (x * x * x))))


def _group_mean(v, avg):
    hi = v.astype(BF16)
    lo = (v - hi.astype(F32)).astype(BF16)
    return _dot(hi, avg) + _dot(lo, avg)


def _odd_mixer_kernel(x_ref, w_ref, sg_ref, sw_ref, sb_ref, cw_ref, wo_ref, xf_ref, g_ref, b_ref,
                      xo_ref, xb_ref, o_ref, tail_ref, *, tiles_per_seq, alpha):
    tm = x_ref.shape[0]
    W = MIX_HALF
    first = (pl.program_id(0) % tiles_per_seq) == 0
    x = x_ref[...]
    lane = lax.broadcasted_iota(jnp.int32, (CHUNK, LANES), 1)
    lo_half = lane < HEAD_DIM
    r = lax.broadcasted_iota(jnp.int32, (W, W), 0) // HEAD_DIM
    c = lax.broadcasted_iota(jnp.int32, (W, W), 1) // HEAD_DIM
    avg = jnp.where(r == c, 1.0 / HEAD_DIM, 0.0).astype(BF16)
    tri = (lax.broadcasted_iota(jnp.int32, (CHUNK, CHUNK), 1)
           <= lax.broadcasted_iota(jnp.int32, (CHUNK, CHUNK), 0))

    u = _gelu_tanh(_dot(x, w_ref[:, 0:W]))
    v = _gelu_tanh(_dot(x, w_ref[:, W:2 * W]))
    vc = v - _group_mean(v, avg)
    var = _group_mean(vc * vc, avg)
    vn = (vc * lax.rsqrt(var + LN_EPS) * sg_ref[...]).astype(BF16)
    for n in range(tm // CHUNK):
        rows = slice(n * CHUNK, (n + 1) * CHUNK)
        for p in range(N_HEADS // 2):
            cols = slice(p * LANES, (p + 1) * LANES)
            vp = vn[rows, cols]
            w_e = jnp.where(tri, sw_ref[2 * p], 0.0).astype(BF16)
            w_o = jnp.where(tri, sw_ref[2 * p + 1], 0.0).astype(BF16)
            mix = (_dot(w_e, jnp.where(lo_half, vp, 0)) + _dot(w_o, jnp.where(lo_half, 0, vp))
                   + sb_ref[:, cols])
            o_ref[rows, cols] = (u[rows, cols] * mix).astype(o_ref.dtype)

    g_b = _dot(x, w_ref[:, 2 * W:3 * W])
    y = _dot(x, w_ref[:, 3 * W:4 * W]) * _dot(x, w_ref[:, 4 * W:5 * W])
    conv = _causal_conv3(y, jnp.where(first, 0.0, tail_ref[...]), cw_ref[...])
    tail_ref[...] = y[tm - SUBLANES:tm, :]
    o_ref[:, W:2 * W] = (g_b * conv).astype(o_ref.dtype)

    out = _layer_norm_rows(alpha * xf_ref[...] + _dot(o_ref[...], wo_ref[...]), g_ref[...], b_ref[...])
    xo_ref[...] = out
    xb_ref[...] = out.astype(BF16)


def _odd_mixer(xb, w_in, sgu_g, sgu_w, sgu_b_tile, conv_w, w_o, xf, g, b, alpha, seq, tm):
    n, k = xb.shape
    W = MIX_HALF
    d = w_o.shape[1]
    const2 = lambda shape: pl.BlockSpec(shape, lambda i: (0, 0))
    row_tile = lambda width: pl.BlockSpec((tm, width), lambda i: (i, 0))
    return pl.pallas_call(
        functools.partial(_odd_mixer_kernel, tiles_per_seq=seq // tm, alpha=alpha),
        grid=(n // tm,),
        in_specs=[row_tile(k), const2((k, 5 * W)), const2((1, W)),
                  pl.BlockSpec((N_HEADS, CHUNK, CHUNK), lambda i: (0, 0, 0)),
                  const2((CHUNK, W)), const2((CONV_W, W)),
                  const2((2 * W, d)), row_tile(d), const2((1, d)), const2((1, d))],
        out_specs=[row_tile(d), row_tile(d)],
        out_shape=[jax.ShapeDtypeStruct((n, d), F32), jax.ShapeDtypeStruct((n, d), BF16)],
        scratch_shapes=[pltpu.VMEM((tm, 2 * W), BF16),
                        pltpu.VMEM((SUBLANES, W), F32)],
        compiler_params=_cparams(("arbitrary",)),
        name="odd_mixer",
    )(xb, w_in, sgu_g, sgu_w, sgu_b_tile, conv_w, w_o, xf, g, b)


def _t5_bucket_np(dist):
    max_exact = N_BUCKETS // 2
    n = np.maximum(dist, 0)
    nf = np.maximum(n, 1).astype(np.float32)
    large = max_exact + (np.log(nf / max_exact) / np.float32(math.log(T5_MAX_DIST / max_exact))
                         * (N_BUCKETS - max_exact)).astype(np.int32)
    large = np.minimum(large, N_BUCKETS - 1)
    return np.where(n < max_exact, n, large)


def _bias_tables(rel_bias):
    T = BLK
    s = np.arange(T)[:, None]
    t = np.arange(T)[None, :]
    bucket = _t5_bucket_np(np.stack([t + d * T - s for d in range(3)]))
    far_bucket = N_BUCKETS - 1
    assert (bucket[2] == far_bucket).all()
    assert (_t5_bucket_np(np.arange(T + 1, 64 * T)) == far_bucket).all()
    rb = rel_bias.astype(F32) * LOG2E
    tiles = sum(jnp.where(jnp.asarray(bucket == b)[..., None], rb[b], 0.0) for b in range(N_BUCKETS))
    tiles = jnp.swapaxes(tiles, 2, 3).reshape(3, T, N_HEADS * T)
    far = jnp.tile(jnp.repeat(rb[far_bucket], T)[None, :], (1, 2))
    return tiles, far


def _placement_constants():
    H, L = N_HEADS, LANES
    pq = np.zeros((3 * L, H * L), np.float32)
    cq = np.zeros((1, H * L), np.float32)
    pk = np.zeros((3 * L, (H // 2) * L), np.float32)
    ck = np.zeros((1, (H // 2) * L), np.float32)
    for h in range(H):
        p, e = divmod(h, 2)
        for piece in range(3):
            pq[piece * L + h, h * L + 6 + piece] = 1.0
            cq[0, h * L + 3 * e + piece] = 1.0
            pk[piece * L + h, p * L + 3 * e + piece] = -1.0
            ck[0, p * L + 6 + piece] = 1.0
    return jnp.asarray(pq, BF16), jnp.asarray(cq), jnp.asarray(pk, BF16), jnp.asarray(ck)


def _even_weights(w_in, w_uk, w_uv):
    hd, H = HEAD_DIM, N_HEADS
    widths = (H * hd, KV_RANK, H * hd, hd, H, H * hd, H * hd, H * hd, H)
    offs = np.cumsum((0,) + widths)
    seg = lambda s: w_in[:, offs[s]:offs[s + 1]]
    q_a, c_kv, q_idx, k_idx, w_idx, q_b, k_b, v_b, f_in = [seg(s) for s in range(9)]
    scale = hd ** -0.5
    w_big = jnp.concatenate([q_a, q_idx * scale, q_b * (scale * LOG2E), k_b, v_b, c_kv,
                             k_idx, k_idx], axis=1).astype(BF16)
    pad = jnp.zeros((w_in.shape[0], LANES - 2 * H), w_in.dtype)
    w_small = jnp.concatenate([w_idx, f_in, pad], axis=1).astype(BF16)
    wuk_t = jnp.swapaxes(w_uk, 1, 2) * (scale * LOG2E)
    odd = (np.arange(H) % 2 == 1)[:, None, None]
    zk, zv = jnp.zeros_like(wuk_t), jnp.zeros_like(w_uv)
    wuk_pad = jnp.where(odd, jnp.concatenate([zk, wuk_t], axis=1), jnp.concatenate([wuk_t, zk], axis=1))
    wuv_pad = jnp.where(odd, jnp.concatenate([zv, w_uv], axis=2), jnp.concatenate([w_uv, zv], axis=2))
    return w_big, w_small, wuk_pad.astype(BF16), wuv_pad.astype(BF16)


def kernel(x, ln_g, ln_b, rel_bias, ev_w_in, ev_w_uk, ev_w_uv, ev_b_f, ev_w_o,
           od_w_in, od_sgu_g, od_sgu_w, od_sgu_b, od_conv_w, od_w_o,
           ffn_w_up, ffn_conv_w, ffn_w_down):
    bsz, seq, d = x.shape
    depth = ln_g.shape[0]
    n = bsz * seq
    alpha = (2.0 * depth) ** 0.25
    k_sel = min(TOPK_MAX, seq // 4)
    H = N_HEADS
    assert seq % QUERIES == 0 and seq % CHUNK == 0
    tm_ffn = min(1024, seq)

    bias_tiles, bias_far = _bias_tables(rel_bias)
    place = _placement_constants()
    xf = x.reshape(n, d)
    xb = xf
    for layer in range(depth):
        j = layer // 2
        g0, b0 = ln_g[layer, 0][None, :], ln_b[layer, 0][None, :]
        g1, b1 = ln_g[layer, 1][None, :], ln_b[layer, 1][None, :]
        if layer % 2 == 0:
            w_big, w_small, wuk_pad, wuv_pad = _even_weights(ev_w_in[j], ev_w_uk[j], ev_w_uv[j])
            hb, hs = _even_proj(xb, w_big, w_small, tm_ffn)
            hb = hb.reshape(bsz, seq, -1)
            hs_t = jnp.swapaxes(hs.reshape(bsz, seq, -1)[:, :, 0:2 * H], 1, 2)
            cum_f = _cumf(hs_t[:, H:2 * H], ev_b_f[j][:, None])
            f_rows = jnp.pad(jnp.swapaxes(cum_f, 1, 2), ((0, 0), (0, 0), (0, LANES - H)))
            mix = _attention(hb, hs_t[:, 0:H], f_rows, wuk_pad, wuv_pad, bias_tiles, bias_far,
                             place, k_sel)
            xf, xb = _proj_ln(mix.reshape(n, d), ev_w_o[j].astype(BF16), xf, g0, b0, alpha, tm_ffn)
        else:
            sgu_b_tile = jnp.repeat(jnp.swapaxes(od_sgu_b[j], 0, 1), HEAD_DIM, axis=1)
            xf, xb = _odd_mixer(xb, od_w_in[j].astype(BF16), od_sgu_g[j].reshape(1, -1), od_sgu_w[j],
                                sgu_b_tile, od_conv_w[j], od_w_o[j].astype(BF16), xf, g0, b0, alpha,
                                seq, min(1024, seq))
        act = _ffn_up(xb, ffn_w_up[layer].astype(BF16), ffn_conv_w[layer], seq, min(4096, seq), 256,
                      min(512, seq))
        xf, xb = _proj_ln(act, ffn_w_down[layer].astype(BF16), xf, g1, b1, alpha, tm_ffn)
    return xf.reshape(bsz, seq, d)
```

```python
import functools
import math

import numpy as np
import jax
import jax.numpy as jnp
from jax import lax
from jax.experimental import pallas as pl
from jax.experimental.pallas import tpu as pltpu

F32 = jnp.float32
BF16 = jnp.bfloat16

HEAD_DIM = 64
N_HEADS = 8
KV_RANK = 128
TOPK_MAX = 256
N_BUCKETS = 32
T5_MAX_DIST = 128
CHUNK = 128
CONV_W = 3
LN_EPS = 1e-5
MIX_HALF = N_HEADS * HEAD_DIM

LANES = 128
SUBLANES = 8
BF16_ROWS = 16
VMEM_LIMIT = 56 * 1024 * 1024

BLK = 128
QUERIES = 2 * BLK
PV_ROWS = KV_RANK + BF16_ROWS
NEG_BIG = -1e30
MASKED = -(2.0 ** 100)
F32_MAX = float(np.finfo(np.float32).max)
LOG2E = math.log2(math.e)
SEARCH_UNROLL = 3
LOOKAHEAD = 2
N_SLOTS = LOOKAHEAD + 1


def _cparams(sem):
    return pltpu.CompilerParams(dimension_semantics=sem, vmem_limit_bytes=VMEM_LIMIT)


def _dot(a, b):
    return jnp.dot(a, b, preferred_element_type=F32)


def _dot_nt(a, b):
    return lax.dot_general(a, b, (((1,), (1,)), ((), ())), preferred_element_type=F32)


def _layer_norm_rows(y, g, b):
    mu = jnp.mean(y, axis=-1, keepdims=True)
    yc = y - mu
    var = jnp.mean(yc * yc, axis=-1, keepdims=True)
    return yc * lax.rsqrt(var + LN_EPS) * g + b


def _tree(op, x):
    while x.shape[0] > 1:
        half = x.shape[0] // 2
        x = op(x[:half], x[half:])
    return x[0]


def _bf16_pieces(v):
    hi = v.astype(BF16)
    r1 = v - hi.astype(F32)
    mid = r1.astype(BF16)
    lo = (r1 - mid.astype(F32)).astype(BF16)
    return jnp.concatenate([hi, mid, lo], axis=1)


def _even_proj_kernel(x_ref, wb_ref, ws_ref, hb_ref, hs_ref):
    x = x_ref[...].astype(BF16)
    hb_ref[...] = _dot(x, wb_ref[...]).astype(hb_ref.dtype)
    hs_ref[...] = _dot(x, ws_ref[...])


def _even_proj(xb, w_big, w_small, tm):
    n, k = xb.shape
    mb = w_big.shape[1]
    ms = w_small.shape[1]
    return pl.pallas_call(
        _even_proj_kernel,
        grid=(n // tm,),
        in_specs=[pl.BlockSpec((tm, k), lambda i: (i, 0)),
                  pl.BlockSpec((k, mb), lambda i: (0, 0)),
                  pl.BlockSpec((k, ms), lambda i: (0, 0))],
        out_specs=[pl.BlockSpec((tm, mb), lambda i: (i, 0)),
                   pl.BlockSpec((tm, ms), lambda i: (i, 0))],
        out_shape=[jax.ShapeDtypeStruct((n, mb), BF16),
                   jax.ShapeDtypeStruct((n, ms), F32)],
        compiler_params=_cparams(("parallel",)),
        name="even_proj",
    )(xb, w_big, w_small)


def _cumf_kernel(f_ref, b_ref, o_ref):
    seq = f_ref.shape[-1]
    row = lax.broadcasted_iota(jnp.int32, (LANES, LANES), 0)
    col = lax.broadcasted_iota(jnp.int32, (LANES, LANES), 1)
    upper = jnp.where(row <= col, 1.0, 0.0).astype(F32)
    carry = jnp.zeros((f_ref.shape[0], 1), F32)
    for c in range(seq // LANES):
        z = f_ref[:, c * LANES:(c + 1) * LANES] + b_ref[...]
        lf = jnp.minimum(z, 0.0) - jnp.log1p(jnp.exp(-jnp.abs(z)))
        cs = jnp.dot(lf, upper, preferred_element_type=F32,
                     precision=lax.Precision.HIGHEST) + carry
        o_ref[:, c * LANES:(c + 1) * LANES] = cs
        carry = cs[:, LANES - 1:LANES]


def _cumf(f_t, b_f):
    bsz, nh, seq = f_t.shape
    return pl.pallas_call(
        _cumf_kernel,
        grid=(bsz,),
        in_specs=[pl.BlockSpec((None, nh, seq), lambda b: (b, 0, 0)),
                  pl.BlockSpec((nh, 1), lambda b: (0, 0))],
        out_specs=pl.BlockSpec((None, nh, seq), lambda b: (b, 0, 0)),
        out_shape=jax.ShapeDtypeStruct((bsz, nh, seq), F32),
        compiler_params=_cparams(("parallel",)),
        name="fox_cumf",
    )(f_t, b_f)


def _attn_kernel(qa_ref, qi_ref, qb_ref, wt_ref, fq_ref, kb_ref, vb_ref, ckv_ref, kidx_ref, fk_ref,
                 wuk_ref, wuv_ref, bias_ref, bfar_ref, pq_ref, cq_ref, pk_ref, ck_ref,
                 o_ref,
                 score_ref, lhsd_ref, ct_ref, kaug_ref, vt_ref, qsel_ref, qaug_ref, qim_ref,
                 m_ref, acc_ref, work_ref, *, k_sel):
    T = BLK
    Q = QUERIES
    H = N_HEADS
    NP = H // 2
    NT = 2 * NP
    TW = 2 * T
    g = pl.program_id(1)
    i0 = 2 * g
    nblk = ct_ref.shape[0]
    lane = lax.broadcasted_iota(jnp.int32, (T, LANES), 1)
    lo_half = lane < HEAD_DIM
    ones_rows = jnp.ones((BF16_ROWS, T), BF16)
    causal_qq = (lax.broadcasted_iota(jnp.int32, (Q, Q), 0) <= lax.broadcasted_iota(jnp.int32, (Q, Q), 1))

    @pl.when(g == 0)
    def _per_batch():
        def prep(j, c):
            r = pl.ds(pl.multiple_of(j * T, T), T)
            cb = ckv_ref[r, :]
            lhsd_ref[0, r, 0:KV_RANK] = cb
            lhsd_ref[1, r, 0:KV_RANK] = cb
            ct_ref[j, 0:KV_RANK, :] = cb.astype(F32).T.astype(BF16)
            ct_ref[j, KV_RANK:PV_ROWS, :] = ones_rows
            ext = (_dot(_bf16_pieces(fk_ref[r, :] * LOG2E), pk_ref[...]) + ck_ref[...]).astype(BF16)
            for p in range(NP):
                cols = slice(p * LANES, (p + 1) * LANES)
                kaug_ref[p, r, 0:LANES] = kb_ref[r, cols]
                kaug_ref[p, r, LANES:2 * LANES] = ext[:, cols]
                vt_ref[p, j, 0:LANES, :] = vb_ref[r, cols].astype(F32).T.astype(BF16)
                vt_ref[p, j, LANES:PV_ROWS, :] = ones_rows
            return c
        lax.fori_loop(0, nblk, prep, 0)

    w_rows = wt_ref[...] * (H ** -0.5)
    ext_q = (_dot(_bf16_pieces(fq_ref[...] * LOG2E), pq_ref[...]) + cq_ref[...]).astype(BF16)
    eye = jnp.where(lax.broadcasted_iota(jnp.int32, (T, T), 0) == lax.broadcasted_iota(jnp.int32, (T, T), 1),
                    1.0, 0.0).astype(BF16)
    for tt in range(NT):
        hf, p = divmod(tt, NP)
        qrows = slice(hf * T, (hf + 1) * T)
        cols = slice(p * LANES, (p + 1) * LANES)
        qa_pair, qi_pair, qb_pair = qa_ref[qrows, cols], qi_ref[qrows, cols], qb_ref[qrows, cols]
        for e in range(2):
            h = 2 * p + e
            rows = slice(e * T, (e + 1) * T)
            mine = lo_half if e == 0 else jnp.logical_not(lo_half)
            qsel_ref[tt, rows, 0:KV_RANK] = _dot(qa_pair, wuk_ref[h]).astype(BF16)
            qsel_ref[tt, rows, KV_RANK:2 * KV_RANK] = eye
            qaug_ref[tt, rows, 0:LANES] = jnp.where(mine, qb_pair, 0)
            qaug_ref[tt, rows, LANES:2 * LANES] = ext_q[qrows, h * LANES:(h + 1) * LANES]
            qim_ref[tt, rows, :] = jnp.where(mine, qi_pair, 0)

    def reset_state():
        m_ref[...] = jnp.full(m_ref.shape, NEG_BIG, F32)
        acc_ref[...] = jnp.zeros(acc_ref.shape, F32)

    def tile_cols(tt):
        return slice(tt * TW, (tt + 1) * TW)

    def online_update(tt, s, pv_lhs):
        cols = tile_cols(tt)
        m_old = m_ref[:, cols]
        s_max = _tree(jnp.maximum, s.reshape(-1, SUBLANES, TW))
        m_new = jnp.maximum(m_old, jnp.max(s_max, axis=0, keepdims=True))
        alpha = jnp.exp2(m_old - m_new)
        pr = jnp.exp2((s - m_new[0:1, :]).astype(BF16))
        acc_ref[:, cols] = acc_ref[:, cols] * alpha[0:1, :] + _dot(pv_lhs, pr)
        m_ref[:, cols] = m_new

    def key_rows(j, nblocks):
        return pl.ds(pl.multiple_of(j * T, T), nblocks * T)

    def blocks_t(load, j, nblocks):
        return load(j) if nblocks == 1 else jnp.concatenate([load(j + b) for b in range(nblocks)], axis=1)

    def slot_tile(slot, tt, nblocks):
        return (slot * NT + tt, slice(0, nblocks * T), slice(None))

    def run_staged(steps):
        for k in range(min(LOOKAHEAD, len(steps))):
            steps[k][0](k % N_SLOTS)
        for k, (_, absorb_k) in enumerate(steps):
            if k + LOOKAHEAD < len(steps):
                steps[k + LOOKAHEAD][0]((k + LOOKAHEAD) % N_SLOTS)
            absorb_k(k % N_SLOTS)

    def staged_steps(j0, n_steps, logits, absorb):
        run_staged([(functools.partial(logits, j0 + 2 * k, 2), functools.partial(absorb, j0 + 2 * k, 2))
                    for k in range(n_steps)])

    def for_far_blocks(n_far, logits, absorb):
        n_oct = lax.shift_right_logical(n_far, 3)

        def oct_body(jj, c):
            staged_steps(8 * jj, 4, logits, absorb)
            return c
        lax.fori_loop(0, n_oct, oct_body, 0)
        return 8 * n_oct, lax.shift_right_logical(n_far & 6, 1)

    def finish_blocks(leftover, far_step, before_step, diag_step, first_diag_step):
        rest_j, n_rest = leftover
        for n in range(4):
            @pl.when(jnp.logical_and(g >= 1, n_rest == n))
            def _finish(n=n):
                run_staged([far_step(rest_j + 2 * k) for k in range(n)] + [before_step, diag_step])

        @pl.when(g == 0)
        def _first():
            run_staged([first_diag_step])

    def idx_scores(j, nblocks):
        k_blk = kidx_ref[key_rows(j, nblocks), :]
        acc = [jnp.zeros((nblocks * T, T), F32), jnp.zeros((nblocks * T, T), F32)]
        for tt in range(NT):
            hf, p = divmod(tt, NP)
            s = _dot_nt(k_blk, qim_ref[tt])
            for e in range(2):
                h = 2 * p + e
                w_h = w_rows[h:h + 1, hf * T:(hf + 1) * T]
                acc[hf] = acc[hf] + w_h * jnp.maximum(s[:, e * T:(e + 1) * T], 0.0)
        return jnp.concatenate(acc, axis=1)

    def fox_logits(j, nblocks, slot, causal=False):
        for tt in range(NT):
            hf, p = divmod(tt, NP)
            s = _dot_nt(kaug_ref[p, key_rows(j, nblocks), :], qaug_ref[tt])
            if causal:
                ok = causal_qq[:, hf * T:(hf + 1) * T]
                s = jnp.where(jnp.concatenate([ok, ok], axis=1), s, MASKED)
            work_ref[slot_tile(slot, tt, nblocks)] = s

    def fox_absorb(j, nblocks, slot):
        for tt in range(NT):
            p = tt % NP
            online_update(tt, work_ref[slot_tile(slot, tt, nblocks)],
                          blocks_t(lambda jb: vt_ref[p, jb], j, nblocks))

    def idx_fox_logits(j, nblocks, slot):
        score_ref[pl.ds(j, nblocks)] = idx_scores(j, nblocks).reshape(nblocks, T, Q)
        fox_logits(j, nblocks, slot)

    reset_state()
    leftover = for_far_blocks(jnp.maximum(i0 - 2, 0), idx_fox_logits, fox_absorb)

    def diag_logits(slot):
        score_ref[pl.ds(i0, 2)] = jnp.where(causal_qq, idx_scores(i0, 2), -jnp.inf).reshape(2, T, Q)
        fox_logits(i0, 2, slot, causal=True)

    def fox_far_step(j):
        return (lambda slot: idx_fox_logits(j, 2, slot), lambda slot: fox_absorb(j, 2, slot))

    fox_diag_step = (diag_logits, lambda slot: fox_absorb(i0, 2, slot))
    finish_blocks(leftover, fox_far_step, fox_far_step(i0 - 2), fox_diag_step, fox_diag_step)

    row_lo = lax.broadcasted_iota(jnp.int32, (LANES, T), 0) < HEAD_DIM
    for tt in range(NT):
        hf, p = divmod(tt, NP)
        c0 = slice(tt * TW, tt * TW + T)
        c1 = slice(tt * TW + T, (tt + 1) * TW)
        o_t = jnp.where(row_lo, acc_ref[0:LANES, c0] / acc_ref[LANES:LANES + 1, c0],
                        acc_ref[0:LANES, c1] / acc_ref[LANES:LANES + 1, c1])
        o_ref[hf * T:(hf + 1) * T, MIX_HALF + p * LANES:MIX_HALF + (p + 1) * LANES] = (
            o_t.T.astype(o_ref.dtype))

    kf = float(k_sel)
    t_pos = (g * Q + lax.broadcasted_iota(jnp.int32, (SUBLANES, Q), 1)).astype(F32)
    searching = (t_pos + 1.0) > kf
    G2 = 2 * T // SUBLANES
    n_pair = g + 1
    key_in_pair = (lax.broadcasted_iota(jnp.int32, (G2, SUBLANES, Q), 0) * SUBLANES
                   + lax.broadcasted_iota(jnp.int32, (G2, SUBLANES, Q), 1))

    def rep(v):
        return jnp.broadcast_to(v, (SUBLANES, Q))

    def score_pair(jj):
        return score_ref[pl.ds(2 * jj, 2)].reshape(G2, SUBLANES, Q)

    def count(*pred_fns, load=score_pair):
        def body(jj, cs):
            s = load(jj)
            return tuple(c + _tree(jnp.add, jnp.where(f(s, jj), 1.0, 0.0)) for c, f in zip(cs, pred_fns))
        zero = jnp.zeros((SUBLANES, Q), F32)
        cs = lax.fori_loop(0, n_pair, body, (zero,) * len(pred_fns))
        return tuple(rep(jnp.sum(c, axis=0, keepdims=True)) for c in cs)

    def count_ge(thr):
        return count(lambda s, jj: s >= thr[None])[0]

    def select_threshold():
        def first_body(jj, carry):
            mn, mx, ge0, gt0 = carry
            s = score_pair(jj)
            return (jnp.minimum(mn, _tree(jnp.minimum, jnp.where(s == -jnp.inf, jnp.inf, s))),
                    jnp.maximum(mx, _tree(jnp.maximum, s)),
                    ge0 + _tree(jnp.add, jnp.where(s >= 0.0, 1.0, 0.0)),
                    gt0 + _tree(jnp.add, jnp.where(s > 0.0, 1.0, 0.0)))

        zero = jnp.zeros((SUBLANES, Q), F32)
        mn, mx, c_ge0, c_gt0 = lax.fori_loop(0, n_pair, first_body, (zero + jnp.inf, zero - jnp.inf, zero, zero))
        mn = rep(jnp.min(mn, axis=0, keepdims=True))
        mx = rep(jnp.max(mx, axis=0, keepdims=True))
        c_ge0 = rep(jnp.sum(c_ge0, axis=0, keepdims=True))
        c_gt0 = rep(jnp.sum(c_gt0, axis=0, keepdims=True))
        above_mx = mx + jnp.maximum(jnp.abs(mx) * 2.0 ** -20, 1e-30)
        thr_ge0 = c_ge0 >= kf
        thr_gt0 = c_gt0 >= kf
        lo0 = jnp.where(thr_ge0, 0.0, mn)
        cnt_lo0 = jnp.where(thr_ge0, c_ge0, t_pos + 1.0)
        hi0 = jnp.where(thr_gt0, above_mx, 0.0)
        cnt_hi0 = jnp.where(thr_gt0, 0.0, jnp.where(thr_ge0, c_gt0, c_ge0))
        undecided = thr_gt0 | jnp.logical_not(thr_ge0)
        active0 = jnp.where(searching & undecided & (cnt_lo0 > kf), 1.0, 0.0)

        def cond(st):
            return jnp.logical_and(st[5] > 0.0, st[6] < 2200)

        def bisect(st):
            lo, hi, cnt_lo, cnt_hi, active = st
            mid = 0.5 * lo + 0.5 * hi
            adjacent = (mid <= lo) | (mid >= hi)
            c = count_ge(mid)
            act = active > 0.0
            go_lo = act & jnp.logical_not(adjacent) & (c >= kf)
            go_hi = act & jnp.logical_not(adjacent) & (c < kf)
            lo = jnp.where(go_lo, mid, lo)
            cnt_lo = jnp.where(go_lo, c, cnt_lo)
            hi = jnp.where(go_hi, mid, hi)
            cnt_hi = jnp.where(go_hi, c, cnt_hi)
            active = jnp.where(act & jnp.logical_not(adjacent) & (cnt_lo > kf), 1.0, 0.0)
            return lo, hi, cnt_lo, cnt_hi, active

        def body(st):
            inner = st[:5]
            for _ in range(SEARCH_UNROLL):
                inner = bisect(inner)
            return inner + (jnp.max(inner[4]), st[6] + SEARCH_UNROLL)

        lo, hi, cnt_lo, cnt_hi, _, _, _ = lax.while_loop(
            cond, body, (lo0, hi0, cnt_lo0, cnt_hi0, active0, jnp.max(active0), 0))
        thr = jnp.where(searching, lo, -F32_MAX)

        tied = searching & (cnt_lo > kf)
        need = kf - cnt_hi

        @pl.when(jnp.max(jnp.where(tied, 1.0, 0.0)) > 0.0)
        def _break_ties():
            def tie_pair(jj):
                return work_ref[jj].reshape(G2, SUBLANES, Q)

            def mark_body(jj, c):
                key_index = (2 * jj * T + key_in_pair).astype(F32)
                work_ref[jj] = (
                    jnp.where(score_pair(jj) == thr[None], key_index, jnp.inf).reshape(2 * T, Q))
                return c
            lax.fori_loop(0, n_pair, mark_body, 0)

            def count_tied_upto(jcut):
                return count(lambda e, jj: e <= jcut[None], load=tie_pair)[0]

            def bs_body(_, st):
                jl, jh = st
                jm = jnp.floor(0.5 * (jl + jh))
                ok = count_tied_upto(jm) >= need
                return jnp.where(ok, jl, jm), jnp.where(ok, jm, jh)

            n_keys = (2 * n_pair * T).astype(F32)
            jl0 = jnp.full((SUBLANES, Q), -1.0, F32)
            jh0 = jnp.zeros((SUBLANES, Q), F32) + (n_keys - 1.0)
            n_steps = int(math.ceil(math.log2(nblk * T))) + 1
            _, jcut = lax.fori_loop(0, n_steps, bs_body, (jl0, jh0))

            def drop_body(jj, c):
                e = tie_pair(jj)
                drop = tied[None] & (e > jcut[None]) & (e < jnp.inf)
                score_ref[pl.ds(2 * jj, 2)] = jnp.where(drop, -jnp.inf, score_pair(jj)).reshape(2, T, Q)
                return c
            lax.fori_loop(0, n_pair, drop_body, 0)

        return thr

    thr = lax.cond((i0 + 2) * T > k_sel, select_threshold,
                   lambda: jnp.full((SUBLANES, Q), -F32_MAX, F32))

    def mask_body(jj, c):
        add = jnp.where(score_pair(jj) >= thr[None], 0.0, MASKED).reshape(2 * T, Q).astype(BF16)
        for hf in range(2):
            lhsd_ref[hf, key_rows(2 * jj, 2), KV_RANK:2 * KV_RANK] = add[:, hf * T:(hf + 1) * T]
        return c

    lax.fori_loop(0, n_pair, mask_body, 0)

    def dsa_logits(j, nblocks, slot, bias=None):
        for tt in range(NT):
            hf, p = divmod(tt, NP)
            s = _dot_nt(lhsd_ref[hf, key_rows(j, nblocks), :], qsel_ref[tt])
            if bias is not None:
                s = s + bias(hf, slice(p * TW, (p + 1) * TW))
            work_ref[slot_tile(slot, tt, nblocks)] = s

    def dsa_absorb(j, nblocks, slot):
        pv_lhs = blocks_t(lambda jb: ct_ref[jb], j, nblocks)
        for tt in range(NT):
            online_update(tt, work_ref[slot_tile(slot, tt, nblocks)], pv_lhs)

    def bias_before(hf, cols):
        return jnp.concatenate([bias_ref[2, :, cols], bias_ref[1 + hf, :, cols]], axis=0)

    def bias_diag(hf, cols):
        first = bias_ref[hf, :, cols]
        return jnp.concatenate([first, bias_ref[0, :, cols]], axis=0)

    reset_state()
    leftover = for_far_blocks(jnp.maximum(i0 - 2, 0), dsa_logits, dsa_absorb)

    def dsa_far_step(j):
        return (lambda slot: dsa_logits(j, 2, slot), lambda slot: dsa_absorb(j, 2, slot))

    def absorb_first_biased(j):
        def absorb(slot):
            m_ref[...] = m_ref[...] + bfar_ref[...]
            dsa_absorb(j, 2, slot)
        return absorb

    diag_step = lambda absorb: (lambda slot: dsa_logits(i0, 2, slot, bias_diag), absorb)
    before_step = (lambda slot: dsa_logits(i0 - 2, 2, slot, bias_before), absorb_first_biased(i0 - 2))
    finish_blocks(leftover, dsa_far_step, before_step,
                  diag_step(lambda slot: dsa_absorb(i0, 2, slot)), diag_step(absorb_first_biased(0)))

    for tt in range(NT):
        hf, p = divmod(tt, NP)
        o_pair = jnp.zeros((T, LANES), F32)
        for e in range(2):
            cols = slice(tt * TW + e * T, tt * TW + (e + 1) * T)
            o_t = acc_ref[0:KV_RANK, cols] / acc_ref[KV_RANK:KV_RANK + 1, cols]
            o_pair = o_pair + _dot(o_t.T.astype(BF16), wuv_ref[2 * p + e])
        o_ref[hf * T:(hf + 1) * T, p * LANES:(p + 1) * LANES] = o_pair.astype(o_ref.dtype)


_EV_QA, _EV_QI, _EV_QB, _EV_KB, _EV_VB = 0, 1, 2, 3, 4
_EV_CKV, _EV_KIDX = 20, 21


def _attention(hb, w_t, f_rows, wuk_pad, wuv_pad, bias_tiles, bias_far, place, k_sel):
    bsz, seq, _ = hb.shape
    T, Q, H = BLK, QUERIES, N_HEADS
    nblk = seq // T
    lanes = 2 * H * T
    pq, cq, pk, ck = place
    wide = lambda c: pl.BlockSpec((None, Q, MIX_HALF), lambda b, i, c=c: (b, i, c))
    once = dict(pipeline_mode=pl.Buffered(1))
    full_wide = lambda c: pl.BlockSpec((None, seq, MIX_HALF), lambda b, i, c=c: (b, 0, c), **once)
    full_narrow = lambda c: pl.BlockSpec((None, seq, LANES), lambda b, i, c=c: (b, 0, c), **once)
    const = lambda a: pl.BlockSpec(a.shape, lambda b, i, nd=a.ndim: (0,) * nd, **once)
    return pl.pallas_call(
        functools.partial(_attn_kernel, k_sel=k_sel),
        grid=(bsz, seq // Q),
        in_specs=[wide(_EV_QA), wide(_EV_QI), wide(_EV_QB),
                  pl.BlockSpec((None, H, Q), lambda b, i: (b, 0, i)),
                  pl.BlockSpec((None, Q, LANES), lambda b, i: (b, i, 0)),
                  full_wide(_EV_KB), full_wide(_EV_VB),
                  full_narrow(_EV_CKV), full_narrow(_EV_KIDX),
                  pl.BlockSpec((None, seq, LANES), lambda b, i: (b, 0, 0), **once),
                  const(wuk_pad), const(wuv_pad), const(bias_tiles), const(bias_far),
                  const(pq), const(cq), const(pk), const(ck)],
        out_specs=pl.BlockSpec((None, Q, 2 * MIX_HALF), lambda b, i: (b, i, 0)),
        out_shape=jax.ShapeDtypeStruct((bsz, seq, 2 * MIX_HALF), BF16),
        scratch_shapes=[pltpu.VMEM((nblk, T, Q), F32),
                        pltpu.VMEM((2, seq, 2 * KV_RANK), BF16),
                        pltpu.VMEM((nblk, PV_ROWS, T), BF16),
                        pltpu.VMEM((H // 2, seq, 2 * LANES), BF16),
                        pltpu.VMEM((H // 2, nblk, PV_ROWS, T), BF16),
                        pltpu.VMEM((H, 2 * T, 2 * KV_RANK), BF16),
                        pltpu.VMEM((H, 2 * T, 2 * LANES), BF16),
                        pltpu.VMEM((H, 2 * T, LANES), BF16),
                        pltpu.VMEM((SUBLANES, lanes), F32),
                        pltpu.VMEM((PV_ROWS, lanes), F32),
                        pltpu.VMEM((max(N_SLOTS * H, nblk // 2), 2 * T, 2 * T), F32)],
        compiler_params=_cparams(("parallel", "arbitrary")),
        name="dsa_fox_attention",
    )(hb, hb, hb, w_t, f_rows, hb, hb, hb, hb, f_rows, wuk_pad, wuv_pad, bias_tiles, bias_far,
      pq, cq, pk, ck)


def _proj_ln_kernel(a_ref, w_ref, x_ref, g_ref, b_ref, xo_ref, xb_ref, *, alpha, sub):
    for c in range(a_ref.shape[0] // sub):
        rows = slice(c * sub, (c + 1) * sub)
        y = alpha * x_ref[rows, :] + _dot(a_ref[rows, :], w_ref[...])
        out = _layer_norm_rows(y, g_ref[...], b_ref[...])
        xo_ref[rows, :] = out
        xb_ref[rows, :] = out.astype(BF16)


def _proj_ln(a, w, x, g, b, alpha, tm):
    n, k = a.shape
    d = w.shape[1]
    return pl.pallas_call(
        functools.partial(_proj_ln_kernel, alpha=alpha, sub=min(128, tm)),
        grid=(n // tm,),
        in_specs=[pl.BlockSpec((tm, k), lambda i: (i, 0)),
                  pl.BlockSpec((k, d), lambda i: (0, 0)),
                  pl.BlockSpec((tm, d), lambda i: (i, 0)),
                  pl.BlockSpec((1, d), lambda i: (0, 0)),
                  pl.BlockSpec((1, d), lambda i: (0, 0))],
        out_specs=[pl.BlockSpec((tm, d), lambda i: (i, 0)),
                   pl.BlockSpec((tm, d), lambda i: (i, 0))],
        out_shape=[jax.ShapeDtypeStruct((n, d), F32),
                   jax.ShapeDtypeStruct((n, d), BF16)],
        compiler_params=_cparams(("parallel",)),
        name="proj_residual_ln",
    )(a, w, x, g, b)


def _causal_conv3(h, tail, cw):
    w0, w1, w2 = cw[0:1, :], cw[1:2, :], cw[2:3, :]
    body = w0 * pltpu.roll(h, 2, axis=0) + w1 * pltpu.roll(h, 1, axis=0) + w2 * h
    head = h[0:SUBLANES, :]
    r = lax.broadcasted_iota(jnp.int32, head.shape, 0)
    hm1 = jnp.where(r < 1, pltpu.roll(tail, 1, axis=0), pltpu.roll(head, 1, axis=0))
    hm2 = jnp.where(r < 2, pltpu.roll(tail, 2, axis=0), pltpu.roll(head, 2, axis=0))
    out_head = w0 * hm2 + w1 * hm1 + w2 * head
    return jnp.concatenate([out_head, body[SUBLANES:, :]], axis=0)


def _ffn_up_kernel(x_ref, wg_ref, wv_ref, cg_ref, cv_ref, o_ref, tg_ref, tv_ref, *h_refs,
                   tiles_per_seq, sub):
    n_chunks = x_ref.shape[0] // sub
    hg_refs, hv_refs = h_refs[:n_chunks], h_refs[n_chunks:]
    first = (pl.program_id(1) % tiles_per_seq) == 0
    body_rows = slice(SUBLANES, SUBLANES + sub)

    def matmuls(c):
        x = x_ref[c * sub:(c + 1) * sub, :]
        hg_refs[c][body_rows, :] = _dot(x, wg_ref[...])
        hv_refs[c][body_rows, :] = _dot(x, wv_ref[...])

    def conv(h_ref, cw):
        return (cw[0:1, :] * h_ref[SUBLANES - 2:SUBLANES - 2 + sub, :]
                + cw[1:2, :] * h_ref[SUBLANES - 1:SUBLANES - 1 + sub, :]
                + cw[2:3, :] * h_ref[body_rows, :])

    def gate(c):
        if c == 0:
            hg_refs[0][0:SUBLANES, :] = jnp.where(first, 0.0, tg_ref[...])
            hv_refs[0][0:SUBLANES, :] = jnp.where(first, 0.0, tv_ref[...])
        else:
            hg_refs[c][0:SUBLANES, :] = hg_refs[c - 1][sub:sub + SUBLANES, :]
            hv_refs[c][0:SUBLANES, :] = hv_refs[c - 1][sub:sub + SUBLANES, :]
        g = conv(hg_refs[c], cg_ref[...])
        v = conv(hv_refs[c], cv_ref[...])
        half_g = 0.5 * g
        o_ref[c * sub:(c + 1) * sub, :] = ((half_g + half_g * jnp.tanh(half_g)) * v).astype(o_ref.dtype)

    matmuls(0)
    for c in range(n_chunks):
        if c + 1 < n_chunks:
            matmuls(c + 1)
        gate(c)
    tg_ref[...] = hg_refs[n_chunks - 1][sub:sub + SUBLANES, :]
    tv_ref[...] = hv_refs[n_chunks - 1][sub:sub + SUBLANES, :]


def _ffn_up(xb, w_up, conv_w, seq, tm, tn, sub):
    n, k = xb.shape
    dff = w_up.shape[1] // 2
    ncol = dff // tn
    return pl.pallas_call(
        functools.partial(_ffn_up_kernel, tiles_per_seq=seq // tm, sub=sub),
        grid=(ncol, n // tm),
        in_specs=[pl.BlockSpec((tm, k), lambda j, i: (i, 0)),
                  pl.BlockSpec((k, tn), lambda j, i: (0, j)),
                  pl.BlockSpec((k, tn), lambda j, i, ncol=ncol: (0, j + ncol)),
                  pl.BlockSpec((CONV_W, tn), lambda j, i: (0, j)),
                  pl.BlockSpec((CONV_W, tn), lambda j, i, ncol=ncol: (0, j + ncol))],
        out_specs=pl.BlockSpec((tm, tn), lambda j, i: (i, j)),
        out_shape=jax.ShapeDtypeStruct((n, dff), BF16),
        scratch_shapes=([pltpu.VMEM((SUBLANES, tn), F32)] * 2
                        + [pltpu.VMEM((sub + SUBLANES, tn), F32)] * (2 * (tm // sub))),
        compiler_params=_cparams(("arbitrary", "arbitrary")),
        name="ffn_up_conv_gate",
    )(xb, w_up, w_up, conv_w, conv_w)


def _gelu_tanh(x):
    return 0.5 * x * (1.0 + jnp.tanh(math.sqrt(2.0 / math.pi) * (x + 0.044715 * (x * x * x))))


def _group_mean(v, avg):
    hi = v.astype(BF16)
    lo = (v - hi.astype(F32)).astype(BF16)
    return _dot(hi, avg) + _dot(lo, avg)


def _odd_mixer_kernel(x_ref, w_ref, sg_ref, sw_ref, sb_ref, cw_ref, wo_ref, xf_ref, g_ref, b_ref,
                      xo_ref, xb_ref, o_ref, tail_ref, *, tiles_per_seq, alpha):
    tm = x_ref.shape[0]
    W = MIX_HALF
    first = (pl.program_id(0) % tiles_per_seq) == 0
    x = x_ref[...]
    lane = lax.broadcasted_iota(jnp.int32, (CHUNK, LANES), 1)
    lo_half = lane < HEAD_DIM
    r = lax.broadcasted_iota(jnp.int32, (W, W), 0) // HEAD_DIM
    c = lax.broadcasted_iota(jnp.int32, (W, W), 1) // HEAD_DIM
    avg = jnp.where(r == c, 1.0 / HEAD_DIM, 0.0).astype(BF16)
    tri = (lax.broadcasted_iota(jnp.int32, (CHUNK, CHUNK), 1)
           <= lax.broadcasted_iota(jnp.int32, (CHUNK, CHUNK), 0))

    u = _gelu_tanh(_dot(x, w_ref[:, 0:W]))
    v = _gelu_tanh(_dot(x, w_ref[:, W:2 * W]))
    vc = v - _group_mean(v, avg)
    var = _group_mean(vc * vc, avg)
    vn = (vc * lax.rsqrt(var + LN_EPS) * sg_ref[...]).astype(BF16)
    for n in range(tm // CHUNK):
        rows = slice(n * CHUNK, (n + 1) * CHUNK)
        for p in range(N_HEADS // 2):
            cols = slice(p * LANES, (p + 1) * LANES)
            vp = vn[rows, cols]
            w_e = jnp.where(tri, sw_ref[2 * p], 0.0).astype(BF16)
            w_o = jnp.where(tri, sw_ref[2 * p + 1], 0.0).astype(BF16)
            mix = (_dot(w_e, jnp.where(lo_half, vp, 0)) + _dot(w_o, jnp.where(lo_half, 0, vp))
                   + sb_ref[:, cols])
            o_ref[rows, cols] = (u[rows, cols] * mix).astype(o_ref.dtype)

    g_b = _dot(x, w_ref[:, 2 * W:3 * W])
    y = _dot(x, w_ref[:, 3 * W:4 * W]) * _dot(x, w_ref[:, 4 * W:5 * W])
    conv = _causal_conv3(y, jnp.where(first, 0.0, tail_ref[...]), cw_ref[...])
    tail_ref[...] = y[tm - SUBLANES:tm, :]
    o_ref[:, W:2 * W] = (g_b * conv).astype(o_ref.dtype)

    out = _layer_norm_rows(alpha * xf_ref[...] + _dot(o_ref[...], wo_ref[...]), g_ref[...], b_ref[...])
    xo_ref[...] = out
    xb_ref[...] = out.astype(BF16)


def _odd_mixer(xb, w_in, sgu_g, sgu_w, sgu_b_tile, conv_w, w_o, xf, g, b, alpha, seq, tm):
    n, k = xb.shape
    W = MIX_HALF
    d = w_o.shape[1]
    const2 = lambda shape: pl.BlockSpec(shape, lambda i: (0, 0))
    row_tile = lambda width: pl.BlockSpec((tm, width), lambda i: (i, 0))
    return pl.pallas_call(
        functools.partial(_odd_mixer_kernel, tiles_per_seq=seq // tm, alpha=alpha),
        grid=(n // tm,),
        in_specs=[row_tile(k), const2((k, 5 * W)), const2((1, W)),
                  pl.BlockSpec((N_HEADS, CHUNK, CHUNK), lambda i: (0, 0, 0)),
                  const2((CHUNK, W)), const2((CONV_W, W)),
                  const2((2 * W, d)), row_tile(d), const2((1, d)), const2((1, d))],
        out_specs=[row_tile(d), row_tile(d)],
        out_shape=[jax.ShapeDtypeStruct((n, d), F32), jax.ShapeDtypeStruct((n, d), BF16)],
        scratch_shapes=[pltpu.VMEM((tm, 2 * W), BF16),
                        pltpu.VMEM((SUBLANES, W), F32)],
        compiler_params=_cparams(("arbitrary",)),
        name="odd_mixer",
    )(xb, w_in, sgu_g, sgu_w, sgu_b_tile, conv_w, w_o, xf, g, b)


def _t5_bucket_np(dist):
    max_exact = N_BUCKETS // 2
    n = np.maximum(dist, 0)
    nf = np.maximum(n, 1).astype(np.float32)
    large = max_exact + (np.log(nf / max_exact) / np.float32(math.log(T5_MAX_DIST / max_exact))
                         * (N_BUCKETS - max_exact)).astype(np.int32)
    large = np.minimum(large, N_BUCKETS - 1)
    return np.where(n < max_exact, n, large)


def _bias_tables(rel_bias):
    T = BLK
    s = np.arange(T)[:, None]
    t = np.arange(T)[None, :]
    bucket = _t5_bucket_np(np.stack([t + d * T - s for d in range(3)]))
    far_bucket = N_BUCKETS - 1
    assert (bucket[2] == far_bucket).all()
    assert (_t5_bucket_np(np.arange(T + 1, 64 * T)) == far_bucket).all()
    rb = rel_bias.astype(F32) * LOG2E
    tiles = sum(jnp.where(jnp.asarray(bucket == b)[..., None], rb[b], 0.0) for b in range(N_BUCKETS))
    tiles = jnp.swapaxes(tiles, 2, 3).reshape(3, T, N_HEADS * T)
    far = jnp.tile(jnp.repeat(rb[far_bucket], T)[None, :], (1, 2))
    return tiles, far


def _placement_constants():
    H, L = N_HEADS, LANES
    pq = np.zeros((3 * L, H * L), np.float32)
    cq = np.zeros((1, H * L), np.float32)
    pk = np.zeros((3 * L, (H // 2) * L), np.float32)
    ck = np.zeros((1, (H // 2) * L), np.float32)
    for h in range(H):
        p, e = divmod(h, 2)
        for piece in range(3):
            pq[piece * L + h, h * L + 6 + piece] = 1.0
            cq[0, h * L + 3 * e + piece] = 1.0
            pk[piece * L + h, p * L + 3 * e + piece] = -1.0
            ck[0, p * L + 6 + piece] = 1.0
    return jnp.asarray(pq, BF16), jnp.asarray(cq), jnp.asarray(pk, BF16), jnp.asarray(ck)


def _even_weights(w_in, w_uk, w_uv):
    hd, H = HEAD_DIM, N_HEADS
    widths = (H * hd, KV_RANK, H * hd, hd, H, H * hd, H * hd, H * hd, H)
    offs = np.cumsum((0,) + widths)
    seg = lambda s: w_in[:, offs[s]:offs[s + 1]]
    q_a, c_kv, q_idx, k_idx, w_idx, q_b, k_b, v_b, f_in = [seg(s) for s in range(9)]
    scale = hd ** -0.5
    w_big = jnp.concatenate([q_a, q_idx * scale, q_b * (scale * LOG2E), k_b, v_b, c_kv,
                             k_idx, k_idx], axis=1).astype(BF16)
    pad = jnp.zeros((w_in.shape[0], LANES - 2 * H), w_in.dtype)
    w_small = jnp.concatenate([w_idx, f_in, pad], axis=1).astype(BF16)
    wuk_t = jnp.swapaxes(w_uk, 1, 2) * (scale * LOG2E)
    odd = (np.arange(H) % 2 == 1)[:, None, None]
    zk, zv = jnp.zeros_like(wuk_t), jnp.zeros_like(w_uv)
    wuk_pad = jnp.where(odd, jnp.concatenate([zk, wuk_t], axis=1), jnp.concatenate([wuk_t, zk], axis=1))
    wuv_pad = jnp.where(odd, jnp.concatenate([zv, w_uv], axis=2), jnp.concatenate([w_uv, zv], axis=2))
    return w_big, w_small, wuk_pad.astype(BF16), wuv_pad.astype(BF16)


def kernel(x, ln_g, ln_b, rel_bias, ev_w_in, ev_w_uk, ev_w_uv, ev_b_f, ev_w_o,
           od_w_in, od_sgu_g, od_sgu_w, od_sgu_b, od_conv_w, od_w_o,
           ffn_w_up, ffn_conv_w, ffn_w_down):
    bsz, seq, d = x.shape
    depth = ln_g.shape[0]
    n = bsz * seq
    alpha = (2.0 * depth) ** 0.25
    k_sel = min(TOPK_MAX, seq // 4)
    H = N_HEADS
    assert seq % QUERIES == 0 and seq % CHUNK == 0
    tm_ffn = min(1024, seq)

    bias_tiles, bias_far = _bias_tables(rel_bias)
    place = _placement_constants()
    xf = x.reshape(n, d)
    xb = xf
    for layer in range(depth):
        j = layer // 2
        g0, b0 = ln_g[layer, 0][None, :], ln_b[layer, 0][None, :]
        g1, b1 = ln_g[layer, 1][None, :], ln_b[layer, 1][None, :]
        if layer % 2 == 0:
            w_big, w_small, wuk_pad, wuv_pad = _even_weights(ev_w_in[j], ev_w_uk[j], ev_w_uv[j])
            hb, hs = _even_proj(xb, w_big, w_small, tm_ffn)
            hb = hb.reshape(bsz, seq, -1)
            hs_t = jnp.swapaxes(hs.reshape(bsz, seq, -1)[:, :, 0:2 * H], 1, 2)
            cum_f = _cumf(hs_t[:, H:2 * H], ev_b_f[j][:, None])
            f_rows = jnp.pad(jnp.swapaxes(cum_f, 1, 2), ((0, 0), (0, 0), (0, LANES - H)))
            mix = _attention(hb, hs_t[:, 0:H], f_rows, wuk_pad, wuv_pad, bias_tiles, bias_far,
                             place, k_sel)
            xf, xb = _proj_ln(mix.reshape(n, d), ev_w_o[j].astype(BF16), xf, g0, b0, alpha, tm_ffn)
        else:
            sgu_b_tile = jnp.repeat(jnp.swapaxes(od_sgu_b[j], 0, 1), HEAD_DIM, axis=1)
            xf, xb = _odd_mixer(xb, od_w_in[j].astype(BF16), od_sgu_g[j].reshape(1, -1), od_sgu_w[j],
                                sgu_b_tile, od_conv_w[j], od_w_o[j].astype(BF16), xf, g0, b0, alpha,
                                seq, min(1024, seq))
        act = _ffn_up(xb, ffn_w_up[layer].astype(BF16), ffn_conv_w[layer], seq, min(4096, seq), 256,
                      min(512, seq))
        xf, xb = _proj_ln(act, ffn_w_down[layer].astype(BF16), xf, g1, b1, alpha, tm_ffn)
    return xf.reshape(bsz, seq, d)
```

```python
import functools
import math

import numpy as np
import jax
import jax.numpy as jnp
from jax import lax
from jax.experimental import pallas as pl
from jax.experimental.pallas import tpu as pltpu

F32 = jnp.float32
BF16 = jnp.bfloat16

HEAD_DIM = 64
N_HEADS = 8
KV_RANK = 128
TOPK_MAX = 256
N_BUCKETS = 32
T5_MAX_DIST = 128
CHUNK = 128
CONV_W = 3
LN_EPS = 1e-5
MIX_HALF = N_HEADS * HEAD_DIM

LANES = 128
SUBLANES = 8
BF16_ROWS = 16
VMEM_LIMIT = 56 * 1024 * 1024

BLK = 128
QUERIES = 2 * BLK
PV_ROWS = KV_RANK + BF16_ROWS
NEG_BIG = -1e30
MASKED = -(2.0 ** 100)
F32_MAX = float(np.finfo(np.float32).max)
LOG2E = math.log2(math.e)
SEARCH_UNROLL = 3
LOOKAHEAD = 2
N_SLOTS = LOOKAHEAD + 1


def _cparams(sem, fuse_inputs=None):
    return pltpu.CompilerParams(dimension_semantics=sem, vmem_limit_bytes=VMEM_LIMIT,
                                allow_input_fusion=fuse_inputs)


def _dot(a, b):
    return jnp.dot(a, b, preferred_element_type=F32)


def _dot_nt(a, b):
    return lax.dot_general(a, b, (((1,), (1,)), ((), ())), preferred_element_type=F32)


def _layer_norm_rows(y, g, b):
    mu = jnp.mean(y, axis=-1, keepdims=True)
    yc = y - mu
    var = jnp.mean(yc * yc, axis=-1, keepdims=True)
    return yc * lax.rsqrt(var + LN_EPS) * g + b


def _tree(op, x):
    while x.shape[0] > 1:
        half = x.shape[0] // 2
        x = op(x[:half], x[half:])
    return x[0]


def _bf16_pieces(v):
    hi = v.astype(BF16)
    r1 = v - hi.astype(F32)
    mid = r1.astype(BF16)
    lo = (r1 - mid.astype(F32)).astype(BF16)
    return jnp.concatenate([hi, mid, lo], axis=1)


def _even_proj_kernel(x_ref, wb_ref, ws_ref, hb_ref, hs_ref):
    x = x_ref[...].astype(BF16)
    hb_ref[...] = _dot(x, wb_ref[...]).astype(hb_ref.dtype)
    hs_ref[...] = _dot(x, ws_ref[...])


def _even_proj(xb, w_big, w_small, tm):
    n, k = xb.shape
    mb = w_big.shape[1]
    ms = w_small.shape[1]
    return pl.pallas_call(
        _even_proj_kernel,
        grid=(n // tm,),
        in_specs=[pl.BlockSpec((tm, k), lambda i: (i, 0)),
                  pl.BlockSpec((k, mb), lambda i: (0, 0)),
                  pl.BlockSpec((k, ms), lambda i: (0, 0))],
        out_specs=[pl.BlockSpec((tm, mb), lambda i: (i, 0)),
                   pl.BlockSpec((tm, ms), lambda i: (i, 0))],
        out_shape=[jax.ShapeDtypeStruct((n, mb), BF16),
                   jax.ShapeDtypeStruct((n, ms), F32)],
        compiler_params=_cparams(("parallel",)),
        name="even_proj",
    )(xb, w_big, w_small)


def _cumf_kernel(f_ref, b_ref, o_ref):
    seq = f_ref.shape[-1]
    row = lax.broadcasted_iota(jnp.int32, (LANES, LANES), 0)
    col = lax.broadcasted_iota(jnp.int32, (LANES, LANES), 1)
    upper = jnp.where(row <= col, 1.0, 0.0).astype(F32)
    carry = jnp.zeros((f_ref.shape[0], 1), F32)
    for c in range(seq // LANES):
        z = f_ref[:, c * LANES:(c + 1) * LANES] + b_ref[...]
        lf = jnp.minimum(z, 0.0) - jnp.log1p(jnp.exp(-jnp.abs(z)))
        cs = jnp.dot(lf, upper, preferred_element_type=F32,
                     precision=lax.Precision.HIGHEST) + carry
        o_ref[:, c * LANES:(c + 1) * LANES] = cs
        carry = cs[:, LANES - 1:LANES]


def _cumf(f_t, b_f):
    bsz, nh, seq = f_t.shape
    return pl.pallas_call(
        _cumf_kernel,
        grid=(bsz,),
        in_specs=[pl.BlockSpec((None, nh, seq), lambda b: (b, 0, 0)),
                  pl.BlockSpec((nh, 1), lambda b: (0, 0))],
        out_specs=pl.BlockSpec((None, nh, seq), lambda b: (b, 0, 0)),
        out_shape=jax.ShapeDtypeStruct((bsz, nh, seq), F32),
        compiler_params=_cparams(("parallel",)),
        name="fox_cumf",
    )(f_t, b_f)


def _attn_kernel(qa_ref, qi_ref, qb_ref, wt_ref, fq_ref, kb_ref, vb_ref, ckv_ref, kidx_ref, fk_ref,
                 wuk_ref, wuv_ref, bias_ref, bfar_ref, pq_ref, cq_ref, pk_ref, ck_ref,
                 o_ref,
                 score_ref, lhsd_ref, ct_ref, kaug_ref, vt_ref, qsel_ref, qaug_ref, qim_ref,
                 m_ref, acc_ref, work_ref, *, k_sel):
    T = BLK
    Q = QUERIES
    H = N_HEADS
    NP = H // 2
    NT = 2 * NP
    TW = 2 * T
    g = pl.program_id(1)
    i0 = 2 * g
    nblk = ct_ref.shape[0]
    lane = lax.broadcasted_iota(jnp.int32, (T, LANES), 1)
    lo_half = lane < HEAD_DIM
    ones_rows = jnp.ones((BF16_ROWS, T), BF16)
    causal_qq = (lax.broadcasted_iota(jnp.int32, (Q, Q), 0) <= lax.broadcasted_iota(jnp.int32, (Q, Q), 1))

    @pl.when(g == 0)
    def _per_batch():
        def prep(j, c):
            r = pl.ds(pl.multiple_of(j * T, T), T)
            cb = ckv_ref[r, :]
            lhsd_ref[0, r, 0:KV_RANK] = cb
            lhsd_ref[1, r, 0:KV_RANK] = cb
            ct_ref[j, 0:KV_RANK, :] = cb.astype(F32).T.astype(BF16)
            ct_ref[j, KV_RANK:PV_ROWS, :] = ones_rows
            ext = (_dot(_bf16_pieces(fk_ref[r, :] * LOG2E), pk_ref[...]) + ck_ref[...]).astype(BF16)
            for p in range(NP):
                cols = slice(p * LANES, (p + 1) * LANES)
                kaug_ref[p, r, 0:LANES] = kb_ref[r, cols]
                kaug_ref[p, r, LANES:2 * LANES] = ext[:, cols]
                vt_ref[p, j, 0:LANES, :] = vb_ref[r, cols].astype(F32).T.astype(BF16)
                vt_ref[p, j, LANES:PV_ROWS, :] = ones_rows
            return c
        lax.fori_loop(0, nblk, prep, 0)

    w_rows = wt_ref[...] * (H ** -0.5)
    ext_q = (_dot(_bf16_pieces(fq_ref[...] * LOG2E), pq_ref[...]) + cq_ref[...]).astype(BF16)
    eye = jnp.where(lax.broadcasted_iota(jnp.int32, (T, T), 0) == lax.broadcasted_iota(jnp.int32, (T, T), 1),
                    1.0, 0.0).astype(BF16)
    for tt in range(NT):
        hf, p = divmod(tt, NP)
        qrows = slice(hf * T, (hf + 1) * T)
        cols = slice(p * LANES, (p + 1) * LANES)
        qa_pair, qi_pair, qb_pair = qa_ref[qrows, cols], qi_ref[qrows, cols], qb_ref[qrows, cols]
        for e in range(2):
            h = 2 * p + e
            rows = slice(e * T, (e + 1) * T)
            mine = lo_half if e == 0 else jnp.logical_not(lo_half)
            qsel_ref[tt, rows, 0:KV_RANK] = _dot(qa_pair, wuk_ref[h]).astype(BF16)
            qsel_ref[tt, rows, KV_RANK:2 * KV_RANK] = eye
            qaug_ref[tt, rows, 0:LANES] = jnp.where(mine, qb_pair, 0)
            qaug_ref[tt, rows, LANES:2 * LANES] = ext_q[qrows, h * LANES:(h + 1) * LANES]
            qim_ref[tt, rows, :] = jnp.where(mine, qi_pair, 0)

    def reset_state():
        m_ref[...] = jnp.full(m_ref.shape, NEG_BIG, F32)
        acc_ref[...] = jnp.zeros(acc_ref.shape, F32)

    def tile_cols(tt):
        return slice(tt * TW, (tt + 1) * TW)

    def online_update(tt, s, pv_lhs):
        cols = tile_cols(tt)
        m_old = m_ref[:, cols]
        s_max = _tree(jnp.maximum, s.reshape(-1, SUBLANES, TW))
        m_new = jnp.maximum(m_old, jnp.max(s_max, axis=0, keepdims=True))
        alpha = jnp.exp2(m_old - m_new)
        pr = jnp.exp2((s - m_new[0:1, :]).astype(BF16))
        acc_ref[:, cols] = acc_ref[:, cols] * alpha[0:1, :] + _dot(pv_lhs, pr)
        m_ref[:, cols] = m_new

    def key_rows(j, nblocks):
        return pl.ds(pl.multiple_of(j * T, T), nblocks * T)

    def blocks_t(load, j, nblocks):
        return load(j) if nblocks == 1 else jnp.concatenate([load(j + b) for b in range(nblocks)], axis=1)

    def slot_tile(slot, tt, nblocks):
        return (slot * NT + tt, slice(0, nblocks * T), slice(None))

    def run_staged(steps):
        for k in range(min(LOOKAHEAD, len(steps))):
            steps[k][0](k % N_SLOTS)
        for k, (_, absorb_k) in enumerate(steps):
            if k + LOOKAHEAD < len(steps):
                steps[k + LOOKAHEAD][0]((k + LOOKAHEAD) % N_SLOTS)
            absorb_k(k % N_SLOTS)

    def staged_steps(j0, n_steps, logits, absorb):
        run_staged([(functools.partial(logits, j0 + 2 * k, 2), functools.partial(absorb, j0 + 2 * k, 2))
                    for k in range(n_steps)])

    def for_far_blocks(n_far, logits, absorb):
        n_oct = lax.shift_right_logical(n_far, 3)

        def oct_body(jj, c):
            staged_steps(8 * jj, 4, logits, absorb)
            return c
        lax.fori_loop(0, n_oct, oct_body, 0)

        @pl.when((n_far & 4) != 0)
        def _quad():
            staged_steps(8 * n_oct, 2, logits, absorb)

        return (n_far & 2) != 0, 8 * n_oct + (n_far & 4)

    def finish_blocks(leftover, far_step, before_step, diag_step):
        has_pair, pair_j = leftover

        @pl.when(jnp.logical_and(g >= 1, has_pair))
        def _three():
            run_staged([far_step(pair_j), before_step, diag_step])

        @pl.when(jnp.logical_and(g >= 1, jnp.logical_not(has_pair)))
        def _two():
            run_staged([before_step, diag_step])

        @pl.when(g == 0)
        def _first():
            run_staged([diag_step])

    def idx_scores(j, nblocks):
        k_blk = kidx_ref[key_rows(j, nblocks), :]
        acc = [jnp.zeros((nblocks * T, T), F32), jnp.zeros((nblocks * T, T), F32)]
        for tt in range(NT):
            hf, p = divmod(tt, NP)
            s = _dot_nt(k_blk, qim_ref[tt])
            for e in range(2):
                h = 2 * p + e
                w_h = w_rows[h:h + 1, hf * T:(hf + 1) * T]
                acc[hf] = acc[hf] + w_h * jnp.maximum(s[:, e * T:(e + 1) * T], 0.0)
        return jnp.concatenate(acc, axis=1)

    def fox_logits(j, nblocks, slot, causal=False):
        for tt in range(NT):
            hf, p = divmod(tt, NP)
            s = _dot_nt(kaug_ref[p, key_rows(j, nblocks), :], qaug_ref[tt])
            if causal:
                ok = causal_qq[:, hf * T:(hf + 1) * T]
                s = jnp.where(jnp.concatenate([ok, ok], axis=1), s, MASKED)
            work_ref[slot_tile(slot, tt, nblocks)] = s

    def fox_absorb(j, nblocks, slot):
        for tt in range(NT):
            p = tt % NP
            online_update(tt, work_ref[slot_tile(slot, tt, nblocks)],
                          blocks_t(lambda jb: vt_ref[p, jb], j, nblocks))

    def idx_fox_logits(j, nblocks, slot):
        score_ref[pl.ds(j, nblocks)] = idx_scores(j, nblocks).reshape(nblocks, T, Q)
        fox_logits(j, nblocks, slot)

    reset_state()
    leftover = for_far_blocks(jnp.maximum(i0 - 2, 0), idx_fox_logits, fox_absorb)

    def diag_logits(slot):
        score_ref[pl.ds(i0, 2)] = jnp.where(causal_qq, idx_scores(i0, 2), -jnp.inf).reshape(2, T, Q)
        fox_logits(i0, 2, slot, causal=True)

    def fox_far_step(j):
        return (lambda slot: idx_fox_logits(j, 2, slot), lambda slot: fox_absorb(j, 2, slot))

    finish_blocks(leftover, fox_far_step, fox_far_step(i0 - 2),
                  (diag_logits, lambda slot: fox_absorb(i0, 2, slot)))

    row_lo = lax.broadcasted_iota(jnp.int32, (LANES, T), 0) < HEAD_DIM
    for tt in range(NT):
        hf, p = divmod(tt, NP)
        c0 = slice(tt * TW, tt * TW + T)
        c1 = slice(tt * TW + T, (tt + 1) * TW)
        o_t = jnp.where(row_lo, acc_ref[0:LANES, c0] / acc_ref[LANES:LANES + 1, c0],
                        acc_ref[0:LANES, c1] / acc_ref[LANES:LANES + 1, c1])
        o_ref[hf * T:(hf + 1) * T, MIX_HALF + p * LANES:MIX_HALF + (p + 1) * LANES] = (
            o_t.T.astype(o_ref.dtype))

    kf = float(k_sel)
    t_pos = (g * Q + lax.broadcasted_iota(jnp.int32, (SUBLANES, Q), 1)).astype(F32)
    searching = (t_pos + 1.0) > kf
    G2 = 2 * T // SUBLANES
    n_pair = g + 1
    key_in_pair = (lax.broadcasted_iota(jnp.int32, (G2, SUBLANES, Q), 0) * SUBLANES
                   + lax.broadcasted_iota(jnp.int32, (G2, SUBLANES, Q), 1))

    def rep(v):
        return jnp.broadcast_to(v, (SUBLANES, Q))

    def score_pair(jj):
        return score_ref[pl.ds(2 * jj, 2)].reshape(G2, SUBLANES, Q)

    def count(*pred_fns, load=score_pair):
        def body(jj, cs):
            s = load(jj)
            return tuple(c + _tree(jnp.add, jnp.where(f(s, jj), 1.0, 0.0)) for c, f in zip(cs, pred_fns))
        zero = jnp.zeros((SUBLANES, Q), F32)
        cs = lax.fori_loop(0, n_pair, body, (zero,) * len(pred_fns))
        return tuple(rep(jnp.sum(c, axis=0, keepdims=True)) for c in cs)

    def count_ge(thr):
        return count(lambda s, jj: s >= thr[None])[0]

    def select_threshold():
        def first_body(jj, carry):
            mn, mx, ge0, gt0 = carry
            s = score_pair(jj)
            return (jnp.minimum(mn, _tree(jnp.minimum, jnp.where(s == -jnp.inf, jnp.inf, s))),
                    jnp.maximum(mx, _tree(jnp.maximum, s)),
                    ge0 + _tree(jnp.add, jnp.where(s >= 0.0, 1.0, 0.0)),
                    gt0 + _tree(jnp.add, jnp.where(s > 0.0, 1.0, 0.0)))

        zero = jnp.zeros((SUBLANES, Q), F32)
        mn, mx, c_ge0, c_gt0 = lax.fori_loop(0, n_pair, first_body, (zero + jnp.inf, zero - jnp.inf, zero, zero))
        mn = rep(jnp.min(mn, axis=0, keepdims=True))
        mx = rep(jnp.max(mx, axis=0, keepdims=True))
        c_ge0 = rep(jnp.sum(c_ge0, axis=0, keepdims=True))
        c_gt0 = rep(jnp.sum(c_gt0, axis=0, keepdims=True))
        above_mx = mx + jnp.maximum(jnp.abs(mx) * 2.0 ** -20, 1e-30)
        thr_ge0 = c_ge0 >= kf
        thr_gt0 = c_gt0 >= kf
        lo0 = jnp.where(thr_ge0, 0.0, mn)
        cnt_lo0 = jnp.where(thr_ge0, c_ge0, t_pos + 1.0)
        hi0 = jnp.where(thr_gt0, above_mx, 0.0)
        cnt_hi0 = jnp.where(thr_gt0, 0.0, jnp.where(thr_ge0, c_gt0, c_ge0))
        undecided = thr_gt0 | jnp.logical_not(thr_ge0)
        active0 = jnp.where(searching & undecided & (cnt_lo0 > kf), 1.0, 0.0)

        def cond(st):
            return jnp.logical_and(st[5] > 0.0, st[6] < 2200)

        def bisect(st):
            lo, hi, cnt_lo, cnt_hi, active = st
            mid = 0.5 * lo + 0.5 * hi
            adjacent = (mid <= lo) | (mid >= hi)
            c = count_ge(mid)
            act = active > 0.0
            go_lo = act & jnp.logical_not(adjacent) & (c >= kf)
            go_hi = act & jnp.logical_not(adjacent) & (c < kf)
            lo = jnp.where(go_lo, mid, lo)
            cnt_lo = jnp.where(go_lo, c, cnt_lo)
            hi = jnp.where(go_hi, mid, hi)
            cnt_hi = jnp.where(go_hi, c, cnt_hi)
            active = jnp.where(act & jnp.logical_not(adjacent) & (cnt_lo > kf), 1.0, 0.0)
            return lo, hi, cnt_lo, cnt_hi, active

        def body(st):
            inner = st[:5]
            for _ in range(SEARCH_UNROLL):
                inner = bisect(inner)
            return inner + (jnp.max(inner[4]), st[6] + SEARCH_UNROLL)

        lo, hi, cnt_lo, cnt_hi, _, _, _ = lax.while_loop(
            cond, body, (lo0, hi0, cnt_lo0, cnt_hi0, active0, jnp.max(active0), 0))
        thr = jnp.where(searching, lo, -F32_MAX)

        tied = searching & (cnt_lo > kf)
        need = kf - cnt_hi

        @pl.when(jnp.max(jnp.where(tied, 1.0, 0.0)) > 0.0)
        def _break_ties():
            def tie_pair(jj):
                return work_ref[jj].reshape(G2, SUBLANES, Q)

            def mark_body(jj, c):
                key_index = (2 * jj * T + key_in_pair).astype(F32)
                work_ref[jj] = (
                    jnp.where(score_pair(jj) == thr[None], key_index, jnp.inf).reshape(2 * T, Q))
                return c
            lax.fori_loop(0, n_pair, mark_body, 0)

            def count_tied_upto(jcut):
                return count(lambda e, jj: e <= jcut[None], load=tie_pair)[0]

            def bs_body(_, st):
                jl, jh = st
                jm = jnp.floor(0.5 * (jl + jh))
                ok = count_tied_upto(jm) >= need
                return jnp.where(ok, jl, jm), jnp.where(ok, jm, jh)

            n_keys = (2 * n_pair * T).astype(F32)
            jl0 = jnp.full((SUBLANES, Q), -1.0, F32)
            jh0 = jnp.zeros((SUBLANES, Q), F32) + (n_keys - 1.0)
            n_steps = int(math.ceil(math.log2(nblk * T))) + 1
            _, jcut = lax.fori_loop(0, n_steps, bs_body, (jl0, jh0))

            def drop_body(jj, c):
                e = tie_pair(jj)
                drop = tied[None] & (e > jcut[None]) & (e < jnp.inf)
                score_ref[pl.ds(2 * jj, 2)] = jnp.where(drop, -jnp.inf, score_pair(jj)).reshape(2, T, Q)
                return c
            lax.fori_loop(0, n_pair, drop_body, 0)

        return thr

    thr = lax.cond((i0 + 2) * T > k_sel, select_threshold,
                   lambda: jnp.full((SUBLANES, Q), -F32_MAX, F32))

    def mask_body(jj, c):
        add = jnp.where(score_pair(jj) >= thr[None], 0.0, MASKED).reshape(2 * T, Q).astype(BF16)
        for hf in range(2):
            lhsd_ref[hf, key_rows(2 * jj, 2), KV_RANK:2 * KV_RANK] = add[:, hf * T:(hf + 1) * T]
        return c

    lax.fori_loop(0, n_pair, mask_body, 0)

    def dsa_logits(j, nblocks, slot, bias=None):
        for tt in range(NT):
            hf, p = divmod(tt, NP)
            s = _dot_nt(lhsd_ref[hf, key_rows(j, nblocks), :], qsel_ref[tt])
            if bias is not None:
                s = s + bias(hf, slice(p * TW, (p + 1) * TW))
            work_ref[slot_tile(slot, tt, nblocks)] = s

    def dsa_absorb(j, nblocks, slot):
        pv_lhs = blocks_t(lambda jb: ct_ref[jb], j, nblocks)
        for tt in range(NT):
            online_update(tt, work_ref[slot_tile(slot, tt, nblocks)], pv_lhs)

    def bias_before(hf, cols):
        return jnp.concatenate([bias_ref[2, :, cols], bias_ref[1 + hf, :, cols]], axis=0)

    def bias_diag(hf, cols):
        first = bias_ref[hf, :, cols]
        return jnp.concatenate([first, bias_ref[0, :, cols]], axis=0)

    reset_state()
    leftover = for_far_blocks(jnp.maximum(i0 - 2, 0), dsa_logits, dsa_absorb)

    def dsa_far_step(j):
        return (lambda slot: dsa_logits(j, 2, slot), lambda slot: dsa_absorb(j, 2, slot))

    def absorb_first_biased(j):
        def absorb(slot):
            m_ref[...] = m_ref[...] + bfar_ref[...]
            dsa_absorb(j, 2, slot)
        return absorb

    diag_step = lambda absorb: (lambda slot: dsa_logits(i0, 2, slot, bias_diag), absorb)
    before_step = (lambda slot: dsa_logits(i0 - 2, 2, slot, bias_before), absorb_first_biased(i0 - 2))
    has_pair, pair_j = leftover

    @pl.when(jnp.logical_and(g >= 1, has_pair))
    def _dsa_three():
        run_staged([dsa_far_step(pair_j), before_step, diag_step(lambda slot: dsa_absorb(i0, 2, slot))])

    @pl.when(jnp.logical_and(g >= 1, jnp.logical_not(has_pair)))
    def _dsa_two():
        run_staged([before_step, diag_step(lambda slot: dsa_absorb(i0, 2, slot))])

    @pl.when(g == 0)
    def _dsa_first():
        run_staged([diag_step(absorb_first_biased(0))])

    for tt in range(NT):
        hf, p = divmod(tt, NP)
        o_pair = jnp.zeros((T, LANES), F32)
        for e in range(2):
            cols = slice(tt * TW + e * T, tt * TW + (e + 1) * T)
            o_t = acc_ref[0:KV_RANK, cols] / acc_ref[KV_RANK:KV_RANK + 1, cols]
            o_pair = o_pair + _dot(o_t.T.astype(BF16), wuv_ref[2 * p + e])
        o_ref[hf * T:(hf + 1) * T, p * LANES:(p + 1) * LANES] = o_pair.astype(o_ref.dtype)


_EV_QA, _EV_QI, _EV_QB, _EV_KB, _EV_VB = 0, 1, 2, 3, 4
_EV_CKV, _EV_KIDX = 20, 21


def _attention(hb, w_t, f_rows, wuk_pad, wuv_pad, bias_tiles, bias_far, place, k_sel):
    bsz, seq, _ = hb.shape
    T, Q, H = BLK, QUERIES, N_HEADS
    nblk = seq // T
    lanes = 2 * H * T
    pq, cq, pk, ck = place
    wide = lambda c: pl.BlockSpec((None, Q, MIX_HALF), lambda b, i, c=c: (b, i, c))
    once = dict(pipeline_mode=pl.Buffered(1))
    full_wide = lambda c: pl.BlockSpec((None, seq, MIX_HALF), lambda b, i, c=c: (b, 0, c), **once)
    full_narrow = lambda c: pl.BlockSpec((None, seq, LANES), lambda b, i, c=c: (b, 0, c), **once)
    const = lambda a: pl.BlockSpec(a.shape, lambda b, i, nd=a.ndim: (0,) * nd, **once)
    return pl.pallas_call(
        functools.partial(_attn_kernel, k_sel=k_sel),
        grid=(bsz, seq // Q),
        in_specs=[wide(_EV_QA), wide(_EV_QI), wide(_EV_QB),
                  pl.BlockSpec((None, H, Q), lambda b, i: (b, 0, i)),
                  pl.BlockSpec((None, Q, LANES), lambda b, i: (b, i, 0)),
                  full_wide(_EV_KB), full_wide(_EV_VB),
                  full_narrow(_EV_CKV), full_narrow(_EV_KIDX),
                  pl.BlockSpec((None, seq, LANES), lambda b, i: (b, 0, 0), **once),
                  const(wuk_pad), const(wuv_pad), const(bias_tiles), const(bias_far),
                  const(pq), const(cq), const(pk), const(ck)],
        out_specs=pl.BlockSpec((None, Q, 2 * MIX_HALF), lambda b, i: (b, i, 0)),
        out_shape=jax.ShapeDtypeStruct((bsz, seq, 2 * MIX_HALF), BF16),
        scratch_shapes=[pltpu.VMEM((nblk, T, Q), F32),
                        pltpu.VMEM((2, seq, 2 * KV_RANK), BF16),
                        pltpu.VMEM((nblk, PV_ROWS, T), BF16),
                        pltpu.VMEM((H // 2, seq, 2 * LANES), BF16),
                        pltpu.VMEM((H // 2, nblk, PV_ROWS, T), BF16),
                        pltpu.VMEM((H, 2 * T, 2 * KV_RANK), BF16),
                        pltpu.VMEM((H, 2 * T, 2 * LANES), BF16),
                        pltpu.VMEM((H, 2 * T, LANES), BF16),
                        pltpu.VMEM((SUBLANES, lanes), F32),
                        pltpu.VMEM((PV_ROWS, lanes), F32),
                        pltpu.VMEM((max(N_SLOTS * H, nblk // 2), 2 * T, 2 * T), F32)],
        compiler_params=_cparams(("parallel", "arbitrary")),
        name="dsa_fox_attention",
    )(hb, hb, hb, w_t, f_rows, hb, hb, hb, hb, f_rows, wuk_pad, wuv_pad, bias_tiles, bias_far,
      pq, cq, pk, ck)


def _proj_ln_kernel(a_ref, w_ref, x_ref, g_ref, b_ref, xo_ref, xb_ref, *, alpha, sub):
    for c in range(a_ref.shape[0] // sub):
        rows = slice(c * sub, (c + 1) * sub)
        y = alpha * x_ref[rows, :] + _dot(a_ref[rows, :], w_ref[...])
        out = _layer_norm_rows(y, g_ref[...], b_ref[...])
        xo_ref[rows, :] = out
        xb_ref[rows, :] = out.astype(BF16)


def _proj_ln(a, w, x, g, b, alpha, tm):
    n, k = a.shape
    d = w.shape[1]
    return pl.pallas_call(
        functools.partial(_proj_ln_kernel, alpha=alpha, sub=min(128, tm)),
        grid=(n // tm,),
        in_specs=[pl.BlockSpec((tm, k), lambda i: (i, 0)),
                  pl.BlockSpec((k, d), lambda i: (0, 0)),
                  pl.BlockSpec((tm, d), lambda i: (i, 0)),
                  pl.BlockSpec((1, d), lambda i: (0, 0)),
                  pl.BlockSpec((1, d), lambda i: (0, 0))],
        out_specs=[pl.BlockSpec((tm, d), lambda i: (i, 0)),
                   pl.BlockSpec((tm, d), lambda i: (i, 0))],
        out_shape=[jax.ShapeDtypeStruct((n, d), F32),
                   jax.ShapeDtypeStruct((n, d), BF16)],
        compiler_params=_cparams(("parallel",), [False, True, False, False, False]),
        name="proj_residual_ln",
    )(a, w, x, g, b)


def _causal_conv3(h, tail, cw):
    w0, w1, w2 = cw[0:1, :], cw[1:2, :], cw[2:3, :]
    body = w0 * pltpu.roll(h, 2, axis=0) + w1 * pltpu.roll(h, 1, axis=0) + w2 * h
    head = h[0:SUBLANES, :]
    r = lax.broadcasted_iota(jnp.int32, head.shape, 0)
    hm1 = jnp.where(r < 1, pltpu.roll(tail, 1, axis=0), pltpu.roll(head, 1, axis=0))
    hm2 = jnp.where(r < 2, pltpu.roll(tail, 2, axis=0), pltpu.roll(head, 2, axis=0))
    out_head = w0 * hm2 + w1 * hm1 + w2 * head
    return jnp.concatenate([out_head, body[SUBLANES:, :]], axis=0)


def _ffn_up_kernel(x_ref, wg_ref, wv_ref, cg_ref, cv_ref, o_ref, tg_ref, tv_ref, *h_refs,
                   tiles_per_seq, sub):
    n_chunks = x_ref.shape[0] // sub
    hg_refs, hv_refs = h_refs[:n_chunks], h_refs[n_chunks:]
    first = (pl.program_id(1) % tiles_per_seq) == 0
    body_rows = slice(SUBLANES, SUBLANES + sub)

    def matmuls(c):
        x = x_ref[c * sub:(c + 1) * sub, :]
        hg_refs[c][body_rows, :] = _dot(x, wg_ref[...])
        hv_refs[c][body_rows, :] = _dot(x, wv_ref[...])

    def conv(h_ref, cw):
        return (cw[0:1, :] * h_ref[SUBLANES - 2:SUBLANES - 2 + sub, :]
                + cw[1:2, :] * h_ref[SUBLANES - 1:SUBLANES - 1 + sub, :]
                + cw[2:3, :] * h_ref[body_rows, :])

    def gate(c):
        if c == 0:
            hg_refs[0][0:SUBLANES, :] = jnp.where(first, 0.0, tg_ref[...])
            hv_refs[0][0:SUBLANES, :] = jnp.where(first, 0.0, tv_ref[...])
        else:
            hg_refs[c][0:SUBLANES, :] = hg_refs[c - 1][sub:sub + SUBLANES, :]
            hv_refs[c][0:SUBLANES, :] = hv_refs[c - 1][sub:sub + SUBLANES, :]
        g = conv(hg_refs[c], cg_ref[...])
        v = conv(hv_refs[c], cv_ref[...])
        half_g = 0.5 * g
        o_ref[c * sub:(c + 1) * sub, :] = ((half_g + half_g * jnp.tanh(half_g)) * v).astype(o_ref.dtype)

    matmuls(0)
    for c in range(n_chunks):
        if c + 1 < n_chunks:
            matmuls(c + 1)
        gate(c)
    tg_ref[...] = hg_refs[n_chunks - 1][sub:sub + SUBLANES, :]
    tv_ref[...] = hv_refs[n_chunks - 1][sub:sub + SUBLANES, :]


def _ffn_up(xb, w_up, conv_w, seq, tm, tn, sub):
    n, k = xb.shape
    dff = w_up.shape[1] // 2
    ncol = dff // tn
    return pl.pallas_call(
        functools.partial(_ffn_up_kernel, tiles_per_seq=seq // tm, sub=sub),
        grid=(ncol, n // tm),
        in_specs=[pl.BlockSpec((tm, k), lambda j, i: (i, 0)),
                  pl.BlockSpec((k, tn), lambda j, i: (0, j)),
                  pl.BlockSpec((k, tn), lambda j, i, ncol=ncol: (0, j + ncol)),
                  pl.BlockSpec((CONV_W, tn), lambda j, i: (0, j)),
                  pl.BlockSpec((CONV_W, tn), lambda j, i, ncol=ncol: (0, j + ncol))],
        out_specs=pl.BlockSpec((tm, tn), lambda j, i: (i, j)),
        out_shape=jax.ShapeDtypeStruct((n, dff), BF16),
        scratch_shapes=([pltpu.VMEM((SUBLANES, tn), F32)] * 2
                        + [pltpu.VMEM((sub + SUBLANES, tn), F32)] * (2 * (tm // sub))),
        compiler_params=_cparams(("arbitrary", "arbitrary"), [False, True, True, False, False]),
        name="ffn_up_conv_gate",
    )(xb, w_up, w_up, conv_w, conv_w)


def _gelu_tanh(x):
    return 0.5 * x * (1.0 + jnp.tanh(math.sqrt(2.0 / math.pi) * (x + 0.044715 * (x * x * x))))


def _group_mean(v, avg):
    hi = v.astype(BF16)
    lo = (v - hi.astype(F32)).astype(BF16)
    return _dot(hi, avg) + _dot(lo, avg)


def _odd_mixer_kernel(x_ref, w_ref, sg_ref, sw_ref, sb_ref, cw_ref, wo_ref, xf_ref, g_ref, b_ref,
                      xo_ref, xb_ref, o_ref, tail_ref, *, tiles_per_seq, alpha):
    tm = x_ref.shape[0]
    W = MIX_HALF
    first = (pl.program_id(0) % tiles_per_seq) == 0
    x = x_ref[...]
    lane = lax.broadcasted_iota(jnp.int32, (CHUNK, LANES), 1)
    lo_half = lane < HEAD_DIM
    r = lax.broadcasted_iota(jnp.int32, (W, W), 0) // HEAD_DIM
    c = lax.broadcasted_iota(jnp.int32, (W, W), 1) // HEAD_DIM
    avg = jnp.where(r == c, 1.0 / HEAD_DIM, 0.0).astype(BF16)
    tri = (lax.broadcasted_iota(jnp.int32, (CHUNK, CHUNK), 1)
           <= lax.broadcasted_iota(jnp.int32, (CHUNK, CHUNK), 0))

    u = _gelu_tanh(_dot(x, w_ref[:, 0:W]))
    v = _gelu_tanh(_dot(x, w_ref[:, W:2 * W]))
    vc = v - _group_mean(v, avg)
    var = _group_mean(vc * vc, avg)
    vn = (vc * lax.rsqrt(var + LN_EPS) * sg_ref[...]).astype(BF16)
    for n in range(tm // CHUNK):
        rows = slice(n * CHUNK, (n + 1) * CHUNK)
        for p in range(N_HEADS // 2):
            cols = slice(p * LANES, (p + 1) * LANES)
            vp = vn[rows, cols]
            w_e = jnp.where(tri, sw_ref[2 * p], 0.0).astype(BF16)
            w_o = jnp.where(tri, sw_ref[2 * p + 1], 0.0).astype(BF16)
            mix = (_dot(w_e, jnp.where(lo_half, vp, 0)) + _dot(w_o, jnp.where(lo_half, 0, vp))
                   + sb_ref[:, cols])
            o_ref[rows, cols] = (u[rows, cols] * mix).astype(o_ref.dtype)

    g_b = _dot(x, w_ref[:, 2 * W:3 * W])
    y = _dot(x, w_ref[:, 3 * W:4 * W]) * _dot(x, w_ref[:, 4 * W:5 * W])
    conv = _causal_conv3(y, jnp.where(first, 0.0, tail_ref[...]), cw_ref[...])
    tail_ref[...] = y[tm - SUBLANES:tm, :]
    o_ref[:, W:2 * W] = (g_b * conv).astype(o_ref.dtype)

    out = _layer_norm_rows(alpha * xf_ref[...] + _dot(o_ref[...], wo_ref[...]), g_ref[...], b_ref[...])
    xo_ref[...] = out
    xb_ref[...] = out.astype(BF16)


def _odd_mixer(xb, w_in, sgu_g, sgu_w, sgu_b_tile, conv_w, w_o, xf, g, b, alpha, seq, tm):
    n, k = xb.shape
    W = MIX_HALF
    d = w_o.shape[1]
    const2 = lambda shape: pl.BlockSpec(shape, lambda i: (0, 0))
    row_tile = lambda width: pl.BlockSpec((tm, width), lambda i: (i, 0))
    return pl.pallas_call(
        functools.partial(_odd_mixer_kernel, tiles_per_seq=seq // tm, alpha=alpha),
        grid=(n // tm,),
        in_specs=[row_tile(k), const2((k, 5 * W)), const2((1, W)),
                  pl.BlockSpec((N_HEADS, CHUNK, CHUNK), lambda i: (0, 0, 0)),
                  const2((CHUNK, W)), const2((CONV_W, W)),
                  const2((2 * W, d)), row_tile(d), const2((1, d)), const2((1, d))],
        out_specs=[row_tile(d), row_tile(d)],
        out_shape=[jax.ShapeDtypeStruct((n, d), F32), jax.ShapeDtypeStruct((n, d), BF16)],
        scratch_shapes=[pltpu.VMEM((tm, 2 * W), BF16),
                        pltpu.VMEM((SUBLANES, W), F32)],
        compiler_params=_cparams(("arbitrary",)),
        name="odd_mixer",
    )(xb, w_in, sgu_g, sgu_w, sgu_b_tile, conv_w, w_o, xf, g, b)


def _t5_bucket_np(dist):
    max_exact = N_BUCKETS // 2
    n = np.maximum(dist, 0)
    nf = np.maximum(n, 1).astype(np.float32)
    large = max_exact + (np.log(nf / max_exact) / np.float32(math.log(T5_MAX_DIST / max_exact))
                         * (N_BUCKETS - max_exact)).astype(np.int32)
    large = np.minimum(large, N_BUCKETS - 1)
    return np.where(n < max_exact, n, large)


def _bias_tables(rel_bias):
    T = BLK
    s = np.arange(T)[:, None]
    t = np.arange(T)[None, :]
    bucket = _t5_bucket_np(np.stack([t + d * T - s for d in range(3)]))
    far_bucket = N_BUCKETS - 1
    assert (bucket[2] == far_bucket).all()
    assert (_t5_bucket_np(np.arange(T + 1, 64 * T)) == far_bucket).all()
    rb = rel_bias.astype(F32) * LOG2E
    tiles = sum(jnp.where(jnp.asarray(bucket == b)[..., None], rb[b], 0.0) for b in range(N_BUCKETS))
    tiles = jnp.swapaxes(tiles, 2, 3).reshape(3, T, N_HEADS * T)
    far = jnp.tile(jnp.repeat(rb[far_bucket], T)[None, :], (1, 2))
    return tiles, far


def _placement_constants():
    H, L = N_HEADS, LANES
    pq = np.zeros((3 * L, H * L), np.float32)
    cq = np.zeros((1, H * L), np.float32)
    pk = np.zeros((3 * L, (H // 2) * L), np.float32)
    ck = np.zeros((1, (H // 2) * L), np.float32)
    for h in range(H):
        p, e = divmod(h, 2)
        for piece in range(3):
            pq[piece * L + h, h * L + 6 + piece] = 1.0
            cq[0, h * L + 3 * e + piece] = 1.0
            pk[piece * L + h, p * L + 3 * e + piece] = -1.0
            ck[0, p * L + 6 + piece] = 1.0
    return jnp.asarray(pq, BF16), jnp.asarray(cq), jnp.asarray(pk, BF16), jnp.asarray(ck)


def _even_weights(w_in, w_uk, w_uv):
    hd, H = HEAD_DIM, N_HEADS
    widths = (H * hd, KV_RANK, H * hd, hd, H, H * hd, H * hd, H * hd, H)
    offs = np.cumsum((0,) + widths)
    seg = lambda s: w_in[:, offs[s]:offs[s + 1]]
    q_a, c_kv, q_idx, k_idx, w_idx, q_b, k_b, v_b, f_in = [seg(s) for s in range(9)]
    scale = hd ** -0.5
    w_big = jnp.concatenate([q_a, q_idx * scale, q_b * (scale * LOG2E), k_b, v_b, c_kv,
                             k_idx, k_idx], axis=1).astype(BF16)
    pad = jnp.zeros((w_in.shape[0], LANES - 2 * H), w_in.dtype)
    w_small = jnp.concatenate([w_idx, f_in, pad], axis=1).astype(BF16)
    wuk_t = jnp.swapaxes(w_uk, 1, 2) * (scale * LOG2E)
    odd = (np.arange(H) % 2 == 1)[:, None, None]
    zk, zv = jnp.zeros_like(wuk_t), jnp.zeros_like(w_uv)
    wuk_pad = jnp.where(odd, jnp.concatenate([zk, wuk_t], axis=1), jnp.concatenate([wuk_t, zk], axis=1))
    wuv_pad = jnp.where(odd, jnp.concatenate([zv, w_uv], axis=2), jnp.concatenate([w_uv, zv], axis=2))
    return w_big, w_small, wuk_pad.astype(BF16), wuv_pad.astype(BF16)


def kernel(x, ln_g, ln_b, rel_bias, ev_w_in, ev_w_uk, ev_w_uv, ev_b_f, ev_w_o,
           od_w_in, od_sgu_g, od_sgu_w, od_sgu_b, od_conv_w, od_w_o,
           ffn_w_up, ffn_conv_w, ffn_w_down):
    bsz, seq, d = x.shape
    depth = ln_g.shape[0]
    n = bsz * seq
    alpha = (2.0 * depth) ** 0.25
    k_sel = min(TOPK_MAX, seq // 4)
    H = N_HEADS
    assert seq % QUERIES == 0 and seq % CHUNK == 0
    tm_ffn = min(1024, seq)

    bias_tiles, bias_far = _bias_tables(rel_bias)
    place = _placement_constants()
    xf = x.reshape(n, d)
    xb = xf
    for layer in range(depth):
        j = layer // 2
        g0, b0 = ln_g[layer, 0][None, :], ln_b[layer, 0][None, :]
        g1, b1 = ln_g[layer, 1][None, :], ln_b[layer, 1][None, :]
        if layer % 2 == 0:
            w_big, w_small, wuk_pad, wuv_pad = _even_weights(ev_w_in[j], ev_w_uk[j], ev_w_uv[j])
            hb, hs = _even_proj(xb, w_big, w_small, tm_ffn)
            hb = hb.reshape(bsz, seq, -1)
            hs_t = jnp.swapaxes(hs.reshape(bsz, seq, -1)[:, :, 0:2 * H], 1, 2)
            cum_f = _cumf(hs_t[:, H:2 * H], ev_b_f[j][:, None])
            f_rows = jnp.pad(jnp.swapaxes(cum_f, 1, 2), ((0, 0), (0, 0), (0, LANES - H)))
            mix = _attention(hb, hs_t[:, 0:H], f_rows, wuk_pad, wuv_pad, bias_tiles, bias_far,
                             place, k_sel)
            xf, xb = _proj_ln(mix.reshape(n, d), ev_w_o[j].astype(BF16), xf, g0, b0, alpha, tm_ffn)
        else:
            sgu_b_tile = jnp.repeat(jnp.swapaxes(od_sgu_b[j], 0, 1), HEAD_DIM, axis=1)
            xf, xb = _odd_mixer(xb, od_w_in[j].astype(BF16), od_sgu_g[j].reshape(1, -1), od_sgu_w[j],
                                sgu_b_tile, od_conv_w[j], od_w_o[j].astype(BF16), xf, g0, b0, alpha,
                                seq, min(1024, seq))
        act = _ffn_up(xb, ffn_w_up[layer].astype(BF16), ffn_conv_w[layer], seq, min(4096, seq), 256,
                      min(512, seq))
        xf, xb = _proj_ln(act, ffn_w_down[layer].astype(BF16), xf, g1, b1, alpha, tm_ffn)
    return xf.reshape(bsz, seq, d)
```

```python
import functools
import math

import numpy as np
import jax
import jax.numpy as jnp
from jax import lax
from jax.experimental import pallas as pl
from jax.experimental.pallas import tpu as pltpu

F32 = jnp.float32
BF16 = jnp.bfloat16

HEAD_DIM = 64
N_HEADS = 8
KV_RANK = 128
TOPK_MAX = 256
N_BUCKETS = 32
T5_MAX_DIST = 128
CHUNK = 128
CONV_W = 3
LN_EPS = 1e-5
MIX_HALF = N_HEADS * HEAD_DIM

LANES = 128
SUBLANES = 8
BF16_ROWS = 16
VMEM_LIMIT = 56 * 1024 * 1024

BLK = 128
QUERIES = 2 * BLK
PV_ROWS = KV_RANK + BF16_ROWS
NEG_BIG = -1e30
MASKED = -(2.0 ** 100)
F32_MAX = float(np.finfo(np.float32).max)
LOG2E = math.log2(math.e)
SEARCH_UNROLL = 3
LOOKAHEAD = 2
N_SLOTS = LOOKAHEAD + 1


def _cparams(sem):
    return pltpu.CompilerParams(dimension_semantics=sem, vmem_limit_bytes=VMEM_LIMIT)


def _dot(a, b):
    return jnp.dot(a, b, preferred_element_type=F32)


def _dot_nt(a, b):
    return lax.dot_general(a, b, (((1,), (1,)), ((), ())), preferred_element_type=F32)


def _layer_norm_rows(y, g, b):
    mu = jnp.mean(y, axis=-1, keepdims=True)
    yc = y - mu
    var = jnp.mean(yc * yc, axis=-1, keepdims=True)
    return yc * lax.rsqrt(var + LN_EPS) * g + b


def _tree(op, x):
    while x.shape[0] > 1:
        half = x.shape[0] // 2
        x = op(x[:half], x[half:])
    return x[0]


def _bf16_pieces(v):
    hi = v.astype(BF16)
    r1 = v - hi.astype(F32)
    mid = r1.astype(BF16)
    lo = (r1 - mid.astype(F32)).astype(BF16)
    return jnp.concatenate([hi, mid, lo], axis=1)


def _even_proj_kernel(x_ref, wb_ref, ws_ref, hb_ref, hs_ref):
    x = x_ref[...].astype(BF16)
    hb_ref[...] = _dot(x, wb_ref[...]).astype(hb_ref.dtype)
    hs_ref[...] = _dot(x, ws_ref[...]).T


def _even_proj(xb, w_big, w_small, tm):
    n, k = xb.shape
    mb = w_big.shape[1]
    ms = w_small.shape[1]
    return pl.pallas_call(
        _even_proj_kernel,
        grid=(n // tm,),
        in_specs=[pl.BlockSpec((tm, k), lambda i: (i, 0)),
                  pl.BlockSpec((k, mb), lambda i: (0, 0)),
                  pl.BlockSpec((k, ms), lambda i: (0, 0))],
        out_specs=[pl.BlockSpec((tm, mb), lambda i: (i, 0)),
                   pl.BlockSpec((ms, tm), lambda i: (0, i))],
        out_shape=[jax.ShapeDtypeStruct((n, mb), BF16),
                   jax.ShapeDtypeStruct((ms, n), F32)],
        compiler_params=_cparams(("parallel",)),
        name="even_proj",
    )(xb, w_big, w_small)


def _cumf_kernel(f_ref, b_ref, o_ref):
    seq = f_ref.shape[-1]
    row = lax.broadcasted_iota(jnp.int32, (LANES, LANES), 0)
    col = lax.broadcasted_iota(jnp.int32, (LANES, LANES), 1)
    upper = jnp.where(row <= col, 1.0, 0.0).astype(F32)
    nh = f_ref.shape[0]
    carry = jnp.zeros((nh, 1), F32)
    pad_rows = jnp.zeros((LANES - nh, LANES), F32)
    for c in range(seq // LANES):
        z = f_ref[:, c * LANES:(c + 1) * LANES] + b_ref[...]
        lf = jnp.minimum(z, 0.0) - jnp.log1p(jnp.exp(-jnp.abs(z)))
        cs = jnp.dot(lf, upper, preferred_element_type=F32,
                     precision=lax.Precision.HIGHEST) + carry
        o_ref[c * LANES:(c + 1) * LANES, :] = jnp.concatenate([cs, pad_rows], axis=0).T
        carry = cs[:, LANES - 1:LANES]


def _cumf(f_t, b_f):
    bsz, nh, seq = f_t.shape
    return pl.pallas_call(
        _cumf_kernel,
        grid=(bsz,),
        in_specs=[pl.BlockSpec((None, nh, seq), lambda b: (b, 0, 0)),
                  pl.BlockSpec((nh, 1), lambda b: (0, 0))],
        out_specs=pl.BlockSpec((None, seq, LANES), lambda b: (b, 0, 0)),
        out_shape=jax.ShapeDtypeStruct((bsz, seq, LANES), F32),
        compiler_params=_cparams(("parallel",)),
        name="fox_cumf",
    )(f_t, b_f)


def _attn_kernel(qa_ref, qi_ref, qb_ref, wt_ref, fq_ref, kb_ref, vb_ref, ckv_ref, kidx_ref, fk_ref,
                 wuk_ref, wuv_ref, bias_ref, bfar_ref, pq_ref, cq_ref, pk_ref, ck_ref,
                 o_ref,
                 score_ref, lhsd_ref, ct_ref, kaug_ref, vt_ref, qsel_ref, qaug_ref, qim_ref,
                 m_ref, acc_ref, work_ref, *, k_sel):
    T = BLK
    Q = QUERIES
    H = N_HEADS
    NP = H // 2
    NT = 2 * NP
    TW = 2 * T
    g = pl.program_id(1)
    i0 = 2 * g
    nblk = ct_ref.shape[0]
    lane = lax.broadcasted_iota(jnp.int32, (T, LANES), 1)
    lo_half = lane < HEAD_DIM
    ones_rows = jnp.ones((BF16_ROWS, T), BF16)
    causal_qq = (lax.broadcasted_iota(jnp.int32, (Q, Q), 0) <= lax.broadcasted_iota(jnp.int32, (Q, Q), 1))

    @pl.when(g == 0)
    def _per_batch():
        def prep(j, c):
            r = pl.ds(pl.multiple_of(j * T, T), T)
            cb = ckv_ref[r, :]
            lhsd_ref[0, r, 0:KV_RANK] = cb
            lhsd_ref[1, r, 0:KV_RANK] = cb
            ct_ref[j, 0:KV_RANK, :] = cb.astype(F32).T.astype(BF16)
            ct_ref[j, KV_RANK:PV_ROWS, :] = ones_rows
            ext = (_dot(_bf16_pieces(fk_ref[r, :] * LOG2E), pk_ref[...]) + ck_ref[...]).astype(BF16)
            for p in range(NP):
                cols = slice(p * LANES, (p + 1) * LANES)
                kaug_ref[p, r, 0:LANES] = kb_ref[r, cols]
                kaug_ref[p, r, LANES:2 * LANES] = ext[:, cols]
                vt_ref[p, j, 0:LANES, :] = vb_ref[r, cols].astype(F32).T.astype(BF16)
                vt_ref[p, j, LANES:PV_ROWS, :] = ones_rows
            return c
        lax.fori_loop(0, nblk, prep, 0)

    w_rows = wt_ref[...] * (H ** -0.5)
    ext_q = (_dot(_bf16_pieces(fq_ref[...] * LOG2E), pq_ref[...]) + cq_ref[...]).astype(BF16)
    eye = jnp.where(lax.broadcasted_iota(jnp.int32, (T, T), 0) == lax.broadcasted_iota(jnp.int32, (T, T), 1),
                    1.0, 0.0).astype(BF16)
    for tt in range(NT):
        hf, p = divmod(tt, NP)
        qrows = slice(hf * T, (hf + 1) * T)
        cols = slice(p * LANES, (p + 1) * LANES)
        qa_pair, qi_pair, qb_pair = qa_ref[qrows, cols], qi_ref[qrows, cols], qb_ref[qrows, cols]
        for e in range(2):
            h = 2 * p + e
            rows = slice(e * T, (e + 1) * T)
            mine = lo_half if e == 0 else jnp.logical_not(lo_half)
            qsel_ref[tt, rows, 0:KV_RANK] = _dot(qa_pair, wuk_ref[h]).astype(BF16)
            qsel_ref[tt, rows, KV_RANK:2 * KV_RANK] = eye
            qaug_ref[tt, rows, 0:LANES] = jnp.where(mine, qb_pair, 0)
            qaug_ref[tt, rows, LANES:2 * LANES] = ext_q[qrows, h * LANES:(h + 1) * LANES]
            qim_ref[tt, rows, :] = jnp.where(mine, qi_pair, 0)

    def reset_state():
        m_ref[...] = jnp.full(m_ref.shape, NEG_BIG, F32)
        acc_ref[...] = jnp.zeros(acc_ref.shape, F32)

    def tile_cols(tt):
        return slice(tt * TW, (tt + 1) * TW)

    def online_update(tt, s, pv_lhs):
        cols = tile_cols(tt)
        m_old = m_ref[:, cols]
        s_max = _tree(jnp.maximum, s.reshape(-1, SUBLANES, TW))
        m_new = jnp.maximum(m_old, jnp.max(s_max, axis=0, keepdims=True))
        alpha = jnp.exp2(m_old - m_new)
        pr = jnp.exp2((s - m_new[0:1, :]).astype(BF16))
        acc_ref[:, cols] = acc_ref[:, cols] * alpha[0:1, :] + _dot(pv_lhs, pr)
        m_ref[:, cols] = m_new

    def key_rows(j, nblocks):
        return pl.ds(pl.multiple_of(j * T, T), nblocks * T)

    def blocks_t(load, j, nblocks):
        return load(j) if nblocks == 1 else jnp.concatenate([load(j + b) for b in range(nblocks)], axis=1)

    def slot_tile(slot, tt, nblocks):
        return (slot * NT + tt, slice(0, nblocks * T), slice(None))

    def run_staged(steps):
        for k in range(min(LOOKAHEAD, len(steps))):
            steps[k][0](k % N_SLOTS)
        for k, (_, absorb_k) in enumerate(steps):
            if k + LOOKAHEAD < len(steps):
                steps[k + LOOKAHEAD][0]((k + LOOKAHEAD) % N_SLOTS)
            absorb_k(k % N_SLOTS)

    def staged_steps(j0, n_steps, logits, absorb):
        run_staged([(functools.partial(logits, j0 + 2 * k, 2), functools.partial(absorb, j0 + 2 * k, 2))
                    for k in range(n_steps)])

    def for_far_blocks(n_far, logits, absorb):
        n_oct = lax.shift_right_logical(n_far, 3)

        def oct_body(jj, c):
            staged_steps(8 * jj, 4, logits, absorb)
            return c
        lax.fori_loop(0, n_oct, oct_body, 0)

        @pl.when((n_far & 4) != 0)
        def _quad():
            staged_steps(8 * n_oct, 2, logits, absorb)

        return (n_far & 2) != 0, 8 * n_oct + (n_far & 4)

    def finish_blocks(leftover, far_step, before_step, diag_step):
        has_pair, pair_j = leftover

        @pl.when(jnp.logical_and(g >= 1, has_pair))
        def _three():
            run_staged([far_step(pair_j), before_step, diag_step])

        @pl.when(jnp.logical_and(g >= 1, jnp.logical_not(has_pair)))
        def _two():
            run_staged([before_step, diag_step])

        @pl.when(g == 0)
        def _first():
            run_staged([diag_step])

    def idx_scores(j, nblocks):
        k_blk = kidx_ref[key_rows(j, nblocks), :]
        acc = [jnp.zeros((nblocks * T, T), F32), jnp.zeros((nblocks * T, T), F32)]
        for tt in range(NT):
            hf, p = divmod(tt, NP)
            s = _dot_nt(k_blk, qim_ref[tt])
            for e in range(2):
                h = 2 * p + e
                w_h = w_rows[h:h + 1, hf * T:(hf + 1) * T]
                acc[hf] = acc[hf] + w_h * jnp.maximum(s[:, e * T:(e + 1) * T], 0.0)
        return jnp.concatenate(acc, axis=1)

    def fox_logits(j, nblocks, slot, causal=False):
        for tt in range(NT):
            hf, p = divmod(tt, NP)
            s = _dot_nt(kaug_ref[p, key_rows(j, nblocks), :], qaug_ref[tt])
            if causal:
                ok = causal_qq[:, hf * T:(hf + 1) * T]
                s = jnp.where(jnp.concatenate([ok, ok], axis=1), s, MASKED)
            work_ref[slot_tile(slot, tt, nblocks)] = s

    def fox_absorb(j, nblocks, slot):
        for tt in range(NT):
            p = tt % NP
            online_update(tt, work_ref[slot_tile(slot, tt, nblocks)],
                          blocks_t(lambda jb: vt_ref[p, jb], j, nblocks))

    def idx_fox_logits(j, nblocks, slot):
        score_ref[pl.ds(j, nblocks)] = idx_scores(j, nblocks).reshape(nblocks, T, Q)
        fox_logits(j, nblocks, slot)

    reset_state()
    leftover = for_far_blocks(jnp.maximum(i0 - 2, 0), idx_fox_logits, fox_absorb)

    def diag_logits(slot):
        score_ref[pl.ds(i0, 2)] = jnp.where(causal_qq, idx_scores(i0, 2), -jnp.inf).reshape(2, T, Q)
        fox_logits(i0, 2, slot, causal=True)

    def fox_far_step(j):
        return (lambda slot: idx_fox_logits(j, 2, slot), lambda slot: fox_absorb(j, 2, slot))

    finish_blocks(leftover, fox_far_step, fox_far_step(i0 - 2),
                  (diag_logits, lambda slot: fox_absorb(i0, 2, slot)))

    row_lo = lax.broadcasted_iota(jnp.int32, (LANES, T), 0) < HEAD_DIM
    for tt in range(NT):
        hf, p = divmod(tt, NP)
        c0 = slice(tt * TW, tt * TW + T)
        c1 = slice(tt * TW + T, (tt + 1) * TW)
        o_t = jnp.where(row_lo, acc_ref[0:LANES, c0] / acc_ref[LANES:LANES + 1, c0],
                        acc_ref[0:LANES, c1] / acc_ref[LANES:LANES + 1, c1])
        o_ref[hf * T:(hf + 1) * T, MIX_HALF + p * LANES:MIX_HALF + (p + 1) * LANES] = (
            o_t.T.astype(o_ref.dtype))

    kf = float(k_sel)
    t_pos = (g * Q + lax.broadcasted_iota(jnp.int32, (SUBLANES, Q), 1)).astype(F32)
    searching = (t_pos + 1.0) > kf
    G2 = 2 * T // SUBLANES
    n_pair = g + 1
    key_in_pair = (lax.broadcasted_iota(jnp.int32, (G2, SUBLANES, Q), 0) * SUBLANES
                   + lax.broadcasted_iota(jnp.int32, (G2, SUBLANES, Q), 1))

    def rep(v):
        return jnp.broadcast_to(v, (SUBLANES, Q))

    def score_pair(jj):
        return score_ref[pl.ds(2 * jj, 2)].reshape(G2, SUBLANES, Q)

    def count(*pred_fns, load=score_pair):
        def body(jj, cs):
            s = load(jj)
            return tuple(c + _tree(jnp.add, jnp.where(f(s, jj), 1.0, 0.0)) for c, f in zip(cs, pred_fns))
        zero = jnp.zeros((SUBLANES, Q), F32)
        cs = lax.fori_loop(0, n_pair, body, (zero,) * len(pred_fns))
        return tuple(rep(jnp.sum(c, axis=0, keepdims=True)) for c in cs)

    def count_ge(thr):
        return count(lambda s, jj: s >= thr[None])[0]

    def select_threshold():
        def first_body(jj, carry):
            mn, mx, ge0, gt0 = carry
            s = score_pair(jj)
            return (jnp.minimum(mn, _tree(jnp.minimum, jnp.where(s == -jnp.inf, jnp.inf, s))),
                    jnp.maximum(mx, _tree(jnp.maximum, s)),
                    ge0 + _tree(jnp.add, jnp.where(s >= 0.0, 1.0, 0.0)),
                    gt0 + _tree(jnp.add, jnp.where(s > 0.0, 1.0, 0.0)))

        zero = jnp.zeros((SUBLANES, Q), F32)
        mn, mx, c_ge0, c_gt0 = lax.fori_loop(0, n_pair, first_body, (zero + jnp.inf, zero - jnp.inf, zero, zero))
        mn = rep(jnp.min(mn, axis=0, keepdims=True))
        mx = rep(jnp.max(mx, axis=0, keepdims=True))
        c_ge0 = rep(jnp.sum(c_ge0, axis=0, keepdims=True))
        c_gt0 = rep(jnp.sum(c_gt0, axis=0, keepdims=True))
        above_mx = mx + jnp.maximum(jnp.abs(mx) * 2.0 ** -20, 1e-30)
        thr_ge0 = c_ge0 >= kf
        thr_gt0 = c_gt0 >= kf
        lo0 = jnp.where(thr_ge0, 0.0, mn)
        cnt_lo0 = jnp.where(thr_ge0, c_ge0, t_pos + 1.0)
        hi0 = jnp.where(thr_gt0, above_mx, 0.0)
        cnt_hi0 = jnp.where(thr_gt0, 0.0, jnp.where(thr_ge0, c_gt0, c_ge0))
        undecided = thr_gt0 | jnp.logical_not(thr_ge0)
        active0 = jnp.where(searching & undecided & (cnt_lo0 > kf), 1.0, 0.0)

        def cond(st):
            return jnp.logical_and(st[5] > 0.0, st[6] < 2200)

        def bisect(st):
            lo, hi, cnt_lo, cnt_hi, active = st
            mid = 0.5 * lo + 0.5 * hi
            adjacent = (mid <= lo) | (mid >= hi)
            c = count_ge(mid)
            act = active > 0.0
            go_lo = act & jnp.logical_not(adjacent) & (c >= kf)
            go_hi = act & jnp.logical_not(adjacent) & (c < kf)
            lo = jnp.where(go_lo, mid, lo)
            cnt_lo = jnp.where(go_lo, c, cnt_lo)
            hi = jnp.where(go_hi, mid, hi)
            cnt_hi = jnp.where(go_hi, c, cnt_hi)
            active = jnp.where(act & jnp.logical_not(adjacent) & (cnt_lo > kf), 1.0, 0.0)
            return lo, hi, cnt_lo, cnt_hi, active

        def body(st):
            inner = st[:5]
            for _ in range(SEARCH_UNROLL):
                inner = bisect(inner)
            return inner + (jnp.max(inner[4]), st[6] + SEARCH_UNROLL)

        lo, hi, cnt_lo, cnt_hi, _, _, _ = lax.while_loop(
            cond, body, (lo0, hi0, cnt_lo0, cnt_hi0, active0, jnp.max(active0), 0))
        thr = jnp.where(searching, lo, -F32_MAX)

        tied = searching & (cnt_lo > kf)
        need = kf - cnt_hi

        @pl.when(jnp.max(jnp.where(tied, 1.0, 0.0)) > 0.0)
        def _break_ties():
            def tie_pair(jj):
                return work_ref[jj].reshape(G2, SUBLANES, Q)

            def mark_body(jj, c):
                key_index = (2 * jj * T + key_in_pair).astype(F32)
                work_ref[jj] = (
                    jnp.where(score_pair(jj) == thr[None], key_index, jnp.inf).reshape(2 * T, Q))
                return c
            lax.fori_loop(0, n_pair, mark_body, 0)

            def count_tied_upto(jcut):
                return count(lambda e, jj: e <= jcut[None], load=tie_pair)[0]

            def bs_body(_, st):
                jl, jh = st
                jm = jnp.floor(0.5 * (jl + jh))
                ok = count_tied_upto(jm) >= need
                return jnp.where(ok, jl, jm), jnp.where(ok, jm, jh)

            n_keys = (2 * n_pair * T).astype(F32)
            jl0 = jnp.full((SUBLANES, Q), -1.0, F32)
            jh0 = jnp.zeros((SUBLANES, Q), F32) + (n_keys - 1.0)
            n_steps = int(math.ceil(math.log2(nblk * T))) + 1
            _, jcut = lax.fori_loop(0, n_steps, bs_body, (jl0, jh0))

            def drop_body(jj, c):
                e = tie_pair(jj)
                drop = tied[None] & (e > jcut[None]) & (e < jnp.inf)
                score_ref[pl.ds(2 * jj, 2)] = jnp.where(drop, -jnp.inf, score_pair(jj)).reshape(2, T, Q)
                return c
            lax.fori_loop(0, n_pair, drop_body, 0)

        return thr

    thr = lax.cond((i0 + 2) * T > k_sel, select_threshold,
                   lambda: jnp.full((SUBLANES, Q), -F32_MAX, F32))

    def mask_body(jj, c):
        add = jnp.where(score_pair(jj) >= thr[None], 0.0, MASKED).reshape(2 * T, Q).astype(BF16)
        for hf in range(2):
            lhsd_ref[hf, key_rows(2 * jj, 2), KV_RANK:2 * KV_RANK] = add[:, hf * T:(hf + 1) * T]
        return c

    lax.fori_loop(0, n_pair, mask_body, 0)

    def dsa_logits(j, nblocks, slot, bias=None):
        for tt in range(NT):
            hf, p = divmod(tt, NP)
            s = _dot_nt(lhsd_ref[hf, key_rows(j, nblocks), :], qsel_ref[tt])
            if bias is not None:
                s = s + bias(hf, slice(p * TW, (p + 1) * TW))
            work_ref[slot_tile(slot, tt, nblocks)] = s

    def dsa_absorb(j, nblocks, slot):
        pv_lhs = blocks_t(lambda jb: ct_ref[jb], j, nblocks)
        for tt in range(NT):
            online_update(tt, work_ref[slot_tile(slot, tt, nblocks)], pv_lhs)

    def bias_before(hf, cols):
        return jnp.concatenate([bias_ref[2, :, cols], bias_ref[1 + hf, :, cols]], axis=0)

    def bias_diag(hf, cols):
        first = bias_ref[hf, :, cols]
        return jnp.concatenate([first, bias_ref[0, :, cols]], axis=0)

    reset_state()
    leftover = for_far_blocks(jnp.maximum(i0 - 2, 0), dsa_logits, dsa_absorb)

    def dsa_far_step(j):
        return (lambda slot: dsa_logits(j, 2, slot), lambda slot: dsa_absorb(j, 2, slot))

    def absorb_first_biased(j):
        def absorb(slot):
            m_ref[...] = m_ref[...] + bfar_ref[...]
            dsa_absorb(j, 2, slot)
        return absorb

    diag_step = lambda absorb: (lambda slot: dsa_logits(i0, 2, slot, bias_diag), absorb)
    before_step = (lambda slot: dsa_logits(i0 - 2, 2, slot, bias_before), absorb_first_biased(i0 - 2))
    has_pair, pair_j = leftover

    @pl.when(jnp.logical_and(g >= 1, has_pair))
    def _dsa_three():
        run_staged([dsa_far_step(pair_j), before_step, diag_step(lambda slot: dsa_absorb(i0, 2, slot))])

    @pl.when(jnp.logical_and(g >= 1, jnp.logical_not(has_pair)))
    def _dsa_two():
        run_staged([before_step, diag_step(lambda slot: dsa_absorb(i0, 2, slot))])

    @pl.when(g == 0)
    def _dsa_first():
        run_staged([diag_step(absorb_first_biased(0))])

    for tt in range(NT):
        hf, p = divmod(tt, NP)
        o_pair = jnp.zeros((T, LANES), F32)
        for e in range(2):
            cols = slice(tt * TW + e * T, tt * TW + (e + 1) * T)
            o_t = acc_ref[0:KV_RANK, cols] / acc_ref[KV_RANK:KV_RANK + 1, cols]
            o_pair = o_pair + _dot(o_t.T.astype(BF16), wuv_ref[2 * p + e])
        o_ref[hf * T:(hf + 1) * T, p * LANES:(p + 1) * LANES] = o_pair.astype(o_ref.dtype)


_EV_QA, _EV_QI, _EV_QB, _EV_KB, _EV_VB = 0, 1, 2, 3, 4
_EV_CKV, _EV_KIDX = 20, 21


def _attention(hb, w_t, f_rows, wuk_pad, wuv_pad, bias_tiles, bias_far, place, k_sel):
    bsz, seq, _ = hb.shape
    T, Q, H = BLK, QUERIES, N_HEADS
    nblk = seq // T
    lanes = 2 * H * T
    pq, cq, pk, ck = place
    wide = lambda c: pl.BlockSpec((None, Q, MIX_HALF), lambda b, i, c=c: (b, i, c))
    once = dict(pipeline_mode=pl.Buffered(1))
    full_wide = lambda c: pl.BlockSpec((None, seq, MIX_HALF), lambda b, i, c=c: (b, 0, c), **once)
    full_narrow = lambda c: pl.BlockSpec((None, seq, LANES), lambda b, i, c=c: (b, 0, c), **once)
    const = lambda a: pl.BlockSpec(a.shape, lambda b, i, nd=a.ndim: (0,) * nd, **once)
    return pl.pallas_call(
        functools.partial(_attn_kernel, k_sel=k_sel),
        grid=(bsz, seq // Q),
        in_specs=[wide(_EV_QA), wide(_EV_QI), wide(_EV_QB),
                  pl.BlockSpec((None, H, Q), lambda b, i: (b, 0, i)),
                  pl.BlockSpec((None, Q, LANES), lambda b, i: (b, i, 0)),
                  full_wide(_EV_KB), full_wide(_EV_VB),
                  full_narrow(_EV_CKV), full_narrow(_EV_KIDX),
                  pl.BlockSpec((None, seq, LANES), lambda b, i: (b, 0, 0), **once),
                  const(wuk_pad), const(wuv_pad), const(bias_tiles), const(bias_far),
                  const(pq), const(cq), const(pk), const(ck)],
        out_specs=pl.BlockSpec((None, Q, 2 * MIX_HALF), lambda b, i: (b, i, 0)),
        out_shape=jax.ShapeDtypeStruct((bsz, seq, 2 * MIX_HALF), BF16),
        scratch_shapes=[pltpu.VMEM((nblk, T, Q), F32),
                        pltpu.VMEM((2, seq, 2 * KV_RANK), BF16),
                        pltpu.VMEM((nblk, PV_ROWS, T), BF16),
                        pltpu.VMEM((H // 2, seq, 2 * LANES), BF16),
                        pltpu.VMEM((H // 2, nblk, PV_ROWS, T), BF16),
                        pltpu.VMEM((H, 2 * T, 2 * KV_RANK), BF16),
                        pltpu.VMEM((H, 2 * T, 2 * LANES), BF16),
                        pltpu.VMEM((H, 2 * T, LANES), BF16),
                        pltpu.VMEM((SUBLANES, lanes), F32),
                        pltpu.VMEM((PV_ROWS, lanes), F32),
                        pltpu.VMEM((max(N_SLOTS * H, nblk // 2), 2 * T, 2 * T), F32)],
        compiler_params=_cparams(("parallel", "arbitrary")),
        name="dsa_fox_attention",
    )(hb, hb, hb, w_t, f_rows, hb, hb, hb, hb, f_rows, wuk_pad, wuv_pad, bias_tiles, bias_far,
      pq, cq, pk, ck)


def _proj_ln_kernel(a_ref, w_ref, x_ref, g_ref, b_ref, xo_ref, xb_ref, *, alpha, sub):
    for c in range(a_ref.shape[0] // sub):
        rows = slice(c * sub, (c + 1) * sub)
        y = alpha * x_ref[rows, :] + _dot(a_ref[rows, :], w_ref[...])
        out = _layer_norm_rows(y, g_ref[...], b_ref[...])
        xo_ref[rows, :] = out
        xb_ref[rows, :] = out.astype(BF16)


def _proj_ln(a, w, x, g, b, alpha, tm):
    n, k = a.shape
    d = w.shape[1]
    return pl.pallas_call(
        functools.partial(_proj_ln_kernel, alpha=alpha, sub=min(128, tm)),
        grid=(n // tm,),
        in_specs=[pl.BlockSpec((tm, k), lambda i: (i, 0)),
                  pl.BlockSpec((k, d), lambda i: (0, 0)),
                  pl.BlockSpec((tm, d), lambda i: (i, 0)),
                  pl.BlockSpec((1, d), lambda i: (0, 0)),
                  pl.BlockSpec((1, d), lambda i: (0, 0))],
        out_specs=[pl.BlockSpec((tm, d), lambda i: (i, 0)),
                   pl.BlockSpec((tm, d), lambda i: (i, 0))],
        out_shape=[jax.ShapeDtypeStruct((n, d), F32),
                   jax.ShapeDtypeStruct((n, d), BF16)],
        compiler_params=_cparams(("parallel",)),
        name="proj_residual_ln",
    )(a, w, x, g, b)


def _causal_conv3(h, tail, cw):
    w0, w1, w2 = cw[0:1, :], cw[1:2, :], cw[2:3, :]
    body = w0 * pltpu.roll(h, 2, axis=0) + w1 * pltpu.roll(h, 1, axis=0) + w2 * h
    head = h[0:SUBLANES, :]
    r = lax.broadcasted_iota(jnp.int32, head.shape, 0)
    hm1 = jnp.where(r < 1, pltpu.roll(tail, 1, axis=0), pltpu.roll(head, 1, axis=0))
    hm2 = jnp.where(r < 2, pltpu.roll(tail, 2, axis=0), pltpu.roll(head, 2, axis=0))
    out_head = w0 * hm2 + w1 * hm1 + w2 * head
    return jnp.concatenate([out_head, body[SUBLANES:, :]], axis=0)


def _ffn_up_kernel(x_ref, wg_ref, wv_ref, cg_ref, cv_ref, o_ref, tg_ref, tv_ref, *h_refs,
                   tiles_per_seq, sub):
    n_chunks = x_ref.shape[0] // sub
    hg_refs, hv_refs = h_refs[:n_chunks], h_refs[n_chunks:]
    first = (pl.program_id(1) % tiles_per_seq) == 0
    body_rows = slice(SUBLANES, SUBLANES + sub)

    def matmuls(c):
        x = x_ref[c * sub:(c + 1) * sub, :]
        hg_refs[c][body_rows, :] = _dot(x, wg_ref[...])
        hv_refs[c][body_rows, :] = _dot(x, wv_ref[...])

    def conv(h_ref, cw):
        return (cw[0:1, :] * h_ref[SUBLANES - 2:SUBLANES - 2 + sub, :]
                + cw[1:2, :] * h_ref[SUBLANES - 1:SUBLANES - 1 + sub, :]
                + cw[2:3, :] * h_ref[body_rows, :])

    def gate(c):
        if c == 0:
            hg_refs[0][0:SUBLANES, :] = jnp.where(first, 0.0, tg_ref[...])
            hv_refs[0][0:SUBLANES, :] = jnp.where(first, 0.0, tv_ref[...])
        else:
            hg_refs[c][0:SUBLANES, :] = hg_refs[c - 1][sub:sub + SUBLANES, :]
            hv_refs[c][0:SUBLANES, :] = hv_refs[c - 1][sub:sub + SUBLANES, :]
        g = conv(hg_refs[c], cg_ref[...])
        v = conv(hv_refs[c], cv_ref[...])
        half_g = 0.5 * g
        o_ref[c * sub:(c + 1) * sub, :] = ((half_g + half_g * jnp.tanh(half_g)) * v).astype(o_ref.dtype)

    matmuls(0)
    for c in range(n_chunks):
        if c + 1 < n_chunks:
            matmuls(c + 1)
        gate(c)
    tg_ref[...] = hg_refs[n_chunks - 1][sub:sub + SUBLANES, :]
    tv_ref[...] = hv_refs[n_chunks - 1][sub:sub + SUBLANES, :]


def _ffn_up(xb, w_up, conv_w, seq, tm, tn, sub):
    n, k = xb.shape
    dff = w_up.shape[1] // 2
    ncol = dff // tn
    return pl.pallas_call(
        functools.partial(_ffn_up_kernel, tiles_per_seq=seq // tm, sub=sub),
        grid=(ncol, n // tm),
        in_specs=[pl.BlockSpec((tm, k), lambda j, i: (i, 0)),
                  pl.BlockSpec((k, tn), lambda j, i: (0, j)),
                  pl.BlockSpec((k, tn), lambda j, i, ncol=ncol: (0, j + ncol)),
                  pl.BlockSpec((CONV_W, tn), lambda j, i: (0, j)),
                  pl.BlockSpec((CONV_W, tn), lambda j, i, ncol=ncol: (0, j + ncol))],
        out_specs=pl.BlockSpec((tm, tn), lambda j, i: (i, j)),
        out_shape=jax.ShapeDtypeStruct((n, dff), BF16),
        scratch_shapes=([pltpu.VMEM((SUBLANES, tn), F32)] * 2
                        + [pltpu.VMEM((sub + SUBLANES, tn), F32)] * (2 * (tm // sub))),
        compiler_params=_cparams(("arbitrary", "arbitrary")),
        name="ffn_up_conv_gate",
    )(xb, w_up, w_up, conv_w, conv_w)


def _gelu_tanh(x):
    return 0.5 * x * (1.0 + jnp.tanh(math.sqrt(2.0 / math.pi) * (x + 0.044715 * (x * x * x))))


def _group_mean(v, avg):
    hi = v.astype(BF16)
    lo = (v - hi.astype(F32)).astype(BF16)
    return _dot(hi, avg) + _dot(lo, avg)


def _odd_mixer_kernel(x_ref, w_ref, sg_ref, sw_ref, sb_ref, cw_ref, wo_ref, xf_ref, g_ref, b_ref,
                      xo_ref, xb_ref, o_ref, tail_ref, *, tiles_per_seq, alpha):
    tm = x_ref.shape[0]
    W = MIX_HALF
    first = (pl.program_id(0) % tiles_per_seq) == 0
    x = x_ref[...]
    lane = lax.broadcasted_iota(jnp.int32, (CHUNK, LANES), 1)
    lo_half = lane < HEAD_DIM
    r = lax.broadcasted_iota(jnp.int32, (W, W), 0) // HEAD_DIM
    c = lax.broadcasted_iota(jnp.int32, (W, W), 1) // HEAD_DIM
    avg = jnp.where(r == c, 1.0 / HEAD_DIM, 0.0).astype(BF16)
    tri = (lax.broadcasted_iota(jnp.int32, (CHUNK, CHUNK), 1)
           <= lax.broadcasted_iota(jnp.int32, (CHUNK, CHUNK), 0))

    u = _gelu_tanh(_dot(x, w_ref[:, 0:W]))
    v = _gelu_tanh(_dot(x, w_ref[:, W:2 * W]))
    vc = v - _group_mean(v, avg)
    var = _group_mean(vc * vc, avg)
    vn = (vc * lax.rsqrt(var + LN_EPS) * sg_ref[...]).astype(BF16)
    for n in range(tm // CHUNK):
        rows = slice(n * CHUNK, (n + 1) * CHUNK)
        for p in range(N_HEADS // 2):
            cols = slice(p * LANES, (p + 1) * LANES)
            vp = vn[rows, cols]
            w_e = jnp.where(tri, sw_ref[2 * p], 0.0).astype(BF16)
            w_o = jnp.where(tri, sw_ref[2 * p + 1], 0.0).astype(BF16)
            mix = (_dot(w_e, jnp.where(lo_half, vp, 0)) + _dot(w_o, jnp.where(lo_half, 0, vp))
                   + sb_ref[:, cols])
            o_ref[rows, cols] = (u[rows, cols] * mix).astype(o_ref.dtype)

    g_b = _dot(x, w_ref[:, 2 * W:3 * W])
    y = _dot(x, w_ref[:, 3 * W:4 * W]) * _dot(x, w_ref[:, 4 * W:5 * W])
    conv = _causal_conv3(y, jnp.where(first, 0.0, tail_ref[...]), cw_ref[...])
    tail_ref[...] = y[tm - SUBLANES:tm, :]
    o_ref[:, W:2 * W] = (g_b * conv).astype(o_ref.dtype)

    out = _layer_norm_rows(alpha * xf_ref[...] + _dot(o_ref[...], wo_ref[...]), g_ref[...], b_ref[...])
    xo_ref[...] = out
    xb_ref[...] = out.astype(BF16)


def _odd_mixer(xb, w_in, sgu_g, sgu_w, sgu_b_tile, conv_w, w_o, xf, g, b, alpha, seq, tm):
    n, k = xb.shape
    W = MIX_HALF
    d = w_o.shape[1]
    const2 = lambda shape: pl.BlockSpec(shape, lambda i: (0, 0))
    row_tile = lambda width: pl.BlockSpec((tm, width), lambda i: (i, 0))
    return pl.pallas_call(
        functools.partial(_odd_mixer_kernel, tiles_per_seq=seq // tm, alpha=alpha),
        grid=(n // tm,),
        in_specs=[row_tile(k), const2((k, 5 * W)), const2((1, W)),
                  pl.BlockSpec((N_HEADS, CHUNK, CHUNK), lambda i: (0, 0, 0)),
                  const2((CHUNK, W)), const2((CONV_W, W)),
                  const2((2 * W, d)), row_tile(d), const2((1, d)), const2((1, d))],
        out_specs=[row_tile(d), row_tile(d)],
        out_shape=[jax.ShapeDtypeStruct((n, d), F32), jax.ShapeDtypeStruct((n, d), BF16)],
        scratch_shapes=[pltpu.VMEM((tm, 2 * W), BF16),
                        pltpu.VMEM((SUBLANES, W), F32)],
        compiler_params=_cparams(("arbitrary",)),
        name="odd_mixer",
    )(xb, w_in, sgu_g, sgu_w, sgu_b_tile, conv_w, w_o, xf, g, b)


def _t5_bucket_np(dist):
    max_exact = N_BUCKETS // 2
    n = np.maximum(dist, 0)
    nf = np.maximum(n, 1).astype(np.float32)
    large = max_exact + (np.log(nf / max_exact) / np.float32(math.log(T5_MAX_DIST / max_exact))
                         * (N_BUCKETS - max_exact)).astype(np.int32)
    large = np.minimum(large, N_BUCKETS - 1)
    return np.where(n < max_exact, n, large)


def _bias_tables(rel_bias):
    T = BLK
    s = np.arange(T)[:, None]
    t = np.arange(T)[None, :]
    bucket = _t5_bucket_np(np.stack([t + d * T - s for d in range(3)]))
    far_bucket = N_BUCKETS - 1
    assert (bucket[2] == far_bucket).all()
    assert (_t5_bucket_np(np.arange(T + 1, 64 * T)) == far_bucket).all()
    rb = rel_bias.astype(F32) * LOG2E
    tiles = sum(jnp.where(jnp.asarray(bucket == b)[..., None], rb[b], 0.0) for b in range(N_BUCKETS))
    tiles = jnp.swapaxes(tiles, 2, 3).reshape(3, T, N_HEADS * T)
    far = jnp.tile(jnp.repeat(rb[far_bucket], T)[None, :], (1, 2))
    return tiles, far


def _placement_constants():
    H, L = N_HEADS, LANES
    pq = np.zeros((3 * L, H * L), np.float32)
    cq = np.zeros((1, H * L), np.float32)
    pk = np.zeros((3 * L, (H // 2) * L), np.float32)
    ck = np.zeros((1, (H // 2) * L), np.float32)
    for h in range(H):
        p, e = divmod(h, 2)
        for piece in range(3):
            pq[piece * L + h, h * L + 6 + piece] = 1.0
            cq[0, h * L + 3 * e + piece] = 1.0
            pk[piece * L + h, p * L + 3 * e + piece] = -1.0
            ck[0, p * L + 6 + piece] = 1.0
    return jnp.asarray(pq, BF16), jnp.asarray(cq), jnp.asarray(pk, BF16), jnp.asarray(ck)


def _even_weights(w_in, w_uk, w_uv):
    hd, H = HEAD_DIM, N_HEADS
    widths = (H * hd, KV_RANK, H * hd, hd, H, H * hd, H * hd, H * hd, H)
    offs = np.cumsum((0,) + widths)
    seg = lambda s: w_in[:, offs[s]:offs[s + 1]]
    q_a, c_kv, q_idx, k_idx, w_idx, q_b, k_b, v_b, f_in = [seg(s) for s in range(9)]
    scale = hd ** -0.5
    w_big = jnp.concatenate([q_a, q_idx * scale, q_b * (scale * LOG2E), k_b, v_b, c_kv,
                             k_idx, k_idx], axis=1).astype(BF16)
    pad = jnp.zeros((w_in.shape[0], LANES - 2 * H), w_in.dtype)
    w_small = jnp.concatenate([w_idx, f_in, pad], axis=1).astype(BF16)
    wuk_t = jnp.swapaxes(w_uk, 1, 2) * (scale * LOG2E)
    odd = (np.arange(H) % 2 == 1)[:, None, None]
    zk, zv = jnp.zeros_like(wuk_t), jnp.zeros_like(w_uv)
    wuk_pad = jnp.where(odd, jnp.concatenate([zk, wuk_t], axis=1), jnp.concatenate([wuk_t, zk], axis=1))
    wuv_pad = jnp.where(odd, jnp.concatenate([zv, w_uv], axis=2), jnp.concatenate([w_uv, zv], axis=2))
    return w_big, w_small, wuk_pad.astype(BF16), wuv_pad.astype(BF16)


def kernel(x, ln_g, ln_b, rel_bias, ev_w_in, ev_w_uk, ev_w_uv, ev_b_f, ev_w_o,
           od_w_in, od_sgu_g, od_sgu_w, od_sgu_b, od_conv_w, od_w_o,
           ffn_w_up, ffn_conv_w, ffn_w_down):
    bsz, seq, d = x.shape
    depth = ln_g.shape[0]
    n = bsz * seq
    alpha = (2.0 * depth) ** 0.25
    k_sel = min(TOPK_MAX, seq // 4)
    H = N_HEADS
    assert seq % QUERIES == 0 and seq % CHUNK == 0
    tm_ffn = min(1024, seq)

    bias_tiles, bias_far = _bias_tables(rel_bias)
    place = _placement_constants()
    xf = x.reshape(n, d)
    xb = xf
    for layer in range(depth):
        j = layer // 2
        g0, b0 = ln_g[layer, 0][None, :], ln_b[layer, 0][None, :]
        g1, b1 = ln_g[layer, 1][None, :], ln_b[layer, 1][None, :]
        if layer % 2 == 0:
            w_big, w_small, wuk_pad, wuv_pad = _even_weights(ev_w_in[j], ev_w_uk[j], ev_w_uv[j])
            hb, hs = _even_proj(xb, w_big, w_small, tm_ffn)
            hb = hb.reshape(bsz, seq, -1)
            hs_t = jnp.swapaxes(hs[0:2 * H].reshape(2 * H, bsz, seq), 0, 1)
            f_rows = _cumf(hs_t[:, H:2 * H], ev_b_f[j][:, None])
            mix = _attention(hb, hs_t[:, 0:H], f_rows, wuk_pad, wuv_pad, bias_tiles, bias_far,
                             place, k_sel)
            xf, xb = _proj_ln(mix.reshape(n, d), ev_w_o[j].astype(BF16), xf, g0, b0, alpha, tm_ffn)
        else:
            sgu_b_tile = jnp.repeat(jnp.swapaxes(od_sgu_b[j], 0, 1), HEAD_DIM, axis=1)
            xf, xb = _odd_mixer(xb, od_w_in[j].astype(BF16), od_sgu_g[j].reshape(1, -1), od_sgu_w[j],
                                sgu_b_tile, od_conv_w[j], od_w_o[j].astype(BF16), xf, g0, b0, alpha,
                                seq, min(1024, seq))
        act = _ffn_up(xb, ffn_w_up[layer].astype(BF16), ffn_conv_w[layer], seq, min(4096, seq), 256,
                      min(512, seq))
        xf, xb = _proj_ln(act, ffn_w_down[layer].astype(BF16), xf, g1, b1, alpha, tm_ffn)
    return xf.reshape(bsz, seq, d)
```

```python
import functools
import math

import numpy as np
import jax
import jax.numpy as jnp
from jax import lax
from jax.experimental import pallas as pl
from jax.experimental.pallas import tpu as pltpu

F32 = jnp.float32
BF16 = jnp.bfloat16

HEAD_DIM = 64
N_HEADS = 8
KV_RANK = 128
TOPK_MAX = 256
N_BUCKETS = 32
T5_MAX_DIST = 128
CHUNK = 128
CONV_W = 3
LN_EPS = 1e-5
MIX_HALF = N_HEADS * HEAD_DIM

LANES = 128
SUBLANES = 8
BF16_ROWS = 16
VMEM_LIMIT = 56 * 1024 * 1024

BLK = 128
QUERIES = 2 * BLK
PV_ROWS = KV_RANK + BF16_ROWS
NEG_BIG = -1e30
MASKED = -(2.0 ** 100)
F32_MAX = float(np.finfo(np.float32).max)
LOG2E = math.log2(math.e)
SEARCH_UNROLL = 3
LOOKAHEAD = 2
N_SLOTS = LOOKAHEAD + 1


def _cparams(sem):
    return pltpu.CompilerParams(dimension_semantics=sem, vmem_limit_bytes=VMEM_LIMIT)


def _dot(a, b):
    return jnp.dot(a, b, preferred_element_type=F32)


def _dot_nt(a, b):
    return lax.dot_general(a, b, (((1,), (1,)), ((), ())), preferred_element_type=F32)


def _layer_norm_rows(y, g, b):
    mu = jnp.mean(y, axis=-1, keepdims=True)
    yc = y - mu
    var = jnp.mean(yc * yc, axis=-1, keepdims=True)
    return yc * lax.rsqrt(var + LN_EPS) * g + b


def _tree(op, x):
    while x.shape[0] > 1:
        half = x.shape[0] // 2
        x = op(x[:half], x[half:])
    return x[0]


def _bf16_pieces(v):
    hi = v.astype(BF16)
    r1 = v - hi.astype(F32)
    mid = r1.astype(BF16)
    lo = (r1 - mid.astype(F32)).astype(BF16)
    return jnp.concatenate([hi, mid, lo], axis=1)


def _even_proj_kernel(x_ref, wb_ref, ws_ref, hb_ref, hs_ref):
    x = x_ref[...].astype(BF16)
    hb_ref[...] = _dot(x, wb_ref[...]).astype(hb_ref.dtype)
    hs_ref[...] = _dot(x, ws_ref[...]).T


def _even_proj(xb, w_big, w_small, tm):
    n, k = xb.shape
    mb = w_big.shape[1]
    ms = w_small.shape[1]
    return pl.pallas_call(
        _even_proj_kernel,
        grid=(n // tm,),
        in_specs=[pl.BlockSpec((tm, k), lambda i: (i, 0)),
                  pl.BlockSpec((k, mb), lambda i: (0, 0)),
                  pl.BlockSpec((k, ms), lambda i: (0, 0))],
        out_specs=[pl.BlockSpec((tm, mb), lambda i: (i, 0)),
                   pl.BlockSpec((ms, tm), lambda i: (0, i))],
        out_shape=[jax.ShapeDtypeStruct((n, mb), BF16),
                   jax.ShapeDtypeStruct((ms, n), F32)],
        compiler_params=_cparams(("parallel",)),
        name="even_proj",
    )(xb, w_big, w_small)


def _cumf_kernel(f_ref, b_ref, o_ref):
    seq = f_ref.shape[-1]
    row = lax.broadcasted_iota(jnp.int32, (LANES, LANES), 0)
    col = lax.broadcasted_iota(jnp.int32, (LANES, LANES), 1)
    upper = jnp.where(row <= col, 1.0, 0.0).astype(F32)
    nh = f_ref.shape[0]
    carry = jnp.zeros((nh, 1), F32)
    pad_rows = jnp.zeros((LANES - nh, LANES), F32)
    for c in range(seq // LANES):
        z = f_ref[:, c * LANES:(c + 1) * LANES] + b_ref[...]
        lf = jnp.minimum(z, 0.0) - jnp.log1p(jnp.exp(-jnp.abs(z)))
        cs = jnp.dot(lf, upper, preferred_element_type=F32,
                     precision=lax.Precision.HIGHEST) + carry
        o_ref[c * LANES:(c + 1) * LANES, :] = jnp.concatenate([cs, pad_rows], axis=0).T
        carry = cs[:, LANES - 1:LANES]


def _cumf(f_t, b_f):
    bsz, nh, seq = f_t.shape
    return pl.pallas_call(
        _cumf_kernel,
        grid=(bsz,),
        in_specs=[pl.BlockSpec((None, nh, seq), lambda b: (b, 0, 0)),
                  pl.BlockSpec((nh, 1), lambda b: (0, 0))],
        out_specs=pl.BlockSpec((None, seq, LANES), lambda b: (b, 0, 0)),
        out_shape=jax.ShapeDtypeStruct((bsz, seq, LANES), F32),
        compiler_params=_cparams(("parallel",)),
        name="fox_cumf",
    )(f_t, b_f)


def _attn_kernel(qa_ref, qi_ref, qb_ref, wt_ref, fq_ref, kb_ref, vb_ref, ckv_ref, kidx_ref, fk_ref,
                 wuk_ref, wuv_ref, bias_ref, bfar_ref, pq_ref, cq_ref, pk_ref, ck_ref,
                 o_ref,
                 score_ref, lhsd_ref, ct_ref, kaug_ref, vt_ref, qsel_ref, qaug_ref, qim_ref,
                 m_ref, acc_ref, work_ref, *, k_sel):
    T = BLK
    Q = QUERIES
    H = N_HEADS
    NP = H // 2
    NT = 2 * NP
    TW = 2 * T
    g = pl.program_id(1)
    i0 = 2 * g
    nblk = ct_ref.shape[0]
    lane = lax.broadcasted_iota(jnp.int32, (T, LANES), 1)
    lo_half = lane < HEAD_DIM
    ones_rows = jnp.ones((BF16_ROWS, T), BF16)
    causal_qq = (lax.broadcasted_iota(jnp.int32, (Q, Q), 0) <= lax.broadcasted_iota(jnp.int32, (Q, Q), 1))

    @pl.when(g == 0)
    def _per_batch():
        def prep(j, c):
            r = pl.ds(pl.multiple_of(j * T, T), T)
            cb = ckv_ref[r, :]
            lhsd_ref[0, r, 0:KV_RANK] = cb
            lhsd_ref[1, r, 0:KV_RANK] = cb
            ct_ref[j, 0:KV_RANK, :] = cb.astype(F32).T.astype(BF16)
            ct_ref[j, KV_RANK:PV_ROWS, :] = ones_rows
            ext = (_dot(_bf16_pieces(fk_ref[r, :] * LOG2E), pk_ref[...]) + ck_ref[...]).astype(BF16)
            for p in range(NP):
                cols = slice(p * LANES, (p + 1) * LANES)
                kaug_ref[p, r, 0:LANES] = kb_ref[r, cols]
                kaug_ref[p, r, LANES:2 * LANES] = ext[:, cols]
                vt_ref[p, j, 0:LANES, :] = vb_ref[r, cols].astype(F32).T.astype(BF16)
                vt_ref[p, j, LANES:PV_ROWS, :] = ones_rows
            return c
        lax.fori_loop(0, nblk, prep, 0)

    w_rows = wt_ref[...] * (H ** -0.5)
    ext_q = (_dot(_bf16_pieces(fq_ref[...] * LOG2E), pq_ref[...]) + cq_ref[...]).astype(BF16)
    eye = jnp.where(lax.broadcasted_iota(jnp.int32, (T, T), 0) == lax.broadcasted_iota(jnp.int32, (T, T), 1),
                    1.0, 0.0).astype(BF16)
    for tt in range(NT):
        hf, p = divmod(tt, NP)
        qrows = slice(hf * T, (hf + 1) * T)
        cols = slice(p * LANES, (p + 1) * LANES)
        qa_pair, qi_pair, qb_pair = qa_ref[qrows, cols], qi_ref[qrows, cols], qb_ref[qrows, cols]
        for e in range(2):
            h = 2 * p + e
            rows = slice(e * T, (e + 1) * T)
            mine = lo_half if e == 0 else jnp.logical_not(lo_half)
            qsel_ref[tt, rows, 0:KV_RANK] = _dot(qa_pair, wuk_ref[h]).astype(BF16)
            qsel_ref[tt, rows, KV_RANK:2 * KV_RANK] = eye
            qaug_ref[tt, rows, 0:LANES] = jnp.where(mine, qb_pair, 0)
            qaug_ref[tt, rows, LANES:2 * LANES] = ext_q[qrows, h * LANES:(h + 1) * LANES]
            qim_ref[tt, rows, :] = jnp.where(mine, qi_pair, 0)

    def reset_state():
        m_ref[...] = jnp.full(m_ref.shape, NEG_BIG, F32)
        acc_ref[...] = jnp.zeros(acc_ref.shape, F32)

    def tile_cols(tt):
        return slice(tt * TW, (tt + 1) * TW)

    def online_update(tt, s, pv_lhs):
        cols = tile_cols(tt)
        m_old = m_ref[:, cols]
        s_max = _tree(jnp.maximum, s.reshape(-1, SUBLANES, TW))
        m_new = jnp.maximum(m_old, jnp.max(s_max, axis=0, keepdims=True))
        alpha = jnp.exp2(m_old - m_new)
        pr = jnp.exp2((s - m_new[0:1, :]).astype(BF16))
        acc_ref[:, cols] = acc_ref[:, cols] * alpha[0:1, :] + _dot(pv_lhs, pr)
        m_ref[:, cols] = m_new

    def key_rows(j, nblocks):
        return pl.ds(pl.multiple_of(j * T, T), nblocks * T)

    def blocks_t(load, j, nblocks):
        return load(j) if nblocks == 1 else jnp.concatenate([load(j + b) for b in range(nblocks)], axis=1)

    def slot_tile(slot, tt, nblocks):
        return (slot * NT + tt, slice(0, nblocks * T), slice(None))

    def run_staged(steps):
        for k in range(min(LOOKAHEAD, len(steps))):
            steps[k][0](k % N_SLOTS)
        for k, (_, absorb_k) in enumerate(steps):
            if k + LOOKAHEAD < len(steps):
                steps[k + LOOKAHEAD][0]((k + LOOKAHEAD) % N_SLOTS)
            absorb_k(k % N_SLOTS)

    def staged_steps(j0, n_steps, logits, absorb):
        run_staged([(functools.partial(logits, j0 + 2 * k, 2), functools.partial(absorb, j0 + 2 * k, 2))
                    for k in range(n_steps)])

    def for_far_blocks(n_far, logits, absorb):
        n_oct = lax.shift_right_logical(n_far, 3)

        def oct_body(jj, c):
            staged_steps(8 * jj, 4, logits, absorb)
            return c
        lax.fori_loop(0, n_oct, oct_body, 0)

        @pl.when((n_far & 4) != 0)
        def _quad():
            staged_steps(8 * n_oct, 2, logits, absorb)

        return (n_far & 2) != 0, 8 * n_oct + (n_far & 4)

    def finish_blocks(leftover, far_step, before_step, diag_step):
        has_pair, pair_j = leftover

        @pl.when(jnp.logical_and(g >= 1, has_pair))
        def _three():
            run_staged([far_step(pair_j), before_step, diag_step])

        @pl.when(jnp.logical_and(g >= 1, jnp.logical_not(has_pair)))
        def _two():
            run_staged([before_step, diag_step])

        @pl.when(g == 0)
        def _first():
            run_staged([diag_step])

    def idx_scores(j, nblocks):
        k_blk = kidx_ref[key_rows(j, nblocks), :]
        acc = [jnp.zeros((nblocks * T, T), F32), jnp.zeros((nblocks * T, T), F32)]
        for tt in range(NT):
            hf, p = divmod(tt, NP)
            s = _dot_nt(k_blk, qim_ref[tt])
            for e in range(2):
                h = 2 * p + e
                w_h = w_rows[h:h + 1, hf * T:(hf + 1) * T]
                acc[hf] = acc[hf] + w_h * jnp.maximum(s[:, e * T:(e + 1) * T], 0.0)
        return jnp.concatenate(acc, axis=1)

    def fox_logits(j, nblocks, slot, causal=False):
        for tt in range(NT):
            hf, p = divmod(tt, NP)
            s = _dot_nt(kaug_ref[p, key_rows(j, nblocks), :], qaug_ref[tt])
            if causal:
                ok = causal_qq[:, hf * T:(hf + 1) * T]
                s = jnp.where(jnp.concatenate([ok, ok], axis=1), s, MASKED)
            work_ref[slot_tile(slot, tt, nblocks)] = s

    def fox_absorb(j, nblocks, slot):
        for tt in range(NT):
            p = tt % NP
            online_update(tt, work_ref[slot_tile(slot, tt, nblocks)],
                          blocks_t(lambda jb: vt_ref[p, jb], j, nblocks))

    def idx_fox_logits(j, nblocks, slot):
        score_ref[pl.ds(j, nblocks)] = idx_scores(j, nblocks).reshape(nblocks, T, Q)
        fox_logits(j, nblocks, slot)

    reset_state()
    leftover = for_far_blocks(jnp.maximum(i0 - 2, 0), idx_fox_logits, fox_absorb)

    def diag_logits(slot):
        score_ref[pl.ds(i0, 2)] = jnp.where(causal_qq, idx_scores(i0, 2), -jnp.inf).reshape(2, T, Q)
        fox_logits(i0, 2, slot, causal=True)

    def fox_far_step(j):
        return (lambda slot: idx_fox_logits(j, 2, slot), lambda slot: fox_absorb(j, 2, slot))

    finish_blocks(leftover, fox_far_step, fox_far_step(i0 - 2),
                  (diag_logits, lambda slot: fox_absorb(i0, 2, slot)))

    row_lo = lax.broadcasted_iota(jnp.int32, (LANES, T), 0) < HEAD_DIM
    for tt in range(NT):
        hf, p = divmod(tt, NP)
        c0 = slice(tt * TW, tt * TW + T)
        c1 = slice(tt * TW + T, (tt + 1) * TW)
        o_t = jnp.where(row_lo, acc_ref[0:LANES, c0] / acc_ref[LANES:LANES + 1, c0],
                        acc_ref[0:LANES, c1] / acc_ref[LANES:LANES + 1, c1])
        o_ref[hf * T:(hf + 1) * T, MIX_HALF + p * LANES:MIX_HALF + (p + 1) * LANES] = (
            o_t.T.astype(o_ref.dtype))

    kf = float(k_sel)
    t_pos = (g * Q + lax.broadcasted_iota(jnp.int32, (SUBLANES, Q), 1)).astype(F32)
    searching = (t_pos + 1.0) > kf
    G2 = 2 * T // SUBLANES
    n_pair = g + 1
    key_in_pair = (lax.broadcasted_iota(jnp.int32, (G2, SUBLANES, Q), 0) * SUBLANES
                   + lax.broadcasted_iota(jnp.int32, (G2, SUBLANES, Q), 1))

    def rep(v):
        return jnp.broadcast_to(v, (SUBLANES, Q))

    def score_pair(jj):
        return score_ref[pl.ds(2 * jj, 2)].reshape(G2, SUBLANES, Q)

    def count(*pred_fns, load=score_pair):
        def body(jj, cs):
            s = load(jj)
            return tuple(c + _tree(jnp.add, jnp.where(f(s, jj), 1.0, 0.0)) for c, f in zip(cs, pred_fns))
        zero = jnp.zeros((SUBLANES, Q), F32)
        cs = lax.fori_loop(0, n_pair, body, (zero,) * len(pred_fns))
        return tuple(rep(jnp.sum(c, axis=0, keepdims=True)) for c in cs)

    def count_ge(thr):
        return count(lambda s, jj: s >= thr[None])[0]

    def select_threshold():
        def first_body(jj, carry):
            mn, mx, ge0, gt0 = carry
            s = score_pair(jj)
            return (jnp.minimum(mn, _tree(jnp.minimum, jnp.where(s == -jnp.inf, jnp.inf, s))),
                    jnp.maximum(mx, _tree(jnp.maximum, s)),
                    ge0 + _tree(jnp.add, jnp.where(s >= 0.0, 1.0, 0.0)),
                    gt0 + _tree(jnp.add, jnp.where(s > 0.0, 1.0, 0.0)))

        zero = jnp.zeros((SUBLANES, Q), F32)
        mn, mx, c_ge0, c_gt0 = lax.fori_loop(0, n_pair, first_body, (zero + jnp.inf, zero - jnp.inf, zero, zero))
        mn = rep(jnp.min(mn, axis=0, keepdims=True))
        mx = rep(jnp.max(mx, axis=0, keepdims=True))
        c_ge0 = rep(jnp.sum(c_ge0, axis=0, keepdims=True))
        c_gt0 = rep(jnp.sum(c_gt0, axis=0, keepdims=True))
        above_mx = mx + jnp.maximum(jnp.abs(mx) * 2.0 ** -20, 1e-30)
        thr_ge0 = c_ge0 >= kf
        thr_gt0 = c_gt0 >= kf
        lo0 = jnp.where(thr_ge0, 0.0, mn)
        cnt_lo0 = jnp.where(thr_ge0, c_ge0, t_pos + 1.0)
        hi0 = jnp.where(thr_gt0, above_mx, 0.0)
        cnt_hi0 = jnp.where(thr_gt0, 0.0, jnp.where(thr_ge0, c_gt0, c_ge0))
        undecided = thr_gt0 | jnp.logical_not(thr_ge0)
        active0 = jnp.where(searching & undecided & (cnt_lo0 > kf), 1.0, 0.0)

        def cond(st):
            return jnp.logical_and(st[5] > 0.0, st[6] < 2200)

        def bisect(st):
            lo, hi, cnt_lo, cnt_hi, active = st
            mid = 0.5 * lo + 0.5 * hi
            adjacent = (mid <= lo) | (mid >= hi)
            c = count_ge(mid)
            act = active > 0.0
            go_lo = act & jnp.logical_not(adjacent) & (c >= kf)
            go_hi = act & jnp.logical_not(adjacent) & (c < kf)
            lo = jnp.where(go_lo, mid, lo)
            cnt_lo = jnp.where(go_lo, c, cnt_lo)
            hi = jnp.where(go_hi, mid, hi)
            cnt_hi = jnp.where(go_hi, c, cnt_hi)
            active = jnp.where(act & jnp.logical_not(adjacent) & (cnt_lo > kf), 1.0, 0.0)
            return lo, hi, cnt_lo, cnt_hi, active

        def body(st):
            inner = st[:5]
            for _ in range(SEARCH_UNROLL):
                inner = bisect(inner)
            return inner + (jnp.max(inner[4]), st[6] + SEARCH_UNROLL)

        lo, hi, cnt_lo, cnt_hi, _, _, _ = lax.while_loop(
            cond, body, (lo0, hi0, cnt_lo0, cnt_hi0, active0, jnp.max(active0), 0))
        thr = jnp.where(searching, lo, -F32_MAX)

        tied = searching & (cnt_lo > kf)
        need = kf - cnt_hi

        @pl.when(jnp.max(jnp.where(tied, 1.0, 0.0)) > 0.0)
        def _break_ties():
            def tie_pair(jj):
                return work_ref[jj].reshape(G2, SUBLANES, Q)

            def mark_body(jj, c):
                key_index = (2 * jj * T + key_in_pair).astype(F32)
                work_ref[jj] = (
                    jnp.where(score_pair(jj) == thr[None], key_index, jnp.inf).reshape(2 * T, Q))
                return c
            lax.fori_loop(0, n_pair, mark_body, 0)

            def count_tied_upto(jcut):
                return count(lambda e, jj: e <= jcut[None], load=tie_pair)[0]

            def bs_body(_, st):
                jl, jh = st
                jm = jnp.floor(0.5 * (jl + jh))
                ok = count_tied_upto(jm) >= need
                return jnp.where(ok, jl, jm), jnp.where(ok, jm, jh)

            n_keys = (2 * n_pair * T).astype(F32)
            jl0 = jnp.full((SUBLANES, Q), -1.0, F32)
            jh0 = jnp.zeros((SUBLANES, Q), F32) + (n_keys - 1.0)
            n_steps = int(math.ceil(math.log2(nblk * T))) + 1
            _, jcut = lax.fori_loop(0, n_steps, bs_body, (jl0, jh0))

            def drop_body(jj, c):
                e = tie_pair(jj)
                drop = tied[None] & (e > jcut[None]) & (e < jnp.inf)
                score_ref[pl.ds(2 * jj, 2)] = jnp.where(drop, -jnp.inf, score_pair(jj)).reshape(2, T, Q)
                return c
            lax.fori_loop(0, n_pair, drop_body, 0)

        return thr

    thr = lax.cond((i0 + 2) * T > k_sel, select_threshold,
                   lambda: jnp.full((SUBLANES, Q), -F32_MAX, F32))

    def mask_body(jj, c):
        add = jnp.where(score_pair(jj) >= thr[None], 0.0, MASKED).reshape(2 * T, Q).astype(BF16)
        for hf in range(2):
            lhsd_ref[hf, key_rows(2 * jj, 2), KV_RANK:2 * KV_RANK] = add[:, hf * T:(hf + 1) * T]
        return c

    lax.fori_loop(0, n_pair, mask_body, 0)

    def dsa_logits(j, nblocks, slot, bias=None):
        for tt in range(NT):
            hf, p = divmod(tt, NP)
            s = _dot_nt(lhsd_ref[hf, key_rows(j, nblocks), :], qsel_ref[tt])
            if bias is not None:
                s = s + bias(hf, slice(p * TW, (p + 1) * TW))
            work_ref[slot_tile(slot, tt, nblocks)] = s

    def dsa_absorb(j, nblocks, slot):
        pv_lhs = blocks_t(lambda jb: ct_ref[jb], j, nblocks)
        for tt in range(NT):
            online_update(tt, work_ref[slot_tile(slot, tt, nblocks)], pv_lhs)

    def bias_before(hf, cols):
        return jnp.concatenate([bias_ref[2, :, cols], bias_ref[1 + hf, :, cols]], axis=0)

    def bias_diag(hf, cols):
        first = bias_ref[hf, :, cols]
        return jnp.concatenate([first, bias_ref[0, :, cols]], axis=0)

    reset_state()
    leftover = for_far_blocks(jnp.maximum(i0 - 2, 0), dsa_logits, dsa_absorb)

    def dsa_far_step(j):
        return (lambda slot: dsa_logits(j, 2, slot), lambda slot: dsa_absorb(j, 2, slot))

    def absorb_first_biased(j):
        def absorb(slot):
            m_ref[...] = m_ref[...] + bfar_ref[...]
            dsa_absorb(j, 2, slot)
        return absorb

    diag_step = lambda absorb: (lambda slot: dsa_logits(i0, 2, slot, bias_diag), absorb)
    before_step = (lambda slot: dsa_logits(i0 - 2, 2, slot, bias_before), absorb_first_biased(i0 - 2))
    has_pair, pair_j = leftover

    @pl.when(jnp.logical_and(g >= 1, has_pair))
    def _dsa_three():
        run_staged([dsa_far_step(pair_j), before_step, diag_step(lambda slot: dsa_absorb(i0, 2, slot))])

    @pl.when(jnp.logical_and(g >= 1, jnp.logical_not(has_pair)))
    def _dsa_two():
        run_staged([before_step, diag_step(lambda slot: dsa_absorb(i0, 2, slot))])

    @pl.when(g == 0)
    def _dsa_first():
        run_staged([diag_step(absorb_first_biased(0))])

    for tt in range(NT):
        hf, p = divmod(tt, NP)
        o_pair = jnp.zeros((T, LANES), F32)
        for e in range(2):
            cols = slice(tt * TW + e * T, tt * TW + (e + 1) * T)
            o_t = acc_ref[0:KV_RANK, cols] / acc_ref[KV_RANK:KV_RANK + 1, cols]
            o_pair = o_pair + _dot(o_t.T.astype(BF16), wuv_ref[2 * p + e])
        o_ref[hf * T:(hf + 1) * T, p * LANES:(p + 1) * LANES] = o_pair.astype(o_ref.dtype)


_EV_QA, _EV_QI, _EV_QB, _EV_KB, _EV_VB = 0, 1, 2, 3, 4
_EV_CKV, _EV_KIDX = 20, 21


def _attention(hb, w_t, f_rows, wuk_pad, wuv_pad, bias_tiles, bias_far, place, k_sel):
    bsz, seq, _ = hb.shape
    T, Q, H = BLK, QUERIES, N_HEADS
    nblk = seq // T
    lanes = 2 * H * T
    pq, cq, pk, ck = place
    wide = lambda c: pl.BlockSpec((None, Q, MIX_HALF), lambda b, i, c=c: (b, i, c))
    once = dict(pipeline_mode=pl.Buffered(1))
    full_wide = lambda c: pl.BlockSpec((None, seq, MIX_HALF), lambda b, i, c=c: (b, 0, c), **once)
    full_narrow = lambda c: pl.BlockSpec((None, seq, LANES), lambda b, i, c=c: (b, 0, c), **once)
    const = lambda a: pl.BlockSpec(a.shape, lambda b, i, nd=a.ndim: (0,) * nd, **once)
    return pl.pallas_call(
        functools.partial(_attn_kernel, k_sel=k_sel),
        grid=(bsz, seq // Q),
        in_specs=[wide(_EV_QA), wide(_EV_QI), wide(_EV_QB),
                  pl.BlockSpec((None, H, Q), lambda b, i: (b, 0, i)),
                  pl.BlockSpec((None, Q, LANES), lambda b, i: (b, i, 0)),
                  full_wide(_EV_KB), full_wide(_EV_VB),
                  full_narrow(_EV_CKV), full_narrow(_EV_KIDX),
                  pl.BlockSpec((None, seq, LANES), lambda b, i: (b, 0, 0), **once),
                  const(wuk_pad), const(wuv_pad), const(bias_tiles), const(bias_far),
                  const(pq), const(cq), const(pk), const(ck)],
        out_specs=pl.BlockSpec((None, Q, 2 * MIX_HALF), lambda b, i: (b, i, 0)),
        out_shape=jax.ShapeDtypeStruct((bsz, seq, 2 * MIX_HALF), BF16),
        scratch_shapes=[pltpu.VMEM((nblk, T, Q), F32),
                        pltpu.VMEM((2, seq, 2 * KV_RANK), BF16),
                        pltpu.VMEM((nblk, PV_ROWS, T), BF16),
                        pltpu.VMEM((H // 2, seq, 2 * LANES), BF16),
                        pltpu.VMEM((H // 2, nblk, PV_ROWS, T), BF16),
                        pltpu.VMEM((H, 2 * T, 2 * KV_RANK), BF16),
                        pltpu.VMEM((H, 2 * T, 2 * LANES), BF16),
                        pltpu.VMEM((H, 2 * T, LANES), BF16),
                        pltpu.VMEM((SUBLANES, lanes), F32),
                        pltpu.VMEM((PV_ROWS, lanes), F32),
                        pltpu.VMEM((max(N_SLOTS * H, nblk // 2), 2 * T, 2 * T), F32)],
        compiler_params=_cparams(("parallel", "arbitrary")),
        name="dsa_fox_attention",
    )(hb, hb, hb, w_t, f_rows, hb, hb, hb, hb, f_rows, wuk_pad, wuv_pad, bias_tiles, bias_far,
      pq, cq, pk, ck)


def _proj_ln_kernel(a_ref, w_ref, x_ref, g_ref, b_ref, xo_ref, xb_ref, *, alpha, sub):
    for c in range(a_ref.shape[0] // sub):
        rows = slice(c * sub, (c + 1) * sub)
        y = alpha * x_ref[rows, :] + _dot(a_ref[rows, :], w_ref[...])
        out = _layer_norm_rows(y, g_ref[...], b_ref[...])
        xo_ref[rows, :] = out
        xb_ref[rows, :] = out.astype(BF16)


def _proj_ln(a, w, layer, x, g, b, alpha, tm):
    n, k = a.shape
    d = w.shape[2]
    return pl.pallas_call(
        functools.partial(_proj_ln_kernel, alpha=alpha, sub=min(128, tm)),
        grid=(n // tm,),
        in_specs=[pl.BlockSpec((tm, k), lambda i: (i, 0)),
                  pl.BlockSpec((None, k, d), lambda i: (layer, 0, 0)),
                  pl.BlockSpec((tm, d), lambda i: (i, 0)),
                  pl.BlockSpec((1, d), lambda i: (0, 0)),
                  pl.BlockSpec((1, d), lambda i: (0, 0))],
        out_specs=[pl.BlockSpec((tm, d), lambda i: (i, 0)),
                   pl.BlockSpec((tm, d), lambda i: (i, 0))],
        out_shape=[jax.ShapeDtypeStruct((n, d), F32),
                   jax.ShapeDtypeStruct((n, d), BF16)],
        compiler_params=_cparams(("parallel",)),
        name="proj_residual_ln",
    )(a, w, x, g, b)


def _causal_conv3(h, tail, cw):
    w0, w1, w2 = cw[0:1, :], cw[1:2, :], cw[2:3, :]
    body = w0 * pltpu.roll(h, 2, axis=0) + w1 * pltpu.roll(h, 1, axis=0) + w2 * h
    head = h[0:SUBLANES, :]
    r = lax.broadcasted_iota(jnp.int32, head.shape, 0)
    hm1 = jnp.where(r < 1, pltpu.roll(tail, 1, axis=0), pltpu.roll(head, 1, axis=0))
    hm2 = jnp.where(r < 2, pltpu.roll(tail, 2, axis=0), pltpu.roll(head, 2, axis=0))
    out_head = w0 * hm2 + w1 * hm1 + w2 * head
    return jnp.concatenate([out_head, body[SUBLANES:, :]], axis=0)


def _ffn_up_kernel(x_ref, wg_ref, wv_ref, cg_ref, cv_ref, o_ref, tg_ref, tv_ref, *h_refs,
                   tiles_per_seq, sub):
    n_chunks = x_ref.shape[0] // sub
    hg_refs, hv_refs = h_refs[:n_chunks], h_refs[n_chunks:]
    first = (pl.program_id(1) % tiles_per_seq) == 0
    body_rows = slice(SUBLANES, SUBLANES + sub)

    def matmuls(c):
        x = x_ref[c * sub:(c + 1) * sub, :]
        hg_refs[c][body_rows, :] = _dot(x, wg_ref[...])
        hv_refs[c][body_rows, :] = _dot(x, wv_ref[...])

    def conv(h_ref, cw):
        return (cw[0:1, :] * h_ref[SUBLANES - 2:SUBLANES - 2 + sub, :]
                + cw[1:2, :] * h_ref[SUBLANES - 1:SUBLANES - 1 + sub, :]
                + cw[2:3, :] * h_ref[body_rows, :])

    def gate(c):
        if c == 0:
            hg_refs[0][0:SUBLANES, :] = jnp.where(first, 0.0, tg_ref[...])
            hv_refs[0][0:SUBLANES, :] = jnp.where(first, 0.0, tv_ref[...])
        else:
            hg_refs[c][0:SUBLANES, :] = hg_refs[c - 1][sub:sub + SUBLANES, :]
            hv_refs[c][0:SUBLANES, :] = hv_refs[c - 1][sub:sub + SUBLANES, :]
        g = conv(hg_refs[c], cg_ref[...])
        v = conv(hv_refs[c], cv_ref[...])
        half_g = 0.5 * g
        o_ref[c * sub:(c + 1) * sub, :] = ((half_g + half_g * jnp.tanh(half_g)) * v).astype(o_ref.dtype)

    matmuls(0)
    for c in range(n_chunks):
        if c + 1 < n_chunks:
            matmuls(c + 1)
        gate(c)
    tg_ref[...] = hg_refs[n_chunks - 1][sub:sub + SUBLANES, :]
    tv_ref[...] = hv_refs[n_chunks - 1][sub:sub + SUBLANES, :]


def _ffn_up(xb, w_up, conv_w, layer, seq, tm, tn, sub):
    n, k = xb.shape
    dff = w_up.shape[2] // 2
    ncol = dff // tn
    return pl.pallas_call(
        functools.partial(_ffn_up_kernel, tiles_per_seq=seq // tm, sub=sub),
        grid=(ncol, n // tm),
        in_specs=[pl.BlockSpec((tm, k), lambda j, i: (i, 0)),
                  pl.BlockSpec((None, k, tn), lambda j, i: (layer, 0, j)),
                  pl.BlockSpec((None, k, tn), lambda j, i, ncol=ncol: (layer, 0, j + ncol)),
                  pl.BlockSpec((None, CONV_W, tn), lambda j, i: (layer, 0, j)),
                  pl.BlockSpec((None, CONV_W, tn), lambda j, i, ncol=ncol: (layer, 0, j + ncol))],
        out_specs=pl.BlockSpec((tm, tn), lambda j, i: (i, j)),
        out_shape=jax.ShapeDtypeStruct((n, dff), BF16),
        scratch_shapes=([pltpu.VMEM((SUBLANES, tn), F32)] * 2
                        + [pltpu.VMEM((sub + SUBLANES, tn), F32)] * (2 * (tm // sub))),
        compiler_params=_cparams(("arbitrary", "arbitrary")),
        name="ffn_up_conv_gate",
    )(xb, w_up, w_up, conv_w, conv_w)


def _gelu_tanh(x):
    return 0.5 * x * (1.0 + jnp.tanh(math.sqrt(2.0 / math.pi) * (x + 0.044715 * (x * x * x))))


def _group_mean(v, avg):
    hi = v.astype(BF16)
    lo = (v - hi.astype(F32)).astype(BF16)
    return _dot(hi, avg) + _dot(lo, avg)


def _odd_mixer_kernel(x_ref, w_ref, sg_ref, sw_ref, sb_ref, cw_ref, wo_ref, xf_ref, g_ref, b_ref,
                      xo_ref, xb_ref, o_ref, tail_ref, *, tiles_per_seq, alpha):
    tm = x_ref.shape[0]
    W = MIX_HALF
    first = (pl.program_id(0) % tiles_per_seq) == 0
    x = x_ref[...]
    lane = lax.broadcasted_iota(jnp.int32, (CHUNK, LANES), 1)
    lo_half = lane < HEAD_DIM
    r = lax.broadcasted_iota(jnp.int32, (W, W), 0) // HEAD_DIM
    c = lax.broadcasted_iota(jnp.int32, (W, W), 1) // HEAD_DIM
    avg = jnp.where(r == c, 1.0 / HEAD_DIM, 0.0).astype(BF16)
    tri = (lax.broadcasted_iota(jnp.int32, (CHUNK, CHUNK), 1)
           <= lax.broadcasted_iota(jnp.int32, (CHUNK, CHUNK), 0))

    u = _gelu_tanh(_dot(x, w_ref[:, 0:W]))
    v = _gelu_tanh(_dot(x, w_ref[:, W:2 * W]))
    vc = v - _group_mean(v, avg)
    var = _group_mean(vc * vc, avg)
    vn = (vc * lax.rsqrt(var + LN_EPS) * sg_ref[...]).astype(BF16)
    for n in range(tm // CHUNK):
        rows = slice(n * CHUNK, (n + 1) * CHUNK)
        for p in range(N_HEADS // 2):
            cols = slice(p * LANES, (p + 1) * LANES)
            vp = vn[rows, cols]
            w_e = jnp.where(tri, sw_ref[2 * p], 0.0).astype(BF16)
            w_o = jnp.where(tri, sw_ref[2 * p + 1], 0.0).astype(BF16)
            mix = (_dot(w_e, jnp.where(lo_half, vp, 0)) + _dot(w_o, jnp.where(lo_half, 0, vp))
                   + sb_ref[:, cols])
            o_ref[rows, cols] = (u[rows, cols] * mix).astype(o_ref.dtype)

    g_b = _dot(x, w_ref[:, 2 * W:3 * W])
    y = _dot(x, w_ref[:, 3 * W:4 * W]) * _dot(x, w_ref[:, 4 * W:5 * W])
    conv = _causal_conv3(y, jnp.where(first, 0.0, tail_ref[...]), cw_ref[...])
    tail_ref[...] = y[tm - SUBLANES:tm, :]
    o_ref[:, W:2 * W] = (g_b * conv).astype(o_ref.dtype)

    out = _layer_norm_rows(alpha * xf_ref[...] + _dot(o_ref[...], wo_ref[...]), g_ref[...], b_ref[...])
    xo_ref[...] = out
    xb_ref[...] = out.astype(BF16)


def _odd_mixer(xb, w_in, sgu_g, sgu_w, sgu_b_tile, conv_w, w_o, xf, g, b, alpha, seq, tm):
    n, k = xb.shape
    W = MIX_HALF
    d = w_o.shape[1]
    const2 = lambda shape: pl.BlockSpec(shape, lambda i: (0, 0))
    row_tile = lambda width: pl.BlockSpec((tm, width), lambda i: (i, 0))
    return pl.pallas_call(
        functools.partial(_odd_mixer_kernel, tiles_per_seq=seq // tm, alpha=alpha),
        grid=(n // tm,),
        in_specs=[row_tile(k), const2((k, 5 * W)), const2((1, W)),
                  pl.BlockSpec((N_HEADS, CHUNK, CHUNK), lambda i: (0, 0, 0)),
                  const2((CHUNK, W)), const2((CONV_W, W)),
                  const2((2 * W, d)), row_tile(d), const2((1, d)), const2((1, d))],
        out_specs=[row_tile(d), row_tile(d)],
        out_shape=[jax.ShapeDtypeStruct((n, d), F32), jax.ShapeDtypeStruct((n, d), BF16)],
        scratch_shapes=[pltpu.VMEM((tm, 2 * W), BF16),
                        pltpu.VMEM((SUBLANES, W), F32)],
        compiler_params=_cparams(("arbitrary",)),
        name="odd_mixer",
    )(xb, w_in, sgu_g, sgu_w, sgu_b_tile, conv_w, w_o, xf, g, b)


def _t5_bucket_np(dist):
    max_exact = N_BUCKETS // 2
    n = np.maximum(dist, 0)
    nf = np.maximum(n, 1).astype(np.float32)
    large = max_exact + (np.log(nf / max_exact) / np.float32(math.log(T5_MAX_DIST / max_exact))
                         * (N_BUCKETS - max_exact)).astype(np.int32)
    large = np.minimum(large, N_BUCKETS - 1)
    return np.where(n < max_exact, n, large)


def _bias_tables(rel_bias):
    T = BLK
    s = np.arange(T)[:, None]
    t = np.arange(T)[None, :]
    bucket = _t5_bucket_np(np.stack([t + d * T - s for d in range(3)]))
    far_bucket = N_BUCKETS - 1
    assert (bucket[2] == far_bucket).all()
    assert (_t5_bucket_np(np.arange(T + 1, 64 * T)) == far_bucket).all()
    rb = rel_bias.astype(F32) * LOG2E
    tiles = sum(jnp.where(jnp.asarray(bucket == b)[..., None], rb[b], 0.0) for b in range(N_BUCKETS))
    tiles = jnp.swapaxes(tiles, 2, 3).reshape(3, T, N_HEADS * T)
    far = jnp.tile(jnp.repeat(rb[far_bucket], T)[None, :], (1, 2))
    return tiles, far


def _placement_constants():
    H, L = N_HEADS, LANES
    pq = np.zeros((3 * L, H * L), np.float32)
    cq = np.zeros((1, H * L), np.float32)
    pk = np.zeros((3 * L, (H // 2) * L), np.float32)
    ck = np.zeros((1, (H // 2) * L), np.float32)
    for h in range(H):
        p, e = divmod(h, 2)
        for piece in range(3):
            pq[piece * L + h, h * L + 6 + piece] = 1.0
            cq[0, h * L + 3 * e + piece] = 1.0
            pk[piece * L + h, p * L + 3 * e + piece] = -1.0
            ck[0, p * L + 6 + piece] = 1.0
    return jnp.asarray(pq, BF16), jnp.asarray(cq), jnp.asarray(pk, BF16), jnp.asarray(ck)


def _even_weights(w_in, w_uk, w_uv):
    hd, H = HEAD_DIM, N_HEADS
    widths = (H * hd, KV_RANK, H * hd, hd, H, H * hd, H * hd, H * hd, H)
    offs = np.cumsum((0,) + widths)
    seg = lambda s: w_in[:, offs[s]:offs[s + 1]]
    q_a, c_kv, q_idx, k_idx, w_idx, q_b, k_b, v_b, f_in = [seg(s) for s in range(9)]
    scale = hd ** -0.5
    w_big = jnp.concatenate([q_a, q_idx * scale, q_b * (scale * LOG2E), k_b, v_b, c_kv,
                             k_idx, k_idx], axis=1).astype(BF16)
    pad = jnp.zeros((w_in.shape[0], LANES - 2 * H), w_in.dtype)
    w_small = jnp.concatenate([w_idx, f_in, pad], axis=1).astype(BF16)
    wuk_t = jnp.swapaxes(w_uk, 1, 2) * (scale * LOG2E)
    odd = (np.arange(H) % 2 == 1)[:, None, None]
    zk, zv = jnp.zeros_like(wuk_t), jnp.zeros_like(w_uv)
    wuk_pad = jnp.where(odd, jnp.concatenate([zk, wuk_t], axis=1), jnp.concatenate([wuk_t, zk], axis=1))
    wuv_pad = jnp.where(odd, jnp.concatenate([zv, w_uv], axis=2), jnp.concatenate([w_uv, zv], axis=2))
    return w_big, w_small, wuk_pad.astype(BF16), wuv_pad.astype(BF16)


def kernel(x, ln_g, ln_b, rel_bias, ev_w_in, ev_w_uk, ev_w_uv, ev_b_f, ev_w_o,
           od_w_in, od_sgu_g, od_sgu_w, od_sgu_b, od_conv_w, od_w_o,
           ffn_w_up, ffn_conv_w, ffn_w_down):
    bsz, seq, d = x.shape
    depth = ln_g.shape[0]
    n = bsz * seq
    alpha = (2.0 * depth) ** 0.25
    k_sel = min(TOPK_MAX, seq // 4)
    H = N_HEADS
    assert seq % QUERIES == 0 and seq % CHUNK == 0
    tm_ffn = min(1024, seq)

    bias_tiles, bias_far = _bias_tables(rel_bias)
    place = _placement_constants()
    w_up_b, w_down_b, ev_w_o_b = ffn_w_up.astype(BF16), ffn_w_down.astype(BF16), ev_w_o.astype(BF16)
    xf = x.reshape(n, d)
    xb = xf
    for layer in range(depth):
        j = layer // 2
        g0, b0 = ln_g[layer, 0][None, :], ln_b[layer, 0][None, :]
        g1, b1 = ln_g[layer, 1][None, :], ln_b[layer, 1][None, :]
        if layer % 2 == 0:
            w_big, w_small, wuk_pad, wuv_pad = _even_weights(ev_w_in[j], ev_w_uk[j], ev_w_uv[j])
            hb, hs = _even_proj(xb, w_big, w_small, tm_ffn)
            hb = hb.reshape(bsz, seq, -1)
            hs_t = jnp.swapaxes(hs[0:2 * H].reshape(2 * H, bsz, seq), 0, 1)
            f_rows = _cumf(hs_t[:, H:2 * H], ev_b_f[j][:, None])
            mix = _attention(hb, hs_t[:, 0:H], f_rows, wuk_pad, wuv_pad, bias_tiles, bias_far,
                             place, k_sel)
            xf, xb = _proj_ln(mix.reshape(n, d), ev_w_o_b, j, xf, g0, b0, alpha, tm_ffn)
        else:
            sgu_b_tile = jnp.repeat(jnp.swapaxes(od_sgu_b[j], 0, 1), HEAD_DIM, axis=1)
            xf, xb = _odd_mixer(xb, od_w_in[j].astype(BF16), od_sgu_g[j].reshape(1, -1), od_sgu_w[j],
                                sgu_b_tile, od_conv_w[j], od_w_o[j].astype(BF16), xf, g0, b0, alpha,
                                seq, min(1024, seq))
        act = _ffn_up(xb, w_up_b, ffn_conv_w, layer, seq, min(4096, seq), 256, min(512, seq))
        xf, xb = _proj_ln(act, w_down_b, layer, xf, g1, b1, alpha, tm_ffn)
    return xf.reshape(bsz, seq, d)
```

```python
import functools
import math

import numpy as np
import jax
import jax.numpy as jnp
from jax import lax
from jax.experimental import pallas as pl
from jax.experimental.pallas import tpu as pltpu

F32 = jnp.float32
BF16 = jnp.bfloat16

HEAD_DIM = 64
N_HEADS = 8
KV_RANK = 128
TOPK_MAX = 256
N_BUCKETS = 32
T5_MAX_DIST = 128
CHUNK = 128
CONV_W = 3
LN_EPS = 1e-5
MIX_HALF = N_HEADS * HEAD_DIM

LANES = 128
SUBLANES = 8
BF16_ROWS = 16
VMEM_LIMIT = 56 * 1024 * 1024

BLK = 128
QUERIES = 2 * BLK
PV_ROWS = KV_RANK + BF16_ROWS
NEG_BIG = -1e30
MASKED = -(2.0 ** 100)
F32_MAX = float(np.finfo(np.float32).max)
LOG2E = math.log2(math.e)
SEARCH_UNROLL = 3
LOOKAHEAD = 2
N_SLOTS = LOOKAHEAD + 1


def _cparams(sem):
    return pltpu.CompilerParams(dimension_semantics=sem, vmem_limit_bytes=VMEM_LIMIT)


def _dot(a, b):
    return jnp.dot(a, b, preferred_element_type=F32)


def _dot_nt(a, b):
    return lax.dot_general(a, b, (((1,), (1,)), ((), ())), preferred_element_type=F32)


def _layer_norm_rows(y, g, b):
    mu = jnp.mean(y, axis=-1, keepdims=True)
    yc = y - mu
    var = jnp.mean(yc * yc, axis=-1, keepdims=True)
    return yc * lax.rsqrt(var + LN_EPS) * g + b


def _tree(op, x):
    while x.shape[0] > 1:
        half = x.shape[0] // 2
        x = op(x[:half], x[half:])
    return x[0]


def _bf16_pieces(v):
    hi = v.astype(BF16)
    r1 = v - hi.astype(F32)
    mid = r1.astype(BF16)
    lo = (r1 - mid.astype(F32)).astype(BF16)
    return jnp.concatenate([hi, mid, lo], axis=1)


def _even_proj_kernel(x_ref, wb_ref, ws_ref, hb_ref, hs_ref):
    x = x_ref[...].astype(BF16)
    hb_ref[...] = _dot(x, wb_ref[...]).astype(hb_ref.dtype)
    hs_ref[...] = _dot(x, ws_ref[...]).T


def _even_proj(xb, w_big, w_small, tm):
    n, k = xb.shape
    mb = w_big.shape[1]
    ms = w_small.shape[1]
    return pl.pallas_call(
        _even_proj_kernel,
        grid=(n // tm,),
        in_specs=[pl.BlockSpec((tm, k), lambda i: (i, 0)),
                  pl.BlockSpec((k, mb), lambda i: (0, 0)),
                  pl.BlockSpec((k, ms), lambda i: (0, 0))],
        out_specs=[pl.BlockSpec((tm, mb), lambda i: (i, 0)),
                   pl.BlockSpec((ms, tm), lambda i: (0, i))],
        out_shape=[jax.ShapeDtypeStruct((n, mb), BF16),
                   jax.ShapeDtypeStruct((ms, n), F32)],
        compiler_params=_cparams(("parallel",)),
        name="even_proj",
    )(xb, w_big, w_small)


def _cumf_kernel(f_ref, b_ref, o_ref):
    seq = f_ref.shape[-1]
    row = lax.broadcasted_iota(jnp.int32, (LANES, LANES), 0)
    col = lax.broadcasted_iota(jnp.int32, (LANES, LANES), 1)
    upper = jnp.where(row <= col, 1.0, 0.0).astype(F32)
    nh = f_ref.shape[0]
    carry = jnp.zeros((nh, 1), F32)
    pad_rows = jnp.zeros((LANES - nh, LANES), F32)
    for c in range(seq // LANES):
        z = f_ref[:, c * LANES:(c + 1) * LANES] + b_ref[...]
        lf = jnp.minimum(z, 0.0) - jnp.log1p(jnp.exp(-jnp.abs(z)))
        cs = jnp.dot(lf, upper, preferred_element_type=F32,
                     precision=lax.Precision.HIGHEST) + carry
        o_ref[c * LANES:(c + 1) * LANES, :] = jnp.concatenate([cs, pad_rows], axis=0).T
        carry = cs[:, LANES - 1:LANES]


def _cumf(f_t, b_f):
    bsz, nh, seq = f_t.shape
    return pl.pallas_call(
        _cumf_kernel,
        grid=(bsz,),
        in_specs=[pl.BlockSpec((None, nh, seq), lambda b: (b, 0, 0)),
                  pl.BlockSpec((nh, 1), lambda b: (0, 0))],
        out_specs=pl.BlockSpec((None, seq, LANES), lambda b: (b, 0, 0)),
        out_shape=jax.ShapeDtypeStruct((bsz, seq, LANES), F32),
        compiler_params=_cparams(("parallel",)),
        name="fox_cumf",
    )(f_t, b_f)


def _attn_kernel(qa_ref, qi_ref, qb_ref, wt_ref, fq_ref, kb_ref, vb_ref, ckv_ref, kidx_ref, fk_ref,
                 wuk_ref, wuv_ref, bias_ref, bfar_ref, pq_ref, cq_ref, pk_ref, ck_ref,
                 o_ref,
                 score_ref, lhsd_ref, ct_ref, kaug_ref, vt_ref, qsel_ref, qaug_ref, qim_ref,
                 m_ref, acc_ref, work_ref, *, k_sel):
    T = BLK
    Q = QUERIES
    H = N_HEADS
    NP = H // 2
    NT = 2 * NP
    TW = 2 * T
    g = pl.program_id(1)
    i0 = 2 * g
    nblk = ct_ref.shape[0]
    lane = lax.broadcasted_iota(jnp.int32, (T, LANES), 1)
    lo_half = lane < HEAD_DIM
    ones_rows = jnp.ones((BF16_ROWS, T), BF16)
    causal_qq = (lax.broadcasted_iota(jnp.int32, (Q, Q), 0) <= lax.broadcasted_iota(jnp.int32, (Q, Q), 1))

    @pl.when(g == 0)
    def _per_batch():
        def prep(j, c):
            r = pl.ds(pl.multiple_of(j * T, T), T)
            cb = ckv_ref[r, :]
            lhsd_ref[0, r, 0:KV_RANK] = cb
            lhsd_ref[1, r, 0:KV_RANK] = cb
            ct_ref[j, 0:KV_RANK, :] = cb.astype(F32).T.astype(BF16)
            ct_ref[j, KV_RANK:PV_ROWS, :] = ones_rows
            ext = (_dot(_bf16_pieces(fk_ref[r, :] * LOG2E), pk_ref[...]) + ck_ref[...]).astype(BF16)
            for p in range(NP):
                cols = slice(p * LANES, (p + 1) * LANES)
                kaug_ref[p, r, 0:LANES] = kb_ref[r, cols]
                kaug_ref[p, r, LANES:2 * LANES] = ext[:, cols]
                vt_ref[p, j, 0:LANES, :] = vb_ref[r, cols].astype(F32).T.astype(BF16)
                vt_ref[p, j, LANES:PV_ROWS, :] = ones_rows
            return c
        lax.fori_loop(0, nblk, prep, 0)

    w_rows = wt_ref[...] * (H ** -0.5)
    ext_q = (_dot(_bf16_pieces(fq_ref[...] * LOG2E), pq_ref[...]) + cq_ref[...]).astype(BF16)
    eye = jnp.where(lax.broadcasted_iota(jnp.int32, (T, T), 0) == lax.broadcasted_iota(jnp.int32, (T, T), 1),
                    1.0, 0.0).astype(BF16)
    for tt in range(NT):
        hf, p = divmod(tt, NP)
        qrows = slice(hf * T, (hf + 1) * T)
        cols = slice(p * LANES, (p + 1) * LANES)
        qa_pair, qi_pair, qb_pair = qa_ref[qrows, cols], qi_ref[qrows, cols], qb_ref[qrows, cols]
        for e in range(2):
            h = 2 * p + e
            rows = slice(e * T, (e + 1) * T)
            mine = lo_half if e == 0 else jnp.logical_not(lo_half)
            qsel_ref[tt, rows, 0:KV_RANK] = _dot(qa_pair, wuk_ref[h]).astype(BF16)
            qsel_ref[tt, rows, KV_RANK:2 * KV_RANK] = eye
            qaug_ref[tt, rows, 0:LANES] = jnp.where(mine, qb_pair, 0)
            qaug_ref[tt, rows, LANES:2 * LANES] = ext_q[qrows, h * LANES:(h + 1) * LANES]
            qim_ref[tt, rows, :] = jnp.where(mine, qi_pair, 0)

    def reset_state():
        m_ref[...] = jnp.full(m_ref.shape, NEG_BIG, F32)
        acc_ref[...] = jnp.zeros(acc_ref.shape, F32)

    def tile_cols(tt):
        return slice(tt * TW, (tt + 1) * TW)

    def online_update(tt, s, pv_lhs):
        cols = tile_cols(tt)
        m_old = m_ref[:, cols]
        s_max = _tree(jnp.maximum, s.reshape(-1, SUBLANES, TW))
        m_new = jnp.maximum(m_old, jnp.max(s_max, axis=0, keepdims=True))
        alpha = jnp.exp2(m_old - m_new)
        pr = jnp.exp2((s - m_new[0:1, :]).astype(BF16))
        acc_ref[:, cols] = acc_ref[:, cols] * alpha[0:1, :] + _dot(pv_lhs, pr)
        m_ref[:, cols] = m_new

    def key_rows(j, nblocks):
        return pl.ds(pl.multiple_of(j * T, T), nblocks * T)

    def blocks_t(load, j, nblocks):
        return load(j) if nblocks == 1 else jnp.concatenate([load(j + b) for b in range(nblocks)], axis=1)

    def slot_tile(slot, tt, nblocks):
        return (slot * NT + tt, slice(0, nblocks * T), slice(None))

    def run_staged(steps):
        for k in range(min(LOOKAHEAD, len(steps))):
            steps[k][0](k % N_SLOTS)
        for k, (_, absorb_k) in enumerate(steps):
            if k + LOOKAHEAD < len(steps):
                steps[k + LOOKAHEAD][0]((k + LOOKAHEAD) % N_SLOTS)
            absorb_k(k % N_SLOTS)

    def staged_steps(j0, n_steps, logits, absorb):
        run_staged([(functools.partial(logits, j0 + 2 * k, 2), functools.partial(absorb, j0 + 2 * k, 2))
                    for k in range(n_steps)])

    def for_far_blocks(n_far, logits, absorb):
        n_oct = lax.shift_right_logical(n_far, 3)

        def oct_body(jj, c):
            staged_steps(8 * jj, 4, logits, absorb)
            return c
        lax.fori_loop(0, n_oct, oct_body, 0)

        @pl.when((n_far & 4) != 0)
        def _quad():
            staged_steps(8 * n_oct, 2, logits, absorb)

        return (n_far & 2) != 0, 8 * n_oct + (n_far & 4)

    def finish_blocks(leftover, far_step, before_step, diag_step):
        has_pair, pair_j = leftover

        @pl.when(jnp.logical_and(g >= 1, has_pair))
        def _three():
            run_staged([far_step(pair_j), before_step, diag_step])

        @pl.when(jnp.logical_and(g >= 1, jnp.logical_not(has_pair)))
        def _two():
            run_staged([before_step, diag_step])

        @pl.when(g == 0)
        def _first():
            run_staged([diag_step])

    def idx_scores(j, nblocks):
        k_blk = kidx_ref[key_rows(j, nblocks), :]
        acc = [jnp.zeros((nblocks * T, T), F32), jnp.zeros((nblocks * T, T), F32)]
        for tt in range(NT):
            hf, p = divmod(tt, NP)
            s = _dot_nt(k_blk, qim_ref[tt])
            for e in range(2):
                h = 2 * p + e
                w_h = w_rows[h:h + 1, hf * T:(hf + 1) * T]
                acc[hf] = acc[hf] + w_h * jnp.maximum(s[:, e * T:(e + 1) * T], 0.0)
        return jnp.concatenate(acc, axis=1)

    def fox_logits(j, nblocks, slot, causal=False):
        for tt in range(NT):
            hf, p = divmod(tt, NP)
            s = _dot_nt(kaug_ref[p, key_rows(j, nblocks), :], qaug_ref[tt])
            if causal:
                ok = causal_qq[:, hf * T:(hf + 1) * T]
                s = jnp.where(jnp.concatenate([ok, ok], axis=1), s, MASKED)
            work_ref[slot_tile(slot, tt, nblocks)] = s

    def fox_absorb(j, nblocks, slot):
        for tt in range(NT):
            p = tt % NP
            online_update(tt, work_ref[slot_tile(slot, tt, nblocks)],
                          blocks_t(lambda jb: vt_ref[p, jb], j, nblocks))

    def idx_fox_logits(j, nblocks, slot):
        score_ref[pl.ds(j, nblocks)] = idx_scores(j, nblocks).reshape(nblocks, T, Q)
        fox_logits(j, nblocks, slot)

    reset_state()
    leftover = for_far_blocks(jnp.maximum(i0 - 2, 0), idx_fox_logits, fox_absorb)

    def diag_logits(slot):
        score_ref[pl.ds(i0, 2)] = jnp.where(causal_qq, idx_scores(i0, 2), -jnp.inf).reshape(2, T, Q)
        fox_logits(i0, 2, slot, causal=True)

    def fox_far_step(j):
        return (lambda slot: idx_fox_logits(j, 2, slot), lambda slot: fox_absorb(j, 2, slot))

    finish_blocks(leftover, fox_far_step, fox_far_step(i0 - 2),
                  (diag_logits, lambda slot: fox_absorb(i0, 2, slot)))

    row_lo = lax.broadcasted_iota(jnp.int32, (LANES, T), 0) < HEAD_DIM
    for tt in range(NT):
        hf, p = divmod(tt, NP)
        c0 = slice(tt * TW, tt * TW + T)
        c1 = slice(tt * TW + T, (tt + 1) * TW)
        o_t = jnp.where(row_lo, acc_ref[0:LANES, c0] / acc_ref[LANES:LANES + 1, c0],
                        acc_ref[0:LANES, c1] / acc_ref[LANES:LANES + 1, c1])
        o_ref[hf * T:(hf + 1) * T, MIX_HALF + p * LANES:MIX_HALF + (p + 1) * LANES] = (
            o_t.T.astype(o_ref.dtype))

    kf = float(k_sel)
    t_pos = (g * Q + lax.broadcasted_iota(jnp.int32, (SUBLANES, Q), 1)).astype(F32)
    searching = (t_pos + 1.0) > kf
    G2 = 2 * T // SUBLANES
    n_pair = g + 1
    key_in_pair = (lax.broadcasted_iota(jnp.int32, (G2, SUBLANES, Q), 0) * SUBLANES
                   + lax.broadcasted_iota(jnp.int32, (G2, SUBLANES, Q), 1))

    def rep(v):
        return jnp.broadcast_to(v, (SUBLANES, Q))

    def score_pair(jj):
        return score_ref[pl.ds(2 * jj, 2)].reshape(G2, SUBLANES, Q)

    def count(*pred_fns, load=score_pair):
        def body(jj, cs):
            s = load(jj)
            return tuple(c + _tree(jnp.add, jnp.where(f(s, jj), 1.0, 0.0)) for c, f in zip(cs, pred_fns))
        zero = jnp.zeros((SUBLANES, Q), F32)
        cs = lax.fori_loop(0, n_pair, body, (zero,) * len(pred_fns))
        return tuple(rep(jnp.sum(c, axis=0, keepdims=True)) for c in cs)

    def count_ge(thr):
        return count(lambda s, jj: s >= thr[None])[0]

    def select_threshold():
        def first_body(jj, carry):
            mn, mx, ge0, gt0 = carry
            s = score_pair(jj)
            return (jnp.minimum(mn, _tree(jnp.minimum, jnp.where(s == -jnp.inf, jnp.inf, s))),
                    jnp.maximum(mx, _tree(jnp.maximum, s)),
                    ge0 + _tree(jnp.add, jnp.where(s >= 0.0, 1.0, 0.0)),
                    gt0 + _tree(jnp.add, jnp.where(s > 0.0, 1.0, 0.0)))

        zero = jnp.zeros((SUBLANES, Q), F32)
        mn, mx, c_ge0, c_gt0 = lax.fori_loop(0, n_pair, first_body, (zero + jnp.inf, zero - jnp.inf, zero, zero))
        mn = rep(jnp.min(mn, axis=0, keepdims=True))
        mx = rep(jnp.max(mx, axis=0, keepdims=True))
        c_ge0 = rep(jnp.sum(c_ge0, axis=0, keepdims=True))
        c_gt0 = rep(jnp.sum(c_gt0, axis=0, keepdims=True))
        above_mx = mx + jnp.maximum(jnp.abs(mx) * 2.0 ** -20, 1e-30)
        thr_ge0 = c_ge0 >= kf
        thr_gt0 = c_gt0 >= kf
        lo0 = jnp.where(thr_ge0, 0.0, mn)
        cnt_lo0 = jnp.where(thr_ge0, c_ge0, t_pos + 1.0)
        hi0 = jnp.where(thr_gt0, above_mx, 0.0)
        cnt_hi0 = jnp.where(thr_gt0, 0.0, jnp.where(thr_ge0, c_gt0, c_ge0))
        undecided = thr_gt0 | jnp.logical_not(thr_ge0)
        active0 = jnp.where(searching & undecided & (cnt_lo0 > kf), 1.0, 0.0)

        def cond(st):
            return jnp.logical_and(st[5] > 0.0, st[6] < 2200)

        def bisect(st):
            lo, hi, cnt_lo, cnt_hi, active = st
            mid = 0.5 * lo + 0.5 * hi
            adjacent = (mid <= lo) | (mid >= hi)
            c = count_ge(mid)
            act = active > 0.0
            go_lo = act & jnp.logical_not(adjacent) & (c >= kf)
            go_hi = act & jnp.logical_not(adjacent) & (c < kf)
            lo = jnp.where(go_lo, mid, lo)
            cnt_lo = jnp.where(go_lo, c, cnt_lo)
            hi = jnp.where(go_hi, mid, hi)
            cnt_hi = jnp.where(go_hi, c, cnt_hi)
            active = jnp.where(act & jnp.logical_not(adjacent) & (cnt_lo > kf), 1.0, 0.0)
            return lo, hi, cnt_lo, cnt_hi, active

        def body(st):
            inner = st[:5]
            for _ in range(SEARCH_UNROLL):
                inner = bisect(inner)
            return inner + (jnp.max(inner[4]), st[6] + SEARCH_UNROLL)

        lo, hi, cnt_lo, cnt_hi, _, _, _ = lax.while_loop(
            cond, body, (lo0, hi0, cnt_lo0, cnt_hi0, active0, jnp.max(active0), 0))
        thr = jnp.where(searching, lo, -F32_MAX)

        tied = searching & (cnt_lo > kf)
        need = kf - cnt_hi

        @pl.when(jnp.max(jnp.where(tied, 1.0, 0.0)) > 0.0)
        def _break_ties():
            def tie_pair(jj):
                return work_ref[jj].reshape(G2, SUBLANES, Q)

            def mark_body(jj, c):
                key_index = (2 * jj * T + key_in_pair).astype(F32)
                work_ref[jj] = (
                    jnp.where(score_pair(jj) == thr[None], key_index, jnp.inf).reshape(2 * T, Q))
                return c
            lax.fori_loop(0, n_pair, mark_body, 0)

            def count_tied_upto(jcut):
                return count(lambda e, jj: e <= jcut[None], load=tie_pair)[0]

            def bs_body(_, st):
                jl, jh = st
                jm = jnp.floor(0.5 * (jl + jh))
                ok = count_tied_upto(jm) >= need
                return jnp.where(ok, jl, jm), jnp.where(ok, jm, jh)

            n_keys = (2 * n_pair * T).astype(F32)
            jl0 = jnp.full((SUBLANES, Q), -1.0, F32)
            jh0 = jnp.zeros((SUBLANES, Q), F32) + (n_keys - 1.0)
            n_steps = int(math.ceil(math.log2(nblk * T))) + 1
            _, jcut = lax.fori_loop(0, n_steps, bs_body, (jl0, jh0))

            def drop_body(jj, c):
                e = tie_pair(jj)
                drop = tied[None] & (e > jcut[None]) & (e < jnp.inf)
                score_ref[pl.ds(2 * jj, 2)] = jnp.where(drop, -jnp.inf, score_pair(jj)).reshape(2, T, Q)
                return c
            lax.fori_loop(0, n_pair, drop_body, 0)

        return thr

    thr = lax.cond((i0 + 2) * T > k_sel, select_threshold,
                   lambda: jnp.full((SUBLANES, Q), -F32_MAX, F32))

    def mask_body(jj, c):
        add = jnp.where(score_pair(jj) >= thr[None], 0.0, MASKED).reshape(2 * T, Q).astype(BF16)
        for hf in range(2):
            lhsd_ref[hf, key_rows(2 * jj, 2), KV_RANK:2 * KV_RANK] = add[:, hf * T:(hf + 1) * T]
        return c

    lax.fori_loop(0, n_pair, mask_body, 0)

    def dsa_logits(j, nblocks, slot, bias=None):
        for tt in range(NT):
            hf, p = divmod(tt, NP)
            s = _dot_nt(lhsd_ref[hf, key_rows(j, nblocks), :], qsel_ref[tt])
            if bias is not None:
                s = s + bias(hf, slice(p * TW, (p + 1) * TW))
            work_ref[slot_tile(slot, tt, nblocks)] = s

    def dsa_absorb(j, nblocks, slot):
        pv_lhs = blocks_t(lambda jb: ct_ref[jb], j, nblocks)
        for tt in range(NT):
            online_update(tt, work_ref[slot_tile(slot, tt, nblocks)], pv_lhs)

    def bias_before(hf, cols):
        return jnp.concatenate([bias_ref[2, :, cols], bias_ref[1 + hf, :, cols]], axis=0)

    def bias_diag(hf, cols):
        first = bias_ref[hf, :, cols]
        return jnp.concatenate([first, bias_ref[0, :, cols]], axis=0)

    reset_state()
    leftover = for_far_blocks(jnp.maximum(i0 - 2, 0), dsa_logits, dsa_absorb)

    def dsa_far_step(j):
        return (lambda slot: dsa_logits(j, 2, slot), lambda slot: dsa_absorb(j, 2, slot))

    def absorb_first_biased(j):
        def absorb(slot):
            m_ref[...] = m_ref[...] + bfar_ref[...]
            dsa_absorb(j, 2, slot)
        return absorb

    diag_step = lambda absorb: (lambda slot: dsa_logits(i0, 2, slot, bias_diag), absorb)
    before_step = (lambda slot: dsa_logits(i0 - 2, 2, slot, bias_before), absorb_first_biased(i0 - 2))
    has_pair, pair_j = leftover

    @pl.when(jnp.logical_and(g >= 1, has_pair))
    def _dsa_three():
        run_staged([dsa_far_step(pair_j), before_step, diag_step(lambda slot: dsa_absorb(i0, 2, slot))])

    @pl.when(jnp.logical_and(g >= 1, jnp.logical_not(has_pair)))
    def _dsa_two():
        run_staged([before_step, diag_step(lambda slot: dsa_absorb(i0, 2, slot))])

    @pl.when(g == 0)
    def _dsa_first():
        run_staged([diag_step(absorb_first_biased(0))])

    for tt in range(NT):
        hf, p = divmod(tt, NP)
        o_pair = jnp.zeros((T, LANES), F32)
        for e in range(2):
            cols = slice(tt * TW + e * T, tt * TW + (e + 1) * T)
            o_t = acc_ref[0:KV_RANK, cols] / acc_ref[KV_RANK:KV_RANK + 1, cols]
            o_pair = o_pair + _dot(o_t.T.astype(BF16), wuv_ref[2 * p + e])
        o_ref[hf * T:(hf + 1) * T, p * LANES:(p + 1) * LANES] = o_pair.astype(o_ref.dtype)


_EV_QA, _EV_QI, _EV_QB, _EV_KB, _EV_VB = 0, 1, 2, 3, 4
_EV_CKV, _EV_KIDX = 20, 21


def _attention(hb, w_t, f_rows, wuk_pad, wuv_pad, bias_tiles, bias_far, place, k_sel):
    bsz, seq, _ = hb.shape
    T, Q, H = BLK, QUERIES, N_HEADS
    nblk = seq // T
    lanes = 2 * H * T
    pq, cq, pk, ck = place
    wide = lambda c: pl.BlockSpec((None, Q, MIX_HALF), lambda b, i, c=c: (b, i, c))
    once = dict(pipeline_mode=pl.Buffered(1))
    full_wide = lambda c: pl.BlockSpec((None, seq, MIX_HALF), lambda b, i, c=c: (b, 0, c), **once)
    full_narrow = lambda c: pl.BlockSpec((None, seq, LANES), lambda b, i, c=c: (b, 0, c), **once)
    const = lambda a: pl.BlockSpec(a.shape, lambda b, i, nd=a.ndim: (0,) * nd, **once)
    return pl.pallas_call(
        functools.partial(_attn_kernel, k_sel=k_sel),
        grid=(bsz, seq // Q),
        in_specs=[wide(_EV_QA), wide(_EV_QI), wide(_EV_QB),
                  pl.BlockSpec((None, H, Q), lambda b, i: (b, 0, i)),
                  pl.BlockSpec((None, Q, LANES), lambda b, i: (b, i, 0)),
                  full_wide(_EV_KB), full_wide(_EV_VB),
                  full_narrow(_EV_CKV), full_narrow(_EV_KIDX),
                  pl.BlockSpec((None, seq, LANES), lambda b, i: (b, 0, 0), **once),
                  const(wuk_pad), const(wuv_pad), const(bias_tiles), const(bias_far),
                  const(pq), const(cq), const(pk), const(ck)],
        out_specs=pl.BlockSpec((None, Q, 2 * MIX_HALF), lambda b, i: (b, i, 0)),
        out_shape=jax.ShapeDtypeStruct((bsz, seq, 2 * MIX_HALF), BF16),
        scratch_shapes=[pltpu.VMEM((nblk, T, Q), F32),
                        pltpu.VMEM((2, seq, 2 * KV_RANK), BF16),
                        pltpu.VMEM((nblk, PV_ROWS, T), BF16),
                        pltpu.VMEM((H // 2, seq, 2 * LANES), BF16),
                        pltpu.VMEM((H // 2, nblk, PV_ROWS, T), BF16),
                        pltpu.VMEM((H, 2 * T, 2 * KV_RANK), BF16),
                        pltpu.VMEM((H, 2 * T, 2 * LANES), BF16),
                        pltpu.VMEM((H, 2 * T, LANES), BF16),
                        pltpu.VMEM((SUBLANES, lanes), F32),
                        pltpu.VMEM((PV_ROWS, lanes), F32),
                        pltpu.VMEM((max(N_SLOTS * H, nblk // 2), 2 * T, 2 * T), F32)],
        compiler_params=_cparams(("parallel", "arbitrary")),
        name="dsa_fox_attention",
    )(hb, hb, hb, w_t, f_rows, hb, hb, hb, hb, f_rows, wuk_pad, wuv_pad, bias_tiles, bias_far,
      pq, cq, pk, ck)


def _proj_ln_kernel(a_ref, w_ref, x_ref, g_ref, b_ref, xo_ref, xb_ref, *, alpha, sub):
    for c in range(a_ref.shape[0] // sub):
        rows = slice(c * sub, (c + 1) * sub)
        y = alpha * x_ref[rows, :] + _dot(a_ref[rows, :], w_ref[...])
        out = _layer_norm_rows(y, g_ref[...], b_ref[...])
        xo_ref[rows, :] = out
        xb_ref[rows, :] = out.astype(BF16)


def _proj_ln(a, w, layer, x, g, b, alpha, tm):
    n, k = a.shape
    d = w.shape[2]
    return pl.pallas_call(
        functools.partial(_proj_ln_kernel, alpha=alpha, sub=min(128, tm)),
        grid=(n // tm,),
        in_specs=[pl.BlockSpec((tm, k), lambda i: (i, 0)),
                  pl.BlockSpec((None, k, d), lambda i: (layer, 0, 0)),
                  pl.BlockSpec((tm, d), lambda i: (i, 0)),
                  pl.BlockSpec((1, d), lambda i: (0, 0)),
                  pl.BlockSpec((1, d), lambda i: (0, 0))],
        out_specs=[pl.BlockSpec((tm, d), lambda i: (i, 0)),
                   pl.BlockSpec((tm, d), lambda i: (i, 0))],
        out_shape=[jax.ShapeDtypeStruct((n, d), F32),
                   jax.ShapeDtypeStruct((n, d), BF16)],
        compiler_params=_cparams(("parallel",)),
        name="proj_residual_ln",
    )(a, w, x, g, b)


def _causal_conv3(h, tail, cw):
    w0, w1, w2 = cw[0:1, :], cw[1:2, :], cw[2:3, :]
    body = w0 * pltpu.roll(h, 2, axis=0) + w1 * pltpu.roll(h, 1, axis=0) + w2 * h
    head = h[0:SUBLANES, :]
    r = lax.broadcasted_iota(jnp.int32, head.shape, 0)
    hm1 = jnp.where(r < 1, pltpu.roll(tail, 1, axis=0), pltpu.roll(head, 1, axis=0))
    hm2 = jnp.where(r < 2, pltpu.roll(tail, 2, axis=0), pltpu.roll(head, 2, axis=0))
    out_head = w0 * hm2 + w1 * hm1 + w2 * head
    return jnp.concatenate([out_head, body[SUBLANES:, :]], axis=0)


def _ffn_up_kernel(x_ref, wg_ref, wv_ref, cg_ref, cv_ref, o_ref, tg_ref, tv_ref, *h_refs,
                   tiles_per_seq, sub):
    n_chunks = x_ref.shape[0] // sub
    hg_refs, hv_refs = h_refs[:n_chunks], h_refs[n_chunks:]
    first = (pl.program_id(1) % tiles_per_seq) == 0
    body_rows = slice(SUBLANES, SUBLANES + sub)

    def matmuls(c):
        x = x_ref[c * sub:(c + 1) * sub, :]
        hg_refs[c][body_rows, :] = _dot(x, wg_ref[...])
        hv_refs[c][body_rows, :] = _dot(x, wv_ref[...])

    def conv(h_ref, cw):
        return (cw[0:1, :] * h_ref[SUBLANES - 2:SUBLANES - 2 + sub, :]
                + cw[1:2, :] * h_ref[SUBLANES - 1:SUBLANES - 1 + sub, :]
                + cw[2:3, :] * h_ref[body_rows, :])

    def gate(c):
        if c == 0:
            hg_refs[0][0:SUBLANES, :] = jnp.where(first, 0.0, tg_ref[...])
            hv_refs[0][0:SUBLANES, :] = jnp.where(first, 0.0, tv_ref[...])
        else:
            hg_refs[c][0:SUBLANES, :] = hg_refs[c - 1][sub:sub + SUBLANES, :]
            hv_refs[c][0:SUBLANES, :] = hv_refs[c - 1][sub:sub + SUBLANES, :]
        g = conv(hg_refs[c], cg_ref[...])
        v = conv(hv_refs[c], cv_ref[...])
        half_g = 0.5 * g
        o_ref[c * sub:(c + 1) * sub, :] = ((half_g + half_g * jnp.tanh(half_g)) * v).astype(o_ref.dtype)

    matmuls(0)
    for c in range(n_chunks):
        if c + 1 < n_chunks:
            matmuls(c + 1)
        gate(c)
    tg_ref[...] = hg_refs[n_chunks - 1][sub:sub + SUBLANES, :]
    tv_ref[...] = hv_refs[n_chunks - 1][sub:sub + SUBLANES, :]


def _ffn_up(xb, w_up, conv_w, layer, seq, tm, tn, sub):
    n, k = xb.shape
    dff = w_up.shape[2] // 2
    ncol = dff // tn
    return pl.pallas_call(
        functools.partial(_ffn_up_kernel, tiles_per_seq=seq // tm, sub=sub),
        grid=(ncol, n // tm),
        in_specs=[pl.BlockSpec((tm, k), lambda j, i: (i, 0)),
                  pl.BlockSpec((None, k, tn), lambda j, i: (layer, 0, j)),
                  pl.BlockSpec((None, k, tn), lambda j, i, ncol=ncol: (layer, 0, j + ncol)),
                  pl.BlockSpec((None, CONV_W, tn), lambda j, i: (layer, 0, j)),
                  pl.BlockSpec((None, CONV_W, tn), lambda j, i, ncol=ncol: (layer, 0, j + ncol))],
        out_specs=pl.BlockSpec((tm, tn), lambda j, i: (i, j)),
        out_shape=jax.ShapeDtypeStruct((n, dff), BF16),
        scratch_shapes=([pltpu.VMEM((SUBLANES, tn), F32)] * 2
                        + [pltpu.VMEM((sub + SUBLANES, tn), F32)] * (2 * (tm // sub))),
        compiler_params=_cparams(("arbitrary", "arbitrary")),
        name="ffn_up_conv_gate",
    )(xb, w_up, w_up, conv_w, conv_w)


def _gelu_tanh(x):
    return 0.5 * x * (1.0 + jnp.tanh(math.sqrt(2.0 / math.pi) * (x + 0.044715 * (x * x * x))))


def _group_mean(v, avg):
    hi = v.astype(BF16)
    lo = (v - hi.astype(F32)).astype(BF16)
    return _dot(hi, avg) + _dot(lo, avg)


def _odd_mixer_kernel(x_ref, w_ref, sg_ref, sw_ref, sb_ref, cw_ref, wo_ref, xf_ref, g_ref, b_ref,
                      xo_ref, xb_ref, o_ref, tail_ref, *, tiles_per_seq, alpha):
    tm = x_ref.shape[0]
    W = MIX_HALF
    first = (pl.program_id(0) % tiles_per_seq) == 0
    x = x_ref[...]
    lane = lax.broadcasted_iota(jnp.int32, (CHUNK, LANES), 1)
    lo_half = lane < HEAD_DIM
    r = lax.broadcasted_iota(jnp.int32, (W, W), 0) // HEAD_DIM
    c = lax.broadcasted_iota(jnp.int32, (W, W), 1) // HEAD_DIM
    avg = jnp.where(r == c, 1.0 / HEAD_DIM, 0.0).astype(BF16)
    tri = (lax.broadcasted_iota(jnp.int32, (CHUNK, CHUNK), 1)
           <= lax.broadcasted_iota(jnp.int32, (CHUNK, CHUNK), 0))

    u = _gelu_tanh(_dot(x, w_ref[:, 0:W]))
    v = _gelu_tanh(_dot(x, w_ref[:, W:2 * W]))
    vc = v - _group_mean(v, avg)
    var = _group_mean(vc * vc, avg)
    vn = (vc * lax.rsqrt(var + LN_EPS) * sg_ref[...]).astype(BF16)
    for n in range(tm // CHUNK):
        rows = slice(n * CHUNK, (n + 1) * CHUNK)
        for p in range(N_HEADS // 2):
            cols = slice(p * LANES, (p + 1) * LANES)
            vp = vn[rows, cols]
            w_e = jnp.where(tri, sw_ref[2 * p], 0.0).astype(BF16)
            w_o = jnp.where(tri, sw_ref[2 * p + 1], 0.0).astype(BF16)
            mix = (_dot(w_e, jnp.where(lo_half, vp, 0)) + _dot(w_o, jnp.where(lo_half, 0, vp))
                   + sb_ref[:, cols])
            o_ref[rows, cols] = (u[rows, cols] * mix).astype(o_ref.dtype)

    g_b = _dot(x, w_ref[:, 2 * W:3 * W])
    y = _dot(x, w_ref[:, 3 * W:4 * W]) * _dot(x, w_ref[:, 4 * W:5 * W])
    conv = _causal_conv3(y, jnp.where(first, 0.0, tail_ref[...]), cw_ref[...])
    tail_ref[...] = y[tm - SUBLANES:tm, :]
    o_ref[:, W:2 * W] = (g_b * conv).astype(o_ref.dtype)

    out = _layer_norm_rows(alpha * xf_ref[...] + _dot(o_ref[...], wo_ref[...]), g_ref[...], b_ref[...])
    xo_ref[...] = out
    xb_ref[...] = out.astype(BF16)


def _odd_mixer(xb, w_in, sgu_g, sgu_w, sgu_b_tile, conv_w, w_o, layer, xf, g, b, alpha, seq, tm):
    n, k = xb.shape
    W = MIX_HALF
    d = w_o.shape[2]
    const2 = lambda shape: pl.BlockSpec(shape, lambda i: (0, 0))
    of_layer = lambda shape: pl.BlockSpec((None,) + shape, lambda i: (layer, 0, 0))
    row_tile = lambda width: pl.BlockSpec((tm, width), lambda i: (i, 0))
    return pl.pallas_call(
        functools.partial(_odd_mixer_kernel, tiles_per_seq=seq // tm, alpha=alpha),
        grid=(n // tm,),
        in_specs=[row_tile(k), of_layer((k, 5 * W)), const2((1, W)),
                  pl.BlockSpec((N_HEADS, CHUNK, CHUNK), lambda i: (0, 0, 0)),
                  const2((CHUNK, W)), const2((CONV_W, W)),
                  of_layer((2 * W, d)), row_tile(d), const2((1, d)), const2((1, d))],
        out_specs=[row_tile(d), row_tile(d)],
        out_shape=[jax.ShapeDtypeStruct((n, d), F32), jax.ShapeDtypeStruct((n, d), BF16)],
        scratch_shapes=[pltpu.VMEM((tm, 2 * W), BF16),
                        pltpu.VMEM((SUBLANES, W), F32)],
        compiler_params=_cparams(("arbitrary",)),
        name="odd_mixer",
    )(xb, w_in, sgu_g, sgu_w, sgu_b_tile, conv_w, w_o, xf, g, b)


def _t5_bucket_np(dist):
    max_exact = N_BUCKETS // 2
    n = np.maximum(dist, 0)
    nf = np.maximum(n, 1).astype(np.float32)
    large = max_exact + (np.log(nf / max_exact) / np.float32(math.log(T5_MAX_DIST / max_exact))
                         * (N_BUCKETS - max_exact)).astype(np.int32)
    large = np.minimum(large, N_BUCKETS - 1)
    return np.where(n < max_exact, n, large)


def _bias_tables(rel_bias):
    T = BLK
    s = np.arange(T)[:, None]
    t = np.arange(T)[None, :]
    bucket = _t5_bucket_np(np.stack([t + d * T - s for d in range(3)]))
    far_bucket = N_BUCKETS - 1
    assert (bucket[2] == far_bucket).all()
    assert (_t5_bucket_np(np.arange(T + 1, 64 * T)) == far_bucket).all()
    rb = rel_bias.astype(F32) * LOG2E
    tiles = sum(jnp.where(jnp.asarray(bucket == b)[..., None], rb[b], 0.0) for b in range(N_BUCKETS))
    tiles = jnp.swapaxes(tiles, 2, 3).reshape(3, T, N_HEADS * T)
    far = jnp.tile(jnp.repeat(rb[far_bucket], T)[None, :], (1, 2))
    return tiles, far


def _placement_constants():
    H, L = N_HEADS, LANES
    pq = np.zeros((3 * L, H * L), np.float32)
    cq = np.zeros((1, H * L), np.float32)
    pk = np.zeros((3 * L, (H // 2) * L), np.float32)
    ck = np.zeros((1, (H // 2) * L), np.float32)
    for h in range(H):
        p, e = divmod(h, 2)
        for piece in range(3):
            pq[piece * L + h, h * L + 6 + piece] = 1.0
            cq[0, h * L + 3 * e + piece] = 1.0
            pk[piece * L + h, p * L + 3 * e + piece] = -1.0
            ck[0, p * L + 6 + piece] = 1.0
    return jnp.asarray(pq, BF16), jnp.asarray(cq), jnp.asarray(pk, BF16), jnp.asarray(ck)


def _even_weights(w_in, w_uk, w_uv):
    hd, H = HEAD_DIM, N_HEADS
    widths = (H * hd, KV_RANK, H * hd, hd, H, H * hd, H * hd, H * hd, H)
    offs = np.cumsum((0,) + widths)
    seg = lambda s: w_in[:, offs[s]:offs[s + 1]]
    q_a, c_kv, q_idx, k_idx, w_idx, q_b, k_b, v_b, f_in = [seg(s) for s in range(9)]
    scale = hd ** -0.5
    w_big = jnp.concatenate([q_a, q_idx * scale, q_b * (scale * LOG2E), k_b, v_b, c_kv,
                             k_idx, k_idx], axis=1).astype(BF16)
    pad = jnp.zeros((w_in.shape[0], LANES - 2 * H), w_in.dtype)
    w_small = jnp.concatenate([w_idx, f_in, pad], axis=1).astype(BF16)
    wuk_t = jnp.swapaxes(w_uk, 1, 2) * (scale * LOG2E)
    odd = (np.arange(H) % 2 == 1)[:, None, None]
    zk, zv = jnp.zeros_like(wuk_t), jnp.zeros_like(w_uv)
    wuk_pad = jnp.where(odd, jnp.concatenate([zk, wuk_t], axis=1), jnp.concatenate([wuk_t, zk], axis=1))
    wuv_pad = jnp.where(odd, jnp.concatenate([zv, w_uv], axis=2), jnp.concatenate([w_uv, zv], axis=2))
    return w_big, w_small, wuk_pad.astype(BF16), wuv_pad.astype(BF16)


def kernel(x, ln_g, ln_b, rel_bias, ev_w_in, ev_w_uk, ev_w_uv, ev_b_f, ev_w_o,
           od_w_in, od_sgu_g, od_sgu_w, od_sgu_b, od_conv_w, od_w_o,
           ffn_w_up, ffn_conv_w, ffn_w_down):
    bsz, seq, d = x.shape
    depth = ln_g.shape[0]
    n = bsz * seq
    alpha = (2.0 * depth) ** 0.25
    k_sel = min(TOPK_MAX, seq // 4)
    H = N_HEADS
    assert seq % QUERIES == 0 and seq % CHUNK == 0
    tm_ffn = min(1024, seq)

    bias_tiles, bias_far = _bias_tables(rel_bias)
    place = _placement_constants()
    w_up_b, w_down_b, ev_w_o_b = ffn_w_up.astype(BF16), ffn_w_down.astype(BF16), ev_w_o.astype(BF16)
    od_w_in_b, od_w_o_b = od_w_in.astype(BF16), od_w_o.astype(BF16)
    xf = x.reshape(n, d)
    xb = xf
    for layer in range(depth):
        j = layer // 2
        g0, b0 = ln_g[layer, 0][None, :], ln_b[layer, 0][None, :]
        g1, b1 = ln_g[layer, 1][None, :], ln_b[layer, 1][None, :]
        if layer % 2 == 0:
            w_big, w_small, wuk_pad, wuv_pad = _even_weights(ev_w_in[j], ev_w_uk[j], ev_w_uv[j])
            hb, hs = _even_proj(xb, w_big, w_small, tm_ffn)
            hb = hb.reshape(bsz, seq, -1)
            hs_t = jnp.swapaxes(hs[0:2 * H].reshape(2 * H, bsz, seq), 0, 1)
            f_rows = _cumf(hs_t[:, H:2 * H], ev_b_f[j][:, None])
            mix = _attention(hb, hs_t[:, 0:H], f_rows, wuk_pad, wuv_pad, bias_tiles, bias_far,
                             place, k_sel)
            xf, xb = _proj_ln(mix.reshape(n, d), ev_w_o_b, j, xf, g0, b0, alpha, tm_ffn)
        else:
            sgu_b_tile = jnp.repeat(jnp.swapaxes(od_sgu_b[j], 0, 1), HEAD_DIM, axis=1)
            xf, xb = _odd_mixer(xb, od_w_in_b, od_sgu_g[j].reshape(1, -1), od_sgu_w[j],
                                sgu_b_tile, od_conv_w[j], od_w_o_b, j, xf, g0, b0, alpha,
                                seq, min(1024, seq))
        act = _ffn_up(xb, w_up_b, ffn_conv_w, layer, seq, min(4096, seq), 256, min(512, seq))
        xf, xb = _proj_ln(act, w_down_b, layer, xf, g1, b1, alpha, tm_ffn)
    return xf.reshape(bsz, seq, d)
```

```python
import functools
import math

import numpy as np
import jax
import jax.numpy as jnp
from jax import lax
from jax.experimental import pallas as pl
from jax.experimental.pallas import tpu as pltpu

F32 = jnp.float32
BF16 = jnp.bfloat16

HEAD_DIM = 64
N_HEADS = 8
KV_RANK = 128
TOPK_MAX = 256
N_BUCKETS = 32
T5_MAX_DIST = 128
CHUNK = 128
CONV_W = 3
LN_EPS = 1e-5
MIX_HALF = N_HEADS * HEAD_DIM

LANES = 128
SUBLANES = 8
BF16_ROWS = 16
VMEM_LIMIT = 56 * 1024 * 1024

BLK = 128
QUERIES = 2 * BLK
PV_ROWS = KV_RANK + BF16_ROWS
NEG_BIG = -1e30
MASKED = -(2.0 ** 100)
F32_MAX = float(np.finfo(np.float32).max)
LOG2E = math.log2(math.e)
SEARCH_UNROLL = 3
LOOKAHEAD = 2
N_SLOTS = LOOKAHEAD + 1


def _cparams(sem):
    return pltpu.CompilerParams(dimension_semantics=sem, vmem_limit_bytes=VMEM_LIMIT)


def _dot(a, b):
    return jnp.dot(a, b, preferred_element_type=F32)


def _dot_nt(a, b):
    return lax.dot_general(a, b, (((1,), (1,)), ((), ())), preferred_element_type=F32)


def _layer_norm_rows(y, g, b):
    mu = jnp.mean(y, axis=-1, keepdims=True)
    yc = y - mu
    var = jnp.mean(yc * yc, axis=-1, keepdims=True)
    return yc * lax.rsqrt(var + LN_EPS) * g + b


def _tree(op, x):
    while x.shape[0] > 1:
        half = x.shape[0] // 2
        x = op(x[:half], x[half:])
    return x[0]


def _bf16_pieces(v):
    hi = v.astype(BF16)
    r1 = v - hi.astype(F32)
    mid = r1.astype(BF16)
    lo = (r1 - mid.astype(F32)).astype(BF16)
    return jnp.concatenate([hi, mid, lo], axis=1)


def _even_proj_kernel(x_ref, wb_ref, ws_ref, hb_ref, hs_ref):
    x = x_ref[...].astype(BF16)
    hb_ref[...] = _dot(x, wb_ref[...]).astype(hb_ref.dtype)
    hs_ref[...] = _dot(x, ws_ref[...]).T


def _even_proj(xb, w_big, w_small, tm):
    n, k = xb.shape
    mb = w_big.shape[1]
    ms = w_small.shape[1]
    return pl.pallas_call(
        _even_proj_kernel,
        grid=(n // tm,),
        in_specs=[pl.BlockSpec((tm, k), lambda i: (i, 0)),
                  pl.BlockSpec((k, mb), lambda i: (0, 0), pipeline_mode=pl.Buffered(1)),
                  pl.BlockSpec((k, ms), lambda i: (0, 0), pipeline_mode=pl.Buffered(1))],
        out_specs=[pl.BlockSpec((tm, mb), lambda i: (i, 0)),
                   pl.BlockSpec((ms, tm), lambda i: (0, i))],
        out_shape=[jax.ShapeDtypeStruct((n, mb), BF16),
                   jax.ShapeDtypeStruct((ms, n), F32)],
        compiler_params=_cparams(("parallel",)),
        name="even_proj",
    )(xb, w_big, w_small)


def _cumf_kernel(f_ref, b_ref, o_ref):
    seq = f_ref.shape[-1]
    row = lax.broadcasted_iota(jnp.int32, (LANES, LANES), 0)
    col = lax.broadcasted_iota(jnp.int32, (LANES, LANES), 1)
    upper = jnp.where(row <= col, 1.0, 0.0).astype(F32)
    nh = f_ref.shape[0]
    carry = jnp.zeros((nh, 1), F32)
    pad_rows = jnp.zeros((LANES - nh, LANES), F32)
    for c in range(seq // LANES):
        z = f_ref[:, c * LANES:(c + 1) * LANES] + b_ref[...]
        lf = jnp.minimum(z, 0.0) - jnp.log1p(jnp.exp(-jnp.abs(z)))
        cs = jnp.dot(lf, upper, preferred_element_type=F32,
                     precision=lax.Precision.HIGHEST) + carry
        o_ref[c * LANES:(c + 1) * LANES, :] = jnp.concatenate([cs, pad_rows], axis=0).T
        carry = cs[:, LANES - 1:LANES]


def _cumf(f_t, b_f):
    bsz, nh, seq = f_t.shape
    return pl.pallas_call(
        _cumf_kernel,
        grid=(bsz,),
        in_specs=[pl.BlockSpec((None, nh, seq), lambda b: (b, 0, 0)),
                  pl.BlockSpec((nh, 1), lambda b: (0, 0))],
        out_specs=pl.BlockSpec((None, seq, LANES), lambda b: (b, 0, 0)),
        out_shape=jax.ShapeDtypeStruct((bsz, seq, LANES), F32),
        compiler_params=_cparams(("parallel",)),
        name="fox_cumf",
    )(f_t, b_f)


def _attn_kernel(qa_ref, qi_ref, qb_ref, wt_ref, fq_ref, kb_ref, vb_ref, ckv_ref, kidx_ref, fk_ref,
                 wuk_ref, wuv_ref, bias_ref, bfar_ref, pq_ref, cq_ref, pk_ref, ck_ref,
                 o_ref,
                 score_ref, lhsd_ref, ct_ref, kaug_ref, vt_ref, qsel_ref, qaug_ref, qim_ref,
                 m_ref, acc_ref, work_ref, *, k_sel):
    T = BLK
    Q = QUERIES
    H = N_HEADS
    NP = H // 2
    NT = 2 * NP
    TW = 2 * T
    g = pl.program_id(1)
    i0 = 2 * g
    nblk = ct_ref.shape[0]
    lane = lax.broadcasted_iota(jnp.int32, (T, LANES), 1)
    lo_half = lane < HEAD_DIM
    ones_rows = jnp.ones((BF16_ROWS, T), BF16)
    causal_qq = (lax.broadcasted_iota(jnp.int32, (Q, Q), 0) <= lax.broadcasted_iota(jnp.int32, (Q, Q), 1))

    @pl.when(g == 0)
    def _per_batch():
        def prep(j, c):
            r = pl.ds(pl.multiple_of(j * T, T), T)
            cb = ckv_ref[r, :]
            lhsd_ref[0, r, 0:KV_RANK] = cb
            lhsd_ref[1, r, 0:KV_RANK] = cb
            ct_ref[j, 0:KV_RANK, :] = cb.astype(F32).T.astype(BF16)
            ct_ref[j, KV_RANK:PV_ROWS, :] = ones_rows
            ext = (_dot(_bf16_pieces(fk_ref[r, :] * LOG2E), pk_ref[...]) + ck_ref[...]).astype(BF16)
            for p in range(NP):
                cols = slice(p * LANES, (p + 1) * LANES)
                kaug_ref[p, r, 0:LANES] = kb_ref[r, cols]
                kaug_ref[p, r, LANES:2 * LANES] = ext[:, cols]
                vt_ref[p, j, 0:LANES, :] = vb_ref[r, cols].astype(F32).T.astype(BF16)
                vt_ref[p, j, LANES:PV_ROWS, :] = ones_rows
            return c
        lax.fori_loop(0, nblk, prep, 0)

    w_rows = wt_ref[...] * (H ** -0.5)
    ext_q = (_dot(_bf16_pieces(fq_ref[...] * LOG2E), pq_ref[...]) + cq_ref[...]).astype(BF16)
    eye = jnp.where(lax.broadcasted_iota(jnp.int32, (T, T), 0) == lax.broadcasted_iota(jnp.int32, (T, T), 1),
                    1.0, 0.0).astype(BF16)
    for tt in range(NT):
        hf, p = divmod(tt, NP)
        qrows = slice(hf * T, (hf + 1) * T)
        cols = slice(p * LANES, (p + 1) * LANES)
        qa_pair, qi_pair, qb_pair = qa_ref[qrows, cols], qi_ref[qrows, cols], qb_ref[qrows, cols]
        for e in range(2):
            h = 2 * p + e
            rows = slice(e * T, (e + 1) * T)
            mine = lo_half if e == 0 else jnp.logical_not(lo_half)
            qsel_ref[tt, rows, 0:KV_RANK] = _dot(qa_pair, wuk_ref[h]).astype(BF16)
            qsel_ref[tt, rows, KV_RANK:2 * KV_RANK] = eye
            qaug_ref[tt, rows, 0:LANES] = jnp.where(mine, qb_pair, 0)
            qaug_ref[tt, rows, LANES:2 * LANES] = ext_q[qrows, h * LANES:(h + 1) * LANES]
            qim_ref[tt, rows, :] = jnp.where(mine, qi_pair, 0)

    def reset_state():
        m_ref[...] = jnp.full(m_ref.shape, NEG_BIG, F32)
        acc_ref[...] = jnp.zeros(acc_ref.shape, F32)

    def tile_cols(tt):
        return slice(tt * TW, (tt + 1) * TW)

    def online_update(tt, s, pv_lhs):
        cols = tile_cols(tt)
        m_old = m_ref[:, cols]
        s_max = _tree(jnp.maximum, s.reshape(-1, SUBLANES, TW))
        m_new = jnp.maximum(m_old, jnp.max(s_max, axis=0, keepdims=True))
        alpha = jnp.exp2(m_old - m_new)
        pr = jnp.exp2((s - m_new[0:1, :]).astype(BF16))
        acc_ref[:, cols] = acc_ref[:, cols] * alpha[0:1, :] + _dot(pv_lhs, pr)
        m_ref[:, cols] = m_new

    def key_rows(j, nblocks):
        return pl.ds(pl.multiple_of(j * T, T), nblocks * T)

    def blocks_t(load, j, nblocks):
        return load(j) if nblocks == 1 else jnp.concatenate([load(j + b) for b in range(nblocks)], axis=1)

    def slot_tile(slot, tt, nblocks):
        return (slot * NT + tt, slice(0, nblocks * T), slice(None))

    def run_staged(steps):
        for k in range(min(LOOKAHEAD, len(steps))):
            steps[k][0](k % N_SLOTS)
        for k, (_, absorb_k) in enumerate(steps):
            if k + LOOKAHEAD < len(steps):
                steps[k + LOOKAHEAD][0]((k + LOOKAHEAD) % N_SLOTS)
            absorb_k(k % N_SLOTS)

    def staged_steps(j0, n_steps, logits, absorb):
        run_staged([(functools.partial(logits, j0 + 2 * k, 2), functools.partial(absorb, j0 + 2 * k, 2))
                    for k in range(n_steps)])

    def for_far_blocks(n_far, logits, absorb):
        n_oct = lax.shift_right_logical(n_far, 3)

        def oct_body(jj, c):
            staged_steps(8 * jj, 4, logits, absorb)
            return c
        lax.fori_loop(0, n_oct, oct_body, 0)

        @pl.when((n_far & 4) != 0)
        def _quad():
            staged_steps(8 * n_oct, 2, logits, absorb)

        return (n_far & 2) != 0, 8 * n_oct + (n_far & 4)

    def finish_blocks(leftover, far_step, before_step, diag_step):
        has_pair, pair_j = leftover

        @pl.when(jnp.logical_and(g >= 1, has_pair))
        def _three():
            run_staged([far_step(pair_j), before_step, diag_step])

        @pl.when(jnp.logical_and(g >= 1, jnp.logical_not(has_pair)))
        def _two():
            run_staged([before_step, diag_step])

        @pl.when(g == 0)
        def _first():
            run_staged([diag_step])

    def idx_scores(j, nblocks):
        k_blk = kidx_ref[key_rows(j, nblocks), :]
        acc = [jnp.zeros((nblocks * T, T), F32), jnp.zeros((nblocks * T, T), F32)]
        for tt in range(NT):
            hf, p = divmod(tt, NP)
            s = _dot_nt(k_blk, qim_ref[tt])
            for e in range(2):
                h = 2 * p + e
                w_h = w_rows[h:h + 1, hf * T:(hf + 1) * T]
                acc[hf] = acc[hf] + w_h * jnp.maximum(s[:, e * T:(e + 1) * T], 0.0)
        return jnp.concatenate(acc, axis=1)

    def fox_logits(j, nblocks, slot, causal=False):
        for tt in range(NT):
            hf, p = divmod(tt, NP)
            s = _dot_nt(kaug_ref[p, key_rows(j, nblocks), :], qaug_ref[tt])
            if causal:
                ok = causal_qq[:, hf * T:(hf + 1) * T]
                s = jnp.where(jnp.concatenate([ok, ok], axis=1), s, MASKED)
            work_ref[slot_tile(slot, tt, nblocks)] = s

    def fox_absorb(j, nblocks, slot):
        for tt in range(NT):
            p = tt % NP
            online_update(tt, work_ref[slot_tile(slot, tt, nblocks)],
                          blocks_t(lambda jb: vt_ref[p, jb], j, nblocks))

    def idx_fox_logits(j, nblocks, slot):
        score_ref[pl.ds(j, nblocks)] = idx_scores(j, nblocks).reshape(nblocks, T, Q)
        fox_logits(j, nblocks, slot)

    reset_state()
    leftover = for_far_blocks(jnp.maximum(i0 - 2, 0), idx_fox_logits, fox_absorb)

    def diag_logits(slot):
        score_ref[pl.ds(i0, 2)] = jnp.where(causal_qq, idx_scores(i0, 2), -jnp.inf).reshape(2, T, Q)
        fox_logits(i0, 2, slot, causal=True)

    def fox_far_step(j):
        return (lambda slot: idx_fox_logits(j, 2, slot), lambda slot: fox_absorb(j, 2, slot))

    finish_blocks(leftover, fox_far_step, fox_far_step(i0 - 2),
                  (diag_logits, lambda slot: fox_absorb(i0, 2, slot)))

    row_lo = lax.broadcasted_iota(jnp.int32, (LANES, T), 0) < HEAD_DIM
    for tt in range(NT):
        hf, p = divmod(tt, NP)
        c0 = slice(tt * TW, tt * TW + T)
        c1 = slice(tt * TW + T, (tt + 1) * TW)
        o_t = jnp.where(row_lo, acc_ref[0:LANES, c0] / acc_ref[LANES:LANES + 1, c0],
                        acc_ref[0:LANES, c1] / acc_ref[LANES:LANES + 1, c1])
        o_ref[hf * T:(hf + 1) * T, MIX_HALF + p * LANES:MIX_HALF + (p + 1) * LANES] = (
            o_t.T.astype(o_ref.dtype))

    kf = float(k_sel)
    t_pos = (g * Q + lax.broadcasted_iota(jnp.int32, (SUBLANES, Q), 1)).astype(F32)
    searching = (t_pos + 1.0) > kf
    G2 = 2 * T // SUBLANES
    n_pair = g + 1
    key_in_pair = (lax.broadcasted_iota(jnp.int32, (G2, SUBLANES, Q), 0) * SUBLANES
                   + lax.broadcasted_iota(jnp.int32, (G2, SUBLANES, Q), 1))

    def rep(v):
        return jnp.broadcast_to(v, (SUBLANES, Q))

    def score_pair(jj):
        return score_ref[pl.ds(2 * jj, 2)].reshape(G2, SUBLANES, Q)

    def count(*pred_fns, load=score_pair):
        def body(jj, cs):
            s = load(jj)
            return tuple(c + _tree(jnp.add, jnp.where(f(s, jj), 1.0, 0.0)) for c, f in zip(cs, pred_fns))
        zero = jnp.zeros((SUBLANES, Q), F32)
        cs = lax.fori_loop(0, n_pair, body, (zero,) * len(pred_fns))
        return tuple(rep(jnp.sum(c, axis=0, keepdims=True)) for c in cs)

    def count_ge(thr):
        return count(lambda s, jj: s >= thr[None])[0]

    def select_threshold():
        def first_body(jj, carry):
            mn, mx, ge0, gt0 = carry
            s = score_pair(jj)
            return (jnp.minimum(mn, _tree(jnp.minimum, jnp.where(s == -jnp.inf, jnp.inf, s))),
                    jnp.maximum(mx, _tree(jnp.maximum, s)),
                    ge0 + _tree(jnp.add, jnp.where(s >= 0.0, 1.0, 0.0)),
                    gt0 + _tree(jnp.add, jnp.where(s > 0.0, 1.0, 0.0)))

        zero = jnp.zeros((SUBLANES, Q), F32)
        mn, mx, c_ge0, c_gt0 = lax.fori_loop(0, n_pair, first_body, (zero + jnp.inf, zero - jnp.inf, zero, zero))
        mn = rep(jnp.min(mn, axis=0, keepdims=True))
        mx = rep(jnp.max(mx, axis=0, keepdims=True))
        c_ge0 = rep(jnp.sum(c_ge0, axis=0, keepdims=True))
        c_gt0 = rep(jnp.sum(c_gt0, axis=0, keepdims=True))
        above_mx = mx + jnp.maximum(jnp.abs(mx) * 2.0 ** -20, 1e-30)
        thr_ge0 = c_ge0 >= kf
        thr_gt0 = c_gt0 >= kf
        lo0 = jnp.where(thr_ge0, 0.0, mn)
        cnt_lo0 = jnp.where(thr_ge0, c_ge0, t_pos + 1.0)
        hi0 = jnp.where(thr_gt0, above_mx, 0.0)
        cnt_hi0 = jnp.where(thr_gt0, 0.0, jnp.where(thr_ge0, c_gt0, c_ge0))
        undecided = thr_gt0 | jnp.logical_not(thr_ge0)
        active0 = jnp.where(searching & undecided & (cnt_lo0 > kf), 1.0, 0.0)

        def cond(st):
            return jnp.logical_and(st[5] > 0.0, st[6] < 2200)

        def bisect(st):
            lo, hi, cnt_lo, cnt_hi, active = st
            mid = 0.5 * lo + 0.5 * hi
            adjacent = (mid <= lo) | (mid >= hi)
            c = count_ge(mid)
            act = active > 0.0
            go_lo = act & jnp.logical_not(adjacent) & (c >= kf)
            go_hi = act & jnp.logical_not(adjacent) & (c < kf)
            lo = jnp.where(go_lo, mid, lo)
            cnt_lo = jnp.where(go_lo, c, cnt_lo)
            hi = jnp.where(go_hi, mid, hi)
            cnt_hi = jnp.where(go_hi, c, cnt_hi)
            active = jnp.where(act & jnp.logical_not(adjacent) & (cnt_lo > kf), 1.0, 0.0)
            return lo, hi, cnt_lo, cnt_hi, active

        def body(st):
            inner = st[:5]
            for _ in range(SEARCH_UNROLL):
                inner = bisect(inner)
            return inner + (jnp.max(inner[4]), st[6] + SEARCH_UNROLL)

        lo, hi, cnt_lo, cnt_hi, _, _, _ = lax.while_loop(
            cond, body, (lo0, hi0, cnt_lo0, cnt_hi0, active0, jnp.max(active0), 0))
        thr = jnp.where(searching, lo, -F32_MAX)

        tied = searching & (cnt_lo > kf)
        need = kf - cnt_hi

        @pl.when(jnp.max(jnp.where(tied, 1.0, 0.0)) > 0.0)
        def _break_ties():
            def tie_pair(jj):
                return work_ref[jj].reshape(G2, SUBLANES, Q)

            def mark_body(jj, c):
                key_index = (2 * jj * T + key_in_pair).astype(F32)
                work_ref[jj] = (
                    jnp.where(score_pair(jj) == thr[None], key_index, jnp.inf).reshape(2 * T, Q))
                return c
            lax.fori_loop(0, n_pair, mark_body, 0)

            def count_tied_upto(jcut):
                return count(lambda e, jj: e <= jcut[None], load=tie_pair)[0]

            def bs_body(_, st):
                jl, jh = st
                jm = jnp.floor(0.5 * (jl + jh))
                ok = count_tied_upto(jm) >= need
                return jnp.where(ok, jl, jm), jnp.where(ok, jm, jh)

            n_keys = (2 * n_pair * T).astype(F32)
            jl0 = jnp.full((SUBLANES, Q), -1.0, F32)
            jh0 = jnp.zeros((SUBLANES, Q), F32) + (n_keys - 1.0)
            n_steps = int(math.ceil(math.log2(nblk * T))) + 1
            _, jcut = lax.fori_loop(0, n_steps, bs_body, (jl0, jh0))

            def drop_body(jj, c):
                e = tie_pair(jj)
                drop = tied[None] & (e > jcut[None]) & (e < jnp.inf)
                score_ref[pl.ds(2 * jj, 2)] = jnp.where(drop, -jnp.inf, score_pair(jj)).reshape(2, T, Q)
                return c
            lax.fori_loop(0, n_pair, drop_body, 0)

        return thr

    thr = lax.cond((i0 + 2) * T > k_sel, select_threshold,
                   lambda: jnp.full((SUBLANES, Q), -F32_MAX, F32))

    def mask_body(jj, c):
        add = jnp.where(score_pair(jj) >= thr[None], 0.0, MASKED).reshape(2 * T, Q).astype(BF16)
        for hf in range(2):
            lhsd_ref[hf, key_rows(2 * jj, 2), KV_RANK:2 * KV_RANK] = add[:, hf * T:(hf + 1) * T]
        return c

    lax.fori_loop(0, n_pair, mask_body, 0)

    def dsa_logits(j, nblocks, slot, bias=None):
        for tt in range(NT):
            hf, p = divmod(tt, NP)
            s = _dot_nt(lhsd_ref[hf, key_rows(j, nblocks), :], qsel_ref[tt])
            if bias is not None:
                s = s + bias(hf, slice(p * TW, (p + 1) * TW))
            work_ref[slot_tile(slot, tt, nblocks)] = s

    def dsa_absorb(j, nblocks, slot):
        pv_lhs = blocks_t(lambda jb: ct_ref[jb], j, nblocks)
        for tt in range(NT):
            online_update(tt, work_ref[slot_tile(slot, tt, nblocks)], pv_lhs)

    def bias_before(hf, cols):
        return jnp.concatenate([bias_ref[2, :, cols], bias_ref[1 + hf, :, cols]], axis=0)

    def bias_diag(hf, cols):
        first = bias_ref[hf, :, cols]
        return jnp.concatenate([first, bias_ref[0, :, cols]], axis=0)

    reset_state()
    leftover = for_far_blocks(jnp.maximum(i0 - 2, 0), dsa_logits, dsa_absorb)

    def dsa_far_step(j):
        return (lambda slot: dsa_logits(j, 2, slot), lambda slot: dsa_absorb(j, 2, slot))

    def absorb_first_biased(j):
        def absorb(slot):
            m_ref[...] = m_ref[...] + bfar_ref[...]
            dsa_absorb(j, 2, slot)
        return absorb

    diag_step = lambda absorb: (lambda slot: dsa_logits(i0, 2, slot, bias_diag), absorb)
    before_step = (lambda slot: dsa_logits(i0 - 2, 2, slot, bias_before), absorb_first_biased(i0 - 2))
    has_pair, pair_j = leftover

    @pl.when(jnp.logical_and(g >= 1, has_pair))
    def _dsa_three():
        run_staged([dsa_far_step(pair_j), before_step, diag_step(lambda slot: dsa_absorb(i0, 2, slot))])

    @pl.when(jnp.logical_and(g >= 1, jnp.logical_not(has_pair)))
    def _dsa_two():
        run_staged([before_step, diag_step(lambda slot: dsa_absorb(i0, 2, slot))])

    @pl.when(g == 0)
    def _dsa_first():
        run_staged([diag_step(absorb_first_biased(0))])

    for tt in range(NT):
        hf, p = divmod(tt, NP)
        o_pair = jnp.zeros((T, LANES), F32)
        for e in range(2):
            cols = slice(tt * TW + e * T, tt * TW + (e + 1) * T)
            o_t = acc_ref[0:KV_RANK, cols] / acc_ref[KV_RANK:KV_RANK + 1, cols]
            o_pair = o_pair + _dot(o_t.T.astype(BF16), wuv_ref[2 * p + e])
        o_ref[hf * T:(hf + 1) * T, p * LANES:(p + 1) * LANES] = o_pair.astype(o_ref.dtype)


_EV_QA, _EV_QI, _EV_QB, _EV_KB, _EV_VB = 0, 1, 2, 3, 4
_EV_CKV, _EV_KIDX = 20, 21


def _attention(hb, w_t, f_rows, wuk_pad, wuv_pad, bias_tiles, bias_far, place, k_sel):
    bsz, seq, _ = hb.shape
    T, Q, H = BLK, QUERIES, N_HEADS
    nblk = seq // T
    lanes = 2 * H * T
    pq, cq, pk, ck = place
    wide = lambda c: pl.BlockSpec((None, Q, MIX_HALF), lambda b, i, c=c: (b, i, c))
    once = dict(pipeline_mode=pl.Buffered(1))
    full_wide = lambda c: pl.BlockSpec((None, seq, MIX_HALF), lambda b, i, c=c: (b, 0, c), **once)
    full_narrow = lambda c: pl.BlockSpec((None, seq, LANES), lambda b, i, c=c: (b, 0, c), **once)
    const = lambda a: pl.BlockSpec(a.shape, lambda b, i, nd=a.ndim: (0,) * nd, **once)
    return pl.pallas_call(
        functools.partial(_attn_kernel, k_sel=k_sel),
        grid=(bsz, seq // Q),
        in_specs=[wide(_EV_QA), wide(_EV_QI), wide(_EV_QB),
                  pl.BlockSpec((None, H, Q), lambda b, i: (b, 0, i)),
                  pl.BlockSpec((None, Q, LANES), lambda b, i: (b, i, 0)),
                  full_wide(_EV_KB), full_wide(_EV_VB),
                  full_narrow(_EV_CKV), full_narrow(_EV_KIDX),
                  pl.BlockSpec((None, seq, LANES), lambda b, i: (b, 0, 0), **once),
                  const(wuk_pad), const(wuv_pad), const(bias_tiles), const(bias_far),
                  const(pq), const(cq), const(pk), const(ck)],
        out_specs=pl.BlockSpec((None, Q, 2 * MIX_HALF), lambda b, i: (b, i, 0)),
        out_shape=jax.ShapeDtypeStruct((bsz, seq, 2 * MIX_HALF), BF16),
        scratch_shapes=[pltpu.VMEM((nblk, T, Q), F32),
                        pltpu.VMEM((2, seq, 2 * KV_RANK), BF16),
                        pltpu.VMEM((nblk, PV_ROWS, T), BF16),
                        pltpu.VMEM((H // 2, seq, 2 * LANES), BF16),
                        pltpu.VMEM((H // 2, nblk, PV_ROWS, T), BF16),
                        pltpu.VMEM((H, 2 * T, 2 * KV_RANK), BF16),
                        pltpu.VMEM((H, 2 * T, 2 * LANES), BF16),
                        pltpu.VMEM((H, 2 * T, LANES), BF16),
                        pltpu.VMEM((SUBLANES, lanes), F32),
                        pltpu.VMEM((PV_ROWS, lanes), F32),
                        pltpu.VMEM((max(N_SLOTS * H, nblk // 2), 2 * T, 2 * T), F32)],
        compiler_params=_cparams(("parallel", "arbitrary")),
        name="dsa_fox_attention",
    )(hb, hb, hb, w_t, f_rows, hb, hb, hb, hb, f_rows, wuk_pad, wuv_pad, bias_tiles, bias_far,
      pq, cq, pk, ck)


def _proj_ln_kernel(a_ref, w_ref, x_ref, g_ref, b_ref, xo_ref, xb_ref, *, alpha, sub):
    for c in range(a_ref.shape[0] // sub):
        rows = slice(c * sub, (c + 1) * sub)
        y = alpha * x_ref[rows, :] + _dot(a_ref[rows, :], w_ref[...])
        out = _layer_norm_rows(y, g_ref[...], b_ref[...])
        xo_ref[rows, :] = out
        xb_ref[rows, :] = out.astype(BF16)


def _proj_ln(a, w, layer, x, g, b, alpha, tm):
    n, k = a.shape
    d = w.shape[2]
    return pl.pallas_call(
        functools.partial(_proj_ln_kernel, alpha=alpha, sub=min(128, tm)),
        grid=(n // tm,),
        in_specs=[pl.BlockSpec((tm, k), lambda i: (i, 0)),
                  pl.BlockSpec((None, k, d), lambda i: (layer, 0, 0)),
                  pl.BlockSpec((tm, d), lambda i: (i, 0)),
                  pl.BlockSpec((1, d), lambda i: (0, 0)),
                  pl.BlockSpec((1, d), lambda i: (0, 0))],
        out_specs=[pl.BlockSpec((tm, d), lambda i: (i, 0)),
                   pl.BlockSpec((tm, d), lambda i: (i, 0))],
        out_shape=[jax.ShapeDtypeStruct((n, d), F32),
                   jax.ShapeDtypeStruct((n, d), BF16)],
        compiler_params=_cparams(("parallel",)),
        name="proj_residual_ln",
    )(a, w, x, g, b)


def _causal_conv3(h, tail, cw):
    w0, w1, w2 = cw[0:1, :], cw[1:2, :], cw[2:3, :]
    body = w0 * pltpu.roll(h, 2, axis=0) + w1 * pltpu.roll(h, 1, axis=0) + w2 * h
    head = h[0:SUBLANES, :]
    r = lax.broadcasted_iota(jnp.int32, head.shape, 0)
    hm1 = jnp.where(r < 1, pltpu.roll(tail, 1, axis=0), pltpu.roll(head, 1, axis=0))
    hm2 = jnp.where(r < 2, pltpu.roll(tail, 2, axis=0), pltpu.roll(head, 2, axis=0))
    out_head = w0 * hm2 + w1 * hm1 + w2 * head
    return jnp.concatenate([out_head, body[SUBLANES:, :]], axis=0)


def _ffn_up_kernel(x_ref, wg_ref, wv_ref, cg_ref, cv_ref, o_ref, tg_ref, tv_ref, *h_refs,
                   tiles_per_seq, sub):
    n_chunks = x_ref.shape[0] // sub
    hg_refs, hv_refs = h_refs[:n_chunks], h_refs[n_chunks:]
    first = (pl.program_id(1) % tiles_per_seq) == 0
    body_rows = slice(SUBLANES, SUBLANES + sub)

    def matmuls(c):
        x = x_ref[c * sub:(c + 1) * sub, :]
        hg_refs[c][body_rows, :] = _dot(x, wg_ref[...])
        hv_refs[c][body_rows, :] = _dot(x, wv_ref[...])

    def conv(h_ref, cw):
        return (cw[0:1, :] * h_ref[SUBLANES - 2:SUBLANES - 2 + sub, :]
                + cw[1:2, :] * h_ref[SUBLANES - 1:SUBLANES - 1 + sub, :]
                + cw[2:3, :] * h_ref[body_rows, :])

    def gate(c):
        if c == 0:
            hg_refs[0][0:SUBLANES, :] = jnp.where(first, 0.0, tg_ref[...])
            hv_refs[0][0:SUBLANES, :] = jnp.where(first, 0.0, tv_ref[...])
        else:
            hg_refs[c][0:SUBLANES, :] = hg_refs[c - 1][sub:sub + SUBLANES, :]
            hv_refs[c][0:SUBLANES, :] = hv_refs[c - 1][sub:sub + SUBLANES, :]
        g = conv(hg_refs[c], cg_ref[...])
        v = conv(hv_refs[c], cv_ref[...])
        half_g = 0.5 * g
        o_ref[c * sub:(c + 1) * sub, :] = ((half_g + half_g * jnp.tanh(half_g)) * v).astype(o_ref.dtype)

    matmuls(0)
    for c in range(n_chunks):
        if c + 1 < n_chunks:
            matmuls(c + 1)
        gate(c)
    tg_ref[...] = hg_refs[n_chunks - 1][sub:sub + SUBLANES, :]
    tv_ref[...] = hv_refs[n_chunks - 1][sub:sub + SUBLANES, :]


def _ffn_up(xb, w_up, conv_w, layer, seq, tm, tn, sub):
    n, k = xb.shape
    dff = w_up.shape[2] // 2
    ncol = dff // tn
    return pl.pallas_call(
        functools.partial(_ffn_up_kernel, tiles_per_seq=seq // tm, sub=sub),
        grid=(ncol, n // tm),
        in_specs=[pl.BlockSpec((tm, k), lambda j, i: (i, 0)),
                  pl.BlockSpec((None, k, tn), lambda j, i: (layer, 0, j)),
                  pl.BlockSpec((None, k, tn), lambda j, i, ncol=ncol: (layer, 0, j + ncol)),
                  pl.BlockSpec((None, CONV_W, tn), lambda j, i: (layer, 0, j)),
                  pl.BlockSpec((None, CONV_W, tn), lambda j, i, ncol=ncol: (layer, 0, j + ncol))],
        out_specs=pl.BlockSpec((tm, tn), lambda j, i: (i, j)),
        out_shape=jax.ShapeDtypeStruct((n, dff), BF16),
        scratch_shapes=([pltpu.VMEM((SUBLANES, tn), F32)] * 2
                        + [pltpu.VMEM((sub + SUBLANES, tn), F32)] * (2 * (tm // sub))),
        compiler_params=_cparams(("arbitrary", "arbitrary")),
        name="ffn_up_conv_gate",
    )(xb, w_up, w_up, conv_w, conv_w)


def _gelu_tanh(x):
    return 0.5 * x * (1.0 + jnp.tanh(math.sqrt(2.0 / math.pi) * (x + 0.044715 * (x * x * x))))


def _group_mean(v, avg):
    hi = v.astype(BF16)
    lo = (v - hi.astype(F32)).astype(BF16)
    return _dot(hi, avg) + _dot(lo, avg)


def _odd_mixer_kernel(x_ref, w_ref, sg_ref, sw_ref, sb_ref, cw_ref, wo_ref, xf_ref, g_ref, b_ref,
                      xo_ref, xb_ref, o_ref, tail_ref, *, tiles_per_seq, alpha):
    tm = x_ref.shape[0]
    W = MIX_HALF
    first = (pl.program_id(0) % tiles_per_seq) == 0
    x = x_ref[...]
    lane = lax.broadcasted_iota(jnp.int32, (CHUNK, LANES), 1)
    lo_half = lane < HEAD_DIM
    r = lax.broadcasted_iota(jnp.int32, (W, W), 0) // HEAD_DIM
    c = lax.broadcasted_iota(jnp.int32, (W, W), 1) // HEAD_DIM
    avg = jnp.where(r == c, 1.0 / HEAD_DIM, 0.0).astype(BF16)
    tri = (lax.broadcasted_iota(jnp.int32, (CHUNK, CHUNK), 1)
           <= lax.broadcasted_iota(jnp.int32, (CHUNK, CHUNK), 0))

    u = _gelu_tanh(_dot(x, w_ref[:, 0:W]))
    v = _gelu_tanh(_dot(x, w_ref[:, W:2 * W]))
    vc = v - _group_mean(v, avg)
    var = _group_mean(vc * vc, avg)
    vn = (vc * lax.rsqrt(var + LN_EPS) * sg_ref[...]).astype(BF16)
    for n in range(tm // CHUNK):
        rows = slice(n * CHUNK, (n + 1) * CHUNK)
        for p in range(N_HEADS // 2):
            cols = slice(p * LANES, (p + 1) * LANES)
            vp = vn[rows, cols]
            w_e = jnp.where(tri, sw_ref[2 * p], 0.0).astype(BF16)
            w_o = jnp.where(tri, sw_ref[2 * p + 1], 0.0).astype(BF16)
            mix = (_dot(w_e, jnp.where(lo_half, vp, 0)) + _dot(w_o, jnp.where(lo_half, 0, vp))
                   + sb_ref[:, cols])
            o_ref[rows, cols] = (u[rows, cols] * mix).astype(o_ref.dtype)

    g_b = _dot(x, w_ref[:, 2 * W:3 * W])
    y = _dot(x, w_ref[:, 3 * W:4 * W]) * _dot(x, w_ref[:, 4 * W:5 * W])
    conv = _causal_conv3(y, jnp.where(first, 0.0, tail_ref[...]), cw_ref[...])
    tail_ref[...] = y[tm - SUBLANES:tm, :]
    o_ref[:, W:2 * W] = (g_b * conv).astype(o_ref.dtype)

    out = _layer_norm_rows(alpha * xf_ref[...] + _dot(o_ref[...], wo_ref[...]), g_ref[...], b_ref[...])
    xo_ref[...] = out
    xb_ref[...] = out.astype(BF16)


def _odd_mixer(xb, w_in, sgu_g, sgu_w, sgu_b_tile, conv_w, w_o, layer, xf, g, b, alpha, seq, tm):
    n, k = xb.shape
    W = MIX_HALF
    d = w_o.shape[2]
    const2 = lambda shape: pl.BlockSpec(shape, lambda i: (0, 0))
    of_layer = lambda shape: pl.BlockSpec((None,) + shape, lambda i: (layer, 0, 0))
    row_tile = lambda width: pl.BlockSpec((tm, width), lambda i: (i, 0))
    return pl.pallas_call(
        functools.partial(_odd_mixer_kernel, tiles_per_seq=seq // tm, alpha=alpha),
        grid=(n // tm,),
        in_specs=[row_tile(k), of_layer((k, 5 * W)), const2((1, W)),
                  pl.BlockSpec((N_HEADS, CHUNK, CHUNK), lambda i: (0, 0, 0)),
                  const2((CHUNK, W)), const2((CONV_W, W)),
                  of_layer((2 * W, d)), row_tile(d), const2((1, d)), const2((1, d))],
        out_specs=[row_tile(d), row_tile(d)],
        out_shape=[jax.ShapeDtypeStruct((n, d), F32), jax.ShapeDtypeStruct((n, d), BF16)],
        scratch_shapes=[pltpu.VMEM((tm, 2 * W), BF16),
                        pltpu.VMEM((SUBLANES, W), F32)],
        compiler_params=_cparams(("arbitrary",)),
        name="odd_mixer",
    )(xb, w_in, sgu_g, sgu_w, sgu_b_tile, conv_w, w_o, xf, g, b)


def _t5_bucket_np(dist):
    max_exact = N_BUCKETS // 2
    n = np.maximum(dist, 0)
    nf = np.maximum(n, 1).astype(np.float32)
    large = max_exact + (np.log(nf / max_exact) / np.float32(math.log(T5_MAX_DIST / max_exact))
                         * (N_BUCKETS - max_exact)).astype(np.int32)
    large = np.minimum(large, N_BUCKETS - 1)
    return np.where(n < max_exact, n, large)


def _bias_tables(rel_bias):
    T = BLK
    s = np.arange(T)[:, None]
    t = np.arange(T)[None, :]
    bucket = _t5_bucket_np(np.stack([t + d * T - s for d in range(3)]))
    far_bucket = N_BUCKETS - 1
    assert (bucket[2] == far_bucket).all()
    assert (_t5_bucket_np(np.arange(T + 1, 64 * T)) == far_bucket).all()
    rb = rel_bias.astype(F32) * LOG2E
    tiles = sum(jnp.where(jnp.asarray(bucket == b)[..., None], rb[b], 0.0) for b in range(N_BUCKETS))
    tiles = jnp.swapaxes(tiles, 2, 3).reshape(3, T, N_HEADS * T)
    far = jnp.tile(jnp.repeat(rb[far_bucket], T)[None, :], (1, 2))
    return tiles, far


def _placement_constants():
    H, L = N_HEADS, LANES
    pq = np.zeros((3 * L, H * L), np.float32)
    cq = np.zeros((1, H * L), np.float32)
    pk = np.zeros((3 * L, (H // 2) * L), np.float32)
    ck = np.zeros((1, (H // 2) * L), np.float32)
    for h in range(H):
        p, e = divmod(h, 2)
        for piece in range(3):
            pq[piece * L + h, h * L + 6 + piece] = 1.0
            cq[0, h * L + 3 * e + piece] = 1.0
            pk[piece * L + h, p * L + 3 * e + piece] = -1.0
            ck[0, p * L + 6 + piece] = 1.0
    return jnp.asarray(pq, BF16), jnp.asarray(cq), jnp.asarray(pk, BF16), jnp.asarray(ck)


def _even_weights(w_in, w_uk, w_uv):
    hd, H = HEAD_DIM, N_HEADS
    widths = (H * hd, KV_RANK, H * hd, hd, H, H * hd, H * hd, H * hd, H)
    offs = np.cumsum((0,) + widths)
    seg = lambda s: w_in[:, offs[s]:offs[s + 1]]
    q_a, c_kv, q_idx, k_idx, w_idx, q_b, k_b, v_b, f_in = [seg(s) for s in range(9)]
    scale = hd ** -0.5
    w_big = jnp.concatenate([q_a, q_idx * scale, q_b * (scale * LOG2E), k_b, v_b, c_kv,
                             k_idx, k_idx], axis=1).astype(BF16)
    pad = jnp.zeros((w_in.shape[0], LANES - 2 * H), w_in.dtype)
    w_small = jnp.concatenate([w_idx, f_in, pad], axis=1).astype(BF16)
    wuk_t = jnp.swapaxes(w_uk, 1, 2) * (scale * LOG2E)
    odd = (np.arange(H) % 2 == 1)[:, None, None]
    zk, zv = jnp.zeros_like(wuk_t), jnp.zeros_like(w_uv)
    wuk_pad = jnp.where(odd, jnp.concatenate([zk, wuk_t], axis=1), jnp.concatenate([wuk_t, zk], axis=1))
    wuv_pad = jnp.where(odd, jnp.concatenate([zv, w_uv], axis=2), jnp.concatenate([w_uv, zv], axis=2))
    return w_big, w_small, wuk_pad.astype(BF16), wuv_pad.astype(BF16)


def kernel(x, ln_g, ln_b, rel_bias, ev_w_in, ev_w_uk, ev_w_uv, ev_b_f, ev_w_o,
           od_w_in, od_sgu_g, od_sgu_w, od_sgu_b, od_conv_w, od_w_o,
           ffn_w_up, ffn_conv_w, ffn_w_down):
    bsz, seq, d = x.shape
    depth = ln_g.shape[0]
    n = bsz * seq
    alpha = (2.0 * depth) ** 0.25
    k_sel = min(TOPK_MAX, seq // 4)
    H = N_HEADS
    assert seq % QUERIES == 0 and seq % CHUNK == 0
    tm_ffn = min(1024, seq)

    bias_tiles, bias_far = _bias_tables(rel_bias)
    place = _placement_constants()
    w_up_b, w_down_b, ev_w_o_b = ffn_w_up.astype(BF16), ffn_w_down.astype(BF16), ev_w_o.astype(BF16)
    od_w_in_b, od_w_o_b = od_w_in.astype(BF16), od_w_o.astype(BF16)
    xf = x.reshape(n, d)
    xb = xf
    for layer in range(depth):
        j = layer // 2
        g0, b0 = ln_g[layer, 0][None, :], ln_b[layer, 0][None, :]
        g1, b1 = ln_g[layer, 1][None, :], ln_b[layer, 1][None, :]
        if layer % 2 == 0:
            w_big, w_small, wuk_pad, wuv_pad = _even_weights(ev_w_in[j], ev_w_uk[j], ev_w_uv[j])
            hb, hs = _even_proj(xb, w_big, w_small, min(2048, seq))
            hb = hb.reshape(bsz, seq, -1)
            hs_t = jnp.swapaxes(hs[0:2 * H].reshape(2 * H, bsz, seq), 0, 1)
            f_rows = _cumf(hs_t[:, H:2 * H], ev_b_f[j][:, None])
            mix = _attention(hb, hs_t[:, 0:H], f_rows, wuk_pad, wuv_pad, bias_tiles, bias_far,
                             place, k_sel)
            xf, xb = _proj_ln(mix.reshape(n, d), ev_w_o_b, j, xf, g0, b0, alpha, tm_ffn)
        else:
            sgu_b_tile = jnp.repeat(jnp.swapaxes(od_sgu_b[j], 0, 1), HEAD_DIM, axis=1)
            xf, xb = _odd_mixer(xb, od_w_in_b, od_sgu_g[j].reshape(1, -1), od_sgu_w[j],
                                sgu_b_tile, od_conv_w[j], od_w_o_b, j, xf, g0, b0, alpha,
                                seq, min(1024, seq))
        act = _ffn_up(xb, w_up_b, ffn_conv_w, layer, seq, min(4096, seq), 256, min(512, seq))
        xf, xb = _proj_ln(act, w_down_b, layer, xf, g1, b1, alpha, tm_ffn)
    return xf.reshape(bsz, seq, d)
```
